```python
import jax, jax.numpy as jnp
from jax import lax
import numpy as np

D_MODEL = 1024
BATCH = 8
SEQ = 8192
DEPTH = 1

HG_HEADS = 8
HG_KEY_DIM = 128
HG_VAL_DIM = D_MODEL // HG_HEADS
HG_CHUNK = 64
FOX_HEADS = 16
FOX_HEAD_DIM = 64
FOX_BLOCK = 128
FOX_F_BIAS_INIT = 2.0
D_FF = 2816
CONV_WIDTH = 3
EPS = 1e-6

HG_QK = HG_HEADS * HG_KEY_DIM
HG_V = HG_HEADS * HG_VAL_DIM
FOX_W = FOX_HEADS * FOX_HEAD_DIM
SPLITS = (HG_QK, HG_QK, HG_V, HG_V, FOX_W, FOX_W, FOX_W, FOX_HEADS, D_MODEL, D_MODEL)
D_IN = sum(SPLITS)

kernel_name = 'hgrn2_fox_gated_hybrid_block'


def rms_norm(x, g):
    xf = x.astype(jnp.float32)
    y = xf * lax.rsqrt(jnp.mean(xf * xf, axis=-1, keepdims=True) + EPS)
    return (y * g.astype(jnp.float32)).astype(x.dtype)


def hgrn2_mixer(q, f_logit, i, g, lb, g_norm):
    B, S, _ = q.shape
    n_chunks = S // HG_CHUNK
    f32 = jnp.float32
    q = jax.nn.silu(q.astype(f32))
    f = lb + (1.0 - lb) * jax.nn.sigmoid(f_logit.astype(f32))
    k = 1.0 - f
    log_f = jnp.log(f)

    def chunks(t, d):
        return t.astype(f32).reshape(B, n_chunks, HG_CHUNK, HG_HEADS, d).transpose(1, 0, 3, 2, 4)

    xs = (chunks(q, HG_KEY_DIM), chunks(k, HG_KEY_DIM), chunks(log_f, HG_KEY_DIM), chunks(i, HG_VAL_DIM))
    causal = jnp.tril(jnp.ones((HG_CHUNK, HG_CHUNK), dtype=bool))[:, :, None]

    def step(state, inp):
        qc, kc, gc, vc = inp
        b = jnp.cumsum(gc, axis=2)
        o_inter = jnp.einsum('bhtk,bhkv->bhtv', qc * jnp.exp(b), state)
        rel = b[:, :, :, None, :] - b[:, :, None, :, :]
        decay = jnp.exp(jnp.where(causal, rel, -jnp.inf))
        scores = jnp.einsum('bhtsk,bhsk->bhts', qc[:, :, :, None, :] * decay, kc)
        o_intra = jnp.einsum('bhts,bhsv->bhtv', scores, vc)
        b_end = b[:, :, -1:, :]
        state = jnp.exp(b_end[:, :, 0, :])[..., None] * state + jnp.einsum('bhsk,bhsv->bhkv', kc * jnp.exp(b_end - b), vc)
        return state, o_inter + o_intra

    state0 = jnp.zeros((B, HG_HEADS, HG_KEY_DIM, HG_VAL_DIM), f32)
    _, o = lax.scan(step, state0, xs)
    o = o.transpose(1, 0, 3, 2, 4).reshape(B, S, HG_HEADS, HG_VAL_DIM)
    o = rms_norm(o, g_norm) * jax.nn.silu(g.astype(f32)).reshape(B, S, HG_HEADS, HG_VAL_DIM)
    return o.reshape(B, S, HG_V).astype(i.dtype)


def fox_mixer(q, k, v, f_logit, f_bias):
    B, S, _ = q.shape
    n_blocks = S // FOX_BLOCK
    f32 = jnp.float32
    q = q.reshape(B, S, FOX_HEADS, FOX_HEAD_DIM) * FOX_HEAD_DIM ** -0.5
    k = k.reshape(B, S, FOX_HEADS, FOX_HEAD_DIM)
    v = v.reshape(B, S, FOX_HEADS, FOX_HEAD_DIM)
    log_f = jax.nn.log_sigmoid(f_logit.astype(f32) + f_bias.astype(f32))
    c = jnp.cumsum(log_f, axis=1).transpose(0, 2, 1)
    q_blocks = q.reshape(B, n_blocks, FOX_BLOCK, FOX_HEADS, FOX_HEAD_DIM).transpose(1, 0, 2, 3, 4)
    c_blocks = c.reshape(B, FOX_HEADS, n_blocks, FOX_BLOCK).transpose(2, 0, 1, 3)
    key_pos = jnp.arange(S)

    def attend(args):
        blk, qb, cb = args
        logits = jnp.einsum('bqhd,bshd->bhqs', qb, k).astype(f32)
        logits = logits + (cb[..., None] - c[:, :, None, :])
        q_pos = blk * FOX_BLOCK + jnp.arange(FOX_BLOCK)
        mask = key_pos[None, :] <= q_pos[:, None]
        p = jax.nn.softmax(jnp.where(mask, logits, -jnp.inf), axis=-1)
        return jnp.einsum('bhqs,bshd->bqhd', p.astype(v.dtype), v)

    o = lax.map(attend, (jnp.arange(n_blocks), q_blocks, c_blocks))
    return o.transpose(1, 0, 2, 3, 4).reshape(B, S, FOX_W)


def conv_glu_ffn(x, w_up, conv_w, conv_b, w_down):
    S = x.shape[1]
    u = x @ w_up
    u_pad = jnp.pad(u, ((0, 0), (CONV_WIDTH - 1, 0), (0, 0)))
    acc = conv_b
    for j in range(CONV_WIDTH):
        acc = acc + conv_w[j] * u_pad[:, j:j + S]
    gate, val = jnp.split(acc, 2, axis=-1)
    return (jax.nn.gelu(gate, approximate=False) * val) @ w_down


def _fwd_setup_inputs(seed: int = 0) -> dict:
    key = jax.random.key(seed)
    ks = jax.random.split(key, 16)
    f32 = jnp.float32

    def dense(k, shape, fan_in):
        return jax.random.normal(k, shape, f32) * fan_in ** -0.5

    def gain(k, shape):
        return 1.0 + 0.02 * jax.random.normal(k, shape, f32)

    return {
        'x': jax.random.normal(ks[0], (BATCH, SEQ, D_MODEL), f32),
        'norm_mix': gain(ks[1], (DEPTH, D_MODEL)),
        'w_in': dense(ks[2], (DEPTH, D_MODEL, D_IN), D_MODEL),
        'fox_f_bias': FOX_F_BIAS_INIT + 0.5 * jax.random.normal(ks[3], (DEPTH, FOX_HEADS), f32),
        'hg_lb_logits': 0.1 * jax.random.normal(ks[4], (DEPTH + 1, HG_QK), f32),
        'hg_norm': gain(ks[5], (DEPTH, HG_VAL_DIM)),
        'w_branch_a': dense(ks[6], (DEPTH, HG_V, D_MODEL), HG_V),
        'w_branch_b': dense(ks[7], (DEPTH, FOX_W, D_MODEL), FOX_W),
        'w_out': dense(ks[8], (DEPTH, D_MODEL, D_MODEL), D_MODEL),
        'norm_ffn': gain(ks[9], (DEPTH, D_MODEL)),
        'w_up': dense(ks[10], (DEPTH, D_MODEL, 2 * D_FF), D_MODEL),
        'conv_w': dense(ks[11], (DEPTH, CONV_WIDTH, 2 * D_FF), CONV_WIDTH),
        'conv_b': 0.02 * jax.random.normal(ks[12], (DEPTH, 2 * D_FF), f32),
        'w_down': dense(ks[13], (DEPTH, D_FF, D_MODEL), D_FF),
        'norm_final': gain(ks[14], (D_MODEL,)),
    }


def _fwd_reference(x, norm_mix, w_in, fox_f_bias, hg_lb_logits, hg_norm, w_branch_a, w_branch_b, w_out, norm_ffn, w_up, conv_w, conv_b, w_down, norm_final):
    h = x
    lb_all = jnp.cumsum(jax.nn.softmax(hg_lb_logits.astype(jnp.float32), axis=0), axis=0)
    cuts = np.cumsum(SPLITS)[:-1].tolist()
    for layer in range(DEPTH):
        n = rms_norm(h, norm_mix[layer])
        proj = n @ w_in[layer]
        hq, hf, hi, hg, fq, fk, fv, ff, ga, gb = jnp.split(proj, cuts, axis=-1)
        o_a = hgrn2_mixer(hq, hf, hi, hg, lb_all[layer], hg_norm[layer])
        o_b = fox_mixer(fq, fk, fv, ff, fox_f_bias[layer])
        merged = jax.nn.sigmoid(ga) * (o_a @ w_branch_a[layer]) + jax.nn.sigmoid(gb) * (o_b @ w_branch_b[layer])
        h = h + merged @ w_out[layer]
        h = h + conv_glu_ffn(rms_norm(h, norm_ffn[layer]), w_up[layer], conv_w[layer], conv_b[layer], w_down[layer])
    return rms_norm(h, norm_final)


import jax as _jax
import jax.numpy as _jnp

TWIN_FORMAT = 'train_step'
FWD_PARAMS = ['x', 'norm_mix', 'w_in', 'fox_f_bias', 'hg_lb_logits', 'hg_norm', 'w_branch_a', 'w_branch_b', 'w_out', 'norm_ffn', 'w_up', 'conv_w', 'conv_b', 'w_down', 'norm_final']
TWIN_WEIGHTS = ['norm_mix', 'w_in', 'fox_f_bias', 'hg_lb_logits', 'hg_norm', 'w_branch_a', 'w_branch_b', 'w_out', 'norm_ffn', 'w_up', 'conv_w', 'conv_b', 'w_down', 'norm_final']
TWIN_DIFF_INPUT = 'x'
TWIN_INPUTS = ['x', 'norm_mix', 'w_in', 'fox_f_bias', 'hg_lb_logits', 'hg_norm', 'w_branch_a', 'w_branch_b', 'w_out', 'norm_ffn', 'w_up', 'conv_w', 'conv_b', 'w_down', 'norm_final', 'loss_target', 'm_norm_mix', 'm_w_in', 'm_fox_f_bias', 'm_hg_lb_logits', 'm_hg_norm', 'm_w_branch_a', 'm_w_branch_b', 'm_w_out', 'm_norm_ffn', 'm_w_up', 'm_conv_w', 'm_conv_b', 'm_w_down', 'm_norm_final', 'v_norm_mix', 'v_w_in', 'v_fox_f_bias', 'v_hg_lb_logits', 'v_hg_norm', 'v_w_branch_a', 'v_w_branch_b', 'v_w_out', 'v_norm_ffn', 'v_w_up', 'v_conv_w', 'v_conv_b', 'v_w_down', 'v_norm_final']
TWIN_OUTPUTS = ['loss', 'grad_x', 'grad_norm_mix', 'grad_w_in', 'grad_fox_f_bias', 'grad_hg_lb_logits', 'grad_hg_norm', 'grad_w_branch_a', 'grad_w_branch_b', 'grad_w_out', 'grad_norm_ffn', 'grad_w_up', 'grad_conv_w', 'grad_conv_b', 'grad_w_down', 'grad_norm_final', 'delta_norm_mix', 'delta_w_in', 'delta_fox_f_bias', 'delta_hg_lb_logits', 'delta_hg_norm', 'delta_w_branch_a', 'delta_w_branch_b', 'delta_w_out', 'delta_norm_ffn', 'delta_w_up', 'delta_conv_w', 'delta_conv_b', 'delta_w_down', 'delta_norm_final', 'new_m_norm_mix', 'new_m_w_in', 'new_m_fox_f_bias', 'new_m_hg_lb_logits', 'new_m_hg_norm', 'new_m_w_branch_a', 'new_m_w_branch_b', 'new_m_w_out', 'new_m_norm_ffn', 'new_m_w_up', 'new_m_conv_w', 'new_m_conv_b', 'new_m_w_down', 'new_m_norm_final', 'new_v_norm_mix', 'new_v_w_in', 'new_v_fox_f_bias', 'new_v_hg_lb_logits', 'new_v_hg_norm', 'new_v_w_branch_a', 'new_v_w_branch_b', 'new_v_w_out', 'new_v_norm_ffn', 'new_v_w_up', 'new_v_conv_w', 'new_v_conv_b', 'new_v_w_down', 'new_v_norm_final']
TWIN_LEAF_KINDS = {'loss': 'loss', 'grad_x': 'grad_x', 'grad_norm_mix': 'grad_w', 'grad_w_in': 'grad_w', 'grad_fox_f_bias': 'grad_w', 'grad_hg_lb_logits': 'grad_w', 'grad_hg_norm': 'grad_w', 'grad_w_branch_a': 'grad_w', 'grad_w_branch_b': 'grad_w', 'grad_w_out': 'grad_w', 'grad_norm_ffn': 'grad_w', 'grad_w_up': 'grad_w', 'grad_conv_w': 'grad_w', 'grad_conv_b': 'grad_w', 'grad_w_down': 'grad_w', 'grad_norm_final': 'grad_w', 'delta_norm_mix': 'delta_w', 'delta_w_in': 'delta_w', 'delta_fox_f_bias': 'delta_w', 'delta_hg_lb_logits': 'delta_w', 'delta_hg_norm': 'delta_w', 'delta_w_branch_a': 'delta_w', 'delta_w_branch_b': 'delta_w', 'delta_w_out': 'delta_w', 'delta_norm_ffn': 'delta_w', 'delta_w_up': 'delta_w', 'delta_conv_w': 'delta_w', 'delta_conv_b': 'delta_w', 'delta_w_down': 'delta_w', 'delta_norm_final': 'delta_w', 'new_m_norm_mix': 'new_m', 'new_m_w_in': 'new_m', 'new_m_fox_f_bias': 'new_m', 'new_m_hg_lb_logits': 'new_m', 'new_m_hg_norm': 'new_m', 'new_m_w_branch_a': 'new_m', 'new_m_w_branch_b': 'new_m', 'new_m_w_out': 'new_m', 'new_m_norm_ffn': 'new_m', 'new_m_w_up': 'new_m', 'new_m_conv_w': 'new_m', 'new_m_conv_b': 'new_m', 'new_m_w_down': 'new_m', 'new_m_norm_final': 'new_m', 'new_v_norm_mix': 'new_v', 'new_v_w_in': 'new_v', 'new_v_fox_f_bias': 'new_v', 'new_v_hg_lb_logits': 'new_v', 'new_v_hg_norm': 'new_v', 'new_v_w_branch_a': 'new_v', 'new_v_w_branch_b': 'new_v', 'new_v_w_out': 'new_v', 'new_v_norm_ffn': 'new_v', 'new_v_w_up': 'new_v', 'new_v_conv_w': 'new_v', 'new_v_conv_b': 'new_v', 'new_v_w_down': 'new_v', 'new_v_norm_final': 'new_v'}


def _forward(args):
    return _fwd_reference(*[args[k] for k in FWD_PARAMS])


def _output_shape():
    def fwd():
        inp = _fwd_setup_inputs(0)
        return _fwd_reference(*[inp[k] for k in FWD_PARAMS])
    out = _jax.eval_shape(fwd)
    return out.shape, out.dtype

N_MICROBATCH = 1
ADAM_LR = 0.001
ADAM_B1 = 0.9
ADAM_B2 = 0.999
ADAM_EPS = 1e-08
ADAM_WD = 0.01
ADAM_STEP = 10
PER_EXAMPLE_BATCH_AXIS = {'x': 0, 'loss_target': 0}
SHARED_INPUTS = []
_WEIGHT_DTYPES = {'norm_mix': _jnp.float32, 'w_in': _jnp.float32, 'fox_f_bias': _jnp.float32, 'hg_lb_logits': _jnp.float32, 'hg_norm': _jnp.float32, 'w_branch_a': _jnp.float32, 'w_branch_b': _jnp.float32, 'w_out': _jnp.float32, 'norm_ffn': _jnp.float32, 'w_up': _jnp.float32, 'conv_w': _jnp.float32, 'conv_b': _jnp.float32, 'w_down': _jnp.float32, 'norm_final': _jnp.float32}
MOMENT_SCALE = {'norm_mix': 1.580357e-01, 'w_in': 5.258715e-02, 'fox_f_bias': 2.008647e-01, 'hg_lb_logits': 8.221181e-03, 'hg_norm': 2.381864e-01, 'w_branch_a': 8.669063e-02, 'w_branch_b': 6.021212e-02, 'w_out': 1.049395e-01, 'norm_ffn': 1.951928e-01, 'w_up': 8.218197e-02, 'conv_w': 8.160251e-02, 'conv_b': 8.044154e-02, 'w_down': 1.344729e-01, 'norm_final': 6.398498e+01}


def _to_microbatches(a, axis):
    t = _jnp.moveaxis(a, axis, 0)
    t = t.reshape((N_MICROBATCH, t.shape[0] // N_MICROBATCH) + t.shape[1:])
    return _jnp.moveaxis(t, 1, axis + 1)


def setup_inputs(seed: int = 0) -> dict:
    inp = _fwd_setup_inputs(seed)
    key = _jax.random.fold_in(_jax.random.key(seed), 7919)
    shape, _ = _output_shape()
    out = dict(inp)
    out["loss_target"] = _jax.random.normal(_jax.random.fold_in(key, 0), shape, _jnp.float32)
    for i, name in enumerate(TWIN_WEIGHTS):
        w = inp[name].astype(_jnp.float32)
        if MOMENT_SCALE is None:
            s = _jnp.sqrt(_jnp.mean(_jnp.square(w)) + 1e-30)
        else:
            s = MOMENT_SCALE[name]
        km, kv = _jax.random.split(_jax.random.fold_in(key, i + 1))
        out[name] = w
        out["m_" + name] = s * _jax.random.normal(km, w.shape, _jnp.float32)
        out["v_" + name] = (s * s) * _jax.random.uniform(kv, w.shape, _jnp.float32, 0.5, 1.5)
    if N_MICROBATCH > 1:
        for name, axis in PER_EXAMPLE_BATCH_AXIS.items():
            out[name] = _to_microbatches(out[name], axis)
    return {'x': out['x'], 'norm_mix': out['norm_mix'], 'w_in': out['w_in'], 'fox_f_bias': out['fox_f_bias'], 'hg_lb_logits': out['hg_lb_logits'], 'hg_norm': out['hg_norm'], 'w_branch_a': out['w_branch_a'], 'w_branch_b': out['w_branch_b'], 'w_out': out['w_out'], 'norm_ffn': out['norm_ffn'], 'w_up': out['w_up'], 'conv_w': out['conv_w'], 'conv_b': out['conv_b'], 'w_down': out['w_down'], 'norm_final': out['norm_final'], 'loss_target': out['loss_target'], 'm_norm_mix': out['m_norm_mix'], 'm_w_in': out['m_w_in'], 'm_fox_f_bias': out['m_fox_f_bias'], 'm_hg_lb_logits': out['m_hg_lb_logits'], 'm_hg_norm': out['m_hg_norm'], 'm_w_branch_a': out['m_w_branch_a'], 'm_w_branch_b': out['m_w_branch_b'], 'm_w_out': out['m_w_out'], 'm_norm_ffn': out['m_norm_ffn'], 'm_w_up': out['m_w_up'], 'm_conv_w': out['m_conv_w'], 'm_conv_b': out['m_conv_b'], 'm_w_down': out['m_w_down'], 'm_norm_final': out['m_norm_final'], 'v_norm_mix': out['v_norm_mix'], 'v_w_in': out['v_w_in'], 'v_fox_f_bias': out['v_fox_f_bias'], 'v_hg_lb_logits': out['v_hg_lb_logits'], 'v_hg_norm': out['v_hg_norm'], 'v_w_branch_a': out['v_w_branch_a'], 'v_w_branch_b': out['v_w_branch_b'], 'v_w_out': out['v_w_out'], 'v_norm_ffn': out['v_norm_ffn'], 'v_w_up': out['v_w_up'], 'v_conv_w': out['v_conv_w'], 'v_conv_b': out['v_conv_b'], 'v_w_down': out['v_w_down'], 'v_norm_final': out['v_norm_final']}


def _loss(weights, diff, rest, loss_target):
    with _jax.named_scope("forward"):
        args = {**rest, TWIN_DIFF_INPUT: diff, **{k: w.astype(_WEIGHT_DTYPES[k]) for k, w in weights.items()}}
        y = _forward(args)
    with _jax.named_scope("loss_head"):
        err = _jnp.square(y.astype(_jnp.float32) - loss_target)
        return 0.5 * _jnp.sum(_jnp.mean(err, axis=-1)) if err.ndim else 0.5 * err


def _adamw(w, g, m, v):
    m = ADAM_B1 * m + (1.0 - ADAM_B1) * g
    v = ADAM_B2 * v + (1.0 - ADAM_B2) * _jnp.square(g)
    m_hat = m / (1.0 - ADAM_B1 ** ADAM_STEP)
    v_hat = v / (1.0 - ADAM_B2 ** ADAM_STEP)
    delta = -ADAM_LR * (m_hat / (_jnp.sqrt(v_hat) + ADAM_EPS) + ADAM_WD * w)
    return delta, m, v


def reference(x, norm_mix, w_in, fox_f_bias, hg_lb_logits, hg_norm, w_branch_a, w_branch_b, w_out, norm_ffn, w_up, conv_w, conv_b, w_down, norm_final, loss_target, m_norm_mix, m_w_in, m_fox_f_bias, m_hg_lb_logits, m_hg_norm, m_w_branch_a, m_w_branch_b, m_w_out, m_norm_ffn, m_w_up, m_conv_w, m_conv_b, m_w_down, m_norm_final, v_norm_mix, v_w_in, v_fox_f_bias, v_hg_lb_logits, v_hg_norm, v_w_branch_a, v_w_branch_b, v_w_out, v_norm_ffn, v_w_up, v_conv_w, v_conv_b, v_w_down, v_norm_final):
    given = dict(x=x, norm_mix=norm_mix, w_in=w_in, fox_f_bias=fox_f_bias, hg_lb_logits=hg_lb_logits, hg_norm=hg_norm, w_branch_a=w_branch_a, w_branch_b=w_branch_b, w_out=w_out, norm_ffn=norm_ffn, w_up=w_up, conv_w=conv_w, conv_b=conv_b, w_down=w_down, norm_final=norm_final, loss_target=loss_target, m_norm_mix=m_norm_mix, m_w_in=m_w_in, m_fox_f_bias=m_fox_f_bias, m_hg_lb_logits=m_hg_lb_logits, m_hg_norm=m_hg_norm, m_w_branch_a=m_w_branch_a, m_w_branch_b=m_w_branch_b, m_w_out=m_w_out, m_norm_ffn=m_norm_ffn, m_w_up=m_w_up, m_conv_w=m_conv_w, m_conv_b=m_conv_b, m_w_down=m_w_down, m_norm_final=m_norm_final, v_norm_mix=v_norm_mix, v_w_in=v_w_in, v_fox_f_bias=v_fox_f_bias, v_hg_lb_logits=v_hg_lb_logits, v_hg_norm=v_hg_norm, v_w_branch_a=v_w_branch_a, v_w_branch_b=v_w_branch_b, v_w_out=v_w_out, v_norm_ffn=v_norm_ffn, v_w_up=v_w_up, v_conv_w=v_conv_w, v_conv_b=v_conv_b, v_w_down=v_w_down, v_norm_final=v_norm_final)
    weights = {n: given[n] for n in TWIN_WEIGHTS}
    shared = {n: given[n] for n in SHARED_INPUTS}
    per_example = {n: given[n] for n in ['x']}
    grad_fn = _jax.value_and_grad(_loss, argnums=(0, 1))

    def one_microbatch(ex, loss_target):
        ex = dict(ex)
        diff = ex.pop(TWIN_DIFF_INPUT)
        return grad_fn(weights, diff, {**shared, **ex}, loss_target)

    if N_MICROBATCH == 1:
        loss, (grad_w, grad_x) = one_microbatch(per_example, given["loss_target"])
    else:
        def body(carry, xs):
            loss_sum, grad_sum = carry
            l_k, (gw_k, gx_k) = one_microbatch(xs[0], xs[1])
            with _jax.named_scope("update"):
                return (loss_sum + l_k, _jax.tree.map(_jnp.add, grad_sum, gw_k)), gx_k

        init = (_jnp.zeros((), _jnp.float32), _jax.tree.map(_jnp.zeros_like, weights))
        (loss, grad_w), grad_x = _jax.lax.scan(body, init, (per_example, given["loss_target"]))
    with _jax.named_scope("update"):
        delta_w, new_m, new_v = {}, {}, {}
        for n in TWIN_WEIGHTS:
            delta_w[n], new_m[n], new_v[n] = _adamw(weights[n], grad_w[n], given["m_" + n], given["v_" + n])
    return (loss, grad_x, *[grad_w[n] for n in TWIN_WEIGHTS], *[delta_w[n] for n in TWIN_WEIGHTS],
            *[new_m[n] for n in TWIN_WEIGHTS], *[new_v[n] for n in TWIN_WEIGHTS])
```

```python
import functools

import jax
import jax.numpy as jnp
from jax import lax
from jax.experimental import pallas as pl
from jax.experimental.pallas import tpu as pltpu

F32 = jnp.float32
BF16 = jnp.bfloat16

D = 1024
HG_H, HG_DK = 8, 128
FOX_H, FOX_D = 16, 64
D_FF = 2816
EPS = 1e-6
N_CHIP = 4
LANES = 128
W_IN_SHARD = 2308
W_UP_SHARD = 1408
W_DOWN_SHARD = 704
FF_COL = 7168
ADAM_LR, ADAM_B1, ADAM_B2, ADAM_EPS, ADAM_WD, ADAM_STEP = 0.001, 0.9, 0.999, 1e-08, 0.01, 10

HG_C = 16
HG_T = 256
FOX_T = 512
VMEM_LIMIT = 56 * 1024 * 1024
MESH = pl.DeviceIdType.MESH
ANY = pl.BlockSpec(memory_space=pl.ANY)


def _cparams(sem):
    return pltpu.CompilerParams(dimension_semantics=sem, vmem_limit_bytes=VMEM_LIMIT)


def _sigmoid(x):
    return 1.0 / (1.0 + jnp.exp(-x))


def _dot(a, b, dims):
    return lax.dot_general(a.astype(BF16), b.astype(BF16), (dims, ((), ())), preferred_element_type=F32)


NN = ((1,), (0,))
NT = ((1,), (1,))
TN = ((0,), (0,))


def _split_dot(tri, x, parts, dims=NN):
    acc = None
    r = x
    for _ in range(parts):
        p = r.astype(BF16)
        t = lax.dot_general(tri, p, (dims, ((), ())), preferred_element_type=F32)
        acc = t if acc is None else acc + t
        r = r - p.astype(F32)
    return acc


def _rb(arr, tm, width, cb=0):
    return (arr, (tm, width), lambda i: (i, cb))


def _cst(arr):
    return (arr, arr.shape, lambda i: (0,) * arr.ndim)


def _rows(name, fn, n_rows, tm, ins, outs, accs=(), reverse=False):
    n_in, n_out, n_acc = len(ins), len(outs), len(accs)
    nb = n_rows // tm

    def body(*refs):
        vals = [r[...] for r in refs[:n_in]]
        o, a = fn(*vals)
        for r, v in zip(refs[n_in:n_in + n_out], o):
            r[...] = v.astype(r.dtype)
        if n_acc:
            acc_refs = refs[n_in + n_out:]

            @pl.when(pl.program_id(0) == 0)
            def _():
                for r in acc_refs:
                    r[...] = jnp.zeros_like(r)

            for r, v in zip(acc_refs, a):
                r[...] += v

    if reverse:
        rowmap = lambda i: (nb - 1 - i, 0)
    else:
        rowmap = lambda i: (i, 0)
    in_specs = [pl.BlockSpec(bs, im) for (_, bs, im) in ins]
    out_specs = [pl.BlockSpec((tm, w), rowmap) for (w, _) in outs]
    out_specs += [pl.BlockSpec((r, w), lambda i: (0, 0)) for (r, w) in accs]
    out_shape = [jax.ShapeDtypeStruct((n_rows, w), dt) for (w, dt) in outs]
    out_shape += [jax.ShapeDtypeStruct((r, w), F32) for (r, w) in accs]
    res = pl.pallas_call(
        body, name=name, grid=(nb,), in_specs=in_specs, out_specs=out_specs, out_shape=out_shape,
        compiler_params=_cparams(("arbitrary",)),
    )(*[a for a, _, _ in ins])
    return (res[:n_out], res[n_out:]) if n_acc else res


def _mm(name, a, b, mode, out_dtype, tm, tn, tk, res=None):
    if mode == "nn":
        (m, k), n = a.shape, b.shape[1]
    elif mode == "nt":
        (m, k), n = a.shape, b.shape[0]
    else:
        (k, m), n = a.shape, b.shape[1]
    tm, tn, tk = min(tm, m), min(tn, n), min(tk, k)
    assert m % tm == 0 and n % tn == 0 and k % tk == 0, (name, m, n, k, tm, tn, tk)
    if mode == "nn":
        a_spec = pl.BlockSpec((tm, tk), lambda i, j, kk: (i, kk))
        b_spec = pl.BlockSpec((tk, tn), lambda i, j, kk: (kk, j))
        dims = NN
    elif mode == "nt":
        a_spec = pl.BlockSpec((tm, tk), lambda i, j, kk: (i, kk))
        b_spec = pl.BlockSpec((tn, tk), lambda i, j, kk: (j, kk))
        dims = NT
    else:
        a_spec = pl.BlockSpec((tk, tm), lambda i, j, kk: (kk, i))
        b_spec = pl.BlockSpec((tk, tn), lambda i, j, kk: (kk, j))
        dims = TN
    nk = k // tk
    has_res = res is not None

    def body(*refs):
        a_ref, b_ref = refs[0], refs[1]
        r_ref = refs[2] if has_res else None
        o_ref = refs[3] if has_res else refs[2]
        part = _dot(a_ref[...], b_ref[...], dims)

        def finish(val):
            if has_res:
                val = val + r_ref[...]
            o_ref[...] = val.astype(o_ref.dtype)

        if nk == 1:
            finish(part)
        else:
            acc_ref = refs[-1]
            kk = pl.program_id(2)

            @pl.when(kk == 0)
            def _():
                acc_ref[...] = part

            @pl.when(kk > 0)
            def _():
                acc_ref[...] += part

            @pl.when(kk == nk - 1)
            def _():
                finish(acc_ref[...])

    in_specs = [a_spec, b_spec]
    args = [a, b]
    if has_res:
        in_specs.append(pl.BlockSpec((tm, tn), lambda i, j, kk: (i, j)))
        args.append(res)
    return pl.pallas_call(
        body, name=name, grid=(m // tm, n // tn, nk), in_specs=in_specs,
        out_specs=pl.BlockSpec((tm, tn), lambda i, j, kk: (i, j)),
        out_shape=jax.ShapeDtypeStruct((m, n), out_dtype),
        scratch_shapes=[pltpu.VMEM((tm, tn), F32)] if nk > 1 else [],
        compiler_params=_cparams(("arbitrary", "arbitrary", "arbitrary")),
    )(*args)


def _rms_fwd(name, x, gain, tm=256):
    s = x.shape[0]

    def fn(xb, g):
        r = lax.rsqrt(jnp.mean(xb * xb, axis=-1, keepdims=True) + EPS)
        return (xb * r * g,), ()

    return _rows(name, fn, s, tm, [_rb(x, tm, D), _cst(gain)], [(D, BF16)])[0]


def _rms_bwd(name, x, gain, dns, dres, tm=256):
    s = x.shape[0]
    n_dn = len(dns)

    def fn(xb, g, *rest):
        dn = rest[0]
        for t in rest[1:n_dn]:
            dn = dn + t
        r = lax.rsqrt(jnp.mean(xb * xb, axis=-1, keepdims=True) + EPS)
        xhat = xb * r
        dxh = dn * g
        dx = r * (dxh - xhat * jnp.mean(dxh * xhat, axis=-1, keepdims=True)) + rest[n_dn]
        return (dx,), (jnp.sum(dn * xhat, axis=0, keepdims=True),)

    ins = [_rb(x, tm, D), _cst(gain)] + [_rb(t, tm, D) for t in dns] + [_rb(dres, tm, D)]
    return _rows(name, fn, s, tm, ins, [(D, F32)], [(1, D)])


def _final(h2, target, gain, tm=256):
    s = h2.shape[0]

    def fn(hb, tb, g):
        r = lax.rsqrt(jnp.mean(hb * hb, axis=-1, keepdims=True) + EPS)
        xhat = hb * r
        e = xhat * g - tb
        dy = e * (1.0 / D)
        dxh = dy * g
        dh = r * (dxh - xhat * jnp.mean(dxh * xhat, axis=-1, keepdims=True))
        lrow = 0.5 * jnp.sum(jnp.sum(e * e, axis=-1, keepdims=True) * (1.0 / D), axis=0, keepdims=True)
        return (dh,), (jnp.sum(dy * xhat, axis=0, keepdims=True), jnp.broadcast_to(lrow, (1, LANES)))

    return _rows("final_norm_loss", fn, s, tm, [_rb(h2, tm, D), _rb(target, tm, D), _cst(gain)],
                 [(D, F32)], [(1, D), (1, LANES)])


def _merge_fwd(pa, pb, proj, tm=256):
    s = pa.shape[0]

    def fn(a, b, ga, gb):
        return (_sigmoid(ga) * a + _sigmoid(gb) * b,), ()

    ins = [_rb(pa, tm, D), _rb(pb, tm, D), _rb(proj, tm, D, 7), _rb(proj, tm, D, 8)]
    return _rows("merge_fwd", fn, s, tm, ins, [(D, BF16)])[0]


def _merge_bwd(dmerged, pa, pb, proj, tm=256):
    s = pa.shape[0]

    def fn(dm, a, b, ga, gb):
        sa, sb = _sigmoid(ga), _sigmoid(gb)
        return (dm * sa, dm * sb, dm * a * sa * (1.0 - sa), dm * b * sb * (1.0 - sb)), ()

    ins = [_rb(dmerged, tm, D), _rb(pa, tm, D), _rb(pb, tm, D), _rb(proj, tm, D, 7), _rb(proj, tm, D, 8)]
    return _rows("merge_bwd", fn, s, tm, ins, [(D, BF16), (D, BF16), (D, BF16), (D, BF16)])


def _gelu_parts(x):
    cdf = 0.5 * (1.0 + lax.erf(x * 0.7071067811865476))
    pdf = 0.3989422804014327 * jnp.exp(-0.5 * x * x)
    return x * cdf, cdf + x * pdf


def _conv_taps(u_ext, n_out, first):
    n = u_ext.shape[0]
    cur = u_ext[8:8 + n_out]
    m1 = pltpu.roll(u_ext, 1, 0)[8:8 + n_out]
    m2 = pltpu.roll(u_ext, 2, 0)[8:8 + n_out]
    return m2, m1, cur


def _convglu_fwd(u, conv_w8, conv_b, tm=64):
    s, w = u.shape
    tb = tm // 8

    def fn(ub, up, cw, cb):
        i = pl.program_id(0)
        up = jnp.where(i == 0, 0.0, up)
        m2, m1, cur = _conv_taps(jnp.concatenate([up, ub], axis=0), tm, None)
        acc = cb + cw[0:1] * m2 + cw[1:2] * m1 + cw[2:3] * cur
        act, _ = _gelu_parts(acc[:, :D_FF])
        return (act * acc[:, D_FF:],), ()

    ins = [_rb(u, tm, w), (u, (8, w), lambda i: (jnp.maximum(i * tb - 1, 0), 0)), _cst(conv_w8), _cst(conv_b)]
    return _rows("convglu_fwd", fn, s, tm, ins, [(D_FF, BF16)])[0]


def _convglu_bwd(u, dact, conv_w8, conv_b, tm=64):
    s, w = u.shape
    tb = tm // 8
    nb = s // tm

    def fn(ub, up, un, db, dn, cw, cb):
        i = pl.program_id(0)
        up = jnp.where(i == 0, 0.0, up)
        dn = jnp.where(i == nb - 1, 0.0, dn)
        ne = tm + 8
        m2, m1, cur = _conv_taps(jnp.concatenate([up, ub, un], axis=0), ne, None)
        acc = cb + cw[0:1] * m2 + cw[1:2] * m1 + cw[2:3] * cur
        de = jnp.concatenate([db, dn], axis=0)
        gl, dgl = _gelu_parts(acc[:, :D_FF])
        dacc = jnp.concatenate([de * acc[:, D_FF:] * dgl, de * gl], axis=1)
        p1 = pltpu.roll(dacc, ne - 1, 0)[:tm]
        p2 = pltpu.roll(dacc, ne - 2, 0)[:tm]
        d0 = dacc[:tm]
        du = cw[2:3] * d0 + cw[1:2] * p1 + cw[0:1] * p2
        zero5 = jnp.zeros((5, w), F32)
        dcw = jnp.concatenate([
            jnp.sum(d0 * m2[:tm], axis=0, keepdims=True), jnp.sum(d0 * m1[:tm], axis=0, keepdims=True),
            jnp.sum(d0 * cur[:tm], axis=0, keepdims=True), zero5], axis=0)
        return (du,), (dcw, jnp.sum(d0, axis=0, keepdims=True))

    ins = [
        _rb(u, tm, w),
        (u, (8, w), lambda i: (jnp.maximum(i * tb - 1, 0), 0)),
        (u, (8, w), lambda i: (jnp.minimum((i + 1) * tb, s // 8 - 1), 0)),
        _rb(dact, tm, D_FF),
        (dact, (8, D_FF), lambda i: (jnp.minimum((i + 1) * tb, s // 8 - 1), 0)),
        _cst(conv_w8), _cst(conv_b),
    ]
    return _rows("convglu_bwd", fn, s, tm, ins, [(w, BF16)], [(8, w), (1, w)])


def _hg_gates(hq, hf, lb):
    sq = _sigmoid(hq)
    q = hq * sq
    sg = _sigmoid(hf)
    f = lb + (1.0 - lb) * sg
    return q, sq, sg, f, 1.0 - f, jnp.log(f)


def _lb_of(logits):
    l0, l1 = logits[0:1], logits[1:2]
    mx = jnp.maximum(l0, l1)
    e0, e1 = jnp.exp(l0 - mx), jnp.exp(l1 - mx)
    return e0 / (e0 + e1)


def _tri(n, lower):
    r = lax.broadcasted_iota(jnp.int32, (n, n), 0)
    c = lax.broadcasted_iota(jnp.int32, (n, n), 1)
    return jnp.where((r >= c) if lower else (r <= c), 1.0, 0.0).astype(BF16)


def _hg_intra_terms(q, kk, b, t_iota):
    ws, ps = [], []
    for s in range(HG_C):
        p = jnp.where(t_iota >= s, jnp.exp(jnp.minimum(b - b[s:s + 1], 0.0)), 0.0)
        ps.append(p)
        ws.append(q * kk[s:s + 1] * p)
    return jnp.concatenate(ws, axis=0), ps


def _hg_fwd(proj, lb_logits):
    s = proj.shape[0]
    nt = s // HG_T
    nc = HG_T // HG_C

    def body(q_ref, f_ref, i_ref, l_ref, o_ref, st_ref, state):
        @pl.when(pl.program_id(1) == 0)
        def _():
            state[...] = jnp.zeros_like(state)

        st_ref[0, 0] = state[...]
        lb = _lb_of(l_ref[...])
        ones = jnp.ones((HG_DK, HG_DK), BF16)
        tril = _tri(HG_C, True)
        t_iota = lax.broadcasted_iota(jnp.int32, (HG_C, HG_DK), 0)

        def chunk(ci, carry):
            r = pl.ds(pl.multiple_of(ci * HG_C, HG_C), HG_C)
            q, _, _, _, kk, g = _hg_gates(q_ref[r, :], f_ref[r, :], lb)
            v = i_ref[r, :]
            b = _split_dot(tril, g, 3)
            b_end = b[HG_C - 1:HG_C]
            st = state[...]
            o = _dot(q * jnp.exp(b), st, NT)
            w_all, _ = _hg_intra_terms(q, kk, b, t_iota)
            a_all = _dot(w_all, ones, NN)
            for si in range(HG_C):
                o = o + a_all[si * HG_C:(si + 1) * HG_C] * v[si:si + 1]
            o_ref[r, :] = o
            state[...] = st * jnp.exp(b_end) + _dot(v, kk * jnp.exp(b_end - b), TN)
            return carry

        lax.fori_loop(0, nc, chunk, 0)

    col = lambda off: pl.BlockSpec((HG_T, HG_DK), lambda h, t: (t, off + h))
    return pl.pallas_call(
        body, name="hgrn2_fwd", grid=(HG_H, nt),
        in_specs=[col(0), col(8), col(16), pl.BlockSpec((2, HG_DK), lambda h, t: (0, h))],
        out_specs=[pl.BlockSpec((HG_T, HG_DK), lambda h, t: (t, h)),
                   pl.BlockSpec((1, 1, HG_DK, HG_DK), lambda h, t: (h, t, 0, 0))],
        out_shape=[jax.ShapeDtypeStruct((s, D), F32), jax.ShapeDtypeStruct((HG_H, nt, HG_DK, HG_DK), F32)],
        scratch_shapes=[pltpu.VMEM((HG_DK, HG_DK), F32)],
        compiler_params=_cparams(("arbitrary", "arbitrary")),
    )(proj, proj, proj, lb_logits)


def _hg_bwd(proj, lb_logits, states, do_raw):
    s = proj.shape[0]
    nt = s // HG_T
    nc = HG_T // HG_C

    def body(q_ref, f_ref, i_ref, l_ref, st_ref, do_ref, dq_ref, df_ref, di_ref, dl_ref, st_all, adj):
        tb = pl.program_id(1)

        @pl.when(tb == 0)
        def _():
            adj[...] = jnp.zeros_like(adj)
            dl_ref[...] = jnp.zeros_like(dl_ref)

        lb = _lb_of(l_ref[...])
        ones = jnp.ones((HG_DK, HG_DK), BF16)
        tril = _tri(HG_C, True)
        triu = _tri(HG_C, False)
        t_iota = lax.broadcasted_iota(jnp.int32, (HG_C, HG_DK), 0)

        def gates(ci):
            r = pl.ds(pl.multiple_of(ci * HG_C, HG_C), HG_C)
            hq, hf = q_ref[r, :], f_ref[r, :]
            q, sq, sg, f, kk, g = _hg_gates(hq, hf, lb)
            b = _split_dot(tril, g, 3)
            return r, hq, q, sq, sg, f, kk, b, i_ref[r, :]

        def fwd_chunk(ci, st):
            _, _, _, _, _, _, kk, b, v = gates(ci)
            st_all[ci] = st
            b_end = b[HG_C - 1:HG_C]
            return st * jnp.exp(b_end) + _dot(v, kk * jnp.exp(b_end - b), TN)

        lax.fori_loop(0, nc, fwd_chunk, st_ref[0, 0])

        def bwd_chunk(cj, dlb):
            ci = nc - 1 - cj
            r, hq, q, sq, sg, f, kk, b, v = gates(ci)
            do = do_ref[r, :]
            b_end = b[HG_C - 1:HG_C]
            e_b = jnp.exp(b)
            kd = kk * jnp.exp(b_end - b)
            st0 = st_all[ci]
            e = adj[...]
            st_end = st0 * jnp.exp(b_end) + _dot(v, kd, TN)
            tail = jnp.sum(e * st_end, axis=0, keepdims=True)
            w_all, ps = _hg_intra_terms(q, kk, b, t_iota)
            a_all = _dot(w_all, ones, NN)
            x_all = jnp.concatenate([do * v[si:si + 1] for si in range(HG_C)], axis=0)
            da_all = _dot(x_all, ones, NN)
            dq = e_b * _dot(do, st0, NN)
            dk = jnp.exp(b_end - b) * _dot(v, e, NN)
            dv = _dot(kd, e, NT)
            for si in range(HG_C):
                da = da_all[si * HG_C:(si + 1) * HG_C]
                aa = a_all[si * HG_C:(si + 1) * HG_C]
                dap = da * ps[si]
                dq = dq + dap * kk[si:si + 1]
                dk_row = jnp.sum(dap * q, axis=0, keepdims=True)
                dv_row = jnp.sum(aa * do, axis=0, keepdims=True)
                hit = t_iota == si
                dk = dk + jnp.where(hit, dk_row, 0.0)
                dv = dv + jnp.where(hit, dv_row, 0.0)
            adj[...] = e * jnp.exp(b_end) + _dot(do, q * e_b, TN)
            dg = _split_dot(triu, q * dq - kk * dk, 2) + tail
            dfg = dg / f - dk
            dq_ref[r, :] = (dq * sq * (1.0 + hq * (1.0 - sq))).astype(dq_ref.dtype)
            df_ref[r, :] = (dfg * (1.0 - lb) * sg * (1.0 - sg)).astype(df_ref.dtype)
            di_ref[r, :] = dv.astype(di_ref.dtype)
            return dlb + jnp.sum(dfg * (1.0 - sg), axis=0, keepdims=True)

        dlb = lax.fori_loop(0, nc, bwd_chunk, jnp.zeros((1, HG_DK), F32))
        dl0 = dlb * lb * (1.0 - lb)
        dl_ref[...] += jnp.concatenate([dl0, -dl0], axis=0)

    col = lambda off: pl.BlockSpec((HG_T, HG_DK), lambda h, t: (nt - 1 - t, off + h))
    out_col = pl.BlockSpec((HG_T, HG_DK), lambda h, t: (nt - 1 - t, h))
    return pl.pallas_call(
        body, name="hgrn2_bwd", grid=(HG_H, nt),
        in_specs=[col(0), col(8), col(16), pl.BlockSpec((2, HG_DK), lambda h, t: (0, h)),
                  pl.BlockSpec((1, 1, HG_DK, HG_DK), lambda h, t: (h, nt - 1 - t, 0, 0)), col(0)],
        out_specs=[out_col, out_col, out_col, pl.BlockSpec((2, HG_DK), lambda h, t: (0, h))],
        out_shape=[jax.ShapeDtypeStruct((s, D), BF16)] * 3 + [jax.ShapeDtypeStruct((2, D), F32)],
        scratch_shapes=[pltpu.VMEM((nc, HG_DK, HG_DK), F32), pltpu.VMEM((HG_DK, HG_DK), F32)],
        compiler_params=_cparams(("arbitrary", "arbitrary")),
    )(proj, proj, proj, lb_logits, states, do_raw)


def _hg_post_fwd(o_raw, proj, gnorm, tm=256):
    s = o_raw.shape[0]

    def fn(o, hg, gn):
        outs = []
        for h in range(HG_H):
            sl = slice(h * HG_DK, (h + 1) * HG_DK)
            oh, gh = o[:, sl], hg[:, sl]
            r = lax.rsqrt(jnp.mean(oh * oh, axis=-1, keepdims=True) + EPS)
            outs.append(oh * r * gn * (gh * _sigmoid(gh)))
        return (jnp.concatenate(outs, axis=1),), ()

    return _rows("hgrn2_out_fwd", fn, s, tm, [_rb(o_raw, tm, D), _rb(proj, tm, D, 3), _cst(gnorm)], [(D, BF16)])[0]


def _hg_post_bwd(do_a, o_raw, proj, gnorm, tm=256):
    s = o_raw.shape[0]

    def fn(da, o, hg, gn):
        dos, dhgs = [], []
        dgn = jnp.zeros((1, HG_DK), F32)
        for h in range(HG_H):
            sl = slice(h * HG_DK, (h + 1) * HG_DK)
            oh, gh, dh = o[:, sl], hg[:, sl], da[:, sl]
            r = lax.rsqrt(jnp.mean(oh * oh, axis=-1, keepdims=True) + EPS)
            xhat = oh * r
            sg = _sigmoid(gh)
            dy = dh * (gh * sg)
            dhgs.append(dh * xhat * gn * sg * (1.0 + gh * (1.0 - sg)))
            dgn = dgn + jnp.sum(dy * xhat, axis=0, keepdims=True)
            dxh = dy * gn
            dos.append(r * (dxh - xhat * jnp.mean(dxh * xhat, axis=-1, keepdims=True)))
        return (jnp.concatenate(dos, axis=1), jnp.concatenate(dhgs, axis=1)), (dgn,)

    ins = [_rb(do_a, tm, D), _rb(o_raw, tm, D), _rb(proj, tm, D, 3), _cst(gnorm)]
    return _rows("hgrn2_out_bwd", fn, s, tm, ins, [(D, F32), (D, BF16)], [(1, HG_DK)])


def _log_sigmoid(z):
    return jnp.minimum(z, 0.0) - jnp.log(1.0 + jnp.exp(-jnp.abs(z)))


def _fox_prep(pff, bias, tm=256):
    s = pff.shape[0]

    def body(p_ref, b_ref, ct_ref, carry):
        @pl.when(pl.program_id(0) == 0)
        def _():
            carry[...] = jnp.zeros_like(carry)

        lf = _log_sigmoid(p_ref[...] + b_ref[...])
        c = _split_dot(_tri(tm, True), lf, 3) + carry[0:1]
        carry[...] = jnp.broadcast_to(c[tm - 1:tm], carry.shape)
        ct_ref[...] = c.T

    return pl.pallas_call(
        body, name="fox_gate_cumsum", grid=(s // tm,),
        in_specs=[pl.BlockSpec((tm, LANES), lambda i: (i, 0)), pl.BlockSpec((1, LANES), lambda i: (0, 0))],
        out_specs=pl.BlockSpec((LANES, tm), lambda i: (0, i)),
        out_shape=jax.ShapeDtypeStruct((LANES, s), F32),
        scratch_shapes=[pltpu.VMEM((8, LANES), F32)],
        compiler_params=_cparams(("arbitrary",)),
    )(pff, bias)


def _fox_gate_bwd(dct, pff, bias, tm=256):
    s = pff.shape[0]
    nb = s // tm

    def body(d_ref, p_ref, b_ref, dff_ref, db_ref, carry):
        @pl.when(pl.program_id(0) == 0)
        def _():
            carry[...] = jnp.zeros_like(carry)
            db_ref[...] = jnp.zeros_like(db_ref)

        dc = d_ref[...].T
        dlf = _split_dot(_tri(tm, False), dc, 3) + carry[0:1]
        carry[...] = jnp.broadcast_to(dlf[0:1], carry.shape)
        dff = dlf * _sigmoid(-(p_ref[...] + b_ref[...]))
        dff_ref[...] = dff
        db_ref[...] += jnp.sum(dff, axis=0, keepdims=True)

    return pl.pallas_call(
        body, name="fox_gate_bwd", grid=(nb,),
        in_specs=[pl.BlockSpec((LANES, tm), lambda i: (0, nb - 1 - i)),
                  pl.BlockSpec((tm, LANES), lambda i: (nb - 1 - i, 0)), pl.BlockSpec((1, LANES), lambda i: (0, 0))],
        out_specs=[pl.BlockSpec((tm, LANES), lambda i: (nb - 1 - i, 0)), pl.BlockSpec((1, LANES), lambda i: (0, 0))],
        out_shape=[jax.ShapeDtypeStruct((s, LANES), F32), jax.ShapeDtypeStruct((1, LANES), F32)],
        scratch_shapes=[pltpu.VMEM((8, LANES), F32)],
        compiler_params=_cparams(("arbitrary",)),
    )(dct, pff, bias)


def _diag_mask(t):
    r = lax.broadcasted_iota(jnp.int32, (t, t), 0)
    c = lax.broadcasted_iota(jnp.int32, (t, t), 1)
    return r >= c


def _fox_fwd(q, k, v, crow):
    h, s, dh = q.shape
    t = min(FOX_T, s)
    nq = s // t

    def body(q_ref, k_ref, v_ref, c_ref, o_ref, lse_ref):
        i = pl.program_id(1)
        qs = q_ref[0] * 0.125

        def step(j, carry, masked):
            m, l, acc = carry
            rows = pl.ds(pl.multiple_of(j * t, t), t)
            sc = _dot(qs, k_ref[0, rows, :], NT) - c_ref[0, j]
            if masked:
                sc = jnp.where(_diag_mask(t), sc, -1e30)
            m_new = jnp.maximum(m, jnp.max(sc, axis=-1, keepdims=True))
            p = jnp.exp(sc - m_new)
            alpha = jnp.exp(m - m_new)
            l = alpha * l + jnp.sum(p, axis=-1, keepdims=True)
            acc = alpha * acc + _dot(p, v_ref[0, rows, :], NN)
            return m_new, l, acc

        init = (jnp.full((t, 1), -1e30, F32), jnp.zeros((t, 1), F32), jnp.zeros((t, dh), F32))
        carry = lax.fori_loop(0, i, lambda j, c: step(j, c, False), init)
        m, l, acc = step(i, carry, True)
        o_ref[0] = (acc / l).astype(o_ref.dtype)
        lse_ref[0] = m + jnp.log(l)

    whole = pl.BlockSpec((1, s, dh), lambda hh, i: (hh, 0, 0))
    blk = pl.BlockSpec((1, t, dh), lambda hh, i: (hh, i, 0))
    return pl.pallas_call(
        body, name="fox_attn_fwd", grid=(h, nq),
        in_specs=[blk, whole, whole, pl.BlockSpec((1, nq, 1, t), lambda hh, i: (hh, 0, 0, 0))],
        out_specs=[blk, pl.BlockSpec((1, t, 1), lambda hh, i: (hh, i, 0))],
        out_shape=[jax.ShapeDtypeStruct((h, s, dh), BF16), jax.ShapeDtypeStruct((h, s, 1), F32)],
        compiler_params=_cparams(("arbitrary", "arbitrary")),
    )(q, k, v, crow)


def _fox_bwd(q, k, v, crow, o, lse, do):
    h, s, dh = q.shape
    t = min(FOX_T, s)
    nq = s // t

    def body(q_ref, k_ref, v_ref, c_ref, o_ref, lse_ref, do_ref, dq_ref, dk_ref, dv_ref, dc_ref):
        i = pl.program_id(1)

        @pl.when(i == 0)
        def _():
            dk_ref[...] = jnp.zeros_like(dk_ref)
            dv_ref[...] = jnp.zeros_like(dv_ref)
            dc_ref[...] = jnp.zeros_like(dc_ref)

        qs = q_ref[0] * 0.125
        dob = do_ref[0]
        lse_b = lse_ref[0]
        delta = jnp.sum(dob.astype(F32) * o_ref[0].astype(F32), axis=-1, keepdims=True)

        def step(j, carry, masked):
            dq, rsum = carry
            rows = pl.ds(pl.multiple_of(j * t, t), t)
            kj, vj = k_ref[0, rows, :], v_ref[0, rows, :]
            sc = _dot(qs, kj, NT) - c_ref[0, j]
            if masked:
                sc = jnp.where(_diag_mask(t), sc, -1e30)
            p = jnp.exp(sc - lse_b)
            ds = p * (_dot(dob, vj, NT) - delta)
            dk_ref[0, rows, :] += _dot(ds, qs, TN)
            dv_ref[0, rows, :] += _dot(p, dob, TN)
            dc_ref[0, j] += -jnp.sum(ds, axis=0, keepdims=True)
            return dq + _dot(ds, kj, NN), rsum + _split_dot(ones8, ds, 3, NT)

        ones8 = jnp.ones((8, t), BF16)
        init = (jnp.zeros((t, dh), F32), jnp.zeros((8, t), F32))
        dq, rsum = step(i, lax.fori_loop(0, i, lambda j, c: step(j, c, False), init), True)
        dc_ref[0, i] += rsum[0:1]
        dq_ref[0] = (dq * 0.125).astype(dq_ref.dtype)

    whole = pl.BlockSpec((1, s, dh), lambda hh, i: (hh, 0, 0))
    blk = pl.BlockSpec((1, t, dh), lambda hh, i: (hh, i, 0))
    crow_spec = pl.BlockSpec((1, nq, 1, t), lambda hh, i: (hh, 0, 0, 0))
    return pl.pallas_call(
        body, name="fox_attn_bwd", grid=(h, nq),
        in_specs=[blk, whole, whole, crow_spec, blk, pl.BlockSpec((1, t, 1), lambda hh, i: (hh, i, 0)), blk],
        out_specs=[blk, whole, whole, crow_spec],
        out_shape=[jax.ShapeDtypeStruct((h, s, dh), BF16), jax.ShapeDtypeStruct((h, s, dh), F32),
                   jax.ShapeDtypeStruct((h, s, dh), F32), jax.ShapeDtypeStruct((h, nq, 1, t), F32)],
        compiler_params=_cparams(("arbitrary", "arbitrary")),
    )(q, k, v, crow, o, lse, do)


def _to_heads(x2d):
    s = x2d.shape[0]
    return x2d.astype(BF16).reshape(s, FOX_H, FOX_D).transpose(1, 0, 2)


def _from_heads(x3d):
    return x3d.transpose(1, 0, 2).reshape(x3d.shape[1], FOX_H * FOX_D)


def _adamw(name, w, g, m, v, tm=None):
    rows, width = w.shape
    tm = rows if tm is None else tm
    c1 = 1.0 - ADAM_B1 ** ADAM_STEP
    c2 = 1.0 - ADAM_B2 ** ADAM_STEP

    def fn(wb, gb, mb, vb):
        m_new = ADAM_B1 * mb + (1.0 - ADAM_B1) * gb
        v_new = ADAM_B2 * vb + (1.0 - ADAM_B2) * (gb * gb)
        delta = -ADAM_LR * ((m_new / c1) / (jnp.sqrt(v_new / c2) + ADAM_EPS) + ADAM_WD * wb)
        return (delta, m_new, v_new), ()

    ins = [_rb(a, tm, width) for a in (w, g, m, v)]
    return _rows(name, fn, rows, tm, ins, [(width, F32)] * 3)


def _me():
    return lax.axis_index("x"), lax.axis_index("y"), lax.axis_index("c")


def _all_gather8(name, block):
    m, n = block.shape

    def body(x_ref, out_ref, send_sems, recv_sems, local_sem):
        x, y, c = _me()
        me, sibling = (x, y, c), (x, y, 1 - c)
        chips = [(1 - x, y), (x, 1 - y), (1 - x, 1 - y)]

        def slot(px, py, pc):
            return out_ref.at[4 * px + 2 * py + pc]

        def copy(k, blk, to, src=None):
            return pltpu.make_async_remote_copy(
                src_ref=slot(*blk) if src is None else src, dst_ref=slot(*blk),
                send_sem=send_sems.at[k], recv_sem=recv_sems.at[k], device_id=to, device_id_type=MESH)

        mine = pltpu.make_async_copy(x_ref, slot(*me), local_sem)
        mine.start()
        first = [copy(0, me, sibling, src=x_ref)]
        first += [copy(1 + j, me, (*chip, c), src=x_ref) for j, chip in enumerate(chips)]
        for cp in first:
            cp.start()
        passed = [copy(4 + j, (*chip, c), sibling) for j, chip in enumerate(chips)]
        for j, chip in enumerate(chips):
            copy(1 + j, (*chip, c), me).wait_recv()
            passed[j].start()
        copy(0, sibling, me).wait_recv()
        for j, chip in enumerate(chips):
            copy(4 + j, (*chip, 1 - c), me).wait_recv()
        for cp in first + passed:
            cp.wait_send()
        mine.wait()

    return pl.pallas_call(
        body, name=name, in_specs=[ANY], out_specs=ANY,
        out_shape=jax.ShapeDtypeStruct((8, m, n), block.dtype),
        scratch_shapes=[pltpu.SemaphoreType.DMA((7,)), pltpu.SemaphoreType.DMA((7,)), pltpu.SemaphoreType.DMA],
    )(block)


def _swap_sibling(name, g2, pick_other):
    shape = g2.shape[1:]

    def body(g_ref, out_ref, send_sem, recv_sem):
        x, y, c = _me()
        src = g_ref.at[1 - c] if pick_other else g_ref.at[0]
        cp = pltpu.make_async_remote_copy(src_ref=src, dst_ref=out_ref, send_sem=send_sem, recv_sem=recv_sem,
                                          device_id=(x, y, 1 - c), device_id_type=MESH)
        cp.start()
        cp.wait()

    return pl.pallas_call(
        body, name=name, in_specs=[ANY], out_specs=ANY, out_shape=jax.ShapeDtypeStruct(shape, g2.dtype),
        scratch_shapes=[pltpu.SemaphoreType.DMA, pltpu.SemaphoreType.DMA],
    )(g2)


def _chip_exchange(name, p):
    def body(p_ref, out_ref, send_sems, recv_sems, local_sem):
        x, y, c = _me()
        my_chip = 2 * x + y
        chips = [(1 - x, y), (x, 1 - y), (1 - x, 1 - y)]
        mine = pltpu.make_async_copy(p_ref.at[my_chip], out_ref.at[my_chip], local_sem)
        mine.start()
        sends = []
        for k, (px, py) in enumerate(chips):
            sends.append(pltpu.make_async_remote_copy(
                src_ref=p_ref.at[2 * px + py], dst_ref=out_ref.at[my_chip], send_sem=send_sems.at[k],
                recv_sem=recv_sems.at[k], device_id=(px, py, c), device_id_type=MESH))
        for cp in sends:
            cp.start()
        for k, (px, py) in enumerate(chips):
            pltpu.make_async_remote_copy(
                src_ref=p_ref.at[my_chip], dst_ref=out_ref.at[2 * px + py], send_sem=send_sems.at[k],
                recv_sem=recv_sems.at[k], device_id=(px, py, c), device_id_type=MESH).wait_recv()
        for cp in sends:
            cp.wait_send()
        mine.wait()

    return pl.pallas_call(
        body, name=name, in_specs=[ANY], out_specs=ANY, out_shape=jax.ShapeDtypeStruct(p.shape, p.dtype),
        scratch_shapes=[pltpu.SemaphoreType.DMA((3,)), pltpu.SemaphoreType.DMA((3,)), pltpu.SemaphoreType.DMA],
    )(p)


def _all_reduce_small(name, block):
    r, n = block.shape

    def body(x_ref, sum_ref, gath, send_sems, recv_sems):
        x, y, c = _me()
        me = 4 * x + 2 * y + c
        gath[me] = x_ref[...]
        sends = []
        for k in range(1, 8):
            px = x ^ ((k >> 2) & 1)
            py = y ^ ((k >> 1) & 1)
            pc = c ^ (k & 1)
            sends.append(pltpu.make_async_remote_copy(
                src_ref=x_ref, dst_ref=gath.at[me], send_sem=send_sems.at[k - 1], recv_sem=recv_sems.at[k - 1],
                device_id=(px, py, pc), device_id_type=MESH))
        for cp in sends:
            cp.start()
        for k in range(1, 8):
            peer = me ^ k
            pltpu.make_async_remote_copy(
                src_ref=x_ref, dst_ref=gath.at[peer], send_sem=send_sems.at[k - 1], recv_sem=recv_sems.at[k - 1],
                device_id=(x, y, c), device_id_type=MESH).wait_recv()
        for cp in sends:
            cp.wait_send()
        acc = gath[0]
        for d in range(1, 8):
            acc = acc + gath[d]
        sum_ref[...] = acc

    vm = pl.BlockSpec(memory_space=pltpu.VMEM)
    return pl.pallas_call(
        body, name=name, in_specs=[vm], out_specs=vm, out_shape=jax.ShapeDtypeStruct((r, n), F32),
        scratch_shapes=[pltpu.VMEM((8, r, n), F32), pltpu.SemaphoreType.DMA((7,)), pltpu.SemaphoreType.DMA((7,))],
    )(block)


def _add2(name, a, b, tm):
    rows = a.shape[0]
    return _rows(name, lambda p, q: ((p + q,), ()), rows, tm, [_rb(a, tm, LANES), _rb(b, tm, LANES)], [(LANES, F32)])[0]


def _add4(name, p, tm):
    m = p.shape[1]
    flat = p.reshape(4 * m, LANES)
    nb = m // tm
    ins = [(flat, (tm, LANES), (lambda i, j=j: (j * nb + i, 0))) for j in range(4)]
    return _rows(name, lambda a, b, c, d: ((((a + b) + c) + d,), ()), m, tm, ins, [(LANES, F32)])[0]


SEG_ROWS = (D * W_IN_SHARD // LANES, 256 * D // LANES, 256 * D // LANES, 256 * D // LANES,
            D * W_UP_SHARD // LANES, W_DOWN_SHARD * D // LANES)
GRAD_ROWS = sum(SEG_ROWS)
CONVW_ROWS = 3 * W_UP_SHARD * 2 // LANES
GATHER_ROWS = 41600


def _flat(a):
    return a.reshape(-1, LANES)


def _gather_weights(w_in, w_a, w_b, w_out, w_up, w_down, conv_w):
    c = lax.axis_index("c")
    bits = lax.bitcast_convert_type(conv_w, BF16)
    pieces = [_flat(t.astype(BF16)) for t in (w_in, w_a, w_b, w_out, w_up, w_down)] + [_flat(bits)]
    pad = GATHER_ROWS - GRAD_ROWS - CONVW_ROWS
    shard = jnp.concatenate(pieces + [jnp.zeros((pad, LANES), BF16)], axis=0)
    half = GATHER_ROWS // 2
    mine = lax.dynamic_slice_in_dim(shard, c * half, half, axis=0)
    full = _all_gather8("all_gather_weights", mine).reshape(N_CHIP, GATHER_ROWS, LANES)
    offs = [0]
    for r in SEG_ROWS:
        offs.append(offs[-1] + r)
    seg = lambda i: full[:, offs[i]:offs[i + 1]]
    wi = seg(0).reshape(N_CHIP, D, W_IN_SHARD).transpose(1, 0, 2).reshape(D, N_CHIP * W_IN_SHARD)
    w_main = jnp.concatenate([wi[:, :FF_COL], wi[:, FF_COL + FOX_H:]], axis=1)
    w_ff = jnp.pad(wi[:, FF_COL:FF_COL + FOX_H], ((0, 0), (0, LANES - FOX_H)))
    wa, wb, wo = (seg(i).reshape(D, D) for i in (1, 2, 3))
    wu = seg(4).reshape(N_CHIP, D, W_UP_SHARD).transpose(1, 0, 2).reshape(D, 2 * D_FF)
    wd = seg(5).reshape(D_FF, D)
    cw_bits = full[:, GRAD_ROWS:GRAD_ROWS + CONVW_ROWS].reshape(N_CHIP, 3, W_UP_SHARD, 2)
    cw = lax.bitcast_convert_type(cw_bits, F32).transpose(1, 0, 2).reshape(3, 2 * D_FF)
    return w_main, w_ff, wa, wb, wo, wu, wd, cw


def _reduce_scatter_grads(d_main, d_ff, d_a, d_b, d_o, d_u, d_d):
    c = lax.axis_index("c")
    d_in = jnp.concatenate([d_main[:, :FF_COL], d_ff[:, :FOX_H], d_main[:, FF_COL:]], axis=1)
    per_chip = [
        d_in.reshape(D, N_CHIP, W_IN_SHARD).transpose(1, 0, 2).reshape(N_CHIP, -1, LANES),
        d_a.reshape(N_CHIP, -1, LANES), d_b.reshape(N_CHIP, -1, LANES), d_o.reshape(N_CHIP, -1, LANES),
        d_u.reshape(D, N_CHIP, W_UP_SHARD).transpose(1, 0, 2).reshape(N_CHIP, -1, LANES),
        d_d.reshape(N_CHIP, -1, LANES),
        jnp.zeros((N_CHIP, GATHER_ROWS - GRAD_ROWS, LANES), F32),
    ]
    half = GATHER_ROWS // 2
    g = jnp.concatenate(per_chip, axis=1).reshape(N_CHIP, 2, half, LANES).transpose(1, 0, 2, 3)
    from_sibling = _swap_sibling("grad_swap_halves", g, True)
    mine = lax.dynamic_index_in_dim(g, c, axis=0, keepdims=False)
    tm = half // 5
    chip_sum = _add2("grad_chip_sum", mine.reshape(-1, LANES), from_sibling.reshape(-1, LANES), tm)
    pieces = _chip_exchange("grad_chip_exchange", chip_sum.reshape(N_CHIP, half, LANES))
    mine_half = _add4("grad_sum_chips", pieces, tm)
    other_half = _swap_sibling("grad_share_half", mine_half[None], False)
    lo = jnp.where(c == 0, mine_half, other_half)
    hi = jnp.where(c == 0, other_half, mine_half)
    return jnp.concatenate([lo, hi], axis=0)


def _local_step(x, target, norm_mix, fox_f_bias, hg_lb_logits, hg_norm, norm_ffn, conv_b, norm_final,
                w_main, w_ff, wa, wb, wo, wu, wd, conv_w):
    s = x.shape[0]
    bias = jnp.pad(fox_f_bias, ((0, 0), (0, LANES - FOX_H)))
    conv_w8 = jnp.pad(conv_w, ((0, 5), (0, 0)))
    t = min(FOX_T, s)

    n1 = _rms_fwd("norm_mix_fwd", x, norm_mix)
    proj = _mm("in_proj", n1, w_main, "nn", F32, 1024, 1024, D)
    pff = _mm("in_proj_forget", n1, w_ff, "nn", F32, 1024, LANES, D)
    ct = _fox_prep(pff, bias)
    crow = ct[:FOX_H].reshape(FOX_H, s // t, 1, t)
    qh, kh, vh = (_to_heads(proj[:, (4 + i) * D:(5 + i) * D]) for i in range(3))
    ob_h, lse = _fox_fwd(qh, kh, vh, crow)
    o_b = _from_heads(ob_h)
    o_raw, states = _hg_fwd(proj, hg_lb_logits)
    o_a = _hg_post_fwd(o_raw, proj, hg_norm)
    pa = _mm("branch_a", o_a, wa, "nn", F32, 1024, 1024, D)
    pb = _mm("branch_b", o_b, wb, "nn", F32, 1024, 1024, D)
    merged = _merge_fwd(pa, pb, proj)
    h1 = _mm("out_proj", merged, wo, "nn", F32, 1024, 1024, D, res=x)
    n2 = _rms_fwd("norm_ffn_fwd", h1, norm_ffn)
    u = _mm("ffn_up", n2, wu, "nn", F32, 1024, W_UP_SHARD, D)
    act = _convglu_fwd(u, conv_w8, conv_b)
    h2 = _mm("ffn_down", act, wd, "nn", F32, 512, 1024, D_FF, res=h1)
    (dh2,), (d_norm_final, loss_row) = _final(h2, target, norm_final)

    dact = _mm("ffn_down_dx", dh2, wd, "nt", BF16, 1024, D_FF, D)
    d_wd = _mm("ffn_down_dw", act, dh2, "tn", F32, D_FF // 2, 1024, 512)
    (du,), (d_conv_w8, d_conv_b) = _convglu_bwd(u, dact, conv_w8, conv_b)
    dn2 = _mm("ffn_up_dx", du, wu, "nt", F32, 1024, 1024, W_UP_SHARD)
    d_wu = _mm("ffn_up_dw", n2, du, "tn", F32, 1024, W_UP_SHARD, 512)
    (dh1,), (d_norm_ffn,) = _rms_bwd("norm_ffn_bwd", h1, norm_ffn, [dn2], dh2)

    dmerged = _mm("out_proj_dx", dh1, wo, "nt", F32, 1024, 1024, D)
    d_wo = _mm("out_proj_dw", merged, dh1, "tn", F32, 1024, 1024, 512)
    dpa, dpb, dga, dgb = _merge_bwd(dmerged, pa, pb, proj)
    do_a = _mm("branch_a_dx", dpa, wa, "nt", F32, 1024, 1024, D)
    do_b = _mm("branch_b_dx", dpb, wb, "nt", BF16, 1024, 1024, D)
    d_wa = _mm("branch_a_dw", o_a, dpa, "tn", F32, 1024, 1024, 512)
    d_wb = _mm("branch_b_dw", o_b, dpb, "tn", F32, 1024, 1024, 512)

    (do_raw, dhg), (d_hg_norm,) = _hg_post_bwd(do_a, o_raw, proj, hg_norm)
    dhq, dhf, dhi, d_lb_logits = _hg_bwd(proj, hg_lb_logits, states, do_raw)

    dqh, dkh, dvh, dcrow = _fox_bwd(qh, kh, vh, crow, ob_h, lse, _to_heads(do_b))
    dct = jnp.pad(dcrow.reshape(FOX_H, s), ((0, LANES - FOX_H), (0, 0)))
    dff, d_bias = _fox_gate_bwd(dct, pff, bias)
    dfq, dfk, dfv = _from_heads(dqh), _from_heads(dkh.astype(BF16)), _from_heads(dvh.astype(BF16))

    dproj = jnp.concatenate([dhq, dhf, dhi, dhg, dfq, dfk, dfv, dga, dgb], axis=1)
    dn1_main = _mm("in_proj_dx", dproj, w_main, "nt", F32, 1024, 1024, 1024)
    dn1_ff = _mm("in_proj_forget_dx", dff, w_ff, "nt", F32, 1024, 1024, LANES)
    d_w_main = _mm("in_proj_dw", n1, dproj, "tn", F32, 1024, 1024, 512)
    d_w_ff = _mm("in_proj_forget_dw", n1, dff, "tn", F32, 1024, LANES, 512)
    (dx,), (d_norm_mix,) = _rms_bwd("norm_mix_bwd", x, norm_mix, [dn1_main, dn1_ff], dh1)

    small = dict(norm_mix=d_norm_mix, fox_f_bias=d_bias[:, :FOX_H], hg_lb_logits=d_lb_logits, hg_norm=d_hg_norm,
                 norm_ffn=d_norm_ffn, conv_b=d_conv_b, norm_final=d_norm_final, conv_w=d_conv_w8[:3], loss=loss_row)
    big = (d_w_main, d_w_ff, d_wa, d_wb, d_wo, d_wu, d_wd)
    return dx, small, big


SMALL_KEYS = ("norm_mix", "fox_f_bias", "hg_lb_logits", "hg_norm", "norm_ffn", "conv_b", "norm_final")


def _pack_small(parts):
    rows, layout = [], []
    for key, arr in parts:
        flat = arr.reshape(-1)
        n = flat.shape[0]
        nr = -(-n // LANES)
        rows.append(jnp.pad(flat, (0, nr * LANES - n)).reshape(nr, LANES))
        layout.append((key, arr.shape, n, nr))
    packed = jnp.concatenate(rows, axis=0)
    pad = -packed.shape[0] % 8
    return jnp.pad(packed, ((0, pad), (0, 0))), layout


def _unpack_small(packed, layout):
    out, r0 = {}, 0
    for key, shape, n, nr in layout:
        out[key] = packed[r0:r0 + nr].reshape(-1)[:n].reshape(shape)
        r0 += nr
    return out


def kernel(x, norm_mix, w_in, fox_f_bias, hg_lb_logits, hg_norm, w_branch_a, w_branch_b, w_out, norm_ffn, w_up, conv_w, conv_b, w_down, norm_final, loss_target, m_norm_mix, m_w_in, m_fox_f_bias, m_hg_lb_logits, m_hg_norm, m_w_branch_a, m_w_branch_b, m_w_out, m_norm_ffn, m_w_up, m_conv_w, m_conv_b, m_w_down, m_norm_final, v_norm_mix, v_w_in, v_fox_f_bias, v_hg_lb_logits, v_hg_norm, v_w_branch_a, v_w_branch_b, v_w_out, v_norm_ffn, v_w_up, v_conv_w, v_conv_b, v_w_down, v_norm_final):
    chip = 2 * lax.axis_index("x") + lax.axis_index("y")
    w_main, w_ff, wa, wb, wo, wu, wd, cw = _gather_weights(
        w_in[0], w_branch_a[0], w_branch_b[0], w_out[0], w_up[0], w_down[0], conv_w[0])
    dx, small, big = _local_step(
        x[0], loss_target[0], norm_mix, fox_f_bias, hg_lb_logits, hg_norm, norm_ffn, conv_b,
        norm_final.reshape(1, D), w_main, w_ff, wa, wb, wo, wu, wd, cw)

    packed, layout = _pack_small([(k, small[k]) for k in SMALL_KEYS + ("conv_w", "loss")])
    red = _unpack_small(_all_reduce_small("all_reduce_small", packed), layout)
    loss = red["loss"][0, 0]
    g_conv_w = lax.dynamic_slice_in_dim(red["conv_w"], chip * W_UP_SHARD, W_UP_SHARD, axis=1)

    gflat = _reduce_scatter_grads(*big)
    offs = [0]
    for r in SEG_ROWS:
        offs.append(offs[-1] + r)
    shapes = [(D, W_IN_SHARD), (256, D), (256, D), (256, D), (D, W_UP_SHARD), (W_DOWN_SHARD, D)]
    g_big = [gflat[offs[i]:offs[i + 1]].reshape(shapes[i]) for i in range(6)]

    names = ["norm_mix", "w_in", "fox_f_bias", "hg_lb_logits", "hg_norm", "w_branch_a", "w_branch_b", "w_out",
             "norm_ffn", "w_up", "conv_w", "conv_b", "w_down", "norm_final"]
    weights = dict(norm_mix=norm_mix, w_in=w_in, fox_f_bias=fox_f_bias, hg_lb_logits=hg_lb_logits, hg_norm=hg_norm,
                   w_branch_a=w_branch_a, w_branch_b=w_branch_b, w_out=w_out, norm_ffn=norm_ffn, w_up=w_up,
                   conv_w=conv_w, conv_b=conv_b, w_down=w_down, norm_final=norm_final)
    ms = dict(norm_mix=m_norm_mix, w_in=m_w_in, fox_f_bias=m_fox_f_bias, hg_lb_logits=m_hg_lb_logits,
              hg_norm=m_hg_norm, w_branch_a=m_w_branch_a, w_branch_b=m_w_branch_b, w_out=m_w_out,
              norm_ffn=m_norm_ffn, w_up=m_w_up, conv_w=m_conv_w, conv_b=m_conv_b, w_down=m_w_down,
              norm_final=m_norm_final)
    vs = dict(norm_mix=v_norm_mix, w_in=v_w_in, fox_f_bias=v_fox_f_bias, hg_lb_logits=v_hg_lb_logits,
              hg_norm=v_hg_norm, w_branch_a=v_w_branch_a, w_branch_b=v_w_branch_b, w_out=v_w_out,
              norm_ffn=v_norm_ffn, w_up=v_w_up, conv_w=v_conv_w, conv_b=v_conv_b, w_down=v_w_down,
              norm_final=v_norm_final)

    grads, deltas, new_m, new_v = {}, {}, {}, {}
    big_names = ["w_in", "w_branch_a", "w_branch_b", "w_out", "w_up", "w_down"]
    for name, g2 in zip(big_names, g_big):
        shape = weights[name].shape
        rows = g2.shape[0]
        d_, m_, v_ = _adamw("adamw_" + name, weights[name][0], g2, ms[name][0], vs[name][0], tm=rows // 8)
        grads[name], deltas[name], new_m[name], new_v[name] = (a.reshape(shape) for a in (g2, d_, m_, v_))
    shape = conv_w.shape
    d_, m_, v_ = _adamw("adamw_conv_w", conv_w[0], g_conv_w, m_conv_w[0], v_conv_w[0])
    grads["conv_w"], deltas["conv_w"], new_m["conv_w"], new_v["conv_w"] = (
        a.reshape(shape) for a in (g_conv_w, d_, m_, v_))
    gs = {k: red[k].reshape(weights[k].shape) for k in SMALL_KEYS}
    pw, lay = _pack_small([(k, weights[k]) for k in SMALL_KEYS])
    pg, _ = _pack_small([(k, gs[k]) for k in SMALL_KEYS])
    pm, _ = _pack_small([(k, ms[k]) for k in SMALL_KEYS])
    pv, _ = _pack_small([(k, vs[k]) for k in SMALL_KEYS])
    d_, m_, v_ = (_unpack_small(a, lay) for a in _adamw("adamw_small", pw, pg, pm, pv))
    for k in SMALL_KEYS:
        grads[k], deltas[k], new_m[k], new_v[k] = gs[k], d_[k], m_[k], v_[k]

    return (loss, dx[None], *[grads[n] for n in names], *[deltas[n] for n in names],
            *[new_m[n] for n in names], *[new_v[n] for n in names])
```

```python
import functools

import jax
import jax.numpy as jnp
from jax import lax
from jax.experimental import pallas as pl
from jax.experimental.pallas import tpu as pltpu

F32 = jnp.float32
BF16 = jnp.bfloat16

D = 1024
HG_H, HG_DK = 8, 128
FOX_H, FOX_D = 16, 64
D_FF = 2816
EPS = 1e-6
N_CHIP = 4
LANES = 128
W_IN_SHARD = 2308
W_UP_SHARD = 1408
W_DOWN_SHARD = 704
FF_COL = 7168
ADAM_LR, ADAM_B1, ADAM_B2, ADAM_EPS, ADAM_WD, ADAM_STEP = 0.001, 0.9, 0.999, 1e-08, 0.01, 10

HG_C = 16
HG_T = 256
HG_UNROLL = 4
HG_UNROLL_BWD = 2
FOX_T = 512
VMEM_LIMIT = 56 * 1024 * 1024
MESH = pl.DeviceIdType.MESH
ANY = pl.BlockSpec(memory_space=pl.ANY)


def _cparams(sem):
    return pltpu.CompilerParams(dimension_semantics=sem, vmem_limit_bytes=VMEM_LIMIT)


def _sigmoid(x):
    return 1.0 / (1.0 + jnp.exp(-x))


def _dot(a, b, dims):
    return lax.dot_general(a.astype(BF16), b.astype(BF16), (dims, ((), ())), preferred_element_type=F32)


NN = ((1,), (0,))
NT = ((1,), (1,))
TN = ((0,), (0,))


def _split_dot(tri, x, parts, dims=NN):
    acc = None
    r = x
    for _ in range(parts):
        p = r.astype(BF16)
        t = lax.dot_general(tri, p, (dims, ((), ())), preferred_element_type=F32)
        acc = t if acc is None else acc + t
        r = r - p.astype(F32)
    return acc


def _rb(arr, tm, width, cb=0):
    return (arr, (tm, width), lambda i: (i, cb))


def _cst(arr):
    return (arr, arr.shape, lambda i: (0,) * arr.ndim)


def _rows(name, fn, n_rows, tm, ins, outs, accs=(), reverse=False):
    n_in, n_out, n_acc = len(ins), len(outs), len(accs)
    nb = n_rows // tm

    def body(*refs):
        vals = [r[...] for r in refs[:n_in]]
        o, a = fn(*vals)
        for r, v in zip(refs[n_in:n_in + n_out], o):
            r[...] = v.astype(r.dtype)
        if n_acc:
            acc_refs = refs[n_in + n_out:]

            @pl.when(pl.program_id(0) == 0)
            def _():
                for r in acc_refs:
                    r[...] = jnp.zeros_like(r)

            for r, v in zip(acc_refs, a):
                r[...] += v

    if reverse:
        rowmap = lambda i: (nb - 1 - i, 0)
    else:
        rowmap = lambda i: (i, 0)
    in_specs = [pl.BlockSpec(bs, im) for (_, bs, im) in ins]
    out_specs = [pl.BlockSpec((tm, w), rowmap) for (w, _) in outs]
    out_specs += [pl.BlockSpec((r, w), lambda i: (0, 0)) for (r, w) in accs]
    out_shape = [jax.ShapeDtypeStruct((n_rows, w), dt) for (w, dt) in outs]
    out_shape += [jax.ShapeDtypeStruct((r, w), F32) for (r, w) in accs]
    res = pl.pallas_call(
        body, name=name, grid=(nb,), in_specs=in_specs, out_specs=out_specs, out_shape=out_shape,
        compiler_params=_cparams(("arbitrary",)),
    )(*[a for a, _, _ in ins])
    return (res[:n_out], res[n_out:]) if n_acc else res


def _mm(name, a, b, mode, out_dtype, tm, tn, tk, res=None):
    if mode == "nn":
        (m, k), n = a.shape, b.shape[1]
    elif mode == "nt":
        (m, k), n = a.shape, b.shape[0]
    else:
        (k, m), n = a.shape, b.shape[1]
    tm, tn, tk = min(tm, m), min(tn, n), min(tk, k)
    assert m % tm == 0 and n % tn == 0 and k % tk == 0, (name, m, n, k, tm, tn, tk)
    if mode == "nn":
        a_spec = pl.BlockSpec((tm, tk), lambda i, j, kk: (i, kk))
        b_spec = pl.BlockSpec((tk, tn), lambda i, j, kk: (kk, j))
        dims = NN
    elif mode == "nt":
        a_spec = pl.BlockSpec((tm, tk), lambda i, j, kk: (i, kk))
        b_spec = pl.BlockSpec((tn, tk), lambda i, j, kk: (j, kk))
        dims = NT
    else:
        a_spec = pl.BlockSpec((tk, tm), lambda i, j, kk: (kk, i))
        b_spec = pl.BlockSpec((tk, tn), lambda i, j, kk: (kk, j))
        dims = TN
    nk = k // tk
    has_res = res is not None

    def body(*refs):
        a_ref, b_ref = refs[0], refs[1]
        r_ref = refs[2] if has_res else None
        o_ref = refs[3] if has_res else refs[2]
        part = _dot(a_ref[...], b_ref[...], dims)

        def finish(val):
            if has_res:
                val = val + r_ref[...]
            o_ref[...] = val.astype(o_ref.dtype)

        if nk == 1:
            finish(part)
        else:
            acc_ref = refs[-1]
            kk = pl.program_id(2)

            @pl.when(kk == 0)
            def _():
                acc_ref[...] = part

            @pl.when(kk > 0)
            def _():
                acc_ref[...] += part

            @pl.when(kk == nk - 1)
            def _():
                finish(acc_ref[...])

    in_specs = [a_spec, b_spec]
    args = [a, b]
    if has_res:
        in_specs.append(pl.BlockSpec((tm, tn), lambda i, j, kk: (i, j)))
        args.append(res)
    return pl.pallas_call(
        body, name=name, grid=(m // tm, n // tn, nk), in_specs=in_specs,
        out_specs=pl.BlockSpec((tm, tn), lambda i, j, kk: (i, j)),
        out_shape=jax.ShapeDtypeStruct((m, n), out_dtype),
        scratch_shapes=[pltpu.VMEM((tm, tn), F32)] if nk > 1 else [],
        compiler_params=_cparams(("arbitrary", "arbitrary", "arbitrary")),
    )(*args)


def _rms_fwd(name, x, gain, tm=256):
    s = x.shape[0]

    def fn(xb, g):
        r = lax.rsqrt(jnp.mean(xb * xb, axis=-1, keepdims=True) + EPS)
        return (xb * r * g,), ()

    return _rows(name, fn, s, tm, [_rb(x, tm, D), _cst(gain)], [(D, BF16)])[0]


def _rms_bwd(name, x, gain, dns, dres, tm=256):
    s = x.shape[0]
    n_dn = len(dns)

    def fn(xb, g, *rest):
        dn = rest[0]
        for t in rest[1:n_dn]:
            dn = dn + t
        r = lax.rsqrt(jnp.mean(xb * xb, axis=-1, keepdims=True) + EPS)
        xhat = xb * r
        dxh = dn * g
        dx = r * (dxh - xhat * jnp.mean(dxh * xhat, axis=-1, keepdims=True)) + rest[n_dn]
        return (dx,), (jnp.sum(dn * xhat, axis=0, keepdims=True),)

    ins = [_rb(x, tm, D), _cst(gain)] + [_rb(t, tm, D) for t in dns] + [_rb(dres, tm, D)]
    return _rows(name, fn, s, tm, ins, [(D, F32)], [(1, D)])


def _final(h2, target, gain, tm=256):
    s = h2.shape[0]

    def fn(hb, tb, g):
        r = lax.rsqrt(jnp.mean(hb * hb, axis=-1, keepdims=True) + EPS)
        xhat = hb * r
        e = xhat * g - tb
        dy = e * (1.0 / D)
        dxh = dy * g
        dh = r * (dxh - xhat * jnp.mean(dxh * xhat, axis=-1, keepdims=True))
        lrow = 0.5 * jnp.sum(jnp.sum(e * e, axis=-1, keepdims=True) * (1.0 / D), axis=0, keepdims=True)
        return (dh,), (jnp.sum(dy * xhat, axis=0, keepdims=True), jnp.broadcast_to(lrow, (1, LANES)))

    return _rows("final_norm_loss", fn, s, tm, [_rb(h2, tm, D), _rb(target, tm, D), _cst(gain)],
                 [(D, F32)], [(1, D), (1, LANES)])


def _merge_fwd(pa, pb, proj, tm=256):
    s = pa.shape[0]

    def fn(a, b, ga, gb):
        return (_sigmoid(ga) * a + _sigmoid(gb) * b,), ()

    ins = [_rb(pa, tm, D), _rb(pb, tm, D), _rb(proj, tm, D, 7), _rb(proj, tm, D, 8)]
    return _rows("merge_fwd", fn, s, tm, ins, [(D, BF16)])[0]


def _merge_bwd(dmerged, pa, pb, proj, tm=256):
    s = pa.shape[0]

    def fn(dm, a, b, ga, gb):
        sa, sb = _sigmoid(ga), _sigmoid(gb)
        return (dm * sa, dm * sb, dm * a * sa * (1.0 - sa), dm * b * sb * (1.0 - sb)), ()

    ins = [_rb(dmerged, tm, D), _rb(pa, tm, D), _rb(pb, tm, D), _rb(proj, tm, D, 7), _rb(proj, tm, D, 8)]
    return _rows("merge_bwd", fn, s, tm, ins, [(D, BF16), (D, BF16), (D, BF16), (D, BF16)])


def _gelu_parts(x):
    cdf = 0.5 * (1.0 + lax.erf(x * 0.7071067811865476))
    pdf = 0.3989422804014327 * jnp.exp(-0.5 * x * x)
    return x * cdf, cdf + x * pdf


def _conv_taps(u_ext, n_out, first):
    n = u_ext.shape[0]
    cur = u_ext[8:8 + n_out]
    m1 = pltpu.roll(u_ext, 1, 0)[8:8 + n_out]
    m2 = pltpu.roll(u_ext, 2, 0)[8:8 + n_out]
    return m2, m1, cur


def _convglu_fwd(u, conv_w8, conv_b, tm=64):
    s, w = u.shape
    tb = tm // 8

    def fn(ub, up, cw, cb):
        i = pl.program_id(0)
        up = jnp.where(i == 0, 0.0, up)
        m2, m1, cur = _conv_taps(jnp.concatenate([up, ub], axis=0), tm, None)
        acc = cb + cw[0:1] * m2 + cw[1:2] * m1 + cw[2:3] * cur
        act, _ = _gelu_parts(acc[:, :D_FF])
        return (act * acc[:, D_FF:],), ()

    ins = [_rb(u, tm, w), (u, (8, w), lambda i: (jnp.maximum(i * tb - 1, 0), 0)), _cst(conv_w8), _cst(conv_b)]
    return _rows("convglu_fwd", fn, s, tm, ins, [(D_FF, BF16)])[0]


def _convglu_bwd(u, dact, conv_w8, conv_b, tm=64):
    s, w = u.shape
    tb = tm // 8
    nb = s // tm

    def fn(ub, up, un, db, dn, cw, cb):
        i = pl.program_id(0)
        up = jnp.where(i == 0, 0.0, up)
        dn = jnp.where(i == nb - 1, 0.0, dn)
        ne = tm + 8
        m2, m1, cur = _conv_taps(jnp.concatenate([up, ub, un], axis=0), ne, None)
        acc = cb + cw[0:1] * m2 + cw[1:2] * m1 + cw[2:3] * cur
        de = jnp.concatenate([db, dn], axis=0)
        gl, dgl = _gelu_parts(acc[:, :D_FF])
        dacc = jnp.concatenate([de * acc[:, D_FF:] * dgl, de * gl], axis=1)
        p1 = pltpu.roll(dacc, ne - 1, 0)[:tm]
        p2 = pltpu.roll(dacc, ne - 2, 0)[:tm]
        d0 = dacc[:tm]
        du = cw[2:3] * d0 + cw[1:2] * p1 + cw[0:1] * p2
        zero5 = jnp.zeros((5, w), F32)
        dcw = jnp.concatenate([
            jnp.sum(d0 * m2[:tm], axis=0, keepdims=True), jnp.sum(d0 * m1[:tm], axis=0, keepdims=True),
            jnp.sum(d0 * cur[:tm], axis=0, keepdims=True), zero5], axis=0)
        return (du,), (dcw, jnp.sum(d0, axis=0, keepdims=True))

    ins = [
        _rb(u, tm, w),
        (u, (8, w), lambda i: (jnp.maximum(i * tb - 1, 0), 0)),
        (u, (8, w), lambda i: (jnp.minimum((i + 1) * tb, s // 8 - 1), 0)),
        _rb(dact, tm, D_FF),
        (dact, (8, D_FF), lambda i: (jnp.minimum((i + 1) * tb, s // 8 - 1), 0)),
        _cst(conv_w8), _cst(conv_b),
    ]
    return _rows("convglu_bwd", fn, s, tm, ins, [(w, BF16)], [(8, w), (1, w)])


def _chunk_scan(x, t_iota, reverse):
    k = 1
    while k < HG_C:
        if reverse:
            x = x + jnp.where(t_iota < HG_C - k, pltpu.roll(x, HG_C - k, 0), 0.0)
        else:
            x = x + jnp.where(t_iota >= k, pltpu.roll(x, k, 0), 0.0)
        k *= 2
    return x


def _hg_gates(hq, hf, lb):
    sq = _sigmoid(hq)
    q = hq * sq
    sg = _sigmoid(hf)
    f = lb + (1.0 - lb) * sg
    return q, sq, sg, f, 1.0 - f, jnp.log(f)


def _lb_of(logits):
    l0, l1 = logits[0:1], logits[1:2]
    mx = jnp.maximum(l0, l1)
    e0, e1 = jnp.exp(l0 - mx), jnp.exp(l1 - mx)
    return e0 / (e0 + e1)


def _tri(n, lower):
    r = lax.broadcasted_iota(jnp.int32, (n, n), 0)
    c = lax.broadcasted_iota(jnp.int32, (n, n), 1)
    return jnp.where((r >= c) if lower else (r <= c), 1.0, 0.0).astype(BF16)


def _hg_intra_terms(q, kk, b, t_iota):
    ws, ps = [], []
    for s in range(HG_C):
        p = jnp.where(t_iota >= s, jnp.exp(jnp.minimum(b - b[s:s + 1], 0.0)), 0.0)
        ps.append(p)
        ws.append(q * kk[s:s + 1] * p)
    return jnp.concatenate(ws, axis=0), ps


def _hg_fwd(proj, lb_logits):
    s = proj.shape[0]
    nt = s // HG_T
    nc = HG_T // HG_C

    def body(q_ref, f_ref, i_ref, l_ref, o_ref, st_ref, state):
        @pl.when(pl.program_id(1) == 0)
        def _():
            state[...] = jnp.zeros_like(state)

        st_ref[0, 0] = state[...]
        lb = _lb_of(l_ref[...])
        ones = jnp.ones((HG_DK, HG_DK), BF16)
        t_iota = lax.broadcasted_iota(jnp.int32, (HG_C, HG_DK), 0)
        cc = HG_C * HG_C

        def group(gi, st):
            units = []
            for u in range(HG_UNROLL):
                r = pl.ds(pl.multiple_of((gi * HG_UNROLL + u) * HG_C, HG_C), HG_C)
                q, _, _, _, kk, g = _hg_gates(q_ref[r, :], f_ref[r, :], lb)
                b = _chunk_scan(g, t_iota, False)
                b_end = b[HG_C - 1:HG_C]
                w_all, _ = _hg_intra_terms(q, kk, b, t_iota)
                units.append((r, i_ref[r, :], q * jnp.exp(b), jnp.exp(b_end), kk * jnp.exp(b_end - b), w_all))
            a_all = _dot(jnp.concatenate([un[5] for un in units], axis=0), ones, NN)
            kvs = [_dot(v, kd, TN) for (_, v, _, _, kd, _) in units]
            sts = [st]
            for (_, _, _, dec, _, _), kv in zip(units, kvs):
                sts.append(sts[-1] * dec + kv)
            for ui, (r, v, qd, _, _, _) in enumerate(units):
                o = _dot(qd, sts[ui], NT)
                for si in range(HG_C):
                    o = o + a_all[ui * cc + si * HG_C:ui * cc + (si + 1) * HG_C] * v[si:si + 1]
                o_ref[r, :] = o
            return sts[-1]

        state[...] = lax.fori_loop(0, nc // HG_UNROLL, group, state[...])

    col = lambda off: pl.BlockSpec((HG_T, HG_DK), lambda h, t: (t, off + h))
    return pl.pallas_call(
        body, name="hgrn2_fwd", grid=(HG_H, nt),
        in_specs=[col(0), col(8), col(16), pl.BlockSpec((2, HG_DK), lambda h, t: (0, h))],
        out_specs=[pl.BlockSpec((HG_T, HG_DK), lambda h, t: (t, h)),
                   pl.BlockSpec((1, 1, HG_DK, HG_DK), lambda h, t: (h, t, 0, 0))],
        out_shape=[jax.ShapeDtypeStruct((s, D), F32), jax.ShapeDtypeStruct((HG_H, nt, HG_DK, HG_DK), F32)],
        scratch_shapes=[pltpu.VMEM((HG_DK, HG_DK), F32)],
        compiler_params=_cparams(("arbitrary", "arbitrary")),
    )(proj, proj, proj, lb_logits)


def _hg_bwd(proj, lb_logits, states, do_raw):
    s = proj.shape[0]
    nt = s // HG_T
    nc = HG_T // HG_C

    def body(q_ref, f_ref, i_ref, l_ref, st_ref, do_ref, dq_ref, df_ref, di_ref, dl_ref, st_all, adj):
        tb = pl.program_id(1)

        @pl.when(tb == 0)
        def _():
            adj[...] = jnp.zeros_like(adj)
            dl_ref[...] = jnp.zeros_like(dl_ref)

        lb = _lb_of(l_ref[...])
        ones = jnp.ones((HG_DK, HG_DK), BF16)
        t_iota = lax.broadcasted_iota(jnp.int32, (HG_C, HG_DK), 0)
        cc = HG_C * HG_C

        def fwd_group(gi, st):
            terms = []
            for u in range(HG_UNROLL):
                ci = gi * HG_UNROLL + u
                r = pl.ds(pl.multiple_of(ci * HG_C, HG_C), HG_C)
                _, _, _, _, kk, g = _hg_gates(q_ref[r, :], f_ref[r, :], lb)
                b = _chunk_scan(g, t_iota, False)
                b_end = b[HG_C - 1:HG_C]
                terms.append((ci, jnp.exp(b_end), _dot(i_ref[r, :], kk * jnp.exp(b_end - b), TN)))
            for ci, dec, kv in terms:
                st_all[ci] = st
                st = st * dec + kv
            return st

        lax.fori_loop(0, nc // HG_UNROLL, fwd_group, st_ref[0, 0])

        def bwd_group(gj, dlb):
            units = []
            for u in range(HG_UNROLL_BWD):
                ci = nc - 1 - (gj * HG_UNROLL_BWD + u)
                r = pl.ds(pl.multiple_of(ci * HG_C, HG_C), HG_C)
                hq, hf, v, do = q_ref[r, :], f_ref[r, :], i_ref[r, :], do_ref[r, :]
                q, sq, sg, f, kk, g = _hg_gates(hq, hf, lb)
                b = _chunk_scan(g, t_iota, False)
                b_end = b[HG_C - 1:HG_C]
                e_b, e_be, dec = jnp.exp(b), jnp.exp(b_end - b), jnp.exp(b_end)
                w_all, ps = _hg_intra_terms(q, kk, b, t_iota)
                x_all = jnp.concatenate([do * v[si:si + 1] for si in range(HG_C)], axis=0)
                units.append(dict(ci=ci, r=r, hq=hq, v=v, do=do, q=q, sq=sq, sg=sg, f=f, kk=kk, e_b=e_b, e_be=e_be,
                                  dec=dec, kd=kk * e_be, w=w_all, ps=ps, x=x_all))
            both = _dot(jnp.concatenate([un["w"] for un in units] + [un["x"] for un in units], axis=0), ones, NN)
            st0s = [st_all[un["ci"]] for un in units]
            st_ends = [st0 * un["dec"] + _dot(un["v"], un["kd"], TN) for un, st0 in zip(units, st0s)]
            dqks = [_dot(un["do"], un["q"] * un["e_b"], TN) for un in units]
            es = [adj[...]]
            for un, dqk in zip(units, dqks):
                es.append(es[-1] * un["dec"] + dqk)
            adj[...] = es[-1]
            for ui, un in enumerate(units):
                e, q, kk, v, do = es[ui], un["q"], un["kk"], un["v"], un["do"]
                tail = jnp.sum(e * st_ends[ui], axis=0, keepdims=True)
                dq = un["e_b"] * _dot(do, st0s[ui], NN)
                dk = un["e_be"] * _dot(v, e, NN)
                dv = _dot(un["kd"], e, NT)
                a0 = ui * cc
                d0 = (HG_UNROLL_BWD + ui) * cc
                for si in range(HG_C):
                    da = both[d0 + si * HG_C:d0 + (si + 1) * HG_C]
                    aa = both[a0 + si * HG_C:a0 + (si + 1) * HG_C]
                    dap = da * un["ps"][si]
                    dq = dq + dap * kk[si:si + 1]
                    hit = t_iota == si
                    dk = dk + jnp.where(hit, jnp.sum(dap * q, axis=0, keepdims=True), 0.0)
                    dv = dv + jnp.where(hit, jnp.sum(aa * do, axis=0, keepdims=True), 0.0)
                dg = _chunk_scan(q * dq - kk * dk, t_iota, True) + tail
                dfg = dg / un["f"] - dk
                sq, sg, hq, r = un["sq"], un["sg"], un["hq"], un["r"]
                dq_ref[r, :] = (dq * sq * (1.0 + hq * (1.0 - sq))).astype(dq_ref.dtype)
                df_ref[r, :] = (dfg * (1.0 - lb) * sg * (1.0 - sg)).astype(df_ref.dtype)
                di_ref[r, :] = dv.astype(di_ref.dtype)
                dlb = dlb + jnp.sum(dfg * (1.0 - sg), axis=0, keepdims=True)
            return dlb

        dlb = lax.fori_loop(0, nc // HG_UNROLL_BWD, bwd_group, jnp.zeros((1, HG_DK), F32))
        dl0 = dlb * lb * (1.0 - lb)
        dl_ref[...] += jnp.concatenate([dl0, -dl0], axis=0)

    col = lambda off: pl.BlockSpec((HG_T, HG_DK), lambda h, t: (nt - 1 - t, off + h))
    out_col = pl.BlockSpec((HG_T, HG_DK), lambda h, t: (nt - 1 - t, h))
    return pl.pallas_call(
        body, name="hgrn2_bwd", grid=(HG_H, nt),
        in_specs=[col(0), col(8), col(16), pl.BlockSpec((2, HG_DK), lambda h, t: (0, h)),
                  pl.BlockSpec((1, 1, HG_DK, HG_DK), lambda h, t: (h, nt - 1 - t, 0, 0)), col(0)],
        out_specs=[out_col, out_col, out_col, pl.BlockSpec((2, HG_DK), lambda h, t: (0, h))],
        out_shape=[jax.ShapeDtypeStruct((s, D), BF16)] * 3 + [jax.ShapeDtypeStruct((2, D), F32)],
        scratch_shapes=[pltpu.VMEM((nc, HG_DK, HG_DK), F32), pltpu.VMEM((HG_DK, HG_DK), F32)],
        compiler_params=_cparams(("arbitrary", "arbitrary")),
    )(proj, proj, proj, lb_logits, states, do_raw)


def _hg_post_fwd(o_raw, proj, gnorm, tm=256):
    s = o_raw.shape[0]

    def fn(o, hg, gn):
        outs = []
        for h in range(HG_H):
            sl = slice(h * HG_DK, (h + 1) * HG_DK)
            oh, gh = o[:, sl], hg[:, sl]
            r = lax.rsqrt(jnp.mean(oh * oh, axis=-1, keepdims=True) + EPS)
            outs.append(oh * r * gn * (gh * _sigmoid(gh)))
        return (jnp.concatenate(outs, axis=1),), ()

    return _rows("hgrn2_out_fwd", fn, s, tm, [_rb(o_raw, tm, D), _rb(proj, tm, D, 3), _cst(gnorm)], [(D, BF16)])[0]


def _hg_post_bwd(do_a, o_raw, proj, gnorm, tm=256):
    s = o_raw.shape[0]

    def fn(da, o, hg, gn):
        dos, dhgs = [], []
        dgn = jnp.zeros((1, HG_DK), F32)
        for h in range(HG_H):
            sl = slice(h * HG_DK, (h + 1) * HG_DK)
            oh, gh, dh = o[:, sl], hg[:, sl], da[:, sl]
            r = lax.rsqrt(jnp.mean(oh * oh, axis=-1, keepdims=True) + EPS)
            xhat = oh * r
            sg = _sigmoid(gh)
            dy = dh * (gh * sg)
            dhgs.append(dh * xhat * gn * sg * (1.0 + gh * (1.0 - sg)))
            dgn = dgn + jnp.sum(dy * xhat, axis=0, keepdims=True)
            dxh = dy * gn
            dos.append(r * (dxh - xhat * jnp.mean(dxh * xhat, axis=-1, keepdims=True)))
        return (jnp.concatenate(dos, axis=1), jnp.concatenate(dhgs, axis=1)), (dgn,)

    ins = [_rb(do_a, tm, D), _rb(o_raw, tm, D), _rb(proj, tm, D, 3), _cst(gnorm)]
    return _rows("hgrn2_out_bwd", fn, s, tm, ins, [(D, F32), (D, BF16)], [(1, HG_DK)])


def _log_sigmoid(z):
    return jnp.minimum(z, 0.0) - jnp.log(1.0 + jnp.exp(-jnp.abs(z)))


def _fox_prep(pff, bias, tm=256):
    s = pff.shape[0]

    def body(p_ref, b_ref, ct_ref, carry):
        @pl.when(pl.program_id(0) == 0)
        def _():
            carry[...] = jnp.zeros_like(carry)

        lf = _log_sigmoid(p_ref[...] + b_ref[...])
        c = _split_dot(_tri(tm, True), lf, 3) + carry[0:1]
        carry[...] = jnp.broadcast_to(c[tm - 1:tm], carry.shape)
        ct_ref[...] = c.T

    return pl.pallas_call(
        body, name="fox_gate_cumsum", grid=(s // tm,),
        in_specs=[pl.BlockSpec((tm, LANES), lambda i: (i, 0)), pl.BlockSpec((1, LANES), lambda i: (0, 0))],
        out_specs=pl.BlockSpec((LANES, tm), lambda i: (0, i)),
        out_shape=jax.ShapeDtypeStruct((LANES, s), F32),
        scratch_shapes=[pltpu.VMEM((8, LANES), F32)],
        compiler_params=_cparams(("arbitrary",)),
    )(pff, bias)


def _fox_gate_bwd(dct, pff, bias, tm=256):
    s = pff.shape[0]
    nb = s // tm

    def body(d_ref, p_ref, b_ref, dff_ref, db_ref, carry):
        @pl.when(pl.program_id(0) == 0)
        def _():
            carry[...] = jnp.zeros_like(carry)
            db_ref[...] = jnp.zeros_like(db_ref)

        dc = d_ref[...].T
        dlf = _split_dot(_tri(tm, False), dc, 3) + carry[0:1]
        carry[...] = jnp.broadcast_to(dlf[0:1], carry.shape)
        dff = dlf * _sigmoid(-(p_ref[...] + b_ref[...]))
        dff_ref[...] = dff
        db_ref[...] += jnp.sum(dff, axis=0, keepdims=True)

    return pl.pallas_call(
        body, name="fox_gate_bwd", grid=(nb,),
        in_specs=[pl.BlockSpec((LANES, tm), lambda i: (0, nb - 1 - i)),
                  pl.BlockSpec((tm, LANES), lambda i: (nb - 1 - i, 0)), pl.BlockSpec((1, LANES), lambda i: (0, 0))],
        out_specs=[pl.BlockSpec((tm, LANES), lambda i: (nb - 1 - i, 0)), pl.BlockSpec((1, LANES), lambda i: (0, 0))],
        out_shape=[jax.ShapeDtypeStruct((s, LANES), F32), jax.ShapeDtypeStruct((1, LANES), F32)],
        scratch_shapes=[pltpu.VMEM((8, LANES), F32)],
        compiler_params=_cparams(("arbitrary",)),
    )(dct, pff, bias)


def _diag_mask(t):
    r = lax.broadcasted_iota(jnp.int32, (t, t), 0)
    c = lax.broadcasted_iota(jnp.int32, (t, t), 1)
    return r >= c


def _fox_fwd(q, k, v, crow):
    h, s, dh = q.shape
    t = min(FOX_T, s)
    nq = s // t

    def body(q_ref, k_ref, v_ref, c_ref, o_ref, lse_ref):
        i = pl.program_id(1)
        qs = q_ref[0] * 0.125

        def step(j, carry, masked):
            m, l, acc = carry
            rows = pl.ds(pl.multiple_of(j * t, t), t)
            sc = _dot(qs, k_ref[0, rows, :], NT) - c_ref[0, j]
            if masked:
                sc = jnp.where(_diag_mask(t), sc, -1e30)
            m_new = jnp.maximum(m, jnp.max(sc, axis=-1, keepdims=True))
            p = jnp.exp(sc - m_new)
            alpha = jnp.exp(m - m_new)
            l = alpha * l + jnp.sum(p, axis=-1, keepdims=True)
            acc = alpha * acc + _dot(p, v_ref[0, rows, :], NN)
            return m_new, l, acc

        init = (jnp.full((t, 1), -1e30, F32), jnp.zeros((t, 1), F32), jnp.zeros((t, dh), F32))
        carry = lax.fori_loop(0, i, lambda j, c: step(j, c, False), init)
        m, l, acc = step(i, carry, True)
        o_ref[0] = (acc / l).astype(o_ref.dtype)
        lse_ref[0] = m + jnp.log(l)

    whole = pl.BlockSpec((1, s, dh), lambda hh, i: (hh, 0, 0))
    blk = pl.BlockSpec((1, t, dh), lambda hh, i: (hh, i, 0))
    return pl.pallas_call(
        body, name="fox_attn_fwd", grid=(h, nq),
        in_specs=[blk, whole, whole, pl.BlockSpec((1, nq, 1, t), lambda hh, i: (hh, 0, 0, 0))],
        out_specs=[blk, pl.BlockSpec((1, t, 1), lambda hh, i: (hh, i, 0))],
        out_shape=[jax.ShapeDtypeStruct((h, s, dh), BF16), jax.ShapeDtypeStruct((h, s, 1), F32)],
        compiler_params=_cparams(("arbitrary", "arbitrary")),
    )(q, k, v, crow)


def _fox_bwd(q, k, v, crow, o, lse, do):
    h, s, dh = q.shape
    t = min(FOX_T, s)
    nq = s // t

    def body(q_ref, k_ref, v_ref, c_ref, o_ref, lse_ref, do_ref, dq_ref, dk_ref, dv_ref, dc_ref):
        i = pl.program_id(1)

        @pl.when(i == 0)
        def _():
            dk_ref[...] = jnp.zeros_like(dk_ref)
            dv_ref[...] = jnp.zeros_like(dv_ref)
            dc_ref[...] = jnp.zeros_like(dc_ref)

        qs = q_ref[0] * 0.125
        dob = do_ref[0]
        lse_b = lse_ref[0]
        delta = jnp.sum(dob.astype(F32) * o_ref[0].astype(F32), axis=-1, keepdims=True)

        def step(j, carry, masked):
            dq, rsum = carry
            rows = pl.ds(pl.multiple_of(j * t, t), t)
            kj, vj = k_ref[0, rows, :], v_ref[0, rows, :]
            sc = _dot(qs, kj, NT) - c_ref[0, j]
            if masked:
                sc = jnp.where(_diag_mask(t), sc, -1e30)
            p = jnp.exp(sc - lse_b)
            ds = p * (_dot(dob, vj, NT) - delta)
            dk_ref[0, rows, :] += _dot(ds, qs, TN)
            dv_ref[0, rows, :] += _dot(p, dob, TN)
            dc_ref[0, j] += -jnp.sum(ds, axis=0, keepdims=True)
            return dq + _dot(ds, kj, NN), rsum + jnp.sum(ds, axis=1, keepdims=True)

        init = (jnp.zeros((t, dh), F32), jnp.zeros((t, 1), F32))
        dq, rsum = step(i, lax.fori_loop(0, i, lambda j, c: step(j, c, False), init), True)
        dc_ref[0, i] += jnp.transpose(jnp.broadcast_to(rsum, (t, LANES)))[0:1]
        dq_ref[0] = (dq * 0.125).astype(dq_ref.dtype)

    whole = pl.BlockSpec((1, s, dh), lambda hh, i: (hh, 0, 0))
    blk = pl.BlockSpec((1, t, dh), lambda hh, i: (hh, i, 0))
    crow_spec = pl.BlockSpec((1, nq, 1, t), lambda hh, i: (hh, 0, 0, 0))
    return pl.pallas_call(
        body, name="fox_attn_bwd", grid=(h, nq),
        in_specs=[blk, whole, whole, crow_spec, blk, pl.BlockSpec((1, t, 1), lambda hh, i: (hh, i, 0)), blk],
        out_specs=[blk, whole, whole, crow_spec],
        out_shape=[jax.ShapeDtypeStruct((h, s, dh), BF16), jax.ShapeDtypeStruct((h, s, dh), F32),
                   jax.ShapeDtypeStruct((h, s, dh), F32), jax.ShapeDtypeStruct((h, nq, 1, t), F32)],
        compiler_params=_cparams(("arbitrary", "arbitrary")),
    )(q, k, v, crow, o, lse, do)


def _to_heads(x2d):
    s = x2d.shape[0]
    return x2d.astype(BF16).reshape(s, FOX_H, FOX_D).transpose(1, 0, 2)


def _from_heads(x3d):
    return x3d.transpose(1, 0, 2).reshape(x3d.shape[1], FOX_H * FOX_D)


def _adamw(name, w, g, m, v, tm=None):
    rows, width = w.shape
    tm = rows if tm is None else tm
    c1 = 1.0 - ADAM_B1 ** ADAM_STEP
    c2 = 1.0 - ADAM_B2 ** ADAM_STEP

    def fn(wb, gb, mb, vb):
        m_new = ADAM_B1 * mb + (1.0 - ADAM_B1) * gb
        v_new = ADAM_B2 * vb + (1.0 - ADAM_B2) * (gb * gb)
        delta = -ADAM_LR * ((m_new / c1) / (jnp.sqrt(v_new / c2) + ADAM_EPS) + ADAM_WD * wb)
        return (delta, m_new, v_new), ()

    ins = [_rb(a, tm, width) for a in (w, g, m, v)]
    return _rows(name, fn, rows, tm, ins, [(width, F32)] * 3)


def _me():
    return lax.axis_index("x"), lax.axis_index("y"), lax.axis_index("c")


def _all_gather8(name, block):
    m, n = block.shape

    def body(x_ref, out_ref, send_sems, recv_sems, local_sem):
        x, y, c = _me()
        me, sibling = (x, y, c), (x, y, 1 - c)
        chips = [(1 - x, y), (x, 1 - y), (1 - x, 1 - y)]

        def slot(px, py, pc):
            return out_ref.at[4 * px + 2 * py + pc]

        def copy(k, blk, to, src=None):
            return pltpu.make_async_remote_copy(
                src_ref=slot(*blk) if src is None else src, dst_ref=slot(*blk),
                send_sem=send_sems.at[k], recv_sem=recv_sems.at[k], device_id=to, device_id_type=MESH)

        mine = pltpu.make_async_copy(x_ref, slot(*me), local_sem)
        mine.start()
        first = [copy(0, me, sibling, src=x_ref)]
        first += [copy(1 + j, me, (*chip, c), src=x_ref) for j, chip in enumerate(chips)]
        for cp in first:
            cp.start()
        passed = [copy(4 + j, (*chip, c), sibling) for j, chip in enumerate(chips)]
        for j, chip in enumerate(chips):
            copy(1 + j, (*chip, c), me).wait_recv()
            passed[j].start()
        copy(0, sibling, me).wait_recv()
        for j, chip in enumerate(chips):
            copy(4 + j, (*chip, 1 - c), me).wait_recv()
        for cp in first + passed:
            cp.wait_send()
        mine.wait()

    return pl.pallas_call(
        body, name=name, in_specs=[ANY], out_specs=ANY,
        out_shape=jax.ShapeDtypeStruct((8, m, n), block.dtype),
        scratch_shapes=[pltpu.SemaphoreType.DMA((7,)), pltpu.SemaphoreType.DMA((7,)), pltpu.SemaphoreType.DMA],
    )(block)


def _swap_sibling(name, g2, pick_other):
    shape = g2.shape[1:]

    def body(g_ref, out_ref, send_sem, recv_sem):
        x, y, c = _me()
        src = g_ref.at[1 - c] if pick_other else g_ref.at[0]
        cp = pltpu.make_async_remote_copy(src_ref=src, dst_ref=out_ref, send_sem=send_sem, recv_sem=recv_sem,
                                          device_id=(x, y, 1 - c), device_id_type=MESH)
        cp.start()
        cp.wait()

    return pl.pallas_call(
        body, name=name, in_specs=[ANY], out_specs=ANY, out_shape=jax.ShapeDtypeStruct(shape, g2.dtype),
        scratch_shapes=[pltpu.SemaphoreType.DMA, pltpu.SemaphoreType.DMA],
    )(g2)


def _chip_exchange(name, p):
    def body(p_ref, out_ref, send_sems, recv_sems, local_sem):
        x, y, c = _me()
        my_chip = 2 * x + y
        chips = [(1 - x, y), (x, 1 - y), (1 - x, 1 - y)]
        mine = pltpu.make_async_copy(p_ref.at[my_chip], out_ref.at[my_chip], local_sem)
        mine.start()
        sends = []
        for k, (px, py) in enumerate(chips):
            sends.append(pltpu.make_async_remote_copy(
                src_ref=p_ref.at[2 * px + py], dst_ref=out_ref.at[my_chip], send_sem=send_sems.at[k],
                recv_sem=recv_sems.at[k], device_id=(px, py, c), device_id_type=MESH))
        for cp in sends:
            cp.start()
        for k, (px, py) in enumerate(chips):
            pltpu.make_async_remote_copy(
                src_ref=p_ref.at[my_chip], dst_ref=out_ref.at[2 * px + py], send_sem=send_sems.at[k],
                recv_sem=recv_sems.at[k], device_id=(px, py, c), device_id_type=MESH).wait_recv()
        for cp in sends:
            cp.wait_send()
        mine.wait()

    return pl.pallas_call(
        body, name=name, in_specs=[ANY], out_specs=ANY, out_shape=jax.ShapeDtypeStruct(p.shape, p.dtype),
        scratch_shapes=[pltpu.SemaphoreType.DMA((3,)), pltpu.SemaphoreType.DMA((3,)), pltpu.SemaphoreType.DMA],
    )(p)


def _all_reduce_small(name, block):
    r, n = block.shape

    def body(x_ref, sum_ref, gath, send_sems, recv_sems):
        x, y, c = _me()
        me = 4 * x + 2 * y + c
        gath[me] = x_ref[...]
        sends = []
        for k in range(1, 8):
            px = x ^ ((k >> 2) & 1)
            py = y ^ ((k >> 1) & 1)
            pc = c ^ (k & 1)
            sends.append(pltpu.make_async_remote_copy(
                src_ref=x_ref, dst_ref=gath.at[me], send_sem=send_sems.at[k - 1], recv_sem=recv_sems.at[k - 1],
                device_id=(px, py, pc), device_id_type=MESH))
        for cp in sends:
            cp.start()
        for k in range(1, 8):
            peer = me ^ k
            pltpu.make_async_remote_copy(
                src_ref=x_ref, dst_ref=gath.at[peer], send_sem=send_sems.at[k - 1], recv_sem=recv_sems.at[k - 1],
                device_id=(x, y, c), device_id_type=MESH).wait_recv()
        for cp in sends:
            cp.wait_send()
        acc = gath[0]
        for d in range(1, 8):
            acc = acc + gath[d]
        sum_ref[...] = acc

    vm = pl.BlockSpec(memory_space=pltpu.VMEM)
    return pl.pallas_call(
        body, name=name, in_specs=[vm], out_specs=vm, out_shape=jax.ShapeDtypeStruct((r, n), F32),
        scratch_shapes=[pltpu.VMEM((8, r, n), F32), pltpu.SemaphoreType.DMA((7,)), pltpu.SemaphoreType.DMA((7,))],
    )(block)


def _add2(name, a, b, tm):
    rows = a.shape[0]
    return _rows(name, lambda p, q: ((p + q,), ()), rows, tm, [_rb(a, tm, LANES), _rb(b, tm, LANES)], [(LANES, F32)])[0]


def _add4(name, p, tm):
    m = p.shape[1]
    flat = p.reshape(4 * m, LANES)
    nb = m // tm
    ins = [(flat, (tm, LANES), (lambda i, j=j: (j * nb + i, 0))) for j in range(4)]
    return _rows(name, lambda a, b, c, d: ((((a + b) + c) + d,), ()), m, tm, ins, [(LANES, F32)])[0]


SEG_ROWS = (D * W_IN_SHARD // LANES, 256 * D // LANES, 256 * D // LANES, 256 * D // LANES,
            D * W_UP_SHARD // LANES, W_DOWN_SHARD * D // LANES)
GRAD_ROWS = sum(SEG_ROWS)
CONVW_ROWS = 3 * W_UP_SHARD * 2 // LANES
GATHER_ROWS = 41600


def _flat(a):
    return a.reshape(-1, LANES)


def _gather_weights(w_in, w_a, w_b, w_out, w_up, w_down, conv_w):
    c = lax.axis_index("c")
    bits = lax.bitcast_convert_type(conv_w, BF16)
    pieces = [_flat(t.astype(BF16)) for t in (w_in, w_a, w_b, w_out, w_up, w_down)] + [_flat(bits)]
    pad = GATHER_ROWS - GRAD_ROWS - CONVW_ROWS
    shard = jnp.concatenate(pieces + [jnp.zeros((pad, LANES), BF16)], axis=0)
    half = GATHER_ROWS // 2
    mine = lax.dynamic_slice_in_dim(shard, c * half, half, axis=0)
    full = _all_gather8("all_gather_weights", mine).reshape(N_CHIP, GATHER_ROWS, LANES)
    offs = [0]
    for r in SEG_ROWS:
        offs.append(offs[-1] + r)
    seg = lambda i: full[:, offs[i]:offs[i + 1]]
    wi = seg(0).reshape(N_CHIP, D, W_IN_SHARD).transpose(1, 0, 2).reshape(D, N_CHIP * W_IN_SHARD)
    w_main = jnp.concatenate([wi[:, :FF_COL], wi[:, FF_COL + FOX_H:]], axis=1)
    w_ff = jnp.pad(wi[:, FF_COL:FF_COL + FOX_H], ((0, 0), (0, LANES - FOX_H)))
    wa, wb, wo = (seg(i).reshape(D, D) for i in (1, 2, 3))
    wu = seg(4).reshape(N_CHIP, D, W_UP_SHARD).transpose(1, 0, 2).reshape(D, 2 * D_FF)
    wd = seg(5).reshape(D_FF, D)
    cw_bits = full[:, GRAD_ROWS:GRAD_ROWS + CONVW_ROWS].reshape(N_CHIP, 3, W_UP_SHARD, 2)
    cw = lax.bitcast_convert_type(cw_bits, F32).transpose(1, 0, 2).reshape(3, 2 * D_FF)
    return w_main, w_ff, wa, wb, wo, wu, wd, cw


def _reduce_scatter_grads(d_main, d_ff, d_a, d_b, d_o, d_u, d_d):
    c = lax.axis_index("c")
    d_in = jnp.concatenate([d_main[:, :FF_COL], d_ff[:, :FOX_H], d_main[:, FF_COL:]], axis=1)
    per_chip = [
        d_in.reshape(D, N_CHIP, W_IN_SHARD).transpose(1, 0, 2).reshape(N_CHIP, -1, LANES),
        d_a.reshape(N_CHIP, -1, LANES), d_b.reshape(N_CHIP, -1, LANES), d_o.reshape(N_CHIP, -1, LANES),
        d_u.reshape(D, N_CHIP, W_UP_SHARD).transpose(1, 0, 2).reshape(N_CHIP, -1, LANES),
        d_d.reshape(N_CHIP, -1, LANES),
        jnp.zeros((N_CHIP, GATHER_ROWS - GRAD_ROWS, LANES), F32),
    ]
    half = GATHER_ROWS // 2
    g = jnp.concatenate(per_chip, axis=1).reshape(N_CHIP, 2, half, LANES).transpose(1, 0, 2, 3)
    from_sibling = _swap_sibling("grad_swap_halves", g, True)
    mine = lax.dynamic_index_in_dim(g, c, axis=0, keepdims=False)
    tm = half // 5
    chip_sum = _add2("grad_chip_sum", mine.reshape(-1, LANES), from_sibling.reshape(-1, LANES), tm)
    pieces = _chip_exchange("grad_chip_exchange", chip_sum.reshape(N_CHIP, half, LANES))
    mine_half = _add4("grad_sum_chips", pieces, tm)
    other_half = _swap_sibling("grad_share_half", mine_half[None], False)
    lo = jnp.where(c == 0, mine_half, other_half)
    hi = jnp.where(c == 0, other_half, mine_half)
    return jnp.concatenate([lo, hi], axis=0)


def _local_step(x, target, norm_mix, fox_f_bias, hg_lb_logits, hg_norm, norm_ffn, conv_b, norm_final,
                w_main, w_ff, wa, wb, wo, wu, wd, conv_w):
    s = x.shape[0]
    bias = jnp.pad(fox_f_bias, ((0, 0), (0, LANES - FOX_H)))
    conv_w8 = jnp.pad(conv_w, ((0, 5), (0, 0)))
    t = min(FOX_T, s)

    n1 = _rms_fwd("norm_mix_fwd", x, norm_mix)
    proj = _mm("in_proj", n1, w_main, "nn", F32, 1024, 1024, D)
    pff = _mm("in_proj_forget", n1, w_ff, "nn", F32, 1024, LANES, D)
    ct = _fox_prep(pff, bias)
    crow = ct[:FOX_H].reshape(FOX_H, s // t, 1, t)
    qh, kh, vh = (_to_heads(proj[:, (4 + i) * D:(5 + i) * D]) for i in range(3))
    ob_h, lse = _fox_fwd(qh, kh, vh, crow)
    o_b = _from_heads(ob_h)
    o_raw, states = _hg_fwd(proj, hg_lb_logits)
    o_a = _hg_post_fwd(o_raw, proj, hg_norm)
    pa = _mm("branch_a", o_a, wa, "nn", F32, 1024, 1024, D)
    pb = _mm("branch_b", o_b, wb, "nn", F32, 1024, 1024, D)
    merged = _merge_fwd(pa, pb, proj)
    h1 = _mm("out_proj", merged, wo, "nn", F32, 1024, 1024, D, res=x)
    n2 = _rms_fwd("norm_ffn_fwd", h1, norm_ffn)
    u = _mm("ffn_up", n2, wu, "nn", F32, 1024, W_UP_SHARD, D)
    act = _convglu_fwd(u, conv_w8, conv_b)
    h2 = _mm("ffn_down", act, wd, "nn", F32, 512, 1024, D_FF, res=h1)
    (dh2,), (d_norm_final, loss_row) = _final(h2, target, norm_final)

    dact = _mm("ffn_down_dx", dh2, wd, "nt", BF16, 1024, D_FF, D)
    d_wd = _mm("ffn_down_dw", act, dh2, "tn", F32, D_FF // 2, 1024, 512)
    (du,), (d_conv_w8, d_conv_b) = _convglu_bwd(u, dact, conv_w8, conv_b)
    dn2 = _mm("ffn_up_dx", du, wu, "nt", F32, 1024, 1024, W_UP_SHARD)
    d_wu = _mm("ffn_up_dw", n2, du, "tn", F32, 1024, W_UP_SHARD, 512)
    (dh1,), (d_norm_ffn,) = _rms_bwd("norm_ffn_bwd", h1, norm_ffn, [dn2], dh2)

    dmerged = _mm("out_proj_dx", dh1, wo, "nt", F32, 1024, 1024, D)
    d_wo = _mm("out_proj_dw", merged, dh1, "tn", F32, 1024, 1024, 512)
    dpa, dpb, dga, dgb = _merge_bwd(dmerged, pa, pb, proj)
    do_a = _mm("branch_a_dx", dpa, wa, "nt", F32, 1024, 1024, D)
    do_b = _mm("branch_b_dx", dpb, wb, "nt", BF16, 1024, 1024, D)
    d_wa = _mm("branch_a_dw", o_a, dpa, "tn", F32, 1024, 1024, 512)
    d_wb = _mm("branch_b_dw", o_b, dpb, "tn", F32, 1024, 1024, 512)

    (do_raw, dhg), (d_hg_norm,) = _hg_post_bwd(do_a, o_raw, proj, hg_norm)
    dhq, dhf, dhi, d_lb_logits = _hg_bwd(proj, hg_lb_logits, states, do_raw)

    dqh, dkh, dvh, dcrow = _fox_bwd(qh, kh, vh, crow, ob_h, lse, _to_heads(do_b))
    dct = jnp.pad(dcrow.reshape(FOX_H, s), ((0, LANES - FOX_H), (0, 0)))
    dff, d_bias = _fox_gate_bwd(dct, pff, bias)
    dfq, dfk, dfv = _from_heads(dqh), _from_heads(dkh.astype(BF16)), _from_heads(dvh.astype(BF16))

    dproj = jnp.concatenate([dhq, dhf, dhi, dhg, dfq, dfk, dfv, dga, dgb], axis=1)
    dn1_main = _mm("in_proj_dx", dproj, w_main, "nt", F32, 1024, 1024, 1024)
    dn1_ff = _mm("in_proj_forget_dx", dff, w_ff, "nt", F32, 1024, 1024, LANES)
    d_w_main = _mm("in_proj_dw", n1, dproj, "tn", F32, 1024, 1024, 512)
    d_w_ff = _mm("in_proj_forget_dw", n1, dff, "tn", F32, 1024, LANES, 512)
    (dx,), (d_norm_mix,) = _rms_bwd("norm_mix_bwd", x, norm_mix, [dn1_main, dn1_ff], dh1)

    small = dict(norm_mix=d_norm_mix, fox_f_bias=d_bias[:, :FOX_H], hg_lb_logits=d_lb_logits, hg_norm=d_hg_norm,
                 norm_ffn=d_norm_ffn, conv_b=d_conv_b, norm_final=d_norm_final, conv_w=d_conv_w8[:3], loss=loss_row)
    big = (d_w_main, d_w_ff, d_wa, d_wb, d_wo, d_wu, d_wd)
    return dx, small, big


SMALL_KEYS = ("norm_mix", "fox_f_bias", "hg_lb_logits", "hg_norm", "norm_ffn", "conv_b", "norm_final")


def _pack_small(parts):
    rows, layout = [], []
    for key, arr in parts:
        flat = arr.reshape(-1)
        n = flat.shape[0]
        nr = -(-n // LANES)
        rows.append(jnp.pad(flat, (0, nr * LANES - n)).reshape(nr, LANES))
        layout.append((key, arr.shape, n, nr))
    packed = jnp.concatenate(rows, axis=0)
    pad = -packed.shape[0] % 8
    return jnp.pad(packed, ((0, pad), (0, 0))), layout


def _unpack_small(packed, layout):
    out, r0 = {}, 0
    for key, shape, n, nr in layout:
        out[key] = packed[r0:r0 + nr].reshape(-1)[:n].reshape(shape)
        r0 += nr
    return out


def kernel(x, norm_mix, w_in, fox_f_bias, hg_lb_logits, hg_norm, w_branch_a, w_branch_b, w_out, norm_ffn, w_up, conv_w, conv_b, w_down, norm_final, loss_target, m_norm_mix, m_w_in, m_fox_f_bias, m_hg_lb_logits, m_hg_norm, m_w_branch_a, m_w_branch_b, m_w_out, m_norm_ffn, m_w_up, m_conv_w, m_conv_b, m_w_down, m_norm_final, v_norm_mix, v_w_in, v_fox_f_bias, v_hg_lb_logits, v_hg_norm, v_w_branch_a, v_w_branch_b, v_w_out, v_norm_ffn, v_w_up, v_conv_w, v_conv_b, v_w_down, v_norm_final):
    chip = 2 * lax.axis_index("x") + lax.axis_index("y")
    w_main, w_ff, wa, wb, wo, wu, wd, cw = _gather_weights(
        w_in[0], w_branch_a[0], w_branch_b[0], w_out[0], w_up[0], w_down[0], conv_w[0])
    dx, small, big = _local_step(
        x[0], loss_target[0], norm_mix, fox_f_bias, hg_lb_logits, hg_norm, norm_ffn, conv_b,
        norm_final.reshape(1, D), w_main, w_ff, wa, wb, wo, wu, wd, cw)

    packed, layout = _pack_small([(k, small[k]) for k in SMALL_KEYS + ("conv_w", "loss")])
    red = _unpack_small(_all_reduce_small("all_reduce_small", packed), layout)
    loss = red["loss"][0, 0]
    g_conv_w = lax.dynamic_slice_in_dim(red["conv_w"], chip * W_UP_SHARD, W_UP_SHARD, axis=1)

    gflat = _reduce_scatter_grads(*big)
    offs = [0]
    for r in SEG_ROWS:
        offs.append(offs[-1] + r)
    shapes = [(D, W_IN_SHARD), (256, D), (256, D), (256, D), (D, W_UP_SHARD), (W_DOWN_SHARD, D)]
    g_big = [gflat[offs[i]:offs[i + 1]].reshape(shapes[i]) for i in range(6)]

    names = ["norm_mix", "w_in", "fox_f_bias", "hg_lb_logits", "hg_norm", "w_branch_a", "w_branch_b", "w_out",
             "norm_ffn", "w_up", "conv_w", "conv_b", "w_down", "norm_final"]
    weights = dict(norm_mix=norm_mix, w_in=w_in, fox_f_bias=fox_f_bias, hg_lb_logits=hg_lb_logits, hg_norm=hg_norm,
                   w_branch_a=w_branch_a, w_branch_b=w_branch_b, w_out=w_out, norm_ffn=norm_ffn, w_up=w_up,
                   conv_w=conv_w, conv_b=conv_b, w_down=w_down, norm_final=norm_final)
    ms = dict(norm_mix=m_norm_mix, w_in=m_w_in, fox_f_bias=m_fox_f_bias, hg_lb_logits=m_hg_lb_logits,
              hg_norm=m_hg_norm, w_branch_a=m_w_branch_a, w_branch_b=m_w_branch_b, w_out=m_w_out,
              norm_ffn=m_norm_ffn, w_up=m_w_up, conv_w=m_conv_w, conv_b=m_conv_b, w_down=m_w_down,
              norm_final=m_norm_final)
    vs = dict(norm_mix=v_norm_mix, w_in=v_w_in, fox_f_bias=v_fox_f_bias, hg_lb_logits=v_hg_lb_logits,
              hg_norm=v_hg_norm, w_branch_a=v_w_branch_a, w_branch_b=v_w_branch_b, w_out=v_w_out,
              norm_ffn=v_norm_ffn, w_up=v_w_up, conv_w=v_conv_w, conv_b=v_conv_b, w_down=v_w_down,
              norm_final=v_norm_final)

    grads, deltas, new_m, new_v = {}, {}, {}, {}
    big_names = ["w_in", "w_branch_a", "w_branch_b", "w_out", "w_up", "w_down"]
    for name, g2 in zip(big_names, g_big):
        shape = weights[name].shape
        rows = g2.shape[0]
        d_, m_, v_ = _adamw("adamw_" + name, weights[name][0], g2, ms[name][0], vs[name][0], tm=rows // 8)
        grads[name], deltas[name], new_m[name], new_v[name] = (a.reshape(shape) for a in (g2, d_, m_, v_))
    shape = conv_w.shape
    d_, m_, v_ = _adamw("adamw_conv_w", conv_w[0], g_conv_w, m_conv_w[0], v_conv_w[0])
    grads["conv_w"], deltas["conv_w"], new_m["conv_w"], new_v["conv_w"] = (
        a.reshape(shape) for a in (g_conv_w, d_, m_, v_))
    gs = {k: red[k].reshape(weights[k].shape) for k in SMALL_KEYS}
    pw, lay = _pack_small([(k, weights[k]) for k in SMALL_KEYS])
    pg, _ = _pack_small([(k, gs[k]) for k in SMALL_KEYS])
    pm, _ = _pack_small([(k, ms[k]) for k in SMALL_KEYS])
    pv, _ = _pack_small([(k, vs[k]) for k in SMALL_KEYS])
    d_, m_, v_ = (_unpack_small(a, lay) for a in _adamw("adamw_small", pw, pg, pm, pv))
    for k in SMALL_KEYS:
        grads[k], deltas[k], new_m[k], new_v[k] = gs[k], d_[k], m_[k], v_[k]

    return (loss, dx[None], *[grads[n] for n in names], *[deltas[n] for n in names],
            *[new_m[n] for n in names], *[new_v[n] for n in names])
```

```python
import functools

import jax
import jax.numpy as jnp
from jax import lax
from jax.experimental import pallas as pl
from jax.experimental.pallas import tpu as pltpu

F32 = jnp.float32
BF16 = jnp.bfloat16

D = 1024
HG_H, HG_DK = 8, 128
FOX_H, FOX_D = 16, 64
D_FF = 2816
EPS = 1e-6
N_CHIP = 4
LANES = 128
W_IN_SHARD = 2308
W_UP_SHARD = 1408
W_DOWN_SHARD = 704
FF_COL = 7168
ADAM_LR, ADAM_B1, ADAM_B2, ADAM_EPS, ADAM_WD, ADAM_STEP = 0.001, 0.9, 0.999, 1e-08, 0.01, 10

HG_C = 16
HG_T = 256
HG_UNROLL = 4
HG_UNROLL_BWD = 2
FOX_T = 512
VMEM_LIMIT = 56 * 1024 * 1024
MESH = pl.DeviceIdType.MESH
ANY = pl.BlockSpec(memory_space=pl.ANY)


def _cparams(sem):
    return pltpu.CompilerParams(dimension_semantics=sem, vmem_limit_bytes=VMEM_LIMIT)


def _sigmoid(x):
    return 1.0 / (1.0 + jnp.exp(-x))


def _dot(a, b, dims):
    return lax.dot_general(a.astype(BF16), b.astype(BF16), (dims, ((), ())), preferred_element_type=F32)


NN = ((1,), (0,))
NT = ((1,), (1,))
TN = ((0,), (0,))


def _split_dot(tri, x, parts, dims=NN):
    acc = None
    r = x
    for _ in range(parts):
        p = r.astype(BF16)
        t = lax.dot_general(tri, p, (dims, ((), ())), preferred_element_type=F32)
        acc = t if acc is None else acc + t
        r = r - p.astype(F32)
    return acc


def _rb(arr, tm, width, cb=0):
    return (arr, (tm, width), lambda i: (i, cb))


def _cst(arr):
    return (arr, arr.shape, lambda i: (0,) * arr.ndim)


def _rows(name, fn, n_rows, tm, ins, outs, accs=(), reverse=False):
    n_in, n_out, n_acc = len(ins), len(outs), len(accs)
    nb = n_rows // tm

    def body(*refs):
        vals = [r[...] for r in refs[:n_in]]
        o, a = fn(*vals)
        for r, v in zip(refs[n_in:n_in + n_out], o):
            r[...] = v.astype(r.dtype)
        if n_acc:
            acc_refs = refs[n_in + n_out:]

            @pl.when(pl.program_id(0) == 0)
            def _():
                for r in acc_refs:
                    r[...] = jnp.zeros_like(r)

            for r, v in zip(acc_refs, a):
                r[...] += v

    if reverse:
        rowmap = lambda i: (nb - 1 - i, 0)
    else:
        rowmap = lambda i: (i, 0)
    in_specs = [pl.BlockSpec(bs, im) for (_, bs, im) in ins]
    out_specs = [pl.BlockSpec((tm, w), rowmap) for (w, _) in outs]
    out_specs += [pl.BlockSpec((r, w), lambda i: (0, 0)) for (r, w) in accs]
    out_shape = [jax.ShapeDtypeStruct((n_rows, w), dt) for (w, dt) in outs]
    out_shape += [jax.ShapeDtypeStruct((r, w), F32) for (r, w) in accs]
    res = pl.pallas_call(
        body, name=name, grid=(nb,), in_specs=in_specs, out_specs=out_specs, out_shape=out_shape,
        compiler_params=_cparams(("arbitrary",)),
    )(*[a for a, _, _ in ins])
    return (res[:n_out], res[n_out:]) if n_acc else res


def _mm(name, a, b, mode, out_dtype, tm, tn, tk, res=None):
    if mode == "nn":
        (m, k), n = a.shape, b.shape[1]
    elif mode == "nt":
        (m, k), n = a.shape, b.shape[0]
    else:
        (k, m), n = a.shape, b.shape[1]
    tm, tn, tk = min(tm, m), min(tn, n), min(tk, k)
    assert m % tm == 0 and n % tn == 0 and k % tk == 0, (name, m, n, k, tm, tn, tk)
    if mode == "nn":
        a_spec = pl.BlockSpec((tm, tk), lambda i, j, kk: (i, kk))
        b_spec = pl.BlockSpec((tk, tn), lambda i, j, kk: (kk, j))
        dims = NN
    elif mode == "nt":
        a_spec = pl.BlockSpec((tm, tk), lambda i, j, kk: (i, kk))
        b_spec = pl.BlockSpec((tn, tk), lambda i, j, kk: (j, kk))
        dims = NT
    else:
        a_spec = pl.BlockSpec((tk, tm), lambda i, j, kk: (kk, i))
        b_spec = pl.BlockSpec((tk, tn), lambda i, j, kk: (kk, j))
        dims = TN
    nk = k // tk
    has_res = res is not None

    def body(*refs):
        a_ref, b_ref = refs[0], refs[1]
        r_ref = refs[2] if has_res else None
        o_ref = refs[3] if has_res else refs[2]
        part = _dot(a_ref[...], b_ref[...], dims)

        def finish(val):
            if has_res:
                val = val + r_ref[...]
            o_ref[...] = val.astype(o_ref.dtype)

        if nk == 1:
            finish(part)
        else:
            acc_ref = refs[-1]
            kk = pl.program_id(2)

            @pl.when(kk == 0)
            def _():
                acc_ref[...] = part

            @pl.when(kk > 0)
            def _():
                acc_ref[...] += part

            @pl.when(kk == nk - 1)
            def _():
                finish(acc_ref[...])

    in_specs = [a_spec, b_spec]
    args = [a, b]
    if has_res:
        in_specs.append(pl.BlockSpec((tm, tn), lambda i, j, kk: (i, j)))
        args.append(res)
    return pl.pallas_call(
        body, name=name, grid=(m // tm, n // tn, nk), in_specs=in_specs,
        out_specs=pl.BlockSpec((tm, tn), lambda i, j, kk: (i, j)),
        out_shape=jax.ShapeDtypeStruct((m, n), out_dtype),
        scratch_shapes=[pltpu.VMEM((tm, tn), F32)] if nk > 1 else [],
        compiler_params=_cparams(("arbitrary", "arbitrary", "arbitrary")),
    )(*args)


def _mm_sum_nt(name, pieces, w, extra, tm, tn):
    n_p = len(pieces)
    m, k = pieces[0].shape
    n = w.shape[0]
    xa, xb = extra
    ke = xa.shape[1]
    tm, tn = min(tm, m), min(tn, n)

    def body(*refs):
        p_refs, w_ref, xa_ref, xb_ref, o_ref, acc_ref = refs[:n_p], refs[n_p], refs[n_p + 1], refs[n_p + 2], refs[-2], refs[-1]
        kk = pl.program_id(2)

        @pl.when(kk == 0)
        def _():
            acc_ref[...] = jnp.zeros_like(acc_ref)

        for i in range(n_p):
            @pl.when(kk == i)
            def _(i=i):
                acc_ref[...] += _dot(p_refs[i][...], w_ref[...], NT)

        @pl.when(kk == n_p)
        def _():
            o_ref[...] = acc_ref[...] + _dot(xa_ref[...], xb_ref[...], NT)

    in_specs = [pl.BlockSpec((tm, k), lambda i, j, kk: (i, 0)) for _ in range(n_p)]
    in_specs.append(pl.BlockSpec((tn, k), lambda i, j, kk: (j, jnp.minimum(kk, n_p - 1))))
    in_specs += [pl.BlockSpec((tm, ke), lambda i, j, kk: (i, 0)), pl.BlockSpec((tn, ke), lambda i, j, kk: (j, 0))]
    return pl.pallas_call(
        body, name=name, grid=(m // tm, n // tn, n_p + 1), in_specs=in_specs,
        out_specs=pl.BlockSpec((tm, tn), lambda i, j, kk: (i, j)),
        out_shape=jax.ShapeDtypeStruct((m, n), F32),
        scratch_shapes=[pltpu.VMEM((tm, tn), F32)],
        compiler_params=_cparams(("arbitrary", "arbitrary", "arbitrary")),
    )(*pieces, w, xa, xb)


def _rms_fwd(name, x, gain, tm=256):
    s = x.shape[0]

    def fn(xb, g):
        r = lax.rsqrt(jnp.mean(xb * xb, axis=-1, keepdims=True) + EPS)
        return (xb * r * g,), ()

    return _rows(name, fn, s, tm, [_rb(x, tm, D), _cst(gain)], [(D, BF16)])[0]


def _rms_bwd(name, x, gain, dns, dres, tm=256):
    s = x.shape[0]
    n_dn = len(dns)

    def fn(xb, g, *rest):
        dn = rest[0]
        for t in rest[1:n_dn]:
            dn = dn + t
        r = lax.rsqrt(jnp.mean(xb * xb, axis=-1, keepdims=True) + EPS)
        xhat = xb * r
        dxh = dn * g
        dx = r * (dxh - xhat * jnp.mean(dxh * xhat, axis=-1, keepdims=True)) + rest[n_dn]
        return (dx,), (jnp.sum(dn * xhat, axis=0, keepdims=True),)

    ins = [_rb(x, tm, D), _cst(gain)] + [_rb(t, tm, D) for t in dns] + [_rb(dres, tm, D)]
    return _rows(name, fn, s, tm, ins, [(D, F32)], [(1, D)])


def _final(h2, target, gain, tm=256):
    s = h2.shape[0]

    def fn(hb, tb, g):
        r = lax.rsqrt(jnp.mean(hb * hb, axis=-1, keepdims=True) + EPS)
        xhat = hb * r
        e = xhat * g - tb
        dy = e * (1.0 / D)
        dxh = dy * g
        dh = r * (dxh - xhat * jnp.mean(dxh * xhat, axis=-1, keepdims=True))
        lrow = 0.5 * jnp.sum(jnp.sum(e * e, axis=-1, keepdims=True) * (1.0 / D), axis=0, keepdims=True)
        return (dh,), (jnp.sum(dy * xhat, axis=0, keepdims=True), jnp.broadcast_to(lrow, (1, LANES)))

    return _rows("final_norm_loss", fn, s, tm, [_rb(h2, tm, D), _rb(target, tm, D), _cst(gain)],
                 [(D, F32)], [(1, D), (1, LANES)])


def _merge_fwd(pa, pb, proj, tm=256):
    s = pa.shape[0]

    def fn(a, b, ga, gb):
        return (_sigmoid(ga) * a + _sigmoid(gb) * b,), ()

    ins = [_rb(pa, tm, D), _rb(pb, tm, D), _rb(proj, tm, D, 7), _rb(proj, tm, D, 8)]
    return _rows("merge_fwd", fn, s, tm, ins, [(D, BF16)])[0]


def _merge_bwd(dmerged, pa, pb, proj, tm=256):
    s = pa.shape[0]

    def fn(dm, a, b, ga, gb):
        sa, sb = _sigmoid(ga), _sigmoid(gb)
        return (dm * sa, dm * sb, dm * a * sa * (1.0 - sa), dm * b * sb * (1.0 - sb)), ()

    ins = [_rb(dmerged, tm, D), _rb(pa, tm, D), _rb(pb, tm, D), _rb(proj, tm, D, 7), _rb(proj, tm, D, 8)]
    return _rows("merge_bwd", fn, s, tm, ins, [(D, BF16), (D, BF16), (D, BF16), (D, BF16)])


def _gelu_parts(x):
    cdf = 0.5 * (1.0 + lax.erf(x * 0.7071067811865476))
    pdf = 0.3989422804014327 * jnp.exp(-0.5 * x * x)
    return x * cdf, cdf + x * pdf


def _conv_taps(u_ext, n_out, first):
    n = u_ext.shape[0]
    cur = u_ext[8:8 + n_out]
    m1 = pltpu.roll(u_ext, 1, 0)[8:8 + n_out]
    m2 = pltpu.roll(u_ext, 2, 0)[8:8 + n_out]
    return m2, m1, cur


def _convglu_fwd(u, conv_w8, conv_b, tm=64):
    s, w = u.shape
    tb = tm // 8

    def fn(ub, up, cw, cb):
        i = pl.program_id(0)
        up = jnp.where(i == 0, 0.0, up)
        m2, m1, cur = _conv_taps(jnp.concatenate([up, ub], axis=0), tm, None)
        acc = cb + cw[0:1] * m2 + cw[1:2] * m1 + cw[2:3] * cur
        act, _ = _gelu_parts(acc[:, :D_FF])
        return (act * acc[:, D_FF:],), ()

    ins = [_rb(u, tm, w), (u, (8, w), lambda i: (jnp.maximum(i * tb - 1, 0), 0)), _cst(conv_w8), _cst(conv_b)]
    return _rows("convglu_fwd", fn, s, tm, ins, [(D_FF, BF16)])[0]


def _convglu_bwd(u, dact, conv_w8, conv_b, tm=64):
    s, w = u.shape
    tb = tm // 8
    nb = s // tm

    def fn(ub, up, un, db, dn, cw, cb):
        i = pl.program_id(0)
        up = jnp.where(i == 0, 0.0, up)
        dn = jnp.where(i == nb - 1, 0.0, dn)
        ne = tm + 8
        m2, m1, cur = _conv_taps(jnp.concatenate([up, ub, un], axis=0), ne, None)
        acc = cb + cw[0:1] * m2 + cw[1:2] * m1 + cw[2:3] * cur
        de = jnp.concatenate([db, dn], axis=0)
        gl, dgl = _gelu_parts(acc[:, :D_FF])
        dacc = jnp.concatenate([de * acc[:, D_FF:] * dgl, de * gl], axis=1)
        p1 = pltpu.roll(dacc, ne - 1, 0)[:tm]
        p2 = pltpu.roll(dacc, ne - 2, 0)[:tm]
        d0 = dacc[:tm]
        du = cw[2:3] * d0 + cw[1:2] * p1 + cw[0:1] * p2
        zero5 = jnp.zeros((5, w), F32)
        dcw = jnp.concatenate([
            jnp.sum(d0 * m2[:tm], axis=0, keepdims=True), jnp.sum(d0 * m1[:tm], axis=0, keepdims=True),
            jnp.sum(d0 * cur[:tm], axis=0, keepdims=True), zero5], axis=0)
        return (du,), (dcw, jnp.sum(d0, axis=0, keepdims=True))

    ins = [
        _rb(u, tm, w),
        (u, (8, w), lambda i: (jnp.maximum(i * tb - 1, 0), 0)),
        (u, (8, w), lambda i: (jnp.minimum((i + 1) * tb, s // 8 - 1), 0)),
        _rb(dact, tm, D_FF),
        (dact, (8, D_FF), lambda i: (jnp.minimum((i + 1) * tb, s // 8 - 1), 0)),
        _cst(conv_w8), _cst(conv_b),
    ]
    return _rows("convglu_bwd", fn, s, tm, ins, [(w, BF16)], [(8, w), (1, w)])


def _chunk_scan(x, t_iota, reverse):
    k = 1
    while k < HG_C:
        if reverse:
            x = x + jnp.where(t_iota < HG_C - k, pltpu.roll(x, HG_C - k, 0), 0.0)
        else:
            x = x + jnp.where(t_iota >= k, pltpu.roll(x, k, 0), 0.0)
        k *= 2
    return x


def _hg_gates(hq, hf, lb):
    sq = _sigmoid(hq)
    q = hq * sq
    sg = _sigmoid(hf)
    f = lb + (1.0 - lb) * sg
    return q, sq, sg, f, 1.0 - f, jnp.log(f)


def _lb_of(logits):
    l0, l1 = logits[0:1], logits[1:2]
    mx = jnp.maximum(l0, l1)
    e0, e1 = jnp.exp(l0 - mx), jnp.exp(l1 - mx)
    return e0 / (e0 + e1)


def _tri(n, lower):
    r = lax.broadcasted_iota(jnp.int32, (n, n), 0)
    c = lax.broadcasted_iota(jnp.int32, (n, n), 1)
    return jnp.where((r >= c) if lower else (r <= c), 1.0, 0.0).astype(BF16)


def _hg_intra_terms(q, kk, b, t_iota):
    ws, ps = [], []
    for s in range(HG_C):
        p = jnp.where(t_iota >= s, jnp.exp(jnp.minimum(b - b[s:s + 1], 0.0)), 0.0)
        ps.append(p)
        ws.append(q * kk[s:s + 1] * p)
    return jnp.concatenate(ws, axis=0), ps


def _hg_fwd(proj, lb_logits):
    s = proj.shape[0]
    nt = s // HG_T
    nc = HG_T // HG_C

    def body(q_ref, f_ref, i_ref, l_ref, o_ref, st_ref, state):
        @pl.when(pl.program_id(1) == 0)
        def _():
            state[...] = jnp.zeros_like(state)

        st_ref[0, 0] = state[...]
        lb = _lb_of(l_ref[...])
        ones = jnp.ones((HG_DK, HG_DK), BF16)
        t_iota = lax.broadcasted_iota(jnp.int32, (HG_C, HG_DK), 0)
        cc = HG_C * HG_C

        def group(gi, st):
            units = []
            for u in range(HG_UNROLL):
                r = pl.ds(pl.multiple_of((gi * HG_UNROLL + u) * HG_C, HG_C), HG_C)
                q, _, _, _, kk, g = _hg_gates(q_ref[r, :], f_ref[r, :], lb)
                b = _chunk_scan(g, t_iota, False)
                b_end = b[HG_C - 1:HG_C]
                w_all, _ = _hg_intra_terms(q, kk, b, t_iota)
                units.append((r, i_ref[r, :], q * jnp.exp(b), jnp.exp(b_end), kk * jnp.exp(b_end - b), w_all))
            a_all = _dot(jnp.concatenate([un[5] for un in units], axis=0), ones, NN)
            kvs = [_dot(v, kd, TN) for (_, v, _, _, kd, _) in units]
            sts = [st]
            for (_, _, _, dec, _, _), kv in zip(units, kvs):
                sts.append(sts[-1] * dec + kv)
            for ui, (r, v, qd, _, _, _) in enumerate(units):
                o = _dot(qd, sts[ui], NT)
                for si in range(HG_C):
                    o = o + a_all[ui * cc + si * HG_C:ui * cc + (si + 1) * HG_C] * v[si:si + 1]
                o_ref[r, :] = o
            return sts[-1]

        state[...] = lax.fori_loop(0, nc // HG_UNROLL, group, state[...])

    col = lambda off: pl.BlockSpec((HG_T, HG_DK), lambda h, t: (t, off + h))
    return pl.pallas_call(
        body, name="hgrn2_fwd", grid=(HG_H, nt),
        in_specs=[col(0), col(8), col(16), pl.BlockSpec((2, HG_DK), lambda h, t: (0, h))],
        out_specs=[pl.BlockSpec((HG_T, HG_DK), lambda h, t: (t, h)),
                   pl.BlockSpec((1, 1, HG_DK, HG_DK), lambda h, t: (h, t, 0, 0))],
        out_shape=[jax.ShapeDtypeStruct((s, D), F32), jax.ShapeDtypeStruct((HG_H, nt, HG_DK, HG_DK), F32)],
        scratch_shapes=[pltpu.VMEM((HG_DK, HG_DK), F32)],
        compiler_params=_cparams(("arbitrary", "arbitrary")),
    )(proj, proj, proj, lb_logits)


def _hg_bwd(proj, lb_logits, states, do_raw):
    s = proj.shape[0]
    nt = s // HG_T
    nc = HG_T // HG_C

    def body(q_ref, f_ref, i_ref, l_ref, st_ref, do_ref, dq_ref, df_ref, di_ref, dl_ref, st_all, adj):
        tb = pl.program_id(1)

        @pl.when(tb == 0)
        def _():
            adj[...] = jnp.zeros_like(adj)
            dl_ref[...] = jnp.zeros_like(dl_ref)

        lb = _lb_of(l_ref[...])
        ones = jnp.ones((HG_DK, HG_DK), BF16)
        t_iota = lax.broadcasted_iota(jnp.int32, (HG_C, HG_DK), 0)
        cc = HG_C * HG_C

        def fwd_group(gi, st):
            terms = []
            for u in range(HG_UNROLL):
                ci = gi * HG_UNROLL + u
                r = pl.ds(pl.multiple_of(ci * HG_C, HG_C), HG_C)
                _, _, _, _, kk, g = _hg_gates(q_ref[r, :], f_ref[r, :], lb)
                b = _chunk_scan(g, t_iota, False)
                b_end = b[HG_C - 1:HG_C]
                terms.append((ci, jnp.exp(b_end), _dot(i_ref[r, :], kk * jnp.exp(b_end - b), TN)))
            for ci, dec, kv in terms:
                st_all[ci] = st
                st = st * dec + kv
            return st

        lax.fori_loop(0, nc // HG_UNROLL, fwd_group, st_ref[0, 0])

        def bwd_group(gj, dlb):
            units = []
            for u in range(HG_UNROLL_BWD):
                ci = nc - 1 - (gj * HG_UNROLL_BWD + u)
                r = pl.ds(pl.multiple_of(ci * HG_C, HG_C), HG_C)
                hq, hf, v, do = q_ref[r, :], f_ref[r, :], i_ref[r, :], do_ref[r, :]
                q, sq, sg, f, kk, g = _hg_gates(hq, hf, lb)
                b = _chunk_scan(g, t_iota, False)
                b_end = b[HG_C - 1:HG_C]
                e_b, e_be, dec = jnp.exp(b), jnp.exp(b_end - b), jnp.exp(b_end)
                w_all, ps = _hg_intra_terms(q, kk, b, t_iota)
                x_all = jnp.concatenate([do * v[si:si + 1] for si in range(HG_C)], axis=0)
                units.append(dict(ci=ci, r=r, hq=hq, v=v, do=do, q=q, sq=sq, sg=sg, f=f, kk=kk, e_b=e_b, e_be=e_be,
                                  dec=dec, kd=kk * e_be, w=w_all, ps=ps, x=x_all))
            both = _dot(jnp.concatenate([un["w"] for un in units] + [un["x"] for un in units], axis=0), ones, NN)
            st0s = [st_all[un["ci"]] for un in units]
            st_ends = [st0 * un["dec"] + _dot(un["v"], un["kd"], TN) for un, st0 in zip(units, st0s)]
            dqks = [_dot(un["do"], un["q"] * un["e_b"], TN) for un in units]
            es = [adj[...]]
            for un, dqk in zip(units, dqks):
                es.append(es[-1] * un["dec"] + dqk)
            adj[...] = es[-1]
            for ui, un in enumerate(units):
                e, q, kk, v, do = es[ui], un["q"], un["kk"], un["v"], un["do"]
                tail = jnp.sum(e * st_ends[ui], axis=0, keepdims=True)
                dq = un["e_b"] * _dot(do, st0s[ui], NN)
                dk = un["e_be"] * _dot(v, e, NN)
                dv = _dot(un["kd"], e, NT)
                a0 = ui * cc
                d0 = (HG_UNROLL_BWD + ui) * cc
                for si in range(HG_C):
                    da = both[d0 + si * HG_C:d0 + (si + 1) * HG_C]
                    aa = both[a0 + si * HG_C:a0 + (si + 1) * HG_C]
                    dap = da * un["ps"][si]
                    dq = dq + dap * kk[si:si + 1]
                    hit = t_iota == si
                    dk = dk + jnp.where(hit, jnp.sum(dap * q, axis=0, keepdims=True), 0.0)
                    dv = dv + jnp.where(hit, jnp.sum(aa * do, axis=0, keepdims=True), 0.0)
                dg = _chunk_scan(q * dq - kk * dk, t_iota, True) + tail
                dfg = dg / un["f"] - dk
                sq, sg, hq, r = un["sq"], un["sg"], un["hq"], un["r"]
                dq_ref[r, :] = (dq * sq * (1.0 + hq * (1.0 - sq))).astype(dq_ref.dtype)
                df_ref[r, :] = (dfg * (1.0 - lb) * sg * (1.0 - sg)).astype(df_ref.dtype)
                di_ref[r, :] = dv.astype(di_ref.dtype)
                dlb = dlb + jnp.sum(dfg * (1.0 - sg), axis=0, keepdims=True)
            return dlb

        dlb = lax.fori_loop(0, nc // HG_UNROLL_BWD, bwd_group, jnp.zeros((1, HG_DK), F32))
        dl0 = dlb * lb * (1.0 - lb)
        dl_ref[...] += jnp.concatenate([dl0, -dl0], axis=0)

    col = lambda off: pl.BlockSpec((HG_T, HG_DK), lambda h, t: (nt - 1 - t, off + h))
    out_col = pl.BlockSpec((HG_T, HG_DK), lambda h, t: (nt - 1 - t, h))
    return pl.pallas_call(
        body, name="hgrn2_bwd", grid=(HG_H, nt),
        in_specs=[col(0), col(8), col(16), pl.BlockSpec((2, HG_DK), lambda h, t: (0, h)),
                  pl.BlockSpec((1, 1, HG_DK, HG_DK), lambda h, t: (h, nt - 1 - t, 0, 0)), col(0)],
        out_specs=[out_col, out_col, out_col, pl.BlockSpec((2, HG_DK), lambda h, t: (0, h))],
        out_shape=[jax.ShapeDtypeStruct((s, D), BF16)] * 3 + [jax.ShapeDtypeStruct((2, D), F32)],
        scratch_shapes=[pltpu.VMEM((nc, HG_DK, HG_DK), F32), pltpu.VMEM((HG_DK, HG_DK), F32)],
        compiler_params=_cparams(("arbitrary", "arbitrary")),
    )(proj, proj, proj, lb_logits, states, do_raw)


def _hg_post_fwd(o_raw, proj, gnorm, tm=256):
    s = o_raw.shape[0]

    def fn(o, hg, gn):
        outs = []
        for h in range(HG_H):
            sl = slice(h * HG_DK, (h + 1) * HG_DK)
            oh, gh = o[:, sl], hg[:, sl]
            r = lax.rsqrt(jnp.mean(oh * oh, axis=-1, keepdims=True) + EPS)
            outs.append(oh * r * gn * (gh * _sigmoid(gh)))
        return (jnp.concatenate(outs, axis=1),), ()

    return _rows("hgrn2_out_fwd", fn, s, tm, [_rb(o_raw, tm, D), _rb(proj, tm, D, 3), _cst(gnorm)], [(D, BF16)])[0]


def _hg_post_bwd(do_a, o_raw, proj, gnorm, tm=256):
    s = o_raw.shape[0]

    def fn(da, o, hg, gn):
        dos, dhgs = [], []
        dgn = jnp.zeros((1, HG_DK), F32)
        for h in range(HG_H):
            sl = slice(h * HG_DK, (h + 1) * HG_DK)
            oh, gh, dh = o[:, sl], hg[:, sl], da[:, sl]
            r = lax.rsqrt(jnp.mean(oh * oh, axis=-1, keepdims=True) + EPS)
            xhat = oh * r
            sg = _sigmoid(gh)
            dy = dh * (gh * sg)
            dhgs.append(dh * xhat * gn * sg * (1.0 + gh * (1.0 - sg)))
            dgn = dgn + jnp.sum(dy * xhat, axis=0, keepdims=True)
            dxh = dy * gn
            dos.append(r * (dxh - xhat * jnp.mean(dxh * xhat, axis=-1, keepdims=True)))
        return (jnp.concatenate(dos, axis=1), jnp.concatenate(dhgs, axis=1)), (dgn,)

    ins = [_rb(do_a, tm, D), _rb(o_raw, tm, D), _rb(proj, tm, D, 3), _cst(gnorm)]
    return _rows("hgrn2_out_bwd", fn, s, tm, ins, [(D, F32), (D, BF16)], [(1, HG_DK)])


def _log_sigmoid(z):
    return jnp.minimum(z, 0.0) - jnp.log(1.0 + jnp.exp(-jnp.abs(z)))


def _fox_gate_bwd(dct, pff, bias, tm=256):
    s = pff.shape[0]
    nb = s // tm

    def body(d_ref, p_ref, b_ref, dff_ref, db_ref, carry):
        @pl.when(pl.program_id(0) == 0)
        def _():
            carry[...] = jnp.zeros_like(carry)
            db_ref[...] = jnp.zeros_like(db_ref)

        dc = d_ref[...].T
        dlf = _split_dot(_tri(tm, False), dc, 3) + carry[0:1]
        carry[...] = jnp.broadcast_to(dlf[0:1], carry.shape)
        dff = dlf * _sigmoid(-(p_ref[...] + b_ref[...]))
        dff_ref[...] = dff
        db_ref[...] += jnp.sum(dff, axis=0, keepdims=True)

    return pl.pallas_call(
        body, name="fox_gate_bwd", grid=(nb,),
        in_specs=[pl.BlockSpec((LANES, tm), lambda i: (0, nb - 1 - i)),
                  pl.BlockSpec((tm, LANES), lambda i: (nb - 1 - i, 0)), pl.BlockSpec((1, LANES), lambda i: (0, 0))],
        out_specs=[pl.BlockSpec((tm, LANES), lambda i: (nb - 1 - i, 0)), pl.BlockSpec((1, LANES), lambda i: (0, 0))],
        out_shape=[jax.ShapeDtypeStruct((s, LANES), F32), jax.ShapeDtypeStruct((1, LANES), F32)],
        scratch_shapes=[pltpu.VMEM((8, LANES), F32)],
        compiler_params=_cparams(("arbitrary",)),
    )(dct, pff, bias)


def _diag_mask(t):
    r = lax.broadcasted_iota(jnp.int32, (t, t), 0)
    c = lax.broadcasted_iota(jnp.int32, (t, t), 1)
    return r >= c


AUX_ONES = 6


def _pieces(x):
    h = x.astype(BF16)
    r = x - h.astype(F32)
    m = r.astype(BF16)
    return h, m, (r - m.astype(F32)).astype(BF16)


def _lane_put(lane, cols, base):
    out = None
    for i, col in enumerate(cols):
        term = jnp.where(lane == base + i, col.astype(F32), 0.0)
        out = term if out is None else out + term
    return out


def _fox_prep2(proj, pff, bias, tm=256):
    s = pff.shape[0]

    def body(q_ref, k_ref, v_ref, p_ref, b_ref, qb_ref, kb_ref, vb_ref, ka_ref, carry):
        @pl.when(pl.program_id(0) == 0)
        def _():
            carry[...] = jnp.zeros_like(carry)

        qb_ref[...] = (q_ref[...] * 0.125).astype(BF16)
        kb_ref[...] = k_ref[...].astype(BF16)
        vb_ref[...] = v_ref[...].astype(BF16)
        lf = _log_sigmoid(p_ref[...] + b_ref[...])
        c = _split_dot(_tri(tm, True), lf, 3) + carry[0:1]
        carry[...] = jnp.broadcast_to(c[tm - 1:tm], carry.shape)
        lane = lax.broadcasted_iota(jnp.int32, (tm, LANES), 1)
        ones = jnp.where((lane >= AUX_ONES) & (lane < AUX_ONES + 6), 1.0, 0.0)
        for p in range(FOX_H // 2):
            aux = ones
            for z in range(2):
                col = jnp.sum(jnp.where(lane == 2 * p + z, c, 0.0), axis=1, keepdims=True)
                aux = aux + _lane_put(lane, _pieces(-col), 3 * z)
            ka_ref[:, p * LANES:(p + 1) * LANES] = aux.astype(BF16)

    row = lambda cb: pl.BlockSpec((tm, D), lambda i: (i, cb))
    return pl.pallas_call(
        body, name="fox_prep", grid=(s // tm,),
        in_specs=[row(4), row(5), row(6), pl.BlockSpec((tm, LANES), lambda i: (i, 0)),
                  pl.BlockSpec((1, LANES), lambda i: (0, 0))],
        out_specs=[row(0)] * 4, out_shape=[jax.ShapeDtypeStruct((s, D), BF16)] * 4,
        scratch_shapes=[pltpu.VMEM((8, LANES), F32)],
        compiler_params=_cparams(("arbitrary",)),
    )(proj, proj, proj, pff, bias)


def _fox_fwd2(qb, kb, vb, ka):
    s = qb.shape[0]
    t = min(FOX_T, s)
    nq = s // t

    def body(q_ref, k_ref, v_ref, ka_ref, o_ref, la_ref):
        i = pl.program_id(1)
        lane = lax.broadcasted_iota(jnp.int32, (t, LANES), 1)
        in_a = lane < FOX_D
        q = q_ref[...]
        zero = jnp.zeros_like(q)
        qh = [jnp.where(in_a, q, zero), jnp.where(in_a, zero, q)]
        c_ones = [jnp.where((lane >= 3 * z) & (lane < 3 * z + 3), 1.0, 0.0) for z in range(2)]
        dmask = _diag_mask(t)

        def keys(j):
            rows = pl.ds(pl.multiple_of(j * t, t), t)
            return jnp.concatenate([k_ref[rows, :], ka_ref[rows, :]], axis=1), rows

        def logits(qx, kk, masked):
            e = lax.dot_general(qx, kk, (NT, ((), ())), preferred_element_type=F32)
            return jnp.where(dmask, e, -1e30) if masked else e

        qc = [jnp.concatenate([qh[z], c_ones[z].astype(BF16)], axis=1) for z in range(2)]

        def step(j, carry, masked):
            kk, rows = keys(j)
            vj = v_ref[rows, :]
            scores = [logits(qc[z], kk, masked) for z in range(2)]
            out = []
            for z in range(2):
                m, l, acc = carry[z]
                m_new = jnp.maximum(m, jnp.max(scores[z], axis=1, keepdims=True))
                p = jnp.exp(scores[z] - m_new)
                alpha = jnp.exp(m - m_new)
                out.append((m_new, alpha * l + jnp.sum(p, axis=1, keepdims=True), alpha * acc + _dot(p, vj, NN)))
            return tuple(out)

        init = tuple((jnp.full((t, 1), -1e30, F32), jnp.zeros((t, 1), F32), jnp.zeros((t, LANES), F32))
                     for _ in range(2))
        (ma, la, acc_a), (mb, lb, acc_b) = step(i, lax.fori_loop(0, i, lambda j, c: step(j, c, False), init), True)
        o_ref[...] = jnp.where(in_a, acc_a / la, acc_b / lb).astype(o_ref.dtype)
        la_ref[...] = (_lane_put(lane, _pieces(-(ma + jnp.log(la))), AUX_ONES)
                       + _lane_put(lane, _pieces(-(mb + jnp.log(lb))), AUX_ONES + 3)).astype(la_ref.dtype)

    blk = pl.BlockSpec((t, LANES), lambda p, i: (i, p))
    whole = pl.BlockSpec((s, LANES), lambda p, i: (0, p))
    return pl.pallas_call(
        body, name="fox_attn_fwd", grid=(FOX_H // 2, nq), in_specs=[blk, whole, whole, whole],
        out_specs=[blk, blk], out_shape=[jax.ShapeDtypeStruct((s, D), BF16)] * 2,
        compiler_params=_cparams(("arbitrary", "arbitrary")),
    )(qb, kb, vb, ka)


def _fox_bwd2(qb, kb, vb, ka, ob, laux, dob):
    s = qb.shape[0]
    t = min(FOX_T, s)
    nq = s // t

    def body(q_ref, k_ref, v_ref, ka_ref, o_ref, la_ref, do_ref, dq_ref, dk_ref, dv_ref, dc_ref):
        i = pl.program_id(1)

        @pl.when(i == 0)
        def _():
            dk_ref[...] = jnp.zeros_like(dk_ref)
            dv_ref[...] = jnp.zeros_like(dv_ref)
            dc_ref[...] = jnp.zeros_like(dc_ref)

        lane = lax.broadcasted_iota(jnp.int32, (t, LANES), 1)
        in_a = lane < FOX_D
        q, do, la = q_ref[...], do_ref[...], la_ref[...].astype(F32)
        zero = jnp.zeros_like(q)
        qh = [jnp.where(in_a, q, zero), jnp.where(in_a, zero, q)]
        doh = [jnp.where(in_a, do, zero), jnp.where(in_a, zero, do)]
        prod = do.astype(F32) * o_ref[...].astype(F32)
        qx, dox = [], []
        for z in range(2):
            delta = jnp.sum(jnp.where(in_a if z == 0 else ~in_a, prod, 0.0), axis=1, keepdims=True)
            c_ones = jnp.where((lane >= 3 * z) & (lane < 3 * z + 3), 1.0, 0.0)
            lse_lanes = (lane >= AUX_ONES + 3 * z) & (lane < AUX_ONES + 3 * z + 3)
            qx.append(jnp.concatenate([qh[z], (c_ones + jnp.where(lse_lanes, la, 0.0)).astype(BF16)], axis=1))
            dox.append(jnp.concatenate([doh[z], _lane_put(lane, _pieces(-delta), 3 * z).astype(BF16)], axis=1))
        v_ones = jnp.where(lane < 6, 1.0, 0.0).astype(BF16)
        dmask = _diag_mask(t)

        def step(j, carry, masked):
            rows = pl.ds(pl.multiple_of(j * t, t), t)
            kj, vj = k_ref[rows, :], v_ref[rows, :]
            kk = jnp.concatenate([kj, ka_ref[rows, :]], axis=1)
            vv = jnp.concatenate([vj, v_ones], axis=1)
            out = []
            dk_add, dv_add = None, None
            for z in range(2):
                dq, rsum = carry[z]
                e = lax.dot_general(qx[z], kk, (NT, ((), ())), preferred_element_type=F32)
                if masked:
                    e = jnp.where(dmask, e, -1e30)
                p = jnp.exp(e)
                ds = p * lax.dot_general(dox[z], vv, (NT, ((), ())), preferred_element_type=F32)
                dkz, dvz = _dot(ds, qh[z], TN), _dot(p, doh[z], TN)
                dk_add = dkz if dk_add is None else dk_add + dkz
                dv_add = dvz if dv_add is None else dv_add + dvz
                dc_ref[0, z, j] += -jnp.sum(ds, axis=0, keepdims=True)
                out.append((dq + _dot(ds, kj, NN), rsum + jnp.sum(ds, axis=1, keepdims=True)))
            dk_ref[rows, :] += dk_add
            dv_ref[rows, :] += dv_add
            return tuple(out)

        init = tuple((jnp.zeros((t, LANES), F32), jnp.zeros((t, 1), F32)) for _ in range(2))
        (dq_a, rs_a), (dq_b, rs_b) = step(i, lax.fori_loop(0, i, lambda j, c: step(j, c, False), init), True)
        for z, rs in enumerate((rs_a, rs_b)):
            dc_ref[0, z, i] += jnp.transpose(jnp.broadcast_to(rs, (t, LANES)))[0:1]
        dq_ref[...] = (jnp.where(in_a, dq_a, dq_b) * 0.125).astype(dq_ref.dtype)

    blk = pl.BlockSpec((t, LANES), lambda p, i: (i, p))
    whole = pl.BlockSpec((s, LANES), lambda p, i: (0, p))
    dc_spec = pl.BlockSpec((1, 2, nq, 1, t), lambda p, i: (p, 0, 0, 0, 0))
    return pl.pallas_call(
        body, name="fox_attn_bwd", grid=(FOX_H // 2, nq),
        in_specs=[blk, whole, whole, whole, blk, blk, blk], out_specs=[blk, whole, whole, dc_spec],
        out_shape=[jax.ShapeDtypeStruct((s, D), BF16), jax.ShapeDtypeStruct((s, D), F32),
                   jax.ShapeDtypeStruct((s, D), F32), jax.ShapeDtypeStruct((FOX_H // 2, 2, nq, 1, t), F32)],
        compiler_params=_cparams(("arbitrary", "arbitrary")),
    )(qb, kb, vb, ka, ob, laux, dob)


def _adamw(name, w, g, m, v, tm=None):
    rows, width = w.shape
    tm = rows if tm is None else tm
    c1 = 1.0 - ADAM_B1 ** ADAM_STEP
    c2 = 1.0 - ADAM_B2 ** ADAM_STEP

    def fn(wb, gb, mb, vb):
        m_new = ADAM_B1 * mb + (1.0 - ADAM_B1) * gb
        v_new = ADAM_B2 * vb + (1.0 - ADAM_B2) * (gb * gb)
        delta = -ADAM_LR * ((m_new / c1) / (jnp.sqrt(v_new / c2) + ADAM_EPS) + ADAM_WD * wb)
        return (delta, m_new, v_new), ()

    ins = [_rb(a, tm, width) for a in (w, g, m, v)]
    return _rows(name, fn, rows, tm, ins, [(width, F32)] * 3)


def _me():
    return lax.axis_index("x"), lax.axis_index("y"), lax.axis_index("c")


def _all_gather8(name, block):
    m, n = block.shape

    def body(x_ref, out_ref, send_sems, recv_sems, local_sem):
        x, y, c = _me()
        me, sibling = (x, y, c), (x, y, 1 - c)
        chips = [(1 - x, y), (x, 1 - y), (1 - x, 1 - y)]

        def slot(px, py, pc):
            return out_ref.at[4 * px + 2 * py + pc]

        def copy(k, blk, to, src=None):
            return pltpu.make_async_remote_copy(
                src_ref=slot(*blk) if src is None else src, dst_ref=slot(*blk),
                send_sem=send_sems.at[k], recv_sem=recv_sems.at[k], device_id=to, device_id_type=MESH)

        mine = pltpu.make_async_copy(x_ref, slot(*me), local_sem)
        mine.start()
        first = [copy(0, me, sibling, src=x_ref)]
        first += [copy(1 + j, me, (*chip, c), src=x_ref) for j, chip in enumerate(chips)]
        for cp in first:
            cp.start()
        passed = [copy(4 + j, (*chip, c), sibling) for j, chip in enumerate(chips)]
        for j, chip in enumerate(chips):
            copy(1 + j, (*chip, c), me).wait_recv()
            passed[j].start()
        copy(0, sibling, me).wait_recv()
        for j, chip in enumerate(chips):
            copy(4 + j, (*chip, 1 - c), me).wait_recv()
        for cp in first + passed:
            cp.wait_send()
        mine.wait()

    return pl.pallas_call(
        body, name=name, in_specs=[ANY], out_specs=ANY,
        out_shape=jax.ShapeDtypeStruct((8, m, n), block.dtype),
        scratch_shapes=[pltpu.SemaphoreType.DMA((7,)), pltpu.SemaphoreType.DMA((7,)), pltpu.SemaphoreType.DMA],
    )(block)


def _swap_sibling(name, g2, pick_other):
    shape = g2.shape[1:]

    def body(g_ref, out_ref, send_sem, recv_sem):
        x, y, c = _me()
        src = g_ref.at[1 - c] if pick_other else g_ref.at[0]
        cp = pltpu.make_async_remote_copy(src_ref=src, dst_ref=out_ref, send_sem=send_sem, recv_sem=recv_sem,
                                          device_id=(x, y, 1 - c), device_id_type=MESH)
        cp.start()
        cp.wait()

    return pl.pallas_call(
        body, name=name, in_specs=[ANY], out_specs=ANY, out_shape=jax.ShapeDtypeStruct(shape, g2.dtype),
        scratch_shapes=[pltpu.SemaphoreType.DMA, pltpu.SemaphoreType.DMA],
    )(g2)


def _chip_exchange(name, p):
    def body(p_ref, out_ref, send_sems, recv_sems, local_sem):
        x, y, c = _me()
        my_chip = 2 * x + y
        chips = [(1 - x, y), (x, 1 - y), (1 - x, 1 - y)]
        mine = pltpu.make_async_copy(p_ref.at[my_chip], out_ref.at[my_chip], local_sem)
        mine.start()
        sends = []
        for k, (px, py) in enumerate(chips):
            sends.append(pltpu.make_async_remote_copy(
                src_ref=p_ref.at[2 * px + py], dst_ref=out_ref.at[my_chip], send_sem=send_sems.at[k],
                recv_sem=recv_sems.at[k], device_id=(px, py, c), device_id_type=MESH))
        for cp in sends:
            cp.start()
        for k, (px, py) in enumerate(chips):
            pltpu.make_async_remote_copy(
                src_ref=p_ref.at[my_chip], dst_ref=out_ref.at[2 * px + py], send_sem=send_sems.at[k],
                recv_sem=recv_sems.at[k], device_id=(px, py, c), device_id_type=MESH).wait_recv()
        for cp in sends:
            cp.wait_send()
        mine.wait()

    return pl.pallas_call(
        body, name=name, in_specs=[ANY], out_specs=ANY, out_shape=jax.ShapeDtypeStruct(p.shape, p.dtype),
        scratch_shapes=[pltpu.SemaphoreType.DMA((3,)), pltpu.SemaphoreType.DMA((3,)), pltpu.SemaphoreType.DMA],
    )(p)


def _all_reduce_small(name, block):
    r, n = block.shape

    def body(x_ref, sum_ref, gath, send_sems, recv_sems):
        x, y, c = _me()
        me = 4 * x + 2 * y + c
        gath[me] = x_ref[...]
        sends = []
        for k in range(1, 8):
            px = x ^ ((k >> 2) & 1)
            py = y ^ ((k >> 1) & 1)
            pc = c ^ (k & 1)
            sends.append(pltpu.make_async_remote_copy(
                src_ref=x_ref, dst_ref=gath.at[me], send_sem=send_sems.at[k - 1], recv_sem=recv_sems.at[k - 1],
                device_id=(px, py, pc), device_id_type=MESH))
        for cp in sends:
            cp.start()
        for k in range(1, 8):
            peer = me ^ k
            pltpu.make_async_remote_copy(
                src_ref=x_ref, dst_ref=gath.at[peer], send_sem=send_sems.at[k - 1], recv_sem=recv_sems.at[k - 1],
                device_id=(x, y, c), device_id_type=MESH).wait_recv()
        for cp in sends:
            cp.wait_send()
        acc = gath[0]
        for d in range(1, 8):
            acc = acc + gath[d]
        sum_ref[...] = acc

    vm = pl.BlockSpec(memory_space=pltpu.VMEM)
    return pl.pallas_call(
        body, name=name, in_specs=[vm], out_specs=vm, out_shape=jax.ShapeDtypeStruct((r, n), F32),
        scratch_shapes=[pltpu.VMEM((8, r, n), F32), pltpu.SemaphoreType.DMA((7,)), pltpu.SemaphoreType.DMA((7,))],
    )(block)


def _add2(name, a, b, tm):
    rows = a.shape[0]
    return _rows(name, lambda p, q: ((p + q,), ()), rows, tm, [_rb(a, tm, LANES), _rb(b, tm, LANES)], [(LANES, F32)])[0]


def _add4(name, p, tm):
    m = p.shape[1]
    flat = p.reshape(4 * m, LANES)
    nb = m // tm
    ins = [(flat, (tm, LANES), (lambda i, j=j: (j * nb + i, 0))) for j in range(4)]
    return _rows(name, lambda a, b, c, d: ((((a + b) + c) + d,), ()), m, tm, ins, [(LANES, F32)])[0]


SEG_ROWS = (D * W_IN_SHARD // LANES, 256 * D // LANES, 256 * D // LANES, 256 * D // LANES,
            D * W_UP_SHARD // LANES, W_DOWN_SHARD * D // LANES)
GRAD_ROWS = sum(SEG_ROWS)
CONVW_ROWS = 3 * W_UP_SHARD * 2 // LANES
GATHER_ROWS = 41600


def _flat(a):
    return a.reshape(-1, LANES)


def _gather_weights(w_in, w_a, w_b, w_out, w_up, w_down, conv_w):
    c = lax.axis_index("c")
    bits = lax.bitcast_convert_type(conv_w, BF16)
    pieces = [_flat(t.astype(BF16)) for t in (w_in, w_a, w_b, w_out, w_up, w_down)] + [_flat(bits)]
    pad = GATHER_ROWS - GRAD_ROWS - CONVW_ROWS
    shard = jnp.concatenate(pieces + [jnp.zeros((pad, LANES), BF16)], axis=0)
    half = GATHER_ROWS // 2
    mine = lax.dynamic_slice_in_dim(shard, c * half, half, axis=0)
    full = _all_gather8("all_gather_weights", mine).reshape(N_CHIP, GATHER_ROWS, LANES)
    offs = [0]
    for r in SEG_ROWS:
        offs.append(offs[-1] + r)
    seg = lambda i: full[:, offs[i]:offs[i + 1]]
    wi = seg(0).reshape(N_CHIP, D, W_IN_SHARD).transpose(1, 0, 2).reshape(D, N_CHIP * W_IN_SHARD)
    w_main = jnp.concatenate([wi[:, :FF_COL], wi[:, FF_COL + FOX_H:]], axis=1)
    w_ff = jnp.pad(wi[:, FF_COL:FF_COL + FOX_H], ((0, 0), (0, LANES - FOX_H)))
    wa, wb, wo = (seg(i).reshape(D, D) for i in (1, 2, 3))
    wu = seg(4).reshape(N_CHIP, D, W_UP_SHARD).transpose(1, 0, 2).reshape(D, 2 * D_FF)
    wd = seg(5).reshape(D_FF, D)
    cw_bits = full[:, GRAD_ROWS:GRAD_ROWS + CONVW_ROWS].reshape(N_CHIP, 3, W_UP_SHARD, 2)
    cw = lax.bitcast_convert_type(cw_bits, F32).transpose(1, 0, 2).reshape(3, 2 * D_FF)
    return w_main, w_ff, wa, wb, wo, wu, wd, cw


def _reduce_scatter_grads(d_main, d_ff, d_a, d_b, d_o, d_u, d_d):
    c = lax.axis_index("c")
    d_in = jnp.concatenate(d_main[:7] + [d_ff[:, :FOX_H]] + d_main[7:], axis=1)
    per_chip = [
        d_in.reshape(D, N_CHIP, W_IN_SHARD).transpose(1, 0, 2).reshape(N_CHIP, -1, LANES),
        d_a.reshape(N_CHIP, -1, LANES), d_b.reshape(N_CHIP, -1, LANES), d_o.reshape(N_CHIP, -1, LANES),
        d_u.reshape(D, N_CHIP, W_UP_SHARD).transpose(1, 0, 2).reshape(N_CHIP, -1, LANES),
        d_d.reshape(N_CHIP, -1, LANES),
        jnp.zeros((N_CHIP, GATHER_ROWS - GRAD_ROWS, LANES), F32),
    ]
    half = GATHER_ROWS // 2
    g = jnp.concatenate(per_chip, axis=1).reshape(N_CHIP, 2, half, LANES).transpose(1, 0, 2, 3)
    from_sibling = _swap_sibling("grad_swap_halves", g, True)
    mine = lax.dynamic_index_in_dim(g, c, axis=0, keepdims=False)
    tm = half // 5
    chip_sum = _add2("grad_chip_sum", mine.reshape(-1, LANES), from_sibling.reshape(-1, LANES), tm)
    pieces = _chip_exchange("grad_chip_exchange", chip_sum.reshape(N_CHIP, half, LANES))
    mine_half = _add4("grad_sum_chips", pieces, tm)
    other_half = _swap_sibling("grad_share_half", mine_half[None], False)
    lo = jnp.where(c == 0, mine_half, other_half)
    hi = jnp.where(c == 0, other_half, mine_half)
    return jnp.concatenate([lo, hi], axis=0)


def _local_step(x, target, norm_mix, fox_f_bias, hg_lb_logits, hg_norm, norm_ffn, conv_b, norm_final,
                w_main, w_ff, wa, wb, wo, wu, wd, conv_w):
    s = x.shape[0]
    bias = jnp.pad(fox_f_bias, ((0, 0), (0, LANES - FOX_H)))
    conv_w8 = jnp.pad(conv_w, ((0, 5), (0, 0)))
    t = min(FOX_T, s)

    n1 = _rms_fwd("norm_mix_fwd", x, norm_mix)
    proj = _mm("in_proj", n1, w_main, "nn", F32, 1024, 1024, D)
    pff = _mm("in_proj_forget", n1, w_ff, "nn", F32, 1024, LANES, D)
    qb, kb, vb, ka = _fox_prep2(proj, pff, bias)
    o_b, laux = _fox_fwd2(qb, kb, vb, ka)
    o_raw, states = _hg_fwd(proj, hg_lb_logits)
    o_a = _hg_post_fwd(o_raw, proj, hg_norm)
    pa = _mm("branch_a", o_a, wa, "nn", F32, 1024, 1024, D)
    pb = _mm("branch_b", o_b, wb, "nn", F32, 1024, 1024, D)
    merged = _merge_fwd(pa, pb, proj)
    h1 = _mm("out_proj", merged, wo, "nn", F32, 1024, 1024, D, res=x)
    n2 = _rms_fwd("norm_ffn_fwd", h1, norm_ffn)
    u = _mm("ffn_up", n2, wu, "nn", F32, 1024, W_UP_SHARD, D)
    act = _convglu_fwd(u, conv_w8, conv_b)
    h2 = _mm("ffn_down", act, wd, "nn", F32, 512, 1024, D_FF, res=h1)
    (dh2,), (d_norm_final, loss_row) = _final(h2, target, norm_final)

    dact = _mm("ffn_down_dx", dh2, wd, "nt", BF16, 1024, D_FF, D)
    d_wd = _mm("ffn_down_dw", act, dh2, "tn", F32, D_FF // 2, 1024, 512)
    (du,), (d_conv_w8, d_conv_b) = _convglu_bwd(u, dact, conv_w8, conv_b)
    dn2 = _mm("ffn_up_dx", du, wu, "nt", F32, 1024, 1024, W_UP_SHARD)
    d_wu = _mm("ffn_up_dw", n2, du, "tn", F32, 1024, W_UP_SHARD, 512)
    (dh1,), (d_norm_ffn,) = _rms_bwd("norm_ffn_bwd", h1, norm_ffn, [dn2], dh2)

    dmerged = _mm("out_proj_dx", dh1, wo, "nt", F32, 1024, 1024, D)
    d_wo = _mm("out_proj_dw", merged, dh1, "tn", F32, 1024, 1024, 512)
    dpa, dpb, dga, dgb = _merge_bwd(dmerged, pa, pb, proj)
    do_a = _mm("branch_a_dx", dpa, wa, "nt", F32, 1024, 1024, D)
    do_b = _mm("branch_b_dx", dpb, wb, "nt", BF16, 1024, 1024, D)
    d_wa = _mm("branch_a_dw", o_a, dpa, "tn", F32, 1024, 1024, 512)
    d_wb = _mm("branch_b_dw", o_b, dpb, "tn", F32, 1024, 1024, 512)

    (do_raw, dhg), (d_hg_norm,) = _hg_post_bwd(do_a, o_raw, proj, hg_norm)
    dhq, dhf, dhi, d_lb_logits = _hg_bwd(proj, hg_lb_logits, states, do_raw)

    dfq, dfk, dfv, dcrow = _fox_bwd2(qb, kb, vb, ka, o_b, laux, do_b)
    dct = jnp.pad(dcrow.reshape(FOX_H, s), ((0, LANES - FOX_H), (0, 0)))
    dff, d_bias = _fox_gate_bwd(dct, pff, bias)

    pieces = [dhq, dhf, dhi, dhg, dfq, dfk, dfv, dga, dgb]
    dn1 = _mm_sum_nt("in_proj_dx", pieces, w_main, (dff, w_ff), 512, 1024)
    d_w_main = [_mm("in_proj_dw_%d" % i, n1, p, "tn", F32, 1024, 1024, 512) for i, p in enumerate(pieces)]
    d_w_ff = _mm("in_proj_forget_dw", n1, dff, "tn", F32, 1024, LANES, 512)
    (dx,), (d_norm_mix,) = _rms_bwd("norm_mix_bwd", x, norm_mix, [dn1], dh1)

    small = dict(norm_mix=d_norm_mix, fox_f_bias=d_bias[:, :FOX_H], hg_lb_logits=d_lb_logits, hg_norm=d_hg_norm,
                 norm_ffn=d_norm_ffn, conv_b=d_conv_b, norm_final=d_norm_final, conv_w=d_conv_w8[:3], loss=loss_row)
    big = (d_w_main, d_w_ff, d_wa, d_wb, d_wo, d_wu, d_wd)
    return dx, small, big


SMALL_KEYS = ("norm_mix", "fox_f_bias", "hg_lb_logits", "hg_norm", "norm_ffn", "conv_b", "norm_final")


def _pack_small(parts):
    rows, layout = [], []
    for key, arr in parts:
        flat = arr.reshape(-1)
        n = flat.shape[0]
        nr = -(-n // LANES)
        rows.append(jnp.pad(flat, (0, nr * LANES - n)).reshape(nr, LANES))
        layout.append((key, arr.shape, n, nr))
    packed = jnp.concatenate(rows, axis=0)
    pad = -packed.shape[0] % 8
    return jnp.pad(packed, ((0, pad), (0, 0))), layout


def _unpack_small(packed, layout):
    out, r0 = {}, 0
    for key, shape, n, nr in layout:
        out[key] = packed[r0:r0 + nr].reshape(-1)[:n].reshape(shape)
        r0 += nr
    return out


def kernel(x, norm_mix, w_in, fox_f_bias, hg_lb_logits, hg_norm, w_branch_a, w_branch_b, w_out, norm_ffn, w_up, conv_w, conv_b, w_down, norm_final, loss_target, m_norm_mix, m_w_in, m_fox_f_bias, m_hg_lb_logits, m_hg_norm, m_w_branch_a, m_w_branch_b, m_w_out, m_norm_ffn, m_w_up, m_conv_w, m_conv_b, m_w_down, m_norm_final, v_norm_mix, v_w_in, v_fox_f_bias, v_hg_lb_logits, v_hg_norm, v_w_branch_a, v_w_branch_b, v_w_out, v_norm_ffn, v_w_up, v_conv_w, v_conv_b, v_w_down, v_norm_final):
    chip = 2 * lax.axis_index("x") + lax.axis_index("y")
    w_main, w_ff, wa, wb, wo, wu, wd, cw = _gather_weights(
        w_in[0], w_branch_a[0], w_branch_b[0], w_out[0], w_up[0], w_down[0], conv_w[0])
    dx, small, big = _local_step(
        x[0], loss_target[0], norm_mix, fox_f_bias, hg_lb_logits, hg_norm, norm_ffn, conv_b,
        norm_final.reshape(1, D), w_main, w_ff, wa, wb, wo, wu, wd, cw)

    packed, layout = _pack_small([(k, small[k]) for k in SMALL_KEYS + ("conv_w", "loss")])
    red = _unpack_small(_all_reduce_small("all_reduce_small", packed), layout)
    loss = red["loss"][0, 0]
    g_conv_w = lax.dynamic_slice_in_dim(red["conv_w"], chip * W_UP_SHARD, W_UP_SHARD, axis=1)

    gflat = _reduce_scatter_grads(*big)
    offs = [0]
    for r in SEG_ROWS:
        offs.append(offs[-1] + r)
    shapes = [(D, W_IN_SHARD), (256, D), (256, D), (256, D), (D, W_UP_SHARD), (W_DOWN_SHARD, D)]
    g_big = [gflat[offs[i]:offs[i + 1]].reshape(shapes[i]) for i in range(6)]

    names = ["norm_mix", "w_in", "fox_f_bias", "hg_lb_logits", "hg_norm", "w_branch_a", "w_branch_b", "w_out",
             "norm_ffn", "w_up", "conv_w", "conv_b", "w_down", "norm_final"]
    weights = dict(norm_mix=norm_mix, w_in=w_in, fox_f_bias=fox_f_bias, hg_lb_logits=hg_lb_logits, hg_norm=hg_norm,
                   w_branch_a=w_branch_a, w_branch_b=w_branch_b, w_out=w_out, norm_ffn=norm_ffn, w_up=w_up,
                   conv_w=conv_w, conv_b=conv_b, w_down=w_down, norm_final=norm_final)
    ms = dict(norm_mix=m_norm_mix, w_in=m_w_in, fox_f_bias=m_fox_f_bias, hg_lb_logits=m_hg_lb_logits,
              hg_norm=m_hg_norm, w_branch_a=m_w_branch_a, w_branch_b=m_w_branch_b, w_out=m_w_out,
              norm_ffn=m_norm_ffn, w_up=m_w_up, conv_w=m_conv_w, conv_b=m_conv_b, w_down=m_w_down,
              norm_final=m_norm_final)
    vs = dict(norm_mix=v_norm_mix, w_in=v_w_in, fox_f_bias=v_fox_f_bias, hg_lb_logits=v_hg_lb_logits,
              hg_norm=v_hg_norm, w_branch_a=v_w_branch_a, w_branch_b=v_w_branch_b, w_out=v_w_out,
              norm_ffn=v_norm_ffn, w_up=v_w_up, conv_w=v_conv_w, conv_b=v_conv_b, w_down=v_w_down,
              norm_final=v_norm_final)

    grads, deltas, new_m, new_v = {}, {}, {}, {}
    big_names = ["w_in", "w_branch_a", "w_branch_b", "w_out", "w_up", "w_down"]
    for name, g2 in zip(big_names, g_big):
        shape = weights[name].shape
        rows = g2.shape[0]
        d_, m_, v_ = _adamw("adamw_" + name, weights[name][0], g2, ms[name][0], vs[name][0], tm=rows // 8)
        grads[name], deltas[name], new_m[name], new_v[name] = (a.reshape(shape) for a in (g2, d_, m_, v_))
    shape = conv_w.shape
    d_, m_, v_ = _adamw("adamw_conv_w", conv_w[0], g_conv_w, m_conv_w[0], v_conv_w[0])
    grads["conv_w"], deltas["conv_w"], new_m["conv_w"], new_v["conv_w"] = (
        a.reshape(shape) for a in (g_conv_w, d_, m_, v_))
    gs = {k: red[k].reshape(weights[k].shape) for k in SMALL_KEYS}
    pw, lay = _pack_small([(k, weights[k]) for k in SMALL_KEYS])
    pg, _ = _pack_small([(k, gs[k]) for k in SMALL_KEYS])
    pm, _ = _pack_small([(k, ms[k]) for k in SMALL_KEYS])
    pv, _ = _pack_small([(k, vs[k]) for k in SMALL_KEYS])
    d_, m_, v_ = (_unpack_small(a, lay) for a in _adamw("adamw_small", pw, pg, pm, pv))
    for k in SMALL_KEYS:
        grads[k], deltas[k], new_m[k], new_v[k] = gs[k], d_[k], m_[k], v_[k]

    return (loss, dx[None], *[grads[n] for n in names], *[deltas[n] for n in names],
            *[new_m[n] for n in names], *[new_v[n] for n in names])
```

```python
import functools

import jax
import jax.numpy as jnp
from jax import lax
from jax.experimental import pallas as pl
from jax.experimental.pallas import tpu as pltpu

F32 = jnp.float32
BF16 = jnp.bfloat16

D = 1024
HG_H, HG_DK = 8, 128
FOX_H, FOX_D = 16, 64
D_FF = 2816
EPS = 1e-6
N_CHIP = 4
LANES = 128
W_IN_SHARD = 2308
W_UP_SHARD = 1408
W_DOWN_SHARD = 704
FF_COL = 7168
ADAM_LR, ADAM_B1, ADAM_B2, ADAM_EPS, ADAM_WD, ADAM_STEP = 0.001, 0.9, 0.999, 1e-08, 0.01, 10

HG_C = 16
HG_T = 256
HG_UNROLL = 4
HG_UNROLL_BWD = 2
FOX_T = 512
VMEM_LIMIT = 56 * 1024 * 1024
MESH = pl.DeviceIdType.MESH
ANY = pl.BlockSpec(memory_space=pl.ANY)


def _cparams(sem):
    return pltpu.CompilerParams(dimension_semantics=sem, vmem_limit_bytes=VMEM_LIMIT)


def _sigmoid(x):
    return 1.0 / (1.0 + jnp.exp(-x))


def _dot(a, b, dims):
    return lax.dot_general(a.astype(BF16), b.astype(BF16), (dims, ((), ())), preferred_element_type=F32)


NN = ((1,), (0,))
NT = ((1,), (1,))
TN = ((0,), (0,))


def _split_dot(tri, x, parts, dims=NN):
    acc = None
    r = x
    for _ in range(parts):
        p = r.astype(BF16)
        t = lax.dot_general(tri, p, (dims, ((), ())), preferred_element_type=F32)
        acc = t if acc is None else acc + t
        r = r - p.astype(F32)
    return acc


def _rb(arr, tm, width, cb=0):
    return (arr, (tm, width), lambda i: (i, cb))


def _cst(arr):
    return (arr, arr.shape, lambda i: (0,) * arr.ndim)


def _rows(name, fn, n_rows, tm, ins, outs, accs=(), reverse=False):
    n_in, n_out, n_acc = len(ins), len(outs), len(accs)
    nb = n_rows // tm

    def body(*refs):
        vals = [r[...] for r in refs[:n_in]]
        o, a = fn(*vals)
        for r, v in zip(refs[n_in:n_in + n_out], o):
            r[...] = v.astype(r.dtype)
        if n_acc:
            acc_refs = refs[n_in + n_out:]

            @pl.when(pl.program_id(0) == 0)
            def _():
                for r in acc_refs:
                    r[...] = jnp.zeros_like(r)

            for r, v in zip(acc_refs, a):
                r[...] += v

    if reverse:
        rowmap = lambda i: (nb - 1 - i, 0)
    else:
        rowmap = lambda i: (i, 0)
    in_specs = [pl.BlockSpec(bs, im) for (_, bs, im) in ins]
    out_specs = [pl.BlockSpec((tm, w), rowmap) for (w, _) in outs]
    out_specs += [pl.BlockSpec((r, w), lambda i: (0, 0)) for (r, w) in accs]
    out_shape = [jax.ShapeDtypeStruct((n_rows, w), dt) for (w, dt) in outs]
    out_shape += [jax.ShapeDtypeStruct((r, w), F32) for (r, w) in accs]
    res = pl.pallas_call(
        body, name=name, grid=(nb,), in_specs=in_specs, out_specs=out_specs, out_shape=out_shape,
        compiler_params=_cparams(("arbitrary",)),
    )(*[a for a, _, _ in ins])
    return (res[:n_out], res[n_out:]) if n_acc else res


def _mm(name, a, b, mode, out_dtype, tm, tn, tk, res=None):
    if mode == "nn":
        (m, k), n = a.shape, b.shape[1]
    elif mode == "nt":
        (m, k), n = a.shape, b.shape[0]
    else:
        (k, m), n = a.shape, b.shape[1]
    tm, tn, tk = min(tm, m), min(tn, n), min(tk, k)
    assert m % tm == 0 and n % tn == 0 and k % tk == 0, (name, m, n, k, tm, tn, tk)
    if mode == "nn":
        a_spec = pl.BlockSpec((tm, tk), lambda i, j, kk: (i, kk))
        b_spec = pl.BlockSpec((tk, tn), lambda i, j, kk: (kk, j))
        dims = NN
    elif mode == "nt":
        a_spec = pl.BlockSpec((tm, tk), lambda i, j, kk: (i, kk))
        b_spec = pl.BlockSpec((tn, tk), lambda i, j, kk: (j, kk))
        dims = NT
    else:
        a_spec = pl.BlockSpec((tk, tm), lambda i, j, kk: (kk, i))
        b_spec = pl.BlockSpec((tk, tn), lambda i, j, kk: (kk, j))
        dims = TN
    nk = k // tk
    has_res = res is not None

    def body(*refs):
        a_ref, b_ref = refs[0], refs[1]
        r_ref = refs[2] if has_res else None
        o_ref = refs[3] if has_res else refs[2]
        part = _dot(a_ref[...], b_ref[...], dims)

        def finish(val):
            if has_res:
                val = val + r_ref[...]
            o_ref[...] = val.astype(o_ref.dtype)

        if nk == 1:
            finish(part)
        else:
            acc_ref = refs[-1]
            kk = pl.program_id(2)

            @pl.when(kk == 0)
            def _():
                acc_ref[...] = part

            @pl.when(kk > 0)
            def _():
                acc_ref[...] += part

            @pl.when(kk == nk - 1)
            def _():
                finish(acc_ref[...])

    in_specs = [a_spec, b_spec]
    args = [a, b]
    if has_res:
        in_specs.append(pl.BlockSpec((tm, tn), lambda i, j, kk: (i, j)))
        args.append(res)
    return pl.pallas_call(
        body, name=name, grid=(m // tm, n // tn, nk), in_specs=in_specs,
        out_specs=pl.BlockSpec((tm, tn), lambda i, j, kk: (i, j)),
        out_shape=jax.ShapeDtypeStruct((m, n), out_dtype),
        scratch_shapes=[pltpu.VMEM((tm, tn), F32)] if nk > 1 else [],
        compiler_params=_cparams(("arbitrary", "arbitrary", "arbitrary")),
    )(*args)


def _mm_sum_nt(name, pieces, w, extra, tm, tn):
    n_p = len(pieces)
    m, k = pieces[0].shape
    n = w.shape[0]
    xa, xb = extra
    ke = xa.shape[1]
    tm, tn = min(tm, m), min(tn, n)

    def body(*refs):
        p_refs, w_ref, xa_ref, xb_ref, o_ref, acc_ref = refs[:n_p], refs[n_p], refs[n_p + 1], refs[n_p + 2], refs[-2], refs[-1]
        kk = pl.program_id(2)

        @pl.when(kk == 0)
        def _():
            acc_ref[...] = jnp.zeros_like(acc_ref)

        for i in range(n_p):
            @pl.when(kk == i)
            def _(i=i):
                acc_ref[...] += _dot(p_refs[i][...], w_ref[...], NT)

        @pl.when(kk == n_p)
        def _():
            o_ref[...] = acc_ref[...] + _dot(xa_ref[...], xb_ref[...], NT)

    in_specs = [pl.BlockSpec((tm, k), lambda i, j, kk: (i, 0)) for _ in range(n_p)]
    in_specs.append(pl.BlockSpec((tn, k), lambda i, j, kk: (j, jnp.minimum(kk, n_p - 1))))
    in_specs += [pl.BlockSpec((tm, ke), lambda i, j, kk: (i, 0)), pl.BlockSpec((tn, ke), lambda i, j, kk: (j, 0))]
    return pl.pallas_call(
        body, name=name, grid=(m // tm, n // tn, n_p + 1), in_specs=in_specs,
        out_specs=pl.BlockSpec((tm, tn), lambda i, j, kk: (i, j)),
        out_shape=jax.ShapeDtypeStruct((m, n), F32),
        scratch_shapes=[pltpu.VMEM((tm, tn), F32)],
        compiler_params=_cparams(("arbitrary", "arbitrary", "arbitrary")),
    )(*pieces, w, xa, xb)


def _rms_fwd(name, x, gain, tm=256):
    s = x.shape[0]

    def fn(xb, g):
        r = lax.rsqrt(jnp.mean(xb * xb, axis=-1, keepdims=True) + EPS)
        return (xb * r * g,), ()

    return _rows(name, fn, s, tm, [_rb(x, tm, D), _cst(gain)], [(D, BF16)])[0]


def _rms_bwd(name, x, gain, dns, dres, tm=256):
    s = x.shape[0]
    n_dn = len(dns)

    def fn(xb, g, *rest):
        dn = rest[0]
        for t in rest[1:n_dn]:
            dn = dn + t
        r = lax.rsqrt(jnp.mean(xb * xb, axis=-1, keepdims=True) + EPS)
        xhat = xb * r
        dxh = dn * g
        dx = r * (dxh - xhat * jnp.mean(dxh * xhat, axis=-1, keepdims=True)) + rest[n_dn]
        return (dx,), (jnp.sum(dn * xhat, axis=0, keepdims=True),)

    ins = [_rb(x, tm, D), _cst(gain)] + [_rb(t, tm, D) for t in dns] + [_rb(dres, tm, D)]
    return _rows(name, fn, s, tm, ins, [(D, F32)], [(1, D)])


def _final(h2, target, gain, tm=256):
    s = h2.shape[0]

    def fn(hb, tb, g):
        r = lax.rsqrt(jnp.mean(hb * hb, axis=-1, keepdims=True) + EPS)
        xhat = hb * r
        e = xhat * g - tb
        dy = e * (1.0 / D)
        dxh = dy * g
        dh = r * (dxh - xhat * jnp.mean(dxh * xhat, axis=-1, keepdims=True))
        lrow = 0.5 * jnp.sum(jnp.sum(e * e, axis=-1, keepdims=True) * (1.0 / D), axis=0, keepdims=True)
        return (dh,), (jnp.sum(dy * xhat, axis=0, keepdims=True), jnp.broadcast_to(lrow, (1, LANES)))

    return _rows("final_norm_loss", fn, s, tm, [_rb(h2, tm, D), _rb(target, tm, D), _cst(gain)],
                 [(D, F32)], [(1, D), (1, LANES)])


def _merge_fwd(pa, pb, proj, tm=256):
    s = pa.shape[0]

    def fn(a, b, ga, gb):
        return (_sigmoid(ga) * a + _sigmoid(gb) * b,), ()

    ins = [_rb(pa, tm, D), _rb(pb, tm, D), _rb(proj, tm, D, 7), _rb(proj, tm, D, 8)]
    return _rows("merge_fwd", fn, s, tm, ins, [(D, BF16)])[0]


def _merge_bwd(dmerged, pa, pb, proj, tm=256):
    s = pa.shape[0]

    def fn(dm, a, b, ga, gb):
        sa, sb = _sigmoid(ga), _sigmoid(gb)
        return (dm * sa, dm * sb, dm * a * sa * (1.0 - sa), dm * b * sb * (1.0 - sb)), ()

    ins = [_rb(dmerged, tm, D), _rb(pa, tm, D), _rb(pb, tm, D), _rb(proj, tm, D, 7), _rb(proj, tm, D, 8)]
    return _rows("merge_bwd", fn, s, tm, ins, [(D, BF16), (D, BF16), (D, BF16), (D, BF16)])


def _gelu_parts(x):
    cdf = 0.5 * (1.0 + lax.erf(x * 0.7071067811865476))
    pdf = 0.3989422804014327 * jnp.exp(-0.5 * x * x)
    return x * cdf, cdf + x * pdf


def _conv_taps(u_ext, n_out, first):
    n = u_ext.shape[0]
    cur = u_ext[8:8 + n_out]
    m1 = pltpu.roll(u_ext, 1, 0)[8:8 + n_out]
    m2 = pltpu.roll(u_ext, 2, 0)[8:8 + n_out]
    return m2, m1, cur


def _convglu_fwd(u, conv_w8, conv_b, tm=64):
    s, w = u.shape
    tb = tm // 8

    def fn(ub, up, cw, cb):
        i = pl.program_id(0)
        up = jnp.where(i == 0, 0.0, up)
        m2, m1, cur = _conv_taps(jnp.concatenate([up, ub], axis=0), tm, None)
        acc = cb + cw[0:1] * m2 + cw[1:2] * m1 + cw[2:3] * cur
        act, _ = _gelu_parts(acc[:, :D_FF])
        return (act * acc[:, D_FF:],), ()

    ins = [_rb(u, tm, w), (u, (8, w), lambda i: (jnp.maximum(i * tb - 1, 0), 0)), _cst(conv_w8), _cst(conv_b)]
    return _rows("convglu_fwd", fn, s, tm, ins, [(D_FF, BF16)])[0]


def _convglu_bwd(u, dact, conv_w8, conv_b, tm=64):
    s, w = u.shape
    tb = tm // 8
    nb = s // tm

    def fn(ub, up, un, db, dn, cw, cb):
        i = pl.program_id(0)
        up = jnp.where(i == 0, 0.0, up)
        dn = jnp.where(i == nb - 1, 0.0, dn)
        ne = tm + 8
        m2, m1, cur = _conv_taps(jnp.concatenate([up, ub, un], axis=0), ne, None)
        acc = cb + cw[0:1] * m2 + cw[1:2] * m1 + cw[2:3] * cur
        de = jnp.concatenate([db, dn], axis=0)
        gl, dgl = _gelu_parts(acc[:, :D_FF])
        dacc = jnp.concatenate([de * acc[:, D_FF:] * dgl, de * gl], axis=1)
        p1 = pltpu.roll(dacc, ne - 1, 0)[:tm]
        p2 = pltpu.roll(dacc, ne - 2, 0)[:tm]
        d0 = dacc[:tm]
        du = cw[2:3] * d0 + cw[1:2] * p1 + cw[0:1] * p2
        zero5 = jnp.zeros((5, w), F32)
        dcw = jnp.concatenate([
            jnp.sum(d0 * m2[:tm], axis=0, keepdims=True), jnp.sum(d0 * m1[:tm], axis=0, keepdims=True),
            jnp.sum(d0 * cur[:tm], axis=0, keepdims=True), zero5], axis=0)
        return (du,), (dcw, jnp.sum(d0, axis=0, keepdims=True))

    ins = [
        _rb(u, tm, w),
        (u, (8, w), lambda i: (jnp.maximum(i * tb - 1, 0), 0)),
        (u, (8, w), lambda i: (jnp.minimum((i + 1) * tb, s // 8 - 1), 0)),
        _rb(dact, tm, D_FF),
        (dact, (8, D_FF), lambda i: (jnp.minimum((i + 1) * tb, s // 8 - 1), 0)),
        _cst(conv_w8), _cst(conv_b),
    ]
    return _rows("convglu_bwd", fn, s, tm, ins, [(w, BF16)], [(8, w), (1, w)])


def _chunk_scan(x, t_iota, reverse):
    k = 1
    while k < HG_C:
        if reverse:
            x = x + jnp.where(t_iota < HG_C - k, pltpu.roll(x, HG_C - k, 0), 0.0)
        else:
            x = x + jnp.where(t_iota >= k, pltpu.roll(x, k, 0), 0.0)
        k *= 2
    return x


def _hg_gates(hq, hf, lb):
    sq = _sigmoid(hq)
    q = hq * sq
    sg = _sigmoid(hf)
    f = lb + (1.0 - lb) * sg
    return q, sq, sg, f, 1.0 - f, jnp.log(f)


def _lb_of(logits):
    l0, l1 = logits[0:1], logits[1:2]
    mx = jnp.maximum(l0, l1)
    e0, e1 = jnp.exp(l0 - mx), jnp.exp(l1 - mx)
    return e0 / (e0 + e1)


def _tri(n, lower):
    r = lax.broadcasted_iota(jnp.int32, (n, n), 0)
    c = lax.broadcasted_iota(jnp.int32, (n, n), 1)
    return jnp.where((r >= c) if lower else (r <= c), 1.0, 0.0).astype(BF16)


def _hg_intra_terms(q, kk, b, t_iota):
    ws, ps = [], []
    for s in range(HG_C):
        p = jnp.where(t_iota >= s, jnp.exp(b - b[s:s + 1]), 0.0)
        ps.append(p)
        ws.append(q * kk[s:s + 1] * p)
    return jnp.concatenate(ws, axis=0), ps


def _hg_fwd(proj, lb_logits):
    s = proj.shape[0]
    nt = s // HG_T
    nc = HG_T // HG_C

    def body(q_ref, f_ref, i_ref, l_ref, o_ref, st_ref, state):
        @pl.when(pl.program_id(1) == 0)
        def _():
            state[...] = jnp.zeros_like(state)

        st_ref[0, 0] = state[...]
        lb = _lb_of(l_ref[...])
        ones = jnp.ones((HG_DK, HG_DK), BF16)
        t_iota = lax.broadcasted_iota(jnp.int32, (HG_C, HG_DK), 0)
        cc = HG_C * HG_C

        def group(gi, st):
            units = []
            for u in range(HG_UNROLL):
                r = pl.ds(pl.multiple_of((gi * HG_UNROLL + u) * HG_C, HG_C), HG_C)
                q, _, _, _, kk, g = _hg_gates(q_ref[r, :], f_ref[r, :], lb)
                b = _chunk_scan(g, t_iota, False)
                b_end = b[HG_C - 1:HG_C]
                w_all, _ = _hg_intra_terms(q, kk, b, t_iota)
                units.append((r, i_ref[r, :], q * jnp.exp(b), jnp.exp(b_end), kk * jnp.exp(b_end - b), w_all))
            a_all = _dot(jnp.concatenate([un[5] for un in units], axis=0), ones, NN)
            kvs = [_dot(v, kd, TN) for (_, v, _, _, kd, _) in units]
            sts = [st]
            for (_, _, _, dec, _, _), kv in zip(units, kvs):
                sts.append(sts[-1] * dec + kv)
            for ui, (r, v, qd, _, _, _) in enumerate(units):
                o = _dot(qd, sts[ui], NT)
                for si in range(HG_C):
                    o = o + a_all[ui * cc + si * HG_C:ui * cc + (si + 1) * HG_C] * v[si:si + 1]
                o_ref[r, :] = o
            return sts[-1]

        state[...] = lax.fori_loop(0, nc // HG_UNROLL, group, state[...])

    col = lambda off: pl.BlockSpec((HG_T, HG_DK), lambda h, t: (t, off + h))
    return pl.pallas_call(
        body, name="hgrn2_fwd", grid=(HG_H, nt),
        in_specs=[col(0), col(8), col(16), pl.BlockSpec((2, HG_DK), lambda h, t: (0, h))],
        out_specs=[pl.BlockSpec((HG_T, HG_DK), lambda h, t: (t, h)),
                   pl.BlockSpec((1, 1, HG_DK, HG_DK), lambda h, t: (h, t, 0, 0))],
        out_shape=[jax.ShapeDtypeStruct((s, D), F32), jax.ShapeDtypeStruct((HG_H, nt, HG_DK, HG_DK), F32)],
        scratch_shapes=[pltpu.VMEM((HG_DK, HG_DK), F32)],
        compiler_params=_cparams(("arbitrary", "arbitrary")),
    )(proj, proj, proj, lb_logits)


def _hg_bwd(proj, lb_logits, states, do_raw):
    s = proj.shape[0]
    nt = s // HG_T
    nc = HG_T // HG_C

    def body(q_ref, f_ref, i_ref, l_ref, st_ref, do_ref, dq_ref, df_ref, di_ref, dl_ref, st_all, adj):
        tb = pl.program_id(1)

        @pl.when(tb == 0)
        def _():
            adj[...] = jnp.zeros_like(adj)
            dl_ref[...] = jnp.zeros_like(dl_ref)

        lb = _lb_of(l_ref[...])
        ones = jnp.ones((HG_DK, HG_DK), BF16)
        t_iota = lax.broadcasted_iota(jnp.int32, (HG_C, HG_DK), 0)
        cc = HG_C * HG_C

        def fwd_group(gi, st):
            terms = []
            for u in range(HG_UNROLL):
                ci = gi * HG_UNROLL + u
                r = pl.ds(pl.multiple_of(ci * HG_C, HG_C), HG_C)
                _, _, _, _, kk, g = _hg_gates(q_ref[r, :], f_ref[r, :], lb)
                b = _chunk_scan(g, t_iota, False)
                b_end = b[HG_C - 1:HG_C]
                terms.append((ci, jnp.exp(b_end), _dot(i_ref[r, :], kk * jnp.exp(b_end - b), TN)))
            for ci, dec, kv in terms:
                st_all[ci] = st
                st = st * dec + kv
            return st

        lax.fori_loop(0, nc // HG_UNROLL, fwd_group, st_ref[0, 0])

        def bwd_group(gj, dlb):
            units = []
            for u in range(HG_UNROLL_BWD):
                ci = nc - 1 - (gj * HG_UNROLL_BWD + u)
                r = pl.ds(pl.multiple_of(ci * HG_C, HG_C), HG_C)
                hq, hf, v, do = q_ref[r, :], f_ref[r, :], i_ref[r, :], do_ref[r, :]
                q, sq, sg, f, kk, g = _hg_gates(hq, hf, lb)
                b = _chunk_scan(g, t_iota, False)
                b_end = b[HG_C - 1:HG_C]
                e_b, e_be, dec = jnp.exp(b), jnp.exp(b_end - b), jnp.exp(b_end)
                w_all, ps = _hg_intra_terms(q, kk, b, t_iota)
                x_all = jnp.concatenate([do * v[si:si + 1] for si in range(HG_C)], axis=0)
                units.append(dict(ci=ci, r=r, hq=hq, v=v, do=do, q=q, sq=sq, sg=sg, f=f, kk=kk, e_b=e_b, e_be=e_be,
                                  dec=dec, kd=kk * e_be, w=w_all, ps=ps, x=x_all))
            both = _dot(jnp.concatenate([un["w"] for un in units] + [un["x"] for un in units], axis=0), ones, NN)
            st0s = [st_all[un["ci"]] for un in units]
            st_ends = [st0 * un["dec"] + _dot(un["v"], un["kd"], TN) for un, st0 in zip(units, st0s)]
            dqks = [_dot(un["do"], un["q"] * un["e_b"], TN) for un in units]
            es = [adj[...]]
            for un, dqk in zip(units, dqks):
                es.append(es[-1] * un["dec"] + dqk)
            adj[...] = es[-1]
            for ui, un in enumerate(units):
                e, q, kk, v, do = es[ui], un["q"], un["kk"], un["v"], un["do"]
                tail = jnp.sum(e * st_ends[ui], axis=0, keepdims=True)
                dq = un["e_b"] * _dot(do, st0s[ui], NN)
                dk = un["e_be"] * _dot(v, e, NN)
                dv = _dot(un["kd"], e, NT)
                a0 = ui * cc
                d0 = (HG_UNROLL_BWD + ui) * cc
                for si in range(HG_C):
                    da = both[d0 + si * HG_C:d0 + (si + 1) * HG_C]
                    aa = both[a0 + si * HG_C:a0 + (si + 1) * HG_C]
                    dap = da * un["ps"][si]
                    dq = dq + dap * kk[si:si + 1]
                    hit = t_iota == si
                    dk = dk + jnp.where(hit, jnp.sum(dap * q, axis=0, keepdims=True), 0.0)
                    dv = dv + jnp.where(hit, jnp.sum(aa * do, axis=0, keepdims=True), 0.0)
                dg = _chunk_scan(q * dq - kk * dk, t_iota, True) + tail
                dfg = dg / un["f"] - dk
                sq, sg, hq, r = un["sq"], un["sg"], un["hq"], un["r"]
                dq_ref[r, :] = (dq * sq * (1.0 + hq * (1.0 - sq))).astype(dq_ref.dtype)
                df_ref[r, :] = (dfg * (1.0 - lb) * sg * (1.0 - sg)).astype(df_ref.dtype)
                di_ref[r, :] = dv.astype(di_ref.dtype)
                dlb = dlb + jnp.sum(dfg * (1.0 - sg), axis=0, keepdims=True)
            return dlb

        dlb = lax.fori_loop(0, nc // HG_UNROLL_BWD, bwd_group, jnp.zeros((1, HG_DK), F32))
        dl0 = dlb * lb * (1.0 - lb)
        dl_ref[...] += jnp.concatenate([dl0, -dl0], axis=0)

    col = lambda off: pl.BlockSpec((HG_T, HG_DK), lambda h, t: (nt - 1 - t, off + h))
    out_col = pl.BlockSpec((HG_T, HG_DK), lambda h, t: (nt - 1 - t, h))
    return pl.pallas_call(
        body, name="hgrn2_bwd", grid=(HG_H, nt),
        in_specs=[col(0), col(8), col(16), pl.BlockSpec((2, HG_DK), lambda h, t: (0, h)),
                  pl.BlockSpec((1, 1, HG_DK, HG_DK), lambda h, t: (h, nt - 1 - t, 0, 0)), col(0)],
        out_specs=[out_col, out_col, out_col, pl.BlockSpec((2, HG_DK), lambda h, t: (0, h))],
        out_shape=[jax.ShapeDtypeStruct((s, D), BF16)] * 3 + [jax.ShapeDtypeStruct((2, D), F32)],
        scratch_shapes=[pltpu.VMEM((nc, HG_DK, HG_DK), F32), pltpu.VMEM((HG_DK, HG_DK), F32)],
        compiler_params=_cparams(("arbitrary", "arbitrary")),
    )(proj, proj, proj, lb_logits, states, do_raw)


def _hg_post_fwd(o_raw, proj, gnorm, tm=256):
    s = o_raw.shape[0]

    def fn(o, hg, gn):
        outs = []
        for h in range(HG_H):
            sl = slice(h * HG_DK, (h + 1) * HG_DK)
            oh, gh = o[:, sl], hg[:, sl]
            r = lax.rsqrt(jnp.mean(oh * oh, axis=-1, keepdims=True) + EPS)
            outs.append(oh * r * gn * (gh * _sigmoid(gh)))
        return (jnp.concatenate(outs, axis=1),), ()

    return _rows("hgrn2_out_fwd", fn, s, tm, [_rb(o_raw, tm, D), _rb(proj, tm, D, 3), _cst(gnorm)], [(D, BF16)])[0]


def _hg_post_bwd(do_a, o_raw, proj, gnorm, tm=256):
    s = o_raw.shape[0]

    def fn(da, o, hg, gn):
        dos, dhgs = [], []
        dgn = jnp.zeros((1, HG_DK), F32)
        for h in range(HG_H):
            sl = slice(h * HG_DK, (h + 1) * HG_DK)
            oh, gh, dh = o[:, sl], hg[:, sl], da[:, sl]
            r = lax.rsqrt(jnp.mean(oh * oh, axis=-1, keepdims=True) + EPS)
            xhat = oh * r
            sg = _sigmoid(gh)
            dy = dh * (gh * sg)
            dhgs.append(dh * xhat * gn * sg * (1.0 + gh * (1.0 - sg)))
            dgn = dgn + jnp.sum(dy * xhat, axis=0, keepdims=True)
            dxh = dy * gn
            dos.append(r * (dxh - xhat * jnp.mean(dxh * xhat, axis=-1, keepdims=True)))
        return (jnp.concatenate(dos, axis=1), jnp.concatenate(dhgs, axis=1)), (dgn,)

    ins = [_rb(do_a, tm, D), _rb(o_raw, tm, D), _rb(proj, tm, D, 3), _cst(gnorm)]
    return _rows("hgrn2_out_bwd", fn, s, tm, ins, [(D, F32), (D, BF16)], [(1, HG_DK)])


def _log_sigmoid(z):
    return jnp.minimum(z, 0.0) - jnp.log(1.0 + jnp.exp(-jnp.abs(z)))


def _fox_gate_bwd(dct, pff, bias, tm=256):
    s = pff.shape[0]
    nb = s // tm

    def body(d_ref, p_ref, b_ref, dff_ref, db_ref, carry):
        @pl.when(pl.program_id(0) == 0)
        def _():
            carry[...] = jnp.zeros_like(carry)
            db_ref[...] = jnp.zeros_like(db_ref)

        dc = d_ref[...].T
        dlf = _split_dot(_tri(tm, False), dc, 3) + carry[0:1]
        carry[...] = jnp.broadcast_to(dlf[0:1], carry.shape)
        dff = dlf * _sigmoid(-(p_ref[...] + b_ref[...]))
        dff_ref[...] = dff
        db_ref[...] += jnp.sum(dff, axis=0, keepdims=True)

    return pl.pallas_call(
        body, name="fox_gate_bwd", grid=(nb,),
        in_specs=[pl.BlockSpec((LANES, tm), lambda i: (0, nb - 1 - i)),
                  pl.BlockSpec((tm, LANES), lambda i: (nb - 1 - i, 0)), pl.BlockSpec((1, LANES), lambda i: (0, 0))],
        out_specs=[pl.BlockSpec((tm, LANES), lambda i: (nb - 1 - i, 0)), pl.BlockSpec((1, LANES), lambda i: (0, 0))],
        out_shape=[jax.ShapeDtypeStruct((s, LANES), F32), jax.ShapeDtypeStruct((1, LANES), F32)],
        scratch_shapes=[pltpu.VMEM((8, LANES), F32)],
        compiler_params=_cparams(("arbitrary",)),
    )(dct, pff, bias)


def _diag_mask(t):
    r = lax.broadcasted_iota(jnp.int32, (t, t), 0)
    c = lax.broadcasted_iota(jnp.int32, (t, t), 1)
    return r >= c


AUX_ONES = 6


def _pieces(x):
    h = x.astype(BF16)
    r = x - h.astype(F32)
    m = r.astype(BF16)
    return h, m, (r - m.astype(F32)).astype(BF16)


def _lane_put(lane, cols, base):
    out = None
    for i, col in enumerate(cols):
        term = jnp.where(lane == base + i, col.astype(F32), 0.0)
        out = term if out is None else out + term
    return out


def _fox_prep2(proj, pff, bias, tm=256):
    s = pff.shape[0]

    def body(q_ref, k_ref, v_ref, p_ref, b_ref, qb_ref, kb_ref, vb_ref, ka_ref, carry):
        @pl.when(pl.program_id(0) == 0)
        def _():
            carry[...] = jnp.zeros_like(carry)

        qb_ref[...] = (q_ref[...] * 0.125).astype(BF16)
        kb_ref[...] = k_ref[...].astype(BF16)
        vb_ref[...] = v_ref[...].astype(BF16)
        lf = _log_sigmoid(p_ref[...] + b_ref[...])
        c = _split_dot(_tri(tm, True), lf, 3) + carry[0:1]
        carry[...] = jnp.broadcast_to(c[tm - 1:tm], carry.shape)
        lane = lax.broadcasted_iota(jnp.int32, (tm, LANES), 1)
        ones = jnp.where((lane >= AUX_ONES) & (lane < AUX_ONES + 6), 1.0, 0.0)
        for p in range(FOX_H // 2):
            aux = ones
            for z in range(2):
                col = jnp.sum(jnp.where(lane == 2 * p + z, c, 0.0), axis=1, keepdims=True)
                aux = aux + _lane_put(lane, _pieces(-col), 3 * z)
            ka_ref[:, p * LANES:(p + 1) * LANES] = aux.astype(BF16)

    row = lambda cb: pl.BlockSpec((tm, D), lambda i: (i, cb))
    return pl.pallas_call(
        body, name="fox_prep", grid=(s // tm,),
        in_specs=[row(4), row(5), row(6), pl.BlockSpec((tm, LANES), lambda i: (i, 0)),
                  pl.BlockSpec((1, LANES), lambda i: (0, 0))],
        out_specs=[row(0)] * 4, out_shape=[jax.ShapeDtypeStruct((s, D), BF16)] * 4,
        scratch_shapes=[pltpu.VMEM((8, LANES), F32)],
        compiler_params=_cparams(("arbitrary",)),
    )(proj, proj, proj, pff, bias)


def _fox_fwd2(qb, kb, vb, ka):
    s = qb.shape[0]
    t = min(FOX_T, s)
    nq = s // t

    def body(q_ref, k_ref, v_ref, ka_ref, o_ref, la_ref):
        i = pl.program_id(1)
        lane = lax.broadcasted_iota(jnp.int32, (t, LANES), 1)
        in_a = lane < FOX_D
        q = q_ref[...]
        zero = jnp.zeros_like(q)
        qh = [jnp.where(in_a, q, zero), jnp.where(in_a, zero, q)]
        c_ones = [jnp.where((lane >= 3 * z) & (lane < 3 * z + 3), 1.0, 0.0) for z in range(2)]
        dmask = _diag_mask(t)

        def keys(j):
            rows = pl.ds(pl.multiple_of(j * t, t), t)
            return jnp.concatenate([k_ref[rows, :], ka_ref[rows, :]], axis=1), rows

        def logits(qx, kk, masked):
            e = lax.dot_general(qx, kk, (NT, ((), ())), preferred_element_type=F32)
            return jnp.where(dmask, e, -1e30) if masked else e

        qc = [jnp.concatenate([qh[z], c_ones[z].astype(BF16)], axis=1) for z in range(2)]

        def step(j, carry, masked):
            kk, rows = keys(j)
            vj = v_ref[rows, :]
            scores = [logits(qc[z], kk, masked) for z in range(2)]
            one = jnp.ones_like(vj)
            vh = [jnp.where(in_a, vj, one), jnp.where(in_a, one, vj)]
            out = []
            for z in range(2):
                m, acc = carry[z]
                m_new = jnp.maximum(m, jnp.max(scores[z], axis=1, keepdims=True))
                p = jnp.exp(scores[z] - m_new)
                out.append((m_new, jnp.exp(m - m_new) * acc + _dot(p, vh[z], NN)))
            return tuple(out)

        init = tuple((jnp.full((t, 1), -1e30, F32), jnp.zeros((t, LANES), F32)) for _ in range(2))
        (ma, acc_a), (mb, acc_b) = step(i, lax.fori_loop(0, i, lambda j, c: step(j, c, False), init), True)
        la = jnp.sum(jnp.where(lane == FOX_D, acc_a, 0.0), axis=1, keepdims=True)
        lb = jnp.sum(jnp.where(lane == 0, acc_b, 0.0), axis=1, keepdims=True)
        o_ref[...] = jnp.where(in_a, acc_a / la, acc_b / lb).astype(o_ref.dtype)
        la_ref[...] = (_lane_put(lane, _pieces(-(ma + jnp.log(la))), AUX_ONES)
                       + _lane_put(lane, _pieces(-(mb + jnp.log(lb))), AUX_ONES + 3)).astype(la_ref.dtype)

    blk = pl.BlockSpec((t, LANES), lambda p, i: (i, p))
    whole = pl.BlockSpec((s, LANES), lambda p, i: (0, p))
    return pl.pallas_call(
        body, name="fox_attn_fwd", grid=(FOX_H // 2, nq), in_specs=[blk, whole, whole, whole],
        out_specs=[blk, blk], out_shape=[jax.ShapeDtypeStruct((s, D), BF16)] * 2,
        compiler_params=_cparams(("arbitrary", "arbitrary")),
    )(qb, kb, vb, ka)


def _fox_bwd2(qb, kb, vb, ka, ob, laux, dob):
    s = qb.shape[0]
    t = min(FOX_T, s)
    nq = s // t

    def body(q_ref, k_ref, v_ref, ka_ref, o_ref, la_ref, do_ref, dq_ref, dk_ref, dv_ref, dc_ref):
        i = pl.program_id(1)

        @pl.when(i == 0)
        def _():
            dk_ref[...] = jnp.zeros_like(dk_ref)
            dv_ref[...] = jnp.zeros_like(dv_ref)
            dc_ref[...] = jnp.zeros_like(dc_ref)

        lane = lax.broadcasted_iota(jnp.int32, (t, LANES), 1)
        in_a = lane < FOX_D
        q, do, la = q_ref[...], do_ref[...], la_ref[...].astype(F32)
        zero = jnp.zeros_like(q)
        qh = [jnp.where(in_a, q, zero), jnp.where(in_a, zero, q)]
        doh = [jnp.where(in_a, do, zero), jnp.where(in_a, zero, do)]
        prod = do.astype(F32) * o_ref[...].astype(F32)
        qx, dox = [], []
        for z in range(2):
            delta = jnp.sum(jnp.where(in_a if z == 0 else ~in_a, prod, 0.0), axis=1, keepdims=True)
            c_ones = jnp.where((lane >= 3 * z) & (lane < 3 * z + 3), 1.0, 0.0)
            lse_lanes = (lane >= AUX_ONES + 3 * z) & (lane < AUX_ONES + 3 * z + 3)
            qx.append(jnp.concatenate([qh[z], (c_ones + jnp.where(lse_lanes, la, 0.0)).astype(BF16)], axis=1))
            dox.append(jnp.concatenate([doh[z], _lane_put(lane, _pieces(-delta), 3 * z).astype(BF16)], axis=1))
        v_ones = jnp.where(lane < 6, 1.0, 0.0).astype(BF16)
        dmask = _diag_mask(t)

        def step(j, carry, masked):
            rows = pl.ds(pl.multiple_of(j * t, t), t)
            kj, vj = k_ref[rows, :], v_ref[rows, :]
            kk = jnp.concatenate([kj, ka_ref[rows, :]], axis=1)
            vv = jnp.concatenate([vj, v_ones], axis=1)
            out = []
            dk_add, dv_add = None, None
            for z in range(2):
                dq, rsum = carry[z]
                e = lax.dot_general(qx[z], kk, (NT, ((), ())), preferred_element_type=F32)
                if masked:
                    e = jnp.where(dmask, e, -1e30)
                p = jnp.exp(e)
                ds = p * lax.dot_general(dox[z], vv, (NT, ((), ())), preferred_element_type=F32)
                dkz, dvz = _dot(ds, qh[z], TN), _dot(p, doh[z], TN)
                dk_add = dkz if dk_add is None else dk_add + dkz
                dv_add = dvz if dv_add is None else dv_add + dvz
                dc_ref[0, z, j] += -jnp.sum(ds, axis=0, keepdims=True)
                out.append((dq + _dot(ds, kj, NN), rsum + jnp.sum(ds, axis=1, keepdims=True)))
            dk_ref[rows, :] += dk_add
            dv_ref[rows, :] += dv_add
            return tuple(out)

        init = tuple((jnp.zeros((t, LANES), F32), jnp.zeros((t, 1), F32)) for _ in range(2))
        (dq_a, rs_a), (dq_b, rs_b) = step(i, lax.fori_loop(0, i, lambda j, c: step(j, c, False), init), True)
        for z, rs in enumerate((rs_a, rs_b)):
            dc_ref[0, z, i] += jnp.transpose(jnp.broadcast_to(rs, (t, LANES)))[0:1]
        dq_ref[...] = (jnp.where(in_a, dq_a, dq_b) * 0.125).astype(dq_ref.dtype)

    blk = pl.BlockSpec((t, LANES), lambda p, i: (i, p))
    whole = pl.BlockSpec((s, LANES), lambda p, i: (0, p))
    return pl.pallas_call(
        body, name="fox_attn_bwd", grid=(FOX_H // 2, nq),
        in_specs=[blk, whole, whole, whole, blk, blk, blk],
        out_specs=[blk, whole, whole, pl.BlockSpec((1, 2, nq, 1, t), lambda p, i: (p, 0, 0, 0, 0))],
        out_shape=[jax.ShapeDtypeStruct((s, D), BF16), jax.ShapeDtypeStruct((s, D), F32),
                   jax.ShapeDtypeStruct((s, D), F32), jax.ShapeDtypeStruct((FOX_H // 2, 2, nq, 1, t), F32)],
        compiler_params=_cparams(("arbitrary", "arbitrary")),
    )(qb, kb, vb, ka, ob, laux, dob)


def _adamw(name, w, g, m, v, tm=None):
    rows, width = w.shape
    tm = rows if tm is None else tm
    c1 = 1.0 - ADAM_B1 ** ADAM_STEP
    c2 = 1.0 - ADAM_B2 ** ADAM_STEP

    def fn(wb, gb, mb, vb):
        m_new = ADAM_B1 * mb + (1.0 - ADAM_B1) * gb
        v_new = ADAM_B2 * vb + (1.0 - ADAM_B2) * (gb * gb)
        delta = -ADAM_LR * ((m_new / c1) / (jnp.sqrt(v_new / c2) + ADAM_EPS) + ADAM_WD * wb)
        return (delta, m_new, v_new), ()

    ins = [_rb(a, tm, width) for a in (w, g, m, v)]
    return _rows(name, fn, rows, tm, ins, [(width, F32)] * 3)


def _me():
    return lax.axis_index("x"), lax.axis_index("y"), lax.axis_index("c")


def _all_gather8(name, block):
    m, n = block.shape

    def body(x_ref, out_ref, send_sems, recv_sems, local_sem):
        x, y, c = _me()
        me, sibling = (x, y, c), (x, y, 1 - c)
        chips = [(1 - x, y), (x, 1 - y), (1 - x, 1 - y)]

        def slot(px, py, pc):
            return out_ref.at[4 * px + 2 * py + pc]

        def copy(k, blk, to, src=None):
            return pltpu.make_async_remote_copy(
                src_ref=slot(*blk) if src is None else src, dst_ref=slot(*blk),
                send_sem=send_sems.at[k], recv_sem=recv_sems.at[k], device_id=to, device_id_type=MESH)

        mine = pltpu.make_async_copy(x_ref, slot(*me), local_sem)
        mine.start()
        first = [copy(0, me, sibling, src=x_ref)]
        first += [copy(1 + j, me, (*chip, c), src=x_ref) for j, chip in enumerate(chips)]
        for cp in first:
            cp.start()
        passed = [copy(4 + j, (*chip, c), sibling) for j, chip in enumerate(chips)]
        for j, chip in enumerate(chips):
            copy(1 + j, (*chip, c), me).wait_recv()
            passed[j].start()
        copy(0, sibling, me).wait_recv()
        for j, chip in enumerate(chips):
            copy(4 + j, (*chip, 1 - c), me).wait_recv()
        for cp in first + passed:
            cp.wait_send()
        mine.wait()

    return pl.pallas_call(
        body, name=name, in_specs=[ANY], out_specs=ANY,
        out_shape=jax.ShapeDtypeStruct((8, m, n), block.dtype),
        scratch_shapes=[pltpu.SemaphoreType.DMA((7,)), pltpu.SemaphoreType.DMA((7,)), pltpu.SemaphoreType.DMA],
    )(block)


def _swap_halves(name, g):
    n, _, m, lanes = g.shape

    def body(g_ref, got_ref, mine_ref, send_sems, recv_sems, local_sems):
        x, y, c = _me()
        copies = []
        for j in range(n):
            copies.append(pltpu.make_async_remote_copy(
                src_ref=g_ref.at[j, 1 - c], dst_ref=got_ref.at[j], send_sem=send_sems.at[j], recv_sem=recv_sems.at[j],
                device_id=(x, y, 1 - c), device_id_type=MESH))
            copies.append(pltpu.make_async_copy(g_ref.at[j, c], mine_ref.at[j], local_sems.at[j]))
        for cp in copies:
            cp.start()
        for cp in copies:
            cp.wait()

    out = jax.ShapeDtypeStruct((n, m, lanes), g.dtype)
    return pl.pallas_call(
        body, name=name, in_specs=[ANY], out_specs=[ANY, ANY], out_shape=[out, out],
        scratch_shapes=[pltpu.SemaphoreType.DMA((n,)), pltpu.SemaphoreType.DMA((n,)), pltpu.SemaphoreType.DMA((n,))],
    )(g)


def _share_half(name, mine):
    m, lanes = mine.shape

    def body(m_ref, out_ref, send_sem, recv_sem, local_sem):
        x, y, c = _me()
        sibling = (x, y, 1 - c)
        local = pltpu.make_async_copy(m_ref, out_ref.at[c], local_sem)
        push = pltpu.make_async_remote_copy(src_ref=m_ref, dst_ref=out_ref.at[c], send_sem=send_sem, recv_sem=recv_sem,
                                            device_id=sibling, device_id_type=MESH)
        local.start()
        push.start()
        pltpu.make_async_remote_copy(src_ref=m_ref, dst_ref=out_ref.at[1 - c], send_sem=send_sem, recv_sem=recv_sem,
                                     device_id=sibling, device_id_type=MESH).wait_recv()
        push.wait_send()
        local.wait()

    return pl.pallas_call(
        body, name=name, in_specs=[ANY], out_specs=ANY, out_shape=jax.ShapeDtypeStruct((2, m, lanes), mine.dtype),
        scratch_shapes=[pltpu.SemaphoreType.DMA, pltpu.SemaphoreType.DMA, pltpu.SemaphoreType.DMA],
    )(mine)


def _chip_exchange(name, p):
    def body(p_ref, out_ref, send_sems, recv_sems, local_sem):
        x, y, c = _me()
        my_chip = 2 * x + y
        chips = [(1 - x, y), (x, 1 - y), (1 - x, 1 - y)]
        mine = pltpu.make_async_copy(p_ref.at[my_chip], out_ref.at[my_chip], local_sem)
        mine.start()
        sends = []
        for k, (px, py) in enumerate(chips):
            sends.append(pltpu.make_async_remote_copy(
                src_ref=p_ref.at[2 * px + py], dst_ref=out_ref.at[my_chip], send_sem=send_sems.at[k],
                recv_sem=recv_sems.at[k], device_id=(px, py, c), device_id_type=MESH))
        for cp in sends:
            cp.start()
        for k, (px, py) in enumerate(chips):
            pltpu.make_async_remote_copy(
                src_ref=p_ref.at[my_chip], dst_ref=out_ref.at[2 * px + py], send_sem=send_sems.at[k],
                recv_sem=recv_sems.at[k], device_id=(px, py, c), device_id_type=MESH).wait_recv()
        for cp in sends:
            cp.wait_send()
        mine.wait()

    return pl.pallas_call(
        body, name=name, in_specs=[ANY], out_specs=ANY, out_shape=jax.ShapeDtypeStruct(p.shape, p.dtype),
        scratch_shapes=[pltpu.SemaphoreType.DMA((3,)), pltpu.SemaphoreType.DMA((3,)), pltpu.SemaphoreType.DMA],
    )(p)


def _all_reduce_small(name, block):
    r, n = block.shape

    def body(x_ref, sum_ref, gath, send_sems, recv_sems):
        x, y, c = _me()
        me = 4 * x + 2 * y + c
        gath[me] = x_ref[...]
        sends = []
        for k in range(1, 8):
            px = x ^ ((k >> 2) & 1)
            py = y ^ ((k >> 1) & 1)
            pc = c ^ (k & 1)
            sends.append(pltpu.make_async_remote_copy(
                src_ref=x_ref, dst_ref=gath.at[me], send_sem=send_sems.at[k - 1], recv_sem=recv_sems.at[k - 1],
                device_id=(px, py, pc), device_id_type=MESH))
        for cp in sends:
            cp.start()
        for k in range(1, 8):
            peer = me ^ k
            pltpu.make_async_remote_copy(
                src_ref=x_ref, dst_ref=gath.at[peer], send_sem=send_sems.at[k - 1], recv_sem=recv_sems.at[k - 1],
                device_id=(x, y, c), device_id_type=MESH).wait_recv()
        for cp in sends:
            cp.wait_send()
        acc = gath[0]
        for d in range(1, 8):
            acc = acc + gath[d]
        sum_ref[...] = acc

    vm = pl.BlockSpec(memory_space=pltpu.VMEM)
    return pl.pallas_call(
        body, name=name, in_specs=[vm], out_specs=vm, out_shape=jax.ShapeDtypeStruct((r, n), F32),
        scratch_shapes=[pltpu.VMEM((8, r, n), F32), pltpu.SemaphoreType.DMA((7,)), pltpu.SemaphoreType.DMA((7,))],
    )(block)


def _add2(name, a, b, tm):
    rows = a.shape[0]
    return _rows(name, lambda p, q: ((p + q,), ()), rows, tm, [_rb(a, tm, LANES), _rb(b, tm, LANES)], [(LANES, BF16)])[0]


def _add4(name, p, tm):
    m = p.shape[1]
    flat = p.reshape(4 * m, LANES)
    nb = m // tm
    ins = [(flat, (tm, LANES), (lambda i, j=j: (j * nb + i, 0))) for j in range(4)]
    f32 = lambda v: v.astype(F32)
    return _rows(name, lambda a, b, c, d: ((((f32(a) + f32(b)) + f32(c)) + f32(d),), ()), m, tm, ins, [(LANES, F32)])[0]


SEG_ROWS = (D * W_IN_SHARD // LANES, 256 * D // LANES, 256 * D // LANES, 256 * D // LANES,
            D * W_UP_SHARD // LANES, W_DOWN_SHARD * D // LANES)
GRAD_ROWS = sum(SEG_ROWS)
CONVW_ROWS = 3 * W_UP_SHARD * 2 // LANES
GATHER_ROWS = 41600


def _flat(a):
    return a.reshape(-1, LANES)


def _gather_weights(w_in, w_a, w_b, w_out, w_up, w_down, conv_w):
    c = lax.axis_index("c")
    bits = lax.bitcast_convert_type(conv_w, BF16)
    pieces = [_flat(t.astype(BF16)) for t in (w_in, w_a, w_b, w_out, w_up, w_down)] + [_flat(bits)]
    pad = GATHER_ROWS - GRAD_ROWS - CONVW_ROWS
    shard = jnp.concatenate(pieces + [jnp.zeros((pad, LANES), BF16)], axis=0)
    half = GATHER_ROWS // 2
    mine = lax.dynamic_slice_in_dim(shard, c * half, half, axis=0)
    full = _all_gather8("all_gather_weights", mine).reshape(N_CHIP, GATHER_ROWS, LANES)
    offs = [0]
    for r in SEG_ROWS:
        offs.append(offs[-1] + r)
    seg = lambda i: full[:, offs[i]:offs[i + 1]]
    wi = seg(0).reshape(N_CHIP, D, W_IN_SHARD).transpose(1, 0, 2).reshape(D, N_CHIP * W_IN_SHARD)
    w_main = jnp.concatenate([wi[:, :FF_COL], wi[:, FF_COL + FOX_H:]], axis=1)
    w_ff = jnp.pad(wi[:, FF_COL:FF_COL + FOX_H], ((0, 0), (0, LANES - FOX_H)))
    wa, wb, wo = (seg(i).reshape(D, D) for i in (1, 2, 3))
    wu = seg(4).reshape(N_CHIP, D, W_UP_SHARD).transpose(1, 0, 2).reshape(D, 2 * D_FF)
    wd = seg(5).reshape(D_FF, D)
    cw_bits = full[:, GRAD_ROWS:GRAD_ROWS + CONVW_ROWS].reshape(N_CHIP, 3, W_UP_SHARD, 2)
    cw = lax.bitcast_convert_type(cw_bits, F32).transpose(1, 0, 2).reshape(3, 2 * D_FF)
    return w_main, w_ff, wa, wb, wo, wu, wd, cw


def _reduce_scatter_grads(d_main, d_ff, d_a, d_b, d_o, d_u, d_d):
    c = lax.axis_index("c")
    d_in = jnp.concatenate(d_main[:7] + [d_ff[:, :FOX_H]] + d_main[7:], axis=1)
    per_chip = [
        d_in.reshape(D, N_CHIP, W_IN_SHARD).transpose(1, 0, 2).reshape(N_CHIP, -1, LANES),
        d_a.reshape(N_CHIP, -1, LANES), d_b.reshape(N_CHIP, -1, LANES), d_o.reshape(N_CHIP, -1, LANES),
        d_u.reshape(D, N_CHIP, W_UP_SHARD).transpose(1, 0, 2).reshape(N_CHIP, -1, LANES),
        d_d.reshape(N_CHIP, -1, LANES),
        jnp.zeros((N_CHIP, GATHER_ROWS - GRAD_ROWS, LANES), F32),
    ]
    half = GATHER_ROWS // 2
    g = jnp.concatenate(per_chip, axis=1).reshape(N_CHIP, 2, half, LANES)
    from_sibling, mine = _swap_halves("grad_swap_halves", g)
    tm = half // 5
    chip_sum = _add2("grad_chip_sum", mine.reshape(-1, LANES), from_sibling.reshape(-1, LANES), tm)
    pieces = _chip_exchange("grad_chip_exchange", chip_sum.reshape(N_CHIP, half, LANES))
    mine_half = _add4("grad_sum_chips", pieces, tm)
    return _share_half("grad_share_half", mine_half).reshape(GATHER_ROWS, LANES)


def _local_step(x, target, norm_mix, fox_f_bias, hg_lb_logits, hg_norm, norm_ffn, conv_b, norm_final,
                w_main, w_ff, wa, wb, wo, wu, wd, conv_w):
    s = x.shape[0]
    bias = jnp.pad(fox_f_bias, ((0, 0), (0, LANES - FOX_H)))
    conv_w8 = jnp.pad(conv_w, ((0, 5), (0, 0)))
    t = min(FOX_T, s)

    n1 = _rms_fwd("norm_mix_fwd", x, norm_mix)
    proj = _mm("in_proj", n1, w_main, "nn", F32, 1024, 1024, D)
    pff = _mm("in_proj_forget", n1, w_ff, "nn", F32, 1024, LANES, D)
    qb, kb, vb, ka = _fox_prep2(proj, pff, bias)
    o_b, laux = _fox_fwd2(qb, kb, vb, ka)
    o_raw, states = _hg_fwd(proj, hg_lb_logits)
    o_a = _hg_post_fwd(o_raw, proj, hg_norm)
    pa = _mm("branch_a", o_a, wa, "nn", F32, 1024, 1024, D)
    pb = _mm("branch_b", o_b, wb, "nn", F32, 1024, 1024, D)
    merged = _merge_fwd(pa, pb, proj)
    h1 = _mm("out_proj", merged, wo, "nn", F32, 1024, 1024, D, res=x)
    n2 = _rms_fwd("norm_ffn_fwd", h1, norm_ffn)
    u = _mm("ffn_up", n2, wu, "nn", F32, 1024, W_UP_SHARD, D)
    act = _convglu_fwd(u, conv_w8, conv_b)
    h2 = _mm("ffn_down", act, wd, "nn", F32, 512, 1024, D_FF, res=h1)
    (dh2,), (d_norm_final, loss_row) = _final(h2, target, norm_final)

    dact = _mm("ffn_down_dx", dh2, wd, "nt", BF16, 1024, D_FF, D)
    d_wd = _mm("ffn_down_dw", act, dh2, "tn", F32, D_FF // 2, 1024, 512)
    (du,), (d_conv_w8, d_conv_b) = _convglu_bwd(u, dact, conv_w8, conv_b)
    dn2 = _mm("ffn_up_dx", du, wu, "nt", F32, 1024, 1024, W_UP_SHARD)
    d_wu = _mm("ffn_up_dw", n2, du, "tn", F32, 1024, W_UP_SHARD, 512)
    (dh1,), (d_norm_ffn,) = _rms_bwd("norm_ffn_bwd", h1, norm_ffn, [dn2], dh2)

    dmerged = _mm("out_proj_dx", dh1, wo, "nt", F32, 1024, 1024, D)
    d_wo = _mm("out_proj_dw", merged, dh1, "tn", F32, 1024, 1024, 512)
    dpa, dpb, dga, dgb = _merge_bwd(dmerged, pa, pb, proj)
    do_a = _mm("branch_a_dx", dpa, wa, "nt", F32, 1024, 1024, D)
    do_b = _mm("branch_b_dx", dpb, wb, "nt", BF16, 1024, 1024, D)
    d_wa = _mm("branch_a_dw", o_a, dpa, "tn", F32, 1024, 1024, 512)
    d_wb = _mm("branch_b_dw", o_b, dpb, "tn", F32, 1024, 1024, 512)

    (do_raw, dhg), (d_hg_norm,) = _hg_post_bwd(do_a, o_raw, proj, hg_norm)
    dhq, dhf, dhi, d_lb_logits = _hg_bwd(proj, hg_lb_logits, states, do_raw)

    dfq, dfk, dfv, dcrow = _fox_bwd2(qb, kb, vb, ka, o_b, laux, do_b)
    dct = jnp.pad(dcrow.reshape(FOX_H, s), ((0, LANES - FOX_H), (0, 0)))
    dff, d_bias = _fox_gate_bwd(dct, pff, bias)

    pieces = [dhq, dhf, dhi, dhg, dfq, dfk, dfv, dga, dgb]
    dn1 = _mm_sum_nt("in_proj_dx", pieces, w_main, (dff, w_ff), 512, 1024)
    d_w_main = [_mm("in_proj_dw_%d" % i, n1, p, "tn", F32, 1024, 1024, 512) for i, p in enumerate(pieces)]
    d_w_ff = _mm("in_proj_forget_dw", n1, dff, "tn", F32, 1024, LANES, 512)
    (dx,), (d_norm_mix,) = _rms_bwd("norm_mix_bwd", x, norm_mix, [dn1], dh1)

    small = dict(norm_mix=d_norm_mix, fox_f_bias=d_bias[:, :FOX_H], hg_lb_logits=d_lb_logits, hg_norm=d_hg_norm,
                 norm_ffn=d_norm_ffn, conv_b=d_conv_b, norm_final=d_norm_final, conv_w=d_conv_w8[:3], loss=loss_row)
    big = (d_w_main, d_w_ff, d_wa, d_wb, d_wo, d_wu, d_wd)
    return dx, small, big


SMALL_KEYS = ("norm_mix", "fox_f_bias", "hg_lb_logits", "hg_norm", "norm_ffn", "conv_b", "norm_final")


def _pack_small(parts):
    rows, layout = [], []
    for key, arr in parts:
        flat = arr.reshape(-1)
        n = flat.shape[0]
        nr = -(-n // LANES)
        rows.append(jnp.pad(flat, (0, nr * LANES - n)).reshape(nr, LANES))
        layout.append((key, arr.shape, n, nr))
    packed = jnp.concatenate(rows, axis=0)
    pad = -packed.shape[0] % 8
    return jnp.pad(packed, ((0, pad), (0, 0))), layout


def _unpack_small(packed, layout):
    out, r0 = {}, 0
    for key, shape, n, nr in layout:
        out[key] = packed[r0:r0 + nr].reshape(-1)[:n].reshape(shape)
        r0 += nr
    return out


def kernel(x, norm_mix, w_in, fox_f_bias, hg_lb_logits, hg_norm, w_branch_a, w_branch_b, w_out, norm_ffn, w_up, conv_w, conv_b, w_down, norm_final, loss_target, m_norm_mix, m_w_in, m_fox_f_bias, m_hg_lb_logits, m_hg_norm, m_w_branch_a, m_w_branch_b, m_w_out, m_norm_ffn, m_w_up, m_conv_w, m_conv_b, m_w_down, m_norm_final, v_norm_mix, v_w_in, v_fox_f_bias, v_hg_lb_logits, v_hg_norm, v_w_branch_a, v_w_branch_b, v_w_out, v_norm_ffn, v_w_up, v_conv_w, v_conv_b, v_w_down, v_norm_final):
    chip = 2 * lax.axis_index("x") + lax.axis_index("y")
    w_main, w_ff, wa, wb, wo, wu, wd, cw = _gather_weights(
        w_in[0], w_branch_a[0], w_branch_b[0], w_out[0], w_up[0], w_down[0], conv_w[0])
    dx, small, big = _local_step(
        x[0], loss_target[0], norm_mix, fox_f_bias, hg_lb_logits, hg_norm, norm_ffn, conv_b,
        norm_final.reshape(1, D), w_main, w_ff, wa, wb, wo, wu, wd, cw)

    packed, layout = _pack_small([(k, small[k]) for k in SMALL_KEYS + ("conv_w", "loss")])
    red = _unpack_small(_all_reduce_small("all_reduce_small", packed), layout)
    loss = red["loss"][0, 0]
    g_conv_w = lax.dynamic_slice_in_dim(red["conv_w"], chip * W_UP_SHARD, W_UP_SHARD, axis=1)

    gflat = _reduce_scatter_grads(*big)
    offs = [0]
    for r in SEG_ROWS:
        offs.append(offs[-1] + r)
    shapes = [(D, W_IN_SHARD), (256, D), (256, D), (256, D), (D, W_UP_SHARD), (W_DOWN_SHARD, D)]
    g_big = [gflat[offs[i]:offs[i + 1]].reshape(shapes[i]) for i in range(6)]

    names = ["norm_mix", "w_in", "fox_f_bias", "hg_lb_logits", "hg_norm", "w_branch_a", "w_branch_b", "w_out",
             "norm_ffn", "w_up", "conv_w", "conv_b", "w_down", "norm_final"]
    weights = dict(norm_mix=norm_mix, w_in=w_in, fox_f_bias=fox_f_bias, hg_lb_logits=hg_lb_logits, hg_norm=hg_norm,
                   w_branch_a=w_branch_a, w_branch_b=w_branch_b, w_out=w_out, norm_ffn=norm_ffn, w_up=w_up,
                   conv_w=conv_w, conv_b=conv_b, w_down=w_down, norm_final=norm_final)
    ms = dict(norm_mix=m_norm_mix, w_in=m_w_in, fox_f_bias=m_fox_f_bias, hg_lb_logits=m_hg_lb_logits,
              hg_norm=m_hg_norm, w_branch_a=m_w_branch_a, w_branch_b=m_w_branch_b, w_out=m_w_out,
              norm_ffn=m_norm_ffn, w_up=m_w_up, conv_w=m_conv_w, conv_b=m_conv_b, w_down=m_w_down,
              norm_final=m_norm_final)
    vs = dict(norm_mix=v_norm_mix, w_in=v_w_in, fox_f_bias=v_fox_f_bias, hg_lb_logits=v_hg_lb_logits,
              hg_norm=v_hg_norm, w_branch_a=v_w_branch_a, w_branch_b=v_w_branch_b, w_out=v_w_out,
              norm_ffn=v_norm_ffn, w_up=v_w_up, conv_w=v_conv_w, conv_b=v_conv_b, w_down=v_w_down,
              norm_final=v_norm_final)

    grads, deltas, new_m, new_v = {}, {}, {}, {}
    big_names = ["w_in", "w_branch_a", "w_branch_b", "w_out", "w_up", "w_down"]
    for name, g2 in zip(big_names, g_big):
        shape = weights[name].shape
        rows = g2.shape[0]
        d_, m_, v_ = _adamw("adamw_" + name, weights[name][0], g2, ms[name][0], vs[name][0], tm=rows // 8)
        grads[name], deltas[name], new_m[name], new_v[name] = (a.reshape(shape) for a in (g2, d_, m_, v_))
    shape = conv_w.shape
    d_, m_, v_ = _adamw("adamw_conv_w", conv_w[0], g_conv_w, m_conv_w[0], v_conv_w[0])
    grads["conv_w"], deltas["conv_w"], new_m["conv_w"], new_v["conv_w"] = (
        a.reshape(shape) for a in (g_conv_w, d_, m_, v_))
    gs = {k: red[k].reshape(weights[k].shape) for k in SMALL_KEYS}
    pw, lay = _pack_small([(k, weights[k]) for k in SMALL_KEYS])
    pg, _ = _pack_small([(k, gs[k]) for k in SMALL_KEYS])
    pm, _ = _pack_small([(k, ms[k]) for k in SMALL_KEYS])
    pv, _ = _pack_small([(k, vs[k]) for k in SMALL_KEYS])
    d_, m_, v_ = (_unpack_small(a, lay) for a in _adamw("adamw_small", pw, pg, pm, pv))
    for k in SMALL_KEYS:
        grads[k], deltas[k], new_m[k], new_v[k] = gs[k], d_[k], m_[k], v_[k]

    return (loss, dx[None], *[grads[n] for n in names], *[deltas[n] for n in names],
            *[new_m[n] for n in names], *[new_v[n] for n in names])
```

```python
import functools

import jax
import jax.numpy as jnp
from jax import lax
from jax.experimental import pallas as pl
from jax.experimental.pallas import tpu as pltpu

F32 = jnp.float32
BF16 = jnp.bfloat16

D = 1024
HG_H, HG_DK = 8, 128
FOX_H, FOX_D = 16, 64
D_FF = 2816
EPS = 1e-6
N_CHIP = 4
LANES = 128
W_IN_SHARD = 2308
W_UP_SHARD = 1408
W_DOWN_SHARD = 704
FF_COL = 7168
ADAM_LR, ADAM_B1, ADAM_B2, ADAM_EPS, ADAM_WD, ADAM_STEP = 0.001, 0.9, 0.999, 1e-08, 0.01, 10

HG_C = 16
HG_T = 256
HG_UNROLL = 4
HG_UNROLL_BWD = 2
FOX_T = 512
VMEM_LIMIT = 56 * 1024 * 1024
MESH = pl.DeviceIdType.MESH
ANY = pl.BlockSpec(memory_space=pl.ANY)


def _cparams(sem):
    return pltpu.CompilerParams(dimension_semantics=sem, vmem_limit_bytes=VMEM_LIMIT)


def _sigmoid(x):
    return 1.0 / (1.0 + jnp.exp(-x))


def _dot(a, b, dims):
    return lax.dot_general(a.astype(BF16), b.astype(BF16), (dims, ((), ())), preferred_element_type=F32)


NN = ((1,), (0,))
NT = ((1,), (1,))
TN = ((0,), (0,))


def _split_dot(tri, x, parts, dims=NN):
    acc = None
    r = x
    for _ in range(parts):
        p = r.astype(BF16)
        t = lax.dot_general(tri, p, (dims, ((), ())), preferred_element_type=F32)
        acc = t if acc is None else acc + t
        r = r - p.astype(F32)
    return acc


def _rb(arr, tm, width, cb=0):
    return (arr, (tm, width), lambda i: (i, cb))


def _cst(arr):
    return (arr, arr.shape, lambda i: (0,) * arr.ndim)


def _rows(name, fn, n_rows, tm, ins, outs, accs=(), reverse=False):
    n_in, n_out, n_acc = len(ins), len(outs), len(accs)
    nb = n_rows // tm

    def body(*refs):
        vals = [r[...] for r in refs[:n_in]]
        o, a = fn(*vals)
        for r, v in zip(refs[n_in:n_in + n_out], o):
            r[...] = v.astype(r.dtype)
        if n_acc:
            acc_refs = refs[n_in + n_out:]

            @pl.when(pl.program_id(0) == 0)
            def _():
                for r in acc_refs:
                    r[...] = jnp.zeros_like(r)

            for r, v in zip(acc_refs, a):
                r[...] += v

    if reverse:
        rowmap = lambda i: (nb - 1 - i, 0)
    else:
        rowmap = lambda i: (i, 0)
    in_specs = [pl.BlockSpec(bs, im) for (_, bs, im) in ins]
    out_specs = [pl.BlockSpec((tm, w), rowmap) for (w, _) in outs]
    out_specs += [pl.BlockSpec((r, w), lambda i: (0, 0)) for (r, w) in accs]
    out_shape = [jax.ShapeDtypeStruct((n_rows, w), dt) for (w, dt) in outs]
    out_shape += [jax.ShapeDtypeStruct((r, w), F32) for (r, w) in accs]
    res = pl.pallas_call(
        body, name=name, grid=(nb,), in_specs=in_specs, out_specs=out_specs, out_shape=out_shape,
        compiler_params=_cparams(("arbitrary",)),
    )(*[a for a, _, _ in ins])
    return (res[:n_out], res[n_out:]) if n_acc else res


def _mm(name, a, b, mode, out_dtype, tm, tn, tk, res=None):
    if mode == "nn":
        (m, k), n = a.shape, b.shape[1]
    elif mode == "nt":
        (m, k), n = a.shape, b.shape[0]
    else:
        (k, m), n = a.shape, b.shape[1]
    tm, tn, tk = min(tm, m), min(tn, n), min(tk, k)
    assert m % tm == 0 and n % tn == 0 and k % tk == 0, (name, m, n, k, tm, tn, tk)
    if mode == "nn":
        a_spec = pl.BlockSpec((tm, tk), lambda i, j, kk: (i, kk))
        b_spec = pl.BlockSpec((tk, tn), lambda i, j, kk: (kk, j))
        dims = NN
    elif mode == "nt":
        a_spec = pl.BlockSpec((tm, tk), lambda i, j, kk: (i, kk))
        b_spec = pl.BlockSpec((tn, tk), lambda i, j, kk: (j, kk))
        dims = NT
    else:
        a_spec = pl.BlockSpec((tk, tm), lambda i, j, kk: (kk, i))
        b_spec = pl.BlockSpec((tk, tn), lambda i, j, kk: (kk, j))
        dims = TN
    nk = k // tk
    has_res = res is not None

    def body(*refs):
        a_ref, b_ref = refs[0], refs[1]
        r_ref = refs[2] if has_res else None
        o_ref = refs[3] if has_res else refs[2]
        part = _dot(a_ref[...], b_ref[...], dims)

        def finish(val):
            if has_res:
                val = val + r_ref[...]
            o_ref[...] = val.astype(o_ref.dtype)

        if nk == 1:
            finish(part)
        else:
            acc_ref = refs[-1]
            kk = pl.program_id(2)

            @pl.when(kk == 0)
            def _():
                acc_ref[...] = part

            @pl.when(kk > 0)
            def _():
                acc_ref[...] += part

            @pl.when(kk == nk - 1)
            def _():
                finish(acc_ref[...])

    in_specs = [a_spec, b_spec]
    args = [a, b]
    if has_res:
        in_specs.append(pl.BlockSpec((tm, tn), lambda i, j, kk: (i, j)))
        args.append(res)
    return pl.pallas_call(
        body, name=name, grid=(m // tm, n // tn, nk), in_specs=in_specs,
        out_specs=pl.BlockSpec((tm, tn), lambda i, j, kk: (i, j)),
        out_shape=jax.ShapeDtypeStruct((m, n), out_dtype),
        scratch_shapes=[pltpu.VMEM((tm, tn), F32)] if nk > 1 else [],
        compiler_params=_cparams(("arbitrary", "arbitrary", "arbitrary")),
    )(*args)


def _mm_sum_nt(name, pieces, w, extra, tm, tn):
    n_p = len(pieces)
    m, k = pieces[0].shape
    n = w.shape[0]
    xa, xb = extra
    ke = xa.shape[1]
    tm, tn = min(tm, m), min(tn, n)

    def body(*refs):
        p_refs, w_ref, xa_ref, xb_ref, o_ref, acc_ref = refs[:n_p], refs[n_p], refs[n_p + 1], refs[n_p + 2], refs[-2], refs[-1]
        kk = pl.program_id(2)

        @pl.when(kk == 0)
        def _():
            acc_ref[...] = jnp.zeros_like(acc_ref)

        for i in range(n_p):
            @pl.when(kk == i)
            def _(i=i):
                acc_ref[...] += _dot(p_refs[i][...], w_ref[...], NT)

        @pl.when(kk == n_p)
        def _():
            o_ref[...] = acc_ref[...] + _dot(xa_ref[...], xb_ref[...], NT)

    in_specs = [pl.BlockSpec((tm, k), lambda i, j, kk: (i, 0)) for _ in range(n_p)]
    in_specs.append(pl.BlockSpec((tn, k), lambda i, j, kk: (j, jnp.minimum(kk, n_p - 1))))
    in_specs += [pl.BlockSpec((tm, ke), lambda i, j, kk: (i, 0)), pl.BlockSpec((tn, ke), lambda i, j, kk: (j, 0))]
    return pl.pallas_call(
        body, name=name, grid=(m // tm, n // tn, n_p + 1), in_specs=in_specs,
        out_specs=pl.BlockSpec((tm, tn), lambda i, j, kk: (i, j)),
        out_shape=jax.ShapeDtypeStruct((m, n), F32),
        scratch_shapes=[pltpu.VMEM((tm, tn), F32)],
        compiler_params=_cparams(("arbitrary", "arbitrary", "arbitrary")),
    )(*pieces, w, xa, xb)


def _rms_fwd(name, x, gain, tm=256):
    s = x.shape[0]

    def fn(xb, g):
        r = lax.rsqrt(jnp.mean(xb * xb, axis=-1, keepdims=True) + EPS)
        return (xb * r * g,), ()

    return _rows(name, fn, s, tm, [_rb(x, tm, D), _cst(gain)], [(D, BF16)])[0]


def _rms_bwd(name, x, gain, dns, dres, tm=256):
    s = x.shape[0]
    n_dn = len(dns)

    def fn(xb, g, *rest):
        dn = rest[0]
        for t in rest[1:n_dn]:
            dn = dn + t
        r = lax.rsqrt(jnp.mean(xb * xb, axis=-1, keepdims=True) + EPS)
        xhat = xb * r
        dxh = dn * g
        dx = r * (dxh - xhat * jnp.mean(dxh * xhat, axis=-1, keepdims=True)) + rest[n_dn]
        return (dx,), (jnp.sum(dn * xhat, axis=0, keepdims=True),)

    ins = [_rb(x, tm, D), _cst(gain)] + [_rb(t, tm, D) for t in dns] + [_rb(dres, tm, D)]
    return _rows(name, fn, s, tm, ins, [(D, F32)], [(1, D)])


def _final(h2, target, gain, tm=256):
    s = h2.shape[0]

    def fn(hb, tb, g):
        r = lax.rsqrt(jnp.mean(hb * hb, axis=-1, keepdims=True) + EPS)
        xhat = hb * r
        e = xhat * g - tb
        dy = e * (1.0 / D)
        dxh = dy * g
        dh = r * (dxh - xhat * jnp.mean(dxh * xhat, axis=-1, keepdims=True))
        lrow = 0.5 * jnp.sum(jnp.sum(e * e, axis=-1, keepdims=True) * (1.0 / D), axis=0, keepdims=True)
        return (dh,), (jnp.sum(dy * xhat, axis=0, keepdims=True), jnp.broadcast_to(lrow, (1, LANES)))

    return _rows("final_norm_loss", fn, s, tm, [_rb(h2, tm, D), _rb(target, tm, D), _cst(gain)],
                 [(D, F32)], [(1, D), (1, LANES)])


def _merge_fwd(pa, pb, proj, tm=256):
    s = pa.shape[0]

    def fn(a, b, ga, gb):
        return (_sigmoid(ga) * a + _sigmoid(gb) * b,), ()

    ins = [_rb(pa, tm, D), _rb(pb, tm, D), _rb(proj, tm, D, 7), _rb(proj, tm, D, 8)]
    return _rows("merge_fwd", fn, s, tm, ins, [(D, BF16)])[0]


def _merge_bwd(dmerged, pa, pb, proj, tm=256):
    s = pa.shape[0]

    def fn(dm, a, b, ga, gb):
        sa, sb = _sigmoid(ga), _sigmoid(gb)
        return (dm * sa, dm * sb, dm * a * sa * (1.0 - sa), dm * b * sb * (1.0 - sb)), ()

    ins = [_rb(dmerged, tm, D), _rb(pa, tm, D), _rb(pb, tm, D), _rb(proj, tm, D, 7), _rb(proj, tm, D, 8)]
    return _rows("merge_bwd", fn, s, tm, ins, [(D, BF16), (D, BF16), (D, BF16), (D, BF16)])


def _gelu_parts(x):
    cdf = 0.5 * (1.0 + lax.erf(x * 0.7071067811865476))
    pdf = 0.3989422804014327 * jnp.exp(-0.5 * x * x)
    return x * cdf, cdf + x * pdf


def _conv_taps(u_ext, n_out, first):
    n = u_ext.shape[0]
    cur = u_ext[8:8 + n_out]
    m1 = pltpu.roll(u_ext, 1, 0)[8:8 + n_out]
    m2 = pltpu.roll(u_ext, 2, 0)[8:8 + n_out]
    return m2, m1, cur


def _convglu_fwd(u, conv_w8, conv_b, tm=64):
    s, w = u.shape
    tb = tm // 8

    def fn(ub, up, cw, cb):
        i = pl.program_id(0)
        up = jnp.where(i == 0, 0.0, up)
        m2, m1, cur = _conv_taps(jnp.concatenate([up, ub], axis=0), tm, None)
        acc = cb + cw[0:1] * m2 + cw[1:2] * m1 + cw[2:3] * cur
        act, _ = _gelu_parts(acc[:, :D_FF])
        return (act * acc[:, D_FF:],), ()

    ins = [_rb(u, tm, w), (u, (8, w), lambda i: (jnp.maximum(i * tb - 1, 0), 0)), _cst(conv_w8), _cst(conv_b)]
    return _rows("convglu_fwd", fn, s, tm, ins, [(D_FF, BF16)])[0]


def _convglu_bwd(u, dact, conv_w8, conv_b, tm=64):
    s, w = u.shape
    tb = tm // 8
    nb = s // tm

    def fn(ub, up, un, db, dn, cw, cb):
        i = pl.program_id(0)
        up = jnp.where(i == 0, 0.0, up)
        dn = jnp.where(i == nb - 1, 0.0, dn)
        ne = tm + 8
        m2, m1, cur = _conv_taps(jnp.concatenate([up, ub, un], axis=0), ne, None)
        acc = cb + cw[0:1] * m2 + cw[1:2] * m1 + cw[2:3] * cur
        de = jnp.concatenate([db, dn], axis=0)
        gl, dgl = _gelu_parts(acc[:, :D_FF])
        dacc = jnp.concatenate([de * acc[:, D_FF:] * dgl, de * gl], axis=1)
        p1 = pltpu.roll(dacc, ne - 1, 0)[:tm]
        p2 = pltpu.roll(dacc, ne - 2, 0)[:tm]
        d0 = dacc[:tm]
        du = cw[2:3] * d0 + cw[1:2] * p1 + cw[0:1] * p2
        zero5 = jnp.zeros((5, w), F32)
        dcw = jnp.concatenate([
            jnp.sum(d0 * m2[:tm], axis=0, keepdims=True), jnp.sum(d0 * m1[:tm], axis=0, keepdims=True),
            jnp.sum(d0 * cur[:tm], axis=0, keepdims=True), zero5], axis=0)
        return (du,), (dcw, jnp.sum(d0, axis=0, keepdims=True))

    ins = [
        _rb(u, tm, w),
        (u, (8, w), lambda i: (jnp.maximum(i * tb - 1, 0), 0)),
        (u, (8, w), lambda i: (jnp.minimum((i + 1) * tb, s // 8 - 1), 0)),
        _rb(dact, tm, D_FF),
        (dact, (8, D_FF), lambda i: (jnp.minimum((i + 1) * tb, s // 8 - 1), 0)),
        _cst(conv_w8), _cst(conv_b),
    ]
    return _rows("convglu_bwd", fn, s, tm, ins, [(w, BF16)], [(8, w), (1, w)])


def _chunk_scan(x, t_iota, reverse):
    k = 1
    while k < HG_C:
        if reverse:
            x = x + jnp.where(t_iota < HG_C - k, pltpu.roll(x, HG_C - k, 0), 0.0)
        else:
            x = x + jnp.where(t_iota >= k, pltpu.roll(x, k, 0), 0.0)
        k *= 2
    return x


def _hg_gates(hq, hf, lb):
    sq = _sigmoid(hq)
    q = hq * sq
    sg = _sigmoid(hf)
    f = lb + (1.0 - lb) * sg
    return q, sq, sg, f, 1.0 - f, jnp.log(f)


def _lb_of(logits):
    l0, l1 = logits[0:1], logits[1:2]
    mx = jnp.maximum(l0, l1)
    e0, e1 = jnp.exp(l0 - mx), jnp.exp(l1 - mx)
    return e0 / (e0 + e1)


def _tri(n, lower):
    r = lax.broadcasted_iota(jnp.int32, (n, n), 0)
    c = lax.broadcasted_iota(jnp.int32, (n, n), 1)
    return jnp.where((r >= c) if lower else (r <= c), 1.0, 0.0).astype(BF16)


def _hg_intra_terms(q, kk, b, t_iota):
    ws, ps = [], []
    for s in range(HG_C):
        p = jnp.where(t_iota >= s, jnp.exp(b - b[s:s + 1]), 0.0)
        ps.append(p)
        ws.append(q * kk[s:s + 1] * p)
    return jnp.concatenate(ws, axis=0), ps


def _hg_fwd(proj, lb_logits):
    s = proj.shape[0]
    nt = s // HG_T
    nc = HG_T // HG_C

    def body(q_ref, f_ref, i_ref, l_ref, o_ref, st_ref, state):
        @pl.when(pl.program_id(1) == 0)
        def _():
            state[...] = jnp.zeros_like(state)

        st_ref[0, 0] = state[...]
        lb = _lb_of(l_ref[...])
        ones = jnp.ones((HG_DK, HG_DK), BF16)
        t_iota = lax.broadcasted_iota(jnp.int32, (HG_C, HG_DK), 0)
        cc = HG_C * HG_C

        def group(gi, st):
            units = []
            for u in range(HG_UNROLL):
                r = pl.ds(pl.multiple_of((gi * HG_UNROLL + u) * HG_C, HG_C), HG_C)
                q, _, _, _, kk, g = _hg_gates(q_ref[r, :], f_ref[r, :], lb)
                b = _chunk_scan(g, t_iota, False)
                b_end = b[HG_C - 1:HG_C]
                w_all, _ = _hg_intra_terms(q, kk, b, t_iota)
                units.append((r, i_ref[r, :], q * jnp.exp(b), jnp.exp(b_end), kk * jnp.exp(b_end - b), w_all))
            a_all = _dot(jnp.concatenate([un[5] for un in units], axis=0), ones, NN)
            kvs = [_dot(v, kd, TN) for (_, v, _, _, kd, _) in units]
            sts = [st]
            for (_, _, _, dec, _, _), kv in zip(units, kvs):
                sts.append(sts[-1] * dec + kv)
            for ui, (r, v, qd, _, _, _) in enumerate(units):
                o = _dot(qd, sts[ui], NT)
                for si in range(HG_C):
                    o = o + a_all[ui * cc + si * HG_C:ui * cc + (si + 1) * HG_C] * v[si:si + 1]
                o_ref[r, :] = o
            return sts[-1]

        state[...] = lax.fori_loop(0, nc // HG_UNROLL, group, state[...])

    col = lambda off: pl.BlockSpec((HG_T, HG_DK), lambda h, t: (t, off + h))
    return pl.pallas_call(
        body, name="hgrn2_fwd", grid=(HG_H, nt),
        in_specs=[col(0), col(8), col(16), pl.BlockSpec((2, HG_DK), lambda h, t: (0, h))],
        out_specs=[pl.BlockSpec((HG_T, HG_DK), lambda h, t: (t, h)),
                   pl.BlockSpec((1, 1, HG_DK, HG_DK), lambda h, t: (h, t, 0, 0))],
        out_shape=[jax.ShapeDtypeStruct((s, D), F32), jax.ShapeDtypeStruct((HG_H, nt, HG_DK, HG_DK), F32)],
        scratch_shapes=[pltpu.VMEM((HG_DK, HG_DK), F32)],
        compiler_params=_cparams(("arbitrary", "arbitrary")),
    )(proj, proj, proj, lb_logits)


def _hg_bwd(proj, lb_logits, states, do_raw):
    s = proj.shape[0]
    nt = s // HG_T
    nc = HG_T // HG_C

    def body(q_ref, f_ref, i_ref, l_ref, st_ref, do_ref, dq_ref, df_ref, di_ref, dl_ref, st_all, adj):
        tb = pl.program_id(1)

        @pl.when(tb == 0)
        def _():
            adj[...] = jnp.zeros_like(adj)
            dl_ref[...] = jnp.zeros_like(dl_ref)

        lb = _lb_of(l_ref[...])
        ones = jnp.ones((HG_DK, HG_DK), BF16)
        t_iota = lax.broadcasted_iota(jnp.int32, (HG_C, HG_DK), 0)
        cc = HG_C * HG_C

        def fwd_group(gi, st):
            terms = []
            for u in range(HG_UNROLL):
                ci = gi * HG_UNROLL + u
                r = pl.ds(pl.multiple_of(ci * HG_C, HG_C), HG_C)
                _, _, _, _, kk, g = _hg_gates(q_ref[r, :], f_ref[r, :], lb)
                b = _chunk_scan(g, t_iota, False)
                b_end = b[HG_C - 1:HG_C]
                terms.append((ci, jnp.exp(b_end), _dot(i_ref[r, :], kk * jnp.exp(b_end - b), TN)))
            for ci, dec, kv in terms:
                st_all[ci] = st
                st = st * dec + kv
            return st

        lax.fori_loop(0, nc // HG_UNROLL, fwd_group, st_ref[0, 0])

        def bwd_group(gj, dlb):
            units = []
            for u in range(HG_UNROLL_BWD):
                ci = nc - 1 - (gj * HG_UNROLL_BWD + u)
                r = pl.ds(pl.multiple_of(ci * HG_C, HG_C), HG_C)
                hq, hf, v, do = q_ref[r, :], f_ref[r, :], i_ref[r, :], do_ref[r, :]
                q, sq, sg, f, kk, g = _hg_gates(hq, hf, lb)
                b = _chunk_scan(g, t_iota, False)
                b_end = b[HG_C - 1:HG_C]
                e_b, e_be, dec = jnp.exp(b), jnp.exp(b_end - b), jnp.exp(b_end)
                w_all, ps = _hg_intra_terms(q, kk, b, t_iota)
                x_all = jnp.concatenate([do * v[si:si + 1] for si in range(HG_C)], axis=0)
                units.append(dict(ci=ci, r=r, hq=hq, v=v, do=do, q=q, sq=sq, sg=sg, f=f, kk=kk, e_b=e_b, e_be=e_be,
                                  dec=dec, kd=kk * e_be, w=w_all, ps=ps, x=x_all))
            both = _dot(jnp.concatenate([un["w"] for un in units] + [un["x"] for un in units], axis=0), ones, NN)
            st0s = [st_all[un["ci"]] for un in units]
            st_ends = [st0 * un["dec"] + _dot(un["v"], un["kd"], TN) for un, st0 in zip(units, st0s)]
            dqks = [_dot(un["do"], un["q"] * un["e_b"], TN) for un in units]
            es = [adj[...]]
            for un, dqk in zip(units, dqks):
                es.append(es[-1] * un["dec"] + dqk)
            adj[...] = es[-1]
            for ui, un in enumerate(units):
                e, q, kk, v, do = es[ui], un["q"], un["kk"], un["v"], un["do"]
                tail = jnp.sum(e * st_ends[ui], axis=0, keepdims=True)
                dq = un["e_b"] * _dot(do, st0s[ui], NN)
                dk = un["e_be"] * _dot(v, e, NN)
                dv = _dot(un["kd"], e, NT)
                a0 = ui * cc
                d0 = (HG_UNROLL_BWD + ui) * cc
                for si in range(HG_C):
                    da = both[d0 + si * HG_C:d0 + (si + 1) * HG_C]
                    aa = both[a0 + si * HG_C:a0 + (si + 1) * HG_C]
                    dap = da * un["ps"][si]
                    dq = dq + dap * kk[si:si + 1]
                    hit = t_iota == si
                    dk = dk + jnp.where(hit, jnp.sum(dap * q, axis=0, keepdims=True), 0.0)
                    dv = dv + jnp.where(hit, jnp.sum(aa * do, axis=0, keepdims=True), 0.0)
                dg = _chunk_scan(q * dq - kk * dk, t_iota, True) + tail
                dfg = dg / un["f"] - dk
                sq, sg, hq, r = un["sq"], un["sg"], un["hq"], un["r"]
                dq_ref[r, :] = (dq * sq * (1.0 + hq * (1.0 - sq))).astype(dq_ref.dtype)
                df_ref[r, :] = (dfg * (1.0 - lb) * sg * (1.0 - sg)).astype(df_ref.dtype)
                di_ref[r, :] = dv.astype(di_ref.dtype)
                dlb = dlb + jnp.sum(dfg * (1.0 - sg), axis=0, keepdims=True)
            return dlb

        dlb = lax.fori_loop(0, nc // HG_UNROLL_BWD, bwd_group, jnp.zeros((1, HG_DK), F32))
        dl0 = dlb * lb * (1.0 - lb)
        dl_ref[...] += jnp.concatenate([dl0, -dl0], axis=0)

    col = lambda off: pl.BlockSpec((HG_T, HG_DK), lambda h, t: (nt - 1 - t, off + h))
    out_col = pl.BlockSpec((HG_T, HG_DK), lambda h, t: (nt - 1 - t, h))
    return pl.pallas_call(
        body, name="hgrn2_bwd", grid=(HG_H, nt),
        in_specs=[col(0), col(8), col(16), pl.BlockSpec((2, HG_DK), lambda h, t: (0, h)),
                  pl.BlockSpec((1, 1, HG_DK, HG_DK), lambda h, t: (h, nt - 1 - t, 0, 0)), col(0)],
        out_specs=[out_col, out_col, out_col, pl.BlockSpec((2, HG_DK), lambda h, t: (0, h))],
        out_shape=[jax.ShapeDtypeStruct((s, D), BF16)] * 3 + [jax.ShapeDtypeStruct((2, D), F32)],
        scratch_shapes=[pltpu.VMEM((nc, HG_DK, HG_DK), F32), pltpu.VMEM((HG_DK, HG_DK), F32)],
        compiler_params=_cparams(("arbitrary", "arbitrary")),
    )(proj, proj, proj, lb_logits, states, do_raw)


def _hg_post_fwd(o_raw, proj, gnorm, tm=256):
    s = o_raw.shape[0]

    def fn(o, hg, gn):
        outs = []
        for h in range(HG_H):
            sl = slice(h * HG_DK, (h + 1) * HG_DK)
            oh, gh = o[:, sl], hg[:, sl]
            r = lax.rsqrt(jnp.mean(oh * oh, axis=-1, keepdims=True) + EPS)
            outs.append(oh * r * gn * (gh * _sigmoid(gh)))
        return (jnp.concatenate(outs, axis=1),), ()

    return _rows("hgrn2_out_fwd", fn, s, tm, [_rb(o_raw, tm, D), _rb(proj, tm, D, 3), _cst(gnorm)], [(D, BF16)])[0]


def _hg_post_bwd(do_a, o_raw, proj, gnorm, tm=256):
    s = o_raw.shape[0]

    def fn(da, o, hg, gn):
        dos, dhgs = [], []
        dgn = jnp.zeros((1, HG_DK), F32)
        for h in range(HG_H):
            sl = slice(h * HG_DK, (h + 1) * HG_DK)
            oh, gh, dh = o[:, sl], hg[:, sl], da[:, sl]
            r = lax.rsqrt(jnp.mean(oh * oh, axis=-1, keepdims=True) + EPS)
            xhat = oh * r
            sg = _sigmoid(gh)
            dy = dh * (gh * sg)
            dhgs.append(dh * xhat * gn * sg * (1.0 + gh * (1.0 - sg)))
            dgn = dgn + jnp.sum(dy * xhat, axis=0, keepdims=True)
            dxh = dy * gn
            dos.append(r * (dxh - xhat * jnp.mean(dxh * xhat, axis=-1, keepdims=True)))
        return (jnp.concatenate(dos, axis=1), jnp.concatenate(dhgs, axis=1)), (dgn,)

    ins = [_rb(do_a, tm, D), _rb(o_raw, tm, D), _rb(proj, tm, D, 3), _cst(gnorm)]
    return _rows("hgrn2_out_bwd", fn, s, tm, ins, [(D, F32), (D, BF16)], [(1, HG_DK)])


def _log_sigmoid(z):
    return jnp.minimum(z, 0.0) - jnp.log(1.0 + jnp.exp(-jnp.abs(z)))


def _fox_gate_bwd(dct, pff, bias, tm=256):
    s = pff.shape[0]
    nb = s // tm

    def body(d_ref, p_ref, b_ref, dff_ref, db_ref, carry):
        @pl.when(pl.program_id(0) == 0)
        def _():
            carry[...] = jnp.zeros_like(carry)
            db_ref[...] = jnp.zeros_like(db_ref)

        dc = d_ref[...].T
        dlf = _split_dot(_tri(tm, False), dc, 3) + carry[0:1]
        carry[...] = jnp.broadcast_to(dlf[0:1], carry.shape)
        dff = dlf * _sigmoid(-(p_ref[...] + b_ref[...]))
        dff_ref[...] = dff
        db_ref[...] += jnp.sum(dff, axis=0, keepdims=True)

    return pl.pallas_call(
        body, name="fox_gate_bwd", grid=(nb,),
        in_specs=[pl.BlockSpec((LANES, tm), lambda i: (0, nb - 1 - i)),
                  pl.BlockSpec((tm, LANES), lambda i: (nb - 1 - i, 0)), pl.BlockSpec((1, LANES), lambda i: (0, 0))],
        out_specs=[pl.BlockSpec((tm, LANES), lambda i: (nb - 1 - i, 0)), pl.BlockSpec((1, LANES), lambda i: (0, 0))],
        out_shape=[jax.ShapeDtypeStruct((s, LANES), F32), jax.ShapeDtypeStruct((1, LANES), F32)],
        scratch_shapes=[pltpu.VMEM((8, LANES), F32)],
        compiler_params=_cparams(("arbitrary",)),
    )(dct, pff, bias)


def _diag_mask(t):
    r = lax.broadcasted_iota(jnp.int32, (t, t), 0)
    c = lax.broadcasted_iota(jnp.int32, (t, t), 1)
    return r >= c


AUX_ONES = 6


def _pieces(x):
    h = x.astype(BF16)
    r = x - h.astype(F32)
    m = r.astype(BF16)
    return h, m, (r - m.astype(F32)).astype(BF16)


def _lane_put(lane, cols, base):
    out = None
    for i, col in enumerate(cols):
        term = jnp.where(lane == base + i, col.astype(F32), 0.0)
        out = term if out is None else out + term
    return out


def _fox_prep2(proj, pff, bias, tm=256):
    s = pff.shape[0]

    def body(q_ref, k_ref, v_ref, p_ref, b_ref, qb_ref, kb_ref, vb_ref, ka_ref, carry):
        @pl.when(pl.program_id(0) == 0)
        def _():
            carry[...] = jnp.zeros_like(carry)

        qb_ref[...] = (q_ref[...] * 0.125).astype(BF16)
        kb_ref[...] = k_ref[...].astype(BF16)
        vb_ref[...] = v_ref[...].astype(BF16)
        lf = _log_sigmoid(p_ref[...] + b_ref[...])
        c = _split_dot(_tri(tm, True), lf, 3) + carry[0:1]
        carry[...] = jnp.broadcast_to(c[tm - 1:tm], carry.shape)
        lane = lax.broadcasted_iota(jnp.int32, (tm, LANES), 1)
        ones = jnp.where((lane >= AUX_ONES) & (lane < AUX_ONES + 6), 1.0, 0.0)
        for p in range(FOX_H // 2):
            aux = ones
            for z in range(2):
                col = jnp.sum(jnp.where(lane == 2 * p + z, c, 0.0), axis=1, keepdims=True)
                aux = aux + _lane_put(lane, _pieces(-col), 3 * z)
            ka_ref[:, p * LANES:(p + 1) * LANES] = aux.astype(BF16)

    row = lambda cb: pl.BlockSpec((tm, D), lambda i: (i, cb))
    return pl.pallas_call(
        body, name="fox_prep", grid=(s // tm,),
        in_specs=[row(4), row(5), row(6), pl.BlockSpec((tm, LANES), lambda i: (i, 0)),
                  pl.BlockSpec((1, LANES), lambda i: (0, 0))],
        out_specs=[row(0)] * 4, out_shape=[jax.ShapeDtypeStruct((s, D), BF16)] * 4,
        scratch_shapes=[pltpu.VMEM((8, LANES), F32)],
        compiler_params=_cparams(("arbitrary",)),
    )(proj, proj, proj, pff, bias)


def _fox_fwd2(qb, kb, vb, ka):
    s = qb.shape[0]
    t = min(FOX_T, s)
    nq = s // t

    def body(q_ref, k_ref, v_ref, ka_ref, o_ref, la_ref):
        i = pl.program_id(1)
        lane = lax.broadcasted_iota(jnp.int32, (t, LANES), 1)
        in_a = lane < FOX_D
        q = q_ref[...]
        zero = jnp.zeros_like(q)
        qh = [jnp.where(in_a, q, zero), jnp.where(in_a, zero, q)]
        c_ones = [jnp.where((lane >= 3 * z) & (lane < 3 * z + 3), 1.0, 0.0) for z in range(2)]
        dmask = _diag_mask(t)

        def keys(j):
            rows = pl.ds(pl.multiple_of(j * t, t), t)
            return jnp.concatenate([k_ref[rows, :], ka_ref[rows, :]], axis=1), rows

        def logits(qx, kk, masked):
            e = lax.dot_general(qx, kk, (NT, ((), ())), preferred_element_type=F32)
            return jnp.where(dmask, e, -1e30) if masked else e

        qc = [jnp.concatenate([qh[z], c_ones[z].astype(BF16)], axis=1) for z in range(2)]

        def step(j, carry, masked):
            kk, rows = keys(j)
            vj = v_ref[rows, :]
            scores = [logits(qc[z], kk, masked) for z in range(2)]
            one = jnp.ones_like(vj)
            vh = [jnp.where(in_a, vj, one), jnp.where(in_a, one, vj)]
            out = []
            for z in range(2):
                m, acc = carry[z]
                m_new = jnp.maximum(m, jnp.max(scores[z], axis=1, keepdims=True))
                p = jnp.exp(scores[z] - m_new)
                out.append((m_new, jnp.exp(m - m_new) * acc + _dot(p, vh[z], NN)))
            return tuple(out)

        init = tuple((jnp.full((t, 1), -1e30, F32), jnp.zeros((t, LANES), F32)) for _ in range(2))
        (ma, acc_a), (mb, acc_b) = step(i, lax.fori_loop(0, i, lambda j, c: step(j, c, False), init), True)
        la = jnp.sum(jnp.where(lane == FOX_D, acc_a, 0.0), axis=1, keepdims=True)
        lb = jnp.sum(jnp.where(lane == 0, acc_b, 0.0), axis=1, keepdims=True)
        o_ref[...] = jnp.where(in_a, acc_a / la, acc_b / lb).astype(o_ref.dtype)
        la_ref[...] = (_lane_put(lane, _pieces(-(ma + jnp.log(la))), AUX_ONES)
                       + _lane_put(lane, _pieces(-(mb + jnp.log(lb))), AUX_ONES + 3)).astype(la_ref.dtype)

    blk = pl.BlockSpec((t, LANES), lambda p, i: (i, p))
    whole = pl.BlockSpec((s, LANES), lambda p, i: (0, p))
    return pl.pallas_call(
        body, name="fox_attn_fwd", grid=(FOX_H // 2, nq), in_specs=[blk, whole, whole, whole],
        out_specs=[blk, blk], out_shape=[jax.ShapeDtypeStruct((s, D), BF16)] * 2,
        compiler_params=_cparams(("arbitrary", "arbitrary")),
    )(qb, kb, vb, ka)


def _fox_bwd2(qb, kb, vb, ka, ob, laux, dob):
    s = qb.shape[0]
    t = min(FOX_T, s)
    nq = s // t

    def body(q_ref, k_ref, v_ref, ka_ref, o_ref, la_ref, do_ref, dq_ref, dk_ref, dv_ref, dc_ref):
        i = pl.program_id(1)

        @pl.when(i == 0)
        def _():
            dk_ref[...] = jnp.zeros_like(dk_ref)
            dv_ref[...] = jnp.zeros_like(dv_ref)
            dc_ref[...] = jnp.zeros_like(dc_ref)

        lane = lax.broadcasted_iota(jnp.int32, (t, LANES), 1)
        in_a = lane < FOX_D
        q, do, la = q_ref[...], do_ref[...], la_ref[...].astype(F32)
        zero = jnp.zeros_like(q)
        qh = [jnp.where(in_a, q, zero), jnp.where(in_a, zero, q)]
        doh = [jnp.where(in_a, do, zero), jnp.where(in_a, zero, do)]
        prod = do.astype(F32) * o_ref[...].astype(F32)
        qx, dox = [], []
        for z in range(2):
            delta = jnp.sum(jnp.where(in_a if z == 0 else ~in_a, prod, 0.0), axis=1, keepdims=True)
            c_ones = jnp.where((lane >= 3 * z) & (lane < 3 * z + 3), 1.0, 0.0)
            lse_lanes = (lane >= AUX_ONES + 3 * z) & (lane < AUX_ONES + 3 * z + 3)
            qx.append(jnp.concatenate([qh[z], (c_ones + jnp.where(lse_lanes, la, 0.0)).astype(BF16)], axis=1))
            dox.append(jnp.concatenate([doh[z], _lane_put(lane, _pieces(-delta), 3 * z).astype(BF16)], axis=1))
        v_ones = jnp.where(lane < 6, 1.0, 0.0).astype(BF16)
        dmask = _diag_mask(t)

        def step(j, carry, masked):
            rows = pl.ds(pl.multiple_of(j * t, t), t)
            kj, vj = k_ref[rows, :], v_ref[rows, :]
            kk = jnp.concatenate([kj, ka_ref[rows, :]], axis=1)
            vv = jnp.concatenate([vj, v_ones], axis=1)
            out = []
            dk_add, dv_add = None, None
            for z in range(2):
                dq, rsum = carry[z]
                e = lax.dot_general(qx[z], kk, (NT, ((), ())), preferred_element_type=F32)
                if masked:
                    e = jnp.where(dmask, e, -1e30)
                p = jnp.exp(e)
                ds = p * lax.dot_general(dox[z], vv, (NT, ((), ())), preferred_element_type=F32)
                dkz, dvz = _dot(ds, qh[z], TN), _dot(p, doh[z], TN)
                dk_add = dkz if dk_add is None else dk_add + dkz
                dv_add = dvz if dv_add is None else dv_add + dvz
                dc_ref[0, z, j] += -jnp.sum(ds, axis=0, keepdims=True)
                out.append((dq + _dot(ds, kj, NN), rsum + jnp.sum(ds, axis=1, keepdims=True)))
            dk_ref[rows, :] += dk_add
            dv_ref[rows, :] += dv_add
            return tuple(out)

        init = tuple((jnp.zeros((t, LANES), F32), jnp.zeros((t, 1), F32)) for _ in range(2))
        (dq_a, rs_a), (dq_b, rs_b) = step(i, lax.fori_loop(0, i, lambda j, c: step(j, c, False), init), True)
        for z, rs in enumerate((rs_a, rs_b)):
            dc_ref[0, z, i] += jnp.transpose(jnp.broadcast_to(rs, (t, LANES)))[0:1]
        dq_ref[...] = (jnp.where(in_a, dq_a, dq_b) * 0.125).astype(dq_ref.dtype)

    blk = pl.BlockSpec((t, LANES), lambda p, i: (i, p))
    whole = pl.BlockSpec((s, LANES), lambda p, i: (0, p))
    return pl.pallas_call(
        body, name="fox_attn_bwd", grid=(FOX_H // 2, nq),
        in_specs=[blk, whole, whole, whole, blk, blk, blk],
        out_specs=[blk, whole, whole, pl.BlockSpec((1, 2, nq, 1, t), lambda p, i: (p, 0, 0, 0, 0))],
        out_shape=[jax.ShapeDtypeStruct((s, D), BF16), jax.ShapeDtypeStruct((s, D), F32),
                   jax.ShapeDtypeStruct((s, D), F32), jax.ShapeDtypeStruct((FOX_H // 2, 2, nq, 1, t), F32)],
        compiler_params=_cparams(("arbitrary", "arbitrary")),
    )(qb, kb, vb, ka, ob, laux, dob)


def _adamw(name, w, g, m, v, tm=None):
    rows, width = w.shape
    tm = rows if tm is None else tm
    c1 = 1.0 - ADAM_B1 ** ADAM_STEP
    c2 = 1.0 - ADAM_B2 ** ADAM_STEP

    def fn(wb, gb, mb, vb):
        m_new = ADAM_B1 * mb + (1.0 - ADAM_B1) * gb
        v_new = ADAM_B2 * vb + (1.0 - ADAM_B2) * (gb * gb)
        delta = -ADAM_LR * ((m_new / c1) / (jnp.sqrt(v_new / c2) + ADAM_EPS) + ADAM_WD * wb)
        return (delta, m_new, v_new), ()

    ins = [_rb(a, tm, width) for a in (w, g, m, v)]
    return _rows(name, fn, rows, tm, ins, [(width, F32)] * 3)


def _me():
    return lax.axis_index("x"), lax.axis_index("y"), lax.axis_index("c")


def _all_gather8(name, block):
    m, n = block.shape

    def body(x_ref, out_ref, send_sems, recv_sems):
        x, y, c = _me()
        me, sibling = (x, y, c), (x, y, 1 - c)
        chips = [(1 - x, y), (x, 1 - y), (1 - x, 1 - y)]

        def slot(px, py, pc):
            return out_ref.at[4 * px + 2 * py + pc]

        def copy(k, blk, to, src=None):
            return pltpu.make_async_remote_copy(
                src_ref=slot(*blk) if src is None else src, dst_ref=slot(*blk),
                send_sem=send_sems.at[k], recv_sem=recv_sems.at[k], device_id=to, device_id_type=MESH)

        first = [copy(0, me, sibling, src=x_ref)]
        first += [copy(1 + j, me, (*chip, c), src=x_ref) for j, chip in enumerate(chips)]
        for cp in first:
            cp.start()
        passed = [copy(4 + j, (*chip, c), sibling) for j, chip in enumerate(chips)]
        for j, chip in enumerate(chips):
            copy(1 + j, (*chip, c), me).wait_recv()
            passed[j].start()
        copy(0, sibling, me).wait_recv()
        for j, chip in enumerate(chips):
            copy(4 + j, (*chip, 1 - c), me).wait_recv()
        for cp in first + passed:
            cp.wait_send()

    gathered = pl.pallas_call(
        body, name=name, in_specs=[ANY], out_specs=ANY,
        out_shape=jax.ShapeDtypeStruct((8, m, n), block.dtype),
        scratch_shapes=[pltpu.SemaphoreType.DMA((7,)), pltpu.SemaphoreType.DMA((7,))],
    )(block)
    x, y, c = _me()
    return lax.dynamic_update_slice(gathered, block[None], (4 * x + 2 * y + c, 0, 0))


def _swap_halves(name, g):
    n, _, m, lanes = g.shape

    def body(g_ref, got_ref, send_sems, recv_sems):
        x, y, c = _me()
        copies = [pltpu.make_async_remote_copy(
            src_ref=g_ref.at[j, 1 - c], dst_ref=got_ref.at[j], send_sem=send_sems.at[j], recv_sem=recv_sems.at[j],
            device_id=(x, y, 1 - c), device_id_type=MESH) for j in range(n)]
        for cp in copies:
            cp.start()
        for cp in copies:
            cp.wait()

    return pl.pallas_call(
        body, name=name, in_specs=[ANY], out_specs=ANY, out_shape=jax.ShapeDtypeStruct((n, m, lanes), g.dtype),
        scratch_shapes=[pltpu.SemaphoreType.DMA((n,)), pltpu.SemaphoreType.DMA((n,))],
    )(g)


def _swap_sibling(name, mine):
    def body(m_ref, out_ref, send_sem, recv_sem):
        x, y, c = _me()
        cp = pltpu.make_async_remote_copy(src_ref=m_ref, dst_ref=out_ref, send_sem=send_sem, recv_sem=recv_sem,
                                          device_id=(x, y, 1 - c), device_id_type=MESH)
        cp.start()
        cp.wait()

    return pl.pallas_call(
        body, name=name, in_specs=[ANY], out_specs=ANY, out_shape=jax.ShapeDtypeStruct(mine.shape, mine.dtype),
        scratch_shapes=[pltpu.SemaphoreType.DMA, pltpu.SemaphoreType.DMA],
    )(mine)


def _chip_exchange(name, p):
    def body(p_ref, out_ref, send_sems, recv_sems):
        x, y, c = _me()
        my_chip = 2 * x + y
        chips = [(1 - x, y), (x, 1 - y), (1 - x, 1 - y)]
        sends = []
        for k, (px, py) in enumerate(chips):
            sends.append(pltpu.make_async_remote_copy(
                src_ref=p_ref.at[2 * px + py], dst_ref=out_ref.at[my_chip], send_sem=send_sems.at[k],
                recv_sem=recv_sems.at[k], device_id=(px, py, c), device_id_type=MESH))
        for cp in sends:
            cp.start()
        for k, (px, py) in enumerate(chips):
            pltpu.make_async_remote_copy(
                src_ref=p_ref.at[my_chip], dst_ref=out_ref.at[2 * px + py], send_sem=send_sems.at[k],
                recv_sem=recv_sems.at[k], device_id=(px, py, c), device_id_type=MESH).wait_recv()
        for cp in sends:
            cp.wait_send()

    got = pl.pallas_call(
        body, name=name, in_specs=[ANY], out_specs=ANY, out_shape=jax.ShapeDtypeStruct(p.shape, p.dtype),
        scratch_shapes=[pltpu.SemaphoreType.DMA((3,)), pltpu.SemaphoreType.DMA((3,))],
    )(p)
    x, y, _ = _me()
    my_chip = 2 * x + y
    return lax.dynamic_update_slice(got, lax.dynamic_index_in_dim(p, my_chip, axis=0, keepdims=True), (my_chip, 0, 0))


def _all_reduce_small(name, block):
    r, n = block.shape

    def body(x_ref, sum_ref, gath, send_sems, recv_sems):
        x, y, c = _me()
        me = 4 * x + 2 * y + c
        gath[me] = x_ref[...]
        sends = []
        for k in range(1, 8):
            px = x ^ ((k >> 2) & 1)
            py = y ^ ((k >> 1) & 1)
            pc = c ^ (k & 1)
            sends.append(pltpu.make_async_remote_copy(
                src_ref=x_ref, dst_ref=gath.at[me], send_sem=send_sems.at[k - 1], recv_sem=recv_sems.at[k - 1],
                device_id=(px, py, pc), device_id_type=MESH))
        for cp in sends:
            cp.start()
        for k in range(1, 8):
            peer = me ^ k
            pltpu.make_async_remote_copy(
                src_ref=x_ref, dst_ref=gath.at[peer], send_sem=send_sems.at[k - 1], recv_sem=recv_sems.at[k - 1],
                device_id=(x, y, c), device_id_type=MESH).wait_recv()
        for cp in sends:
            cp.wait_send()
        acc = gath[0]
        for d in range(1, 8):
            acc = acc + gath[d]
        sum_ref[...] = acc

    vm = pl.BlockSpec(memory_space=pltpu.VMEM)
    return pl.pallas_call(
        body, name=name, in_specs=[vm], out_specs=vm, out_shape=jax.ShapeDtypeStruct((r, n), F32),
        scratch_shapes=[pltpu.VMEM((8, r, n), F32), pltpu.SemaphoreType.DMA((7,)), pltpu.SemaphoreType.DMA((7,))],
    )(block)


def _add2(name, a, b, tm):
    rows = a.shape[0]
    return _rows(name, lambda p, q: ((p + q,), ()), rows, tm, [_rb(a, tm, LANES), _rb(b, tm, LANES)], [(LANES, BF16)])[0]


def _add4(name, p, tm):
    m = p.shape[1]
    flat = p.reshape(4 * m, LANES)
    nb = m // tm
    ins = [(flat, (tm, LANES), (lambda i, j=j: (j * nb + i, 0))) for j in range(4)]
    f32 = lambda v: v.astype(F32)
    return _rows(name, lambda a, b, c, d: ((((f32(a) + f32(b)) + f32(c)) + f32(d),), ()), m, tm, ins, [(LANES, F32)])[0]


SEG_ROWS = (D * W_IN_SHARD // LANES, 256 * D // LANES, 256 * D // LANES, 256 * D // LANES,
            D * W_UP_SHARD // LANES, W_DOWN_SHARD * D // LANES)
GRAD_ROWS = sum(SEG_ROWS)
CONVW_ROWS = 3 * W_UP_SHARD * 2 // LANES
GATHER_ROWS = 41600


def _flat(a):
    return a.reshape(-1, LANES)


def _gather_weights(w_in, w_a, w_b, w_out, w_up, w_down, conv_w):
    c = lax.axis_index("c")
    bits = lax.bitcast_convert_type(conv_w, BF16)
    pieces = [_flat(t.astype(BF16)) for t in (w_in, w_a, w_b, w_out, w_up, w_down)] + [_flat(bits)]
    pad = GATHER_ROWS - GRAD_ROWS - CONVW_ROWS
    shard = jnp.concatenate(pieces + [jnp.zeros((pad, LANES), BF16)], axis=0)
    half = GATHER_ROWS // 2
    mine = lax.dynamic_slice_in_dim(shard, c * half, half, axis=0)
    full = _all_gather8("all_gather_weights", mine).reshape(N_CHIP, GATHER_ROWS, LANES)
    offs = [0]
    for r in SEG_ROWS:
        offs.append(offs[-1] + r)
    seg = lambda i: full[:, offs[i]:offs[i + 1]]
    wi = seg(0).reshape(N_CHIP, D, W_IN_SHARD).transpose(1, 0, 2).reshape(D, N_CHIP * W_IN_SHARD)
    w_main = jnp.concatenate([wi[:, :FF_COL], wi[:, FF_COL + FOX_H:]], axis=1)
    w_ff = jnp.pad(wi[:, FF_COL:FF_COL + FOX_H], ((0, 0), (0, LANES - FOX_H)))
    wa, wb, wo = (seg(i).reshape(D, D) for i in (1, 2, 3))
    wu = seg(4).reshape(N_CHIP, D, W_UP_SHARD).transpose(1, 0, 2).reshape(D, 2 * D_FF)
    wd = seg(5).reshape(D_FF, D)
    cw_bits = full[:, GRAD_ROWS:GRAD_ROWS + CONVW_ROWS].reshape(N_CHIP, 3, W_UP_SHARD, 2)
    cw = lax.bitcast_convert_type(cw_bits, F32).transpose(1, 0, 2).reshape(3, 2 * D_FF)
    return w_main, w_ff, wa, wb, wo, wu, wd, cw


def _reduce_scatter_grads(d_main, d_ff, d_a, d_b, d_o, d_u, d_d):
    c = lax.axis_index("c")
    d_in = jnp.concatenate(d_main[:7] + [d_ff[:, :FOX_H]] + d_main[7:], axis=1)
    per_chip = [
        d_in.reshape(D, N_CHIP, W_IN_SHARD).transpose(1, 0, 2).reshape(N_CHIP, -1, LANES),
        d_a.reshape(N_CHIP, -1, LANES), d_b.reshape(N_CHIP, -1, LANES), d_o.reshape(N_CHIP, -1, LANES),
        d_u.reshape(D, N_CHIP, W_UP_SHARD).transpose(1, 0, 2).reshape(N_CHIP, -1, LANES),
        d_d.reshape(N_CHIP, -1, LANES),
        jnp.zeros((N_CHIP, GATHER_ROWS - GRAD_ROWS, LANES), F32),
    ]
    half = GATHER_ROWS // 2
    g = jnp.concatenate(per_chip, axis=1).reshape(N_CHIP, 2, half, LANES)
    from_sibling = _swap_halves("grad_swap_halves", g)
    mine = lax.dynamic_index_in_dim(g, c, axis=1, keepdims=False)
    tm = half // 5
    chip_sum = _add2("grad_chip_sum", mine.reshape(-1, LANES), from_sibling.reshape(-1, LANES), tm)
    pieces = _chip_exchange("grad_chip_exchange", chip_sum.reshape(N_CHIP, half, LANES))
    mine_half = _add4("grad_sum_chips", pieces, tm)
    other_half = _swap_sibling("grad_share_half", mine_half)
    lo = jnp.where(c == 0, mine_half, other_half)
    hi = jnp.where(c == 0, other_half, mine_half)
    return jnp.concatenate([lo, hi], axis=0)


def _local_step(x, target, norm_mix, fox_f_bias, hg_lb_logits, hg_norm, norm_ffn, conv_b, norm_final,
                w_main, w_ff, wa, wb, wo, wu, wd, conv_w):
    s = x.shape[0]
    bias = jnp.pad(fox_f_bias, ((0, 0), (0, LANES - FOX_H)))
    conv_w8 = jnp.pad(conv_w, ((0, 5), (0, 0)))
    t = min(FOX_T, s)

    n1 = _rms_fwd("norm_mix_fwd", x, norm_mix)
    proj = _mm("in_proj", n1, w_main, "nn", F32, 1024, 1024, D)
    pff = _mm("in_proj_forget", n1, w_ff, "nn", F32, 1024, LANES, D)
    qb, kb, vb, ka = _fox_prep2(proj, pff, bias)
    o_b, laux = _fox_fwd2(qb, kb, vb, ka)
    o_raw, states = _hg_fwd(proj, hg_lb_logits)
    o_a = _hg_post_fwd(o_raw, proj, hg_norm)
    pa = _mm("branch_a", o_a, wa, "nn", F32, 1024, 1024, D)
    pb = _mm("branch_b", o_b, wb, "nn", F32, 1024, 1024, D)
    merged = _merge_fwd(pa, pb, proj)
    h1 = _mm("out_proj", merged, wo, "nn", F32, 1024, 1024, D, res=x)
    n2 = _rms_fwd("norm_ffn_fwd", h1, norm_ffn)
    u = _mm("ffn_up", n2, wu, "nn", F32, 1024, W_UP_SHARD, D)
    act = _convglu_fwd(u, conv_w8, conv_b)
    h2 = _mm("ffn_down", act, wd, "nn", F32, 512, 1024, D_FF, res=h1)
    (dh2,), (d_norm_final, loss_row) = _final(h2, target, norm_final)

    dact = _mm("ffn_down_dx", dh2, wd, "nt", BF16, 1024, D_FF, D)
    d_wd = _mm("ffn_down_dw", act, dh2, "tn", F32, D_FF // 2, 1024, 512)
    (du,), (d_conv_w8, d_conv_b) = _convglu_bwd(u, dact, conv_w8, conv_b)
    dn2 = _mm("ffn_up_dx", du, wu, "nt", F32, 1024, 1024, W_UP_SHARD)
    d_wu = _mm("ffn_up_dw", n2, du, "tn", F32, 1024, W_UP_SHARD, 512)
    (dh1,), (d_norm_ffn,) = _rms_bwd("norm_ffn_bwd", h1, norm_ffn, [dn2], dh2)

    dmerged = _mm("out_proj_dx", dh1, wo, "nt", F32, 1024, 1024, D)
    d_wo = _mm("out_proj_dw", merged, dh1, "tn", F32, 1024, 1024, 512)
    dpa, dpb, dga, dgb = _merge_bwd(dmerged, pa, pb, proj)
    do_a = _mm("branch_a_dx", dpa, wa, "nt", F32, 1024, 1024, D)
    do_b = _mm("branch_b_dx", dpb, wb, "nt", BF16, 1024, 1024, D)
    d_wa = _mm("branch_a_dw", o_a, dpa, "tn", F32, 1024, 1024, 512)
    d_wb = _mm("branch_b_dw", o_b, dpb, "tn", F32, 1024, 1024, 512)

    (do_raw, dhg), (d_hg_norm,) = _hg_post_bwd(do_a, o_raw, proj, hg_norm)
    dhq, dhf, dhi, d_lb_logits = _hg_bwd(proj, hg_lb_logits, states, do_raw)

    dfq, dfk, dfv, dcrow = _fox_bwd2(qb, kb, vb, ka, o_b, laux, do_b)
    dct = jnp.pad(dcrow.reshape(FOX_H, s), ((0, LANES - FOX_H), (0, 0)))
    dff, d_bias = _fox_gate_bwd(dct, pff, bias)

    pieces = [dhq, dhf, dhi, dhg, dfq, dfk, dfv, dga, dgb]
    dn1 = _mm_sum_nt("in_proj_dx", pieces, w_main, (dff, w_ff), 512, 1024)
    d_w_main = [_mm("in_proj_dw_%d" % i, n1, p, "tn", F32, 1024, 1024, 512) for i, p in enumerate(pieces)]
    d_w_ff = _mm("in_proj_forget_dw", n1, dff, "tn", F32, 1024, LANES, 512)
    (dx,), (d_norm_mix,) = _rms_bwd("norm_mix_bwd", x, norm_mix, [dn1], dh1)

    small = dict(norm_mix=d_norm_mix, fox_f_bias=d_bias[:, :FOX_H], hg_lb_logits=d_lb_logits, hg_norm=d_hg_norm,
                 norm_ffn=d_norm_ffn, conv_b=d_conv_b, norm_final=d_norm_final, conv_w=d_conv_w8[:3], loss=loss_row)
    big = (d_w_main, d_w_ff, d_wa, d_wb, d_wo, d_wu, d_wd)
    return dx, small, big


SMALL_KEYS = ("norm_mix", "fox_f_bias", "hg_lb_logits", "hg_norm", "norm_ffn", "conv_b", "norm_final")


def _pack_small(parts):
    rows, layout = [], []
    for key, arr in parts:
        flat = arr.reshape(-1)
        n = flat.shape[0]
        nr = -(-n // LANES)
        rows.append(jnp.pad(flat, (0, nr * LANES - n)).reshape(nr, LANES))
        layout.append((key, arr.shape, n, nr))
    packed = jnp.concatenate(rows, axis=0)
    pad = -packed.shape[0] % 8
    return jnp.pad(packed, ((0, pad), (0, 0))), layout


def _unpack_small(packed, layout):
    out, r0 = {}, 0
    for key, shape, n, nr in layout:
        out[key] = packed[r0:r0 + nr].reshape(-1)[:n].reshape(shape)
        r0 += nr
    return out


def kernel(x, norm_mix, w_in, fox_f_bias, hg_lb_logits, hg_norm, w_branch_a, w_branch_b, w_out, norm_ffn, w_up, conv_w, conv_b, w_down, norm_final, loss_target, m_norm_mix, m_w_in, m_fox_f_bias, m_hg_lb_logits, m_hg_norm, m_w_branch_a, m_w_branch_b, m_w_out, m_norm_ffn, m_w_up, m_conv_w, m_conv_b, m_w_down, m_norm_final, v_norm_mix, v_w_in, v_fox_f_bias, v_hg_lb_logits, v_hg_norm, v_w_branch_a, v_w_branch_b, v_w_out, v_norm_ffn, v_w_up, v_conv_w, v_conv_b, v_w_down, v_norm_final):
    chip = 2 * lax.axis_index("x") + lax.axis_index("y")
    w_main, w_ff, wa, wb, wo, wu, wd, cw = _gather_weights(
        w_in[0], w_branch_a[0], w_branch_b[0], w_out[0], w_up[0], w_down[0], conv_w[0])
    dx, small, big = _local_step(
        x[0], loss_target[0], norm_mix, fox_f_bias, hg_lb_logits, hg_norm, norm_ffn, conv_b,
        norm_final.reshape(1, D), w_main, w_ff, wa, wb, wo, wu, wd, cw)

    packed, layout = _pack_small([(k, small[k]) for k in SMALL_KEYS + ("conv_w", "loss")])
    red = _unpack_small(_all_reduce_small("all_reduce_small", packed), layout)
    loss = red["loss"][0, 0]
    g_conv_w = lax.dynamic_slice_in_dim(red["conv_w"], chip * W_UP_SHARD, W_UP_SHARD, axis=1)

    gflat = _reduce_scatter_grads(*big)
    offs = [0]
    for r in SEG_ROWS:
        offs.append(offs[-1] + r)
    shapes = [(D, W_IN_SHARD), (256, D), (256, D), (256, D), (D, W_UP_SHARD), (W_DOWN_SHARD, D)]
    g_big = [gflat[offs[i]:offs[i + 1]].reshape(shapes[i]) for i in range(6)]

    names = ["norm_mix", "w_in", "fox_f_bias", "hg_lb_logits", "hg_norm", "w_branch_a", "w_branch_b", "w_out",
             "norm_ffn", "w_up", "conv_w", "conv_b", "w_down", "norm_final"]
    weights = dict(norm_mix=norm_mix, w_in=w_in, fox_f_bias=fox_f_bias, hg_lb_logits=hg_lb_logits, hg_norm=hg_norm,
                   w_branch_a=w_branch_a, w_branch_b=w_branch_b, w_out=w_out, norm_ffn=norm_ffn, w_up=w_up,
                   conv_w=conv_w, conv_b=conv_b, w_down=w_down, norm_final=norm_final)
    ms = dict(norm_mix=m_norm_mix, w_in=m_w_in, fox_f_bias=m_fox_f_bias, hg_lb_logits=m_hg_lb_logits,
              hg_norm=m_hg_norm, w_branch_a=m_w_branch_a, w_branch_b=m_w_branch_b, w_out=m_w_out,
              norm_ffn=m_norm_ffn, w_up=m_w_up, conv_w=m_conv_w, conv_b=m_conv_b, w_down=m_w_down,
              norm_final=m_norm_final)
    vs = dict(norm_mix=v_norm_mix, w_in=v_w_in, fox_f_bias=v_fox_f_bias, hg_lb_logits=v_hg_lb_logits,
              hg_norm=v_hg_norm, w_branch_a=v_w_branch_a, w_branch_b=v_w_branch_b, w_out=v_w_out,
              norm_ffn=v_norm_ffn, w_up=v_w_up, conv_w=v_conv_w, conv_b=v_conv_b, w_down=v_w_down,
              norm_final=v_norm_final)

    grads, deltas, new_m, new_v = {}, {}, {}, {}
    big_names = ["w_in", "w_branch_a", "w_branch_b", "w_out", "w_up", "w_down"]
    for name, g2 in zip(big_names, g_big):
        shape = weights[name].shape
        rows = g2.shape[0]
        d_, m_, v_ = _adamw("adamw_" + name, weights[name][0], g2, ms[name][0], vs[name][0], tm=rows // 8)
        grads[name], deltas[name], new_m[name], new_v[name] = (a.reshape(shape) for a in (g2, d_, m_, v_))
    shape = conv_w.shape
    d_, m_, v_ = _adamw("adamw_conv_w", conv_w[0], g_conv_w, m_conv_w[0], v_conv_w[0])
    grads["conv_w"], deltas["conv_w"], new_m["conv_w"], new_v["conv_w"] = (
        a.reshape(shape) for a in (g_conv_w, d_, m_, v_))
    gs = {k: red[k].reshape(weights[k].shape) for k in SMALL_KEYS}
    pw, lay = _pack_small([(k, weights[k]) for k in SMALL_KEYS])
    pg, _ = _pack_small([(k, gs[k]) for k in SMALL_KEYS])
    pm, _ = _pack_small([(k, ms[k]) for k in SMALL_KEYS])
    pv, _ = _pack_small([(k, vs[k]) for k in SMALL_KEYS])
    d_, m_, v_ = (_unpack_small(a, lay) for a in _adamw("adamw_small", pw, pg, pm, pv))
    for k in SMALL_KEYS:
        grads[k], deltas[k], new_m[k], new_v[k] = gs[k], d_[k], m_[k], v_[k]

    return (loss, dx[None], *[grads[n] for n in names], *[deltas[n] for n in names],
            *[new_m[n] for n in names], *[new_v[n] for n in names])
```

```python
import functools

import jax
import jax.numpy as jnp
from jax import lax
from jax.experimental import pallas as pl
from jax.experimental.pallas import tpu as pltpu

F32 = jnp.float32
BF16 = jnp.bfloat16

D = 1024
HG_H, HG_DK = 8, 128
FOX_H, FOX_D = 16, 64
D_FF = 2816
EPS = 1e-6
N_CHIP = 4
LANES = 128
W_IN_SHARD = 2308
W_UP_SHARD = 1408
W_DOWN_SHARD = 704
FF_COL = 7168
ADAM_LR, ADAM_B1, ADAM_B2, ADAM_EPS, ADAM_WD, ADAM_STEP = 0.001, 0.9, 0.999, 1e-08, 0.01, 10

HG_C = 16
HG_T = 256
HG_UNROLL = 8
HG_UNROLL_BWD = 4
FOX_T = 512
VMEM_LIMIT = 56 * 1024 * 1024
MESH = pl.DeviceIdType.MESH
ANY = pl.BlockSpec(memory_space=pl.ANY)


def _cparams(sem):
    return pltpu.CompilerParams(dimension_semantics=sem, vmem_limit_bytes=VMEM_LIMIT)


def _sigmoid(x):
    return 1.0 / (1.0 + jnp.exp(-x))


def _dot(a, b, dims):
    return lax.dot_general(a.astype(BF16), b.astype(BF16), (dims, ((), ())), preferred_element_type=F32)


NN = ((1,), (0,))
NT = ((1,), (1,))
TN = ((0,), (0,))


def _split_dot(tri, x, parts, dims=NN):
    acc = None
    r = x
    for _ in range(parts):
        p = r.astype(BF16)
        t = lax.dot_general(tri, p, (dims, ((), ())), preferred_element_type=F32)
        acc = t if acc is None else acc + t
        r = r - p.astype(F32)
    return acc


def _rb(arr, tm, width, cb=0):
    return (arr, (tm, width), lambda i: (i, cb))


def _cst(arr):
    return (arr, arr.shape, lambda i: (0,) * arr.ndim)


def _rows(name, fn, n_rows, tm, ins, outs, accs=(), reverse=False):
    n_in, n_out, n_acc = len(ins), len(outs), len(accs)
    nb = n_rows // tm

    def body(*refs):
        vals = [r[...] for r in refs[:n_in]]
        o, a = fn(*vals)
        for r, v in zip(refs[n_in:n_in + n_out], o):
            r[...] = v.astype(r.dtype)
        if n_acc:
            acc_refs = refs[n_in + n_out:]

            @pl.when(pl.program_id(0) == 0)
            def _():
                for r in acc_refs:
                    r[...] = jnp.zeros_like(r)

            for r, v in zip(acc_refs, a):
                r[...] += v

    if reverse:
        rowmap = lambda i: (nb - 1 - i, 0)
    else:
        rowmap = lambda i: (i, 0)
    in_specs = [pl.BlockSpec(bs, im) for (_, bs, im) in ins]
    out_specs = [pl.BlockSpec((tm, w), rowmap) for (w, _) in outs]
    out_specs += [pl.BlockSpec((r, w), lambda i: (0, 0)) for (r, w) in accs]
    out_shape = [jax.ShapeDtypeStruct((n_rows, w), dt) for (w, dt) in outs]
    out_shape += [jax.ShapeDtypeStruct((r, w), F32) for (r, w) in accs]
    res = pl.pallas_call(
        body, name=name, grid=(nb,), in_specs=in_specs, out_specs=out_specs, out_shape=out_shape,
        compiler_params=_cparams(("arbitrary",)),
    )(*[a for a, _, _ in ins])
    return (res[:n_out], res[n_out:]) if n_acc else res


def _mm(name, a, b, mode, out_dtype, tm, tn, tk, res=None):
    if mode == "nn":
        (m, k), n = a.shape, b.shape[1]
    elif mode == "nt":
        (m, k), n = a.shape, b.shape[0]
    else:
        (k, m), n = a.shape, b.shape[1]
    tm, tn, tk = min(tm, m), min(tn, n), min(tk, k)
    assert m % tm == 0 and n % tn == 0 and k % tk == 0, (name, m, n, k, tm, tn, tk)
    if mode == "nn":
        a_spec = pl.BlockSpec((tm, tk), lambda i, j, kk: (i, kk))
        b_spec = pl.BlockSpec((tk, tn), lambda i, j, kk: (kk, j))
        dims = NN
    elif mode == "nt":
        a_spec = pl.BlockSpec((tm, tk), lambda i, j, kk: (i, kk))
        b_spec = pl.BlockSpec((tn, tk), lambda i, j, kk: (j, kk))
        dims = NT
    else:
        a_spec = pl.BlockSpec((tk, tm), lambda i, j, kk: (kk, i))
        b_spec = pl.BlockSpec((tk, tn), lambda i, j, kk: (kk, j))
        dims = TN
    nk = k // tk
    has_res = res is not None

    def body(*refs):
        a_ref, b_ref = refs[0], refs[1]
        r_ref = refs[2] if has_res else None
        o_ref = refs[3] if has_res else refs[2]
        part = _dot(a_ref[...], b_ref[...], dims)

        def finish(val):
            if has_res:
                val = val + r_ref[...]
            o_ref[...] = val.astype(o_ref.dtype)

        if nk == 1:
            finish(part)
        else:
            acc_ref = refs[-1]
            kk = pl.program_id(2)

            @pl.when(kk == 0)
            def _():
                acc_ref[...] = part

            @pl.when(kk > 0)
            def _():
                acc_ref[...] += part

            @pl.when(kk == nk - 1)
            def _():
                finish(acc_ref[...])

    in_specs = [a_spec, b_spec]
    args = [a, b]
    if has_res:
        in_specs.append(pl.BlockSpec((tm, tn), lambda i, j, kk: (i, j)))
        args.append(res)
    return pl.pallas_call(
        body, name=name, grid=(m // tm, n // tn, nk), in_specs=in_specs,
        out_specs=pl.BlockSpec((tm, tn), lambda i, j, kk: (i, j)),
        out_shape=jax.ShapeDtypeStruct((m, n), out_dtype),
        scratch_shapes=[pltpu.VMEM((tm, tn), F32)] if nk > 1 else [],
        compiler_params=_cparams(("arbitrary", "arbitrary", "arbitrary")),
    )(*args)


def _mm_sum_nt(name, pieces, w, extra, tm, tn):
    n_p = len(pieces)
    m, k = pieces[0].shape
    n = w.shape[0]
    xa, xb = extra
    ke = xa.shape[1]
    tm, tn = min(tm, m), min(tn, n)

    def body(*refs):
        p_refs, w_ref, xa_ref, xb_ref, o_ref, acc_ref = refs[:n_p], refs[n_p], refs[n_p + 1], refs[n_p + 2], refs[-2], refs[-1]
        kk = pl.program_id(2)

        @pl.when(kk == 0)
        def _():
            acc_ref[...] = jnp.zeros_like(acc_ref)

        for i in range(n_p):
            @pl.when(kk == i)
            def _(i=i):
                acc_ref[...] += _dot(p_refs[i][...], w_ref[...], NT)

        @pl.when(kk == n_p)
        def _():
            o_ref[...] = acc_ref[...] + _dot(xa_ref[...], xb_ref[...], NT)

    in_specs = [pl.BlockSpec((tm, k), lambda i, j, kk: (i, 0)) for _ in range(n_p)]
    in_specs.append(pl.BlockSpec((tn, k), lambda i, j, kk: (j, jnp.minimum(kk, n_p - 1))))
    in_specs += [pl.BlockSpec((tm, ke), lambda i, j, kk: (i, 0)), pl.BlockSpec((tn, ke), lambda i, j, kk: (j, 0))]
    return pl.pallas_call(
        body, name=name, grid=(m // tm, n // tn, n_p + 1), in_specs=in_specs,
        out_specs=pl.BlockSpec((tm, tn), lambda i, j, kk: (i, j)),
        out_shape=jax.ShapeDtypeStruct((m, n), F32),
        scratch_shapes=[pltpu.VMEM((tm, tn), F32)],
        compiler_params=_cparams(("arbitrary", "arbitrary", "arbitrary")),
    )(*pieces, w, xa, xb)


def _rms_fwd(name, x, gain, tm=256):
    s = x.shape[0]

    def fn(xb, g):
        r = lax.rsqrt(jnp.mean(xb * xb, axis=-1, keepdims=True) + EPS)
        return (xb * r * g,), ()

    return _rows(name, fn, s, tm, [_rb(x, tm, D), _cst(gain)], [(D, BF16)])[0]


def _rms_bwd(name, x, gain, dns, dres, tm=256):
    s = x.shape[0]
    n_dn = len(dns)

    def fn(xb, g, *rest):
        dn = rest[0]
        for t in rest[1:n_dn]:
            dn = dn + t
        r = lax.rsqrt(jnp.mean(xb * xb, axis=-1, keepdims=True) + EPS)
        xhat = xb * r
        dxh = dn * g
        dx = r * (dxh - xhat * jnp.mean(dxh * xhat, axis=-1, keepdims=True)) + rest[n_dn]
        return (dx,), (jnp.sum(dn * xhat, axis=0, keepdims=True),)

    ins = [_rb(x, tm, D), _cst(gain)] + [_rb(t, tm, D) for t in dns] + [_rb(dres, tm, D)]
    return _rows(name, fn, s, tm, ins, [(D, F32)], [(1, D)])


def _final(h2, target, gain, tm=256):
    s = h2.shape[0]

    def fn(hb, tb, g):
        r = lax.rsqrt(jnp.mean(hb * hb, axis=-1, keepdims=True) + EPS)
        xhat = hb * r
        e = xhat * g - tb
        dy = e * (1.0 / D)
        dxh = dy * g
        dh = r * (dxh - xhat * jnp.mean(dxh * xhat, axis=-1, keepdims=True))
        lrow = 0.5 * jnp.sum(jnp.sum(e * e, axis=-1, keepdims=True) * (1.0 / D), axis=0, keepdims=True)
        return (dh,), (jnp.sum(dy * xhat, axis=0, keepdims=True), jnp.broadcast_to(lrow, (1, LANES)))

    return _rows("final_norm_loss", fn, s, tm, [_rb(h2, tm, D), _rb(target, tm, D), _cst(gain)],
                 [(D, F32)], [(1, D), (1, LANES)])


def _merge_fwd(pa, pb, proj, tm=256):
    s = pa.shape[0]

    def fn(a, b, ga, gb):
        return (_sigmoid(ga) * a + _sigmoid(gb) * b,), ()

    ins = [_rb(pa, tm, D), _rb(pb, tm, D), _rb(proj, tm, D, 7), _rb(proj, tm, D, 8)]
    return _rows("merge_fwd", fn, s, tm, ins, [(D, BF16)])[0]


def _merge_bwd(dmerged, pa, pb, proj, tm=256):
    s = pa.shape[0]

    def fn(dm, a, b, ga, gb):
        sa, sb = _sigmoid(ga), _sigmoid(gb)
        return (dm * sa, dm * sb, dm * a * sa * (1.0 - sa), dm * b * sb * (1.0 - sb)), ()

    ins = [_rb(dmerged, tm, D), _rb(pa, tm, D), _rb(pb, tm, D), _rb(proj, tm, D, 7), _rb(proj, tm, D, 8)]
    return _rows("merge_bwd", fn, s, tm, ins, [(D, BF16), (D, BF16), (D, BF16), (D, BF16)])


def _gelu_parts(x):
    cdf = 0.5 * (1.0 + lax.erf(x * 0.7071067811865476))
    pdf = 0.3989422804014327 * jnp.exp(-0.5 * x * x)
    return x * cdf, cdf + x * pdf


def _conv_taps(u_ext, n_out, first):
    n = u_ext.shape[0]
    cur = u_ext[8:8 + n_out]
    m1 = pltpu.roll(u_ext, 1, 0)[8:8 + n_out]
    m2 = pltpu.roll(u_ext, 2, 0)[8:8 + n_out]
    return m2, m1, cur


def _convglu_fwd(u, conv_w8, conv_b, tm=64):
    s, w = u.shape
    tb = tm // 8

    def fn(ub, up, cw, cb):
        i = pl.program_id(0)
        up = jnp.where(i == 0, 0.0, up)
        m2, m1, cur = _conv_taps(jnp.concatenate([up, ub], axis=0), tm, None)
        acc = cb + cw[0:1] * m2 + cw[1:2] * m1 + cw[2:3] * cur
        act, _ = _gelu_parts(acc[:, :D_FF])
        return (act * acc[:, D_FF:],), ()

    ins = [_rb(u, tm, w), (u, (8, w), lambda i: (jnp.maximum(i * tb - 1, 0), 0)), _cst(conv_w8), _cst(conv_b)]
    return _rows("convglu_fwd", fn, s, tm, ins, [(D_FF, BF16)])[0]


def _convglu_bwd(u, dact, conv_w8, conv_b, tm=64):
    s, w = u.shape
    tb = tm // 8
    nb = s // tm

    def fn(ub, up, un, db, dn, cw, cb):
        i = pl.program_id(0)
        up = jnp.where(i == 0, 0.0, up)
        dn = jnp.where(i == nb - 1, 0.0, dn)
        ne = tm + 8
        m2, m1, cur = _conv_taps(jnp.concatenate([up, ub, un], axis=0), ne, None)
        acc = cb + cw[0:1] * m2 + cw[1:2] * m1 + cw[2:3] * cur
        de = jnp.concatenate([db, dn], axis=0)
        gl, dgl = _gelu_parts(acc[:, :D_FF])
        dacc = jnp.concatenate([de * acc[:, D_FF:] * dgl, de * gl], axis=1)
        p1 = pltpu.roll(dacc, ne - 1, 0)[:tm]
        p2 = pltpu.roll(dacc, ne - 2, 0)[:tm]
        d0 = dacc[:tm]
        du = cw[2:3] * d0 + cw[1:2] * p1 + cw[0:1] * p2
        zero5 = jnp.zeros((5, w), F32)
        dcw = jnp.concatenate([
            jnp.sum(d0 * m2[:tm], axis=0, keepdims=True), jnp.sum(d0 * m1[:tm], axis=0, keepdims=True),
            jnp.sum(d0 * cur[:tm], axis=0, keepdims=True), zero5], axis=0)
        return (du,), (dcw, jnp.sum(d0, axis=0, keepdims=True))

    ins = [
        _rb(u, tm, w),
        (u, (8, w), lambda i: (jnp.maximum(i * tb - 1, 0), 0)),
        (u, (8, w), lambda i: (jnp.minimum((i + 1) * tb, s // 8 - 1), 0)),
        _rb(dact, tm, D_FF),
        (dact, (8, D_FF), lambda i: (jnp.minimum((i + 1) * tb, s // 8 - 1), 0)),
        _cst(conv_w8), _cst(conv_b),
    ]
    return _rows("convglu_bwd", fn, s, tm, ins, [(w, BF16)], [(8, w), (1, w)])


def _chunk_scan(x, t_iota, reverse):
    k = 1
    while k < HG_C:
        if reverse:
            x = x + jnp.where(t_iota < HG_C - k, pltpu.roll(x, HG_C - k, 0), 0.0)
        else:
            x = x + jnp.where(t_iota >= k, pltpu.roll(x, k, 0), 0.0)
        k *= 2
    return x


def _hg_gates(hq, hf, lb):
    sq = _sigmoid(hq)
    q = hq * sq
    sg = _sigmoid(hf)
    f = lb + (1.0 - lb) * sg
    return q, sq, sg, f, 1.0 - f, jnp.log(f)


def _lb_of(logits):
    l0, l1 = logits[0:1], logits[1:2]
    mx = jnp.maximum(l0, l1)
    e0, e1 = jnp.exp(l0 - mx), jnp.exp(l1 - mx)
    return e0 / (e0 + e1)


def _tri(n, lower):
    r = lax.broadcasted_iota(jnp.int32, (n, n), 0)
    c = lax.broadcasted_iota(jnp.int32, (n, n), 1)
    return jnp.where((r >= c) if lower else (r <= c), 1.0, 0.0).astype(BF16)


def _hg_intra_terms(q, kk, b, t_iota):
    ws, ps = [], []
    for s in range(HG_C):
        p = jnp.where(t_iota >= s, jnp.exp(b - b[s:s + 1]), 0.0)
        ps.append(p)
        ws.append(q * kk[s:s + 1] * p)
    return jnp.concatenate(ws, axis=0), ps


def _hg_fwd(proj, lb_logits):
    s = proj.shape[0]
    nt = s // HG_T
    nc = HG_T // HG_C

    def body(q_ref, f_ref, i_ref, l_ref, o_ref, st_ref, state):
        @pl.when(pl.program_id(1) == 0)
        def _():
            state[...] = jnp.zeros_like(state)

        st_ref[0, 0] = state[...]
        lb = _lb_of(l_ref[...])
        ones = jnp.ones((HG_DK, HG_DK), BF16)
        t_iota = lax.broadcasted_iota(jnp.int32, (HG_C, HG_DK), 0)
        cc = HG_C * HG_C

        def group(gi, st):
            units = []
            for u in range(HG_UNROLL):
                r = pl.ds(pl.multiple_of((gi * HG_UNROLL + u) * HG_C, HG_C), HG_C)
                q, _, _, _, kk, g = _hg_gates(q_ref[r, :], f_ref[r, :], lb)
                b = _chunk_scan(g, t_iota, False)
                b_end = b[HG_C - 1:HG_C]
                w_all, _ = _hg_intra_terms(q, kk, b, t_iota)
                units.append((r, i_ref[r, :], q * jnp.exp(b), jnp.exp(b_end), kk * jnp.exp(b_end - b), w_all))
            a_all = _dot(jnp.concatenate([un[5] for un in units], axis=0), ones, NN)
            kvs = [_dot(v, kd, TN) for (_, v, _, _, kd, _) in units]
            sts = [st]
            for (_, _, _, dec, _, _), kv in zip(units, kvs):
                sts.append(sts[-1] * dec + kv)
            for ui, (r, v, qd, _, _, _) in enumerate(units):
                o = _dot(qd, sts[ui], NT)
                for si in range(HG_C):
                    o = o + a_all[ui * cc + si * HG_C:ui * cc + (si + 1) * HG_C] * v[si:si + 1]
                o_ref[r, :] = o
            return sts[-1]

        state[...] = lax.fori_loop(0, nc // HG_UNROLL, group, state[...])

    col = lambda off: pl.BlockSpec((HG_T, HG_DK), lambda h, t: (t, off + h))
    return pl.pallas_call(
        body, name="hgrn2_fwd", grid=(HG_H, nt),
        in_specs=[col(0), col(8), col(16), pl.BlockSpec((2, HG_DK), lambda h, t: (0, h))],
        out_specs=[pl.BlockSpec((HG_T, HG_DK), lambda h, t: (t, h)),
                   pl.BlockSpec((1, 1, HG_DK, HG_DK), lambda h, t: (h, t, 0, 0))],
        out_shape=[jax.ShapeDtypeStruct((s, D), F32), jax.ShapeDtypeStruct((HG_H, nt, HG_DK, HG_DK), F32)],
        scratch_shapes=[pltpu.VMEM((HG_DK, HG_DK), F32)],
        compiler_params=_cparams(("arbitrary", "arbitrary")),
    )(proj, proj, proj, lb_logits)


def _hg_bwd(proj, lb_logits, states, do_raw):
    s = proj.shape[0]
    nt = s // HG_T
    nc = HG_T // HG_C

    def body(q_ref, f_ref, i_ref, l_ref, st_ref, do_ref, dq_ref, df_ref, di_ref, dl_ref, st_all, adj):
        tb = pl.program_id(1)

        @pl.when(tb == 0)
        def _():
            adj[...] = jnp.zeros_like(adj)
            dl_ref[...] = jnp.zeros_like(dl_ref)

        lb = _lb_of(l_ref[...])
        ones = jnp.ones((HG_DK, HG_DK), BF16)
        t_iota = lax.broadcasted_iota(jnp.int32, (HG_C, HG_DK), 0)
        cc = HG_C * HG_C

        def fwd_group(gi, st):
            terms = []
            for u in range(HG_UNROLL):
                ci = gi * HG_UNROLL + u
                r = pl.ds(pl.multiple_of(ci * HG_C, HG_C), HG_C)
                _, _, _, _, kk, g = _hg_gates(q_ref[r, :], f_ref[r, :], lb)
                b = _chunk_scan(g, t_iota, False)
                b_end = b[HG_C - 1:HG_C]
                terms.append((ci, jnp.exp(b_end), _dot(i_ref[r, :], kk * jnp.exp(b_end - b), TN)))
            for ci, dec, kv in terms:
                st_all[ci] = st
                st = st * dec + kv
            return st

        lax.fori_loop(0, nc // HG_UNROLL, fwd_group, st_ref[0, 0])

        def bwd_group(gj, dlb):
            units = []
            for u in range(HG_UNROLL_BWD):
                ci = nc - 1 - (gj * HG_UNROLL_BWD + u)
                r = pl.ds(pl.multiple_of(ci * HG_C, HG_C), HG_C)
                hq, hf, v, do = q_ref[r, :], f_ref[r, :], i_ref[r, :], do_ref[r, :]
                q, sq, sg, f, kk, g = _hg_gates(hq, hf, lb)
                b = _chunk_scan(g, t_iota, False)
                b_end = b[HG_C - 1:HG_C]
                e_b, e_be, dec = jnp.exp(b), jnp.exp(b_end - b), jnp.exp(b_end)
                w_all, ps = _hg_intra_terms(q, kk, b, t_iota)
                x_all = jnp.concatenate([do * v[si:si + 1] for si in range(HG_C)], axis=0)
                units.append(dict(ci=ci, r=r, hq=hq, v=v, do=do, q=q, sq=sq, sg=sg, f=f, kk=kk, e_b=e_b, e_be=e_be,
                                  dec=dec, kd=kk * e_be, w=w_all, ps=ps, x=x_all))
            both = _dot(jnp.concatenate([un["w"] for un in units] + [un["x"] for un in units], axis=0), ones, NN)
            st0s = [st_all[un["ci"]] for un in units]
            st_ends = [st0 * un["dec"] + _dot(un["v"], un["kd"], TN) for un, st0 in zip(units, st0s)]
            dqks = [_dot(un["do"], un["q"] * un["e_b"], TN) for un in units]
            es = [adj[...]]
            for un, dqk in zip(units, dqks):
                es.append(es[-1] * un["dec"] + dqk)
            adj[...] = es[-1]
            for ui, un in enumerate(units):
                e, q, kk, v, do = es[ui], un["q"], un["kk"], un["v"], un["do"]
                tail = jnp.sum(e * st_ends[ui], axis=0, keepdims=True)
                dq = un["e_b"] * _dot(do, st0s[ui], NN)
                dk = un["e_be"] * _dot(v, e, NN)
                dv = _dot(un["kd"], e, NT)
                a0 = ui * cc
                d0 = (HG_UNROLL_BWD + ui) * cc
                for si in range(HG_C):
                    da = both[d0 + si * HG_C:d0 + (si + 1) * HG_C]
                    aa = both[a0 + si * HG_C:a0 + (si + 1) * HG_C]
                    dap = da * un["ps"][si]
                    dq = dq + dap * kk[si:si + 1]
                    hit = t_iota == si
                    dk = dk + jnp.where(hit, jnp.sum(dap * q, axis=0, keepdims=True), 0.0)
                    dv = dv + jnp.where(hit, jnp.sum(aa * do, axis=0, keepdims=True), 0.0)
                dg = _chunk_scan(q * dq - kk * dk, t_iota, True) + tail
                dfg = dg / un["f"] - dk
                sq, sg, hq, r = un["sq"], un["sg"], un["hq"], un["r"]
                dq_ref[r, :] = (dq * sq * (1.0 + hq * (1.0 - sq))).astype(dq_ref.dtype)
                df_ref[r, :] = (dfg * (1.0 - lb) * sg * (1.0 - sg)).astype(df_ref.dtype)
                di_ref[r, :] = dv.astype(di_ref.dtype)
                dlb = dlb + jnp.sum(dfg * (1.0 - sg), axis=0, keepdims=True)
            return dlb

        dlb = lax.fori_loop(0, nc // HG_UNROLL_BWD, bwd_group, jnp.zeros((1, HG_DK), F32))
        dl0 = dlb * lb * (1.0 - lb)
        dl_ref[...] += jnp.concatenate([dl0, -dl0], axis=0)

    col = lambda off: pl.BlockSpec((HG_T, HG_DK), lambda h, t: (nt - 1 - t, off + h))
    out_col = pl.BlockSpec((HG_T, HG_DK), lambda h, t: (nt - 1 - t, h))
    return pl.pallas_call(
        body, name="hgrn2_bwd", grid=(HG_H, nt),
        in_specs=[col(0), col(8), col(16), pl.BlockSpec((2, HG_DK), lambda h, t: (0, h)),
                  pl.BlockSpec((1, 1, HG_DK, HG_DK), lambda h, t: (h, nt - 1 - t, 0, 0)), col(0)],
        out_specs=[out_col, out_col, out_col, pl.BlockSpec((2, HG_DK), lambda h, t: (0, h))],
        out_shape=[jax.ShapeDtypeStruct((s, D), BF16)] * 3 + [jax.ShapeDtypeStruct((2, D), F32)],
        scratch_shapes=[pltpu.VMEM((nc, HG_DK, HG_DK), F32), pltpu.VMEM((HG_DK, HG_DK), F32)],
        compiler_params=_cparams(("arbitrary", "arbitrary")),
    )(proj, proj, proj, lb_logits, states, do_raw)


def _hg_post_fwd(o_raw, proj, gnorm, tm=256):
    s = o_raw.shape[0]

    def fn(o, hg, gn):
        outs = []
        for h in range(HG_H):
            sl = slice(h * HG_DK, (h + 1) * HG_DK)
            oh, gh = o[:, sl], hg[:, sl]
            r = lax.rsqrt(jnp.mean(oh * oh, axis=-1, keepdims=True) + EPS)
            outs.append(oh * r * gn * (gh * _sigmoid(gh)))
        return (jnp.concatenate(outs, axis=1),), ()

    return _rows("hgrn2_out_fwd", fn, s, tm, [_rb(o_raw, tm, D), _rb(proj, tm, D, 3), _cst(gnorm)], [(D, BF16)])[0]


def _hg_post_bwd(do_a, o_raw, proj, gnorm, tm=256):
    s = o_raw.shape[0]

    def fn(da, o, hg, gn):
        dos, dhgs = [], []
        dgn = jnp.zeros((1, HG_DK), F32)
        for h in range(HG_H):
            sl = slice(h * HG_DK, (h + 1) * HG_DK)
            oh, gh, dh = o[:, sl], hg[:, sl], da[:, sl]
            r = lax.rsqrt(jnp.mean(oh * oh, axis=-1, keepdims=True) + EPS)
            xhat = oh * r
            sg = _sigmoid(gh)
            dy = dh * (gh * sg)
            dhgs.append(dh * xhat * gn * sg * (1.0 + gh * (1.0 - sg)))
            dgn = dgn + jnp.sum(dy * xhat, axis=0, keepdims=True)
            dxh = dy * gn
            dos.append(r * (dxh - xhat * jnp.mean(dxh * xhat, axis=-1, keepdims=True)))
        return (jnp.concatenate(dos, axis=1), jnp.concatenate(dhgs, axis=1)), (dgn,)

    ins = [_rb(do_a, tm, D), _rb(o_raw, tm, D), _rb(proj, tm, D, 3), _cst(gnorm)]
    return _rows("hgrn2_out_bwd", fn, s, tm, ins, [(D, F32), (D, BF16)], [(1, HG_DK)])


def _log_sigmoid(z):
    return jnp.minimum(z, 0.0) - jnp.log(1.0 + jnp.exp(-jnp.abs(z)))


def _fox_gate_bwd(dct, pff, bias, tm=256):
    s = pff.shape[0]
    nb = s // tm

    def body(d_ref, p_ref, b_ref, dff_ref, db_ref, carry):
        @pl.when(pl.program_id(0) == 0)
        def _():
            carry[...] = jnp.zeros_like(carry)
            db_ref[...] = jnp.zeros_like(db_ref)

        dc = d_ref[...].T
        dlf = _split_dot(_tri(tm, False), dc, 3) + carry[0:1]
        carry[...] = jnp.broadcast_to(dlf[0:1], carry.shape)
        dff = dlf * _sigmoid(-(p_ref[...] + b_ref[...]))
        dff_ref[...] = dff
        db_ref[...] += jnp.sum(dff, axis=0, keepdims=True)

    return pl.pallas_call(
        body, name="fox_gate_bwd", grid=(nb,),
        in_specs=[pl.BlockSpec((LANES, tm), lambda i: (0, nb - 1 - i)),
                  pl.BlockSpec((tm, LANES), lambda i: (nb - 1 - i, 0)), pl.BlockSpec((1, LANES), lambda i: (0, 0))],
        out_specs=[pl.BlockSpec((tm, LANES), lambda i: (nb - 1 - i, 0)), pl.BlockSpec((1, LANES), lambda i: (0, 0))],
        out_shape=[jax.ShapeDtypeStruct((s, LANES), F32), jax.ShapeDtypeStruct((1, LANES), F32)],
        scratch_shapes=[pltpu.VMEM((8, LANES), F32)],
        compiler_params=_cparams(("arbitrary",)),
    )(dct, pff, bias)


def _diag_mask(t):
    r = lax.broadcasted_iota(jnp.int32, (t, t), 0)
    c = lax.broadcasted_iota(jnp.int32, (t, t), 1)
    return r >= c


AUX_ONES = 6


def _pieces(x):
    h = x.astype(BF16)
    r = x - h.astype(F32)
    m = r.astype(BF16)
    return h, m, (r - m.astype(F32)).astype(BF16)


def _lane_put(lane, cols, base):
    out = None
    for i, col in enumerate(cols):
        term = jnp.where(lane == base + i, col.astype(F32), 0.0)
        out = term if out is None else out + term
    return out


def _fox_prep2(proj, pff, bias, tm=256):
    s = pff.shape[0]

    def body(q_ref, k_ref, v_ref, p_ref, b_ref, qb_ref, kb_ref, vb_ref, ka_ref, carry):
        @pl.when(pl.program_id(0) == 0)
        def _():
            carry[...] = jnp.zeros_like(carry)

        qb_ref[...] = (q_ref[...] * 0.125).astype(BF16)
        kb_ref[...] = k_ref[...].astype(BF16)
        vb_ref[...] = v_ref[...].astype(BF16)
        lf = _log_sigmoid(p_ref[...] + b_ref[...])
        c = _split_dot(_tri(tm, True), lf, 3) + carry[0:1]
        carry[...] = jnp.broadcast_to(c[tm - 1:tm], carry.shape)
        lane = lax.broadcasted_iota(jnp.int32, (tm, LANES), 1)
        ones = jnp.where((lane >= AUX_ONES) & (lane < AUX_ONES + 6), 1.0, 0.0)
        for p in range(FOX_H // 2):
            aux = ones
            for z in range(2):
                col = jnp.sum(jnp.where(lane == 2 * p + z, c, 0.0), axis=1, keepdims=True)
                aux = aux + _lane_put(lane, _pieces(-col), 3 * z)
            ka_ref[:, p * LANES:(p + 1) * LANES] = aux.astype(BF16)

    row = lambda cb: pl.BlockSpec((tm, D), lambda i: (i, cb))
    return pl.pallas_call(
        body, name="fox_prep", grid=(s // tm,),
        in_specs=[row(4), row(5), row(6), pl.BlockSpec((tm, LANES), lambda i: (i, 0)),
                  pl.BlockSpec((1, LANES), lambda i: (0, 0))],
        out_specs=[row(0)] * 4, out_shape=[jax.ShapeDtypeStruct((s, D), BF16)] * 4,
        scratch_shapes=[pltpu.VMEM((8, LANES), F32)],
        compiler_params=_cparams(("arbitrary",)),
    )(proj, proj, proj, pff, bias)


def _fox_fwd2(qb, kb, vb, ka):
    s = qb.shape[0]
    t = min(FOX_T, s)
    nq = s // t

    def body(q_ref, k_ref, v_ref, ka_ref, o_ref, la_ref):
        i = pl.program_id(1)
        lane = lax.broadcasted_iota(jnp.int32, (t, LANES), 1)
        in_a = lane < FOX_D
        q = q_ref[...]
        zero = jnp.zeros_like(q)
        qh = [jnp.where(in_a, q, zero), jnp.where(in_a, zero, q)]
        c_ones = [jnp.where((lane >= 3 * z) & (lane < 3 * z + 3), 1.0, 0.0) for z in range(2)]
        dmask = _diag_mask(t)

        def keys(j):
            rows = pl.ds(pl.multiple_of(j * t, t), t)
            return jnp.concatenate([k_ref[rows, :], ka_ref[rows, :]], axis=1), rows

        def logits(qx, kk, masked):
            e = lax.dot_general(qx, kk, (NT, ((), ())), preferred_element_type=F32)
            return jnp.where(dmask, e, -1e30) if masked else e

        qc = [jnp.concatenate([qh[z], c_ones[z].astype(BF16)], axis=1) for z in range(2)]

        def step(j, carry, masked):
            kk, rows = keys(j)
            vj = v_ref[rows, :]
            scores = [logits(qc[z], kk, masked) for z in range(2)]
            one = jnp.ones_like(vj)
            vh = [jnp.where(in_a, vj, one), jnp.where(in_a, one, vj)]
            out = []
            for z in range(2):
                m, acc = carry[z]
                m_new = jnp.maximum(m, jnp.max(scores[z], axis=1, keepdims=True))
                p = jnp.exp(scores[z] - m_new)
                out.append((m_new, jnp.exp(m - m_new) * acc + _dot(p, vh[z], NN)))
            return tuple(out)

        init = tuple((jnp.full((t, 1), -1e30, F32), jnp.zeros((t, LANES), F32)) for _ in range(2))
        (ma, acc_a), (mb, acc_b) = step(i, lax.fori_loop(0, i, lambda j, c: step(j, c, False), init), True)
        la = jnp.sum(jnp.where(lane == FOX_D, acc_a, 0.0), axis=1, keepdims=True)
        lb = jnp.sum(jnp.where(lane == 0, acc_b, 0.0), axis=1, keepdims=True)
        o_ref[...] = jnp.where(in_a, acc_a / la, acc_b / lb).astype(o_ref.dtype)
        la_ref[...] = (_lane_put(lane, _pieces(-(ma + jnp.log(la))), AUX_ONES)
                       + _lane_put(lane, _pieces(-(mb + jnp.log(lb))), AUX_ONES + 3)).astype(la_ref.dtype)

    blk = pl.BlockSpec((t, LANES), lambda p, i: (i, p))
    whole = pl.BlockSpec((s, LANES), lambda p, i: (0, p))
    return pl.pallas_call(
        body, name="fox_attn_fwd", grid=(FOX_H // 2, nq), in_specs=[blk, whole, whole, whole],
        out_specs=[blk, blk], out_shape=[jax.ShapeDtypeStruct((s, D), BF16)] * 2,
        compiler_params=_cparams(("arbitrary", "arbitrary")),
    )(qb, kb, vb, ka)


def _fox_bwd2(qb, kb, vb, ka, ob, laux, dob):
    s = qb.shape[0]
    t = min(FOX_T, s)
    nq = s // t

    def body(q_ref, k_ref, v_ref, ka_ref, o_ref, la_ref, do_ref, dq_ref, dk_ref, dv_ref, dc_ref):
        i = pl.program_id(1)

        @pl.when(i == 0)
        def _():
            dk_ref[...] = jnp.zeros_like(dk_ref)
            dv_ref[...] = jnp.zeros_like(dv_ref)
            dc_ref[...] = jnp.zeros_like(dc_ref)

        lane = lax.broadcasted_iota(jnp.int32, (t, LANES), 1)
        in_a = lane < FOX_D
        q, do, la = q_ref[...], do_ref[...], la_ref[...].astype(F32)
        zero = jnp.zeros_like(q)
        qh = [jnp.where(in_a, q, zero), jnp.where(in_a, zero, q)]
        doh = [jnp.where(in_a, do, zero), jnp.where(in_a, zero, do)]
        prod = do.astype(F32) * o_ref[...].astype(F32)
        qx, dox = [], []
        for z in range(2):
            delta = jnp.sum(jnp.where(in_a if z == 0 else ~in_a, prod, 0.0), axis=1, keepdims=True)
            c_ones = jnp.where((lane >= 3 * z) & (lane < 3 * z + 3), 1.0, 0.0)
            lse_lanes = (lane >= AUX_ONES + 3 * z) & (lane < AUX_ONES + 3 * z + 3)
            qx.append(jnp.concatenate([qh[z], (c_ones + jnp.where(lse_lanes, la, 0.0)).astype(BF16)], axis=1))
            dox.append(jnp.concatenate([doh[z], _lane_put(lane, _pieces(-delta), 3 * z).astype(BF16)], axis=1))
        v_ones = jnp.where(lane < 6, 1.0, 0.0).astype(BF16)
        dmask = _diag_mask(t)

        def step(j, carry, masked):
            rows = pl.ds(pl.multiple_of(j * t, t), t)
            kj, vj = k_ref[rows, :], v_ref[rows, :]
            kk = jnp.concatenate([kj, ka_ref[rows, :]], axis=1)
            vv = jnp.concatenate([vj, v_ones], axis=1)
            out = []
            dk_add, dv_add = None, None
            for z in range(2):
                dq, rsum = carry[z]
                e = lax.dot_general(qx[z], kk, (NT, ((), ())), preferred_element_type=F32)
                if masked:
                    e = jnp.where(dmask, e, -1e30)
                p = jnp.exp(e)
                ds = p * lax.dot_general(dox[z], vv, (NT, ((), ())), preferred_element_type=F32)
                dkz, dvz = _dot(ds, qh[z], TN), _dot(p, doh[z], TN)
                dk_add = dkz if dk_add is None else dk_add + dkz
                dv_add = dvz if dv_add is None else dv_add + dvz
                dc_ref[0, z, j] += -jnp.sum(ds, axis=0, keepdims=True)
                out.append((dq + _dot(ds, kj, NN), rsum + jnp.sum(ds, axis=1, keepdims=True)))
            dk_ref[rows, :] += dk_add
            dv_ref[rows, :] += dv_add
            return tuple(out)

        init = tuple((jnp.zeros((t, LANES), F32), jnp.zeros((t, 1), F32)) for _ in range(2))
        (dq_a, rs_a), (dq_b, rs_b) = step(i, lax.fori_loop(0, i, lambda j, c: step(j, c, False), init), True)
        for z, rs in enumerate((rs_a, rs_b)):
            dc_ref[0, z, i] += jnp.transpose(jnp.broadcast_to(rs, (t, LANES)))[0:1]
        dq_ref[...] = (jnp.where(in_a, dq_a, dq_b) * 0.125).astype(dq_ref.dtype)

    blk = pl.BlockSpec((t, LANES), lambda p, i: (i, p))
    whole = pl.BlockSpec((s, LANES), lambda p, i: (0, p))
    return pl.pallas_call(
        body, name="fox_attn_bwd", grid=(FOX_H // 2, nq),
        in_specs=[blk, whole, whole, whole, blk, blk, blk],
        out_specs=[blk, whole, whole, pl.BlockSpec((1, 2, nq, 1, t), lambda p, i: (p, 0, 0, 0, 0))],
        out_shape=[jax.ShapeDtypeStruct((s, D), BF16), jax.ShapeDtypeStruct((s, D), F32),
                   jax.ShapeDtypeStruct((s, D), F32), jax.ShapeDtypeStruct((FOX_H // 2, 2, nq, 1, t), F32)],
        compiler_params=_cparams(("arbitrary", "arbitrary")),
    )(qb, kb, vb, ka, ob, laux, dob)


def _adamw(name, w, g, m, v, tm=None):
    rows, width = w.shape
    tm = rows if tm is None else tm
    c1 = 1.0 - ADAM_B1 ** ADAM_STEP
    c2 = 1.0 - ADAM_B2 ** ADAM_STEP

    def fn(wb, gb, mb, vb):
        m_new = ADAM_B1 * mb + (1.0 - ADAM_B1) * gb
        v_new = ADAM_B2 * vb + (1.0 - ADAM_B2) * (gb * gb)
        delta = -ADAM_LR * ((m_new / c1) / (jnp.sqrt(v_new / c2) + ADAM_EPS) + ADAM_WD * wb)
        return (delta, m_new, v_new), ()

    ins = [_rb(a, tm, width) for a in (w, g, m, v)]
    return _rows(name, fn, rows, tm, ins, [(width, F32)] * 3)


def _me():
    return lax.axis_index("x"), lax.axis_index("y"), lax.axis_index("c")


def _all_gather8(name, block):
    m, n = block.shape

    def body(x_ref, out_ref, send_sems, recv_sems):
        x, y, c = _me()
        me, sibling = (x, y, c), (x, y, 1 - c)
        chips = [(1 - x, y), (x, 1 - y), (1 - x, 1 - y)]

        def slot(px, py, pc):
            return out_ref.at[4 * px + 2 * py + pc]

        def copy(k, blk, to, src=None):
            return pltpu.make_async_remote_copy(
                src_ref=slot(*blk) if src is None else src, dst_ref=slot(*blk),
                send_sem=send_sems.at[k], recv_sem=recv_sems.at[k], device_id=to, device_id_type=MESH)

        first = [copy(0, me, sibling, src=x_ref)]
        first += [copy(1 + j, me, (*chip, c), src=x_ref) for j, chip in enumerate(chips)]
        for cp in first:
            cp.start()
        passed = [copy(4 + j, (*chip, c), sibling) for j, chip in enumerate(chips)]
        for j, chip in enumerate(chips):
            copy(1 + j, (*chip, c), me).wait_recv()
            passed[j].start()
        copy(0, sibling, me).wait_recv()
        for j, chip in enumerate(chips):
            copy(4 + j, (*chip, 1 - c), me).wait_recv()
        for cp in first + passed:
            cp.wait_send()

    gathered = pl.pallas_call(
        body, name=name, in_specs=[ANY], out_specs=ANY,
        out_shape=jax.ShapeDtypeStruct((8, m, n), block.dtype),
        scratch_shapes=[pltpu.SemaphoreType.DMA((7,)), pltpu.SemaphoreType.DMA((7,))],
    )(block)
    x, y, c = _me()
    return lax.dynamic_update_slice(gathered, block[None], (4 * x + 2 * y + c, 0, 0))


def _swap_halves(name, g):
    n, _, m, lanes = g.shape

    def body(g_ref, got_ref, send_sems, recv_sems):
        x, y, c = _me()
        copies = [pltpu.make_async_remote_copy(
            src_ref=g_ref.at[j, 1 - c], dst_ref=got_ref.at[j], send_sem=send_sems.at[j], recv_sem=recv_sems.at[j],
            device_id=(x, y, 1 - c), device_id_type=MESH) for j in range(n)]
        for cp in copies:
            cp.start()
        for cp in copies:
            cp.wait()

    return pl.pallas_call(
        body, name=name, in_specs=[ANY], out_specs=ANY, out_shape=jax.ShapeDtypeStruct((n, m, lanes), g.dtype),
        scratch_shapes=[pltpu.SemaphoreType.DMA((n,)), pltpu.SemaphoreType.DMA((n,))],
    )(g)


def _swap_sibling(name, mine):
    def body(m_ref, out_ref, send_sem, recv_sem):
        x, y, c = _me()
        cp = pltpu.make_async_remote_copy(src_ref=m_ref, dst_ref=out_ref, send_sem=send_sem, recv_sem=recv_sem,
                                          device_id=(x, y, 1 - c), device_id_type=MESH)
        cp.start()
        cp.wait()

    return pl.pallas_call(
        body, name=name, in_specs=[ANY], out_specs=ANY, out_shape=jax.ShapeDtypeStruct(mine.shape, mine.dtype),
        scratch_shapes=[pltpu.SemaphoreType.DMA, pltpu.SemaphoreType.DMA],
    )(mine)


def _chip_exchange(name, p):
    def body(p_ref, out_ref, send_sems, recv_sems):
        x, y, c = _me()
        my_chip = 2 * x + y
        chips = [(1 - x, y), (x, 1 - y), (1 - x, 1 - y)]
        sends = []
        for k, (px, py) in enumerate(chips):
            sends.append(pltpu.make_async_remote_copy(
                src_ref=p_ref.at[2 * px + py], dst_ref=out_ref.at[my_chip], send_sem=send_sems.at[k],
                recv_sem=recv_sems.at[k], device_id=(px, py, c), device_id_type=MESH))
        for cp in sends:
            cp.start()
        for k, (px, py) in enumerate(chips):
            pltpu.make_async_remote_copy(
                src_ref=p_ref.at[my_chip], dst_ref=out_ref.at[2 * px + py], send_sem=send_sems.at[k],
                recv_sem=recv_sems.at[k], device_id=(px, py, c), device_id_type=MESH).wait_recv()
        for cp in sends:
            cp.wait_send()

    got = pl.pallas_call(
        body, name=name, in_specs=[ANY], out_specs=ANY, out_shape=jax.ShapeDtypeStruct(p.shape, p.dtype),
        scratch_shapes=[pltpu.SemaphoreType.DMA((3,)), pltpu.SemaphoreType.DMA((3,))],
    )(p)
    x, y, _ = _me()
    my_chip = 2 * x + y
    return lax.dynamic_update_slice(got, lax.dynamic_index_in_dim(p, my_chip, axis=0, keepdims=True), (my_chip, 0, 0))


def _all_reduce_small(name, block):
    r, n = block.shape

    def body(x_ref, sum_ref, gath, send_sems, recv_sems):
        x, y, c = _me()
        me = 4 * x + 2 * y + c
        gath[me] = x_ref[...]
        sends = []
        for k in range(1, 8):
            px = x ^ ((k >> 2) & 1)
            py = y ^ ((k >> 1) & 1)
            pc = c ^ (k & 1)
            sends.append(pltpu.make_async_remote_copy(
                src_ref=x_ref, dst_ref=gath.at[me], send_sem=send_sems.at[k - 1], recv_sem=recv_sems.at[k - 1],
                device_id=(px, py, pc), device_id_type=MESH))
        for cp in sends:
            cp.start()
        for k in range(1, 8):
            peer = me ^ k
            pltpu.make_async_remote_copy(
                src_ref=x_ref, dst_ref=gath.at[peer], send_sem=send_sems.at[k - 1], recv_sem=recv_sems.at[k - 1],
                device_id=(x, y, c), device_id_type=MESH).wait_recv()
        for cp in sends:
            cp.wait_send()
        acc = gath[0]
        for d in range(1, 8):
            acc = acc + gath[d]
        sum_ref[...] = acc

    vm = pl.BlockSpec(memory_space=pltpu.VMEM)
    return pl.pallas_call(
        body, name=name, in_specs=[vm], out_specs=vm, out_shape=jax.ShapeDtypeStruct((r, n), F32),
        scratch_shapes=[pltpu.VMEM((8, r, n), F32), pltpu.SemaphoreType.DMA((7,)), pltpu.SemaphoreType.DMA((7,))],
    )(block)


def _add2(name, a, b, tm):
    rows = a.shape[0]
    return _rows(name, lambda p, q: ((p + q,), ()), rows, tm, [_rb(a, tm, LANES), _rb(b, tm, LANES)], [(LANES, BF16)])[0]


def _add4(name, p, tm):
    m = p.shape[1]
    flat = p.reshape(4 * m, LANES)
    nb = m // tm
    ins = [(flat, (tm, LANES), (lambda i, j=j: (j * nb + i, 0))) for j in range(4)]
    f32 = lambda v: v.astype(F32)
    return _rows(name, lambda a, b, c, d: ((((f32(a) + f32(b)) + f32(c)) + f32(d),), ()), m, tm, ins, [(LANES, F32)])[0]


SEG_ROWS = (D * W_IN_SHARD // LANES, 256 * D // LANES, 256 * D // LANES, 256 * D // LANES,
            D * W_UP_SHARD // LANES, W_DOWN_SHARD * D // LANES)
GRAD_ROWS = sum(SEG_ROWS)
CONVW_ROWS = 3 * W_UP_SHARD * 2 // LANES
GATHER_ROWS = 41600


def _flat(a):
    return a.reshape(-1, LANES)


def _gather_weights(w_in, w_a, w_b, w_out, w_up, w_down, conv_w):
    c = lax.axis_index("c")
    bits = lax.bitcast_convert_type(conv_w, BF16)
    pieces = [_flat(t.astype(BF16)) for t in (w_in, w_a, w_b, w_out, w_up, w_down)] + [_flat(bits)]
    pad = GATHER_ROWS - GRAD_ROWS - CONVW_ROWS
    shard = jnp.concatenate(pieces + [jnp.zeros((pad, LANES), BF16)], axis=0)
    half = GATHER_ROWS // 2
    mine = lax.dynamic_slice_in_dim(shard, c * half, half, axis=0)
    full = _all_gather8("all_gather_weights", mine).reshape(N_CHIP, GATHER_ROWS, LANES)
    offs = [0]
    for r in SEG_ROWS:
        offs.append(offs[-1] + r)
    seg = lambda i: full[:, offs[i]:offs[i + 1]]
    wi = seg(0).reshape(N_CHIP, D, W_IN_SHARD).transpose(1, 0, 2).reshape(D, N_CHIP * W_IN_SHARD)
    w_main = jnp.concatenate([wi[:, :FF_COL], wi[:, FF_COL + FOX_H:]], axis=1)
    w_ff = jnp.pad(wi[:, FF_COL:FF_COL + FOX_H], ((0, 0), (0, LANES - FOX_H)))
    wa, wb, wo = (seg(i).reshape(D, D) for i in (1, 2, 3))
    wu = seg(4).reshape(N_CHIP, D, W_UP_SHARD).transpose(1, 0, 2).reshape(D, 2 * D_FF)
    wd = seg(5).reshape(D_FF, D)
    cw_bits = full[:, GRAD_ROWS:GRAD_ROWS + CONVW_ROWS].reshape(N_CHIP, 3, W_UP_SHARD, 2)
    cw = lax.bitcast_convert_type(cw_bits, F32).transpose(1, 0, 2).reshape(3, 2 * D_FF)
    return w_main, w_ff, wa, wb, wo, wu, wd, cw


def _reduce_scatter_grads(d_main, d_ff, d_a, d_b, d_o, d_u, d_d):
    c = lax.axis_index("c")
    d_in = jnp.concatenate(d_main[:7] + [d_ff[:, :FOX_H]] + d_main[7:], axis=1)
    per_chip = [
        d_in.reshape(D, N_CHIP, W_IN_SHARD).transpose(1, 0, 2).reshape(N_CHIP, -1, LANES),
        d_a.reshape(N_CHIP, -1, LANES), d_b.reshape(N_CHIP, -1, LANES), d_o.reshape(N_CHIP, -1, LANES),
        d_u.reshape(D, N_CHIP, W_UP_SHARD).transpose(1, 0, 2).reshape(N_CHIP, -1, LANES),
        d_d.reshape(N_CHIP, -1, LANES),
        jnp.zeros((N_CHIP, GATHER_ROWS - GRAD_ROWS, LANES), F32),
    ]
    half = GATHER_ROWS // 2
    g = jnp.concatenate(per_chip, axis=1).reshape(N_CHIP, 2, half, LANES)
    from_sibling = _swap_halves("grad_swap_halves", g)
    mine = lax.dynamic_index_in_dim(g, c, axis=1, keepdims=False)
    tm = half // 5
    chip_sum = _add2("grad_chip_sum", mine.reshape(-1, LANES), from_sibling.reshape(-1, LANES), tm)
    pieces = _chip_exchange("grad_chip_exchange", chip_sum.reshape(N_CHIP, half, LANES))
    mine_half = _add4("grad_sum_chips", pieces, tm)
    other_half = _swap_sibling("grad_share_half", mine_half)
    lo = jnp.where(c == 0, mine_half, other_half)
    hi = jnp.where(c == 0, other_half, mine_half)
    return jnp.concatenate([lo, hi], axis=0)


def _local_step(x, target, norm_mix, fox_f_bias, hg_lb_logits, hg_norm, norm_ffn, conv_b, norm_final,
                w_main, w_ff, wa, wb, wo, wu, wd, conv_w):
    s = x.shape[0]
    bias = jnp.pad(fox_f_bias, ((0, 0), (0, LANES - FOX_H)))
    conv_w8 = jnp.pad(conv_w, ((0, 5), (0, 0)))
    t = min(FOX_T, s)

    n1 = _rms_fwd("norm_mix_fwd", x, norm_mix)
    proj = _mm("in_proj", n1, w_main, "nn", F32, 1024, 1024, D)
    pff = _mm("in_proj_forget", n1, w_ff, "nn", F32, 1024, LANES, D)
    qb, kb, vb, ka = _fox_prep2(proj, pff, bias)
    o_b, laux = _fox_fwd2(qb, kb, vb, ka)
    o_raw, states = _hg_fwd(proj, hg_lb_logits)
    o_a = _hg_post_fwd(o_raw, proj, hg_norm)
    pa = _mm("branch_a", o_a, wa, "nn", F32, 1024, 1024, D)
    pb = _mm("branch_b", o_b, wb, "nn", F32, 1024, 1024, D)
    merged = _merge_fwd(pa, pb, proj)
    h1 = _mm("out_proj", merged, wo, "nn", F32, 1024, 1024, D, res=x)
    n2 = _rms_fwd("norm_ffn_fwd", h1, norm_ffn)
    u = _mm("ffn_up", n2, wu, "nn", F32, 1024, W_UP_SHARD, D)
    act = _convglu_fwd(u, conv_w8, conv_b)
    h2 = _mm("ffn_down", act, wd, "nn", F32, 512, 1024, D_FF, res=h1)
    (dh2,), (d_norm_final, loss_row) = _final(h2, target, norm_final)

    dact = _mm("ffn_down_dx", dh2, wd, "nt", BF16, 1024, D_FF, D)
    d_wd = _mm("ffn_down_dw", act, dh2, "tn", F32, D_FF // 2, 1024, 512)
    (du,), (d_conv_w8, d_conv_b) = _convglu_bwd(u, dact, conv_w8, conv_b)
    dn2 = _mm("ffn_up_dx", du, wu, "nt", F32, 1024, 1024, W_UP_SHARD)
    d_wu = _mm("ffn_up_dw", n2, du, "tn", F32, 1024, W_UP_SHARD, 512)
    (dh1,), (d_norm_ffn,) = _rms_bwd("norm_ffn_bwd", h1, norm_ffn, [dn2], dh2)

    dmerged = _mm("out_proj_dx", dh1, wo, "nt", F32, 1024, 1024, D)
    d_wo = _mm("out_proj_dw", merged, dh1, "tn", F32, 1024, 1024, 512)
    dpa, dpb, dga, dgb = _merge_bwd(dmerged, pa, pb, proj)
    do_a = _mm("branch_a_dx", dpa, wa, "nt", F32, 1024, 1024, D)
    do_b = _mm("branch_b_dx", dpb, wb, "nt", BF16, 1024, 1024, D)
    d_wa = _mm("branch_a_dw", o_a, dpa, "tn", F32, 1024, 1024, 512)
    d_wb = _mm("branch_b_dw", o_b, dpb, "tn", F32, 1024, 1024, 512)

    (do_raw, dhg), (d_hg_norm,) = _hg_post_bwd(do_a, o_raw, proj, hg_norm)
    dhq, dhf, dhi, d_lb_logits = _hg_bwd(proj, hg_lb_logits, states, do_raw)

    dfq, dfk, dfv, dcrow = _fox_bwd2(qb, kb, vb, ka, o_b, laux, do_b)
    dct = jnp.pad(dcrow.reshape(FOX_H, s), ((0, LANES - FOX_H), (0, 0)))
    dff, d_bias = _fox_gate_bwd(dct, pff, bias)

    pieces = [dhq, dhf, dhi, dhg, dfq, dfk, dfv, dga, dgb]
    dn1 = _mm_sum_nt("in_proj_dx", pieces, w_main, (dff, w_ff), 512, 1024)
    d_w_main = [_mm("in_proj_dw_%d" % i, n1, p, "tn", F32, 1024, 1024, 512) for i, p in enumerate(pieces)]
    d_w_ff = _mm("in_proj_forget_dw", n1, dff, "tn", F32, 1024, LANES, 512)
    (dx,), (d_norm_mix,) = _rms_bwd("norm_mix_bwd", x, norm_mix, [dn1], dh1)

    small = dict(norm_mix=d_norm_mix, fox_f_bias=d_bias[:, :FOX_H], hg_lb_logits=d_lb_logits, hg_norm=d_hg_norm,
                 norm_ffn=d_norm_ffn, conv_b=d_conv_b, norm_final=d_norm_final, conv_w=d_conv_w8[:3], loss=loss_row)
    big = (d_w_main, d_w_ff, d_wa, d_wb, d_wo, d_wu, d_wd)
    return dx, small, big


SMALL_KEYS = ("norm_mix", "fox_f_bias", "hg_lb_logits", "hg_norm", "norm_ffn", "conv_b", "norm_final")


def _pack_small(parts):
    rows, layout = [], []
    for key, arr in parts:
        flat = arr.reshape(-1)
        n = flat.shape[0]
        nr = -(-n // LANES)
        rows.append(jnp.pad(flat, (0, nr * LANES - n)).reshape(nr, LANES))
        layout.append((key, arr.shape, n, nr))
    packed = jnp.concatenate(rows, axis=0)
    pad = -packed.shape[0] % 8
    return jnp.pad(packed, ((0, pad), (0, 0))), layout


def _unpack_small(packed, layout):
    out, r0 = {}, 0
    for key, shape, n, nr in layout:
        out[key] = packed[r0:r0 + nr].reshape(-1)[:n].reshape(shape)
        r0 += nr
    return out


def kernel(x, norm_mix, w_in, fox_f_bias, hg_lb_logits, hg_norm, w_branch_a, w_branch_b, w_out, norm_ffn, w_up, conv_w, conv_b, w_down, norm_final, loss_target, m_norm_mix, m_w_in, m_fox_f_bias, m_hg_lb_logits, m_hg_norm, m_w_branch_a, m_w_branch_b, m_w_out, m_norm_ffn, m_w_up, m_conv_w, m_conv_b, m_w_down, m_norm_final, v_norm_mix, v_w_in, v_fox_f_bias, v_hg_lb_logits, v_hg_norm, v_w_branch_a, v_w_branch_b, v_w_out, v_norm_ffn, v_w_up, v_conv_w, v_conv_b, v_w_down, v_norm_final):
    chip = 2 * lax.axis_index("x") + lax.axis_index("y")
    w_main, w_ff, wa, wb, wo, wu, wd, cw = _gather_weights(
        w_in[0], w_branch_a[0], w_branch_b[0], w_out[0], w_up[0], w_down[0], conv_w[0])
    dx, small, big = _local_step(
        x[0], loss_target[0], norm_mix, fox_f_bias, hg_lb_logits, hg_norm, norm_ffn, conv_b,
        norm_final.reshape(1, D), w_main, w_ff, wa, wb, wo, wu, wd, cw)

    packed, layout = _pack_small([(k, small[k]) for k in SMALL_KEYS + ("conv_w", "loss")])
    red = _unpack_small(_all_reduce_small("all_reduce_small", packed), layout)
    loss = red["loss"][0, 0]
    g_conv_w = lax.dynamic_slice_in_dim(red["conv_w"], chip * W_UP_SHARD, W_UP_SHARD, axis=1)

    gflat = _reduce_scatter_grads(*big)
    offs = [0]
    for r in SEG_ROWS:
        offs.append(offs[-1] + r)
    shapes = [(D, W_IN_SHARD), (256, D), (256, D), (256, D), (D, W_UP_SHARD), (W_DOWN_SHARD, D)]
    g_big = [gflat[offs[i]:offs[i + 1]].reshape(shapes[i]) for i in range(6)]

    names = ["norm_mix", "w_in", "fox_f_bias", "hg_lb_logits", "hg_norm", "w_branch_a", "w_branch_b", "w_out",
             "norm_ffn", "w_up", "conv_w", "conv_b", "w_down", "norm_final"]
    weights = dict(norm_mix=norm_mix, w_in=w_in, fox_f_bias=fox_f_bias, hg_lb_logits=hg_lb_logits, hg_norm=hg_norm,
                   w_branch_a=w_branch_a, w_branch_b=w_branch_b, w_out=w_out, norm_ffn=norm_ffn, w_up=w_up,
                   conv_w=conv_w, conv_b=conv_b, w_down=w_down, norm_final=norm_final)
    ms = dict(norm_mix=m_norm_mix, w_in=m_w_in, fox_f_bias=m_fox_f_bias, hg_lb_logits=m_hg_lb_logits,
              hg_norm=m_hg_norm, w_branch_a=m_w_branch_a, w_branch_b=m_w_branch_b, w_out=m_w_out,
              norm_ffn=m_norm_ffn, w_up=m_w_up, conv_w=m_conv_w, conv_b=m_conv_b, w_down=m_w_down,
              norm_final=m_norm_final)
    vs = dict(norm_mix=v_norm_mix, w_in=v_w_in, fox_f_bias=v_fox_f_bias, hg_lb_logits=v_hg_lb_logits,
              hg_norm=v_hg_norm, w_branch_a=v_w_branch_a, w_branch_b=v_w_branch_b, w_out=v_w_out,
              norm_ffn=v_norm_ffn, w_up=v_w_up, conv_w=v_conv_w, conv_b=v_conv_b, w_down=v_w_down,
              norm_final=v_norm_final)

    grads, deltas, new_m, new_v = {}, {}, {}, {}
    big_names = ["w_in", "w_branch_a", "w_branch_b", "w_out", "w_up", "w_down"]
    for name, g2 in zip(big_names, g_big):
        shape = weights[name].shape
        rows = g2.shape[0]
        d_, m_, v_ = _adamw("adamw_" + name, weights[name][0], g2, ms[name][0], vs[name][0], tm=rows // 8)
        grads[name], deltas[name], new_m[name], new_v[name] = (a.reshape(shape) for a in (g2, d_, m_, v_))
    shape = conv_w.shape
    d_, m_, v_ = _adamw("adamw_conv_w", conv_w[0], g_conv_w, m_conv_w[0], v_conv_w[0])
    grads["conv_w"], deltas["conv_w"], new_m["conv_w"], new_v["conv_w"] = (
        a.reshape(shape) for a in (g_conv_w, d_, m_, v_))
    gs = {k: red[k].reshape(weights[k].shape) for k in SMALL_KEYS}
    pw, lay = _pack_small([(k, weights[k]) for k in SMALL_KEYS])
    pg, _ = _pack_small([(k, gs[k]) for k in SMALL_KEYS])
    pm, _ = _pack_small([(k, ms[k]) for k in SMALL_KEYS])
    pv, _ = _pack_small([(k, vs[k]) for k in SMALL_KEYS])
    d_, m_, v_ = (_unpack_small(a, lay) for a in _adamw("adamw_small", pw, pg, pm, pv))
    for k in SMALL_KEYS:
        grads[k], deltas[k], new_m[k], new_v[k] = gs[k], d_[k], m_[k], v_[k]

    return (loss, dx[None], *[grads[n] for n in names], *[deltas[n] for n in names],
            *[new_m[n] for n in names], *[new_v[n] for n in names])
```

```python
import functools

import jax
import jax.numpy as jnp
from jax import lax
from jax.experimental import pallas as pl
from jax.experimental.pallas import tpu as pltpu

F32 = jnp.float32
BF16 = jnp.bfloat16

D = 1024
HG_H, HG_DK = 8, 128
FOX_H, FOX_D = 16, 64
D_FF = 2816
EPS = 1e-6
N_CHIP = 4
LANES = 128
W_IN_SHARD = 2308
W_UP_SHARD = 1408
W_DOWN_SHARD = 704
FF_COL = 7168
ADAM_LR, ADAM_B1, ADAM_B2, ADAM_EPS, ADAM_WD, ADAM_STEP = 0.001, 0.9, 0.999, 1e-08, 0.01, 10

HG_C = 16
HG_T = 256
HG_UNROLL = 8
HG_UNROLL_BWD = 4
FOX_T = 512
VMEM_LIMIT = 56 * 1024 * 1024
MESH = pl.DeviceIdType.MESH
ANY = pl.BlockSpec(memory_space=pl.ANY)


def _cparams(sem):
    return pltpu.CompilerParams(dimension_semantics=sem, vmem_limit_bytes=VMEM_LIMIT)


def _sigmoid(x):
    return 1.0 / (1.0 + jnp.exp(-x))


def _dot(a, b, dims):
    return lax.dot_general(a.astype(BF16), b.astype(BF16), (dims, ((), ())), preferred_element_type=F32)


NN = ((1,), (0,))
NT = ((1,), (1,))
TN = ((0,), (0,))


def _split_dot(tri, x, parts, dims=NN):
    acc = None
    r = x
    for _ in range(parts):
        p = r.astype(BF16)
        t = lax.dot_general(tri, p, (dims, ((), ())), preferred_element_type=F32)
        acc = t if acc is None else acc + t
        r = r - p.astype(F32)
    return acc


def _rb(arr, tm, width, cb=0):
    return (arr, (tm, width), lambda i: (i, cb))


def _cst(arr):
    return (arr, arr.shape, lambda i: (0,) * arr.ndim)


def _rows(name, fn, n_rows, tm, ins, outs, accs=(), reverse=False):
    n_in, n_out, n_acc = len(ins), len(outs), len(accs)
    nb = n_rows // tm

    def body(*refs):
        vals = [r[...] for r in refs[:n_in]]
        o, a = fn(*vals)
        for r, v in zip(refs[n_in:n_in + n_out], o):
            r[...] = v.astype(r.dtype)
        if n_acc:
            acc_refs = refs[n_in + n_out:]

            @pl.when(pl.program_id(0) == 0)
            def _():
                for r in acc_refs:
                    r[...] = jnp.zeros_like(r)

            for r, v in zip(acc_refs, a):
                r[...] += v

    if reverse:
        rowmap = lambda i: (nb - 1 - i, 0)
    else:
        rowmap = lambda i: (i, 0)
    in_specs = [pl.BlockSpec(bs, im) for (_, bs, im) in ins]
    out_specs = [pl.BlockSpec((tm, w), rowmap) for (w, _) in outs]
    out_specs += [pl.BlockSpec((r, w), lambda i: (0, 0)) for (r, w) in accs]
    out_shape = [jax.ShapeDtypeStruct((n_rows, w), dt) for (w, dt) in outs]
    out_shape += [jax.ShapeDtypeStruct((r, w), F32) for (r, w) in accs]
    res = pl.pallas_call(
        body, name=name, grid=(nb,), in_specs=in_specs, out_specs=out_specs, out_shape=out_shape,
        compiler_params=_cparams(("arbitrary",)),
    )(*[a for a, _, _ in ins])
    return (res[:n_out], res[n_out:]) if n_acc else res


def _mm(name, a, b, mode, out_dtype, tm, tn, tk, res=None):
    if mode == "nn":
        (m, k), n = a.shape, b.shape[1]
    elif mode == "nt":
        (m, k), n = a.shape, b.shape[0]
    else:
        (k, m), n = a.shape, b.shape[1]
    tm, tn, tk = min(tm, m), min(tn, n), min(tk, k)
    assert m % tm == 0 and n % tn == 0 and k % tk == 0, (name, m, n, k, tm, tn, tk)
    if mode == "nn":
        a_spec = pl.BlockSpec((tm, tk), lambda i, j, kk: (i, kk))
        b_spec = pl.BlockSpec((tk, tn), lambda i, j, kk: (kk, j))
        dims = NN
    elif mode == "nt":
        a_spec = pl.BlockSpec((tm, tk), lambda i, j, kk: (i, kk))
        b_spec = pl.BlockSpec((tn, tk), lambda i, j, kk: (j, kk))
        dims = NT
    else:
        a_spec = pl.BlockSpec((tk, tm), lambda i, j, kk: (kk, i))
        b_spec = pl.BlockSpec((tk, tn), lambda i, j, kk: (kk, j))
        dims = TN
    nk = k // tk
    has_res = res is not None

    def body(*refs):
        a_ref, b_ref = refs[0], refs[1]
        r_ref = refs[2] if has_res else None
        o_ref = refs[3] if has_res else refs[2]
        part = _dot(a_ref[...], b_ref[...], dims)

        def finish(val):
            if has_res:
                val = val + r_ref[...]
            o_ref[...] = val.astype(o_ref.dtype)

        if nk == 1:
            finish(part)
        else:
            acc_ref = refs[-1]
            kk = pl.program_id(2)

            @pl.when(kk == 0)
            def _():
                acc_ref[...] = part

            @pl.when(kk > 0)
            def _():
                acc_ref[...] += part

            @pl.when(kk == nk - 1)
            def _():
                finish(acc_ref[...])

    in_specs = [a_spec, b_spec]
    args = [a, b]
    if has_res:
        in_specs.append(pl.BlockSpec((tm, tn), lambda i, j, kk: (i, j)))
        args.append(res)
    return pl.pallas_call(
        body, name=name, grid=(m // tm, n // tn, nk), in_specs=in_specs,
        out_specs=pl.BlockSpec((tm, tn), lambda i, j, kk: (i, j)),
        out_shape=jax.ShapeDtypeStruct((m, n), out_dtype),
        scratch_shapes=[pltpu.VMEM((tm, tn), F32)] if nk > 1 else [],
        compiler_params=_cparams(("arbitrary", "arbitrary", "arbitrary")),
    )(*args)


def _mm_sum_nt(name, pieces, w, extra, tm, tn):
    n_p = len(pieces)
    m, k = pieces[0].shape
    n = w.shape[0]
    xa, xb = extra
    ke = xa.shape[1]
    tm, tn = min(tm, m), min(tn, n)

    def body(*refs):
        p_refs, w_ref, xa_ref, xb_ref, o_ref, acc_ref = refs[:n_p], refs[n_p], refs[n_p + 1], refs[n_p + 2], refs[-2], refs[-1]
        kk = pl.program_id(2)

        @pl.when(kk == 0)
        def _():
            acc_ref[...] = jnp.zeros_like(acc_ref)

        for i in range(n_p):
            @pl.when(kk == i)
            def _(i=i):
                acc_ref[...] += _dot(p_refs[i][...], w_ref[...], NT)

        @pl.when(kk == n_p)
        def _():
            o_ref[...] = acc_ref[...] + _dot(xa_ref[...], xb_ref[...], NT)

    in_specs = [pl.BlockSpec((tm, k), lambda i, j, kk: (i, 0)) for _ in range(n_p)]
    in_specs.append(pl.BlockSpec((tn, k), lambda i, j, kk: (j, jnp.minimum(kk, n_p - 1))))
    in_specs += [pl.BlockSpec((tm, ke), lambda i, j, kk: (i, 0)), pl.BlockSpec((tn, ke), lambda i, j, kk: (j, 0))]
    return pl.pallas_call(
        body, name=name, grid=(m // tm, n // tn, n_p + 1), in_specs=in_specs,
        out_specs=pl.BlockSpec((tm, tn), lambda i, j, kk: (i, j)),
        out_shape=jax.ShapeDtypeStruct((m, n), F32),
        scratch_shapes=[pltpu.VMEM((tm, tn), F32)],
        compiler_params=_cparams(("arbitrary", "arbitrary", "arbitrary")),
    )(*pieces, w, xa, xb)


def _rms_fwd(name, x, gain, tm=256):
    s = x.shape[0]

    def fn(xb, g):
        r = lax.rsqrt(jnp.mean(xb * xb, axis=-1, keepdims=True) + EPS)
        return (xb * r * g,), ()

    return _rows(name, fn, s, tm, [_rb(x, tm, D), _cst(gain)], [(D, BF16)])[0]


def _rms_bwd(name, x, gain, dns, dres, tm=256):
    s = x.shape[0]
    n_dn = len(dns)

    def fn(xb, g, *rest):
        dn = rest[0]
        for t in rest[1:n_dn]:
            dn = dn + t
        r = lax.rsqrt(jnp.mean(xb * xb, axis=-1, keepdims=True) + EPS)
        xhat = xb * r
        dxh = dn * g
        dx = r * (dxh - xhat * jnp.mean(dxh * xhat, axis=-1, keepdims=True)) + rest[n_dn]
        return (dx,), (jnp.sum(dn * xhat, axis=0, keepdims=True),)

    ins = [_rb(x, tm, D), _cst(gain)] + [_rb(t, tm, D) for t in dns] + [_rb(dres, tm, D)]
    return _rows(name, fn, s, tm, ins, [(D, F32)], [(1, D)])


def _final(h2, target, gain, tm=256):
    s = h2.shape[0]

    def fn(hb, tb, g):
        r = lax.rsqrt(jnp.mean(hb * hb, axis=-1, keepdims=True) + EPS)
        xhat = hb * r
        e = xhat * g - tb
        dy = e * (1.0 / D)
        dxh = dy * g
        dh = r * (dxh - xhat * jnp.mean(dxh * xhat, axis=-1, keepdims=True))
        lrow = 0.5 * jnp.sum(jnp.sum(e * e, axis=-1, keepdims=True) * (1.0 / D), axis=0, keepdims=True)
        return (dh,), (jnp.sum(dy * xhat, axis=0, keepdims=True), jnp.broadcast_to(lrow, (1, LANES)))

    return _rows("final_norm_loss", fn, s, tm, [_rb(h2, tm, D), _rb(target, tm, D), _cst(gain)],
                 [(D, F32)], [(1, D), (1, LANES)])


def _merge_fwd(pa, pb, proj, tm=256):
    s = pa.shape[0]

    def fn(a, b, ga, gb):
        return (_sigmoid(ga) * a + _sigmoid(gb) * b,), ()

    ins = [_rb(pa, tm, D), _rb(pb, tm, D), _rb(proj, tm, D, 7), _rb(proj, tm, D, 8)]
    return _rows("merge_fwd", fn, s, tm, ins, [(D, BF16)])[0]


def _merge_bwd(dmerged, pa, pb, proj, tm=256):
    s = pa.shape[0]

    def fn(dm, a, b, ga, gb):
        sa, sb = _sigmoid(ga), _sigmoid(gb)
        return (dm * sa, dm * sb, dm * a * sa * (1.0 - sa), dm * b * sb * (1.0 - sb)), ()

    ins = [_rb(dmerged, tm, D), _rb(pa, tm, D), _rb(pb, tm, D), _rb(proj, tm, D, 7), _rb(proj, tm, D, 8)]
    return _rows("merge_bwd", fn, s, tm, ins, [(D, BF16), (D, BF16), (D, BF16), (D, BF16)])


def _gelu_parts(x):
    cdf = 0.5 * (1.0 + lax.erf(x * 0.7071067811865476))
    pdf = 0.3989422804014327 * jnp.exp(-0.5 * x * x)
    return x * cdf, cdf + x * pdf


def _conv_taps(u_ext, n_out, first):
    n = u_ext.shape[0]
    cur = u_ext[8:8 + n_out]
    m1 = pltpu.roll(u_ext, 1, 0)[8:8 + n_out]
    m2 = pltpu.roll(u_ext, 2, 0)[8:8 + n_out]
    return m2, m1, cur


def _convglu_fwd(u, conv_w8, conv_b, tm=64):
    s, w = u.shape
    tb = tm // 8

    def fn(ub, up, cw, cb):
        i = pl.program_id(0)
        up = jnp.where(i == 0, 0.0, up)
        m2, m1, cur = _conv_taps(jnp.concatenate([up, ub], axis=0), tm, None)
        acc = cb + cw[0:1] * m2 + cw[1:2] * m1 + cw[2:3] * cur
        act, _ = _gelu_parts(acc[:, :D_FF])
        return (act * acc[:, D_FF:],), ()

    ins = [_rb(u, tm, w), (u, (8, w), lambda i: (jnp.maximum(i * tb - 1, 0), 0)), _cst(conv_w8), _cst(conv_b)]
    return _rows("convglu_fwd", fn, s, tm, ins, [(D_FF, BF16)])[0]


def _convglu_bwd(u, dact, conv_w8, conv_b, tm=64):
    s, w = u.shape
    tb = tm // 8
    nb = s // tm

    def fn(ub, up, un, db, dn, cw, cb):
        i = pl.program_id(0)
        up = jnp.where(i == 0, 0.0, up)
        dn = jnp.where(i == nb - 1, 0.0, dn)
        ne = tm + 8
        m2, m1, cur = _conv_taps(jnp.concatenate([up, ub, un], axis=0), ne, None)
        acc = cb + cw[0:1] * m2 + cw[1:2] * m1 + cw[2:3] * cur
        de = jnp.concatenate([db, dn], axis=0)
        gl, dgl = _gelu_parts(acc[:, :D_FF])
        dacc = jnp.concatenate([de * acc[:, D_FF:] * dgl, de * gl], axis=1)
        p1 = pltpu.roll(dacc, ne - 1, 0)[:tm]
        p2 = pltpu.roll(dacc, ne - 2, 0)[:tm]
        d0 = dacc[:tm]
        du = cw[2:3] * d0 + cw[1:2] * p1 + cw[0:1] * p2
        zero5 = jnp.zeros((5, w), F32)
        dcw = jnp.concatenate([
            jnp.sum(d0 * m2[:tm], axis=0, keepdims=True), jnp.sum(d0 * m1[:tm], axis=0, keepdims=True),
            jnp.sum(d0 * cur[:tm], axis=0, keepdims=True), zero5], axis=0)
        return (du,), (dcw, jnp.sum(d0, axis=0, keepdims=True))

    ins = [
        _rb(u, tm, w),
        (u, (8, w), lambda i: (jnp.maximum(i * tb - 1, 0), 0)),
        (u, (8, w), lambda i: (jnp.minimum((i + 1) * tb, s // 8 - 1), 0)),
        _rb(dact, tm, D_FF),
        (dact, (8, D_FF), lambda i: (jnp.minimum((i + 1) * tb, s // 8 - 1), 0)),
        _cst(conv_w8), _cst(conv_b),
    ]
    return _rows("convglu_bwd", fn, s, tm, ins, [(w, BF16)], [(8, w), (1, w)])


def _chunk_scan(x, t_iota, reverse):
    k = 1
    while k < HG_C:
        if reverse:
            x = x + jnp.where(t_iota < HG_C - k, pltpu.roll(x, HG_C - k, 0), 0.0)
        else:
            x = x + jnp.where(t_iota >= k, pltpu.roll(x, k, 0), 0.0)
        k *= 2
    return x


def _hg_gates(hq, hf, lb):
    sq = _sigmoid(hq)
    q = hq * sq
    sg = _sigmoid(hf)
    f = lb + (1.0 - lb) * sg
    return q, sq, sg, f, 1.0 - f, jnp.log(f)


def _lb_of(logits):
    l0, l1 = logits[0:1], logits[1:2]
    mx = jnp.maximum(l0, l1)
    e0, e1 = jnp.exp(l0 - mx), jnp.exp(l1 - mx)
    return e0 / (e0 + e1)


def _tri(n, lower):
    r = lax.broadcasted_iota(jnp.int32, (n, n), 0)
    c = lax.broadcasted_iota(jnp.int32, (n, n), 1)
    return jnp.where((r >= c) if lower else (r <= c), 1.0, 0.0).astype(BF16)


def _hg_intra_terms(q, kk, b, t_iota):
    ws, ps = [], []
    for s in range(HG_C):
        p = jnp.where(t_iota >= s, jnp.exp(b - b[s:s + 1]), 0.0)
        ps.append(p)
        ws.append(q * kk[s:s + 1] * p)
    return jnp.concatenate(ws, axis=0), ps


def _hg_fwd(proj, lb_logits):
    s = proj.shape[0]
    nt = s // HG_T
    nc = HG_T // HG_C

    def body(q_ref, f_ref, i_ref, l_ref, o_ref, st_ref, state):
        @pl.when(pl.program_id(1) == 0)
        def _():
            state[...] = jnp.zeros_like(state)

        st_ref[0, 0] = state[...]
        lb = _lb_of(l_ref[...])
        ones = jnp.ones((HG_DK, HG_DK), BF16)
        t_iota = lax.broadcasted_iota(jnp.int32, (HG_C, HG_DK), 0)
        cc = HG_C * HG_C

        def group(gi, st):
            units = []
            for u in range(HG_UNROLL):
                r = pl.ds(pl.multiple_of((gi * HG_UNROLL + u) * HG_C, HG_C), HG_C)
                q, _, _, _, kk, g = _hg_gates(q_ref[r, :], f_ref[r, :], lb)
                b = _chunk_scan(g, t_iota, False)
                b_end = b[HG_C - 1:HG_C]
                w_all, _ = _hg_intra_terms(q, kk, b, t_iota)
                units.append((r, i_ref[r, :], q * jnp.exp(b), jnp.exp(b_end), kk * jnp.exp(b_end - b), w_all))
            a_all = _dot(jnp.concatenate([un[5] for un in units], axis=0), ones, NN)
            kvs = [_dot(v, kd, TN) for (_, v, _, _, kd, _) in units]
            sts = [st]
            for (_, _, _, dec, _, _), kv in zip(units, kvs):
                sts.append(sts[-1] * dec + kv)
            for ui, (r, v, qd, _, _, _) in enumerate(units):
                o = _dot(qd, sts[ui], NT)
                for si in range(HG_C):
                    o = o + a_all[ui * cc + si * HG_C:ui * cc + (si + 1) * HG_C] * v[si:si + 1]
                o_ref[r, :] = o
            return sts[-1]

        state[...] = lax.fori_loop(0, nc // HG_UNROLL, group, state[...])

    col = lambda off: pl.BlockSpec((HG_T, HG_DK), lambda h, t: (t, off + h))
    return pl.pallas_call(
        body, name="hgrn2_fwd", grid=(HG_H, nt),
        in_specs=[col(0), col(8), col(16), pl.BlockSpec((2, HG_DK), lambda h, t: (0, h))],
        out_specs=[pl.BlockSpec((HG_T, HG_DK), lambda h, t: (t, h)),
                   pl.BlockSpec((1, 1, HG_DK, HG_DK), lambda h, t: (h, t, 0, 0))],
        out_shape=[jax.ShapeDtypeStruct((s, D), F32), jax.ShapeDtypeStruct((HG_H, nt, HG_DK, HG_DK), F32)],
        scratch_shapes=[pltpu.VMEM((HG_DK, HG_DK), F32)],
        compiler_params=_cparams(("arbitrary", "arbitrary")),
    )(proj, proj, proj, lb_logits)


def _hg_bwd(proj, lb_logits, states, do_raw):
    s = proj.shape[0]
    nt = s // HG_T
    nc = HG_T // HG_C

    def body(q_ref, f_ref, i_ref, l_ref, st_ref, do_ref, dq_ref, df_ref, di_ref, dl_ref, st_all, adj):
        tb = pl.program_id(1)

        @pl.when(tb == 0)
        def _():
            adj[...] = jnp.zeros_like(adj)
            dl_ref[...] = jnp.zeros_like(dl_ref)

        lb = _lb_of(l_ref[...])
        ones = jnp.ones((HG_DK, HG_DK), BF16)
        t_iota = lax.broadcasted_iota(jnp.int32, (HG_C, HG_DK), 0)
        cc = HG_C * HG_C

        def fwd_group(gi, st):
            terms = []
            for u in range(HG_UNROLL):
                ci = gi * HG_UNROLL + u
                r = pl.ds(pl.multiple_of(ci * HG_C, HG_C), HG_C)
                _, _, _, _, kk, g = _hg_gates(q_ref[r, :], f_ref[r, :], lb)
                b = _chunk_scan(g, t_iota, False)
                b_end = b[HG_C - 1:HG_C]
                terms.append((ci, jnp.exp(b_end), _dot(i_ref[r, :], kk * jnp.exp(b_end - b), TN)))
            for ci, dec, kv in terms:
                st_all[ci] = st
                st = st * dec + kv
            return st

        lax.fori_loop(0, nc // HG_UNROLL, fwd_group, st_ref[0, 0])

        def bwd_group(gj, dlb):
            units = []
            for u in range(HG_UNROLL_BWD):
                ci = nc - 1 - (gj * HG_UNROLL_BWD + u)
                r = pl.ds(pl.multiple_of(ci * HG_C, HG_C), HG_C)
                hq, hf, v, do = q_ref[r, :], f_ref[r, :], i_ref[r, :], do_ref[r, :]
                q, sq, sg, f, kk, g = _hg_gates(hq, hf, lb)
                b = _chunk_scan(g, t_iota, False)
                b_end = b[HG_C - 1:HG_C]
                e_b, e_be, dec = jnp.exp(b), jnp.exp(b_end - b), jnp.exp(b_end)
                w_all, ps = _hg_intra_terms(q, kk, b, t_iota)
                x_all = jnp.concatenate([do * v[si:si + 1] for si in range(HG_C)], axis=0)
                units.append(dict(ci=ci, r=r, hq=hq, v=v, do=do, q=q, sq=sq, sg=sg, f=f, kk=kk, e_b=e_b, e_be=e_be,
                                  dec=dec, kd=kk * e_be, w=w_all, ps=ps, x=x_all))
            both = _dot(jnp.concatenate([un["w"] for un in units] + [un["x"] for un in units], axis=0), ones, NN)
            st0s = [st_all[un["ci"]] for un in units]
            st_ends = [st0 * un["dec"] + _dot(un["v"], un["kd"], TN) for un, st0 in zip(units, st0s)]
            dqks = [_dot(un["do"], un["q"] * un["e_b"], TN) for un in units]
            es = [adj[...]]
            for un, dqk in zip(units, dqks):
                es.append(es[-1] * un["dec"] + dqk)
            adj[...] = es[-1]
            for ui, un in enumerate(units):
                e, q, kk, v, do = es[ui], un["q"], un["kk"], un["v"], un["do"]
                tail = jnp.sum(e * st_ends[ui], axis=0, keepdims=True)
                dq = un["e_b"] * _dot(do, st0s[ui], NN)
                dk = un["e_be"] * _dot(v, e, NN)
                dv = _dot(un["kd"], e, NT)
                a0 = ui * cc
                d0 = (HG_UNROLL_BWD + ui) * cc
                for si in range(HG_C):
                    da = both[d0 + si * HG_C:d0 + (si + 1) * HG_C]
                    aa = both[a0 + si * HG_C:a0 + (si + 1) * HG_C]
                    dap = da * un["ps"][si]
                    dq = dq + dap * kk[si:si + 1]
                    hit = t_iota == si
                    dk = dk + jnp.where(hit, jnp.sum(dap * q, axis=0, keepdims=True), 0.0)
                    dv = dv + jnp.where(hit, jnp.sum(aa * do, axis=0, keepdims=True), 0.0)
                dg = _chunk_scan(q * dq - kk * dk, t_iota, True) + tail
                dfg = dg / un["f"] - dk
                sq, sg, hq, r = un["sq"], un["sg"], un["hq"], un["r"]
                dq_ref[r, :] = (dq * sq * (1.0 + hq * (1.0 - sq))).astype(dq_ref.dtype)
                df_ref[r, :] = (dfg * (1.0 - lb) * sg * (1.0 - sg)).astype(df_ref.dtype)
                di_ref[r, :] = dv.astype(di_ref.dtype)
                dlb = dlb + jnp.sum(dfg * (1.0 - sg), axis=0, keepdims=True)
            return dlb

        dlb = lax.fori_loop(0, nc // HG_UNROLL_BWD, bwd_group, jnp.zeros((1, HG_DK), F32))
        dl0 = dlb * lb * (1.0 - lb)
        dl_ref[...] += jnp.concatenate([dl0, -dl0], axis=0)

    col = lambda off: pl.BlockSpec((HG_T, HG_DK), lambda h, t: (nt - 1 - t, off + h))
    out_col = pl.BlockSpec((HG_T, HG_DK), lambda h, t: (nt - 1 - t, h))
    return pl.pallas_call(
        body, name="hgrn2_bwd", grid=(HG_H, nt),
        in_specs=[col(0), col(8), col(16), pl.BlockSpec((2, HG_DK), lambda h, t: (0, h)),
                  pl.BlockSpec((1, 1, HG_DK, HG_DK), lambda h, t: (h, nt - 1 - t, 0, 0)), col(0)],
        out_specs=[out_col, out_col, out_col, pl.BlockSpec((2, HG_DK), lambda h, t: (0, h))],
        out_shape=[jax.ShapeDtypeStruct((s, D), BF16)] * 3 + [jax.ShapeDtypeStruct((2, D), F32)],
        scratch_shapes=[pltpu.VMEM((nc, HG_DK, HG_DK), F32), pltpu.VMEM((HG_DK, HG_DK), F32)],
        compiler_params=_cparams(("arbitrary", "arbitrary")),
    )(proj, proj, proj, lb_logits, states, do_raw)


def _hg_post_fwd(o_raw, proj, gnorm, tm=256):
    s = o_raw.shape[0]

    def fn(o, hg, gn):
        outs = []
        for h in range(HG_H):
            sl = slice(h * HG_DK, (h + 1) * HG_DK)
            oh, gh = o[:, sl], hg[:, sl]
            r = lax.rsqrt(jnp.mean(oh * oh, axis=-1, keepdims=True) + EPS)
            outs.append(oh * r * gn * (gh * _sigmoid(gh)))
        return (jnp.concatenate(outs, axis=1),), ()

    return _rows("hgrn2_out_fwd", fn, s, tm, [_rb(o_raw, tm, D), _rb(proj, tm, D, 3), _cst(gnorm)], [(D, BF16)])[0]


def _hg_post_bwd(do_a, o_raw, proj, gnorm, tm=256):
    s = o_raw.shape[0]

    def fn(da, o, hg, gn):
        dos, dhgs = [], []
        dgn = jnp.zeros((1, HG_DK), F32)
        for h in range(HG_H):
            sl = slice(h * HG_DK, (h + 1) * HG_DK)
            oh, gh, dh = o[:, sl], hg[:, sl], da[:, sl]
            r = lax.rsqrt(jnp.mean(oh * oh, axis=-1, keepdims=True) + EPS)
            xhat = oh * r
            sg = _sigmoid(gh)
            dy = dh * (gh * sg)
            dhgs.append(dh * xhat * gn * sg * (1.0 + gh * (1.0 - sg)))
            dgn = dgn + jnp.sum(dy * xhat, axis=0, keepdims=True)
            dxh = dy * gn
            dos.append(r * (dxh - xhat * jnp.mean(dxh * xhat, axis=-1, keepdims=True)))
        return (jnp.concatenate(dos, axis=1), jnp.concatenate(dhgs, axis=1)), (dgn,)

    ins = [_rb(do_a, tm, D), _rb(o_raw, tm, D), _rb(proj, tm, D, 3), _cst(gnorm)]
    return _rows("hgrn2_out_bwd", fn, s, tm, ins, [(D, F32), (D, BF16)], [(1, HG_DK)])


def _log_sigmoid(z):
    return jnp.minimum(z, 0.0) - jnp.log(1.0 + jnp.exp(-jnp.abs(z)))


def _fox_gate_bwd(dct, pff, bias, tm=256):
    s = pff.shape[0]
    nb = s // tm

    def body(d_ref, p_ref, b_ref, dff_ref, db_ref, carry):
        @pl.when(pl.program_id(0) == 0)
        def _():
            carry[...] = jnp.zeros_like(carry)
            db_ref[...] = jnp.zeros_like(db_ref)

        dc = d_ref[...].T
        dlf = _split_dot(_tri(tm, False), dc, 3) + carry[0:1]
        carry[...] = jnp.broadcast_to(dlf[0:1], carry.shape)
        dff = dlf * _sigmoid(-(p_ref[...] + b_ref[...]))
        dff_ref[...] = dff
        db_ref[...] += jnp.sum(dff, axis=0, keepdims=True)

    return pl.pallas_call(
        body, name="fox_gate_bwd", grid=(nb,),
        in_specs=[pl.BlockSpec((LANES, tm), lambda i: (0, nb - 1 - i)),
                  pl.BlockSpec((tm, LANES), lambda i: (nb - 1 - i, 0)), pl.BlockSpec((1, LANES), lambda i: (0, 0))],
        out_specs=[pl.BlockSpec((tm, LANES), lambda i: (nb - 1 - i, 0)), pl.BlockSpec((1, LANES), lambda i: (0, 0))],
        out_shape=[jax.ShapeDtypeStruct((s, LANES), F32), jax.ShapeDtypeStruct((1, LANES), F32)],
        scratch_shapes=[pltpu.VMEM((8, LANES), F32)],
        compiler_params=_cparams(("arbitrary",)),
    )(dct, pff, bias)


def _diag_mask(t):
    r = lax.broadcasted_iota(jnp.int32, (t, t), 0)
    c = lax.broadcasted_iota(jnp.int32, (t, t), 1)
    return r >= c


AUX_ONES = 6


def _pieces(x):
    h = x.astype(BF16)
    r = x - h.astype(F32)
    m = r.astype(BF16)
    return h, m, (r - m.astype(F32)).astype(BF16)


def _lane_put(lane, cols, base):
    out = None
    for i, col in enumerate(cols):
        term = jnp.where(lane == base + i, col.astype(F32), 0.0)
        out = term if out is None else out + term
    return out


def _fox_prep2(proj, pff, bias, tm=256):
    s = pff.shape[0]

    def body(q_ref, k_ref, v_ref, p_ref, b_ref, qb_ref, kb_ref, vb_ref, ka_ref, carry):
        @pl.when(pl.program_id(0) == 0)
        def _():
            carry[...] = jnp.zeros_like(carry)

        qb_ref[...] = (q_ref[...] * 0.125).astype(BF16)
        kb_ref[...] = k_ref[...].astype(BF16)
        vb_ref[...] = v_ref[...].astype(BF16)
        lf = _log_sigmoid(p_ref[...] + b_ref[...])
        c = _split_dot(_tri(tm, True), lf, 3) + carry[0:1]
        carry[...] = jnp.broadcast_to(c[tm - 1:tm], carry.shape)
        lane = lax.broadcasted_iota(jnp.int32, (tm, LANES), 1)
        ones = jnp.where((lane >= AUX_ONES) & (lane < AUX_ONES + 6), 1.0, 0.0)
        for p in range(FOX_H // 2):
            aux = ones
            for z in range(2):
                col = jnp.sum(jnp.where(lane == 2 * p + z, c, 0.0), axis=1, keepdims=True)
                aux = aux + _lane_put(lane, _pieces(-col), 3 * z)
            ka_ref[:, p * LANES:(p + 1) * LANES] = aux.astype(BF16)

    row = lambda cb: pl.BlockSpec((tm, D), lambda i: (i, cb))
    return pl.pallas_call(
        body, name="fox_prep", grid=(s // tm,),
        in_specs=[row(4), row(5), row(6), pl.BlockSpec((tm, LANES), lambda i: (i, 0)),
                  pl.BlockSpec((1, LANES), lambda i: (0, 0))],
        out_specs=[row(0)] * 4, out_shape=[jax.ShapeDtypeStruct((s, D), BF16)] * 4,
        scratch_shapes=[pltpu.VMEM((8, LANES), F32)],
        compiler_params=_cparams(("arbitrary",)),
    )(proj, proj, proj, pff, bias)


def _fox_fwd2(qb, kb, vb, ka):
    s = qb.shape[0]
    t = min(FOX_T, s)
    nq = s // t

    def body(q_ref, k_ref, v_ref, ka_ref, o_ref, la_ref):
        i = pl.program_id(1)
        lane = lax.broadcasted_iota(jnp.int32, (t, LANES), 1)
        in_a = lane < FOX_D
        q = q_ref[...]
        zero = jnp.zeros_like(q)
        qh = [jnp.where(in_a, q, zero), jnp.where(in_a, zero, q)]
        c_ones = [jnp.where((lane >= 3 * z) & (lane < 3 * z + 3), 1.0, 0.0) for z in range(2)]
        dmask = _diag_mask(t)

        def keys(j):
            rows = pl.ds(pl.multiple_of(j * t, t), t)
            return jnp.concatenate([k_ref[rows, :], ka_ref[rows, :]], axis=1), rows

        def logits(qx, kk, masked):
            e = lax.dot_general(qx, kk, (NT, ((), ())), preferred_element_type=F32)
            return jnp.where(dmask, e, -1e30) if masked else e

        qc = [jnp.concatenate([qh[z], c_ones[z].astype(BF16)], axis=1) for z in range(2)]

        def step(j, carry, masked):
            kk, rows = keys(j)
            vj = v_ref[rows, :]
            scores = [logits(qc[z], kk, masked) for z in range(2)]
            one = jnp.ones_like(vj)
            vh = [jnp.where(in_a, vj, one), jnp.where(in_a, one, vj)]
            out = []
            for z in range(2):
                m, acc = carry[z]
                m_new = jnp.maximum(m, jnp.max(scores[z], axis=1, keepdims=True))
                p = jnp.exp(scores[z] - m_new)
                out.append((m_new, jnp.exp(m - m_new) * acc + _dot(p, vh[z], NN)))
            return tuple(out)

        init = tuple((jnp.full((t, 1), -1e30, F32), jnp.zeros((t, LANES), F32)) for _ in range(2))
        (ma, acc_a), (mb, acc_b) = step(i, lax.fori_loop(0, i, lambda j, c: step(j, c, False), init), True)
        la = jnp.sum(jnp.where(lane == FOX_D, acc_a, 0.0), axis=1, keepdims=True)
        lb = jnp.sum(jnp.where(lane == 0, acc_b, 0.0), axis=1, keepdims=True)
        o_ref[...] = jnp.where(in_a, acc_a / la, acc_b / lb).astype(o_ref.dtype)
        la_ref[...] = (_lane_put(lane, _pieces(-(ma + jnp.log(la))), AUX_ONES)
                       + _lane_put(lane, _pieces(-(mb + jnp.log(lb))), AUX_ONES + 3)).astype(la_ref.dtype)

    blk = pl.BlockSpec((t, LANES), lambda p, i: (i, p))
    whole = pl.BlockSpec((s, LANES), lambda p, i: (0, p))
    return pl.pallas_call(
        body, name="fox_attn_fwd", grid=(FOX_H // 2, nq), in_specs=[blk, whole, whole, whole],
        out_specs=[blk, blk], out_shape=[jax.ShapeDtypeStruct((s, D), BF16)] * 2,
        compiler_params=_cparams(("arbitrary", "arbitrary")),
    )(qb, kb, vb, ka)


def _fox_bwd2(qb, kb, vb, ka, ob, laux, dob):
    s = qb.shape[0]
    t = min(FOX_T, s)
    nq = s // t

    def body(q_ref, k_ref, v_ref, ka_ref, o_ref, la_ref, do_ref, dq_ref, dk_ref, dv_ref, dc_ref, dkt, dvt):
        i = pl.program_id(1)

        @pl.when(i == 0)
        def _():
            dkt[...] = jnp.zeros_like(dkt)
            dvt[...] = jnp.zeros_like(dvt)
            dc_ref[...] = jnp.zeros_like(dc_ref)

        lane = lax.broadcasted_iota(jnp.int32, (t, LANES), 1)
        in_a = lane < FOX_D
        q, do, la = q_ref[...], do_ref[...], la_ref[...].astype(F32)
        zero = jnp.zeros_like(q)
        qh = [jnp.where(in_a, q, zero), jnp.where(in_a, zero, q)]
        doh = [jnp.where(in_a, do, zero), jnp.where(in_a, zero, do)]
        qt = [h.astype(F32).T.astype(BF16) for h in qh]
        dot_ = [h.astype(F32).T.astype(BF16) for h in doh]
        prod = do.astype(F32) * o_ref[...].astype(F32)
        qx, dox = [], []
        for z in range(2):
            delta = jnp.sum(jnp.where(in_a if z == 0 else ~in_a, prod, 0.0), axis=1, keepdims=True)
            c_ones = jnp.where((lane >= 3 * z) & (lane < 3 * z + 3), 1.0, 0.0)
            lse_lanes = (lane >= AUX_ONES + 3 * z) & (lane < AUX_ONES + 3 * z + 3)
            qx.append(jnp.concatenate([qh[z], (c_ones + jnp.where(lse_lanes, la, 0.0)).astype(BF16)], axis=1))
            dox.append(jnp.concatenate([doh[z], _lane_put(lane, _pieces(-delta), 3 * z).astype(BF16)], axis=1))
        v_ones = jnp.where(lane < 6, 1.0, 0.0).astype(BF16)
        dmask = _diag_mask(t)

        def step(j, carry, masked):
            rows = pl.ds(pl.multiple_of(j * t, t), t)
            kj, vj = k_ref[rows, :], v_ref[rows, :]
            kk = jnp.concatenate([kj, ka_ref[rows, :]], axis=1)
            vv = jnp.concatenate([vj, v_ones], axis=1)
            out = []
            dk_add, dv_add = None, None
            for z in range(2):
                dq, rsum = carry[z]
                e = lax.dot_general(qx[z], kk, (NT, ((), ())), preferred_element_type=F32)
                if masked:
                    e = jnp.where(dmask, e, -1e30)
                p = jnp.exp(e)
                ds = p * lax.dot_general(dox[z], vv, (NT, ((), ())), preferred_element_type=F32)
                dkz, dvz = _dot(qt[z], ds, NN), _dot(dot_[z], p, NN)
                dk_add = dkz if dk_add is None else dk_add + dkz
                dv_add = dvz if dv_add is None else dv_add + dvz
                dc_ref[0, z, j] += -jnp.sum(ds, axis=0, keepdims=True)
                out.append((dq + _dot(ds, kj, NN), rsum + jnp.sum(ds, axis=1, keepdims=True)))
            dkt[j] += dk_add
            dvt[j] += dv_add
            return tuple(out)

        init = tuple((jnp.zeros((t, LANES), F32), jnp.zeros((t, 1), F32)) for _ in range(2))
        (dq_a, rs_a), (dq_b, rs_b) = step(i, lax.fori_loop(0, i, lambda j, c: step(j, c, False), init), True)
        for z, rs in enumerate((rs_a, rs_b)):
            dc_ref[0, z, i] += jnp.transpose(jnp.broadcast_to(rs, (t, LANES)))[0:1]
        dq_ref[...] = (jnp.where(in_a, dq_a, dq_b) * 0.125).astype(dq_ref.dtype)

        @pl.when(i == nq - 1)
        def _():
            for jb in range(nq):
                dk_ref[jb * t:(jb + 1) * t, :] = dkt[jb].T
                dv_ref[jb * t:(jb + 1) * t, :] = dvt[jb].T

    blk = pl.BlockSpec((t, LANES), lambda p, i: (i, p))
    whole = pl.BlockSpec((s, LANES), lambda p, i: (0, p))
    return pl.pallas_call(
        body, name="fox_attn_bwd", grid=(FOX_H // 2, nq),
        in_specs=[blk, whole, whole, whole, blk, blk, blk],
        out_specs=[blk, whole, whole, pl.BlockSpec((1, 2, nq, 1, t), lambda p, i: (p, 0, 0, 0, 0))],
        out_shape=[jax.ShapeDtypeStruct((s, D), BF16), jax.ShapeDtypeStruct((s, D), F32),
                   jax.ShapeDtypeStruct((s, D), F32), jax.ShapeDtypeStruct((FOX_H // 2, 2, nq, 1, t), F32)],
        scratch_shapes=[pltpu.VMEM((nq, LANES, t), F32), pltpu.VMEM((nq, LANES, t), F32)],
        compiler_params=_cparams(("arbitrary", "arbitrary")),
    )(qb, kb, vb, ka, ob, laux, dob)


def _adamw(name, w, g, m, v, tm=None):
    rows, width = w.shape
    tm = rows if tm is None else tm
    c1 = 1.0 - ADAM_B1 ** ADAM_STEP
    c2 = 1.0 - ADAM_B2 ** ADAM_STEP

    def fn(wb, gb, mb, vb):
        m_new = ADAM_B1 * mb + (1.0 - ADAM_B1) * gb
        v_new = ADAM_B2 * vb + (1.0 - ADAM_B2) * (gb * gb)
        delta = -ADAM_LR * ((m_new / c1) / (jnp.sqrt(v_new / c2) + ADAM_EPS) + ADAM_WD * wb)
        return (delta, m_new, v_new), ()

    ins = [_rb(a, tm, width) for a in (w, g, m, v)]
    return _rows(name, fn, rows, tm, ins, [(width, F32)] * 3)


def _me():
    return lax.axis_index("x"), lax.axis_index("y"), lax.axis_index("c")


def _all_gather8(name, block):
    m, n = block.shape

    def body(x_ref, out_ref, send_sems, recv_sems):
        x, y, c = _me()
        me, sibling = (x, y, c), (x, y, 1 - c)
        chips = [(1 - x, y), (x, 1 - y), (1 - x, 1 - y)]

        def slot(px, py, pc):
            return out_ref.at[4 * px + 2 * py + pc]

        def copy(k, blk, to, src=None):
            return pltpu.make_async_remote_copy(
                src_ref=slot(*blk) if src is None else src, dst_ref=slot(*blk),
                send_sem=send_sems.at[k], recv_sem=recv_sems.at[k], device_id=to, device_id_type=MESH)

        first = [copy(0, me, sibling, src=x_ref)]
        first += [copy(1 + j, me, (*chip, c), src=x_ref) for j, chip in enumerate(chips)]
        for cp in first:
            cp.start()
        passed = [copy(4 + j, (*chip, c), sibling) for j, chip in enumerate(chips)]
        for j, chip in enumerate(chips):
            copy(1 + j, (*chip, c), me).wait_recv()
            passed[j].start()
        copy(0, sibling, me).wait_recv()
        for j, chip in enumerate(chips):
            copy(4 + j, (*chip, 1 - c), me).wait_recv()
        for cp in first + passed:
            cp.wait_send()

    gathered = pl.pallas_call(
        body, name=name, in_specs=[ANY], out_specs=ANY,
        out_shape=jax.ShapeDtypeStruct((8, m, n), block.dtype),
        scratch_shapes=[pltpu.SemaphoreType.DMA((7,)), pltpu.SemaphoreType.DMA((7,))],
    )(block)
    x, y, c = _me()
    return lax.dynamic_update_slice(gathered, block[None], (4 * x + 2 * y + c, 0, 0))


def _swap_halves(name, g):
    n, _, m, lanes = g.shape

    def body(g_ref, got_ref, send_sems, recv_sems):
        x, y, c = _me()
        copies = [pltpu.make_async_remote_copy(
            src_ref=g_ref.at[j, 1 - c], dst_ref=got_ref.at[j], send_sem=send_sems.at[j], recv_sem=recv_sems.at[j],
            device_id=(x, y, 1 - c), device_id_type=MESH) for j in range(n)]
        for cp in copies:
            cp.start()
        for cp in copies:
            cp.wait()

    return pl.pallas_call(
        body, name=name, in_specs=[ANY], out_specs=ANY, out_shape=jax.ShapeDtypeStruct((n, m, lanes), g.dtype),
        scratch_shapes=[pltpu.SemaphoreType.DMA((n,)), pltpu.SemaphoreType.DMA((n,))],
    )(g)


def _swap_sibling(name, mine):
    def body(m_ref, out_ref, send_sem, recv_sem):
        x, y, c = _me()
        cp = pltpu.make_async_remote_copy(src_ref=m_ref, dst_ref=out_ref, send_sem=send_sem, recv_sem=recv_sem,
                                          device_id=(x, y, 1 - c), device_id_type=MESH)
        cp.start()
        cp.wait()

    return pl.pallas_call(
        body, name=name, in_specs=[ANY], out_specs=ANY, out_shape=jax.ShapeDtypeStruct(mine.shape, mine.dtype),
        scratch_shapes=[pltpu.SemaphoreType.DMA, pltpu.SemaphoreType.DMA],
    )(mine)


def _chip_exchange(name, p):
    def body(p_ref, out_ref, send_sems, recv_sems):
        x, y, c = _me()
        my_chip = 2 * x + y
        chips = [(1 - x, y), (x, 1 - y), (1 - x, 1 - y)]
        sends = []
        for k, (px, py) in enumerate(chips):
            sends.append(pltpu.make_async_remote_copy(
                src_ref=p_ref.at[2 * px + py], dst_ref=out_ref.at[my_chip], send_sem=send_sems.at[k],
                recv_sem=recv_sems.at[k], device_id=(px, py, c), device_id_type=MESH))
        for cp in sends:
            cp.start()
        for k, (px, py) in enumerate(chips):
            pltpu.make_async_remote_copy(
                src_ref=p_ref.at[my_chip], dst_ref=out_ref.at[2 * px + py], send_sem=send_sems.at[k],
                recv_sem=recv_sems.at[k], device_id=(px, py, c), device_id_type=MESH).wait_recv()
        for cp in sends:
            cp.wait_send()

    got = pl.pallas_call(
        body, name=name, in_specs=[ANY], out_specs=ANY, out_shape=jax.ShapeDtypeStruct(p.shape, p.dtype),
        scratch_shapes=[pltpu.SemaphoreType.DMA((3,)), pltpu.SemaphoreType.DMA((3,))],
    )(p)
    x, y, _ = _me()
    my_chip = 2 * x + y
    return lax.dynamic_update_slice(got, lax.dynamic_index_in_dim(p, my_chip, axis=0, keepdims=True), (my_chip, 0, 0))


def _all_reduce_small(name, block):
    r, n = block.shape

    def body(x_ref, sum_ref, gath, send_sems, recv_sems):
        x, y, c = _me()
        me = 4 * x + 2 * y + c
        gath[me] = x_ref[...]
        sends = []
        for k in range(1, 8):
            px = x ^ ((k >> 2) & 1)
            py = y ^ ((k >> 1) & 1)
            pc = c ^ (k & 1)
            sends.append(pltpu.make_async_remote_copy(
                src_ref=x_ref, dst_ref=gath.at[me], send_sem=send_sems.at[k - 1], recv_sem=recv_sems.at[k - 1],
                device_id=(px, py, pc), device_id_type=MESH))
        for cp in sends:
            cp.start()
        for k in range(1, 8):
            peer = me ^ k
            pltpu.make_async_remote_copy(
                src_ref=x_ref, dst_ref=gath.at[peer], send_sem=send_sems.at[k - 1], recv_sem=recv_sems.at[k - 1],
                device_id=(x, y, c), device_id_type=MESH).wait_recv()
        for cp in sends:
            cp.wait_send()
        acc = gath[0]
        for d in range(1, 8):
            acc = acc + gath[d]
        sum_ref[...] = acc

    vm = pl.BlockSpec(memory_space=pltpu.VMEM)
    return pl.pallas_call(
        body, name=name, in_specs=[vm], out_specs=vm, out_shape=jax.ShapeDtypeStruct((r, n), F32),
        scratch_shapes=[pltpu.VMEM((8, r, n), F32), pltpu.SemaphoreType.DMA((7,)), pltpu.SemaphoreType.DMA((7,))],
    )(block)


def _add2(name, a, b, tm):
    rows = a.shape[0]
    return _rows(name, lambda p, q: ((p + q,), ()), rows, tm, [_rb(a, tm, LANES), _rb(b, tm, LANES)], [(LANES, BF16)])[0]


def _add4(name, p, tm):
    m = p.shape[1]
    flat = p.reshape(4 * m, LANES)
    nb = m // tm
    ins = [(flat, (tm, LANES), (lambda i, j=j: (j * nb + i, 0))) for j in range(4)]
    f32 = lambda v: v.astype(F32)
    return _rows(name, lambda a, b, c, d: ((((f32(a) + f32(b)) + f32(c)) + f32(d),), ()), m, tm, ins, [(LANES, F32)])[0]


SEG_ROWS = (D * W_IN_SHARD // LANES, 256 * D // LANES, 256 * D // LANES, 256 * D // LANES,
            D * W_UP_SHARD // LANES, W_DOWN_SHARD * D // LANES)
GRAD_ROWS = sum(SEG_ROWS)
CONVW_ROWS = 3 * W_UP_SHARD * 2 // LANES
GATHER_ROWS = 41600


def _flat(a):
    return a.reshape(-1, LANES)


def _gather_weights(w_in, w_a, w_b, w_out, w_up, w_down, conv_w):
    c = lax.axis_index("c")
    bits = lax.bitcast_convert_type(conv_w, BF16)
    pieces = [_flat(t.astype(BF16)) for t in (w_in, w_a, w_b, w_out, w_up, w_down)] + [_flat(bits)]
    pad = GATHER_ROWS - GRAD_ROWS - CONVW_ROWS
    shard = jnp.concatenate(pieces + [jnp.zeros((pad, LANES), BF16)], axis=0)
    half = GATHER_ROWS // 2
    mine = lax.dynamic_slice_in_dim(shard, c * half, half, axis=0)
    full = _all_gather8("all_gather_weights", mine).reshape(N_CHIP, GATHER_ROWS, LANES)
    offs = [0]
    for r in SEG_ROWS:
        offs.append(offs[-1] + r)
    seg = lambda i: full[:, offs[i]:offs[i + 1]]
    wi = seg(0).reshape(N_CHIP, D, W_IN_SHARD).transpose(1, 0, 2).reshape(D, N_CHIP * W_IN_SHARD)
    w_main = jnp.concatenate([wi[:, :FF_COL], wi[:, FF_COL + FOX_H:]], axis=1)
    w_ff = jnp.pad(wi[:, FF_COL:FF_COL + FOX_H], ((0, 0), (0, LANES - FOX_H)))
    wa, wb, wo = (seg(i).reshape(D, D) for i in (1, 2, 3))
    wu = seg(4).reshape(N_CHIP, D, W_UP_SHARD).transpose(1, 0, 2).reshape(D, 2 * D_FF)
    wd = seg(5).reshape(D_FF, D)
    cw_bits = full[:, GRAD_ROWS:GRAD_ROWS + CONVW_ROWS].reshape(N_CHIP, 3, W_UP_SHARD, 2)
    cw = lax.bitcast_convert_type(cw_bits, F32).transpose(1, 0, 2).reshape(3, 2 * D_FF)
    return w_main, w_ff, wa, wb, wo, wu, wd, cw


def _reduce_scatter_grads(d_main, d_ff, d_a, d_b, d_o, d_u, d_d):
    c = lax.axis_index("c")
    d_in = jnp.concatenate(d_main[:7] + [d_ff[:, :FOX_H]] + d_main[7:], axis=1)
    per_chip = [
        d_in.reshape(D, N_CHIP, W_IN_SHARD).transpose(1, 0, 2).reshape(N_CHIP, -1, LANES),
        d_a.reshape(N_CHIP, -1, LANES), d_b.reshape(N_CHIP, -1, LANES), d_o.reshape(N_CHIP, -1, LANES),
        d_u.reshape(D, N_CHIP, W_UP_SHARD).transpose(1, 0, 2).reshape(N_CHIP, -1, LANES),
        d_d.reshape(N_CHIP, -1, LANES),
        jnp.zeros((N_CHIP, GATHER_ROWS - GRAD_ROWS, LANES), F32),
    ]
    half = GATHER_ROWS // 2
    g = jnp.concatenate(per_chip, axis=1).reshape(N_CHIP, 2, half, LANES)
    from_sibling = _swap_halves("grad_swap_halves", g)
    mine = lax.dynamic_index_in_dim(g, c, axis=1, keepdims=False)
    tm = half // 5
    chip_sum = _add2("grad_chip_sum", mine.reshape(-1, LANES), from_sibling.reshape(-1, LANES), tm)
    pieces = _chip_exchange("grad_chip_exchange", chip_sum.reshape(N_CHIP, half, LANES))
    mine_half = _add4("grad_sum_chips", pieces, tm)
    other_half = _swap_sibling("grad_share_half", mine_half)
    lo = jnp.where(c == 0, mine_half, other_half)
    hi = jnp.where(c == 0, other_half, mine_half)
    return jnp.concatenate([lo, hi], axis=0)


def _local_step(x, target, norm_mix, fox_f_bias, hg_lb_logits, hg_norm, norm_ffn, conv_b, norm_final,
                w_main, w_ff, wa, wb, wo, wu, wd, conv_w):
    s = x.shape[0]
    bias = jnp.pad(fox_f_bias, ((0, 0), (0, LANES - FOX_H)))
    conv_w8 = jnp.pad(conv_w, ((0, 5), (0, 0)))
    t = min(FOX_T, s)

    n1 = _rms_fwd("norm_mix_fwd", x, norm_mix)
    proj = _mm("in_proj", n1, w_main, "nn", F32, 1024, 1024, D)
    pff = _mm("in_proj_forget", n1, w_ff, "nn", F32, 1024, LANES, D)
    qb, kb, vb, ka = _fox_prep2(proj, pff, bias)
    o_b, laux = _fox_fwd2(qb, kb, vb, ka)
    o_raw, states = _hg_fwd(proj, hg_lb_logits)
    o_a = _hg_post_fwd(o_raw, proj, hg_norm)
    pa = _mm("branch_a", o_a, wa, "nn", F32, 1024, 1024, D)
    pb = _mm("branch_b", o_b, wb, "nn", F32, 1024, 1024, D)
    merged = _merge_fwd(pa, pb, proj)
    h1 = _mm("out_proj", merged, wo, "nn", F32, 1024, 1024, D, res=x)
    n2 = _rms_fwd("norm_ffn_fwd", h1, norm_ffn)
    u = _mm("ffn_up", n2, wu, "nn", F32, 1024, W_UP_SHARD, D)
    act = _convglu_fwd(u, conv_w8, conv_b)
    h2 = _mm("ffn_down", act, wd, "nn", F32, 512, 1024, D_FF, res=h1)
    (dh2,), (d_norm_final, loss_row) = _final(h2, target, norm_final)

    dact = _mm("ffn_down_dx", dh2, wd, "nt", BF16, 1024, D_FF, D)
    d_wd = _mm("ffn_down_dw", act, dh2, "tn", F32, D_FF // 2, 1024, 512)
    (du,), (d_conv_w8, d_conv_b) = _convglu_bwd(u, dact, conv_w8, conv_b)
    dn2 = _mm("ffn_up_dx", du, wu, "nt", F32, 1024, 1024, W_UP_SHARD)
    d_wu = _mm("ffn_up_dw", n2, du, "tn", F32, 1024, W_UP_SHARD, 512)
    (dh1,), (d_norm_ffn,) = _rms_bwd("norm_ffn_bwd", h1, norm_ffn, [dn2], dh2)

    dmerged = _mm("out_proj_dx", dh1, wo, "nt", F32, 1024, 1024, D)
    d_wo = _mm("out_proj_dw", merged, dh1, "tn", F32, 1024, 1024, 512)
    dpa, dpb, dga, dgb = _merge_bwd(dmerged, pa, pb, proj)
    do_a = _mm("branch_a_dx", dpa, wa, "nt", F32, 1024, 1024, D)
    do_b = _mm("branch_b_dx", dpb, wb, "nt", BF16, 1024, 1024, D)
    d_wa = _mm("branch_a_dw", o_a, dpa, "tn", F32, 1024, 1024, 512)
    d_wb = _mm("branch_b_dw", o_b, dpb, "tn", F32, 1024, 1024, 512)

    (do_raw, dhg), (d_hg_norm,) = _hg_post_bwd(do_a, o_raw, proj, hg_norm)
    dhq, dhf, dhi, d_lb_logits = _hg_bwd(proj, hg_lb_logits, states, do_raw)

    dfq, dfk, dfv, dcrow = _fox_bwd2(qb, kb, vb, ka, o_b, laux, do_b)
    dct = jnp.pad(dcrow.reshape(FOX_H, s), ((0, LANES - FOX_H), (0, 0)))
    dff, d_bias = _fox_gate_bwd(dct, pff, bias)

    pieces = [dhq, dhf, dhi, dhg, dfq, dfk, dfv, dga, dgb]
    dn1 = _mm_sum_nt("in_proj_dx", pieces, w_main, (dff, w_ff), 512, 1024)
    d_w_main = [_mm("in_proj_dw_%d" % i, n1, p, "tn", F32, 1024, 1024, 512) for i, p in enumerate(pieces)]
    d_w_ff = _mm("in_proj_forget_dw", n1, dff, "tn", F32, 1024, LANES, 512)
    (dx,), (d_norm_mix,) = _rms_bwd("norm_mix_bwd", x, norm_mix, [dn1], dh1)

    small = dict(norm_mix=d_norm_mix, fox_f_bias=d_bias[:, :FOX_H], hg_lb_logits=d_lb_logits, hg_norm=d_hg_norm,
                 norm_ffn=d_norm_ffn, conv_b=d_conv_b, norm_final=d_norm_final, conv_w=d_conv_w8[:3], loss=loss_row)
    big = (d_w_main, d_w_ff, d_wa, d_wb, d_wo, d_wu, d_wd)
    return dx, small, big


SMALL_KEYS = ("norm_mix", "fox_f_bias", "hg_lb_logits", "hg_norm", "norm_ffn", "conv_b", "norm_final")


def _pack_small(parts):
    rows, layout = [], []
    for key, arr in parts:
        flat = arr.reshape(-1)
        n = flat.shape[0]
        nr = -(-n // LANES)
        rows.append(jnp.pad(flat, (0, nr * LANES - n)).reshape(nr, LANES))
        layout.append((key, arr.shape, n, nr))
    packed = jnp.concatenate(rows, axis=0)
    pad = -packed.shape[0] % 8
    return jnp.pad(packed, ((0, pad), (0, 0))), layout


def _unpack_small(packed, layout):
    out, r0 = {}, 0
    for key, shape, n, nr in layout:
        out[key] = packed[r0:r0 + nr].reshape(-1)[:n].reshape(shape)
        r0 += nr
    return out


def kernel(x, norm_mix, w_in, fox_f_bias, hg_lb_logits, hg_norm, w_branch_a, w_branch_b, w_out, norm_ffn, w_up, conv_w, conv_b, w_down, norm_final, loss_target, m_norm_mix, m_w_in, m_fox_f_bias, m_hg_lb_logits, m_hg_norm, m_w_branch_a, m_w_branch_b, m_w_out, m_norm_ffn, m_w_up, m_conv_w, m_conv_b, m_w_down, m_norm_final, v_norm_mix, v_w_in, v_fox_f_bias, v_hg_lb_logits, v_hg_norm, v_w_branch_a, v_w_branch_b, v_w_out, v_norm_ffn, v_w_up, v_conv_w, v_conv_b, v_w_down, v_norm_final):
    chip = 2 * lax.axis_index("x") + lax.axis_index("y")
    w_main, w_ff, wa, wb, wo, wu, wd, cw = _gather_weights(
        w_in[0], w_branch_a[0], w_branch_b[0], w_out[0], w_up[0], w_down[0], conv_w[0])
    dx, small, big = _local_step(
        x[0], loss_target[0], norm_mix, fox_f_bias, hg_lb_logits, hg_norm, norm_ffn, conv_b,
        norm_final.reshape(1, D), w_main, w_ff, wa, wb, wo, wu, wd, cw)

    packed, layout = _pack_small([(k, small[k]) for k in SMALL_KEYS + ("conv_w", "loss")])
    red = _unpack_small(_all_reduce_small("all_reduce_small", packed), layout)
    loss = red["loss"][0, 0]
    g_conv_w = lax.dynamic_slice_in_dim(red["conv_w"], chip * W_UP_SHARD, W_UP_SHARD, axis=1)

    gflat = _reduce_scatter_grads(*big)
    offs = [0]
    for r in SEG_ROWS:
        offs.append(offs[-1] + r)
    shapes = [(D, W_IN_SHARD), (256, D), (256, D), (256, D), (D, W_UP_SHARD), (W_DOWN_SHARD, D)]
    g_big = [gflat[offs[i]:offs[i + 1]].reshape(shapes[i]) for i in range(6)]

    names = ["norm_mix", "w_in", "fox_f_bias", "hg_lb_logits", "hg_norm", "w_branch_a", "w_branch_b", "w_out",
             "norm_ffn", "w_up", "conv_w", "conv_b", "w_down", "norm_final"]
    weights = dict(norm_mix=norm_mix, w_in=w_in, fox_f_bias=fox_f_bias, hg_lb_logits=hg_lb_logits, hg_norm=hg_norm,
                   w_branch_a=w_branch_a, w_branch_b=w_branch_b, w_out=w_out, norm_ffn=norm_ffn, w_up=w_up,
                   conv_w=conv_w, conv_b=conv_b, w_down=w_down, norm_final=norm_final)
    ms = dict(norm_mix=m_norm_mix, w_in=m_w_in, fox_f_bias=m_fox_f_bias, hg_lb_logits=m_hg_lb_logits,
              hg_norm=m_hg_norm, w_branch_a=m_w_branch_a, w_branch_b=m_w_branch_b, w_out=m_w_out,
              norm_ffn=m_norm_ffn, w_up=m_w_up, conv_w=m_conv_w, conv_b=m_conv_b, w_down=m_w_down,
              norm_final=m_norm_final)
    vs = dict(norm_mix=v_norm_mix, w_in=v_w_in, fox_f_bias=v_fox_f_bias, hg_lb_logits=v_hg_lb_logits,
              hg_norm=v_hg_norm, w_branch_a=v_w_branch_a, w_branch_b=v_w_branch_b, w_out=v_w_out,
              norm_ffn=v_norm_ffn, w_up=v_w_up, conv_w=v_conv_w, conv_b=v_conv_b, w_down=v_w_down,
              norm_final=v_norm_final)

    grads, deltas, new_m, new_v = {}, {}, {}, {}
    big_names = ["w_in", "w_branch_a", "w_branch_b", "w_out", "w_up", "w_down"]
    for name, g2 in zip(big_names, g_big):
        shape = weights[name].shape
        rows = g2.shape[0]
        d_, m_, v_ = _adamw("adamw_" + name, weights[name][0], g2, ms[name][0], vs[name][0], tm=rows // 8)
        grads[name], deltas[name], new_m[name], new_v[name] = (a.reshape(shape) for a in (g2, d_, m_, v_))
    shape = conv_w.shape
    d_, m_, v_ = _adamw("adamw_conv_w", conv_w[0], g_conv_w, m_conv_w[0], v_conv_w[0])
    grads["conv_w"], deltas["conv_w"], new_m["conv_w"], new_v["conv_w"] = (
        a.reshape(shape) for a in (g_conv_w, d_, m_, v_))
    gs = {k: red[k].reshape(weights[k].shape) for k in SMALL_KEYS}
    pw, lay = _pack_small([(k, weights[k]) for k in SMALL_KEYS])
    pg, _ = _pack_small([(k, gs[k]) for k in SMALL_KEYS])
    pm, _ = _pack_small([(k, ms[k]) for k in SMALL_KEYS])
    pv, _ = _pack_small([(k, vs[k]) for k in SMALL_KEYS])
    d_, m_, v_ = (_unpack_small(a, lay) for a in _adamw("adamw_small", pw, pg, pm, pv))
    for k in SMALL_KEYS:
        grads[k], deltas[k], new_m[k], new_v[k] = gs[k], d_[k], m_[k], v_[k]

    return (loss, dx[None], *[grads[n] for n in names], *[deltas[n] for n in names],
            *[new_m[n] for n in names], *[new_v[n] for n in names])
```

```python
import functools

import jax
import jax.numpy as jnp
from jax import lax
from jax.experimental import pallas as pl
from jax.experimental.pallas import tpu as pltpu

F32 = jnp.float32
BF16 = jnp.bfloat16

D = 1024
HG_H, HG_DK = 8, 128
FOX_H, FOX_D = 16, 64
D_FF = 2816
EPS = 1e-6
N_CHIP = 4
LANES = 128
W_IN_SHARD = 2308
W_UP_SHARD = 1408
W_DOWN_SHARD = 704
FF_COL = 7168
ADAM_LR, ADAM_B1, ADAM_B2, ADAM_EPS, ADAM_WD, ADAM_STEP = 0.001, 0.9, 0.999, 1e-08, 0.01, 10

HG_C = 16
HG_T = 256
HG_UNROLL = 8
HG_UNROLL_BWD = 4
FOX_T = 512
VMEM_LIMIT = 56 * 1024 * 1024
MESH = pl.DeviceIdType.MESH
ANY = pl.BlockSpec(memory_space=pl.ANY)


def _cparams(sem):
    return pltpu.CompilerParams(dimension_semantics=sem, vmem_limit_bytes=VMEM_LIMIT)


def _sigmoid(x):
    return 1.0 / (1.0 + jnp.exp(-x))


def _dot(a, b, dims):
    return lax.dot_general(a.astype(BF16), b.astype(BF16), (dims, ((), ())), preferred_element_type=F32)


NN = ((1,), (0,))
NT = ((1,), (1,))
TN = ((0,), (0,))


def _split_dot(tri, x, parts, dims=NN):
    acc = None
    r = x
    for _ in range(parts):
        p = r.astype(BF16)
        t = lax.dot_general(tri, p, (dims, ((), ())), preferred_element_type=F32)
        acc = t if acc is None else acc + t
        r = r - p.astype(F32)
    return acc


def _rb(arr, tm, width, cb=0):
    return (arr, (tm, width), lambda i: (i, cb))


def _cst(arr):
    return (arr, arr.shape, lambda i: (0,) * arr.ndim)


def _rows(name, fn, n_rows, tm, ins, outs, accs=(), reverse=False):
    n_in, n_out, n_acc = len(ins), len(outs), len(accs)
    nb = n_rows // tm

    def body(*refs):
        vals = [r[...] for r in refs[:n_in]]
        o, a = fn(*vals)
        for r, v in zip(refs[n_in:n_in + n_out], o):
            r[...] = v.astype(r.dtype)
        if n_acc:
            acc_refs = refs[n_in + n_out:]

            @pl.when(pl.program_id(0) == 0)
            def _():
                for r in acc_refs:
                    r[...] = jnp.zeros_like(r)

            for r, v in zip(acc_refs, a):
                r[...] += v

    if reverse:
        rowmap = lambda i: (nb - 1 - i, 0)
    else:
        rowmap = lambda i: (i, 0)
    in_specs = [pl.BlockSpec(bs, im) for (_, bs, im) in ins]
    out_specs = [pl.BlockSpec((tm, w), rowmap) for (w, _) in outs]
    out_specs += [pl.BlockSpec((r, w), lambda i: (0, 0)) for (r, w) in accs]
    out_shape = [jax.ShapeDtypeStruct((n_rows, w), dt) for (w, dt) in outs]
    out_shape += [jax.ShapeDtypeStruct((r, w), F32) for (r, w) in accs]
    res = pl.pallas_call(
        body, name=name, grid=(nb,), in_specs=in_specs, out_specs=out_specs, out_shape=out_shape,
        compiler_params=_cparams(("arbitrary",)),
    )(*[a for a, _, _ in ins])
    return (res[:n_out], res[n_out:]) if n_acc else res


def _mm(name, a, b, mode, out_dtype, tm, tn, tk, res=None):
    if mode == "nn":
        (m, k), n = a.shape, b.shape[1]
    elif mode == "nt":
        (m, k), n = a.shape, b.shape[0]
    else:
        (k, m), n = a.shape, b.shape[1]
    tm, tn, tk = min(tm, m), min(tn, n), min(tk, k)
    assert m % tm == 0 and n % tn == 0 and k % tk == 0, (name, m, n, k, tm, tn, tk)
    if mode == "nn":
        a_spec = pl.BlockSpec((tm, tk), lambda i, j, kk: (i, kk))
        b_spec = pl.BlockSpec((tk, tn), lambda i, j, kk: (kk, j))
        dims = NN
    elif mode == "nt":
        a_spec = pl.BlockSpec((tm, tk), lambda i, j, kk: (i, kk))
        b_spec = pl.BlockSpec((tn, tk), lambda i, j, kk: (j, kk))
        dims = NT
    else:
        a_spec = pl.BlockSpec((tk, tm), lambda i, j, kk: (kk, i))
        b_spec = pl.BlockSpec((tk, tn), lambda i, j, kk: (kk, j))
        dims = TN
    nk = k // tk
    has_res = res is not None

    def body(*refs):
        a_ref, b_ref = refs[0], refs[1]
        r_ref = refs[2] if has_res else None
        o_ref = refs[3] if has_res else refs[2]
        part = _dot(a_ref[...], b_ref[...], dims)

        def finish(val):
            if has_res:
                val = val + r_ref[...]
            o_ref[...] = val.astype(o_ref.dtype)

        if nk == 1:
            finish(part)
        else:
            acc_ref = refs[-1]
            kk = pl.program_id(2)

            @pl.when(kk == 0)
            def _():
                acc_ref[...] = part

            @pl.when(kk > 0)
            def _():
                acc_ref[...] += part

            @pl.when(kk == nk - 1)
            def _():
                finish(acc_ref[...])

    in_specs = [a_spec, b_spec]
    args = [a, b]
    if has_res:
        in_specs.append(pl.BlockSpec((tm, tn), lambda i, j, kk: (i, j)))
        args.append(res)
    return pl.pallas_call(
        body, name=name, grid=(m // tm, n // tn, nk), in_specs=in_specs,
        out_specs=pl.BlockSpec((tm, tn), lambda i, j, kk: (i, j)),
        out_shape=jax.ShapeDtypeStruct((m, n), out_dtype),
        scratch_shapes=[pltpu.VMEM((tm, tn), F32)] if nk > 1 else [],
        compiler_params=_cparams(("arbitrary", "arbitrary", "arbitrary")),
    )(*args)


def _mm_sum_nt(name, pieces, w, extra, tm, tn):
    n_p = len(pieces)
    m, k = pieces[0].shape
    n = w.shape[0]
    xa, xb = extra
    ke = xa.shape[1]
    tm, tn = min(tm, m), min(tn, n)

    def body(*refs):
        p_refs, w_ref, xa_ref, xb_ref, o_ref, acc_ref = refs[:n_p], refs[n_p], refs[n_p + 1], refs[n_p + 2], refs[-2], refs[-1]
        kk = pl.program_id(2)

        @pl.when(kk == 0)
        def _():
            acc_ref[...] = jnp.zeros_like(acc_ref)

        for i in range(n_p):
            @pl.when(kk == i)
            def _(i=i):
                acc_ref[...] += _dot(p_refs[i][...], w_ref[...], NT)

        @pl.when(kk == n_p)
        def _():
            o_ref[...] = acc_ref[...] + _dot(xa_ref[...], xb_ref[...], NT)

    in_specs = [pl.BlockSpec((tm, k), lambda i, j, kk: (i, 0)) for _ in range(n_p)]
    in_specs.append(pl.BlockSpec((tn, k), lambda i, j, kk: (j, jnp.minimum(kk, n_p - 1))))
    in_specs += [pl.BlockSpec((tm, ke), lambda i, j, kk: (i, 0)), pl.BlockSpec((tn, ke), lambda i, j, kk: (j, 0))]
    return pl.pallas_call(
        body, name=name, grid=(m // tm, n // tn, n_p + 1), in_specs=in_specs,
        out_specs=pl.BlockSpec((tm, tn), lambda i, j, kk: (i, j)),
        out_shape=jax.ShapeDtypeStruct((m, n), F32),
        scratch_shapes=[pltpu.VMEM((tm, tn), F32)],
        compiler_params=_cparams(("arbitrary", "arbitrary", "arbitrary")),
    )(*pieces, w, xa, xb)


def _rms_fwd(name, x, gain, tm=256):
    s = x.shape[0]

    def body(x_ref, g_ref, y_ref, yt_ref):
        xb = x_ref[...]
        y = xb * lax.rsqrt(jnp.mean(xb * xb, axis=-1, keepdims=True) + EPS) * g_ref[...]
        y_ref[...] = y.astype(BF16)
        yt_ref[...] = y.T.astype(BF16)

    return pl.pallas_call(
        body, name=name, grid=(s // tm,),
        in_specs=[pl.BlockSpec((tm, D), lambda i: (i, 0)), pl.BlockSpec((1, D), lambda i: (0, 0))],
        out_specs=[pl.BlockSpec((tm, D), lambda i: (i, 0)), pl.BlockSpec((D, tm), lambda i: (0, i))],
        out_shape=[jax.ShapeDtypeStruct((s, D), BF16), jax.ShapeDtypeStruct((D, s), BF16)],
        compiler_params=_cparams(("arbitrary",)),
    )(x, gain)


def _rms_bwd(name, x, gain, dns, dres, tm=256):
    s = x.shape[0]
    n_dn = len(dns)

    def fn(xb, g, *rest):
        dn = rest[0]
        for t in rest[1:n_dn]:
            dn = dn + t
        r = lax.rsqrt(jnp.mean(xb * xb, axis=-1, keepdims=True) + EPS)
        xhat = xb * r
        dxh = dn * g
        dx = r * (dxh - xhat * jnp.mean(dxh * xhat, axis=-1, keepdims=True)) + rest[n_dn]
        return (dx,), (jnp.sum(dn * xhat, axis=0, keepdims=True),)

    ins = [_rb(x, tm, D), _cst(gain)] + [_rb(t, tm, D) for t in dns] + [_rb(dres, tm, D)]
    return _rows(name, fn, s, tm, ins, [(D, F32)], [(1, D)])


def _final(h2, target, gain, tm=256):
    s = h2.shape[0]

    def fn(hb, tb, g):
        r = lax.rsqrt(jnp.mean(hb * hb, axis=-1, keepdims=True) + EPS)
        xhat = hb * r
        e = xhat * g - tb
        dy = e * (1.0 / D)
        dxh = dy * g
        dh = r * (dxh - xhat * jnp.mean(dxh * xhat, axis=-1, keepdims=True))
        lrow = 0.5 * jnp.sum(jnp.sum(e * e, axis=-1, keepdims=True) * (1.0 / D), axis=0, keepdims=True)
        return (dh,), (jnp.sum(dy * xhat, axis=0, keepdims=True), jnp.broadcast_to(lrow, (1, LANES)))

    return _rows("final_norm_loss", fn, s, tm, [_rb(h2, tm, D), _rb(target, tm, D), _cst(gain)],
                 [(D, F32)], [(1, D), (1, LANES)])


def _merge_fwd(pa, pb, proj, tm=256):
    s = pa.shape[0]

    def fn(a, b, ga, gb):
        return (_sigmoid(ga) * a + _sigmoid(gb) * b,), ()

    ins = [_rb(pa, tm, D), _rb(pb, tm, D), _rb(proj, tm, D, 7), _rb(proj, tm, D, 8)]
    return _rows("merge_fwd", fn, s, tm, ins, [(D, BF16)])[0]


def _merge_bwd(dmerged, pa, pb, proj, tm=256):
    s = pa.shape[0]

    def fn(dm, a, b, ga, gb):
        sa, sb = _sigmoid(ga), _sigmoid(gb)
        return (dm * sa, dm * sb, dm * a * sa * (1.0 - sa), dm * b * sb * (1.0 - sb)), ()

    ins = [_rb(dmerged, tm, D), _rb(pa, tm, D), _rb(pb, tm, D), _rb(proj, tm, D, 7), _rb(proj, tm, D, 8)]
    return _rows("merge_bwd", fn, s, tm, ins, [(D, BF16), (D, BF16), (D, BF16), (D, BF16)])


def _gelu_parts(x):
    cdf = 0.5 * (1.0 + lax.erf(x * 0.7071067811865476))
    pdf = 0.3989422804014327 * jnp.exp(-0.5 * x * x)
    return x * cdf, cdf + x * pdf


def _conv_taps(u_ext, n_out, first):
    n = u_ext.shape[0]
    cur = u_ext[8:8 + n_out]
    m1 = pltpu.roll(u_ext, 1, 0)[8:8 + n_out]
    m2 = pltpu.roll(u_ext, 2, 0)[8:8 + n_out]
    return m2, m1, cur


def _convglu_fwd(u, conv_w8, conv_b, tm=64):
    s, w = u.shape
    tb = tm // 8

    def fn(ub, up, cw, cb):
        i = pl.program_id(0)
        up = jnp.where(i == 0, 0.0, up)
        m2, m1, cur = _conv_taps(jnp.concatenate([up, ub], axis=0), tm, None)
        acc = cb + cw[0:1] * m2 + cw[1:2] * m1 + cw[2:3] * cur
        act, _ = _gelu_parts(acc[:, :D_FF])
        return (act * acc[:, D_FF:],), ()

    ins = [_rb(u, tm, w), (u, (8, w), lambda i: (jnp.maximum(i * tb - 1, 0), 0)), _cst(conv_w8), _cst(conv_b)]
    return _rows("convglu_fwd", fn, s, tm, ins, [(D_FF, BF16)])[0]


def _convglu_bwd(u, dact, conv_w8, conv_b, tm=64):
    s, w = u.shape
    tb = tm // 8
    nb = s // tm

    def fn(ub, up, un, db, dn, cw, cb):
        i = pl.program_id(0)
        up = jnp.where(i == 0, 0.0, up)
        dn = jnp.where(i == nb - 1, 0.0, dn)
        ne = tm + 8
        m2, m1, cur = _conv_taps(jnp.concatenate([up, ub, un], axis=0), ne, None)
        acc = cb + cw[0:1] * m2 + cw[1:2] * m1 + cw[2:3] * cur
        de = jnp.concatenate([db, dn], axis=0)
        gl, dgl = _gelu_parts(acc[:, :D_FF])
        dacc = jnp.concatenate([de * acc[:, D_FF:] * dgl, de * gl], axis=1)
        p1 = pltpu.roll(dacc, ne - 1, 0)[:tm]
        p2 = pltpu.roll(dacc, ne - 2, 0)[:tm]
        d0 = dacc[:tm]
        du = cw[2:3] * d0 + cw[1:2] * p1 + cw[0:1] * p2
        zero5 = jnp.zeros((5, w), F32)
        dcw = jnp.concatenate([
            jnp.sum(d0 * m2[:tm], axis=0, keepdims=True), jnp.sum(d0 * m1[:tm], axis=0, keepdims=True),
            jnp.sum(d0 * cur[:tm], axis=0, keepdims=True), zero5], axis=0)
        return (du,), (dcw, jnp.sum(d0, axis=0, keepdims=True))

    ins = [
        _rb(u, tm, w),
        (u, (8, w), lambda i: (jnp.maximum(i * tb - 1, 0), 0)),
        (u, (8, w), lambda i: (jnp.minimum((i + 1) * tb, s // 8 - 1), 0)),
        _rb(dact, tm, D_FF),
        (dact, (8, D_FF), lambda i: (jnp.minimum((i + 1) * tb, s // 8 - 1), 0)),
        _cst(conv_w8), _cst(conv_b),
    ]
    return _rows("convglu_bwd", fn, s, tm, ins, [(w, BF16)], [(8, w), (1, w)])


def _chunk_scan(x, t_iota, reverse):
    k = 1
    while k < HG_C:
        if reverse:
            x = x + jnp.where(t_iota < HG_C - k, pltpu.roll(x, HG_C - k, 0), 0.0)
        else:
            x = x + jnp.where(t_iota >= k, pltpu.roll(x, k, 0), 0.0)
        k *= 2
    return x


def _hg_gates(hq, hf, lb):
    sq = _sigmoid(hq)
    q = hq * sq
    sg = _sigmoid(hf)
    f = lb + (1.0 - lb) * sg
    return q, sq, sg, f, 1.0 - f, jnp.log(f)


def _lb_of(logits):
    l0, l1 = logits[0:1], logits[1:2]
    mx = jnp.maximum(l0, l1)
    e0, e1 = jnp.exp(l0 - mx), jnp.exp(l1 - mx)
    return e0 / (e0 + e1)


def _tri(n, lower):
    r = lax.broadcasted_iota(jnp.int32, (n, n), 0)
    c = lax.broadcasted_iota(jnp.int32, (n, n), 1)
    return jnp.where((r >= c) if lower else (r <= c), 1.0, 0.0).astype(BF16)


def _hg_intra_terms(q, kk, b, t_iota):
    ws, ps = [], []
    for s in range(HG_C):
        p = jnp.where(t_iota >= s, jnp.exp(b - b[s:s + 1]), 0.0)
        ps.append(p)
        ws.append(q * kk[s:s + 1] * p)
    return jnp.concatenate(ws, axis=0), ps


def _hg_fwd(proj, lb_logits):
    s = proj.shape[0]
    nt = s // HG_T
    nc = HG_T // HG_C

    def body(q_ref, f_ref, i_ref, l_ref, o_ref, st_ref, state):
        @pl.when(pl.program_id(1) == 0)
        def _():
            state[...] = jnp.zeros_like(state)

        st_ref[0, 0] = state[...]
        lb = _lb_of(l_ref[...])
        ones = jnp.ones((HG_DK, HG_DK), BF16)
        t_iota = lax.broadcasted_iota(jnp.int32, (HG_C, HG_DK), 0)
        cc = HG_C * HG_C

        def group(gi, st):
            units = []
            for u in range(HG_UNROLL):
                r = pl.ds(pl.multiple_of((gi * HG_UNROLL + u) * HG_C, HG_C), HG_C)
                q, _, _, _, kk, g = _hg_gates(q_ref[r, :], f_ref[r, :], lb)
                b = _chunk_scan(g, t_iota, False)
                b_end = b[HG_C - 1:HG_C]
                w_all, _ = _hg_intra_terms(q, kk, b, t_iota)
                units.append((r, i_ref[r, :], q * jnp.exp(b), jnp.exp(b_end), kk * jnp.exp(b_end - b), w_all))
            a_all = _dot(jnp.concatenate([un[5] for un in units], axis=0), ones, NN)
            kvs = [_dot(v, kd, TN) for (_, v, _, _, kd, _) in units]
            sts = [st]
            for (_, _, _, dec, _, _), kv in zip(units, kvs):
                sts.append(sts[-1] * dec + kv)
            for ui, (r, v, qd, _, _, _) in enumerate(units):
                o = _dot(qd, sts[ui], NT)
                for si in range(HG_C):
                    o = o + a_all[ui * cc + si * HG_C:ui * cc + (si + 1) * HG_C] * v[si:si + 1]
                o_ref[r, :] = o
            return sts[-1]

        state[...] = lax.fori_loop(0, nc // HG_UNROLL, group, state[...])

    col = lambda off: pl.BlockSpec((HG_T, HG_DK), lambda h, t: (t, off + h))
    return pl.pallas_call(
        body, name="hgrn2_fwd", grid=(HG_H, nt),
        in_specs=[col(0), col(8), col(16), pl.BlockSpec((2, HG_DK), lambda h, t: (0, h))],
        out_specs=[pl.BlockSpec((HG_T, HG_DK), lambda h, t: (t, h)),
                   pl.BlockSpec((1, 1, HG_DK, HG_DK), lambda h, t: (h, t, 0, 0))],
        out_shape=[jax.ShapeDtypeStruct((s, D), F32), jax.ShapeDtypeStruct((HG_H, nt, HG_DK, HG_DK), F32)],
        scratch_shapes=[pltpu.VMEM((HG_DK, HG_DK), F32)],
        compiler_params=_cparams(("arbitrary", "arbitrary")),
    )(proj, proj, proj, lb_logits)


def _hg_bwd(proj, lb_logits, states, do_raw):
    s = proj.shape[0]
    nt = s // HG_T
    nc = HG_T // HG_C

    def body(q_ref, f_ref, i_ref, l_ref, st_ref, do_ref, dq_ref, df_ref, di_ref, dl_ref, st_all, adj):
        tb = pl.program_id(1)

        @pl.when(tb == 0)
        def _():
            adj[...] = jnp.zeros_like(adj)
            dl_ref[...] = jnp.zeros_like(dl_ref)

        lb = _lb_of(l_ref[...])
        ones = jnp.ones((HG_DK, HG_DK), BF16)
        t_iota = lax.broadcasted_iota(jnp.int32, (HG_C, HG_DK), 0)
        cc = HG_C * HG_C

        def fwd_group(gi, st):
            terms = []
            for u in range(HG_UNROLL):
                ci = gi * HG_UNROLL + u
                r = pl.ds(pl.multiple_of(ci * HG_C, HG_C), HG_C)
                _, _, _, _, kk, g = _hg_gates(q_ref[r, :], f_ref[r, :], lb)
                b = _chunk_scan(g, t_iota, False)
                b_end = b[HG_C - 1:HG_C]
                terms.append((ci, jnp.exp(b_end), _dot(i_ref[r, :], kk * jnp.exp(b_end - b), TN)))
            for ci, dec, kv in terms:
                st_all[ci] = st
                st = st * dec + kv
            return st

        lax.fori_loop(0, nc // HG_UNROLL, fwd_group, st_ref[0, 0])

        def bwd_group(gj, dlb):
            units = []
            for u in range(HG_UNROLL_BWD):
                ci = nc - 1 - (gj * HG_UNROLL_BWD + u)
                r = pl.ds(pl.multiple_of(ci * HG_C, HG_C), HG_C)
                hq, hf, v, do = q_ref[r, :], f_ref[r, :], i_ref[r, :], do_ref[r, :]
                q, sq, sg, f, kk, g = _hg_gates(hq, hf, lb)
                b = _chunk_scan(g, t_iota, False)
                b_end = b[HG_C - 1:HG_C]
                e_b, e_be, dec = jnp.exp(b), jnp.exp(b_end - b), jnp.exp(b_end)
                w_all, ps = _hg_intra_terms(q, kk, b, t_iota)
                x_all = jnp.concatenate([do * v[si:si + 1] for si in range(HG_C)], axis=0)
                units.append(dict(ci=ci, r=r, hq=hq, v=v, do=do, q=q, sq=sq, sg=sg, f=f, kk=kk, e_b=e_b, e_be=e_be,
                                  dec=dec, kd=kk * e_be, w=w_all, ps=ps, x=x_all))
            both = _dot(jnp.concatenate([un["w"] for un in units] + [un["x"] for un in units], axis=0), ones, NN)
            st0s = [st_all[un["ci"]] for un in units]
            st_ends = [st0 * un["dec"] + _dot(un["v"], un["kd"], TN) for un, st0 in zip(units, st0s)]
            dqks = [_dot(un["do"], un["q"] * un["e_b"], TN) for un in units]
            es = [adj[...]]
            for un, dqk in zip(units, dqks):
                es.append(es[-1] * un["dec"] + dqk)
            adj[...] = es[-1]
            for ui, un in enumerate(units):
                e, q, kk, v, do = es[ui], un["q"], un["kk"], un["v"], un["do"]
                tail = jnp.sum(e * st_ends[ui], axis=0, keepdims=True)
                dq = un["e_b"] * _dot(do, st0s[ui], NN)
                dk = un["e_be"] * _dot(v, e, NN)
                dv = _dot(un["kd"], e, NT)
                a0 = ui * cc
                d0 = (HG_UNROLL_BWD + ui) * cc
                for si in range(HG_C):
                    da = both[d0 + si * HG_C:d0 + (si + 1) * HG_C]
                    aa = both[a0 + si * HG_C:a0 + (si + 1) * HG_C]
                    dap = da * un["ps"][si]
                    dq = dq + dap * kk[si:si + 1]
                    hit = t_iota == si
                    dk = dk + jnp.where(hit, jnp.sum(dap * q, axis=0, keepdims=True), 0.0)
                    dv = dv + jnp.where(hit, jnp.sum(aa * do, axis=0, keepdims=True), 0.0)
                dg = _chunk_scan(q * dq - kk * dk, t_iota, True) + tail
                dfg = dg / un["f"] - dk
                sq, sg, hq, r = un["sq"], un["sg"], un["hq"], un["r"]
                dq_ref[r, :] = (dq * sq * (1.0 + hq * (1.0 - sq))).astype(dq_ref.dtype)
                df_ref[r, :] = (dfg * (1.0 - lb) * sg * (1.0 - sg)).astype(df_ref.dtype)
                di_ref[r, :] = dv.astype(di_ref.dtype)
                dlb = dlb + jnp.sum(dfg * (1.0 - sg), axis=0, keepdims=True)
            return dlb

        dlb = lax.fori_loop(0, nc // HG_UNROLL_BWD, bwd_group, jnp.zeros((1, HG_DK), F32))
        dl0 = dlb * lb * (1.0 - lb)
        dl_ref[...] += jnp.concatenate([dl0, -dl0], axis=0)

    col = lambda off: pl.BlockSpec((HG_T, HG_DK), lambda h, t: (nt - 1 - t, off + h))
    out_col = pl.BlockSpec((HG_T, HG_DK), lambda h, t: (nt - 1 - t, h))
    return pl.pallas_call(
        body, name="hgrn2_bwd", grid=(HG_H, nt),
        in_specs=[col(0), col(8), col(16), pl.BlockSpec((2, HG_DK), lambda h, t: (0, h)),
                  pl.BlockSpec((1, 1, HG_DK, HG_DK), lambda h, t: (h, nt - 1 - t, 0, 0)), col(0)],
        out_specs=[out_col, out_col, out_col, pl.BlockSpec((2, HG_DK), lambda h, t: (0, h))],
        out_shape=[jax.ShapeDtypeStruct((s, D), BF16)] * 3 + [jax.ShapeDtypeStruct((2, D), F32)],
        scratch_shapes=[pltpu.VMEM((nc, HG_DK, HG_DK), F32), pltpu.VMEM((HG_DK, HG_DK), F32)],
        compiler_params=_cparams(("arbitrary", "arbitrary")),
    )(proj, proj, proj, lb_logits, states, do_raw)


def _hg_post_fwd(o_raw, proj, gnorm, tm=256):
    s = o_raw.shape[0]

    def fn(o, hg, gn):
        outs = []
        for h in range(HG_H):
            sl = slice(h * HG_DK, (h + 1) * HG_DK)
            oh, gh = o[:, sl], hg[:, sl]
            r = lax.rsqrt(jnp.mean(oh * oh, axis=-1, keepdims=True) + EPS)
            outs.append(oh * r * gn * (gh * _sigmoid(gh)))
        return (jnp.concatenate(outs, axis=1),), ()

    return _rows("hgrn2_out_fwd", fn, s, tm, [_rb(o_raw, tm, D), _rb(proj, tm, D, 3), _cst(gnorm)], [(D, BF16)])[0]


def _hg_post_bwd(do_a, o_raw, proj, gnorm, tm=256):
    s = o_raw.shape[0]

    def fn(da, o, hg, gn):
        dos, dhgs = [], []
        dgn = jnp.zeros((1, HG_DK), F32)
        for h in range(HG_H):
            sl = slice(h * HG_DK, (h + 1) * HG_DK)
            oh, gh, dh = o[:, sl], hg[:, sl], da[:, sl]
            r = lax.rsqrt(jnp.mean(oh * oh, axis=-1, keepdims=True) + EPS)
            xhat = oh * r
            sg = _sigmoid(gh)
            dy = dh * (gh * sg)
            dhgs.append(dh * xhat * gn * sg * (1.0 + gh * (1.0 - sg)))
            dgn = dgn + jnp.sum(dy * xhat, axis=0, keepdims=True)
            dxh = dy * gn
            dos.append(r * (dxh - xhat * jnp.mean(dxh * xhat, axis=-1, keepdims=True)))
        return (jnp.concatenate(dos, axis=1), jnp.concatenate(dhgs, axis=1)), (dgn,)

    ins = [_rb(do_a, tm, D), _rb(o_raw, tm, D), _rb(proj, tm, D, 3), _cst(gnorm)]
    return _rows("hgrn2_out_bwd", fn, s, tm, ins, [(D, F32), (D, BF16)], [(1, HG_DK)])


def _log_sigmoid(z):
    return jnp.minimum(z, 0.0) - jnp.log(1.0 + jnp.exp(-jnp.abs(z)))


def _fox_gate_bwd(dct, pff, bias, tm=256):
    s = pff.shape[0]
    nb = s // tm

    def body(d_ref, p_ref, b_ref, dff_ref, db_ref, carry):
        @pl.when(pl.program_id(0) == 0)
        def _():
            carry[...] = jnp.zeros_like(carry)
            db_ref[...] = jnp.zeros_like(db_ref)

        dc = d_ref[...].T
        dlf = _split_dot(_tri(tm, False), dc, 3) + carry[0:1]
        carry[...] = jnp.broadcast_to(dlf[0:1], carry.shape)
        dff = dlf * _sigmoid(-(p_ref[...] + b_ref[...]))
        dff_ref[...] = dff
        db_ref[...] += jnp.sum(dff, axis=0, keepdims=True)

    return pl.pallas_call(
        body, name="fox_gate_bwd", grid=(nb,),
        in_specs=[pl.BlockSpec((LANES, tm), lambda i: (0, nb - 1 - i)),
                  pl.BlockSpec((tm, LANES), lambda i: (nb - 1 - i, 0)), pl.BlockSpec((1, LANES), lambda i: (0, 0))],
        out_specs=[pl.BlockSpec((tm, LANES), lambda i: (nb - 1 - i, 0)), pl.BlockSpec((1, LANES), lambda i: (0, 0))],
        out_shape=[jax.ShapeDtypeStruct((s, LANES), F32), jax.ShapeDtypeStruct((1, LANES), F32)],
        scratch_shapes=[pltpu.VMEM((8, LANES), F32)],
        compiler_params=_cparams(("arbitrary",)),
    )(dct, pff, bias)


def _diag_mask(t):
    r = lax.broadcasted_iota(jnp.int32, (t, t), 0)
    c = lax.broadcasted_iota(jnp.int32, (t, t), 1)
    return r >= c


AUX_ONES = 6


def _pieces(x):
    h = x.astype(BF16)
    r = x - h.astype(F32)
    m = r.astype(BF16)
    return h, m, (r - m.astype(F32)).astype(BF16)


def _lane_put(lane, cols, base):
    out = None
    for i, col in enumerate(cols):
        term = jnp.where(lane == base + i, col.astype(F32), 0.0)
        out = term if out is None else out + term
    return out


def _fox_prep2(proj, pff, bias, tm=256):
    s = pff.shape[0]

    def body(q_ref, k_ref, v_ref, p_ref, b_ref, qb_ref, kb_ref, vb_ref, ka_ref, carry):
        @pl.when(pl.program_id(0) == 0)
        def _():
            carry[...] = jnp.zeros_like(carry)

        qb_ref[...] = (q_ref[...] * 0.125).astype(BF16)
        kb_ref[...] = k_ref[...].astype(BF16)
        vb_ref[...] = v_ref[...].astype(BF16)
        lf = _log_sigmoid(p_ref[...] + b_ref[...])
        c = _split_dot(_tri(tm, True), lf, 3) + carry[0:1]
        carry[...] = jnp.broadcast_to(c[tm - 1:tm], carry.shape)
        lane = lax.broadcasted_iota(jnp.int32, (tm, LANES), 1)
        ones = jnp.where((lane >= AUX_ONES) & (lane < AUX_ONES + 6), 1.0, 0.0)
        for p in range(FOX_H // 2):
            aux = ones
            for z in range(2):
                col = jnp.sum(jnp.where(lane == 2 * p + z, c, 0.0), axis=1, keepdims=True)
                aux = aux + _lane_put(lane, _pieces(-col), 3 * z)
            ka_ref[:, p * LANES:(p + 1) * LANES] = aux.astype(BF16)

    row = lambda cb: pl.BlockSpec((tm, D), lambda i: (i, cb))
    return pl.pallas_call(
        body, name="fox_prep", grid=(s // tm,),
        in_specs=[row(4), row(5), row(6), pl.BlockSpec((tm, LANES), lambda i: (i, 0)),
                  pl.BlockSpec((1, LANES), lambda i: (0, 0))],
        out_specs=[row(0)] * 4, out_shape=[jax.ShapeDtypeStruct((s, D), BF16)] * 4,
        scratch_shapes=[pltpu.VMEM((8, LANES), F32)],
        compiler_params=_cparams(("arbitrary",)),
    )(proj, proj, proj, pff, bias)


def _fox_fwd2(qb, kb, vb, ka):
    s = qb.shape[0]
    t = min(FOX_T, s)
    nq = s // t

    def body(q_ref, k_ref, v_ref, ka_ref, o_ref, la_ref):
        i = pl.program_id(1)
        lane = lax.broadcasted_iota(jnp.int32, (t, LANES), 1)
        in_a = lane < FOX_D
        q = q_ref[...]
        zero = jnp.zeros_like(q)
        qh = [jnp.where(in_a, q, zero), jnp.where(in_a, zero, q)]
        c_ones = [jnp.where((lane >= 3 * z) & (lane < 3 * z + 3), 1.0, 0.0) for z in range(2)]

        def keys(j):
            rows = pl.ds(pl.multiple_of(j * t, t), t)
            return jnp.concatenate([k_ref[rows, :], ka_ref[rows, :]], axis=1), rows

        dmask = _diag_mask(t)

        def logits(qx, kk, masked):
            e = lax.dot_general(qx, kk, (NT, ((), ())), preferred_element_type=F32)
            return jnp.where(dmask, e, -1e30) if masked else e

        qc = [jnp.concatenate([qh[z], c_ones[z].astype(BF16)], axis=1) for z in range(2)]

        def step(j, carry, masked):
            kk, rows = keys(j)
            vj = v_ref[rows, :]
            scores = [logits(qc[z], kk, masked) for z in range(2)]
            one = jnp.ones_like(vj)
            vh = [jnp.where(in_a, vj, one), jnp.where(in_a, one, vj)]
            out = []
            for z in range(2):
                m, acc = carry[z]
                m_new = jnp.maximum(m, jnp.max(scores[z], axis=1, keepdims=True))
                p = jnp.exp(scores[z] - m_new)
                out.append((m_new, jnp.exp(m - m_new) * acc + _dot(p, vh[z], NN)))
            return tuple(out)

        init = tuple((jnp.full((t, 1), -1e30, F32), jnp.zeros((t, LANES), F32)) for _ in range(2))
        (ma, acc_a), (mb, acc_b) = step(i, lax.fori_loop(0, i, lambda j, c: step(j, c, False), init), True)
        la = jnp.sum(jnp.where(lane == FOX_D, acc_a, 0.0), axis=1, keepdims=True)
        lb = jnp.sum(jnp.where(lane == 0, acc_b, 0.0), axis=1, keepdims=True)
        o_ref[...] = jnp.where(in_a, acc_a / la, acc_b / lb).astype(o_ref.dtype)
        la_ref[...] = (_lane_put(lane, _pieces(-(ma + jnp.log(la))), AUX_ONES)
                       + _lane_put(lane, _pieces(-(mb + jnp.log(lb))), AUX_ONES + 3)).astype(la_ref.dtype)

    blk = pl.BlockSpec((t, LANES), lambda p, i: (i, p))
    whole = pl.BlockSpec((s, LANES), lambda p, i: (0, p))
    return pl.pallas_call(
        body, name="fox_attn_fwd", grid=(FOX_H // 2, nq), in_specs=[blk, whole, whole, whole],
        out_specs=[blk, blk], out_shape=[jax.ShapeDtypeStruct((s, D), BF16)] * 2,
        compiler_params=_cparams(("arbitrary", "arbitrary")),
    )(qb, kb, vb, ka)


def _fox_bwd2(qb, kb, vb, ka, ob, laux, dob):
    s = qb.shape[0]
    t = min(FOX_T, s)
    nq = s // t

    def body(q_ref, k_ref, v_ref, ka_ref, o_ref, la_ref, do_ref, dq_ref, dk_ref, dv_ref, dc_ref, dkt, dvt):
        i = pl.program_id(1)

        @pl.when(i == 0)
        def _():
            dkt[...] = jnp.zeros_like(dkt)
            dvt[...] = jnp.zeros_like(dvt)
            dc_ref[...] = jnp.zeros_like(dc_ref)

        lane = lax.broadcasted_iota(jnp.int32, (t, LANES), 1)
        in_a = lane < FOX_D
        q, do, la = q_ref[...], do_ref[...], la_ref[...].astype(F32)
        zero = jnp.zeros_like(q)
        qh = [jnp.where(in_a, q, zero), jnp.where(in_a, zero, q)]
        doh = [jnp.where(in_a, do, zero), jnp.where(in_a, zero, do)]
        qt = [h.astype(F32).T.astype(BF16) for h in qh]
        dot_ = [h.astype(F32).T.astype(BF16) for h in doh]
        prod = do.astype(F32) * o_ref[...].astype(F32)
        qx, dox = [], []
        for z in range(2):
            delta = jnp.sum(jnp.where(in_a if z == 0 else ~in_a, prod, 0.0), axis=1, keepdims=True)
            c_ones = jnp.where((lane >= 3 * z) & (lane < 3 * z + 3), 1.0, 0.0)
            lse_lanes = (lane >= AUX_ONES + 3 * z) & (lane < AUX_ONES + 3 * z + 3)
            qx.append(jnp.concatenate([qh[z], (c_ones + jnp.where(lse_lanes, la, 0.0)).astype(BF16)], axis=1))
            dox.append(jnp.concatenate([doh[z], _lane_put(lane, _pieces(-delta), 3 * z).astype(BF16)], axis=1))
        v_ones = jnp.where(lane < 6, 1.0, 0.0).astype(BF16)
        dmask = _diag_mask(t)

        def step(j, carry, masked):
            rows = pl.ds(pl.multiple_of(j * t, t), t)
            kj, vj = k_ref[rows, :], v_ref[rows, :]
            kk = jnp.concatenate([kj, ka_ref[rows, :]], axis=1)
            vv = jnp.concatenate([vj, v_ones], axis=1)
            out = []
            dk_add, dv_add = None, None
            for z in range(2):
                dq, rsum = carry[z]
                e = lax.dot_general(qx[z], kk, (NT, ((), ())), preferred_element_type=F32)
                if masked:
                    e = jnp.where(dmask, e, -1e30)
                p = jnp.exp(e)
                ds = p * lax.dot_general(dox[z], vv, (NT, ((), ())), preferred_element_type=F32)
                dkz, dvz = _dot(qt[z], ds, NN), _dot(dot_[z], p, NN)
                dk_add = dkz if dk_add is None else dk_add + dkz
                dv_add = dvz if dv_add is None else dv_add + dvz
                dc_ref[0, z, j] += -jnp.sum(ds, axis=0, keepdims=True)
                out.append((dq + _dot(ds, kj, NN), rsum + jnp.sum(ds, axis=1, keepdims=True)))
            dkt[j] += dk_add
            dvt[j] += dv_add
            return tuple(out)

        init = tuple((jnp.zeros((t, LANES), F32), jnp.zeros((t, 1), F32)) for _ in range(2))
        (dq_a, rs_a), (dq_b, rs_b) = step(i, lax.fori_loop(0, i, lambda j, c: step(j, c, False), init), True)
        for z, rs in enumerate((rs_a, rs_b)):
            dc_ref[0, z, i] += jnp.transpose(jnp.broadcast_to(rs, (t, LANES)))[0:1]
        dq_ref[...] = (jnp.where(in_a, dq_a, dq_b) * 0.125).astype(dq_ref.dtype)

        @pl.when(i == nq - 1)
        def _():
            for jb in range(nq):
                dk_ref[jb * t:(jb + 1) * t, :] = dkt[jb].T
                dv_ref[jb * t:(jb + 1) * t, :] = dvt[jb].T

    blk = pl.BlockSpec((t, LANES), lambda p, i: (i, p))
    whole = pl.BlockSpec((s, LANES), lambda p, i: (0, p))
    return pl.pallas_call(
        body, name="fox_attn_bwd", grid=(FOX_H // 2, nq),
        in_specs=[blk, whole, whole, whole, blk, blk, blk],
        out_specs=[blk, whole, whole, pl.BlockSpec((1, 2, nq, 1, t), lambda p, i: (p, 0, 0, 0, 0))],
        out_shape=[jax.ShapeDtypeStruct((s, D), BF16), jax.ShapeDtypeStruct((s, D), F32),
                   jax.ShapeDtypeStruct((s, D), F32), jax.ShapeDtypeStruct((FOX_H // 2, 2, nq, 1, t), F32)],
        scratch_shapes=[pltpu.VMEM((nq, LANES, t), F32), pltpu.VMEM((nq, LANES, t), F32)],
        compiler_params=_cparams(("arbitrary", "arbitrary")),
    )(qb, kb, vb, ka, ob, laux, dob)


def _adamw(name, w, g, m, v, tm=None):
    rows, width = w.shape
    tm = rows if tm is None else tm
    c1 = 1.0 - ADAM_B1 ** ADAM_STEP
    c2 = 1.0 - ADAM_B2 ** ADAM_STEP

    def fn(wb, gb, mb, vb):
        m_new = ADAM_B1 * mb + (1.0 - ADAM_B1) * gb
        v_new = ADAM_B2 * vb + (1.0 - ADAM_B2) * (gb * gb)
        delta = -ADAM_LR * ((m_new / c1) / (jnp.sqrt(v_new / c2) + ADAM_EPS) + ADAM_WD * wb)
        return (delta, m_new, v_new), ()

    ins = [_rb(a, tm, width) for a in (w, g, m, v)]
    return _rows(name, fn, rows, tm, ins, [(width, F32)] * 3)


def _me():
    return lax.axis_index("x"), lax.axis_index("y"), lax.axis_index("c")


def _all_gather8(name, block):
    m, n = block.shape

    def body(x_ref, out_ref, send_sems, recv_sems):
        x, y, c = _me()
        me, sibling = (x, y, c), (x, y, 1 - c)
        chips = [(1 - x, y), (x, 1 - y), (1 - x, 1 - y)]

        def slot(px, py, pc):
            return out_ref.at[4 * px + 2 * py + pc]

        def copy(k, blk, to, src=None):
            return pltpu.make_async_remote_copy(
                src_ref=slot(*blk) if src is None else src, dst_ref=slot(*blk),
                send_sem=send_sems.at[k], recv_sem=recv_sems.at[k], device_id=to, device_id_type=MESH)

        first = [copy(0, me, sibling, src=x_ref)]
        first += [copy(1 + j, me, (*chip, c), src=x_ref) for j, chip in enumerate(chips)]
        for cp in first:
            cp.start()
        passed = [copy(4 + j, (*chip, c), sibling) for j, chip in enumerate(chips)]
        for j, chip in enumerate(chips):
            copy(1 + j, (*chip, c), me).wait_recv()
            passed[j].start()
        copy(0, sibling, me).wait_recv()
        for j, chip in enumerate(chips):
            copy(4 + j, (*chip, 1 - c), me).wait_recv()
        for cp in first + passed:
            cp.wait_send()

    gathered = pl.pallas_call(
        body, name=name, in_specs=[ANY], out_specs=ANY,
        out_shape=jax.ShapeDtypeStruct((8, m, n), block.dtype),
        scratch_shapes=[pltpu.SemaphoreType.DMA((7,)), pltpu.SemaphoreType.DMA((7,))],
    )(block)
    x, y, c = _me()
    return lax.dynamic_update_slice(gathered, block[None], (4 * x + 2 * y + c, 0, 0))


def _swap_halves(name, g):
    n, _, m, lanes = g.shape

    def body(g_ref, got_ref, send_sems, recv_sems):
        x, y, c = _me()
        copies = [pltpu.make_async_remote_copy(
            src_ref=g_ref.at[j, 1 - c], dst_ref=got_ref.at[j], send_sem=send_sems.at[j], recv_sem=recv_sems.at[j],
            device_id=(x, y, 1 - c), device_id_type=MESH) for j in range(n)]
        for cp in copies:
            cp.start()
        for cp in copies:
            cp.wait()

    return pl.pallas_call(
        body, name=name, in_specs=[ANY], out_specs=ANY, out_shape=jax.ShapeDtypeStruct((n, m, lanes), g.dtype),
        scratch_shapes=[pltpu.SemaphoreType.DMA((n,)), pltpu.SemaphoreType.DMA((n,))],
    )(g)


def _swap_sibling(name, mine):
    def body(m_ref, out_ref, send_sem, recv_sem):
        x, y, c = _me()
        cp = pltpu.make_async_remote_copy(src_ref=m_ref, dst_ref=out_ref, send_sem=send_sem, recv_sem=recv_sem,
                                          device_id=(x, y, 1 - c), device_id_type=MESH)
        cp.start()
        cp.wait()

    return pl.pallas_call(
        body, name=name, in_specs=[ANY], out_specs=ANY, out_shape=jax.ShapeDtypeStruct(mine.shape, mine.dtype),
        scratch_shapes=[pltpu.SemaphoreType.DMA, pltpu.SemaphoreType.DMA],
    )(mine)


def _chip_exchange(name, p):
    def body(p_ref, out_ref, send_sems, recv_sems):
        x, y, c = _me()
        my_chip = 2 * x + y
        chips = [(1 - x, y), (x, 1 - y), (1 - x, 1 - y)]
        sends = []
        for k, (px, py) in enumerate(chips):
            sends.append(pltpu.make_async_remote_copy(
                src_ref=p_ref.at[2 * px + py], dst_ref=out_ref.at[my_chip], send_sem=send_sems.at[k],
                recv_sem=recv_sems.at[k], device_id=(px, py, c), device_id_type=MESH))
        for cp in sends:
            cp.start()
        for k, (px, py) in enumerate(chips):
            pltpu.make_async_remote_copy(
                src_ref=p_ref.at[my_chip], dst_ref=out_ref.at[2 * px + py], send_sem=send_sems.at[k],
                recv_sem=recv_sems.at[k], device_id=(px, py, c), device_id_type=MESH).wait_recv()
        for cp in sends:
            cp.wait_send()

    got = pl.pallas_call(
        body, name=name, in_specs=[ANY], out_specs=ANY, out_shape=jax.ShapeDtypeStruct(p.shape, p.dtype),
        scratch_shapes=[pltpu.SemaphoreType.DMA((3,)), pltpu.SemaphoreType.DMA((3,))],
    )(p)
    x, y, _ = _me()
    my_chip = 2 * x + y
    return lax.dynamic_update_slice(got, lax.dynamic_index_in_dim(p, my_chip, axis=0, keepdims=True), (my_chip, 0, 0))


def _all_reduce_small(name, block):
    r, n = block.shape

    def body(x_ref, sum_ref, gath, send_sems, recv_sems):
        x, y, c = _me()
        me = 4 * x + 2 * y + c
        gath[me] = x_ref[...]
        sends = []
        for k in range(1, 8):
            px = x ^ ((k >> 2) & 1)
            py = y ^ ((k >> 1) & 1)
            pc = c ^ (k & 1)
            sends.append(pltpu.make_async_remote_copy(
                src_ref=x_ref, dst_ref=gath.at[me], send_sem=send_sems.at[k - 1], recv_sem=recv_sems.at[k - 1],
                device_id=(px, py, pc), device_id_type=MESH))
        for cp in sends:
            cp.start()
        for k in range(1, 8):
            peer = me ^ k
            pltpu.make_async_remote_copy(
                src_ref=x_ref, dst_ref=gath.at[peer], send_sem=send_sems.at[k - 1], recv_sem=recv_sems.at[k - 1],
                device_id=(x, y, c), device_id_type=MESH).wait_recv()
        for cp in sends:
            cp.wait_send()
        acc = gath[0]
        for d in range(1, 8):
            acc = acc + gath[d]
        sum_ref[...] = acc

    vm = pl.BlockSpec(memory_space=pltpu.VMEM)
    return pl.pallas_call(
        body, name=name, in_specs=[vm], out_specs=vm, out_shape=jax.ShapeDtypeStruct((r, n), F32),
        scratch_shapes=[pltpu.VMEM((8, r, n), F32), pltpu.SemaphoreType.DMA((7,)), pltpu.SemaphoreType.DMA((7,))],
    )(block)


def _add2(name, a, b, tm):
    rows = a.shape[0]
    return _rows(name, lambda p, q: ((p + q,), ()), rows, tm, [_rb(a, tm, LANES), _rb(b, tm, LANES)], [(LANES, BF16)])[0]


def _add4(name, p, tm):
    m = p.shape[1]
    flat = p.reshape(4 * m, LANES)
    nb = m // tm
    ins = [(flat, (tm, LANES), (lambda i, j=j: (j * nb + i, 0))) for j in range(4)]
    f32 = lambda v: v.astype(F32)
    return _rows(name, lambda a, b, c, d: ((((f32(a) + f32(b)) + f32(c)) + f32(d),), ()), m, tm, ins, [(LANES, F32)])[0]


SEG_ROWS = (D * W_IN_SHARD // LANES, 256 * D // LANES, 256 * D // LANES, 256 * D // LANES,
            D * W_UP_SHARD // LANES, W_DOWN_SHARD * D // LANES)
GRAD_ROWS = sum(SEG_ROWS)
CONVW_ROWS = 3 * W_UP_SHARD * 2 // LANES
GATHER_ROWS = 41600


def _flat(a):
    return a.reshape(-1, LANES)


def _gather_weights(w_in, w_a, w_b, w_out, w_up, w_down, conv_w):
    c = lax.axis_index("c")
    bits = lax.bitcast_convert_type(conv_w, BF16)
    pieces = [_flat(t.astype(BF16)) for t in (w_in, w_a, w_b, w_out, w_up, w_down)] + [_flat(bits)]
    pad = GATHER_ROWS - GRAD_ROWS - CONVW_ROWS
    shard = jnp.concatenate(pieces + [jnp.zeros((pad, LANES), BF16)], axis=0)
    half = GATHER_ROWS // 2
    mine = lax.dynamic_slice_in_dim(shard, c * half, half, axis=0)
    full = _all_gather8("all_gather_weights", mine).reshape(N_CHIP, GATHER_ROWS, LANES)
    offs = [0]
    for r in SEG_ROWS:
        offs.append(offs[-1] + r)
    seg = lambda i: full[:, offs[i]:offs[i + 1]]
    wi = seg(0).reshape(N_CHIP, D, W_IN_SHARD).transpose(1, 0, 2).reshape(D, N_CHIP * W_IN_SHARD)
    w_main = jnp.concatenate([wi[:, :FF_COL], wi[:, FF_COL + FOX_H:]], axis=1)
    w_ff = jnp.pad(wi[:, FF_COL:FF_COL + FOX_H], ((0, 0), (0, LANES - FOX_H)))
    wa, wb, wo = (seg(i).reshape(D, D) for i in (1, 2, 3))
    wu = seg(4).reshape(N_CHIP, D, W_UP_SHARD).transpose(1, 0, 2).reshape(D, 2 * D_FF)
    wd = seg(5).reshape(D_FF, D)
    cw_bits = full[:, GRAD_ROWS:GRAD_ROWS + CONVW_ROWS].reshape(N_CHIP, 3, W_UP_SHARD, 2)
    cw = lax.bitcast_convert_type(cw_bits, F32).transpose(1, 0, 2).reshape(3, 2 * D_FF)
    return w_main, w_ff, wa, wb, wo, wu, wd, cw


def _reduce_scatter_grads(d_main, d_ff, d_a, d_b, d_o, d_u, d_d):
    c = lax.axis_index("c")
    d_in = jnp.concatenate(d_main[:7] + [d_ff[:, :FOX_H]] + d_main[7:], axis=1)
    per_chip = [
        d_in.reshape(D, N_CHIP, W_IN_SHARD).transpose(1, 0, 2).reshape(N_CHIP, -1, LANES),
        d_a.reshape(N_CHIP, -1, LANES), d_b.reshape(N_CHIP, -1, LANES), d_o.reshape(N_CHIP, -1, LANES),
        d_u.reshape(D, N_CHIP, W_UP_SHARD).transpose(1, 0, 2).reshape(N_CHIP, -1, LANES),
        d_d.reshape(N_CHIP, -1, LANES),
        jnp.zeros((N_CHIP, GATHER_ROWS - GRAD_ROWS, LANES), F32),
    ]
    half = GATHER_ROWS // 2
    g = jnp.concatenate(per_chip, axis=1).reshape(N_CHIP, 2, half, LANES)
    from_sibling = _swap_halves("grad_swap_halves", g)
    mine = lax.dynamic_index_in_dim(g, c, axis=1, keepdims=False)
    tm = half // 5
    chip_sum = _add2("grad_chip_sum", mine.reshape(-1, LANES), from_sibling.reshape(-1, LANES), tm)
    pieces = _chip_exchange("grad_chip_exchange", chip_sum.reshape(N_CHIP, half, LANES))
    mine_half = _add4("grad_sum_chips", pieces, tm)
    other_half = _swap_sibling("grad_share_half", mine_half)
    lo = jnp.where(c == 0, mine_half, other_half)
    hi = jnp.where(c == 0, other_half, mine_half)
    return jnp.concatenate([lo, hi], axis=0)


def _local_step(x, target, norm_mix, fox_f_bias, hg_lb_logits, hg_norm, norm_ffn, conv_b, norm_final,
                w_main, w_ff, wa, wb, wo, wu, wd, conv_w):
    s = x.shape[0]
    bias = jnp.pad(fox_f_bias, ((0, 0), (0, LANES - FOX_H)))
    conv_w8 = jnp.pad(conv_w, ((0, 5), (0, 0)))
    t = min(FOX_T, s)

    n1, n1t = _rms_fwd("norm_mix_fwd", x, norm_mix)
    proj = _mm("in_proj", n1, w_main, "nn", F32, 1024, 1024, D)
    pff = _mm("in_proj_forget", n1, w_ff, "nn", F32, 1024, LANES, D)
    qb, kb, vb, ka = _fox_prep2(proj, pff, bias)
    o_b, laux = _fox_fwd2(qb, kb, vb, ka)
    o_raw, states = _hg_fwd(proj, hg_lb_logits)
    o_a = _hg_post_fwd(o_raw, proj, hg_norm)
    pa = _mm("branch_a", o_a, wa, "nn", F32, 1024, 1024, D)
    pb = _mm("branch_b", o_b, wb, "nn", F32, 1024, 1024, D)
    merged = _merge_fwd(pa, pb, proj)
    h1 = _mm("out_proj", merged, wo, "nn", F32, 1024, 1024, D, res=x)
    n2, n2t = _rms_fwd("norm_ffn_fwd", h1, norm_ffn)
    u = _mm("ffn_up", n2, wu, "nn", F32, 1024, W_UP_SHARD, D)
    act = _convglu_fwd(u, conv_w8, conv_b)
    h2 = _mm("ffn_down", act, wd, "nn", F32, 512, 1024, D_FF, res=h1)
    (dh2,), (d_norm_final, loss_row) = _final(h2, target, norm_final)

    dact = _mm("ffn_down_dx", dh2, wd, "nt", BF16, 1024, D_FF, D)
    d_wd = _mm("ffn_down_dw", act, dh2, "tn", F32, D_FF // 2, 1024, 512)
    (du,), (d_conv_w8, d_conv_b) = _convglu_bwd(u, dact, conv_w8, conv_b)
    dn2 = _mm("ffn_up_dx", du, wu, "nt", F32, 1024, 1024, W_UP_SHARD)
    d_wu = _mm("ffn_up_dw", n2t, du, "nn", F32, 1024, W_UP_SHARD, 512)
    (dh1,), (d_norm_ffn,) = _rms_bwd("norm_ffn_bwd", h1, norm_ffn, [dn2], dh2)

    dmerged = _mm("out_proj_dx", dh1, wo, "nt", F32, 1024, 1024, D)
    d_wo = _mm("out_proj_dw", merged, dh1, "tn", F32, 1024, 1024, 512)
    dpa, dpb, dga, dgb = _merge_bwd(dmerged, pa, pb, proj)
    do_a = _mm("branch_a_dx", dpa, wa, "nt", F32, 1024, 1024, D)
    do_b = _mm("branch_b_dx", dpb, wb, "nt", BF16, 1024, 1024, D)
    d_wa = _mm("branch_a_dw", o_a, dpa, "tn", F32, 1024, 1024, 512)
    d_wb = _mm("branch_b_dw", o_b, dpb, "tn", F32, 1024, 1024, 512)

    (do_raw, dhg), (d_hg_norm,) = _hg_post_bwd(do_a, o_raw, proj, hg_norm)
    dhq, dhf, dhi, d_lb_logits = _hg_bwd(proj, hg_lb_logits, states, do_raw)

    dfq, dfk, dfv, dcrow = _fox_bwd2(qb, kb, vb, ka, o_b, laux, do_b)
    dct = jnp.pad(dcrow.reshape(FOX_H, s), ((0, LANES - FOX_H), (0, 0)))
    dff, d_bias = _fox_gate_bwd(dct, pff, bias)

    pieces = [dhq, dhf, dhi, dhg, dfq, dfk, dfv, dga, dgb]
    dn1 = _mm_sum_nt("in_proj_dx", pieces, w_main, (dff, w_ff), 512, 1024)
    d_w_main = [_mm("in_proj_dw_%d" % i, n1t, p, "nn", F32, 1024, 1024, 512) for i, p in enumerate(pieces)]
    d_w_ff = _mm("in_proj_forget_dw", n1t, dff, "nn", F32, 1024, LANES, 512)
    (dx,), (d_norm_mix,) = _rms_bwd("norm_mix_bwd", x, norm_mix, [dn1], dh1)

    small = dict(norm_mix=d_norm_mix, fox_f_bias=d_bias[:, :FOX_H], hg_lb_logits=d_lb_logits, hg_norm=d_hg_norm,
                 norm_ffn=d_norm_ffn, conv_b=d_conv_b, norm_final=d_norm_final, conv_w=d_conv_w8[:3], loss=loss_row)
    big = (d_w_main, d_w_ff, d_wa, d_wb, d_wo, d_wu, d_wd)
    return dx, small, big


SMALL_KEYS = ("norm_mix", "fox_f_bias", "hg_lb_logits", "hg_norm", "norm_ffn", "conv_b", "norm_final")


def _pack_small(parts):
    rows, layout = [], []
    for key, arr in parts:
        flat = arr.reshape(-1)
        n = flat.shape[0]
        nr = -(-n // LANES)
        rows.append(jnp.pad(flat, (0, nr * LANES - n)).reshape(nr, LANES))
        layout.append((key, arr.shape, n, nr))
    packed = jnp.concatenate(rows, axis=0)
    pad = -packed.shape[0] % 8
    return jnp.pad(packed, ((0, pad), (0, 0))), layout


def _unpack_small(packed, layout):
    out, r0 = {}, 0
    for key, shape, n, nr in layout:
        out[key] = packed[r0:r0 + nr].reshape(-1)[:n].reshape(shape)
        r0 += nr
    return out


def kernel(x, norm_mix, w_in, fox_f_bias, hg_lb_logits, hg_norm, w_branch_a, w_branch_b, w_out, norm_ffn, w_up, conv_w, conv_b, w_down, norm_final, loss_target, m_norm_mix, m_w_in, m_fox_f_bias, m_hg_lb_logits, m_hg_norm, m_w_branch_a, m_w_branch_b, m_w_out, m_norm_ffn, m_w_up, m_conv_w, m_conv_b, m_w_down, m_norm_final, v_norm_mix, v_w_in, v_fox_f_bias, v_hg_lb_logits, v_hg_norm, v_w_branch_a, v_w_branch_b, v_w_out, v_norm_ffn, v_w_up, v_conv_w, v_conv_b, v_w_down, v_norm_final):
    chip = 2 * lax.axis_index("x") + lax.axis_index("y")
    w_main, w_ff, wa, wb, wo, wu, wd, cw = _gather_weights(
        w_in[0], w_branch_a[0], w_branch_b[0], w_out[0], w_up[0], w_down[0], conv_w[0])
    dx, small, big = _local_step(
        x[0], loss_target[0], norm_mix, fox_f_bias, hg_lb_logits, hg_norm, norm_ffn, conv_b,
        norm_final.reshape(1, D), w_main, w_ff, wa, wb, wo, wu, wd, cw)

    packed, layout = _pack_small([(k, small[k]) for k in SMALL_KEYS + ("conv_w", "loss")])
    red = _unpack_small(_all_reduce_small("all_reduce_small", packed), layout)
    loss = red["loss"][0, 0]
    g_conv_w = lax.dynamic_slice_in_dim(red["conv_w"], chip * W_UP_SHARD, W_UP_SHARD, axis=1)

    gflat = _reduce_scatter_grads(*big)
    offs = [0]
    for r in SEG_ROWS:
        offs.append(offs[-1] + r)
    shapes = [(D, W_IN_SHARD), (256, D), (256, D), (256, D), (D, W_UP_SHARD), (W_DOWN_SHARD, D)]
    g_big = [gflat[offs[i]:offs[i + 1]].reshape(shapes[i]) for i in range(6)]

    names = ["norm_mix", "w_in", "fox_f_bias", "hg_lb_logits", "hg_norm", "w_branch_a", "w_branch_b", "w_out",
             "norm_ffn", "w_up", "conv_w", "conv_b", "w_down", "norm_final"]
    weights = dict(norm_mix=norm_mix, w_in=w_in, fox_f_bias=fox_f_bias, hg_lb_logits=hg_lb_logits, hg_norm=hg_norm,
                   w_branch_a=w_branch_a, w_branch_b=w_branch_b, w_out=w_out, norm_ffn=norm_ffn, w_up=w_up,
                   conv_w=conv_w, conv_b=conv_b, w_down=w_down, norm_final=norm_final)
    ms = dict(norm_mix=m_norm_mix, w_in=m_w_in, fox_f_bias=m_fox_f_bias, hg_lb_logits=m_hg_lb_logits,
              hg_norm=m_hg_norm, w_branch_a=m_w_branch_a, w_branch_b=m_w_branch_b, w_out=m_w_out,
              norm_ffn=m_norm_ffn, w_up=m_w_up, conv_w=m_conv_w, conv_b=m_conv_b, w_down=m_w_down,
              norm_final=m_norm_final)
    vs = dict(norm_mix=v_norm_mix, w_in=v_w_in, fox_f_bias=v_fox_f_bias, hg_lb_logits=v_hg_lb_logits,
              hg_norm=v_hg_norm, w_branch_a=v_w_branch_a, w_branch_b=v_w_branch_b, w_out=v_w_out,
              norm_ffn=v_norm_ffn, w_up=v_w_up, conv_w=v_conv_w, conv_b=v_conv_b, w_down=v_w_down,
              norm_final=v_norm_final)

    grads, deltas, new_m, new_v = {}, {}, {}, {}
    big_names = ["w_in", "w_branch_a", "w_branch_b", "w_out", "w_up", "w_down"]
    for name, g2 in zip(big_names, g_big):
        shape = weights[name].shape
        rows = g2.shape[0]
        d_, m_, v_ = _adamw("adamw_" + name, weights[name][0], g2, ms[name][0], vs[name][0], tm=rows // 8)
        grads[name], deltas[name], new_m[name], new_v[name] = (a.reshape(shape) for a in (g2, d_, m_, v_))
    shape = conv_w.shape
    d_, m_, v_ = _adamw("adamw_conv_w", conv_w[0], g_conv_w, m_conv_w[0], v_conv_w[0])
    grads["conv_w"], deltas["conv_w"], new_m["conv_w"], new_v["conv_w"] = (
        a.reshape(shape) for a in (g_conv_w, d_, m_, v_))
    gs = {k: red[k].reshape(weights[k].shape) for k in SMALL_KEYS}
    pw, lay = _pack_small([(k, weights[k]) for k in SMALL_KEYS])
    pg, _ = _pack_small([(k, gs[k]) for k in SMALL_KEYS])
    pm, _ = _pack_small([(k, ms[k]) for k in SMALL_KEYS])
    pv, _ = _pack_small([(k, vs[k]) for k in SMALL_KEYS])
    d_, m_, v_ = (_unpack_small(a, lay) for a in _adamw("adamw_small", pw, pg, pm, pv))
    for k in SMALL_KEYS:
        grads[k], deltas[k], new_m[k], new_v[k] = gs[k], d_[k], m_[k], v_[k]

    return (loss, dx[None], *[grads[n] for n in names], *[deltas[n] for n in names],
            *[new_m[n] for n in names], *[new_v[n] for n in names])
```

```python
import functools

import jax
import jax.numpy as jnp
from jax import lax
from jax.experimental import pallas as pl
from jax.experimental.pallas import tpu as pltpu

F32 = jnp.float32
BF16 = jnp.bfloat16

D = 1024
HG_H, HG_DK = 8, 128
FOX_H, FOX_D = 16, 64
D_FF = 2816
EPS = 1e-6
N_CHIP = 4
LANES = 128
W_IN_SHARD = 2308
W_UP_SHARD = 1408
W_DOWN_SHARD = 704
FF_COL = 7168
ADAM_LR, ADAM_B1, ADAM_B2, ADAM_EPS, ADAM_WD, ADAM_STEP = 0.001, 0.9, 0.999, 1e-08, 0.01, 10

HG_C = 16
HG_T = 256
HG_UNROLL = 8
HG_UNROLL_BWD = 4
FOX_T = 512
DW_TK = 2048
VMEM_LIMIT = 56 * 1024 * 1024
MESH = pl.DeviceIdType.MESH
ANY = pl.BlockSpec(memory_space=pl.ANY)


def _cparams(sem):
    return pltpu.CompilerParams(dimension_semantics=sem, vmem_limit_bytes=VMEM_LIMIT)


def _sigmoid(x):
    return 1.0 / (1.0 + jnp.exp(-x))


def _dot(a, b, dims):
    return lax.dot_general(a.astype(BF16), b.astype(BF16), (dims, ((), ())), preferred_element_type=F32)


NN = ((1,), (0,))
NT = ((1,), (1,))
TN = ((0,), (0,))


def _split_dot(tri, x, parts, dims=NN):
    acc = None
    r = x
    for _ in range(parts):
        p = r.astype(BF16)
        t = lax.dot_general(tri, p, (dims, ((), ())), preferred_element_type=F32)
        acc = t if acc is None else acc + t
        r = r - p.astype(F32)
    return acc


def _rb(arr, tm, width, cb=0):
    return (arr, (tm, width), lambda i: (i, cb))


def _cst(arr):
    return (arr, arr.shape, lambda i: (0,) * arr.ndim)


def _rows(name, fn, n_rows, tm, ins, outs, accs=(), reverse=False):
    n_in, n_out, n_acc = len(ins), len(outs), len(accs)
    nb = n_rows // tm

    def body(*refs):
        vals = [r[...] for r in refs[:n_in]]
        o, a = fn(*vals)
        for r, v in zip(refs[n_in:n_in + n_out], o):
            r[...] = v.astype(r.dtype)
        if n_acc:
            acc_refs = refs[n_in + n_out:]

            @pl.when(pl.program_id(0) == 0)
            def _():
                for r in acc_refs:
                    r[...] = jnp.zeros_like(r)

            for r, v in zip(acc_refs, a):
                r[...] += v

    if reverse:
        rowmap = lambda i: (nb - 1 - i, 0)
    else:
        rowmap = lambda i: (i, 0)
    in_specs = [pl.BlockSpec(bs, im) for (_, bs, im) in ins]
    out_specs = [pl.BlockSpec((tm, w), rowmap) for (w, _) in outs]
    out_specs += [pl.BlockSpec((r, w), lambda i: (0, 0)) for (r, w) in accs]
    out_shape = [jax.ShapeDtypeStruct((n_rows, w), dt) for (w, dt) in outs]
    out_shape += [jax.ShapeDtypeStruct((r, w), F32) for (r, w) in accs]
    res = pl.pallas_call(
        body, name=name, grid=(nb,), in_specs=in_specs, out_specs=out_specs, out_shape=out_shape,
        compiler_params=_cparams(("arbitrary",)),
    )(*[a for a, _, _ in ins])
    return (res[:n_out], res[n_out:]) if n_acc else res


def _mm(name, a, b, mode, out_dtype, tm, tn, tk, res=None):
    if mode == "nn":
        (m, k), n = a.shape, b.shape[1]
    elif mode == "nt":
        (m, k), n = a.shape, b.shape[0]
    else:
        (k, m), n = a.shape, b.shape[1]
    tm, tn, tk = min(tm, m), min(tn, n), min(tk, k)
    assert m % tm == 0 and n % tn == 0 and k % tk == 0, (name, m, n, k, tm, tn, tk)
    if mode == "nn":
        a_spec = pl.BlockSpec((tm, tk), lambda i, j, kk: (i, kk))
        b_spec = pl.BlockSpec((tk, tn), lambda i, j, kk: (kk, j))
        dims = NN
    elif mode == "nt":
        a_spec = pl.BlockSpec((tm, tk), lambda i, j, kk: (i, kk))
        b_spec = pl.BlockSpec((tn, tk), lambda i, j, kk: (j, kk))
        dims = NT
    else:
        a_spec = pl.BlockSpec((tk, tm), lambda i, j, kk: (kk, i))
        b_spec = pl.BlockSpec((tk, tn), lambda i, j, kk: (kk, j))
        dims = TN
    nk = k // tk
    has_res = res is not None
    acc_in_out = out_dtype == F32 and not has_res

    def body(*refs):
        a_ref, b_ref = refs[0], refs[1]
        r_ref = refs[2] if has_res else None
        o_ref = refs[3] if has_res else refs[2]
        part = _dot(a_ref[...], b_ref[...], dims)

        def finish(val):
            if has_res:
                val = val + r_ref[...]
            o_ref[...] = val.astype(o_ref.dtype)

        if nk == 1:
            finish(part)
        elif acc_in_out:
            kk = pl.program_id(2)

            @pl.when(kk == 0)
            def _():
                o_ref[...] = part

            @pl.when(kk > 0)
            def _():
                o_ref[...] += part
        else:
            acc_ref = refs[-1]
            kk = pl.program_id(2)

            @pl.when(kk == 0)
            def _():
                acc_ref[...] = part

            @pl.when(kk > 0)
            def _():
                acc_ref[...] += part

            @pl.when(kk == nk - 1)
            def _():
                finish(acc_ref[...])

    in_specs = [a_spec, b_spec]
    args = [a, b]
    if has_res:
        in_specs.append(pl.BlockSpec((tm, tn), lambda i, j, kk: (i, j)))
        args.append(res)
    return pl.pallas_call(
        body, name=name, grid=(m // tm, n // tn, nk), in_specs=in_specs,
        out_specs=pl.BlockSpec((tm, tn), lambda i, j, kk: (i, j)),
        out_shape=jax.ShapeDtypeStruct((m, n), out_dtype),
        scratch_shapes=[pltpu.VMEM((tm, tn), F32)] if nk > 1 and not acc_in_out else [],
        compiler_params=_cparams(("arbitrary", "arbitrary", "arbitrary")),
    )(*args)


def _mm_sum_nt(name, pieces, w, extra, tm, tn):
    n_p = len(pieces)
    m, k = pieces[0].shape
    n = w.shape[0]
    xa, xb = extra
    ke = xa.shape[1]
    tm, tn = min(tm, m), min(tn, n)

    def body(*refs):
        p_refs, w_ref, xa_ref, xb_ref, o_ref = refs[:n_p], refs[n_p], refs[n_p + 1], refs[n_p + 2], refs[-1]
        kk = pl.program_id(2)

        @pl.when(kk == 0)
        def _():
            o_ref[...] = _dot(p_refs[0][...], w_ref[...], NT)

        for i in range(1, n_p):
            @pl.when(kk == i)
            def _(i=i):
                o_ref[...] += _dot(p_refs[i][...], w_ref[...], NT)

        @pl.when(kk == n_p)
        def _():
            o_ref[...] += _dot(xa_ref[...], xb_ref[...], NT)

    in_specs = [pl.BlockSpec((tm, k), lambda i, j, kk: (i, 0)) for _ in range(n_p)]
    in_specs.append(pl.BlockSpec((tn, k), lambda i, j, kk: (j, jnp.minimum(kk, n_p - 1))))
    in_specs += [pl.BlockSpec((tm, ke), lambda i, j, kk: (i, 0)), pl.BlockSpec((tn, ke), lambda i, j, kk: (j, 0))]
    return pl.pallas_call(
        body, name=name, grid=(m // tm, n // tn, n_p + 1), in_specs=in_specs,
        out_specs=pl.BlockSpec((tm, tn), lambda i, j, kk: (i, j)),
        out_shape=jax.ShapeDtypeStruct((m, n), F32),
        compiler_params=_cparams(("arbitrary", "arbitrary", "arbitrary")),
    )(*pieces, w, xa, xb)


def _rms_fwd(name, x, gain, tm=256):
    s = x.shape[0]

    def body(x_ref, g_ref, y_ref, yt_ref):
        xb = x_ref[...]
        y = xb * lax.rsqrt(jnp.mean(xb * xb, axis=-1, keepdims=True) + EPS) * g_ref[...]
        y_ref[...] = y.astype(BF16)
        yt_ref[...] = y.T.astype(BF16)

    return pl.pallas_call(
        body, name=name, grid=(s // tm,),
        in_specs=[pl.BlockSpec((tm, D), lambda i: (i, 0)), pl.BlockSpec((1, D), lambda i: (0, 0))],
        out_specs=[pl.BlockSpec((tm, D), lambda i: (i, 0)), pl.BlockSpec((D, tm), lambda i: (0, i))],
        out_shape=[jax.ShapeDtypeStruct((s, D), BF16), jax.ShapeDtypeStruct((D, s), BF16)],
        compiler_params=_cparams(("arbitrary",)),
    )(x, gain)


def _rms_bwd(name, x, gain, dns, dres, tm=256):
    s = x.shape[0]
    n_dn = len(dns)

    def fn(xb, g, *rest):
        dn = rest[0]
        for t in rest[1:n_dn]:
            dn = dn + t
        r = lax.rsqrt(jnp.mean(xb * xb, axis=-1, keepdims=True) + EPS)
        xhat = xb * r
        dxh = dn * g
        dx = r * (dxh - xhat * jnp.mean(dxh * xhat, axis=-1, keepdims=True)) + rest[n_dn]
        return (dx,), (jnp.sum(dn * xhat, axis=0, keepdims=True),)

    ins = [_rb(x, tm, D), _cst(gain)] + [_rb(t, tm, D) for t in dns] + [_rb(dres, tm, D)]
    return _rows(name, fn, s, tm, ins, [(D, F32)], [(1, D)])


def _final(h2, target, gain, tm=256):
    s = h2.shape[0]

    def fn(hb, tb, g):
        r = lax.rsqrt(jnp.mean(hb * hb, axis=-1, keepdims=True) + EPS)
        xhat = hb * r
        e = xhat * g - tb
        dy = e * (1.0 / D)
        dxh = dy * g
        dh = r * (dxh - xhat * jnp.mean(dxh * xhat, axis=-1, keepdims=True))
        lrow = 0.5 * jnp.sum(jnp.sum(e * e, axis=-1, keepdims=True) * (1.0 / D), axis=0, keepdims=True)
        return (dh,), (jnp.sum(dy * xhat, axis=0, keepdims=True), jnp.broadcast_to(lrow, (1, LANES)))

    return _rows("final_norm_loss", fn, s, tm, [_rb(h2, tm, D), _rb(target, tm, D), _cst(gain)],
                 [(D, F32)], [(1, D), (1, LANES)])


def _merge_fwd(pa, pb, proj, tm=256):
    s = pa.shape[0]

    def fn(a, b, ga, gb):
        return (_sigmoid(ga) * a + _sigmoid(gb) * b,), ()

    ins = [_rb(pa, tm, D), _rb(pb, tm, D), _rb(proj, tm, D, 7), _rb(proj, tm, D, 8)]
    return _rows("merge_fwd", fn, s, tm, ins, [(D, BF16)])[0]


def _merge_bwd(dmerged, pa, pb, proj, tm=256):
    s = pa.shape[0]

    def fn(dm, a, b, ga, gb):
        sa, sb = _sigmoid(ga), _sigmoid(gb)
        return (dm * sa, dm * sb, dm * a * sa * (1.0 - sa), dm * b * sb * (1.0 - sb)), ()

    ins = [_rb(dmerged, tm, D), _rb(pa, tm, D), _rb(pb, tm, D), _rb(proj, tm, D, 7), _rb(proj, tm, D, 8)]
    return _rows("merge_bwd", fn, s, tm, ins, [(D, BF16), (D, BF16), (D, BF16), (D, BF16)])


def _gelu_parts(x):
    cdf = 0.5 * (1.0 + lax.erf(x * 0.7071067811865476))
    pdf = 0.3989422804014327 * jnp.exp(-0.5 * x * x)
    return x * cdf, cdf + x * pdf


def _conv_taps(u_ext, n_out, first):
    n = u_ext.shape[0]
    cur = u_ext[8:8 + n_out]
    m1 = pltpu.roll(u_ext, 1, 0)[8:8 + n_out]
    m2 = pltpu.roll(u_ext, 2, 0)[8:8 + n_out]
    return m2, m1, cur


def _convglu_fwd(u, conv_w8, conv_b, tm=64):
    s, w = u.shape
    tb = tm // 8

    def fn(ub, up, cw, cb):
        i = pl.program_id(0)
        up = jnp.where(i == 0, 0.0, up)
        m2, m1, cur = _conv_taps(jnp.concatenate([up, ub], axis=0), tm, None)
        acc = cb + cw[0:1] * m2 + cw[1:2] * m1 + cw[2:3] * cur
        act, _ = _gelu_parts(acc[:, :D_FF])
        return (act * acc[:, D_FF:],), ()

    ins = [_rb(u, tm, w), (u, (8, w), lambda i: (jnp.maximum(i * tb - 1, 0), 0)), _cst(conv_w8), _cst(conv_b)]
    return _rows("convglu_fwd", fn, s, tm, ins, [(D_FF, BF16)])[0]


def _convglu_bwd(u, dact, conv_w8, conv_b, tm=64):
    s, w = u.shape
    tb = tm // 8
    nb = s // tm

    def fn(ub, up, un, db, dn, cw, cb):
        i = pl.program_id(0)
        up = jnp.where(i == 0, 0.0, up)
        dn = jnp.where(i == nb - 1, 0.0, dn)
        ne = tm + 8
        m2, m1, cur = _conv_taps(jnp.concatenate([up, ub, un], axis=0), ne, None)
        acc = cb + cw[0:1] * m2 + cw[1:2] * m1 + cw[2:3] * cur
        de = jnp.concatenate([db, dn], axis=0)
        gl, dgl = _gelu_parts(acc[:, :D_FF])
        dacc = jnp.concatenate([de * acc[:, D_FF:] * dgl, de * gl], axis=1)
        p1 = pltpu.roll(dacc, ne - 1, 0)[:tm]
        p2 = pltpu.roll(dacc, ne - 2, 0)[:tm]
        d0 = dacc[:tm]
        du = cw[2:3] * d0 + cw[1:2] * p1 + cw[0:1] * p2
        zero5 = jnp.zeros((5, w), F32)
        dcw = jnp.concatenate([
            jnp.sum(d0 * m2[:tm], axis=0, keepdims=True), jnp.sum(d0 * m1[:tm], axis=0, keepdims=True),
            jnp.sum(d0 * cur[:tm], axis=0, keepdims=True), zero5], axis=0)
        return (du,), (dcw, jnp.sum(d0, axis=0, keepdims=True))

    ins = [
        _rb(u, tm, w),
        (u, (8, w), lambda i: (jnp.maximum(i * tb - 1, 0), 0)),
        (u, (8, w), lambda i: (jnp.minimum((i + 1) * tb, s // 8 - 1), 0)),
        _rb(dact, tm, D_FF),
        (dact, (8, D_FF), lambda i: (jnp.minimum((i + 1) * tb, s // 8 - 1), 0)),
        _cst(conv_w8), _cst(conv_b),
    ]
    return _rows("convglu_bwd", fn, s, tm, ins, [(w, BF16)], [(8, w), (1, w)])


def _chunk_scan(x, t_iota, reverse):
    k = 1
    while k < HG_C:
        if reverse:
            x = x + jnp.where(t_iota < HG_C - k, pltpu.roll(x, HG_C - k, 0), 0.0)
        else:
            x = x + jnp.where(t_iota >= k, pltpu.roll(x, k, 0), 0.0)
        k *= 2
    return x


def _hg_gates(hq, hf, lb):
    sq = _sigmoid(hq)
    q = hq * sq
    sg = _sigmoid(hf)
    f = lb + (1.0 - lb) * sg
    return q, sq, sg, f, 1.0 - f, jnp.log(f)


def _lb_of(logits):
    l0, l1 = logits[0:1], logits[1:2]
    mx = jnp.maximum(l0, l1)
    e0, e1 = jnp.exp(l0 - mx), jnp.exp(l1 - mx)
    return e0 / (e0 + e1)


def _tri(n, lower):
    r = lax.broadcasted_iota(jnp.int32, (n, n), 0)
    c = lax.broadcasted_iota(jnp.int32, (n, n), 1)
    return jnp.where((r >= c) if lower else (r <= c), 1.0, 0.0).astype(BF16)


def _hg_intra_terms(q, kk, b, t_iota):
    ws, ps = [], []
    for s in range(HG_C):
        p = jnp.where(t_iota >= s, jnp.exp(b - b[s:s + 1]), 0.0)
        ps.append(p)
        ws.append(q * kk[s:s + 1] * p)
    return jnp.concatenate(ws, axis=0), ps


def _hg_fwd(proj, lb_logits):
    s = proj.shape[0]
    nt = s // HG_T
    nc = HG_T // HG_C

    def body(q_ref, f_ref, i_ref, l_ref, o_ref, st_ref, state):
        @pl.when(pl.program_id(1) == 0)
        def _():
            state[...] = jnp.zeros_like(state)

        st_ref[0, 0] = state[...]
        lb = _lb_of(l_ref[...])
        ones = jnp.ones((HG_DK, HG_DK), BF16)
        t_iota = lax.broadcasted_iota(jnp.int32, (HG_C, HG_DK), 0)
        cc = HG_C * HG_C

        def group(gi, st):
            units = []
            for u in range(HG_UNROLL):
                r = pl.ds(pl.multiple_of((gi * HG_UNROLL + u) * HG_C, HG_C), HG_C)
                q, _, _, _, kk, g = _hg_gates(q_ref[r, :], f_ref[r, :], lb)
                b = _chunk_scan(g, t_iota, False)
                b_end = b[HG_C - 1:HG_C]
                w_all, _ = _hg_intra_terms(q, kk, b, t_iota)
                units.append((r, i_ref[r, :], q * jnp.exp(b), jnp.exp(b_end), kk * jnp.exp(b_end - b), w_all))
            a_all = _dot(jnp.concatenate([un[5] for un in units], axis=0), ones, NN)
            kvs = [_dot(v, kd, TN) for (_, v, _, _, kd, _) in units]
            sts = [st]
            for (_, _, _, dec, _, _), kv in zip(units, kvs):
                sts.append(sts[-1] * dec + kv)
            for ui, (r, v, qd, _, _, _) in enumerate(units):
                o = _dot(qd, sts[ui], NT)
                for si in range(HG_C):
                    o = o + a_all[ui * cc + si * HG_C:ui * cc + (si + 1) * HG_C] * v[si:si + 1]
                o_ref[r, :] = o
            return sts[-1]

        state[...] = lax.fori_loop(0, nc // HG_UNROLL, group, state[...])

    col = lambda off: pl.BlockSpec((HG_T, HG_DK), lambda h, t: (t, off + h))
    return pl.pallas_call(
        body, name="hgrn2_fwd", grid=(HG_H, nt),
        in_specs=[col(0), col(8), col(16), pl.BlockSpec((2, HG_DK), lambda h, t: (0, h))],
        out_specs=[pl.BlockSpec((HG_T, HG_DK), lambda h, t: (t, h)),
                   pl.BlockSpec((1, 1, HG_DK, HG_DK), lambda h, t: (h, t, 0, 0))],
        out_shape=[jax.ShapeDtypeStruct((s, D), F32), jax.ShapeDtypeStruct((HG_H, nt, HG_DK, HG_DK), F32)],
        scratch_shapes=[pltpu.VMEM((HG_DK, HG_DK), F32)],
        compiler_params=_cparams(("arbitrary", "arbitrary")),
    )(proj, proj, proj, lb_logits)


def _hg_bwd(proj, lb_logits, states, do_raw):
    s = proj.shape[0]
    nt = s // HG_T
    nc = HG_T // HG_C

    def body(q_ref, f_ref, i_ref, l_ref, st_ref, do_ref, dq_ref, df_ref, di_ref, dl_ref, st_all, adj):
        tb = pl.program_id(1)

        @pl.when(tb == 0)
        def _():
            adj[...] = jnp.zeros_like(adj)
            dl_ref[...] = jnp.zeros_like(dl_ref)

        lb = _lb_of(l_ref[...])
        ones = jnp.ones((HG_DK, HG_DK), BF16)
        t_iota = lax.broadcasted_iota(jnp.int32, (HG_C, HG_DK), 0)
        cc = HG_C * HG_C

        def fwd_group(gi, st):
            terms = []
            for u in range(HG_UNROLL):
                ci = gi * HG_UNROLL + u
                r = pl.ds(pl.multiple_of(ci * HG_C, HG_C), HG_C)
                _, _, _, _, kk, g = _hg_gates(q_ref[r, :], f_ref[r, :], lb)
                b = _chunk_scan(g, t_iota, False)
                b_end = b[HG_C - 1:HG_C]
                terms.append((ci, jnp.exp(b_end), _dot(i_ref[r, :], kk * jnp.exp(b_end - b), TN)))
            for ci, dec, kv in terms:
                st_all[ci] = st
                st = st * dec + kv
            return st

        lax.fori_loop(0, nc // HG_UNROLL, fwd_group, st_ref[0, 0])

        def bwd_group(gj, dlb):
            units = []
            for u in range(HG_UNROLL_BWD):
                ci = nc - 1 - (gj * HG_UNROLL_BWD + u)
                r = pl.ds(pl.multiple_of(ci * HG_C, HG_C), HG_C)
                hq, hf, v, do = q_ref[r, :], f_ref[r, :], i_ref[r, :], do_ref[r, :]
                q, sq, sg, f, kk, g = _hg_gates(hq, hf, lb)
                b = _chunk_scan(g, t_iota, False)
                b_end = b[HG_C - 1:HG_C]
                e_b, e_be, dec = jnp.exp(b), jnp.exp(b_end - b), jnp.exp(b_end)
                w_all, ps = _hg_intra_terms(q, kk, b, t_iota)
                x_all = jnp.concatenate([do * v[si:si + 1] for si in range(HG_C)], axis=0)
                units.append(dict(ci=ci, r=r, hq=hq, v=v, do=do, q=q, sq=sq, sg=sg, f=f, kk=kk, e_b=e_b, e_be=e_be,
                                  dec=dec, kd=kk * e_be, w=w_all, ps=ps, x=x_all))
            both = _dot(jnp.concatenate([un["w"] for un in units] + [un["x"] for un in units], axis=0), ones, NN)
            st0s = [st_all[un["ci"]] for un in units]
            st_ends = [st0 * un["dec"] + _dot(un["v"], un["kd"], TN) for un, st0 in zip(units, st0s)]
            dqks = [_dot(un["do"], un["q"] * un["e_b"], TN) for un in units]
            es = [adj[...]]
            for un, dqk in zip(units, dqks):
                es.append(es[-1] * un["dec"] + dqk)
            adj[...] = es[-1]
            for ui, un in enumerate(units):
                e, q, kk, v, do = es[ui], un["q"], un["kk"], un["v"], un["do"]
                tail = jnp.sum(e * st_ends[ui], axis=0, keepdims=True)
                dq = un["e_b"] * _dot(do, st0s[ui], NN)
                dk = un["e_be"] * _dot(v, e, NN)
                dv = _dot(un["kd"], e, NT)
                a0 = ui * cc
                d0 = (HG_UNROLL_BWD + ui) * cc
                for si in range(HG_C):
                    da = both[d0 + si * HG_C:d0 + (si + 1) * HG_C]
                    aa = both[a0 + si * HG_C:a0 + (si + 1) * HG_C]
                    dap = da * un["ps"][si]
                    dq = dq + dap * kk[si:si + 1]
                    hit = t_iota == si
                    dk = dk + jnp.where(hit, jnp.sum(dap * q, axis=0, keepdims=True), 0.0)
                    dv = dv + jnp.where(hit, jnp.sum(aa * do, axis=0, keepdims=True), 0.0)
                dg = _chunk_scan(q * dq - kk * dk, t_iota, True) + tail
                dfg = dg / un["f"] - dk
                sq, sg, hq, r = un["sq"], un["sg"], un["hq"], un["r"]
                dq_ref[r, :] = (dq * sq * (1.0 + hq * (1.0 - sq))).astype(dq_ref.dtype)
                df_ref[r, :] = (dfg * (1.0 - lb) * sg * (1.0 - sg)).astype(df_ref.dtype)
                di_ref[r, :] = dv.astype(di_ref.dtype)
                dlb = dlb + jnp.sum(dfg * (1.0 - sg), axis=0, keepdims=True)
            return dlb

        dlb = lax.fori_loop(0, nc // HG_UNROLL_BWD, bwd_group, jnp.zeros((1, HG_DK), F32))
        dl0 = dlb * lb * (1.0 - lb)
        dl_ref[...] += jnp.concatenate([dl0, -dl0], axis=0)

    col = lambda off: pl.BlockSpec((HG_T, HG_DK), lambda h, t: (nt - 1 - t, off + h))
    out_col = pl.BlockSpec((HG_T, HG_DK), lambda h, t: (nt - 1 - t, h))
    return pl.pallas_call(
        body, name="hgrn2_bwd", grid=(HG_H, nt),
        in_specs=[col(0), col(8), col(16), pl.BlockSpec((2, HG_DK), lambda h, t: (0, h)),
                  pl.BlockSpec((1, 1, HG_DK, HG_DK), lambda h, t: (h, nt - 1 - t, 0, 0)), col(0)],
        out_specs=[out_col, out_col, out_col, pl.BlockSpec((2, HG_DK), lambda h, t: (0, h))],
        out_shape=[jax.ShapeDtypeStruct((s, D), BF16)] * 3 + [jax.ShapeDtypeStruct((2, D), F32)],
        scratch_shapes=[pltpu.VMEM((nc, HG_DK, HG_DK), F32), pltpu.VMEM((HG_DK, HG_DK), F32)],
        compiler_params=_cparams(("arbitrary", "arbitrary")),
    )(proj, proj, proj, lb_logits, states, do_raw)


def _hg_post_fwd(o_raw, proj, gnorm, tm=256):
    s = o_raw.shape[0]

    def fn(o, hg, gn):
        outs = []
        for h in range(HG_H):
            sl = slice(h * HG_DK, (h + 1) * HG_DK)
            oh, gh = o[:, sl], hg[:, sl]
            r = lax.rsqrt(jnp.mean(oh * oh, axis=-1, keepdims=True) + EPS)
            outs.append(oh * r * gn * (gh * _sigmoid(gh)))
        return (jnp.concatenate(outs, axis=1),), ()

    return _rows("hgrn2_out_fwd", fn, s, tm, [_rb(o_raw, tm, D), _rb(proj, tm, D, 3), _cst(gnorm)], [(D, BF16)])[0]


def _hg_post_bwd(do_a, o_raw, proj, gnorm, tm=256):
    s = o_raw.shape[0]

    def fn(da, o, hg, gn):
        dos, dhgs = [], []
        dgn = jnp.zeros((1, HG_DK), F32)
        for h in range(HG_H):
            sl = slice(h * HG_DK, (h + 1) * HG_DK)
            oh, gh, dh = o[:, sl], hg[:, sl], da[:, sl]
            r = lax.rsqrt(jnp.mean(oh * oh, axis=-1, keepdims=True) + EPS)
            xhat = oh * r
            sg = _sigmoid(gh)
            dy = dh * (gh * sg)
            dhgs.append(dh * xhat * gn * sg * (1.0 + gh * (1.0 - sg)))
            dgn = dgn + jnp.sum(dy * xhat, axis=0, keepdims=True)
            dxh = dy * gn
            dos.append(r * (dxh - xhat * jnp.mean(dxh * xhat, axis=-1, keepdims=True)))
        return (jnp.concatenate(dos, axis=1), jnp.concatenate(dhgs, axis=1)), (dgn,)

    ins = [_rb(do_a, tm, D), _rb(o_raw, tm, D), _rb(proj, tm, D, 3), _cst(gnorm)]
    return _rows("hgrn2_out_bwd", fn, s, tm, ins, [(D, F32), (D, BF16)], [(1, HG_DK)])


def _log_sigmoid(z):
    return jnp.minimum(z, 0.0) - jnp.log(1.0 + jnp.exp(-jnp.abs(z)))


def _fox_gate_bwd(dct, pff, bias, tm=256):
    s = pff.shape[0]
    nb = s // tm

    def body(d_ref, p_ref, b_ref, dff_ref, db_ref, carry):
        @pl.when(pl.program_id(0) == 0)
        def _():
            carry[...] = jnp.zeros_like(carry)
            db_ref[...] = jnp.zeros_like(db_ref)

        dc = d_ref[...].T
        dlf = _split_dot(_tri(tm, False), dc, 3) + carry[0:1]
        carry[...] = jnp.broadcast_to(dlf[0:1], carry.shape)
        dff = dlf * _sigmoid(-(p_ref[...] + b_ref[...]))
        dff_ref[...] = dff
        db_ref[...] += jnp.sum(dff, axis=0, keepdims=True)

    return pl.pallas_call(
        body, name="fox_gate_bwd", grid=(nb,),
        in_specs=[pl.BlockSpec((LANES, tm), lambda i: (0, nb - 1 - i)),
                  pl.BlockSpec((tm, LANES), lambda i: (nb - 1 - i, 0)), pl.BlockSpec((1, LANES), lambda i: (0, 0))],
        out_specs=[pl.BlockSpec((tm, LANES), lambda i: (nb - 1 - i, 0)), pl.BlockSpec((1, LANES), lambda i: (0, 0))],
        out_shape=[jax.ShapeDtypeStruct((s, LANES), F32), jax.ShapeDtypeStruct((1, LANES), F32)],
        scratch_shapes=[pltpu.VMEM((8, LANES), F32)],
        compiler_params=_cparams(("arbitrary",)),
    )(dct, pff, bias)


def _diag_mask(t):
    r = lax.broadcasted_iota(jnp.int32, (t, t), 0)
    c = lax.broadcasted_iota(jnp.int32, (t, t), 1)
    return r >= c


AUX_ONES = 6


def _pieces(x):
    h = x.astype(BF16)
    r = x - h.astype(F32)
    m = r.astype(BF16)
    return h, m, (r - m.astype(F32)).astype(BF16)


def _lane_put(lane, cols, base):
    out = None
    for i, col in enumerate(cols):
        term = jnp.where(lane == base + i, col.astype(F32), 0.0)
        out = term if out is None else out + term
    return out


def _fox_prep2(proj, pff, bias, tm=256):
    s = pff.shape[0]

    def body(q_ref, k_ref, v_ref, p_ref, b_ref, qb_ref, kb_ref, vb_ref, ka_ref, carry):
        @pl.when(pl.program_id(0) == 0)
        def _():
            carry[...] = jnp.zeros_like(carry)

        qb_ref[...] = (q_ref[...] * 0.125).astype(BF16)
        kb_ref[...] = k_ref[...].astype(BF16)
        vb_ref[...] = v_ref[...].astype(BF16)
        lf = _log_sigmoid(p_ref[...] + b_ref[...])
        c = _split_dot(_tri(tm, True), lf, 3) + carry[0:1]
        carry[...] = jnp.broadcast_to(c[tm - 1:tm], carry.shape)
        lane = lax.broadcasted_iota(jnp.int32, (tm, LANES), 1)
        ones = jnp.where((lane >= AUX_ONES) & (lane < AUX_ONES + 6), 1.0, 0.0)
        for p in range(FOX_H // 2):
            aux = ones
            for z in range(2):
                col = jnp.sum(jnp.where(lane == 2 * p + z, c, 0.0), axis=1, keepdims=True)
                aux = aux + _lane_put(lane, _pieces(-col), 3 * z)
            ka_ref[:, p * LANES:(p + 1) * LANES] = aux.astype(BF16)

    row = lambda cb: pl.BlockSpec((tm, D), lambda i: (i, cb))
    return pl.pallas_call(
        body, name="fox_prep", grid=(s // tm,),
        in_specs=[row(4), row(5), row(6), pl.BlockSpec((tm, LANES), lambda i: (i, 0)),
                  pl.BlockSpec((1, LANES), lambda i: (0, 0))],
        out_specs=[row(0)] * 4, out_shape=[jax.ShapeDtypeStruct((s, D), BF16)] * 4,
        scratch_shapes=[pltpu.VMEM((8, LANES), F32)],
        compiler_params=_cparams(("arbitrary",)),
    )(proj, proj, proj, pff, bias)


def _fox_fwd2(qb, kb, vb, ka):
    s = qb.shape[0]
    t = min(FOX_T, s)
    nq = s // t

    def body(q_ref, k_ref, v_ref, ka_ref, o_ref, la_ref):
        i = pl.program_id(1)
        lane = lax.broadcasted_iota(jnp.int32, (t, LANES), 1)
        in_a = lane < FOX_D
        q = q_ref[...]
        zero = jnp.zeros_like(q)
        qh = [jnp.where(in_a, q, zero), jnp.where(in_a, zero, q)]
        c_ones = [jnp.where((lane >= 3 * z) & (lane < 3 * z + 3), 1.0, 0.0) for z in range(2)]

        def keys(j):
            rows = pl.ds(pl.multiple_of(j * t, t), t)
            return jnp.concatenate([k_ref[rows, :], ka_ref[rows, :]], axis=1), rows

        dmask = _diag_mask(t)

        def logits(qx, kk, masked):
            e = lax.dot_general(qx, kk, (NT, ((), ())), preferred_element_type=F32)
            return jnp.where(dmask, e, -1e30) if masked else e

        qc = [jnp.concatenate([qh[z], c_ones[z].astype(BF16)], axis=1) for z in range(2)]

        def step(j, carry, masked):
            kk, rows = keys(j)
            vj = v_ref[rows, :]
            scores = [logits(qc[z], kk, masked) for z in range(2)]
            one = jnp.ones_like(vj)
            vh = [jnp.where(in_a, vj, one), jnp.where(in_a, one, vj)]
            out = []
            for z in range(2):
                m, acc = carry[z]
                m_new = jnp.maximum(m, jnp.max(scores[z], axis=1, keepdims=True))
                p = jnp.exp(scores[z] - m_new)
                out.append((m_new, jnp.exp(m - m_new) * acc + _dot(p, vh[z], NN)))
            return tuple(out)

        init = tuple((jnp.full((t, 1), -1e30, F32), jnp.zeros((t, LANES), F32)) for _ in range(2))
        (ma, acc_a), (mb, acc_b) = step(i, lax.fori_loop(0, i, lambda j, c: step(j, c, False), init), True)
        la = jnp.sum(jnp.where(lane == FOX_D, acc_a, 0.0), axis=1, keepdims=True)
        lb = jnp.sum(jnp.where(lane == 0, acc_b, 0.0), axis=1, keepdims=True)
        o_ref[...] = jnp.where(in_a, acc_a / la, acc_b / lb).astype(o_ref.dtype)
        la_ref[...] = (_lane_put(lane, _pieces(-(ma + jnp.log(la))), AUX_ONES)
                       + _lane_put(lane, _pieces(-(mb + jnp.log(lb))), AUX_ONES + 3)).astype(la_ref.dtype)

    blk = pl.BlockSpec((t, LANES), lambda p, i: (i, p))
    whole = pl.BlockSpec((s, LANES), lambda p, i: (0, p))
    return pl.pallas_call(
        body, name="fox_attn_fwd", grid=(FOX_H // 2, nq), in_specs=[blk, whole, whole, whole],
        out_specs=[blk, blk], out_shape=[jax.ShapeDtypeStruct((s, D), BF16)] * 2,
        compiler_params=_cparams(("arbitrary", "arbitrary")),
    )(qb, kb, vb, ka)


def _fox_bwd2(qb, kb, vb, ka, ob, laux, dob):
    s = qb.shape[0]
    t = min(FOX_T, s)
    nq = s // t

    def body(q_ref, k_ref, v_ref, ka_ref, o_ref, la_ref, do_ref, dq_ref, dk_ref, dv_ref, dc_ref, dkt, dvt):
        i = pl.program_id(1)

        @pl.when(i == 0)
        def _():
            dkt[...] = jnp.zeros_like(dkt)
            dvt[...] = jnp.zeros_like(dvt)
            dc_ref[...] = jnp.zeros_like(dc_ref)

        lane = lax.broadcasted_iota(jnp.int32, (t, LANES), 1)
        in_a = lane < FOX_D
        q, do, la = q_ref[...], do_ref[...], la_ref[...].astype(F32)
        zero = jnp.zeros_like(q)
        qh = [jnp.where(in_a, q, zero), jnp.where(in_a, zero, q)]
        doh = [jnp.where(in_a, do, zero), jnp.where(in_a, zero, do)]
        qt = [h.astype(F32).T.astype(BF16) for h in qh]
        dot_ = [h.astype(F32).T.astype(BF16) for h in doh]
        prod = do.astype(F32) * o_ref[...].astype(F32)
        qx, dox = [], []
        for z in range(2):
            delta = jnp.sum(jnp.where(in_a if z == 0 else ~in_a, prod, 0.0), axis=1, keepdims=True)
            c_ones = jnp.where((lane >= 3 * z) & (lane < 3 * z + 3), 1.0, 0.0)
            lse_lanes = (lane >= AUX_ONES + 3 * z) & (lane < AUX_ONES + 3 * z + 3)
            qx.append(jnp.concatenate([qh[z], (c_ones + jnp.where(lse_lanes, la, 0.0)).astype(BF16)], axis=1))
            dox.append(jnp.concatenate([doh[z], _lane_put(lane, _pieces(-delta), 3 * z).astype(BF16)], axis=1))
        v_ones = jnp.where(lane < 6, 1.0, 0.0).astype(BF16)
        dmask = _diag_mask(t)

        def step(j, carry, masked):
            rows = pl.ds(pl.multiple_of(j * t, t), t)
            kj, vj = k_ref[rows, :], v_ref[rows, :]
            kk = jnp.concatenate([kj, ka_ref[rows, :]], axis=1)
            vv = jnp.concatenate([vj, v_ones], axis=1)
            out = []
            dk_add, dv_add = None, None
            for z in range(2):
                dq, rsum = carry[z]
                e = lax.dot_general(qx[z], kk, (NT, ((), ())), preferred_element_type=F32)
                if masked:
                    e = jnp.where(dmask, e, -1e30)
                p = jnp.exp(e)
                ds = p * lax.dot_general(dox[z], vv, (NT, ((), ())), preferred_element_type=F32)
                dkz, dvz = _dot(qt[z], ds, NN), _dot(dot_[z], p, NN)
                dk_add = dkz if dk_add is None else dk_add + dkz
                dv_add = dvz if dv_add is None else dv_add + dvz
                dc_ref[0, z, j] += -jnp.sum(ds, axis=0, keepdims=True)
                out.append((dq + _dot(ds, kj, NN), rsum + jnp.sum(ds, axis=1, keepdims=True)))
            dkt[j] += dk_add
            dvt[j] += dv_add
            return tuple(out)

        init = tuple((jnp.zeros((t, LANES), F32), jnp.zeros((t, 1), F32)) for _ in range(2))
        (dq_a, rs_a), (dq_b, rs_b) = step(i, lax.fori_loop(0, i, lambda j, c: step(j, c, False), init), True)
        for z, rs in enumerate((rs_a, rs_b)):
            dc_ref[0, z, i] += jnp.transpose(jnp.broadcast_to(rs, (t, LANES)))[0:1]
        dq_ref[...] = (jnp.where(in_a, dq_a, dq_b) * 0.125).astype(dq_ref.dtype)

        @pl.when(i == nq - 1)
        def _():
            for jb in range(nq):
                dk_ref[jb * t:(jb + 1) * t, :] = dkt[jb].T
                dv_ref[jb * t:(jb + 1) * t, :] = dvt[jb].T

    blk = pl.BlockSpec((t, LANES), lambda p, i: (i, p))
    whole = pl.BlockSpec((s, LANES), lambda p, i: (0, p))
    return pl.pallas_call(
        body, name="fox_attn_bwd", grid=(FOX_H // 2, nq),
        in_specs=[blk, whole, whole, whole, blk, blk, blk],
        out_specs=[blk, whole, whole, pl.BlockSpec((1, 2, nq, 1, t), lambda p, i: (p, 0, 0, 0, 0))],
        out_shape=[jax.ShapeDtypeStruct((s, D), BF16), jax.ShapeDtypeStruct((s, D), F32),
                   jax.ShapeDtypeStruct((s, D), F32), jax.ShapeDtypeStruct((FOX_H // 2, 2, nq, 1, t), F32)],
        scratch_shapes=[pltpu.VMEM((nq, LANES, t), F32), pltpu.VMEM((nq, LANES, t), F32)],
        compiler_params=_cparams(("arbitrary", "arbitrary")),
    )(qb, kb, vb, ka, ob, laux, dob)


def _adamw(name, w, g, m, v, tm=None):
    rows, width = w.shape
    tm = rows if tm is None else tm
    c1 = 1.0 - ADAM_B1 ** ADAM_STEP
    c2 = 1.0 - ADAM_B2 ** ADAM_STEP

    def fn(wb, gb, mb, vb):
        m_new = ADAM_B1 * mb + (1.0 - ADAM_B1) * gb
        v_new = ADAM_B2 * vb + (1.0 - ADAM_B2) * (gb * gb)
        delta = -ADAM_LR * ((m_new / c1) / (jnp.sqrt(v_new / c2) + ADAM_EPS) + ADAM_WD * wb)
        return (delta, m_new, v_new), ()

    ins = [_rb(a, tm, width) for a in (w, g, m, v)]
    return _rows(name, fn, rows, tm, ins, [(width, F32)] * 3)


def _me():
    return lax.axis_index("x"), lax.axis_index("y"), lax.axis_index("c")


def _all_gather8(name, block):
    m, n = block.shape

    def body(x_ref, out_ref, send_sems, recv_sems):
        x, y, c = _me()
        me, sibling = (x, y, c), (x, y, 1 - c)
        chips = [(1 - x, y), (x, 1 - y), (1 - x, 1 - y)]

        def slot(px, py, pc):
            return out_ref.at[4 * px + 2 * py + pc]

        def copy(k, blk, to, src=None):
            return pltpu.make_async_remote_copy(
                src_ref=slot(*blk) if src is None else src, dst_ref=slot(*blk),
                send_sem=send_sems.at[k], recv_sem=recv_sems.at[k], device_id=to, device_id_type=MESH)

        first = [copy(0, me, sibling, src=x_ref)]
        first += [copy(1 + j, me, (*chip, c), src=x_ref) for j, chip in enumerate(chips)]
        for cp in first:
            cp.start()
        passed = [copy(4 + j, (*chip, c), sibling) for j, chip in enumerate(chips)]
        for j, chip in enumerate(chips):
            copy(1 + j, (*chip, c), me).wait_recv()
            passed[j].start()
        copy(0, sibling, me).wait_recv()
        for j, chip in enumerate(chips):
            copy(4 + j, (*chip, 1 - c), me).wait_recv()
        for cp in first + passed:
            cp.wait_send()

    gathered = pl.pallas_call(
        body, name=name, in_specs=[ANY], out_specs=ANY,
        out_shape=jax.ShapeDtypeStruct((8, m, n), block.dtype),
        scratch_shapes=[pltpu.SemaphoreType.DMA((7,)), pltpu.SemaphoreType.DMA((7,))],
    )(block)
    x, y, c = _me()
    return lax.dynamic_update_slice(gathered, block[None], (4 * x + 2 * y + c, 0, 0))


def _swap_halves(name, g):
    n, _, m, lanes = g.shape

    def body(g_ref, got_ref, send_sems, recv_sems):
        x, y, c = _me()
        copies = [pltpu.make_async_remote_copy(
            src_ref=g_ref.at[j, 1 - c], dst_ref=got_ref.at[j], send_sem=send_sems.at[j], recv_sem=recv_sems.at[j],
            device_id=(x, y, 1 - c), device_id_type=MESH) for j in range(n)]
        for cp in copies:
            cp.start()
        for cp in copies:
            cp.wait()

    return pl.pallas_call(
        body, name=name, in_specs=[ANY], out_specs=ANY, out_shape=jax.ShapeDtypeStruct((n, m, lanes), g.dtype),
        scratch_shapes=[pltpu.SemaphoreType.DMA((n,)), pltpu.SemaphoreType.DMA((n,))],
    )(g)


def _swap_sibling(name, mine):
    def body(m_ref, out_ref, send_sem, recv_sem):
        x, y, c = _me()
        cp = pltpu.make_async_remote_copy(src_ref=m_ref, dst_ref=out_ref, send_sem=send_sem, recv_sem=recv_sem,
                                          device_id=(x, y, 1 - c), device_id_type=MESH)
        cp.start()
        cp.wait()

    return pl.pallas_call(
        body, name=name, in_specs=[ANY], out_specs=ANY, out_shape=jax.ShapeDtypeStruct(mine.shape, mine.dtype),
        scratch_shapes=[pltpu.SemaphoreType.DMA, pltpu.SemaphoreType.DMA],
    )(mine)


def _chip_exchange(name, p):
    def body(p_ref, out_ref, send_sems, recv_sems):
        x, y, c = _me()
        my_chip = 2 * x + y
        chips = [(1 - x, y), (x, 1 - y), (1 - x, 1 - y)]
        sends = []
        for k, (px, py) in enumerate(chips):
            sends.append(pltpu.make_async_remote_copy(
                src_ref=p_ref.at[2 * px + py], dst_ref=out_ref.at[my_chip], send_sem=send_sems.at[k],
                recv_sem=recv_sems.at[k], device_id=(px, py, c), device_id_type=MESH))
        for cp in sends:
            cp.start()
        for k, (px, py) in enumerate(chips):
            pltpu.make_async_remote_copy(
                src_ref=p_ref.at[my_chip], dst_ref=out_ref.at[2 * px + py], send_sem=send_sems.at[k],
                recv_sem=recv_sems.at[k], device_id=(px, py, c), device_id_type=MESH).wait_recv()
        for cp in sends:
            cp.wait_send()

    got = pl.pallas_call(
        body, name=name, in_specs=[ANY], out_specs=ANY, out_shape=jax.ShapeDtypeStruct(p.shape, p.dtype),
        scratch_shapes=[pltpu.SemaphoreType.DMA((3,)), pltpu.SemaphoreType.DMA((3,))],
    )(p)
    x, y, _ = _me()
    my_chip = 2 * x + y
    return lax.dynamic_update_slice(got, lax.dynamic_index_in_dim(p, my_chip, axis=0, keepdims=True), (my_chip, 0, 0))


def _all_reduce_small(name, block):
    r, n = block.shape

    def body(x_ref, sum_ref, gath, send_sems, recv_sems):
        x, y, c = _me()
        me = 4 * x + 2 * y + c
        gath[me] = x_ref[...]
        sends = []
        for k in range(1, 8):
            px = x ^ ((k >> 2) & 1)
            py = y ^ ((k >> 1) & 1)
            pc = c ^ (k & 1)
            sends.append(pltpu.make_async_remote_copy(
                src_ref=x_ref, dst_ref=gath.at[me], send_sem=send_sems.at[k - 1], recv_sem=recv_sems.at[k - 1],
                device_id=(px, py, pc), device_id_type=MESH))
        for cp in sends:
            cp.start()
        for k in range(1, 8):
            peer = me ^ k
            pltpu.make_async_remote_copy(
                src_ref=x_ref, dst_ref=gath.at[peer], send_sem=send_sems.at[k - 1], recv_sem=recv_sems.at[k - 1],
                device_id=(x, y, c), device_id_type=MESH).wait_recv()
        for cp in sends:
            cp.wait_send()
        acc = gath[0]
        for d in range(1, 8):
            acc = acc + gath[d]
        sum_ref[...] = acc

    vm = pl.BlockSpec(memory_space=pltpu.VMEM)
    return pl.pallas_call(
        body, name=name, in_specs=[vm], out_specs=vm, out_shape=jax.ShapeDtypeStruct((r, n), F32),
        scratch_shapes=[pltpu.VMEM((8, r, n), F32), pltpu.SemaphoreType.DMA((7,)), pltpu.SemaphoreType.DMA((7,))],
    )(block)


def _add2(name, a, b, tm):
    rows = a.shape[0]
    return _rows(name, lambda p, q: ((p + q,), ()), rows, tm, [_rb(a, tm, LANES), _rb(b, tm, LANES)], [(LANES, BF16)])[0]


def _add4(name, p, tm):
    m = p.shape[1]
    flat = p.reshape(4 * m, LANES)
    nb = m // tm
    ins = [(flat, (tm, LANES), (lambda i, j=j: (j * nb + i, 0))) for j in range(4)]
    f32 = lambda v: v.astype(F32)
    return _rows(name, lambda a, b, c, d: ((((f32(a) + f32(b)) + f32(c)) + f32(d),), ()), m, tm, ins, [(LANES, F32)])[0]


SEG_ROWS = (D * W_IN_SHARD // LANES, 256 * D // LANES, 256 * D // LANES, 256 * D // LANES,
            D * W_UP_SHARD // LANES, W_DOWN_SHARD * D // LANES)
GRAD_ROWS = sum(SEG_ROWS)
CONVW_ROWS = 3 * W_UP_SHARD * 2 // LANES
GATHER_ROWS = 41600


def _flat(a):
    return a.reshape(-1, LANES)


def _gather_weights(w_in, w_a, w_b, w_out, w_up, w_down, conv_w):
    c = lax.axis_index("c")
    bits = lax.bitcast_convert_type(conv_w, BF16)
    pieces = [_flat(t.astype(BF16)) for t in (w_in, w_a, w_b, w_out, w_up, w_down)] + [_flat(bits)]
    pad = GATHER_ROWS - GRAD_ROWS - CONVW_ROWS
    shard = jnp.concatenate(pieces + [jnp.zeros((pad, LANES), BF16)], axis=0)
    half = GATHER_ROWS // 2
    mine = lax.dynamic_slice_in_dim(shard, c * half, half, axis=0)
    full = _all_gather8("all_gather_weights", mine).reshape(N_CHIP, GATHER_ROWS, LANES)
    offs = [0]
    for r in SEG_ROWS:
        offs.append(offs[-1] + r)
    seg = lambda i: full[:, offs[i]:offs[i + 1]]
    wi = seg(0).reshape(N_CHIP, D, W_IN_SHARD).transpose(1, 0, 2).reshape(D, N_CHIP * W_IN_SHARD)
    w_main = jnp.concatenate([wi[:, :FF_COL], wi[:, FF_COL + FOX_H:]], axis=1)
    w_ff = jnp.pad(wi[:, FF_COL:FF_COL + FOX_H], ((0, 0), (0, LANES - FOX_H)))
    wa, wb, wo = (seg(i).reshape(D, D) for i in (1, 2, 3))
    wu = seg(4).reshape(N_CHIP, D, W_UP_SHARD).transpose(1, 0, 2).reshape(D, 2 * D_FF)
    wd = seg(5).reshape(D_FF, D)
    cw_bits = full[:, GRAD_ROWS:GRAD_ROWS + CONVW_ROWS].reshape(N_CHIP, 3, W_UP_SHARD, 2)
    cw = lax.bitcast_convert_type(cw_bits, F32).transpose(1, 0, 2).reshape(3, 2 * D_FF)
    return w_main, w_ff, wa, wb, wo, wu, wd, cw


def _reduce_scatter_grads(d_main, d_ff, d_a, d_b, d_o, d_u, d_d):
    c = lax.axis_index("c")
    d_in = jnp.concatenate(d_main[:7] + [d_ff[:, :FOX_H]] + d_main[7:], axis=1)
    per_chip = [
        d_in.reshape(D, N_CHIP, W_IN_SHARD).transpose(1, 0, 2).reshape(N_CHIP, -1, LANES),
        d_a.reshape(N_CHIP, -1, LANES), d_b.reshape(N_CHIP, -1, LANES), d_o.reshape(N_CHIP, -1, LANES),
        d_u.reshape(D, N_CHIP, W_UP_SHARD).transpose(1, 0, 2).reshape(N_CHIP, -1, LANES),
        d_d.reshape(N_CHIP, -1, LANES),
        jnp.zeros((N_CHIP, GATHER_ROWS - GRAD_ROWS, LANES), F32),
    ]
    half = GATHER_ROWS // 2
    g = jnp.concatenate(per_chip, axis=1).reshape(N_CHIP, 2, half, LANES)
    from_sibling = _swap_halves("grad_swap_halves", g)
    mine = lax.dynamic_index_in_dim(g, c, axis=1, keepdims=False)
    tm = half // 5
    chip_sum = _add2("grad_chip_sum", mine.reshape(-1, LANES), from_sibling.reshape(-1, LANES), tm)
    pieces = _chip_exchange("grad_chip_exchange", chip_sum.reshape(N_CHIP, half, LANES))
    mine_half = _add4("grad_sum_chips", pieces, tm)
    other_half = _swap_sibling("grad_share_half", mine_half)
    lo = jnp.where(c == 0, mine_half, other_half)
    hi = jnp.where(c == 0, other_half, mine_half)
    return jnp.concatenate([lo, hi], axis=0)


def _local_step(x, target, norm_mix, fox_f_bias, hg_lb_logits, hg_norm, norm_ffn, conv_b, norm_final,
                w_main, w_ff, wa, wb, wo, wu, wd, conv_w):
    s = x.shape[0]
    bias = jnp.pad(fox_f_bias, ((0, 0), (0, LANES - FOX_H)))
    conv_w8 = jnp.pad(conv_w, ((0, 5), (0, 0)))
    t = min(FOX_T, s)

    n1, n1t = _rms_fwd("norm_mix_fwd", x, norm_mix)
    proj = _mm("in_proj", n1, w_main, "nn", F32, 1024, 1024, D)
    pff = _mm("in_proj_forget", n1, w_ff, "nn", F32, 1024, LANES, D)
    qb, kb, vb, ka = _fox_prep2(proj, pff, bias)
    o_b, laux = _fox_fwd2(qb, kb, vb, ka)
    o_raw, states = _hg_fwd(proj, hg_lb_logits)
    o_a = _hg_post_fwd(o_raw, proj, hg_norm)
    pa = _mm("branch_a", o_a, wa, "nn", F32, 1024, 1024, D)
    pb = _mm("branch_b", o_b, wb, "nn", F32, 1024, 1024, D)
    merged = _merge_fwd(pa, pb, proj)
    h1 = _mm("out_proj", merged, wo, "nn", F32, 1024, 1024, D, res=x)
    n2, n2t = _rms_fwd("norm_ffn_fwd", h1, norm_ffn)
    u = _mm("ffn_up", n2, wu, "nn", F32, 1024, W_UP_SHARD, D)
    act = _convglu_fwd(u, conv_w8, conv_b)
    h2 = _mm("ffn_down", act, wd, "nn", F32, 512, 1024, D_FF, res=h1)
    (dh2,), (d_norm_final, loss_row) = _final(h2, target, norm_final)

    dact = _mm("ffn_down_dx", dh2, wd, "nt", BF16, 1024, D_FF, D)
    d_wd = _mm("ffn_down_dw", act, dh2, "tn", F32, D_FF // 2, 1024, DW_TK // 2)
    (du,), (d_conv_w8, d_conv_b) = _convglu_bwd(u, dact, conv_w8, conv_b)
    dn2 = _mm("ffn_up_dx", du, wu, "nt", F32, 1024, 1024, W_UP_SHARD)
    d_wu = _mm("ffn_up_dw", n2t, du, "nn", F32, 1024, W_UP_SHARD, DW_TK)
    (dh1,), (d_norm_ffn,) = _rms_bwd("norm_ffn_bwd", h1, norm_ffn, [dn2], dh2)

    dmerged = _mm("out_proj_dx", dh1, wo, "nt", F32, 1024, 1024, D)
    d_wo = _mm("out_proj_dw", merged, dh1, "tn", F32, 1024, 1024, DW_TK)
    dpa, dpb, dga, dgb = _merge_bwd(dmerged, pa, pb, proj)
    do_a = _mm("branch_a_dx", dpa, wa, "nt", F32, 1024, 1024, D)
    do_b = _mm("branch_b_dx", dpb, wb, "nt", BF16, 1024, 1024, D)
    d_wa = _mm("branch_a_dw", o_a, dpa, "tn", F32, 1024, 1024, DW_TK)
    d_wb = _mm("branch_b_dw", o_b, dpb, "tn", F32, 1024, 1024, DW_TK)

    (do_raw, dhg), (d_hg_norm,) = _hg_post_bwd(do_a, o_raw, proj, hg_norm)
    dhq, dhf, dhi, d_lb_logits = _hg_bwd(proj, hg_lb_logits, states, do_raw)

    dfq, dfk, dfv, dcrow = _fox_bwd2(qb, kb, vb, ka, o_b, laux, do_b)
    dct = jnp.pad(dcrow.reshape(FOX_H, s), ((0, LANES - FOX_H), (0, 0)))
    dff, d_bias = _fox_gate_bwd(dct, pff, bias)

    pieces = [dhq, dhf, dhi, dhg, dfq, dfk, dfv, dga, dgb]
    dn1 = _mm_sum_nt("in_proj_dx", pieces, w_main, (dff, w_ff), 512, 1024)
    d_w_main = [_mm("in_proj_dw_%d" % i, n1t, p, "nn", F32, 1024, 1024, DW_TK) for i, p in enumerate(pieces)]
    d_w_ff = _mm("in_proj_forget_dw", n1t, dff, "nn", F32, 1024, LANES, DW_TK)
    (dx,), (d_norm_mix,) = _rms_bwd("norm_mix_bwd", x, norm_mix, [dn1], dh1)

    small = dict(norm_mix=d_norm_mix, fox_f_bias=d_bias[:, :FOX_H], hg_lb_logits=d_lb_logits, hg_norm=d_hg_norm,
                 norm_ffn=d_norm_ffn, conv_b=d_conv_b, norm_final=d_norm_final, conv_w=d_conv_w8[:3], loss=loss_row)
    big = (d_w_main, d_w_ff, d_wa, d_wb, d_wo, d_wu, d_wd)
    return dx, small, big


SMALL_KEYS = ("norm_mix", "fox_f_bias", "hg_lb_logits", "hg_norm", "norm_ffn", "conv_b", "norm_final")


def _pack_small(parts):
    rows, layout = [], []
    for key, arr in parts:
        flat = arr.reshape(-1)
        n = flat.shape[0]
        nr = -(-n // LANES)
        rows.append(jnp.pad(flat, (0, nr * LANES - n)).reshape(nr, LANES))
        layout.append((key, arr.shape, n, nr))
    packed = jnp.concatenate(rows, axis=0)
    pad = -packed.shape[0] % 8
    return jnp.pad(packed, ((0, pad), (0, 0))), layout


def _unpack_small(packed, layout):
    out, r0 = {}, 0
    for key, shape, n, nr in layout:
        out[key] = packed[r0:r0 + nr].reshape(-1)[:n].reshape(shape)
        r0 += nr
    return out


def kernel(x, norm_mix, w_in, fox_f_bias, hg_lb_logits, hg_norm, w_branch_a, w_branch_b, w_out, norm_ffn, w_up, conv_w, conv_b, w_down, norm_final, loss_target, m_norm_mix, m_w_in, m_fox_f_bias, m_hg_lb_logits, m_hg_norm, m_w_branch_a, m_w_branch_b, m_w_out, m_norm_ffn, m_w_up, m_conv_w, m_conv_b, m_w_down, m_norm_final, v_norm_mix, v_w_in, v_fox_f_bias, v_hg_lb_logits, v_hg_norm, v_w_branch_a, v_w_branch_b, v_w_out, v_norm_ffn, v_w_up, v_conv_w, v_conv_b, v_w_down, v_norm_final):
    chip = 2 * lax.axis_index("x") + lax.axis_index("y")
    w_main, w_ff, wa, wb, wo, wu, wd, cw = _gather_weights(
        w_in[0], w_branch_a[0], w_branch_b[0], w_out[0], w_up[0], w_down[0], conv_w[0])
    dx, small, big = _local_step(
        x[0], loss_target[0], norm_mix, fox_f_bias, hg_lb_logits, hg_norm, norm_ffn, conv_b,
        norm_final.reshape(1, D), w_main, w_ff, wa, wb, wo, wu, wd, cw)

    packed, layout = _pack_small([(k, small[k]) for k in SMALL_KEYS + ("conv_w", "loss")])
    red = _unpack_small(_all_reduce_small("all_reduce_small", packed), layout)
    loss = red["loss"][0, 0]
    g_conv_w = lax.dynamic_slice_in_dim(red["conv_w"], chip * W_UP_SHARD, W_UP_SHARD, axis=1)

    gflat = _reduce_scatter_grads(*big)
    offs = [0]
    for r in SEG_ROWS:
        offs.append(offs[-1] + r)
    shapes = [(D, W_IN_SHARD), (256, D), (256, D), (256, D), (D, W_UP_SHARD), (W_DOWN_SHARD, D)]
    g_big = [gflat[offs[i]:offs[i + 1]].reshape(shapes[i]) for i in range(6)]

    names = ["norm_mix", "w_in", "fox_f_bias", "hg_lb_logits", "hg_norm", "w_branch_a", "w_branch_b", "w_out",
             "norm_ffn", "w_up", "conv_w", "conv_b", "w_down", "norm_final"]
    weights = dict(norm_mix=norm_mix, w_in=w_in, fox_f_bias=fox_f_bias, hg_lb_logits=hg_lb_logits, hg_norm=hg_norm,
                   w_branch_a=w_branch_a, w_branch_b=w_branch_b, w_out=w_out, norm_ffn=norm_ffn, w_up=w_up,
                   conv_w=conv_w, conv_b=conv_b, w_down=w_down, norm_final=norm_final)
    ms = dict(norm_mix=m_norm_mix, w_in=m_w_in, fox_f_bias=m_fox_f_bias, hg_lb_logits=m_hg_lb_logits,
              hg_norm=m_hg_norm, w_branch_a=m_w_branch_a, w_branch_b=m_w_branch_b, w_out=m_w_out,
              norm_ffn=m_norm_ffn, w_up=m_w_up, conv_w=m_conv_w, conv_b=m_conv_b, w_down=m_w_down,
              norm_final=m_norm_final)
    vs = dict(norm_mix=v_norm_mix, w_in=v_w_in, fox_f_bias=v_fox_f_bias, hg_lb_logits=v_hg_lb_logits,
              hg_norm=v_hg_norm, w_branch_a=v_w_branch_a, w_branch_b=v_w_branch_b, w_out=v_w_out,
              norm_ffn=v_norm_ffn, w_up=v_w_up, conv_w=v_conv_w, conv_b=v_conv_b, w_down=v_w_down,
              norm_final=v_norm_final)

    grads, deltas, new_m, new_v = {}, {}, {}, {}
    big_names = ["w_in", "w_branch_a", "w_branch_b", "w_out", "w_up", "w_down"]
    for name, g2 in zip(big_names, g_big):
        shape = weights[name].shape
        rows = g2.shape[0]
        d_, m_, v_ = _adamw("adamw_" + name, weights[name][0], g2, ms[name][0], vs[name][0], tm=rows // 8)
        grads[name], deltas[name], new_m[name], new_v[name] = (a.reshape(shape) for a in (g2, d_, m_, v_))
    shape = conv_w.shape
    d_, m_, v_ = _adamw("adamw_conv_w", conv_w[0], g_conv_w, m_conv_w[0], v_conv_w[0])
    grads["conv_w"], deltas["conv_w"], new_m["conv_w"], new_v["conv_w"] = (
        a.reshape(shape) for a in (g_conv_w, d_, m_, v_))
    gs = {k: red[k].reshape(weights[k].shape) for k in SMALL_KEYS}
    pw, lay = _pack_small([(k, weights[k]) for k in SMALL_KEYS])
    pg, _ = _pack_small([(k, gs[k]) for k in SMALL_KEYS])
    pm, _ = _pack_small([(k, ms[k]) for k in SMALL_KEYS])
    pv, _ = _pack_small([(k, vs[k]) for k in SMALL_KEYS])
    d_, m_, v_ = (_unpack_small(a, lay) for a in _adamw("adamw_small", pw, pg, pm, pv))
    for k in SMALL_KEYS:
        grads[k], deltas[k], new_m[k], new_v[k] = gs[k], d_[k], m_[k], v_[k]

    return (loss, dx[None], *[grads[n] for n in names], *[deltas[n] for n in names],
            *[new_m[n] for n in names], *[new_v[n] for n in names])
```

```python
import functools

import jax
import jax.numpy as jnp
from jax import lax
from jax.experimental import pallas as pl
from jax.experimental.pallas import tpu as pltpu

F32 = jnp.float32
BF16 = jnp.bfloat16

D = 1024
HG_H, HG_DK = 8, 128
FOX_H, FOX_D = 16, 64
D_FF = 2816
EPS = 1e-6
N_CHIP = 4
LANES = 128
W_IN_SHARD = 2308
W_UP_SHARD = 1408
W_DOWN_SHARD = 704
FF_COL = 7168
ADAM_LR, ADAM_B1, ADAM_B2, ADAM_EPS, ADAM_WD, ADAM_STEP = 0.001, 0.9, 0.999, 1e-08, 0.01, 10

HG_C = 16
HG_T = 256
HG_UNROLL = 16
HG_UNROLL_BWD = 4
FOX_T = 512
DW_TK = 2048
VMEM_LIMIT = 56 * 1024 * 1024
MESH = pl.DeviceIdType.MESH
ANY = pl.BlockSpec(memory_space=pl.ANY)


def _cparams(sem):
    return pltpu.CompilerParams(dimension_semantics=sem, vmem_limit_bytes=VMEM_LIMIT)


def _sigmoid(x):
    return 1.0 / (1.0 + jnp.exp(-x))


def _dot(a, b, dims):
    return lax.dot_general(a.astype(BF16), b.astype(BF16), (dims, ((), ())), preferred_element_type=F32)


NN = ((1,), (0,))
NT = ((1,), (1,))
TN = ((0,), (0,))


def _split_dot(tri, x, parts, dims=NN):
    acc = None
    r = x
    for _ in range(parts):
        p = r.astype(BF16)
        t = lax.dot_general(tri, p, (dims, ((), ())), preferred_element_type=F32)
        acc = t if acc is None else acc + t
        r = r - p.astype(F32)
    return acc


def _rb(arr, tm, width, cb=0):
    return (arr, (tm, width), lambda i: (i, cb))


def _cst(arr):
    return (arr, arr.shape, lambda i: (0,) * arr.ndim)


def _rows(name, fn, n_rows, tm, ins, outs, accs=(), reverse=False):
    n_in, n_out, n_acc = len(ins), len(outs), len(accs)
    nb = n_rows // tm

    def body(*refs):
        vals = [r[...] for r in refs[:n_in]]
        o, a = fn(*vals)
        for r, v in zip(refs[n_in:n_in + n_out], o):
            r[...] = v.astype(r.dtype)
        if n_acc:
            acc_refs = refs[n_in + n_out:]

            @pl.when(pl.program_id(0) == 0)
            def _():
                for r in acc_refs:
                    r[...] = jnp.zeros_like(r)

            for r, v in zip(acc_refs, a):
                r[...] += v

    if reverse:
        rowmap = lambda i: (nb - 1 - i, 0)
    else:
        rowmap = lambda i: (i, 0)
    in_specs = [pl.BlockSpec(bs, im) for (_, bs, im) in ins]
    out_specs = [pl.BlockSpec((tm, w), rowmap) for (w, _) in outs]
    out_specs += [pl.BlockSpec((r, w), lambda i: (0, 0)) for (r, w) in accs]
    out_shape = [jax.ShapeDtypeStruct((n_rows, w), dt) for (w, dt) in outs]
    out_shape += [jax.ShapeDtypeStruct((r, w), F32) for (r, w) in accs]
    res = pl.pallas_call(
        body, name=name, grid=(nb,), in_specs=in_specs, out_specs=out_specs, out_shape=out_shape,
        compiler_params=_cparams(("arbitrary",)),
    )(*[a for a, _, _ in ins])
    return (res[:n_out], res[n_out:]) if n_acc else res


def _mm(name, a, b, mode, out_dtype, tm, tn, tk, res=None):
    if mode == "nn":
        (m, k), n = a.shape, b.shape[1]
    elif mode == "nt":
        (m, k), n = a.shape, b.shape[0]
    else:
        (k, m), n = a.shape, b.shape[1]
    tm, tn, tk = min(tm, m), min(tn, n), min(tk, k)
    assert m % tm == 0 and n % tn == 0 and k % tk == 0, (name, m, n, k, tm, tn, tk)
    if mode == "nn":
        a_spec = pl.BlockSpec((tm, tk), lambda i, j, kk: (i, kk))
        b_spec = pl.BlockSpec((tk, tn), lambda i, j, kk: (kk, j))
        dims = NN
    elif mode == "nt":
        a_spec = pl.BlockSpec((tm, tk), lambda i, j, kk: (i, kk))
        b_spec = pl.BlockSpec((tn, tk), lambda i, j, kk: (j, kk))
        dims = NT
    else:
        a_spec = pl.BlockSpec((tk, tm), lambda i, j, kk: (kk, i))
        b_spec = pl.BlockSpec((tk, tn), lambda i, j, kk: (kk, j))
        dims = TN
    nk = k // tk
    has_res = res is not None
    acc_in_out = out_dtype == F32 and not has_res

    def body(*refs):
        a_ref, b_ref = refs[0], refs[1]
        r_ref = refs[2] if has_res else None
        o_ref = refs[3] if has_res else refs[2]
        part = _dot(a_ref[...], b_ref[...], dims)

        def finish(val):
            if has_res:
                val = val + r_ref[...]
            o_ref[...] = val.astype(o_ref.dtype)

        if nk == 1:
            finish(part)
        elif acc_in_out:
            kk = pl.program_id(2)

            @pl.when(kk == 0)
            def _():
                o_ref[...] = part

            @pl.when(kk > 0)
            def _():
                o_ref[...] += part
        else:
            acc_ref = refs[-1]
            kk = pl.program_id(2)

            @pl.when(kk == 0)
            def _():
                acc_ref[...] = part

            @pl.when(kk > 0)
            def _():
                acc_ref[...] += part

            @pl.when(kk == nk - 1)
            def _():
                finish(acc_ref[...])

    in_specs = [a_spec, b_spec]
    args = [a, b]
    if has_res:
        in_specs.append(pl.BlockSpec((tm, tn), lambda i, j, kk: (i, j)))
        args.append(res)
    return pl.pallas_call(
        body, name=name, grid=(m // tm, n // tn, nk), in_specs=in_specs,
        out_specs=pl.BlockSpec((tm, tn), lambda i, j, kk: (i, j)),
        out_shape=jax.ShapeDtypeStruct((m, n), out_dtype),
        scratch_shapes=[pltpu.VMEM((tm, tn), F32)] if nk > 1 and not acc_in_out else [],
        compiler_params=_cparams(("arbitrary", "arbitrary", "arbitrary")),
    )(*args)


def _mm_sum_nt(name, pieces, w, extra, tm, tn):
    n_p = len(pieces)
    m, k = pieces[0].shape
    n = w.shape[0]
    xa, xb = extra
    ke = xa.shape[1]
    tm, tn = min(tm, m), min(tn, n)

    def body(*refs):
        p_refs, w_ref, xa_ref, xb_ref, o_ref = refs[:n_p], refs[n_p], refs[n_p + 1], refs[n_p + 2], refs[-1]
        kk = pl.program_id(2)

        @pl.when(kk == 0)
        def _():
            o_ref[...] = _dot(p_refs[0][...], w_ref[...], NT)

        for i in range(1, n_p):
            @pl.when(kk == i)
            def _(i=i):
                o_ref[...] += _dot(p_refs[i][...], w_ref[...], NT)

        @pl.when(kk == n_p)
        def _():
            o_ref[...] += _dot(xa_ref[...], xb_ref[...], NT)

    in_specs = [pl.BlockSpec((tm, k), lambda i, j, kk: (i, 0)) for _ in range(n_p)]
    in_specs.append(pl.BlockSpec((tn, k), lambda i, j, kk: (j, jnp.minimum(kk, n_p - 1))))
    in_specs += [pl.BlockSpec((tm, ke), lambda i, j, kk: (i, 0)), pl.BlockSpec((tn, ke), lambda i, j, kk: (j, 0))]
    return pl.pallas_call(
        body, name=name, grid=(m // tm, n // tn, n_p + 1), in_specs=in_specs,
        out_specs=pl.BlockSpec((tm, tn), lambda i, j, kk: (i, j)),
        out_shape=jax.ShapeDtypeStruct((m, n), F32),
        compiler_params=_cparams(("arbitrary", "arbitrary", "arbitrary")),
    )(*pieces, w, xa, xb)


def _rms_fwd(name, x, gain, tm=256):
    s = x.shape[0]

    def body(x_ref, g_ref, y_ref, yt_ref):
        xb = x_ref[...]
        y = xb * lax.rsqrt(jnp.mean(xb * xb, axis=-1, keepdims=True) + EPS) * g_ref[...]
        y_ref[...] = y.astype(BF16)
        yt_ref[...] = y.T.astype(BF16)

    return pl.pallas_call(
        body, name=name, grid=(s // tm,),
        in_specs=[pl.BlockSpec((tm, D), lambda i: (i, 0)), pl.BlockSpec((1, D), lambda i: (0, 0))],
        out_specs=[pl.BlockSpec((tm, D), lambda i: (i, 0)), pl.BlockSpec((D, tm), lambda i: (0, i))],
        out_shape=[jax.ShapeDtypeStruct((s, D), BF16), jax.ShapeDtypeStruct((D, s), BF16)],
        compiler_params=_cparams(("arbitrary",)),
    )(x, gain)


def _rms_bwd(name, x, gain, dns, dres, tm=256):
    s = x.shape[0]
    n_dn = len(dns)

    def fn(xb, g, *rest):
        dn = rest[0]
        for t in rest[1:n_dn]:
            dn = dn + t
        r = lax.rsqrt(jnp.mean(xb * xb, axis=-1, keepdims=True) + EPS)
        xhat = xb * r
        dxh = dn * g
        dx = r * (dxh - xhat * jnp.mean(dxh * xhat, axis=-1, keepdims=True)) + rest[n_dn]
        return (dx,), (jnp.sum(dn * xhat, axis=0, keepdims=True),)

    ins = [_rb(x, tm, D), _cst(gain)] + [_rb(t, tm, D) for t in dns] + [_rb(dres, tm, D)]
    return _rows(name, fn, s, tm, ins, [(D, F32)], [(1, D)])


def _final(h2, target, gain, tm=256):
    s = h2.shape[0]

    def fn(hb, tb, g):
        r = lax.rsqrt(jnp.mean(hb * hb, axis=-1, keepdims=True) + EPS)
        xhat = hb * r
        e = xhat * g - tb
        dy = e * (1.0 / D)
        dxh = dy * g
        dh = r * (dxh - xhat * jnp.mean(dxh * xhat, axis=-1, keepdims=True))
        lrow = 0.5 * jnp.sum(jnp.sum(e * e, axis=-1, keepdims=True) * (1.0 / D), axis=0, keepdims=True)
        return (dh,), (jnp.sum(dy * xhat, axis=0, keepdims=True), jnp.broadcast_to(lrow, (1, LANES)))

    return _rows("final_norm_loss", fn, s, tm, [_rb(h2, tm, D), _rb(target, tm, D), _cst(gain)],
                 [(D, F32)], [(1, D), (1, LANES)])


def _merge_fwd(pa, pb, proj, tm=256):
    s = pa.shape[0]

    def fn(a, b, ga, gb):
        return (_sigmoid(ga) * a + _sigmoid(gb) * b,), ()

    ins = [_rb(pa, tm, D), _rb(pb, tm, D), _rb(proj, tm, D, 7), _rb(proj, tm, D, 8)]
    return _rows("merge_fwd", fn, s, tm, ins, [(D, BF16)])[0]


def _merge_bwd(dmerged, pa, pb, proj, tm=256):
    s = pa.shape[0]

    def fn(dm, a, b, ga, gb):
        sa, sb = _sigmoid(ga), _sigmoid(gb)
        return (dm * sa, dm * sb, dm * a * sa * (1.0 - sa), dm * b * sb * (1.0 - sb)), ()

    ins = [_rb(dmerged, tm, D), _rb(pa, tm, D), _rb(pb, tm, D), _rb(proj, tm, D, 7), _rb(proj, tm, D, 8)]
    return _rows("merge_bwd", fn, s, tm, ins, [(D, BF16), (D, BF16), (D, BF16), (D, BF16)])


def _gelu_parts(x):
    cdf = 0.5 * (1.0 + lax.erf(x * 0.7071067811865476))
    pdf = 0.3989422804014327 * jnp.exp(-0.5 * x * x)
    return x * cdf, cdf + x * pdf


def _conv_taps(u_ext, n_out, first):
    n = u_ext.shape[0]
    cur = u_ext[8:8 + n_out]
    m1 = pltpu.roll(u_ext, 1, 0)[8:8 + n_out]
    m2 = pltpu.roll(u_ext, 2, 0)[8:8 + n_out]
    return m2, m1, cur


def _convglu_fwd(u, conv_w8, conv_b, tm=64):
    s, w = u.shape
    tb = tm // 8

    def fn(ub, up, cw, cb):
        i = pl.program_id(0)
        up = jnp.where(i == 0, 0.0, up)
        m2, m1, cur = _conv_taps(jnp.concatenate([up, ub], axis=0), tm, None)
        acc = cb + cw[0:1] * m2 + cw[1:2] * m1 + cw[2:3] * cur
        gl, dgl = _gelu_parts(acc[:, :D_FF])
        val = acc[:, D_FF:]
        return (gl * val, gl, val * dgl), ()

    ins = [_rb(u, tm, w), (u, (8, w), lambda i: (jnp.maximum(i * tb - 1, 0), 0)), _cst(conv_w8), _cst(conv_b)]
    return _rows("convglu_fwd", fn, s, tm, ins, [(D_FF, BF16)] * 3)


def _convglu_bwd(u, dact, gl, gd, conv_w8, tm=64):
    s, w = u.shape
    tb = tm // 8
    nb = s // tm

    def fn(ub, up, db, dn, glb, gln, gdb, gdn, cw):
        i = pl.program_id(0)
        up = jnp.where(i == 0, 0.0, up)
        dn = jnp.where(i == nb - 1, 0.0, dn.astype(F32))
        ne = tm + 8
        m2, m1, cur = _conv_taps(jnp.concatenate([up, ub], axis=0), tm, None)
        ext = lambda blk, nxt: jnp.concatenate([blk.astype(F32), nxt.astype(F32)], axis=0)
        de = ext(db, dn)
        dacc = jnp.concatenate([de * ext(gdb, gdn), de * ext(glb, gln)], axis=1)
        p1 = pltpu.roll(dacc, ne - 1, 0)[:tm]
        p2 = pltpu.roll(dacc, ne - 2, 0)[:tm]
        d0 = dacc[:tm]
        du = cw[2:3] * d0 + cw[1:2] * p1 + cw[0:1] * p2
        zero5 = jnp.zeros((5, w), F32)
        dcw = jnp.concatenate([
            jnp.sum(d0 * m2, axis=0, keepdims=True), jnp.sum(d0 * m1, axis=0, keepdims=True),
            jnp.sum(d0 * cur, axis=0, keepdims=True), zero5], axis=0)
        return (du,), (dcw, jnp.sum(d0, axis=0, keepdims=True))

    nxt = lambda arr: (arr, (8, D_FF), lambda i: (jnp.minimum((i + 1) * tb, s // 8 - 1), 0))
    ins = [_rb(u, tm, w), (u, (8, w), lambda i: (jnp.maximum(i * tb - 1, 0), 0)),
           _rb(dact, tm, D_FF), nxt(dact), _rb(gl, tm, D_FF), nxt(gl), _rb(gd, tm, D_FF), nxt(gd), _cst(conv_w8)]
    return _rows("convglu_bwd", fn, s, tm, ins, [(w, BF16)], [(8, w), (1, w)])


def _chunk_scan(x, t_iota, reverse):
    k = 1
    while k < HG_C:
        if reverse:
            x = x + jnp.where(t_iota < HG_C - k, pltpu.roll(x, HG_C - k, 0), 0.0)
        else:
            x = x + jnp.where(t_iota >= k, pltpu.roll(x, k, 0), 0.0)
        k *= 2
    return x


def _hg_gates(hq, hf, lb):
    sq = _sigmoid(hq)
    q = hq * sq
    sg = _sigmoid(hf)
    f = lb + (1.0 - lb) * sg
    return q, sq, sg, f, 1.0 - f, jnp.log(f)


def _lb_of(logits):
    l0, l1 = logits[0:1], logits[1:2]
    mx = jnp.maximum(l0, l1)
    e0, e1 = jnp.exp(l0 - mx), jnp.exp(l1 - mx)
    return e0 / (e0 + e1)


def _tri(n, lower):
    r = lax.broadcasted_iota(jnp.int32, (n, n), 0)
    c = lax.broadcasted_iota(jnp.int32, (n, n), 1)
    return jnp.where((r >= c) if lower else (r <= c), 1.0, 0.0).astype(BF16)


def _hg_intra_terms(q, kk, b, t_iota):
    ws, ps = [], []
    for s in range(HG_C):
        p = jnp.where(t_iota >= s, jnp.exp(b - b[s:s + 1]), 0.0)
        ps.append(p)
        ws.append(q * kk[s:s + 1] * p)
    return jnp.concatenate(ws, axis=0), ps


def _hg_fwd(proj, lb_logits):
    s = proj.shape[0]
    nt = s // HG_T
    nc = HG_T // HG_C

    def body(q_ref, f_ref, i_ref, l_ref, o_ref, st_ref, state):
        @pl.when(pl.program_id(1) == 0)
        def _():
            state[...] = jnp.zeros_like(state)

        st_ref[0, 0] = state[...]
        lb = _lb_of(l_ref[...])
        ones = jnp.ones((HG_DK, HG_DK), BF16)
        t_iota = lax.broadcasted_iota(jnp.int32, (HG_C, HG_DK), 0)
        cc = HG_C * HG_C

        def group(gi, st):
            units = []
            for u in range(HG_UNROLL):
                r = pl.ds(pl.multiple_of((gi * HG_UNROLL + u) * HG_C, HG_C), HG_C)
                q, _, _, _, kk, g = _hg_gates(q_ref[r, :], f_ref[r, :], lb)
                b = _chunk_scan(g, t_iota, False)
                b_end = b[HG_C - 1:HG_C]
                w_all, _ = _hg_intra_terms(q, kk, b, t_iota)
                units.append((r, i_ref[r, :], q * jnp.exp(b), jnp.exp(b_end), kk * jnp.exp(b_end - b), w_all))
            a_all = _dot(jnp.concatenate([un[5] for un in units], axis=0), ones, NN)
            kvs = [_dot(v, kd, TN) for (_, v, _, _, kd, _) in units]
            sts = [st]
            for (_, _, _, dec, _, _), kv in zip(units, kvs):
                sts.append(sts[-1] * dec + kv)
            for ui, (r, v, qd, _, _, _) in enumerate(units):
                o = _dot(qd, sts[ui], NT)
                for si in range(HG_C):
                    o = o + a_all[ui * cc + si * HG_C:ui * cc + (si + 1) * HG_C] * v[si:si + 1]
                o_ref[r, :] = o
            return sts[-1]

        state[...] = lax.fori_loop(0, nc // HG_UNROLL, group, state[...])

    col = lambda off: pl.BlockSpec((HG_T, HG_DK), lambda h, t: (t, off + h))
    return pl.pallas_call(
        body, name="hgrn2_fwd", grid=(HG_H, nt),
        in_specs=[col(0), col(8), col(16), pl.BlockSpec((2, HG_DK), lambda h, t: (0, h))],
        out_specs=[pl.BlockSpec((HG_T, HG_DK), lambda h, t: (t, h)),
                   pl.BlockSpec((1, 1, HG_DK, HG_DK), lambda h, t: (h, t, 0, 0))],
        out_shape=[jax.ShapeDtypeStruct((s, D), F32), jax.ShapeDtypeStruct((HG_H, nt, HG_DK, HG_DK), F32)],
        scratch_shapes=[pltpu.VMEM((HG_DK, HG_DK), F32)],
        compiler_params=_cparams(("arbitrary", "arbitrary")),
    )(proj, proj, proj, lb_logits)


def _hg_bwd(proj, lb_logits, states, do_raw):
    s = proj.shape[0]
    nt = s // HG_T
    nc = HG_T // HG_C

    def body(q_ref, f_ref, i_ref, l_ref, st_ref, do_ref, dq_ref, df_ref, di_ref, dl_ref, st_all, adj):
        tb = pl.program_id(1)

        @pl.when(tb == 0)
        def _():
            adj[...] = jnp.zeros_like(adj)
            dl_ref[...] = jnp.zeros_like(dl_ref)

        lb = _lb_of(l_ref[...])
        ones = jnp.ones((HG_DK, HG_DK), BF16)
        t_iota = lax.broadcasted_iota(jnp.int32, (HG_C, HG_DK), 0)
        cc = HG_C * HG_C

        def fwd_group(gi, st):
            terms = []
            for u in range(HG_UNROLL):
                ci = gi * HG_UNROLL + u
                r = pl.ds(pl.multiple_of(ci * HG_C, HG_C), HG_C)
                _, _, _, _, kk, g = _hg_gates(q_ref[r, :], f_ref[r, :], lb)
                b = _chunk_scan(g, t_iota, False)
                b_end = b[HG_C - 1:HG_C]
                terms.append((ci, jnp.exp(b_end), _dot(i_ref[r, :], kk * jnp.exp(b_end - b), TN)))
            for ci, dec, kv in terms:
                st_all[ci] = st
                st = st * dec + kv
            return st

        lax.fori_loop(0, nc // HG_UNROLL, fwd_group, st_ref[0, 0])

        def bwd_group(gj, dlb):
            units = []
            for u in range(HG_UNROLL_BWD):
                ci = nc - 1 - (gj * HG_UNROLL_BWD + u)
                r = pl.ds(pl.multiple_of(ci * HG_C, HG_C), HG_C)
                hq, hf, v, do = q_ref[r, :], f_ref[r, :], i_ref[r, :], do_ref[r, :]
                q, sq, sg, f, kk, g = _hg_gates(hq, hf, lb)
                b = _chunk_scan(g, t_iota, False)
                b_end = b[HG_C - 1:HG_C]
                e_b, e_be, dec = jnp.exp(b), jnp.exp(b_end - b), jnp.exp(b_end)
                w_all, ps = _hg_intra_terms(q, kk, b, t_iota)
                x_all = jnp.concatenate([do * v[si:si + 1] for si in range(HG_C)], axis=0)
                units.append(dict(ci=ci, r=r, hq=hq, v=v, do=do, q=q, sq=sq, sg=sg, f=f, kk=kk, e_b=e_b, e_be=e_be,
                                  dec=dec, kd=kk * e_be, w=w_all, ps=ps, x=x_all))
            both = _dot(jnp.concatenate([un["w"] for un in units] + [un["x"] for un in units], axis=0), ones, NN)
            st0s = [st_all[un["ci"]] for un in units]
            st_ends = [st0 * un["dec"] + _dot(un["v"], un["kd"], TN) for un, st0 in zip(units, st0s)]
            dqks = [_dot(un["do"], un["q"] * un["e_b"], TN) for un in units]
            es = [adj[...]]
            for un, dqk in zip(units, dqks):
                es.append(es[-1] * un["dec"] + dqk)
            adj[...] = es[-1]
            for ui, un in enumerate(units):
                e, q, kk, v, do = es[ui], un["q"], un["kk"], un["v"], un["do"]
                tail = jnp.sum(e * st_ends[ui], axis=0, keepdims=True)
                dq = un["e_b"] * _dot(do, st0s[ui], NN)
                dk = un["e_be"] * _dot(v, e, NN)
                dv = _dot(un["kd"], e, NT)
                a0 = ui * cc
                d0 = (HG_UNROLL_BWD + ui) * cc
                for si in range(HG_C):
                    da = both[d0 + si * HG_C:d0 + (si + 1) * HG_C]
                    aa = both[a0 + si * HG_C:a0 + (si + 1) * HG_C]
                    dap = da * un["ps"][si]
                    dq = dq + dap * kk[si:si + 1]
                    hit = t_iota == si
                    dk = dk + jnp.where(hit, jnp.sum(dap * q, axis=0, keepdims=True), 0.0)
                    dv = dv + jnp.where(hit, jnp.sum(aa * do, axis=0, keepdims=True), 0.0)
                dg = _chunk_scan(q * dq - kk * dk, t_iota, True) + tail
                dfg = dg / un["f"] - dk
                sq, sg, hq, r = un["sq"], un["sg"], un["hq"], un["r"]
                dq_ref[r, :] = (dq * sq * (1.0 + hq * (1.0 - sq))).astype(dq_ref.dtype)
                df_ref[r, :] = (dfg * (1.0 - lb) * sg * (1.0 - sg)).astype(df_ref.dtype)
                di_ref[r, :] = dv.astype(di_ref.dtype)
                dlb = dlb + jnp.sum(dfg * (1.0 - sg), axis=0, keepdims=True)
            return dlb

        dlb = lax.fori_loop(0, nc // HG_UNROLL_BWD, bwd_group, jnp.zeros((1, HG_DK), F32))
        dl0 = dlb * lb * (1.0 - lb)
        dl_ref[...] += jnp.concatenate([dl0, -dl0], axis=0)

    col = lambda off: pl.BlockSpec((HG_T, HG_DK), lambda h, t: (nt - 1 - t, off + h))
    out_col = pl.BlockSpec((HG_T, HG_DK), lambda h, t: (nt - 1 - t, h))
    return pl.pallas_call(
        body, name="hgrn2_bwd", grid=(HG_H, nt),
        in_specs=[col(0), col(8), col(16), pl.BlockSpec((2, HG_DK), lambda h, t: (0, h)),
                  pl.BlockSpec((1, 1, HG_DK, HG_DK), lambda h, t: (h, nt - 1 - t, 0, 0)), col(0)],
        out_specs=[out_col, out_col, out_col, pl.BlockSpec((2, HG_DK), lambda h, t: (0, h))],
        out_shape=[jax.ShapeDtypeStruct((s, D), BF16)] * 3 + [jax.ShapeDtypeStruct((2, D), F32)],
        scratch_shapes=[pltpu.VMEM((nc, HG_DK, HG_DK), F32), pltpu.VMEM((HG_DK, HG_DK), F32)],
        compiler_params=_cparams(("arbitrary", "arbitrary")),
    )(proj, proj, proj, lb_logits, states, do_raw)


def _hg_post_fwd(o_raw, proj, gnorm, tm=256):
    s = o_raw.shape[0]

    def fn(o, hg, gn):
        outs = []
        for h in range(HG_H):
            sl = slice(h * HG_DK, (h + 1) * HG_DK)
            oh, gh = o[:, sl], hg[:, sl]
            r = lax.rsqrt(jnp.mean(oh * oh, axis=-1, keepdims=True) + EPS)
            outs.append(oh * r * gn * (gh * _sigmoid(gh)))
        return (jnp.concatenate(outs, axis=1),), ()

    return _rows("hgrn2_out_fwd", fn, s, tm, [_rb(o_raw, tm, D), _rb(proj, tm, D, 3), _cst(gnorm)], [(D, BF16)])[0]


def _hg_post_bwd(do_a, o_raw, proj, gnorm, tm=256):
    s = o_raw.shape[0]

    def fn(da, o, hg, gn):
        dos, dhgs = [], []
        dgn = jnp.zeros((1, HG_DK), F32)
        for h in range(HG_H):
            sl = slice(h * HG_DK, (h + 1) * HG_DK)
            oh, gh, dh = o[:, sl], hg[:, sl], da[:, sl]
            r = lax.rsqrt(jnp.mean(oh * oh, axis=-1, keepdims=True) + EPS)
            xhat = oh * r
            sg = _sigmoid(gh)
            dy = dh * (gh * sg)
            dhgs.append(dh * xhat * gn * sg * (1.0 + gh * (1.0 - sg)))
            dgn = dgn + jnp.sum(dy * xhat, axis=0, keepdims=True)
            dxh = dy * gn
            dos.append(r * (dxh - xhat * jnp.mean(dxh * xhat, axis=-1, keepdims=True)))
        return (jnp.concatenate(dos, axis=1), jnp.concatenate(dhgs, axis=1)), (dgn,)

    ins = [_rb(do_a, tm, D), _rb(o_raw, tm, D), _rb(proj, tm, D, 3), _cst(gnorm)]
    return _rows("hgrn2_out_bwd", fn, s, tm, ins, [(D, F32), (D, BF16)], [(1, HG_DK)])


def _log_sigmoid(z):
    return jnp.minimum(z, 0.0) - jnp.log(1.0 + jnp.exp(-jnp.abs(z)))


def _fox_gate_bwd(dct, pff, bias, tm=256):
    s = pff.shape[0]
    nb = s // tm

    def body(d_ref, p_ref, b_ref, dff_ref, db_ref, carry):
        @pl.when(pl.program_id(0) == 0)
        def _():
            carry[...] = jnp.zeros_like(carry)
            db_ref[...] = jnp.zeros_like(db_ref)

        dc = d_ref[...].T
        dlf = _split_dot(_tri(tm, False), dc, 3) + carry[0:1]
        carry[...] = jnp.broadcast_to(dlf[0:1], carry.shape)
        dff = dlf * _sigmoid(-(p_ref[...] + b_ref[...]))
        dff_ref[...] = dff
        db_ref[...] += jnp.sum(dff, axis=0, keepdims=True)

    return pl.pallas_call(
        body, name="fox_gate_bwd", grid=(nb,),
        in_specs=[pl.BlockSpec((LANES, tm), lambda i: (0, nb - 1 - i)),
                  pl.BlockSpec((tm, LANES), lambda i: (nb - 1 - i, 0)), pl.BlockSpec((1, LANES), lambda i: (0, 0))],
        out_specs=[pl.BlockSpec((tm, LANES), lambda i: (nb - 1 - i, 0)), pl.BlockSpec((1, LANES), lambda i: (0, 0))],
        out_shape=[jax.ShapeDtypeStruct((s, LANES), F32), jax.ShapeDtypeStruct((1, LANES), F32)],
        scratch_shapes=[pltpu.VMEM((8, LANES), F32)],
        compiler_params=_cparams(("arbitrary",)),
    )(dct, pff, bias)


def _diag_mask(t):
    r = lax.broadcasted_iota(jnp.int32, (t, t), 0)
    c = lax.broadcasted_iota(jnp.int32, (t, t), 1)
    return r >= c


AUX_ONES = 6


def _pieces(x):
    h = x.astype(BF16)
    r = x - h.astype(F32)
    m = r.astype(BF16)
    return h, m, (r - m.astype(F32)).astype(BF16)


def _lane_put(lane, cols, base):
    out = None
    for i, col in enumerate(cols):
        term = jnp.where(lane == base + i, col.astype(F32), 0.0)
        out = term if out is None else out + term
    return out


def _fox_prep2(proj, pff, bias, tm=256):
    s = pff.shape[0]

    def body(q_ref, k_ref, v_ref, p_ref, b_ref, qb_ref, kb_ref, vb_ref, ka_ref, carry):
        @pl.when(pl.program_id(0) == 0)
        def _():
            carry[...] = jnp.zeros_like(carry)

        qb_ref[...] = (q_ref[...] * 0.125).astype(BF16)
        kb_ref[...] = k_ref[...].astype(BF16)
        vb_ref[...] = v_ref[...].astype(BF16)
        lf = _log_sigmoid(p_ref[...] + b_ref[...])
        c = _split_dot(_tri(tm, True), lf, 3) + carry[0:1]
        carry[...] = jnp.broadcast_to(c[tm - 1:tm], carry.shape)
        lane = lax.broadcasted_iota(jnp.int32, (tm, LANES), 1)
        ones = jnp.where((lane >= AUX_ONES) & (lane < AUX_ONES + 6), 1.0, 0.0)
        for p in range(FOX_H // 2):
            aux = ones
            for z in range(2):
                col = jnp.sum(jnp.where(lane == 2 * p + z, c, 0.0), axis=1, keepdims=True)
                aux = aux + _lane_put(lane, _pieces(-col), 3 * z)
            ka_ref[:, p * LANES:(p + 1) * LANES] = aux.astype(BF16)

    row = lambda cb: pl.BlockSpec((tm, D), lambda i: (i, cb))
    return pl.pallas_call(
        body, name="fox_prep", grid=(s // tm,),
        in_specs=[row(4), row(5), row(6), pl.BlockSpec((tm, LANES), lambda i: (i, 0)),
                  pl.BlockSpec((1, LANES), lambda i: (0, 0))],
        out_specs=[row(0)] * 4, out_shape=[jax.ShapeDtypeStruct((s, D), BF16)] * 4,
        scratch_shapes=[pltpu.VMEM((8, LANES), F32)],
        compiler_params=_cparams(("arbitrary",)),
    )(proj, proj, proj, pff, bias)


def _fox_fwd2(qb, kb, vb, ka):
    s = qb.shape[0]
    t = min(FOX_T, s)
    nq = s // t

    def body(q_ref, k_ref, v_ref, ka_ref, o_ref, la_ref):
        i = pl.program_id(1)
        lane = lax.broadcasted_iota(jnp.int32, (t, LANES), 1)
        in_a = lane < FOX_D
        q = q_ref[...]
        zero = jnp.zeros_like(q)
        qh = [jnp.where(in_a, q, zero), jnp.where(in_a, zero, q)]
        c_ones = [jnp.where((lane >= 3 * z) & (lane < 3 * z + 3), 1.0, 0.0) for z in range(2)]

        def keys(j):
            rows = pl.ds(pl.multiple_of(j * t, t), t)
            return jnp.concatenate([k_ref[rows, :], ka_ref[rows, :]], axis=1), rows

        dmask = _diag_mask(t)

        def logits(qx, kk, masked):
            e = lax.dot_general(qx, kk, (NT, ((), ())), preferred_element_type=F32)
            return jnp.where(dmask, e, -1e30) if masked else e

        qc = [jnp.concatenate([qh[z], c_ones[z].astype(BF16)], axis=1) for z in range(2)]

        def step(j, carry, masked):
            kk, rows = keys(j)
            vj = v_ref[rows, :]
            scores = [logits(qc[z], kk, masked) for z in range(2)]
            one = jnp.ones_like(vj)
            vh = [jnp.where(in_a, vj, one), jnp.where(in_a, one, vj)]
            out = []
            for z in range(2):
                m, acc = carry[z]
                m_new = jnp.maximum(m, jnp.max(scores[z], axis=1, keepdims=True))
                p = jnp.exp(scores[z] - m_new)
                out.append((m_new, jnp.exp(m - m_new) * acc + _dot(p, vh[z], NN)))
            return tuple(out)

        init = tuple((jnp.full((t, 1), -1e30, F32), jnp.zeros((t, LANES), F32)) for _ in range(2))
        (ma, acc_a), (mb, acc_b) = step(i, lax.fori_loop(0, i, lambda j, c: step(j, c, False), init), True)
        la = jnp.sum(jnp.where(lane == FOX_D, acc_a, 0.0), axis=1, keepdims=True)
        lb = jnp.sum(jnp.where(lane == 0, acc_b, 0.0), axis=1, keepdims=True)
        o_ref[...] = jnp.where(in_a, acc_a / la, acc_b / lb).astype(o_ref.dtype)
        la_ref[...] = (_lane_put(lane, _pieces(-(ma + jnp.log(la))), AUX_ONES)
                       + _lane_put(lane, _pieces(-(mb + jnp.log(lb))), AUX_ONES + 3)).astype(la_ref.dtype)

    blk = pl.BlockSpec((t, LANES), lambda p, i: (i, p))
    whole = pl.BlockSpec((s, LANES), lambda p, i: (0, p))
    return pl.pallas_call(
        body, name="fox_attn_fwd", grid=(FOX_H // 2, nq), in_specs=[blk, whole, whole, whole],
        out_specs=[blk, blk], out_shape=[jax.ShapeDtypeStruct((s, D), BF16)] * 2,
        compiler_params=_cparams(("arbitrary", "arbitrary")),
    )(qb, kb, vb, ka)


def _fox_bwd2(qb, kb, vb, ka, ob, laux, dob):
    s = qb.shape[0]
    t = min(FOX_T, s)
    nq = s // t

    def body(q_ref, k_ref, v_ref, ka_ref, o_ref, la_ref, do_ref, dq_ref, dk_ref, dv_ref, dc_ref, dkt, dvt):
        i = pl.program_id(1)

        @pl.when(i == 0)
        def _():
            dkt[...] = jnp.zeros_like(dkt)
            dvt[...] = jnp.zeros_like(dvt)
            dc_ref[...] = jnp.zeros_like(dc_ref)

        lane = lax.broadcasted_iota(jnp.int32, (t, LANES), 1)
        in_a = lane < FOX_D
        q, do, la = q_ref[...], do_ref[...], la_ref[...].astype(F32)
        zero = jnp.zeros_like(q)
        qh = [jnp.where(in_a, q, zero), jnp.where(in_a, zero, q)]
        doh = [jnp.where(in_a, do, zero), jnp.where(in_a, zero, do)]
        qt = [h.astype(F32).T.astype(BF16) for h in qh]
        dot_ = [h.astype(F32).T.astype(BF16) for h in doh]
        prod = do.astype(F32) * o_ref[...].astype(F32)
        qx, dox = [], []
        for z in range(2):
            delta = jnp.sum(jnp.where(in_a if z == 0 else ~in_a, prod, 0.0), axis=1, keepdims=True)
            c_ones = jnp.where((lane >= 3 * z) & (lane < 3 * z + 3), 1.0, 0.0)
            lse_lanes = (lane >= AUX_ONES + 3 * z) & (lane < AUX_ONES + 3 * z + 3)
            qx.append(jnp.concatenate([qh[z], (c_ones + jnp.where(lse_lanes, la, 0.0)).astype(BF16)], axis=1))
            dox.append(jnp.concatenate([doh[z], _lane_put(lane, _pieces(-delta), 3 * z).astype(BF16)], axis=1))
        v_ones = jnp.where(lane < 6, 1.0, 0.0).astype(BF16)
        dmask = _diag_mask(t)

        def step(j, carry, masked):
            rows = pl.ds(pl.multiple_of(j * t, t), t)
            kj, vj = k_ref[rows, :], v_ref[rows, :]
            kk = jnp.concatenate([kj, ka_ref[rows, :]], axis=1)
            vv = jnp.concatenate([vj, v_ones], axis=1)
            out = []
            dk_add, dv_add = None, None
            for z in range(2):
                dq, rsum = carry[z]
                e = lax.dot_general(qx[z], kk, (NT, ((), ())), preferred_element_type=F32)
                if masked:
                    e = jnp.where(dmask, e, -1e30)
                p = jnp.exp(e)
                ds = p * lax.dot_general(dox[z], vv, (NT, ((), ())), preferred_element_type=F32)
                dkz, dvz = _dot(qt[z], ds, NN), _dot(dot_[z], p, NN)
                dk_add = dkz if dk_add is None else dk_add + dkz
                dv_add = dvz if dv_add is None else dv_add + dvz
                dc_ref[0, z, j] += -jnp.sum(ds, axis=0, keepdims=True)
                out.append((dq + _dot(ds, kj, NN), rsum + jnp.sum(ds, axis=1, keepdims=True)))
            dkt[j] += dk_add
            dvt[j] += dv_add
            return tuple(out)

        init = tuple((jnp.zeros((t, LANES), F32), jnp.zeros((t, 1), F32)) for _ in range(2))
        (dq_a, rs_a), (dq_b, rs_b) = step(i, lax.fori_loop(0, i, lambda j, c: step(j, c, False), init), True)
        for z, rs in enumerate((rs_a, rs_b)):
            dc_ref[0, z, i] += jnp.transpose(jnp.broadcast_to(rs, (t, LANES)))[0:1]
        dq_ref[...] = (jnp.where(in_a, dq_a, dq_b) * 0.125).astype(dq_ref.dtype)

        @pl.when(i == nq - 1)
        def _():
            for jb in range(nq):
                dk_ref[jb * t:(jb + 1) * t, :] = dkt[jb].T.astype(dk_ref.dtype)
                dv_ref[jb * t:(jb + 1) * t, :] = dvt[jb].T.astype(dv_ref.dtype)

    blk = pl.BlockSpec((t, LANES), lambda p, i: (i, p))
    whole = pl.BlockSpec((s, LANES), lambda p, i: (0, p))
    return pl.pallas_call(
        body, name="fox_attn_bwd", grid=(FOX_H // 2, nq),
        in_specs=[blk, whole, whole, whole, blk, blk, blk],
        out_specs=[blk, whole, whole, pl.BlockSpec((1, 2, nq, 1, t), lambda p, i: (p, 0, 0, 0, 0))],
        out_shape=[jax.ShapeDtypeStruct((s, D), BF16)] * 3 + [jax.ShapeDtypeStruct((FOX_H // 2, 2, nq, 1, t), F32)],
        scratch_shapes=[pltpu.VMEM((nq, LANES, t), F32), pltpu.VMEM((nq, LANES, t), F32)],
        compiler_params=_cparams(("arbitrary", "arbitrary")),
    )(qb, kb, vb, ka, ob, laux, dob)


def _adamw(name, w, g, m, v, tm=None):
    rows, width = w.shape
    tm = rows if tm is None else tm
    c1 = 1.0 - ADAM_B1 ** ADAM_STEP
    c2 = 1.0 - ADAM_B2 ** ADAM_STEP

    def fn(wb, gb, mb, vb):
        m_new = ADAM_B1 * mb + (1.0 - ADAM_B1) * gb
        v_new = ADAM_B2 * vb + (1.0 - ADAM_B2) * (gb * gb)
        delta = -ADAM_LR * ((m_new / c1) / (jnp.sqrt(v_new / c2) + ADAM_EPS) + ADAM_WD * wb)
        return (delta, m_new, v_new), ()

    ins = [_rb(a, tm, width) for a in (w, g, m, v)]
    return _rows(name, fn, rows, tm, ins, [(width, F32)] * 3)


def _me():
    return lax.axis_index("x"), lax.axis_index("y"), lax.axis_index("c")


def _all_gather8(name, block):
    m, n = block.shape

    def body(x_ref, out_ref, send_sems, recv_sems):
        x, y, c = _me()
        me, sibling = (x, y, c), (x, y, 1 - c)
        chips = [(1 - x, y), (x, 1 - y), (1 - x, 1 - y)]

        def slot(px, py, pc):
            return out_ref.at[4 * px + 2 * py + pc]

        def copy(k, blk, to, src=None):
            return pltpu.make_async_remote_copy(
                src_ref=slot(*blk) if src is None else src, dst_ref=slot(*blk),
                send_sem=send_sems.at[k], recv_sem=recv_sems.at[k], device_id=to, device_id_type=MESH)

        first = [copy(0, me, sibling, src=x_ref)]
        first += [copy(1 + j, me, (*chip, c), src=x_ref) for j, chip in enumerate(chips)]
        for cp in first:
            cp.start()
        passed = [copy(4 + j, (*chip, c), sibling) for j, chip in enumerate(chips)]
        for j, chip in enumerate(chips):
            copy(1 + j, (*chip, c), me).wait_recv()
            passed[j].start()
        copy(0, sibling, me).wait_recv()
        for j, chip in enumerate(chips):
            copy(4 + j, (*chip, 1 - c), me).wait_recv()
        for cp in first + passed:
            cp.wait_send()

    gathered = pl.pallas_call(
        body, name=name, in_specs=[ANY], out_specs=ANY,
        out_shape=jax.ShapeDtypeStruct((8, m, n), block.dtype),
        scratch_shapes=[pltpu.SemaphoreType.DMA((7,)), pltpu.SemaphoreType.DMA((7,))],
    )(block)
    x, y, c = _me()
    return lax.dynamic_update_slice(gathered, block[None], (4 * x + 2 * y + c, 0, 0))


def _swap_halves(name, g):
    n, _, m, lanes = g.shape

    def body(g_ref, got_ref, send_sems, recv_sems):
        x, y, c = _me()
        copies = [pltpu.make_async_remote_copy(
            src_ref=g_ref.at[j, 1 - c], dst_ref=got_ref.at[j], send_sem=send_sems.at[j], recv_sem=recv_sems.at[j],
            device_id=(x, y, 1 - c), device_id_type=MESH) for j in range(n)]
        for cp in copies:
            cp.start()
        for cp in copies:
            cp.wait()

    return pl.pallas_call(
        body, name=name, in_specs=[ANY], out_specs=ANY, out_shape=jax.ShapeDtypeStruct((n, m, lanes), g.dtype),
        scratch_shapes=[pltpu.SemaphoreType.DMA((n,)), pltpu.SemaphoreType.DMA((n,))],
    )(g)


def _swap_sibling(name, mine):
    def body(m_ref, out_ref, send_sem, recv_sem):
        x, y, c = _me()
        cp = pltpu.make_async_remote_copy(src_ref=m_ref, dst_ref=out_ref, send_sem=send_sem, recv_sem=recv_sem,
                                          device_id=(x, y, 1 - c), device_id_type=MESH)
        cp.start()
        cp.wait()

    return pl.pallas_call(
        body, name=name, in_specs=[ANY], out_specs=ANY, out_shape=jax.ShapeDtypeStruct(mine.shape, mine.dtype),
        scratch_shapes=[pltpu.SemaphoreType.DMA, pltpu.SemaphoreType.DMA],
    )(mine)


def _chip_exchange(name, p):
    def body(p_ref, out_ref, send_sems, recv_sems):
        x, y, c = _me()
        my_chip = 2 * x + y
        chips = [(1 - x, y), (x, 1 - y), (1 - x, 1 - y)]
        sends = []
        for k, (px, py) in enumerate(chips):
            sends.append(pltpu.make_async_remote_copy(
                src_ref=p_ref.at[2 * px + py], dst_ref=out_ref.at[my_chip], send_sem=send_sems.at[k],
                recv_sem=recv_sems.at[k], device_id=(px, py, c), device_id_type=MESH))
        for cp in sends:
            cp.start()
        for k, (px, py) in enumerate(chips):
            pltpu.make_async_remote_copy(
                src_ref=p_ref.at[my_chip], dst_ref=out_ref.at[2 * px + py], send_sem=send_sems.at[k],
                recv_sem=recv_sems.at[k], device_id=(px, py, c), device_id_type=MESH).wait_recv()
        for cp in sends:
            cp.wait_send()

    got = pl.pallas_call(
        body, name=name, in_specs=[ANY], out_specs=ANY, out_shape=jax.ShapeDtypeStruct(p.shape, p.dtype),
        scratch_shapes=[pltpu.SemaphoreType.DMA((3,)), pltpu.SemaphoreType.DMA((3,))],
    )(p)
    x, y, _ = _me()
    my_chip = 2 * x + y
    return lax.dynamic_update_slice(got, lax.dynamic_index_in_dim(p, my_chip, axis=0, keepdims=True), (my_chip, 0, 0))


def _all_reduce_small(name, block):
    r, n = block.shape

    def body(x_ref, sum_ref, gath, send_sems, recv_sems):
        x, y, c = _me()
        me = 4 * x + 2 * y + c
        gath[me] = x_ref[...]
        sends = []
        for k in range(1, 8):
            px = x ^ ((k >> 2) & 1)
            py = y ^ ((k >> 1) & 1)
            pc = c ^ (k & 1)
            sends.append(pltpu.make_async_remote_copy(
                src_ref=x_ref, dst_ref=gath.at[me], send_sem=send_sems.at[k - 1], recv_sem=recv_sems.at[k - 1],
                device_id=(px, py, pc), device_id_type=MESH))
        for cp in sends:
            cp.start()
        for k in range(1, 8):
            peer = me ^ k
            pltpu.make_async_remote_copy(
                src_ref=x_ref, dst_ref=gath.at[peer], send_sem=send_sems.at[k - 1], recv_sem=recv_sems.at[k - 1],
                device_id=(x, y, c), device_id_type=MESH).wait_recv()
        for cp in sends:
            cp.wait_send()
        acc = gath[0]
        for d in range(1, 8):
            acc = acc + gath[d]
        sum_ref[...] = acc

    vm = pl.BlockSpec(memory_space=pltpu.VMEM)
    return pl.pallas_call(
        body, name=name, in_specs=[vm], out_specs=vm, out_shape=jax.ShapeDtypeStruct((r, n), F32),
        scratch_shapes=[pltpu.VMEM((8, r, n), F32), pltpu.SemaphoreType.DMA((7,)), pltpu.SemaphoreType.DMA((7,))],
    )(block)


def _add2(name, a, b, tm):
    rows = a.shape[0]
    return _rows(name, lambda p, q: ((p + q,), ()), rows, tm, [_rb(a, tm, LANES), _rb(b, tm, LANES)], [(LANES, BF16)])[0]


def _add4(name, p, tm):
    m = p.shape[1]
    flat = p.reshape(4 * m, LANES)
    nb = m // tm
    ins = [(flat, (tm, LANES), (lambda i, j=j: (j * nb + i, 0))) for j in range(4)]
    f32 = lambda v: v.astype(F32)
    return _rows(name, lambda a, b, c, d: ((((f32(a) + f32(b)) + f32(c)) + f32(d),), ()), m, tm, ins, [(LANES, F32)])[0]


SEG_ROWS = (D * W_IN_SHARD // LANES, 256 * D // LANES, 256 * D // LANES, 256 * D // LANES,
            D * W_UP_SHARD // LANES, W_DOWN_SHARD * D // LANES)
GRAD_ROWS = sum(SEG_ROWS)
CONVW_ROWS = 3 * W_UP_SHARD * 2 // LANES
GATHER_ROWS = 41600


def _flat(a):
    return a.reshape(-1, LANES)


def _gather_weights(w_in, w_a, w_b, w_out, w_up, w_down, conv_w):
    c = lax.axis_index("c")
    bits = lax.bitcast_convert_type(conv_w, BF16)
    pieces = [_flat(t.astype(BF16)) for t in (w_in, w_a, w_b, w_out, w_up, w_down)] + [_flat(bits)]
    pad = GATHER_ROWS - GRAD_ROWS - CONVW_ROWS
    shard = jnp.concatenate(pieces + [jnp.zeros((pad, LANES), BF16)], axis=0)
    half = GATHER_ROWS // 2
    mine = lax.dynamic_slice_in_dim(shard, c * half, half, axis=0)
    full = _all_gather8("all_gather_weights", mine).reshape(N_CHIP, GATHER_ROWS, LANES)
    offs = [0]
    for r in SEG_ROWS:
        offs.append(offs[-1] + r)
    seg = lambda i: full[:, offs[i]:offs[i + 1]]
    wi = seg(0).reshape(N_CHIP, D, W_IN_SHARD).transpose(1, 0, 2).reshape(D, N_CHIP * W_IN_SHARD)
    w_main = jnp.concatenate([wi[:, :FF_COL], wi[:, FF_COL + FOX_H:]], axis=1)
    w_ff = jnp.pad(wi[:, FF_COL:FF_COL + FOX_H], ((0, 0), (0, LANES - FOX_H)))
    wa, wb, wo = (seg(i).reshape(D, D) for i in (1, 2, 3))
    wu = seg(4).reshape(N_CHIP, D, W_UP_SHARD).transpose(1, 0, 2).reshape(D, 2 * D_FF)
    wd = seg(5).reshape(D_FF, D)
    cw_bits = full[:, GRAD_ROWS:GRAD_ROWS + CONVW_ROWS].reshape(N_CHIP, 3, W_UP_SHARD, 2)
    cw = lax.bitcast_convert_type(cw_bits, F32).transpose(1, 0, 2).reshape(3, 2 * D_FF)
    return w_main, w_ff, wa, wb, wo, wu, wd, cw


def _reduce_scatter_grads(d_main, d_ff, d_a, d_b, d_o, d_u, d_d):
    c = lax.axis_index("c")
    d_in = jnp.concatenate(d_main[:7] + [d_ff[:, :FOX_H]] + d_main[7:], axis=1)
    per_chip = [
        d_in.reshape(D, N_CHIP, W_IN_SHARD).transpose(1, 0, 2).reshape(N_CHIP, -1, LANES),
        d_a.reshape(N_CHIP, -1, LANES), d_b.reshape(N_CHIP, -1, LANES), d_o.reshape(N_CHIP, -1, LANES),
        d_u.reshape(D, N_CHIP, W_UP_SHARD).transpose(1, 0, 2).reshape(N_CHIP, -1, LANES),
        d_d.reshape(N_CHIP, -1, LANES),
        jnp.zeros((N_CHIP, GATHER_ROWS - GRAD_ROWS, LANES), F32),
    ]
    half = GATHER_ROWS // 2
    g = jnp.concatenate(per_chip, axis=1).reshape(N_CHIP, 2, half, LANES)
    from_sibling = _swap_halves("grad_swap_halves", g)
    mine = lax.dynamic_index_in_dim(g, c, axis=1, keepdims=False)
    tm = half // 5
    chip_sum = _add2("grad_chip_sum", mine.reshape(-1, LANES), from_sibling.reshape(-1, LANES), tm)
    pieces = _chip_exchange("grad_chip_exchange", chip_sum.reshape(N_CHIP, half, LANES))
    mine_half = _add4("grad_sum_chips", pieces, tm)
    other_half = _swap_sibling("grad_share_half", mine_half)
    lo = jnp.where(c == 0, mine_half, other_half)
    hi = jnp.where(c == 0, other_half, mine_half)
    return jnp.concatenate([lo, hi], axis=0)


def _local_step(x, target, norm_mix, fox_f_bias, hg_lb_logits, hg_norm, norm_ffn, conv_b, norm_final,
                w_main, w_ff, wa, wb, wo, wu, wd, conv_w):
    s = x.shape[0]
    bias = jnp.pad(fox_f_bias, ((0, 0), (0, LANES - FOX_H)))
    conv_w8 = jnp.pad(conv_w, ((0, 5), (0, 0)))
    t = min(FOX_T, s)

    n1, n1t = _rms_fwd("norm_mix_fwd", x, norm_mix)
    proj = _mm("in_proj", n1, w_main, "nn", F32, 1024, 1024, D)
    pff = _mm("in_proj_forget", n1, w_ff, "nn", F32, 1024, LANES, D)
    qb, kb, vb, ka = _fox_prep2(proj, pff, bias)
    o_b, laux = _fox_fwd2(qb, kb, vb, ka)
    o_raw, states = _hg_fwd(proj, hg_lb_logits)
    o_a = _hg_post_fwd(o_raw, proj, hg_norm)
    pa = _mm("branch_a", o_a, wa, "nn", F32, 1024, 1024, D)
    pb = _mm("branch_b", o_b, wb, "nn", F32, 1024, 1024, D)
    merged = _merge_fwd(pa, pb, proj)
    h1 = _mm("out_proj", merged, wo, "nn", F32, 1024, 1024, D, res=x)
    n2, n2t = _rms_fwd("norm_ffn_fwd", h1, norm_ffn)
    u = _mm("ffn_up", n2, wu, "nn", F32, 1024, W_UP_SHARD, D)
    act, gelu_gate, dact_dgate = _convglu_fwd(u, conv_w8, conv_b)
    h2 = _mm("ffn_down", act, wd, "nn", F32, 512, 1024, D_FF, res=h1)
    (dh2,), (d_norm_final, loss_row) = _final(h2, target, norm_final)

    dact = _mm("ffn_down_dx", dh2, wd, "nt", BF16, 1024, D_FF, D)
    d_wd = _mm("ffn_down_dw", act, dh2, "tn", F32, D_FF // 2, 1024, DW_TK // 2)
    (du,), (d_conv_w8, d_conv_b) = _convglu_bwd(u, dact, gelu_gate, dact_dgate, conv_w8)
    dn2 = _mm("ffn_up_dx", du, wu, "nt", F32, 1024, 1024, W_UP_SHARD)
    d_wu = _mm("ffn_up_dw", n2t, du, "nn", F32, 1024, W_UP_SHARD, DW_TK)
    (dh1,), (d_norm_ffn,) = _rms_bwd("norm_ffn_bwd", h1, norm_ffn, [dn2], dh2)

    dmerged = _mm("out_proj_dx", dh1, wo, "nt", F32, 1024, 1024, D)
    d_wo = _mm("out_proj_dw", merged, dh1, "tn", F32, 1024, 1024, DW_TK)
    dpa, dpb, dga, dgb = _merge_bwd(dmerged, pa, pb, proj)
    do_a = _mm("branch_a_dx", dpa, wa, "nt", F32, 1024, 1024, D)
    do_b = _mm("branch_b_dx", dpb, wb, "nt", BF16, 1024, 1024, D)
    d_wa = _mm("branch_a_dw", o_a, dpa, "tn", F32, 1024, 1024, DW_TK)
    d_wb = _mm("branch_b_dw", o_b, dpb, "tn", F32, 1024, 1024, DW_TK)

    (do_raw, dhg), (d_hg_norm,) = _hg_post_bwd(do_a, o_raw, proj, hg_norm)
    dhq, dhf, dhi, d_lb_logits = _hg_bwd(proj, hg_lb_logits, states, do_raw)

    dfq, dfk, dfv, dcrow = _fox_bwd2(qb, kb, vb, ka, o_b, laux, do_b)
    dct = jnp.pad(dcrow.reshape(FOX_H, s), ((0, LANES - FOX_H), (0, 0)))
    dff, d_bias = _fox_gate_bwd(dct, pff, bias)

    pieces = [dhq, dhf, dhi, dhg, dfq, dfk, dfv, dga, dgb]
    dn1 = _mm_sum_nt("in_proj_dx", pieces, w_main, (dff, w_ff), 1024, 1024)
    d_w_main = [_mm("in_proj_dw_%d" % i, n1t, p, "nn", F32, 1024, 1024, DW_TK) for i, p in enumerate(pieces)]
    d_w_ff = _mm("in_proj_forget_dw", n1t, dff, "nn", F32, 1024, LANES, DW_TK)
    (dx,), (d_norm_mix,) = _rms_bwd("norm_mix_bwd", x, norm_mix, [dn1], dh1)

    small = dict(norm_mix=d_norm_mix, fox_f_bias=d_bias[:, :FOX_H], hg_lb_logits=d_lb_logits, hg_norm=d_hg_norm,
                 norm_ffn=d_norm_ffn, conv_b=d_conv_b, norm_final=d_norm_final, conv_w=d_conv_w8[:3], loss=loss_row)
    big = (d_w_main, d_w_ff, d_wa, d_wb, d_wo, d_wu, d_wd)
    return dx, small, big


SMALL_KEYS = ("norm_mix", "fox_f_bias", "hg_lb_logits", "hg_norm", "norm_ffn", "conv_b", "norm_final")


def _pack_small(parts):
    rows, layout = [], []
    for key, arr in parts:
        flat = arr.reshape(-1)
        n = flat.shape[0]
        nr = -(-n // LANES)
        rows.append(jnp.pad(flat, (0, nr * LANES - n)).reshape(nr, LANES))
        layout.append((key, arr.shape, n, nr))
    packed = jnp.concatenate(rows, axis=0)
    pad = -packed.shape[0] % 8
    return jnp.pad(packed, ((0, pad), (0, 0))), layout


def _unpack_small(packed, layout):
    out, r0 = {}, 0
    for key, shape, n, nr in layout:
        out[key] = packed[r0:r0 + nr].reshape(-1)[:n].reshape(shape)
        r0 += nr
    return out


def kernel(x, norm_mix, w_in, fox_f_bias, hg_lb_logits, hg_norm, w_branch_a, w_branch_b, w_out, norm_ffn, w_up, conv_w, conv_b, w_down, norm_final, loss_target, m_norm_mix, m_w_in, m_fox_f_bias, m_hg_lb_logits, m_hg_norm, m_w_branch_a, m_w_branch_b, m_w_out, m_norm_ffn, m_w_up, m_conv_w, m_conv_b, m_w_down, m_norm_final, v_norm_mix, v_w_in, v_fox_f_bias, v_hg_lb_logits, v_hg_norm, v_w_branch_a, v_w_branch_b, v_w_out, v_norm_ffn, v_w_up, v_conv_w, v_conv_b, v_w_down, v_norm_final):
    chip = 2 * lax.axis_index("x") + lax.axis_index("y")
    w_main, w_ff, wa, wb, wo, wu, wd, cw = _gather_weights(
        w_in[0], w_branch_a[0], w_branch_b[0], w_out[0], w_up[0], w_down[0], conv_w[0])
    dx, small, big = _local_step(
        x[0], loss_target[0], norm_mix, fox_f_bias, hg_lb_logits, hg_norm, norm_ffn, conv_b,
        norm_final.reshape(1, D), w_main, w_ff, wa, wb, wo, wu, wd, cw)

    packed, layout = _pack_small([(k, small[k]) for k in SMALL_KEYS + ("conv_w", "loss")])
    red = _unpack_small(_all_reduce_small("all_reduce_small", packed), layout)
    loss = red["loss"][0, 0]
    g_conv_w = lax.dynamic_slice_in_dim(red["conv_w"], chip * W_UP_SHARD, W_UP_SHARD, axis=1)

    gflat = _reduce_scatter_grads(*big)
    offs = [0]
    for r in SEG_ROWS:
        offs.append(offs[-1] + r)
    shapes = [(D, W_IN_SHARD), (256, D), (256, D), (256, D), (D, W_UP_SHARD), (W_DOWN_SHARD, D)]
    g_big = [gflat[offs[i]:offs[i + 1]].reshape(shapes[i]) for i in range(6)]

    names = ["norm_mix", "w_in", "fox_f_bias", "hg_lb_logits", "hg_norm", "w_branch_a", "w_branch_b", "w_out",
             "norm_ffn", "w_up", "conv_w", "conv_b", "w_down", "norm_final"]
    weights = dict(norm_mix=norm_mix, w_in=w_in, fox_f_bias=fox_f_bias, hg_lb_logits=hg_lb_logits, hg_norm=hg_norm,
                   w_branch_a=w_branch_a, w_branch_b=w_branch_b, w_out=w_out, norm_ffn=norm_ffn, w_up=w_up,
                   conv_w=conv_w, conv_b=conv_b, w_down=w_down, norm_final=norm_final)
    ms = dict(norm_mix=m_norm_mix, w_in=m_w_in, fox_f_bias=m_fox_f_bias, hg_lb_logits=m_hg_lb_logits,
              hg_norm=m_hg_norm, w_branch_a=m_w_branch_a, w_branch_b=m_w_branch_b, w_out=m_w_out,
              norm_ffn=m_norm_ffn, w_up=m_w_up, conv_w=m_conv_w, conv_b=m_conv_b, w_down=m_w_down,
              norm_final=m_norm_final)
    vs = dict(norm_mix=v_norm_mix, w_in=v_w_in, fox_f_bias=v_fox_f_bias, hg_lb_logits=v_hg_lb_logits,
              hg_norm=v_hg_norm, w_branch_a=v_w_branch_a, w_branch_b=v_w_branch_b, w_out=v_w_out,
              norm_ffn=v_norm_ffn, w_up=v_w_up, conv_w=v_conv_w, conv_b=v_conv_b, w_down=v_w_down,
              norm_final=v_norm_final)

    grads, deltas, new_m, new_v = {}, {}, {}, {}
    big_names = ["w_in", "w_branch_a", "w_branch_b", "w_out", "w_up", "w_down"]
    for name, g2 in zip(big_names, g_big):
        shape = weights[name].shape
        rows = g2.shape[0]
        d_, m_, v_ = _adamw("adamw_" + name, weights[name][0], g2, ms[name][0], vs[name][0], tm=rows // 8)
        grads[name], deltas[name], new_m[name], new_v[name] = (a.reshape(shape) for a in (g2, d_, m_, v_))
    shape = conv_w.shape
    d_, m_, v_ = _adamw("adamw_conv_w", conv_w[0], g_conv_w, m_conv_w[0], v_conv_w[0])
    grads["conv_w"], deltas["conv_w"], new_m["conv_w"], new_v["conv_w"] = (
        a.reshape(shape) for a in (g_conv_w, d_, m_, v_))
    gs = {k: red[k].reshape(weights[k].shape) for k in SMALL_KEYS}
    pw, lay = _pack_small([(k, weights[k]) for k in SMALL_KEYS])
    pg, _ = _pack_small([(k, gs[k]) for k in SMALL_KEYS])
    pm, _ = _pack_small([(k, ms[k]) for k in SMALL_KEYS])
    pv, _ = _pack_small([(k, vs[k]) for k in SMALL_KEYS])
    d_, m_, v_ = (_unpack_small(a, lay) for a in _adamw("adamw_small", pw, pg, pm, pv))
    for k in SMALL_KEYS:
        grads[k], deltas[k], new_m[k], new_v[k] = gs[k], d_[k], m_[k], v_[k]

    return (loss, dx[None], *[grads[n] for n in names], *[deltas[n] for n in names],
            *[new_m[n] for n in names], *[new_v[n] for n in names])
```

```python
import functools

import jax
import jax.numpy as jnp
from jax import lax
from jax.experimental import pallas as pl
from jax.experimental.pallas import tpu as pltpu

F32 = jnp.float32
BF16 = jnp.bfloat16

D = 1024
HG_H, HG_DK = 8, 128
FOX_H, FOX_D = 16, 64
D_FF = 2816
EPS = 1e-6
N_CHIP = 4
LANES = 128
W_IN_SHARD = 2308
W_UP_SHARD = 1408
W_DOWN_SHARD = 704
FF_COL = 7168
ADAM_LR, ADAM_B1, ADAM_B2, ADAM_EPS, ADAM_WD, ADAM_STEP = 0.001, 0.9, 0.999, 1e-08, 0.01, 10

HG_C = 16
HG_T = 512
HG_UNROLL = 16
HG_UNROLL_BWD = 4
FOX_T = 512
DW_TK = 2048
VMEM_LIMIT = 56 * 1024 * 1024
MESH = pl.DeviceIdType.MESH
ANY = pl.BlockSpec(memory_space=pl.ANY)


def _cparams(sem):
    return pltpu.CompilerParams(dimension_semantics=sem, vmem_limit_bytes=VMEM_LIMIT)


def _sigmoid(x):
    return 1.0 / (1.0 + jnp.exp(-x))


def _dot(a, b, dims):
    return lax.dot_general(a.astype(BF16), b.astype(BF16), (dims, ((), ())), preferred_element_type=F32)


NN = ((1,), (0,))
NT = ((1,), (1,))
TN = ((0,), (0,))


def _split_dot(tri, x, parts, dims=NN):
    acc = None
    r = x
    for _ in range(parts):
        p = r.astype(BF16)
        t = lax.dot_general(tri, p, (dims, ((), ())), preferred_element_type=F32)
        acc = t if acc is None else acc + t
        r = r - p.astype(F32)
    return acc


def _rb(arr, tm, width, cb=0):
    return (arr, (tm, width), lambda i: (i, cb))


def _cst(arr):
    return (arr, arr.shape, lambda i: (0,) * arr.ndim)


def _rows(name, fn, n_rows, tm, ins, outs, accs=(), reverse=False):
    n_in, n_out, n_acc = len(ins), len(outs), len(accs)
    nb = n_rows // tm

    def body(*refs):
        vals = [r[...] for r in refs[:n_in]]
        o, a = fn(*vals)
        for r, v in zip(refs[n_in:n_in + n_out], o):
            r[...] = v.astype(r.dtype)
        if n_acc:
            acc_refs = refs[n_in + n_out:]

            @pl.when(pl.program_id(0) == 0)
            def _():
                for r in acc_refs:
                    r[...] = jnp.zeros_like(r)

            for r, v in zip(acc_refs, a):
                r[...] += v

    if reverse:
        rowmap = lambda i: (nb - 1 - i, 0)
    else:
        rowmap = lambda i: (i, 0)
    in_specs = [pl.BlockSpec(bs, im) for (_, bs, im) in ins]
    out_specs = [pl.BlockSpec((tm, w), rowmap) for (w, _) in outs]
    out_specs += [pl.BlockSpec((r, w), lambda i: (0, 0)) for (r, w) in accs]
    out_shape = [jax.ShapeDtypeStruct((n_rows, w), dt) for (w, dt) in outs]
    out_shape += [jax.ShapeDtypeStruct((r, w), F32) for (r, w) in accs]
    res = pl.pallas_call(
        body, name=name, grid=(nb,), in_specs=in_specs, out_specs=out_specs, out_shape=out_shape,
        compiler_params=_cparams(("arbitrary",)),
    )(*[a for a, _, _ in ins])
    return (res[:n_out], res[n_out:]) if n_acc else res


def _mm(name, a, b, mode, out_dtype, tm, tn, tk, res=None):
    if mode == "nn":
        (m, k), n = a.shape, b.shape[1]
    elif mode == "nt":
        (m, k), n = a.shape, b.shape[0]
    else:
        (k, m), n = a.shape, b.shape[1]
    tm, tn, tk = min(tm, m), min(tn, n), min(tk, k)
    assert m % tm == 0 and n % tn == 0 and k % tk == 0, (name, m, n, k, tm, tn, tk)
    if mode == "nn":
        a_spec = pl.BlockSpec((tm, tk), lambda i, j, kk: (i, kk))
        b_spec = pl.BlockSpec((tk, tn), lambda i, j, kk: (kk, j))
        dims = NN
    elif mode == "nt":
        a_spec = pl.BlockSpec((tm, tk), lambda i, j, kk: (i, kk))
        b_spec = pl.BlockSpec((tn, tk), lambda i, j, kk: (j, kk))
        dims = NT
    else:
        a_spec = pl.BlockSpec((tk, tm), lambda i, j, kk: (kk, i))
        b_spec = pl.BlockSpec((tk, tn), lambda i, j, kk: (kk, j))
        dims = TN
    nk = k // tk
    has_res = res is not None
    acc_in_out = out_dtype == F32 and not has_res

    def body(*refs):
        a_ref, b_ref = refs[0], refs[1]
        r_ref = refs[2] if has_res else None
        o_ref = refs[3] if has_res else refs[2]
        part = _dot(a_ref[...], b_ref[...], dims)

        def finish(val):
            if has_res:
                val = val + r_ref[...]
            o_ref[...] = val.astype(o_ref.dtype)

        if nk == 1:
            finish(part)
        elif acc_in_out:
            kk = pl.program_id(2)

            @pl.when(kk == 0)
            def _():
                o_ref[...] = part

            @pl.when(kk > 0)
            def _():
                o_ref[...] += part
        else:
            acc_ref = refs[-1]
            kk = pl.program_id(2)

            @pl.when(kk == 0)
            def _():
                acc_ref[...] = part

            @pl.when(kk > 0)
            def _():
                acc_ref[...] += part

            @pl.when(kk == nk - 1)
            def _():
                finish(acc_ref[...])

    in_specs = [a_spec, b_spec]
    args = [a, b]
    if has_res:
        in_specs.append(pl.BlockSpec((tm, tn), lambda i, j, kk: (i, j)))
        args.append(res)
    return pl.pallas_call(
        body, name=name, grid=(m // tm, n // tn, nk), in_specs=in_specs,
        out_specs=pl.BlockSpec((tm, tn), lambda i, j, kk: (i, j)),
        out_shape=jax.ShapeDtypeStruct((m, n), out_dtype),
        scratch_shapes=[pltpu.VMEM((tm, tn), F32)] if nk > 1 and not acc_in_out else [],
        compiler_params=_cparams(("arbitrary", "arbitrary", "arbitrary")),
    )(*args)


def _mm_sum_nt(name, pieces, w, extra, tm, tn):
    n_p = len(pieces)
    m, k = pieces[0].shape
    n = w.shape[0]
    xa, xb = extra
    ke = xa.shape[1]
    tm, tn = min(tm, m), min(tn, n)

    def body(*refs):
        p_refs, w_ref, xa_ref, xb_ref, o_ref = refs[:n_p], refs[n_p], refs[n_p + 1], refs[n_p + 2], refs[-1]
        kk = pl.program_id(2)

        @pl.when(kk == 0)
        def _():
            o_ref[...] = _dot(p_refs[0][...], w_ref[...], NT)

        for i in range(1, n_p):
            @pl.when(kk == i)
            def _(i=i):
                o_ref[...] += _dot(p_refs[i][...], w_ref[...], NT)

        @pl.when(kk == n_p)
        def _():
            o_ref[...] += _dot(xa_ref[...], xb_ref[...], NT)

    in_specs = [pl.BlockSpec((tm, k), lambda i, j, kk: (i, 0)) for _ in range(n_p)]
    in_specs.append(pl.BlockSpec((tn, k), lambda i, j, kk: (j, jnp.minimum(kk, n_p - 1))))
    in_specs += [pl.BlockSpec((tm, ke), lambda i, j, kk: (i, 0)), pl.BlockSpec((tn, ke), lambda i, j, kk: (j, 0))]
    return pl.pallas_call(
        body, name=name, grid=(m // tm, n // tn, n_p + 1), in_specs=in_specs,
        out_specs=pl.BlockSpec((tm, tn), lambda i, j, kk: (i, j)),
        out_shape=jax.ShapeDtypeStruct((m, n), F32),
        compiler_params=_cparams(("arbitrary", "arbitrary", "arbitrary")),
    )(*pieces, w, xa, xb)


def _rms_fwd(name, x, gain, tm=256):
    s = x.shape[0]

    def body(x_ref, g_ref, y_ref, yt_ref):
        xb = x_ref[...]
        y = xb * lax.rsqrt(jnp.mean(xb * xb, axis=-1, keepdims=True) + EPS) * g_ref[...]
        y_ref[...] = y.astype(BF16)
        yt_ref[...] = y.T.astype(BF16)

    return pl.pallas_call(
        body, name=name, grid=(s // tm,),
        in_specs=[pl.BlockSpec((tm, D), lambda i: (i, 0)), pl.BlockSpec((1, D), lambda i: (0, 0))],
        out_specs=[pl.BlockSpec((tm, D), lambda i: (i, 0)), pl.BlockSpec((D, tm), lambda i: (0, i))],
        out_shape=[jax.ShapeDtypeStruct((s, D), BF16), jax.ShapeDtypeStruct((D, s), BF16)],
        compiler_params=_cparams(("arbitrary",)),
    )(x, gain)


def _rms_bwd(name, x, gain, dns, dres, tm=256):
    s = x.shape[0]
    n_dn = len(dns)

    def fn(xb, g, *rest):
        dn = rest[0]
        for t in rest[1:n_dn]:
            dn = dn + t
        r = lax.rsqrt(jnp.mean(xb * xb, axis=-1, keepdims=True) + EPS)
        xhat = xb * r
        dxh = dn * g
        dx = r * (dxh - xhat * jnp.mean(dxh * xhat, axis=-1, keepdims=True)) + rest[n_dn]
        return (dx,), (jnp.sum(dn * xhat, axis=0, keepdims=True),)

    ins = [_rb(x, tm, D), _cst(gain)] + [_rb(t, tm, D) for t in dns] + [_rb(dres, tm, D)]
    return _rows(name, fn, s, tm, ins, [(D, F32)], [(1, D)])


def _final(h2, target, gain, tm=256):
    s = h2.shape[0]

    def fn(hb, tb, g):
        r = lax.rsqrt(jnp.mean(hb * hb, axis=-1, keepdims=True) + EPS)
        xhat = hb * r
        e = xhat * g - tb
        dy = e * (1.0 / D)
        dxh = dy * g
        dh = r * (dxh - xhat * jnp.mean(dxh * xhat, axis=-1, keepdims=True))
        lrow = 0.5 * jnp.sum(jnp.sum(e * e, axis=-1, keepdims=True) * (1.0 / D), axis=0, keepdims=True)
        return (dh,), (jnp.sum(dy * xhat, axis=0, keepdims=True), jnp.broadcast_to(lrow, (1, LANES)))

    return _rows("final_norm_loss", fn, s, tm, [_rb(h2, tm, D), _rb(target, tm, D), _cst(gain)],
                 [(D, F32)], [(1, D), (1, LANES)])


def _merge_fwd(pa, pb, proj, tm=256):
    s = pa.shape[0]

    def fn(a, b, ga, gb):
        return (_sigmoid(ga) * a + _sigmoid(gb) * b,), ()

    ins = [_rb(pa, tm, D), _rb(pb, tm, D), _rb(proj, tm, D, 7), _rb(proj, tm, D, 8)]
    return _rows("merge_fwd", fn, s, tm, ins, [(D, BF16)])[0]


def _merge_bwd(dmerged, pa, pb, proj, tm=256):
    s = pa.shape[0]

    def fn(dm, a, b, ga, gb):
        sa, sb = _sigmoid(ga), _sigmoid(gb)
        return (dm * sa, dm * sb, dm * a * sa * (1.0 - sa), dm * b * sb * (1.0 - sb)), ()

    ins = [_rb(dmerged, tm, D), _rb(pa, tm, D), _rb(pb, tm, D), _rb(proj, tm, D, 7), _rb(proj, tm, D, 8)]
    return _rows("merge_bwd", fn, s, tm, ins, [(D, BF16), (D, BF16), (D, BF16), (D, BF16)])


def _gelu_parts(x):
    cdf = 0.5 * (1.0 + lax.erf(x * 0.7071067811865476))
    pdf = 0.3989422804014327 * jnp.exp(-0.5 * x * x)
    return x * cdf, cdf + x * pdf


def _conv_taps(u_ext, n_out, first):
    n = u_ext.shape[0]
    cur = u_ext[8:8 + n_out]
    m1 = pltpu.roll(u_ext, 1, 0)[8:8 + n_out]
    m2 = pltpu.roll(u_ext, 2, 0)[8:8 + n_out]
    return m2, m1, cur


def _convglu_fwd(u, conv_w8, conv_b, tm=128):
    s, w = u.shape
    tb = tm // 8

    def fn(ub, up, cw, cb):
        i = pl.program_id(0)
        up = jnp.where(i == 0, 0.0, up)
        m2, m1, cur = _conv_taps(jnp.concatenate([up, ub], axis=0), tm, None)
        acc = cb + cw[0:1] * m2 + cw[1:2] * m1 + cw[2:3] * cur
        gl, dgl = _gelu_parts(acc[:, :D_FF])
        val = acc[:, D_FF:]
        return (gl * val, gl, val * dgl), ()

    ins = [_rb(u, tm, w), (u, (8, w), lambda i: (jnp.maximum(i * tb - 1, 0), 0)), _cst(conv_w8), _cst(conv_b)]
    return _rows("convglu_fwd", fn, s, tm, ins, [(D_FF, BF16)] * 3)


def _convglu_bwd(u, dact, gl, gd, conv_w8, tm=128):
    s, w = u.shape
    tb = tm // 8
    nb = s // tm

    def fn(ub, up, db, dn, glb, gln, gdb, gdn, cw):
        i = pl.program_id(0)
        up = jnp.where(i == 0, 0.0, up)
        dn = jnp.where(i == nb - 1, 0.0, dn.astype(F32))
        ne = tm + 8
        m2, m1, cur = _conv_taps(jnp.concatenate([up, ub], axis=0), tm, None)
        ext = lambda blk, nxt: jnp.concatenate([blk.astype(F32), nxt.astype(F32)], axis=0)
        de = ext(db, dn)
        dacc = jnp.concatenate([de * ext(gdb, gdn), de * ext(glb, gln)], axis=1)
        p1 = pltpu.roll(dacc, ne - 1, 0)[:tm]
        p2 = pltpu.roll(dacc, ne - 2, 0)[:tm]
        d0 = dacc[:tm]
        du = cw[2:3] * d0 + cw[1:2] * p1 + cw[0:1] * p2
        zero5 = jnp.zeros((5, w), F32)
        dcw = jnp.concatenate([
            jnp.sum(d0 * m2, axis=0, keepdims=True), jnp.sum(d0 * m1, axis=0, keepdims=True),
            jnp.sum(d0 * cur, axis=0, keepdims=True), zero5], axis=0)
        return (du,), (dcw, jnp.sum(d0, axis=0, keepdims=True))

    nxt = lambda arr: (arr, (8, D_FF), lambda i: (jnp.minimum((i + 1) * tb, s // 8 - 1), 0))
    ins = [_rb(u, tm, w), (u, (8, w), lambda i: (jnp.maximum(i * tb - 1, 0), 0)),
           _rb(dact, tm, D_FF), nxt(dact), _rb(gl, tm, D_FF), nxt(gl), _rb(gd, tm, D_FF), nxt(gd), _cst(conv_w8)]
    return _rows("convglu_bwd", fn, s, tm, ins, [(w, BF16)], [(8, w), (1, w)])


def _chunk_scan(x, t_iota, reverse):
    k = 1
    while k < HG_C:
        if reverse:
            x = x + jnp.where(t_iota < HG_C - k, pltpu.roll(x, HG_C - k, 0), 0.0)
        else:
            x = x + jnp.where(t_iota >= k, pltpu.roll(x, k, 0), 0.0)
        k *= 2
    return x


def _hg_gates(hq, hf, lb):
    sq = _sigmoid(hq)
    q = hq * sq
    sg = _sigmoid(hf)
    f = lb + (1.0 - lb) * sg
    return q, sq, sg, f, 1.0 - f, jnp.log(f)


def _lb_of(logits):
    l0, l1 = logits[0:1], logits[1:2]
    mx = jnp.maximum(l0, l1)
    e0, e1 = jnp.exp(l0 - mx), jnp.exp(l1 - mx)
    return e0 / (e0 + e1)


def _tri(n, lower):
    r = lax.broadcasted_iota(jnp.int32, (n, n), 0)
    c = lax.broadcasted_iota(jnp.int32, (n, n), 1)
    return jnp.where((r >= c) if lower else (r <= c), 1.0, 0.0).astype(BF16)


def _hg_intra_terms(q, kk, b, t_iota):
    ws, ps = [], []
    for s in range(HG_C):
        p = jnp.where(t_iota >= s, jnp.exp(b - b[s:s + 1]), 0.0)
        ps.append(p)
        ws.append(q * kk[s:s + 1] * p)
    return jnp.concatenate(ws, axis=0), ps


def _hg_fwd(proj, lb_logits):
    s = proj.shape[0]
    nt = s // HG_T
    nc = HG_T // HG_C

    def body(q_ref, f_ref, i_ref, l_ref, o_ref, st_ref, state):
        @pl.when(pl.program_id(1) == 0)
        def _():
            state[...] = jnp.zeros_like(state)

        st_ref[0, 0] = state[...]
        lb = _lb_of(l_ref[...])
        ones = jnp.ones((HG_DK, HG_DK), BF16)
        t_iota = lax.broadcasted_iota(jnp.int32, (HG_C, HG_DK), 0)
        cc = HG_C * HG_C

        def group(gi, st):
            units = []
            for u in range(HG_UNROLL):
                r = pl.ds(pl.multiple_of((gi * HG_UNROLL + u) * HG_C, HG_C), HG_C)
                q, _, _, _, kk, g = _hg_gates(q_ref[r, :], f_ref[r, :], lb)
                b = _chunk_scan(g, t_iota, False)
                b_end = b[HG_C - 1:HG_C]
                w_all, _ = _hg_intra_terms(q, kk, b, t_iota)
                units.append((r, i_ref[r, :], q * jnp.exp(b), jnp.exp(b_end), kk * jnp.exp(b_end - b), w_all))
            a_all = _dot(jnp.concatenate([un[5] for un in units], axis=0), ones, NN)
            kvs = [_dot(v, kd, TN) for (_, v, _, _, kd, _) in units]
            sts = [st]
            for (_, _, _, dec, _, _), kv in zip(units, kvs):
                sts.append(sts[-1] * dec + kv)
            for ui, (r, v, qd, _, _, _) in enumerate(units):
                o = _dot(qd, sts[ui], NT)
                for si in range(HG_C):
                    o = o + a_all[ui * cc + si * HG_C:ui * cc + (si + 1) * HG_C] * v[si:si + 1]
                o_ref[r, :] = o
            return sts[-1]

        state[...] = lax.fori_loop(0, nc // HG_UNROLL, group, state[...])

    col = lambda off: pl.BlockSpec((HG_T, HG_DK), lambda h, t: (t, off + h))
    return pl.pallas_call(
        body, name="hgrn2_fwd", grid=(HG_H, nt),
        in_specs=[col(0), col(8), col(16), pl.BlockSpec((2, HG_DK), lambda h, t: (0, h))],
        out_specs=[pl.BlockSpec((HG_T, HG_DK), lambda h, t: (t, h)),
                   pl.BlockSpec((1, 1, HG_DK, HG_DK), lambda h, t: (h, t, 0, 0))],
        out_shape=[jax.ShapeDtypeStruct((s, D), F32), jax.ShapeDtypeStruct((HG_H, nt, HG_DK, HG_DK), F32)],
        scratch_shapes=[pltpu.VMEM((HG_DK, HG_DK), F32)],
        compiler_params=_cparams(("arbitrary", "arbitrary")),
    )(proj, proj, proj, lb_logits)


def _hg_bwd(proj, lb_logits, states, do_raw):
    s = proj.shape[0]
    nt = s // HG_T
    nc = HG_T // HG_C

    def body(q_ref, f_ref, i_ref, l_ref, st_ref, do_ref, dq_ref, df_ref, di_ref, dl_ref, st_all, adj):
        tb = pl.program_id(1)

        @pl.when(tb == 0)
        def _():
            adj[...] = jnp.zeros_like(adj)
            dl_ref[...] = jnp.zeros_like(dl_ref)

        lb = _lb_of(l_ref[...])
        ones = jnp.ones((HG_DK, HG_DK), BF16)
        t_iota = lax.broadcasted_iota(jnp.int32, (HG_C, HG_DK), 0)
        cc = HG_C * HG_C

        def fwd_group(gi, st):
            terms = []
            for u in range(HG_UNROLL):
                ci = gi * HG_UNROLL + u
                r = pl.ds(pl.multiple_of(ci * HG_C, HG_C), HG_C)
                _, _, _, _, kk, g = _hg_gates(q_ref[r, :], f_ref[r, :], lb)
                b = _chunk_scan(g, t_iota, False)
                b_end = b[HG_C - 1:HG_C]
                terms.append((ci, jnp.exp(b_end), _dot(i_ref[r, :], kk * jnp.exp(b_end - b), TN)))
            for ci, dec, kv in terms:
                st_all[ci] = st
                st = st * dec + kv
            return st

        lax.fori_loop(0, nc // HG_UNROLL, fwd_group, st_ref[0, 0])

        def bwd_group(gj, dlb):
            units = []
            for u in range(HG_UNROLL_BWD):
                ci = nc - 1 - (gj * HG_UNROLL_BWD + u)
                r = pl.ds(pl.multiple_of(ci * HG_C, HG_C), HG_C)
                hq, hf, v, do = q_ref[r, :], f_ref[r, :], i_ref[r, :], do_ref[r, :]
                q, sq, sg, f, kk, g = _hg_gates(hq, hf, lb)
                b = _chunk_scan(g, t_iota, False)
                b_end = b[HG_C - 1:HG_C]
                e_b, e_be, dec = jnp.exp(b), jnp.exp(b_end - b), jnp.exp(b_end)
                w_all, ps = _hg_intra_terms(q, kk, b, t_iota)
                x_all = jnp.concatenate([do * v[si:si + 1] for si in range(HG_C)], axis=0)
                units.append(dict(ci=ci, r=r, hq=hq, v=v, do=do, q=q, sq=sq, sg=sg, f=f, kk=kk, e_b=e_b, e_be=e_be,
                                  dec=dec, kd=kk * e_be, w=w_all, ps=ps, x=x_all))
            both = _dot(jnp.concatenate([un["w"] for un in units] + [un["x"] for un in units], axis=0), ones, NN)
            st0s = [st_all[un["ci"]] for un in units]
            st_ends = [st0 * un["dec"] + _dot(un["v"], un["kd"], TN) for un, st0 in zip(units, st0s)]
            dqks = [_dot(un["do"], un["q"] * un["e_b"], TN) for un in units]
            es = [adj[...]]
            for un, dqk in zip(units, dqks):
                es.append(es[-1] * un["dec"] + dqk)
            adj[...] = es[-1]
            for ui, un in enumerate(units):
                e, q, kk, v, do = es[ui], un["q"], un["kk"], un["v"], un["do"]
                tail = jnp.sum(e * st_ends[ui], axis=0, keepdims=True)
                dq = un["e_b"] * _dot(do, st0s[ui], NN)
                dk = un["e_be"] * _dot(v, e, NN)
                dv = _dot(un["kd"], e, NT)
                a0 = ui * cc
                d0 = (HG_UNROLL_BWD + ui) * cc
                for si in range(HG_C):
                    da = both[d0 + si * HG_C:d0 + (si + 1) * HG_C]
                    aa = both[a0 + si * HG_C:a0 + (si + 1) * HG_C]
                    dap = da * un["ps"][si]
                    dq = dq + dap * kk[si:si + 1]
                    hit = t_iota == si
                    dk = dk + jnp.where(hit, jnp.sum(dap * q, axis=0, keepdims=True), 0.0)
                    dv = dv + jnp.where(hit, jnp.sum(aa * do, axis=0, keepdims=True), 0.0)
                dg = _chunk_scan(q * dq - kk * dk, t_iota, True) + tail
                dfg = dg / un["f"] - dk
                sq, sg, hq, r = un["sq"], un["sg"], un["hq"], un["r"]
                dq_ref[r, :] = (dq * sq * (1.0 + hq * (1.0 - sq))).astype(dq_ref.dtype)
                df_ref[r, :] = (dfg * (1.0 - lb) * sg * (1.0 - sg)).astype(df_ref.dtype)
                di_ref[r, :] = dv.astype(di_ref.dtype)
                dlb = dlb + jnp.sum(dfg * (1.0 - sg), axis=0, keepdims=True)
            return dlb

        dlb = lax.fori_loop(0, nc // HG_UNROLL_BWD, bwd_group, jnp.zeros((1, HG_DK), F32))
        dl0 = dlb * lb * (1.0 - lb)
        dl_ref[...] += jnp.concatenate([dl0, -dl0], axis=0)

    col = lambda off: pl.BlockSpec((HG_T, HG_DK), lambda h, t: (nt - 1 - t, off + h))
    out_col = pl.BlockSpec((HG_T, HG_DK), lambda h, t: (nt - 1 - t, h))
    return pl.pallas_call(
        body, name="hgrn2_bwd", grid=(HG_H, nt),
        in_specs=[col(0), col(8), col(16), pl.BlockSpec((2, HG_DK), lambda h, t: (0, h)),
                  pl.BlockSpec((1, 1, HG_DK, HG_DK), lambda h, t: (h, nt - 1 - t, 0, 0)), col(0)],
        out_specs=[out_col, out_col, out_col, pl.BlockSpec((2, HG_DK), lambda h, t: (0, h))],
        out_shape=[jax.ShapeDtypeStruct((s, D), BF16)] * 3 + [jax.ShapeDtypeStruct((2, D), F32)],
        scratch_shapes=[pltpu.VMEM((nc, HG_DK, HG_DK), F32), pltpu.VMEM((HG_DK, HG_DK), F32)],
        compiler_params=_cparams(("arbitrary", "arbitrary")),
    )(proj, proj, proj, lb_logits, states, do_raw)


def _hg_post_fwd(o_raw, proj, gnorm, tm=256):
    s = o_raw.shape[0]

    def fn(o, hg, gn):
        outs = []
        for h in range(HG_H):
            sl = slice(h * HG_DK, (h + 1) * HG_DK)
            oh, gh = o[:, sl], hg[:, sl]
            r = lax.rsqrt(jnp.mean(oh * oh, axis=-1, keepdims=True) + EPS)
            outs.append(oh * r * gn * (gh * _sigmoid(gh)))
        return (jnp.concatenate(outs, axis=1),), ()

    return _rows("hgrn2_out_fwd", fn, s, tm, [_rb(o_raw, tm, D), _rb(proj, tm, D, 3), _cst(gnorm)], [(D, BF16)])[0]


def _hg_post_bwd(do_a, o_raw, proj, gnorm, tm=256):
    s = o_raw.shape[0]

    def fn(da, o, hg, gn):
        dos, dhgs = [], []
        dgn = jnp.zeros((1, HG_DK), F32)
        for h in range(HG_H):
            sl = slice(h * HG_DK, (h + 1) * HG_DK)
            oh, gh, dh = o[:, sl], hg[:, sl], da[:, sl]
            r = lax.rsqrt(jnp.mean(oh * oh, axis=-1, keepdims=True) + EPS)
            xhat = oh * r
            sg = _sigmoid(gh)
            dy = dh * (gh * sg)
            dhgs.append(dh * xhat * gn * sg * (1.0 + gh * (1.0 - sg)))
            dgn = dgn + jnp.sum(dy * xhat, axis=0, keepdims=True)
            dxh = dy * gn
            dos.append(r * (dxh - xhat * jnp.mean(dxh * xhat, axis=-1, keepdims=True)))
        return (jnp.concatenate(dos, axis=1), jnp.concatenate(dhgs, axis=1)), (dgn,)

    ins = [_rb(do_a, tm, D), _rb(o_raw, tm, D), _rb(proj, tm, D, 3), _cst(gnorm)]
    return _rows("hgrn2_out_bwd", fn, s, tm, ins, [(D, F32), (D, BF16)], [(1, HG_DK)])


def _log_sigmoid(z):
    return jnp.minimum(z, 0.0) - jnp.log(1.0 + jnp.exp(-jnp.abs(z)))


def _fox_gate_bwd(dct, pff, bias, tm=256):
    s = pff.shape[0]
    nb = s // tm

    def body(d_ref, p_ref, b_ref, dff_ref, db_ref, carry):
        @pl.when(pl.program_id(0) == 0)
        def _():
            carry[...] = jnp.zeros_like(carry)
            db_ref[...] = jnp.zeros_like(db_ref)

        dc = d_ref[...].T
        dlf = _split_dot(_tri(tm, False), dc, 3) + carry[0:1]
        carry[...] = jnp.broadcast_to(dlf[0:1], carry.shape)
        dff = dlf * _sigmoid(-(p_ref[...] + b_ref[...]))
        dff_ref[...] = dff
        db_ref[...] += jnp.sum(dff, axis=0, keepdims=True)

    return pl.pallas_call(
        body, name="fox_gate_bwd", grid=(nb,),
        in_specs=[pl.BlockSpec((LANES, tm), lambda i: (0, nb - 1 - i)),
                  pl.BlockSpec((tm, LANES), lambda i: (nb - 1 - i, 0)), pl.BlockSpec((1, LANES), lambda i: (0, 0))],
        out_specs=[pl.BlockSpec((tm, LANES), lambda i: (nb - 1 - i, 0)), pl.BlockSpec((1, LANES), lambda i: (0, 0))],
        out_shape=[jax.ShapeDtypeStruct((s, LANES), F32), jax.ShapeDtypeStruct((1, LANES), F32)],
        scratch_shapes=[pltpu.VMEM((8, LANES), F32)],
        compiler_params=_cparams(("arbitrary",)),
    )(dct, pff, bias)


def _diag_mask(t):
    r = lax.broadcasted_iota(jnp.int32, (t, t), 0)
    c = lax.broadcasted_iota(jnp.int32, (t, t), 1)
    return r >= c


AUX_ONES = 6


def _pieces(x):
    h = x.astype(BF16)
    r = x - h.astype(F32)
    m = r.astype(BF16)
    return h, m, (r - m.astype(F32)).astype(BF16)


def _lane_put(lane, cols, base):
    out = None
    for i, col in enumerate(cols):
        term = jnp.where(lane == base + i, col.astype(F32), 0.0)
        out = term if out is None else out + term
    return out


def _fox_prep2(proj, pff, bias, tm=256):
    s = pff.shape[0]

    def body(q_ref, k_ref, v_ref, p_ref, b_ref, qb_ref, kb_ref, vb_ref, ka_ref, carry):
        @pl.when(pl.program_id(0) == 0)
        def _():
            carry[...] = jnp.zeros_like(carry)

        qb_ref[...] = (q_ref[...] * 0.125).astype(BF16)
        kb_ref[...] = k_ref[...].astype(BF16)
        vb_ref[...] = v_ref[...].astype(BF16)
        lf = _log_sigmoid(p_ref[...] + b_ref[...])
        c = _split_dot(_tri(tm, True), lf, 3) + carry[0:1]
        carry[...] = jnp.broadcast_to(c[tm - 1:tm], carry.shape)
        lane = lax.broadcasted_iota(jnp.int32, (tm, LANES), 1)
        ones = jnp.where((lane >= AUX_ONES) & (lane < AUX_ONES + 6), 1.0, 0.0)
        for p in range(FOX_H // 2):
            aux = ones
            for z in range(2):
                col = jnp.sum(jnp.where(lane == 2 * p + z, c, 0.0), axis=1, keepdims=True)
                aux = aux + _lane_put(lane, _pieces(-col), 3 * z)
            ka_ref[:, p * LANES:(p + 1) * LANES] = aux.astype(BF16)

    row = lambda cb: pl.BlockSpec((tm, D), lambda i: (i, cb))
    return pl.pallas_call(
        body, name="fox_prep", grid=(s // tm,),
        in_specs=[row(4), row(5), row(6), pl.BlockSpec((tm, LANES), lambda i: (i, 0)),
                  pl.BlockSpec((1, LANES), lambda i: (0, 0))],
        out_specs=[row(0)] * 4, out_shape=[jax.ShapeDtypeStruct((s, D), BF16)] * 4,
        scratch_shapes=[pltpu.VMEM((8, LANES), F32)],
        compiler_params=_cparams(("arbitrary",)),
    )(proj, proj, proj, pff, bias)


def _fox_fwd2(qb, kb, vb, ka):
    s = qb.shape[0]
    t = min(FOX_T, s)
    nq = s // t

    def body(q_ref, k_ref, v_ref, ka_ref, o_ref, la_ref):
        i = pl.program_id(1)
        lane = lax.broadcasted_iota(jnp.int32, (t, LANES), 1)
        in_a = lane < FOX_D
        q = q_ref[...]
        zero = jnp.zeros_like(q)
        qh = [jnp.where(in_a, q, zero), jnp.where(in_a, zero, q)]
        c_ones = [jnp.where((lane >= 3 * z) & (lane < 3 * z + 3), 1.0, 0.0) for z in range(2)]

        def keys(j):
            rows = pl.ds(pl.multiple_of(j * t, t), t)
            return jnp.concatenate([k_ref[rows, :], ka_ref[rows, :]], axis=1), rows

        dmask = _diag_mask(t)

        def logits(qx, kk, masked):
            e = lax.dot_general(qx, kk, (NT, ((), ())), preferred_element_type=F32)
            return jnp.where(dmask, e, -1e30) if masked else e

        qc = [jnp.concatenate([qh[z], c_ones[z].astype(BF16)], axis=1) for z in range(2)]

        def step(j, carry, masked):
            kk, rows = keys(j)
            vj = v_ref[rows, :]
            scores = [logits(qc[z], kk, masked) for z in range(2)]
            one = jnp.ones_like(vj)
            vh = [jnp.where(in_a, vj, one), jnp.where(in_a, one, vj)]
            out = []
            for z in range(2):
                m, acc = carry[z]
                m_new = jnp.maximum(m, jnp.max(scores[z], axis=1, keepdims=True))
                p = jnp.exp(scores[z] - m_new)
                out.append((m_new, jnp.exp(m - m_new) * acc + _dot(p, vh[z], NN)))
            return tuple(out)

        init = tuple((jnp.full((t, 1), -1e30, F32), jnp.zeros((t, LANES), F32)) for _ in range(2))
        (ma, acc_a), (mb, acc_b) = step(i, lax.fori_loop(0, i, lambda j, c: step(j, c, False), init), True)
        la = jnp.sum(jnp.where(lane == FOX_D, acc_a, 0.0), axis=1, keepdims=True)
        lb = jnp.sum(jnp.where(lane == 0, acc_b, 0.0), axis=1, keepdims=True)
        o_ref[...] = jnp.where(in_a, acc_a / la, acc_b / lb).astype(o_ref.dtype)
        la_ref[...] = (_lane_put(lane, _pieces(-(ma + jnp.log(la))), AUX_ONES)
                       + _lane_put(lane, _pieces(-(mb + jnp.log(lb))), AUX_ONES + 3)).astype(la_ref.dtype)

    blk = pl.BlockSpec((t, LANES), lambda p, i: (i, p))
    whole = pl.BlockSpec((s, LANES), lambda p, i: (0, p))
    return pl.pallas_call(
        body, name="fox_attn_fwd", grid=(FOX_H // 2, nq), in_specs=[blk, whole, whole, whole],
        out_specs=[blk, blk], out_shape=[jax.ShapeDtypeStruct((s, D), BF16)] * 2,
        compiler_params=_cparams(("arbitrary", "arbitrary")),
    )(qb, kb, vb, ka)


def _fox_bwd2(qb, kb, vb, ka, ob, laux, dob):
    s = qb.shape[0]
    t = min(FOX_T, s)
    nq = s // t

    def body(q_ref, k_ref, v_ref, ka_ref, o_ref, la_ref, do_ref, dq_ref, dk_ref, dv_ref, dc_ref, dkt, dvt):
        i = pl.program_id(1)

        @pl.when(i == 0)
        def _():
            dkt[...] = jnp.zeros_like(dkt)
            dvt[...] = jnp.zeros_like(dvt)
            dc_ref[...] = jnp.zeros_like(dc_ref)

        lane = lax.broadcasted_iota(jnp.int32, (t, LANES), 1)
        in_a = lane < FOX_D
        q, do, la = q_ref[...], do_ref[...], la_ref[...].astype(F32)
        zero = jnp.zeros_like(q)
        qh = [jnp.where(in_a, q, zero), jnp.where(in_a, zero, q)]
        doh = [jnp.where(in_a, do, zero), jnp.where(in_a, zero, do)]
        qt = [h.astype(F32).T.astype(BF16) for h in qh]
        dot_ = [h.astype(F32).T.astype(BF16) for h in doh]
        prod = do.astype(F32) * o_ref[...].astype(F32)
        qx, dox = [], []
        for z in range(2):
            delta = jnp.sum(jnp.where(in_a if z == 0 else ~in_a, prod, 0.0), axis=1, keepdims=True)
            c_ones = jnp.where((lane >= 3 * z) & (lane < 3 * z + 3), 1.0, 0.0)
            lse_lanes = (lane >= AUX_ONES + 3 * z) & (lane < AUX_ONES + 3 * z + 3)
            qx.append(jnp.concatenate([qh[z], (c_ones + jnp.where(lse_lanes, la, 0.0)).astype(BF16)], axis=1))
            dox.append(jnp.concatenate([doh[z], _lane_put(lane, _pieces(-delta), 3 * z).astype(BF16)], axis=1))
        v_ones = jnp.where(lane < 6, 1.0, 0.0).astype(BF16)
        dmask = _diag_mask(t)

        def step(j, carry, masked):
            rows = pl.ds(pl.multiple_of(j * t, t), t)
            kj, vj = k_ref[rows, :], v_ref[rows, :]
            kk = jnp.concatenate([kj, ka_ref[rows, :]], axis=1)
            vv = jnp.concatenate([vj, v_ones], axis=1)
            out = []
            dk_add, dv_add = None, None
            for z in range(2):
                dq, rsum = carry[z]
                e = lax.dot_general(qx[z], kk, (NT, ((), ())), preferred_element_type=F32)
                if masked:
                    e = jnp.where(dmask, e, -1e30)
                p = jnp.exp(e)
                ds = p * lax.dot_general(dox[z], vv, (NT, ((), ())), preferred_element_type=F32)
                dkz, dvz = _dot(qt[z], ds, NN), _dot(dot_[z], p, NN)
                dk_add = dkz if dk_add is None else dk_add + dkz
                dv_add = dvz if dv_add is None else dv_add + dvz
                dc_ref[0, z, j] += -jnp.sum(ds, axis=0, keepdims=True)
                out.append((dq + _dot(ds, kj, NN), rsum + jnp.sum(ds, axis=1, keepdims=True)))
            dkt[j] += dk_add
            dvt[j] += dv_add
            return tuple(out)

        init = tuple((jnp.zeros((t, LANES), F32), jnp.zeros((t, 1), F32)) for _ in range(2))
        (dq_a, rs_a), (dq_b, rs_b) = step(i, lax.fori_loop(0, i, lambda j, c: step(j, c, False), init), True)
        for z, rs in enumerate((rs_a, rs_b)):
            dc_ref[0, z, i] += jnp.transpose(jnp.broadcast_to(rs, (t, LANES)))[0:1]
        dq_ref[...] = (jnp.where(in_a, dq_a, dq_b) * 0.125).astype(dq_ref.dtype)

        @pl.when(i == nq - 1)
        def _():
            for jb in range(nq):
                dk_ref[jb * t:(jb + 1) * t, :] = dkt[jb].T.astype(dk_ref.dtype)
                dv_ref[jb * t:(jb + 1) * t, :] = dvt[jb].T.astype(dv_ref.dtype)

    blk = pl.BlockSpec((t, LANES), lambda p, i: (i, p))
    whole = pl.BlockSpec((s, LANES), lambda p, i: (0, p))
    return pl.pallas_call(
        body, name="fox_attn_bwd", grid=(FOX_H // 2, nq),
        in_specs=[blk, whole, whole, whole, blk, blk, blk],
        out_specs=[blk, whole, whole, pl.BlockSpec((1, 2, nq, 1, t), lambda p, i: (p, 0, 0, 0, 0))],
        out_shape=[jax.ShapeDtypeStruct((s, D), BF16)] * 3 + [jax.ShapeDtypeStruct((FOX_H // 2, 2, nq, 1, t), F32)],
        scratch_shapes=[pltpu.VMEM((nq, LANES, t), F32), pltpu.VMEM((nq, LANES, t), F32)],
        compiler_params=_cparams(("arbitrary", "arbitrary")),
    )(qb, kb, vb, ka, ob, laux, dob)


def _adamw(name, w, g, m, v, tm=None):
    rows, width = w.shape
    tm = rows if tm is None else tm
    c1 = 1.0 - ADAM_B1 ** ADAM_STEP
    c2 = 1.0 - ADAM_B2 ** ADAM_STEP

    def fn(wb, gb, mb, vb):
        m_new = ADAM_B1 * mb + (1.0 - ADAM_B1) * gb
        v_new = ADAM_B2 * vb + (1.0 - ADAM_B2) * (gb * gb)
        delta = -ADAM_LR * ((m_new / c1) / (jnp.sqrt(v_new / c2) + ADAM_EPS) + ADAM_WD * wb)
        return (delta, m_new, v_new), ()

    ins = [_rb(a, tm, width) for a in (w, g, m, v)]
    return _rows(name, fn, rows, tm, ins, [(width, F32)] * 3)


def _me():
    return lax.axis_index("x"), lax.axis_index("y"), lax.axis_index("c")


def _all_gather8(name, block):
    m, n = block.shape

    def body(x_ref, out_ref, send_sems, recv_sems):
        x, y, c = _me()
        me, sibling = (x, y, c), (x, y, 1 - c)
        chips = [(1 - x, y), (x, 1 - y), (1 - x, 1 - y)]

        def slot(px, py, pc):
            return out_ref.at[4 * px + 2 * py + pc]

        def copy(k, blk, to, src=None):
            return pltpu.make_async_remote_copy(
                src_ref=slot(*blk) if src is None else src, dst_ref=slot(*blk),
                send_sem=send_sems.at[k], recv_sem=recv_sems.at[k], device_id=to, device_id_type=MESH)

        first = [copy(0, me, sibling, src=x_ref)]
        first += [copy(1 + j, me, (*chip, c), src=x_ref) for j, chip in enumerate(chips)]
        for cp in first:
            cp.start()
        passed = [copy(4 + j, (*chip, c), sibling) for j, chip in enumerate(chips)]
        for j, chip in enumerate(chips):
            copy(1 + j, (*chip, c), me).wait_recv()
            passed[j].start()
        copy(0, sibling, me).wait_recv()
        back = copy(7, sibling, sibling)
        back.start()
        for j, chip in enumerate(chips):
            copy(4 + j, (*chip, 1 - c), me).wait_recv()
        copy(7, me, me).wait_recv()
        for cp in first + passed + [back]:
            cp.wait_send()

    return pl.pallas_call(
        body, name=name, in_specs=[ANY], out_specs=ANY,
        out_shape=jax.ShapeDtypeStruct((8, m, n), block.dtype),
        scratch_shapes=[pltpu.SemaphoreType.DMA((8,)), pltpu.SemaphoreType.DMA((8,))],
    )(block)


def _swap_halves(name, g):
    n, _, m, lanes = g.shape

    def body(g_ref, got_ref, send_sems, recv_sems):
        x, y, c = _me()
        copies = [pltpu.make_async_remote_copy(
            src_ref=g_ref.at[j, 1 - c], dst_ref=got_ref.at[j], send_sem=send_sems.at[j], recv_sem=recv_sems.at[j],
            device_id=(x, y, 1 - c), device_id_type=MESH) for j in range(n)]
        for cp in copies:
            cp.start()
        for cp in copies:
            cp.wait()

    return pl.pallas_call(
        body, name=name, in_specs=[ANY], out_specs=ANY, out_shape=jax.ShapeDtypeStruct((n, m, lanes), g.dtype),
        scratch_shapes=[pltpu.SemaphoreType.DMA((n,)), pltpu.SemaphoreType.DMA((n,))],
    )(g)


def _swap_sibling(name, mine):
    def body(m_ref, out_ref, send_sem, recv_sem):
        x, y, c = _me()
        cp = pltpu.make_async_remote_copy(src_ref=m_ref, dst_ref=out_ref, send_sem=send_sem, recv_sem=recv_sem,
                                          device_id=(x, y, 1 - c), device_id_type=MESH)
        cp.start()
        cp.wait()

    return pl.pallas_call(
        body, name=name, in_specs=[ANY], out_specs=ANY, out_shape=jax.ShapeDtypeStruct(mine.shape, mine.dtype),
        scratch_shapes=[pltpu.SemaphoreType.DMA, pltpu.SemaphoreType.DMA],
    )(mine)


def _chip_exchange(name, p):
    def body(p_ref, out_ref, bounce_ref, send_sems, recv_sems):
        x, y, c = _me()
        my_chip = 2 * x + y
        sibling = (x, y, 1 - c)
        chips = [(1 - x, y), (x, 1 - y), (1 - x, 1 - y)]

        def copy(k, src, dst, to):
            return pltpu.make_async_remote_copy(src_ref=src, dst_ref=dst, send_sem=send_sems.at[k],
                                                recv_sem=recv_sems.at[k], device_id=to, device_id_type=MESH)

        sends = [copy(k, p_ref.at[2 * px + py], out_ref.at[my_chip], (px, py, c)) for k, (px, py) in enumerate(chips)]
        sends.append(copy(3, p_ref.at[my_chip], bounce_ref, sibling))
        for cp in sends:
            cp.start()
        copy(3, p_ref.at[my_chip], bounce_ref, sibling).wait_recv()
        back = copy(4, bounce_ref, out_ref.at[my_chip], sibling)
        back.start()
        for k, (px, py) in enumerate(chips):
            copy(k, p_ref.at[my_chip], out_ref.at[2 * px + py], (px, py, c)).wait_recv()
        copy(4, bounce_ref, out_ref.at[my_chip], sibling).wait_recv()
        for cp in sends + [back]:
            cp.wait_send()

    piece = jax.ShapeDtypeStruct(p.shape[1:], p.dtype)
    return pl.pallas_call(
        body, name=name, in_specs=[ANY], out_specs=[ANY, ANY], out_shape=[jax.ShapeDtypeStruct(p.shape, p.dtype), piece],
        scratch_shapes=[pltpu.SemaphoreType.DMA((5,)), pltpu.SemaphoreType.DMA((5,))],
    )(p)[0]


def _all_reduce_small(name, block):
    r, n = block.shape

    def body(x_ref, sum_ref, gath, send_sems, recv_sems):
        x, y, c = _me()
        me = 4 * x + 2 * y + c
        gath[me] = x_ref[...]
        sends = []
        for k in range(1, 8):
            px = x ^ ((k >> 2) & 1)
            py = y ^ ((k >> 1) & 1)
            pc = c ^ (k & 1)
            sends.append(pltpu.make_async_remote_copy(
                src_ref=x_ref, dst_ref=gath.at[me], send_sem=send_sems.at[k - 1], recv_sem=recv_sems.at[k - 1],
                device_id=(px, py, pc), device_id_type=MESH))
        for cp in sends:
            cp.start()
        for k in range(1, 8):
            peer = me ^ k
            pltpu.make_async_remote_copy(
                src_ref=x_ref, dst_ref=gath.at[peer], send_sem=send_sems.at[k - 1], recv_sem=recv_sems.at[k - 1],
                device_id=(x, y, c), device_id_type=MESH).wait_recv()
        for cp in sends:
            cp.wait_send()
        acc = gath[0]
        for d in range(1, 8):
            acc = acc + gath[d]
        sum_ref[...] = acc

    vm = pl.BlockSpec(memory_space=pltpu.VMEM)
    return pl.pallas_call(
        body, name=name, in_specs=[vm], out_specs=vm, out_shape=jax.ShapeDtypeStruct((r, n), F32),
        scratch_shapes=[pltpu.VMEM((8, r, n), F32), pltpu.SemaphoreType.DMA((7,)), pltpu.SemaphoreType.DMA((7,))],
    )(block)


def _add2(name, a, b, tm):
    rows = a.shape[0]
    return _rows(name, lambda p, q: ((p + q,), ()), rows, tm, [_rb(a, tm, LANES), _rb(b, tm, LANES)], [(LANES, BF16)])[0]


def _add4(name, p, tm):
    m = p.shape[1]
    flat = p.reshape(4 * m, LANES)
    nb = m // tm
    ins = [(flat, (tm, LANES), (lambda i, j=j: (j * nb + i, 0))) for j in range(4)]
    f32 = lambda v: v.astype(F32)
    return _rows(name, lambda a, b, c, d: ((((f32(a) + f32(b)) + f32(c)) + f32(d),), ()), m, tm, ins, [(LANES, F32)])[0]


SEG_ROWS = (D * W_IN_SHARD // LANES, 256 * D // LANES, 256 * D // LANES, 256 * D // LANES,
            D * W_UP_SHARD // LANES, W_DOWN_SHARD * D // LANES)
GRAD_ROWS = sum(SEG_ROWS)
CONVW_ROWS = 3 * W_UP_SHARD * 2 // LANES
GATHER_ROWS = 41600


def _flat(a):
    return a.reshape(-1, LANES)


def _gather_weights(w_in, w_a, w_b, w_out, w_up, w_down, conv_w):
    c = lax.axis_index("c")
    bits = lax.bitcast_convert_type(conv_w, BF16)
    pieces = [_flat(t.astype(BF16)) for t in (w_in, w_a, w_b, w_out, w_up, w_down)] + [_flat(bits)]
    pad = GATHER_ROWS - GRAD_ROWS - CONVW_ROWS
    shard = jnp.concatenate(pieces + [jnp.zeros((pad, LANES), BF16)], axis=0)
    half = GATHER_ROWS // 2
    mine = lax.dynamic_slice_in_dim(shard, c * half, half, axis=0)
    full = _all_gather8("all_gather_weights", mine).reshape(N_CHIP, GATHER_ROWS, LANES)
    offs = [0]
    for r in SEG_ROWS:
        offs.append(offs[-1] + r)
    seg = lambda i: full[:, offs[i]:offs[i + 1]]
    wi = seg(0).reshape(N_CHIP, D, W_IN_SHARD).transpose(1, 0, 2).reshape(D, N_CHIP * W_IN_SHARD)
    w_main = jnp.concatenate([wi[:, :FF_COL], wi[:, FF_COL + FOX_H:]], axis=1)
    w_ff = jnp.pad(wi[:, FF_COL:FF_COL + FOX_H], ((0, 0), (0, LANES - FOX_H)))
    wa, wb, wo = (seg(i).reshape(D, D) for i in (1, 2, 3))
    wu = seg(4).reshape(N_CHIP, D, W_UP_SHARD).transpose(1, 0, 2).reshape(D, 2 * D_FF)
    wd = seg(5).reshape(D_FF, D)
    cw_bits = full[:, GRAD_ROWS:GRAD_ROWS + CONVW_ROWS].reshape(N_CHIP, 3, W_UP_SHARD, 2)
    cw = lax.bitcast_convert_type(cw_bits, F32).transpose(1, 0, 2).reshape(3, 2 * D_FF)
    return w_main, w_ff, wa, wb, wo, wu, wd, cw


def _reduce_scatter_grads(d_main, d_ff, d_a, d_b, d_o, d_u, d_d):
    c = lax.axis_index("c")
    d_in = jnp.concatenate(d_main[:7] + [d_ff[:, :FOX_H]] + d_main[7:], axis=1)
    per_chip = [
        d_in.reshape(D, N_CHIP, W_IN_SHARD).transpose(1, 0, 2).reshape(N_CHIP, -1, LANES),
        d_a.reshape(N_CHIP, -1, LANES), d_b.reshape(N_CHIP, -1, LANES), d_o.reshape(N_CHIP, -1, LANES),
        d_u.reshape(D, N_CHIP, W_UP_SHARD).transpose(1, 0, 2).reshape(N_CHIP, -1, LANES),
        d_d.reshape(N_CHIP, -1, LANES),
        jnp.zeros((N_CHIP, GATHER_ROWS - GRAD_ROWS, LANES), F32),
    ]
    half = GATHER_ROWS // 2
    g = jnp.concatenate(per_chip, axis=1).reshape(N_CHIP, 2, half, LANES)
    from_sibling = _swap_halves("grad_swap_halves", g)
    mine = lax.dynamic_index_in_dim(g, c, axis=1, keepdims=False)
    tm = half // 5
    chip_sum = _add2("grad_chip_sum", mine.reshape(-1, LANES), from_sibling.reshape(-1, LANES), tm)
    pieces = _chip_exchange("grad_chip_exchange", chip_sum.reshape(N_CHIP, half, LANES))
    mine_half = _add4("grad_sum_chips", pieces, tm)
    other_half = _swap_sibling("grad_share_half", mine_half)
    lo = jnp.where(c == 0, mine_half, other_half)
    hi = jnp.where(c == 0, other_half, mine_half)
    return jnp.concatenate([lo, hi], axis=0)


def _local_step(x, target, norm_mix, fox_f_bias, hg_lb_logits, hg_norm, norm_ffn, conv_b, norm_final,
                w_main, w_ff, wa, wb, wo, wu, wd, conv_w):
    s = x.shape[0]
    bias = jnp.pad(fox_f_bias, ((0, 0), (0, LANES - FOX_H)))
    conv_w8 = jnp.pad(conv_w, ((0, 5), (0, 0)))
    t = min(FOX_T, s)

    n1, n1t = _rms_fwd("norm_mix_fwd", x, norm_mix)
    proj = _mm("in_proj", n1, w_main, "nn", F32, 1024, 1024, D)
    pff = _mm("in_proj_forget", n1, w_ff, "nn", F32, 1024, LANES, D)
    qb, kb, vb, ka = _fox_prep2(proj, pff, bias)
    o_b, laux = _fox_fwd2(qb, kb, vb, ka)
    o_raw, states = _hg_fwd(proj, hg_lb_logits)
    o_a = _hg_post_fwd(o_raw, proj, hg_norm)
    pa = _mm("branch_a", o_a, wa, "nn", F32, 1024, 1024, D)
    pb = _mm("branch_b", o_b, wb, "nn", F32, 1024, 1024, D)
    merged = _merge_fwd(pa, pb, proj)
    h1 = _mm("out_proj", merged, wo, "nn", F32, 1024, 1024, D, res=x)
    n2, n2t = _rms_fwd("norm_ffn_fwd", h1, norm_ffn)
    u = _mm("ffn_up", n2, wu, "nn", F32, 1024, W_UP_SHARD, D)
    act, gelu_gate, dact_dgate = _convglu_fwd(u, conv_w8, conv_b)
    h2 = _mm("ffn_down", act, wd, "nn", F32, 512, 1024, D_FF, res=h1)
    (dh2,), (d_norm_final, loss_row) = _final(h2, target, norm_final)

    dact = _mm("ffn_down_dx", dh2, wd, "nt", BF16, 1024, D_FF, D)
    d_wd = _mm("ffn_down_dw", act, dh2, "tn", F32, D_FF // 2, 1024, DW_TK // 2)
    (du,), (d_conv_w8, d_conv_b) = _convglu_bwd(u, dact, gelu_gate, dact_dgate, conv_w8)
    dn2 = _mm("ffn_up_dx", du, wu, "nt", F32, 1024, 1024, W_UP_SHARD)
    d_wu = _mm("ffn_up_dw", n2t, du, "nn", F32, 1024, W_UP_SHARD, DW_TK)
    (dh1,), (d_norm_ffn,) = _rms_bwd("norm_ffn_bwd", h1, norm_ffn, [dn2], dh2)

    dmerged = _mm("out_proj_dx", dh1, wo, "nt", F32, 1024, 1024, D)
    d_wo = _mm("out_proj_dw", merged, dh1, "tn", F32, 1024, 1024, DW_TK)
    dpa, dpb, dga, dgb = _merge_bwd(dmerged, pa, pb, proj)
    do_a = _mm("branch_a_dx", dpa, wa, "nt", F32, 1024, 1024, D)
    do_b = _mm("branch_b_dx", dpb, wb, "nt", BF16, 1024, 1024, D)
    d_wa = _mm("branch_a_dw", o_a, dpa, "tn", F32, 1024, 1024, DW_TK)
    d_wb = _mm("branch_b_dw", o_b, dpb, "tn", F32, 1024, 1024, DW_TK)

    (do_raw, dhg), (d_hg_norm,) = _hg_post_bwd(do_a, o_raw, proj, hg_norm)
    dhq, dhf, dhi, d_lb_logits = _hg_bwd(proj, hg_lb_logits, states, do_raw)

    dfq, dfk, dfv, dcrow = _fox_bwd2(qb, kb, vb, ka, o_b, laux, do_b)
    dct = jnp.pad(dcrow.reshape(FOX_H, s), ((0, LANES - FOX_H), (0, 0)))
    dff, d_bias = _fox_gate_bwd(dct, pff, bias)

    pieces = [dhq, dhf, dhi, dhg, dfq, dfk, dfv, dga, dgb]
    dn1 = _mm_sum_nt("in_proj_dx", pieces, w_main, (dff, w_ff), 1024, 1024)
    d_w_main = [_mm("in_proj_dw_%d" % i, n1t, p, "nn", F32, 1024, 1024, DW_TK) for i, p in enumerate(pieces)]
    d_w_ff = _mm("in_proj_forget_dw", n1t, dff, "nn", F32, 1024, LANES, DW_TK)
    (dx,), (d_norm_mix,) = _rms_bwd("norm_mix_bwd", x, norm_mix, [dn1], dh1)

    small = dict(norm_mix=d_norm_mix, fox_f_bias=d_bias[:, :FOX_H], hg_lb_logits=d_lb_logits, hg_norm=d_hg_norm,
                 norm_ffn=d_norm_ffn, conv_b=d_conv_b, norm_final=d_norm_final, conv_w=d_conv_w8[:3], loss=loss_row)
    big = (d_w_main, d_w_ff, d_wa, d_wb, d_wo, d_wu, d_wd)
    return dx, small, big


SMALL_KEYS = ("norm_mix", "fox_f_bias", "hg_lb_logits", "hg_norm", "norm_ffn", "conv_b", "norm_final")


def _pack_small(parts):
    rows, layout = [], []
    for key, arr in parts:
        flat = arr.reshape(-1)
        n = flat.shape[0]
        nr = -(-n // LANES)
        rows.append(jnp.pad(flat, (0, nr * LANES - n)).reshape(nr, LANES))
        layout.append((key, arr.shape, n, nr))
    packed = jnp.concatenate(rows, axis=0)
    pad = -packed.shape[0] % 8
    return jnp.pad(packed, ((0, pad), (0, 0))), layout


def _unpack_small(packed, layout):
    out, r0 = {}, 0
    for key, shape, n, nr in layout:
        out[key] = packed[r0:r0 + nr].reshape(-1)[:n].reshape(shape)
        r0 += nr
    return out


def kernel(x, norm_mix, w_in, fox_f_bias, hg_lb_logits, hg_norm, w_branch_a, w_branch_b, w_out, norm_ffn, w_up, conv_w, conv_b, w_down, norm_final, loss_target, m_norm_mix, m_w_in, m_fox_f_bias, m_hg_lb_logits, m_hg_norm, m_w_branch_a, m_w_branch_b, m_w_out, m_norm_ffn, m_w_up, m_conv_w, m_conv_b, m_w_down, m_norm_final, v_norm_mix, v_w_in, v_fox_f_bias, v_hg_lb_logits, v_hg_norm, v_w_branch_a, v_w_branch_b, v_w_out, v_norm_ffn, v_w_up, v_conv_w, v_conv_b, v_w_down, v_norm_final):
    chip = 2 * lax.axis_index("x") + lax.axis_index("y")
    w_main, w_ff, wa, wb, wo, wu, wd, cw = _gather_weights(
        w_in[0], w_branch_a[0], w_branch_b[0], w_out[0], w_up[0], w_down[0], conv_w[0])
    dx, small, big = _local_step(
        x[0], loss_target[0], norm_mix, fox_f_bias, hg_lb_logits, hg_norm, norm_ffn, conv_b,
        norm_final.reshape(1, D), w_main, w_ff, wa, wb, wo, wu, wd, cw)

    packed, layout = _pack_small([(k, small[k]) for k in SMALL_KEYS + ("conv_w", "loss")])
    red = _unpack_small(_all_reduce_small("all_reduce_small", packed), layout)
    loss = red["loss"][0, 0]
    g_conv_w = lax.dynamic_slice_in_dim(red["conv_w"], chip * W_UP_SHARD, W_UP_SHARD, axis=1)

    gflat = _reduce_scatter_grads(*big)
    offs = [0]
    for r in SEG_ROWS:
        offs.append(offs[-1] + r)
    shapes = [(D, W_IN_SHARD), (256, D), (256, D), (256, D), (D, W_UP_SHARD), (W_DOWN_SHARD, D)]
    g_big = [gflat[offs[i]:offs[i + 1]].reshape(shapes[i]) for i in range(6)]

    names = ["norm_mix", "w_in", "fox_f_bias", "hg_lb_logits", "hg_norm", "w_branch_a", "w_branch_b", "w_out",
             "norm_ffn", "w_up", "conv_w", "conv_b", "w_down", "norm_final"]
    weights = dict(norm_mix=norm_mix, w_in=w_in, fox_f_bias=fox_f_bias, hg_lb_logits=hg_lb_logits, hg_norm=hg_norm,
                   w_branch_a=w_branch_a, w_branch_b=w_branch_b, w_out=w_out, norm_ffn=norm_ffn, w_up=w_up,
                   conv_w=conv_w, conv_b=conv_b, w_down=w_down, norm_final=norm_final)
    ms = dict(norm_mix=m_norm_mix, w_in=m_w_in, fox_f_bias=m_fox_f_bias, hg_lb_logits=m_hg_lb_logits,
              hg_norm=m_hg_norm, w_branch_a=m_w_branch_a, w_branch_b=m_w_branch_b, w_out=m_w_out,
              norm_ffn=m_norm_ffn, w_up=m_w_up, conv_w=m_conv_w, conv_b=m_conv_b, w_down=m_w_down,
              norm_final=m_norm_final)
    vs = dict(norm_mix=v_norm_mix, w_in=v_w_in, fox_f_bias=v_fox_f_bias, hg_lb_logits=v_hg_lb_logits,
              hg_norm=v_hg_norm, w_branch_a=v_w_branch_a, w_branch_b=v_w_branch_b, w_out=v_w_out,
              norm_ffn=v_norm_ffn, w_up=v_w_up, conv_w=v_conv_w, conv_b=v_conv_b, w_down=v_w_down,
              norm_final=v_norm_final)

    grads, deltas, new_m, new_v = {}, {}, {}, {}
    big_names = ["w_in", "w_branch_a", "w_branch_b", "w_out", "w_up", "w_down"]
    for name, g2 in zip(big_names, g_big):
        shape = weights[name].shape
        rows = g2.shape[0]
        d_, m_, v_ = _adamw("adamw_" + name, weights[name][0], g2, ms[name][0], vs[name][0], tm=rows // 8)
        grads[name], deltas[name], new_m[name], new_v[name] = (a.reshape(shape) for a in (g2, d_, m_, v_))
    shape = conv_w.shape
    d_, m_, v_ = _adamw("adamw_conv_w", conv_w[0], g_conv_w, m_conv_w[0], v_conv_w[0])
    grads["conv_w"], deltas["conv_w"], new_m["conv_w"], new_v["conv_w"] = (
        a.reshape(shape) for a in (g_conv_w, d_, m_, v_))
    gs = {k: red[k].reshape(weights[k].shape) for k in SMALL_KEYS}
    pw, lay = _pack_small([(k, weights[k]) for k in SMALL_KEYS])
    pg, _ = _pack_small([(k, gs[k]) for k in SMALL_KEYS])
    pm, _ = _pack_small([(k, ms[k]) for k in SMALL_KEYS])
    pv, _ = _pack_small([(k, vs[k]) for k in SMALL_KEYS])
    d_, m_, v_ = (_unpack_small(a, lay) for a in _adamw("adamw_small", pw, pg, pm, pv))
    for k in SMALL_KEYS:
        grads[k], deltas[k], new_m[k], new_v[k] = gs[k], d_[k], m_[k], v_[k]

    return (loss, dx[None], *[grads[n] for n in names], *[deltas[n] for n in names],
            *[new_m[n] for n in names], *[new_v[n] for n in names])
```

```python
import functools

import jax
import jax.numpy as jnp
from jax import lax
from jax.experimental import pallas as pl
from jax.experimental.pallas import tpu as pltpu

F32 = jnp.float32
BF16 = jnp.bfloat16

D = 1024
HG_H, HG_DK = 8, 128
FOX_H, FOX_D = 16, 64
D_FF = 2816
EPS = 1e-6
N_CHIP = 4
LANES = 128
W_IN_SHARD = 2308
W_UP_SHARD = 1408
W_DOWN_SHARD = 704
FF_COL = 7168
ADAM_LR, ADAM_B1, ADAM_B2, ADAM_EPS, ADAM_WD, ADAM_STEP = 0.001, 0.9, 0.999, 1e-08, 0.01, 10

HG_C = 16
HG_T = 512
HG_UNROLL = 16
HG_UNROLL_BWD = 4
FOX_T = 512
DW_TK = 2048
VMEM_LIMIT = 56 * 1024 * 1024
MESH = pl.DeviceIdType.MESH
ANY = pl.BlockSpec(memory_space=pl.ANY)


def _cparams(sem):
    return pltpu.CompilerParams(dimension_semantics=sem, vmem_limit_bytes=VMEM_LIMIT)


def _sigmoid(x):
    return 1.0 / (1.0 + jnp.exp(-x))


def _dot(a, b, dims):
    return lax.dot_general(a.astype(BF16), b.astype(BF16), (dims, ((), ())), preferred_element_type=F32)


NN = ((1,), (0,))
NT = ((1,), (1,))
TN = ((0,), (0,))


def _split_dot(tri, x, parts, dims=NN):
    acc = None
    r = x
    for _ in range(parts):
        p = r.astype(BF16)
        t = lax.dot_general(tri, p, (dims, ((), ())), preferred_element_type=F32)
        acc = t if acc is None else acc + t
        r = r - p.astype(F32)
    return acc


def _rb(arr, tm, width, cb=0):
    return (arr, (tm, width), lambda i: (i, cb))


def _cst(arr):
    return (arr, arr.shape, lambda i: (0,) * arr.ndim)


def _rows(name, fn, n_rows, tm, ins, outs, accs=(), reverse=False):
    n_in, n_out, n_acc = len(ins), len(outs), len(accs)
    nb = n_rows // tm

    def body(*refs):
        vals = [r[...] for r in refs[:n_in]]
        o, a = fn(*vals)
        for r, v in zip(refs[n_in:n_in + n_out], o):
            r[...] = v.astype(r.dtype)
        if n_acc:
            acc_refs = refs[n_in + n_out:]

            @pl.when(pl.program_id(0) == 0)
            def _():
                for r in acc_refs:
                    r[...] = jnp.zeros_like(r)

            for r, v in zip(acc_refs, a):
                r[...] += v

    if reverse:
        rowmap = lambda i: (nb - 1 - i, 0)
    else:
        rowmap = lambda i: (i, 0)
    in_specs = [pl.BlockSpec(bs, im) for (_, bs, im) in ins]
    out_specs = [pl.BlockSpec((tm, w), rowmap) for (w, _) in outs]
    out_specs += [pl.BlockSpec((r, w), lambda i: (0, 0)) for (r, w) in accs]
    out_shape = [jax.ShapeDtypeStruct((n_rows, w), dt) for (w, dt) in outs]
    out_shape += [jax.ShapeDtypeStruct((r, w), F32) for (r, w) in accs]
    res = pl.pallas_call(
        body, name=name, grid=(nb,), in_specs=in_specs, out_specs=out_specs, out_shape=out_shape,
        compiler_params=_cparams(("arbitrary",)),
    )(*[a for a, _, _ in ins])
    return (res[:n_out], res[n_out:]) if n_acc else res


def _mm(name, a, b, mode, out_dtype, tm, tn, tk, res=None):
    if mode == "nn":
        (m, k), n = a.shape, b.shape[1]
    elif mode == "nt":
        (m, k), n = a.shape, b.shape[0]
    else:
        (k, m), n = a.shape, b.shape[1]
    tm, tn, tk = min(tm, m), min(tn, n), min(tk, k)
    assert m % tm == 0 and n % tn == 0 and k % tk == 0, (name, m, n, k, tm, tn, tk)
    if mode == "nn":
        a_spec = pl.BlockSpec((tm, tk), lambda i, j, kk: (i, kk))
        b_spec = pl.BlockSpec((tk, tn), lambda i, j, kk: (kk, j))
        dims = NN
    elif mode == "nt":
        a_spec = pl.BlockSpec((tm, tk), lambda i, j, kk: (i, kk))
        b_spec = pl.BlockSpec((tn, tk), lambda i, j, kk: (j, kk))
        dims = NT
    else:
        a_spec = pl.BlockSpec((tk, tm), lambda i, j, kk: (kk, i))
        b_spec = pl.BlockSpec((tk, tn), lambda i, j, kk: (kk, j))
        dims = TN
    nk = k // tk
    has_res = res is not None
    acc_in_out = out_dtype == F32 and not has_res

    def body(*refs):
        a_ref, b_ref = refs[0], refs[1]
        r_ref = refs[2] if has_res else None
        o_ref = refs[3] if has_res else refs[2]
        part = _dot(a_ref[...], b_ref[...], dims)

        def finish(val):
            if has_res:
                val = val + r_ref[...]
            o_ref[...] = val.astype(o_ref.dtype)

        if nk == 1:
            finish(part)
        elif acc_in_out:
            kk = pl.program_id(2)

            @pl.when(kk == 0)
            def _():
                o_ref[...] = part

            @pl.when(kk > 0)
            def _():
                o_ref[...] += part
        else:
            acc_ref = refs[-1]
            kk = pl.program_id(2)

            @pl.when(kk == 0)
            def _():
                acc_ref[...] = part

            @pl.when(kk > 0)
            def _():
                acc_ref[...] += part

            @pl.when(kk == nk - 1)
            def _():
                finish(acc_ref[...])

    in_specs = [a_spec, b_spec]
    args = [a, b]
    if has_res:
        in_specs.append(pl.BlockSpec((tm, tn), lambda i, j, kk: (i, j)))
        args.append(res)
    return pl.pallas_call(
        body, name=name, grid=(m // tm, n // tn, nk), in_specs=in_specs,
        out_specs=pl.BlockSpec((tm, tn), lambda i, j, kk: (i, j)),
        out_shape=jax.ShapeDtypeStruct((m, n), out_dtype),
        scratch_shapes=[pltpu.VMEM((tm, tn), F32)] if nk > 1 and not acc_in_out else [],
        compiler_params=_cparams(("arbitrary", "arbitrary", "arbitrary")),
    )(*args)


def _mm_sum_nt(name, pieces, w, extra, tm, tn):
    n_p = len(pieces)
    m, k = pieces[0].shape
    n = w.shape[0]
    xa, xb = extra
    ke = xa.shape[1]
    tm, tn = min(tm, m), min(tn, n)

    def body(*refs):
        p_refs, w_ref, xa_ref, xb_ref, o_ref = refs[:n_p], refs[n_p], refs[n_p + 1], refs[n_p + 2], refs[-1]
        kk = pl.program_id(2)

        @pl.when(kk == 0)
        def _():
            o_ref[...] = _dot(p_refs[0][...], w_ref[...], NT)

        for i in range(1, n_p):
            @pl.when(kk == i)
            def _(i=i):
                o_ref[...] += _dot(p_refs[i][...], w_ref[...], NT)

        @pl.when(kk == n_p)
        def _():
            o_ref[...] += _dot(xa_ref[...], xb_ref[...], NT)

    in_specs = [pl.BlockSpec((tm, k), lambda i, j, kk: (i, 0)) for _ in range(n_p)]
    in_specs.append(pl.BlockSpec((tn, k), lambda i, j, kk: (j, jnp.minimum(kk, n_p - 1))))
    in_specs += [pl.BlockSpec((tm, ke), lambda i, j, kk: (i, 0)), pl.BlockSpec((tn, ke), lambda i, j, kk: (j, 0))]
    return pl.pallas_call(
        body, name=name, grid=(m // tm, n // tn, n_p + 1), in_specs=in_specs,
        out_specs=pl.BlockSpec((tm, tn), lambda i, j, kk: (i, j)),
        out_shape=jax.ShapeDtypeStruct((m, n), F32),
        compiler_params=_cparams(("arbitrary", "arbitrary", "arbitrary")),
    )(*pieces, w, xa, xb)


def _rms_fwd(name, x, gain, tm=256):
    s = x.shape[0]

    def body(x_ref, g_ref, y_ref, yt_ref):
        xb = x_ref[...]
        y = xb * lax.rsqrt(jnp.mean(xb * xb, axis=-1, keepdims=True) + EPS) * g_ref[...]
        y_ref[...] = y.astype(BF16)
        yt_ref[...] = y.T.astype(BF16)

    return pl.pallas_call(
        body, name=name, grid=(s // tm,),
        in_specs=[pl.BlockSpec((tm, D), lambda i: (i, 0)), pl.BlockSpec((1, D), lambda i: (0, 0))],
        out_specs=[pl.BlockSpec((tm, D), lambda i: (i, 0)), pl.BlockSpec((D, tm), lambda i: (0, i))],
        out_shape=[jax.ShapeDtypeStruct((s, D), BF16), jax.ShapeDtypeStruct((D, s), BF16)],
        compiler_params=_cparams(("arbitrary",)),
    )(x, gain)


def _rms_bwd(name, x, gain, dns, dres, tm=256):
    s = x.shape[0]
    n_dn = len(dns)

    def fn(xb, g, *rest):
        dn = rest[0]
        for t in rest[1:n_dn]:
            dn = dn + t
        r = lax.rsqrt(jnp.mean(xb * xb, axis=-1, keepdims=True) + EPS)
        xhat = xb * r
        dxh = dn * g
        dx = r * (dxh - xhat * jnp.mean(dxh * xhat, axis=-1, keepdims=True)) + rest[n_dn]
        return (dx,), (jnp.sum(dn * xhat, axis=0, keepdims=True),)

    ins = [_rb(x, tm, D), _cst(gain)] + [_rb(t, tm, D) for t in dns] + [_rb(dres, tm, D)]
    return _rows(name, fn, s, tm, ins, [(D, F32)], [(1, D)])


def _final(h2, target, gain, tm=256):
    s = h2.shape[0]

    def fn(hb, tb, g):
        r = lax.rsqrt(jnp.mean(hb * hb, axis=-1, keepdims=True) + EPS)
        xhat = hb * r
        e = xhat * g - tb
        dy = e * (1.0 / D)
        dxh = dy * g
        dh = r * (dxh - xhat * jnp.mean(dxh * xhat, axis=-1, keepdims=True))
        lrow = 0.5 * jnp.sum(jnp.sum(e * e, axis=-1, keepdims=True) * (1.0 / D), axis=0, keepdims=True)
        return (dh,), (jnp.sum(dy * xhat, axis=0, keepdims=True), jnp.broadcast_to(lrow, (1, LANES)))

    return _rows("final_norm_loss", fn, s, tm, [_rb(h2, tm, D), _rb(target, tm, D), _cst(gain)],
                 [(D, F32)], [(1, D), (1, LANES)])


def _merge_fwd(pa, pb, proj, tm=256):
    s = pa.shape[0]

    def fn(a, b, ga, gb):
        return (_sigmoid(ga) * a + _sigmoid(gb) * b,), ()

    ins = [_rb(pa, tm, D), _rb(pb, tm, D), _rb(proj, tm, D, 7), _rb(proj, tm, D, 8)]
    return _rows("merge_fwd", fn, s, tm, ins, [(D, BF16)])[0]


def _merge_bwd(dmerged, pa, pb, proj, tm=256):
    s = pa.shape[0]

    def fn(dm, a, b, ga, gb):
        sa, sb = _sigmoid(ga), _sigmoid(gb)
        return (dm * sa, dm * sb, dm * a * sa * (1.0 - sa), dm * b * sb * (1.0 - sb)), ()

    ins = [_rb(dmerged, tm, D), _rb(pa, tm, D), _rb(pb, tm, D), _rb(proj, tm, D, 7), _rb(proj, tm, D, 8)]
    return _rows("merge_bwd", fn, s, tm, ins, [(D, BF16), (D, BF16), (D, BF16), (D, BF16)])


def _gelu_parts(x):
    cdf = 0.5 * (1.0 + lax.erf(x * 0.7071067811865476))
    pdf = 0.3989422804014327 * jnp.exp(-0.5 * x * x)
    return x * cdf, cdf + x * pdf


def _conv_taps(u_ext, n_out, first):
    n = u_ext.shape[0]
    cur = u_ext[8:8 + n_out]
    m1 = pltpu.roll(u_ext, 1, 0)[8:8 + n_out]
    m2 = pltpu.roll(u_ext, 2, 0)[8:8 + n_out]
    return m2, m1, cur


def _convglu_fwd(u, conv_w8, conv_b, tm=128):
    s, w = u.shape
    tb = tm // 8

    def fn(ub, up, cw, cb):
        i = pl.program_id(0)
        up = jnp.where(i == 0, 0.0, up)
        m2, m1, cur = _conv_taps(jnp.concatenate([up, ub], axis=0), tm, None)
        acc = cb + cw[0:1] * m2 + cw[1:2] * m1 + cw[2:3] * cur
        gl, dgl = _gelu_parts(acc[:, :D_FF])
        val = acc[:, D_FF:]
        return (gl * val, gl, val * dgl), ()

    ins = [_rb(u, tm, w), (u, (8, w), lambda i: (jnp.maximum(i * tb - 1, 0), 0)), _cst(conv_w8), _cst(conv_b)]
    return _rows("convglu_fwd", fn, s, tm, ins, [(D_FF, BF16)] * 3)


def _convglu_bwd(u, dact, gl, gd, conv_w8, tm=128):
    s, w = u.shape
    tb = tm // 8
    nb = s // tm

    def fn(ub, up, db, dn, glb, gln, gdb, gdn, cw):
        i = pl.program_id(0)
        up = jnp.where(i == 0, 0.0, up)
        dn = jnp.where(i == nb - 1, 0.0, dn.astype(F32))
        ne = tm + 8
        m2, m1, cur = _conv_taps(jnp.concatenate([up, ub], axis=0), tm, None)
        ext = lambda blk, nxt: jnp.concatenate([blk.astype(F32), nxt.astype(F32)], axis=0)
        de = ext(db, dn)
        dacc = jnp.concatenate([de * ext(gdb, gdn), de * ext(glb, gln)], axis=1)
        p1 = pltpu.roll(dacc, ne - 1, 0)[:tm]
        p2 = pltpu.roll(dacc, ne - 2, 0)[:tm]
        d0 = dacc[:tm]
        du = cw[2:3] * d0 + cw[1:2] * p1 + cw[0:1] * p2
        zero5 = jnp.zeros((5, w), F32)
        dcw = jnp.concatenate([
            jnp.sum(d0 * m2, axis=0, keepdims=True), jnp.sum(d0 * m1, axis=0, keepdims=True),
            jnp.sum(d0 * cur, axis=0, keepdims=True), zero5], axis=0)
        return (du,), (dcw, jnp.sum(d0, axis=0, keepdims=True))

    nxt = lambda arr: (arr, (8, D_FF), lambda i: (jnp.minimum((i + 1) * tb, s // 8 - 1), 0))
    ins = [_rb(u, tm, w), (u, (8, w), lambda i: (jnp.maximum(i * tb - 1, 0), 0)),
           _rb(dact, tm, D_FF), nxt(dact), _rb(gl, tm, D_FF), nxt(gl), _rb(gd, tm, D_FF), nxt(gd), _cst(conv_w8)]
    return _rows("convglu_bwd", fn, s, tm, ins, [(w, BF16)], [(8, w), (1, w)])


def _chunk_scan(x, t_iota, reverse):
    k = 1
    while k < HG_C:
        if reverse:
            x = x + jnp.where(t_iota < HG_C - k, pltpu.roll(x, HG_C - k, 0), 0.0)
        else:
            x = x + jnp.where(t_iota >= k, pltpu.roll(x, k, 0), 0.0)
        k *= 2
    return x


def _hg_gates(hq, hf, lb):
    sq = _sigmoid(hq)
    q = hq * sq
    sg = _sigmoid(hf)
    f = lb + (1.0 - lb) * sg
    return q, sq, sg, f, 1.0 - f, jnp.log(f)


def _lb_of(logits):
    l0, l1 = logits[0:1], logits[1:2]
    mx = jnp.maximum(l0, l1)
    e0, e1 = jnp.exp(l0 - mx), jnp.exp(l1 - mx)
    return e0 / (e0 + e1)


def _tri(n, lower):
    r = lax.broadcasted_iota(jnp.int32, (n, n), 0)
    c = lax.broadcasted_iota(jnp.int32, (n, n), 1)
    return jnp.where((r >= c) if lower else (r <= c), 1.0, 0.0).astype(BF16)


def _hg_intra_terms(q, kk, b, t_iota):
    ws, ps = [], []
    for s in range(HG_C):
        p = jnp.where(t_iota >= s, jnp.exp(b - b[s:s + 1]), 0.0)
        ps.append(p)
        ws.append(q * kk[s:s + 1] * p)
    return jnp.concatenate(ws, axis=0), ps


def _hg_fwd(proj, lb_logits):
    s = proj.shape[0]
    nt = s // HG_T
    nc = HG_T // HG_C

    def body(q_ref, f_ref, i_ref, l_ref, o_ref, st_ref, state):
        @pl.when(pl.program_id(1) == 0)
        def _():
            state[...] = jnp.zeros_like(state)

        st_ref[0, 0] = state[...]
        lb = _lb_of(l_ref[...])
        ones = jnp.ones((HG_DK, HG_DK), BF16)
        t_iota = lax.broadcasted_iota(jnp.int32, (HG_C, HG_DK), 0)
        cc = HG_C * HG_C

        def group(gi, st):
            units = []
            for u in range(HG_UNROLL):
                r = pl.ds(pl.multiple_of((gi * HG_UNROLL + u) * HG_C, HG_C), HG_C)
                q, _, _, _, kk, g = _hg_gates(q_ref[r, :], f_ref[r, :], lb)
                b = _chunk_scan(g, t_iota, False)
                b_end = b[HG_C - 1:HG_C]
                w_all, _ = _hg_intra_terms(q, kk, b, t_iota)
                units.append((r, i_ref[r, :], q * jnp.exp(b), jnp.exp(b_end), kk * jnp.exp(b_end - b), w_all))
            a_all = _dot(jnp.concatenate([un[5] for un in units], axis=0), ones, NN)
            kvs = [_dot(v, kd, TN) for (_, v, _, _, kd, _) in units]
            sts = [st]
            for (_, _, _, dec, _, _), kv in zip(units, kvs):
                sts.append(sts[-1] * dec + kv)
            for ui, (r, v, qd, _, _, _) in enumerate(units):
                o = _dot(qd, sts[ui], NT)
                for si in range(HG_C):
                    o = o + a_all[ui * cc + si * HG_C:ui * cc + (si + 1) * HG_C] * v[si:si + 1]
                o_ref[r, :] = o
            return sts[-1]

        state[...] = lax.fori_loop(0, nc // HG_UNROLL, group, state[...])

    col = lambda off: pl.BlockSpec((HG_T, HG_DK), lambda h, t: (t, off + h))
    return pl.pallas_call(
        body, name="hgrn2_fwd", grid=(HG_H, nt),
        in_specs=[col(0), col(8), col(16), pl.BlockSpec((2, HG_DK), lambda h, t: (0, h))],
        out_specs=[pl.BlockSpec((HG_T, HG_DK), lambda h, t: (t, h)),
                   pl.BlockSpec((1, 1, HG_DK, HG_DK), lambda h, t: (h, t, 0, 0))],
        out_shape=[jax.ShapeDtypeStruct((s, D), F32), jax.ShapeDtypeStruct((HG_H, nt, HG_DK, HG_DK), F32)],
        scratch_shapes=[pltpu.VMEM((HG_DK, HG_DK), F32)],
        compiler_params=_cparams(("arbitrary", "arbitrary")),
    )(proj, proj, proj, lb_logits)


def _hg_bwd(proj, lb_logits, states, do_raw):
    s = proj.shape[0]
    nt = s // HG_T
    nc = HG_T // HG_C

    def body(q_ref, f_ref, i_ref, l_ref, st_ref, do_ref, dq_ref, df_ref, di_ref, dl_ref, st_all, adj):
        tb = pl.program_id(1)

        @pl.when(tb == 0)
        def _():
            adj[...] = jnp.zeros_like(adj)
            dl_ref[...] = jnp.zeros_like(dl_ref)

        lb = _lb_of(l_ref[...])
        ones = jnp.ones((HG_DK, HG_DK), BF16)
        t_iota = lax.broadcasted_iota(jnp.int32, (HG_C, HG_DK), 0)
        cc = HG_C * HG_C

        def fwd_group(gi, st):
            terms = []
            for u in range(HG_UNROLL):
                ci = gi * HG_UNROLL + u
                r = pl.ds(pl.multiple_of(ci * HG_C, HG_C), HG_C)
                _, _, _, _, kk, g = _hg_gates(q_ref[r, :], f_ref[r, :], lb)
                b = _chunk_scan(g, t_iota, False)
                b_end = b[HG_C - 1:HG_C]
                terms.append((ci, jnp.exp(b_end), _dot(i_ref[r, :], kk * jnp.exp(b_end - b), TN)))
            for ci, dec, kv in terms:
                st_all[ci] = st
                st = st * dec + kv
            return st

        lax.fori_loop(0, nc // HG_UNROLL, fwd_group, st_ref[0, 0])

        def bwd_group(gj, dlb):
            units = []
            for u in range(HG_UNROLL_BWD):
                ci = nc - 1 - (gj * HG_UNROLL_BWD + u)
                r = pl.ds(pl.multiple_of(ci * HG_C, HG_C), HG_C)
                hq, hf, v, do = q_ref[r, :], f_ref[r, :], i_ref[r, :], do_ref[r, :]
                q, sq, sg, f, kk, g = _hg_gates(hq, hf, lb)
                b = _chunk_scan(g, t_iota, False)
                b_end = b[HG_C - 1:HG_C]
                e_b, e_be, dec = jnp.exp(b), jnp.exp(b_end - b), jnp.exp(b_end)
                w_all, ps = _hg_intra_terms(q, kk, b, t_iota)
                x_all = jnp.concatenate([do * v[si:si + 1] for si in range(HG_C)], axis=0)
                units.append(dict(ci=ci, r=r, hq=hq, v=v, do=do, q=q, sq=sq, sg=sg, f=f, kk=kk, e_b=e_b, e_be=e_be,
                                  dec=dec, kd=kk * e_be, w=w_all, ps=ps, x=x_all))
            both = _dot(jnp.concatenate([un["w"] for un in units] + [un["x"] for un in units], axis=0), ones, NN)
            st0s = [st_all[un["ci"]] for un in units]
            st_ends = [st0 * un["dec"] + _dot(un["v"], un["kd"], TN) for un, st0 in zip(units, st0s)]
            dqks = [_dot(un["do"], un["q"] * un["e_b"], TN) for un in units]
            es = [adj[...]]
            for un, dqk in zip(units, dqks):
                es.append(es[-1] * un["dec"] + dqk)
            adj[...] = es[-1]
            for ui, un in enumerate(units):
                e, q, kk, v, do = es[ui], un["q"], un["kk"], un["v"], un["do"]
                tail = jnp.sum(e * st_ends[ui], axis=0, keepdims=True)
                dq = un["e_b"] * _dot(do, st0s[ui], NN)
                dk = un["e_be"] * _dot(v, e, NN)
                dv = _dot(un["kd"], e, NT)
                a0 = ui * cc
                d0 = (HG_UNROLL_BWD + ui) * cc
                for si in range(HG_C):
                    da = both[d0 + si * HG_C:d0 + (si + 1) * HG_C]
                    aa = both[a0 + si * HG_C:a0 + (si + 1) * HG_C]
                    dap = da * un["ps"][si]
                    dq = dq + dap * kk[si:si + 1]
                    hit = t_iota == si
                    dk = dk + jnp.where(hit, jnp.sum(dap * q, axis=0, keepdims=True), 0.0)
                    dv = dv + jnp.where(hit, jnp.sum(aa * do, axis=0, keepdims=True), 0.0)
                dg = _chunk_scan(q * dq - kk * dk, t_iota, True) + tail
                dfg = dg / un["f"] - dk
                sq, sg, hq, r = un["sq"], un["sg"], un["hq"], un["r"]
                dq_ref[r, :] = (dq * sq * (1.0 + hq * (1.0 - sq))).astype(dq_ref.dtype)
                df_ref[r, :] = (dfg * (1.0 - lb) * sg * (1.0 - sg)).astype(df_ref.dtype)
                di_ref[r, :] = dv.astype(di_ref.dtype)
                dlb = dlb + jnp.sum(dfg * (1.0 - sg), axis=0, keepdims=True)
            return dlb

        dlb = lax.fori_loop(0, nc // HG_UNROLL_BWD, bwd_group, jnp.zeros((1, HG_DK), F32))
        dl0 = dlb * lb * (1.0 - lb)
        dl_ref[...] += jnp.concatenate([dl0, -dl0], axis=0)

    col = lambda off: pl.BlockSpec((HG_T, HG_DK), lambda h, t: (nt - 1 - t, off + h))
    out_col = pl.BlockSpec((HG_T, HG_DK), lambda h, t: (nt - 1 - t, h))
    return pl.pallas_call(
        body, name="hgrn2_bwd", grid=(HG_H, nt),
        in_specs=[col(0), col(8), col(16), pl.BlockSpec((2, HG_DK), lambda h, t: (0, h)),
                  pl.BlockSpec((1, 1, HG_DK, HG_DK), lambda h, t: (h, nt - 1 - t, 0, 0)), col(0)],
        out_specs=[out_col, out_col, out_col, pl.BlockSpec((2, HG_DK), lambda h, t: (0, h))],
        out_shape=[jax.ShapeDtypeStruct((s, D), BF16)] * 3 + [jax.ShapeDtypeStruct((2, D), F32)],
        scratch_shapes=[pltpu.VMEM((nc, HG_DK, HG_DK), F32), pltpu.VMEM((HG_DK, HG_DK), F32)],
        compiler_params=_cparams(("arbitrary", "arbitrary")),
    )(proj, proj, proj, lb_logits, states, do_raw)


def _hg_post_fwd(o_raw, proj, gnorm, tm=256):
    s = o_raw.shape[0]

    def fn(o, hg, gn):
        outs = []
        for h in range(HG_H):
            sl = slice(h * HG_DK, (h + 1) * HG_DK)
            oh, gh = o[:, sl], hg[:, sl]
            r = lax.rsqrt(jnp.mean(oh * oh, axis=-1, keepdims=True) + EPS)
            outs.append(oh * r * gn * (gh * _sigmoid(gh)))
        return (jnp.concatenate(outs, axis=1),), ()

    return _rows("hgrn2_out_fwd", fn, s, tm, [_rb(o_raw, tm, D), _rb(proj, tm, D, 3), _cst(gnorm)], [(D, BF16)])[0]


def _hg_post_bwd(do_a, o_raw, proj, gnorm, tm=256):
    s = o_raw.shape[0]

    def fn(da, o, hg, gn):
        dos, dhgs = [], []
        dgn = jnp.zeros((1, HG_DK), F32)
        for h in range(HG_H):
            sl = slice(h * HG_DK, (h + 1) * HG_DK)
            oh, gh, dh = o[:, sl], hg[:, sl], da[:, sl]
            r = lax.rsqrt(jnp.mean(oh * oh, axis=-1, keepdims=True) + EPS)
            xhat = oh * r
            sg = _sigmoid(gh)
            dy = dh * (gh * sg)
            dhgs.append(dh * xhat * gn * sg * (1.0 + gh * (1.0 - sg)))
            dgn = dgn + jnp.sum(dy * xhat, axis=0, keepdims=True)
            dxh = dy * gn
            dos.append(r * (dxh - xhat * jnp.mean(dxh * xhat, axis=-1, keepdims=True)))
        return (jnp.concatenate(dos, axis=1), jnp.concatenate(dhgs, axis=1)), (dgn,)

    ins = [_rb(do_a, tm, D), _rb(o_raw, tm, D), _rb(proj, tm, D, 3), _cst(gnorm)]
    return _rows("hgrn2_out_bwd", fn, s, tm, ins, [(D, F32), (D, BF16)], [(1, HG_DK)])


def _log_sigmoid(z):
    return jnp.minimum(z, 0.0) - jnp.log(1.0 + jnp.exp(-jnp.abs(z)))


def _fox_gate_bwd(dct, pff, bias, tm=256):
    s = pff.shape[0]
    nb = s // tm

    def body(d_ref, p_ref, b_ref, dff_ref, db_ref, carry):
        @pl.when(pl.program_id(0) == 0)
        def _():
            carry[...] = jnp.zeros_like(carry)
            db_ref[...] = jnp.zeros_like(db_ref)

        dc = d_ref[...].T
        dlf = _split_dot(_tri(tm, False), dc, 3) + carry[0:1]
        carry[...] = jnp.broadcast_to(dlf[0:1], carry.shape)
        dff = dlf * _sigmoid(-(p_ref[...] + b_ref[...]))
        dff_ref[...] = dff
        db_ref[...] += jnp.sum(dff, axis=0, keepdims=True)

    return pl.pallas_call(
        body, name="fox_gate_bwd", grid=(nb,),
        in_specs=[pl.BlockSpec((LANES, tm), lambda i: (0, nb - 1 - i)),
                  pl.BlockSpec((tm, LANES), lambda i: (nb - 1 - i, 0)), pl.BlockSpec((1, LANES), lambda i: (0, 0))],
        out_specs=[pl.BlockSpec((tm, LANES), lambda i: (nb - 1 - i, 0)), pl.BlockSpec((1, LANES), lambda i: (0, 0))],
        out_shape=[jax.ShapeDtypeStruct((s, LANES), F32), jax.ShapeDtypeStruct((1, LANES), F32)],
        scratch_shapes=[pltpu.VMEM((8, LANES), F32)],
        compiler_params=_cparams(("arbitrary",)),
    )(dct, pff, bias)


def _diag_mask(t):
    r = lax.broadcasted_iota(jnp.int32, (t, t), 0)
    c = lax.broadcasted_iota(jnp.int32, (t, t), 1)
    return r >= c


AUX_ONES = 6


def _pieces(x):
    h = x.astype(BF16)
    r = x - h.astype(F32)
    m = r.astype(BF16)
    return h, m, (r - m.astype(F32)).astype(BF16)


def _lane_put(lane, cols, base):
    out = None
    for i, col in enumerate(cols):
        term = jnp.where(lane == base + i, col.astype(F32), 0.0)
        out = term if out is None else out + term
    return out


def _fox_prep2(proj, pff, bias, tm=256):
    s = pff.shape[0]

    def body(q_ref, k_ref, v_ref, p_ref, b_ref, qb_ref, kb_ref, vb_ref, ka_ref, carry):
        @pl.when(pl.program_id(0) == 0)
        def _():
            carry[...] = jnp.zeros_like(carry)

        qb_ref[...] = (q_ref[...] * 0.125).astype(BF16)
        kb_ref[...] = k_ref[...].astype(BF16)
        vb_ref[...] = v_ref[...].astype(BF16)
        lf = _log_sigmoid(p_ref[...] + b_ref[...])
        c = _split_dot(_tri(tm, True), lf, 3) + carry[0:1]
        carry[...] = jnp.broadcast_to(c[tm - 1:tm], carry.shape)
        lane = lax.broadcasted_iota(jnp.int32, (tm, LANES), 1)
        ones = jnp.where((lane >= AUX_ONES) & (lane < AUX_ONES + 6), 1.0, 0.0)
        for p in range(FOX_H // 2):
            aux = ones
            for z in range(2):
                col = jnp.sum(jnp.where(lane == 2 * p + z, c, 0.0), axis=1, keepdims=True)
                aux = aux + _lane_put(lane, _pieces(-col), 3 * z)
            ka_ref[:, p * LANES:(p + 1) * LANES] = aux.astype(BF16)

    row = lambda cb: pl.BlockSpec((tm, D), lambda i: (i, cb))
    return pl.pallas_call(
        body, name="fox_prep", grid=(s // tm,),
        in_specs=[row(4), row(5), row(6), pl.BlockSpec((tm, LANES), lambda i: (i, 0)),
                  pl.BlockSpec((1, LANES), lambda i: (0, 0))],
        out_specs=[row(0)] * 4, out_shape=[jax.ShapeDtypeStruct((s, D), BF16)] * 4,
        scratch_shapes=[pltpu.VMEM((8, LANES), F32)],
        compiler_params=_cparams(("arbitrary",)),
    )(proj, proj, proj, pff, bias)


def _fox_fwd2(qb, kb, vb, ka):
    s = qb.shape[0]
    t = min(FOX_T, s)
    nq = s // t

    def body(q_ref, k_ref, v_ref, ka_ref, o_ref, la_ref):
        i = pl.program_id(1)
        lane = lax.broadcasted_iota(jnp.int32, (t, LANES), 1)
        in_a = lane < FOX_D
        q = q_ref[...]
        zero = jnp.zeros_like(q)
        qh = [jnp.where(in_a, q, zero), jnp.where(in_a, zero, q)]
        c_ones = [jnp.where((lane >= 3 * z) & (lane < 3 * z + 3), 1.0, 0.0) for z in range(2)]

        def keys(j):
            rows = pl.ds(pl.multiple_of(j * t, t), t)
            return jnp.concatenate([k_ref[rows, :], ka_ref[rows, :]], axis=1), rows

        dmask = _diag_mask(t)

        def logits(qx, kk, masked):
            e = lax.dot_general(qx, kk, (NT, ((), ())), preferred_element_type=F32)
            return jnp.where(dmask, e, -1e30) if masked else e

        qc = [jnp.concatenate([qh[z], c_ones[z].astype(BF16)], axis=1) for z in range(2)]

        def step(j, carry, masked):
            kk, rows = keys(j)
            vj = v_ref[rows, :]
            scores = [logits(qc[z], kk, masked) for z in range(2)]
            one = jnp.ones_like(vj)
            vh = [jnp.where(in_a, vj, one), jnp.where(in_a, one, vj)]
            out = []
            for z in range(2):
                m, acc = carry[z]
                m_new = jnp.maximum(m, jnp.max(scores[z], axis=1, keepdims=True))
                p = jnp.exp(scores[z] - m_new)
                out.append((m_new, jnp.exp(m - m_new) * acc + _dot(p, vh[z], NN)))
            return tuple(out)

        init = tuple((jnp.full((t, 1), -1e30, F32), jnp.zeros((t, LANES), F32)) for _ in range(2))
        (ma, acc_a), (mb, acc_b) = step(i, lax.fori_loop(0, i, lambda j, c: step(j, c, False), init), True)
        la = jnp.sum(jnp.where(lane == FOX_D, acc_a, 0.0), axis=1, keepdims=True)
        lb = jnp.sum(jnp.where(lane == 0, acc_b, 0.0), axis=1, keepdims=True)
        o_ref[...] = jnp.where(in_a, acc_a / la, acc_b / lb).astype(o_ref.dtype)
        la_ref[...] = (_lane_put(lane, _pieces(-(ma + jnp.log(la))), AUX_ONES)
                       + _lane_put(lane, _pieces(-(mb + jnp.log(lb))), AUX_ONES + 3)).astype(la_ref.dtype)

    blk = pl.BlockSpec((t, LANES), lambda p, i: (i, p))
    whole = pl.BlockSpec((s, LANES), lambda p, i: (0, p))
    return pl.pallas_call(
        body, name="fox_attn_fwd", grid=(FOX_H // 2, nq), in_specs=[blk, whole, whole, whole],
        out_specs=[blk, blk], out_shape=[jax.ShapeDtypeStruct((s, D), BF16)] * 2,
        compiler_params=_cparams(("arbitrary", "arbitrary")),
    )(qb, kb, vb, ka)


def _fox_bwd2(qb, kb, vb, ka, ob, laux, dob):
    s = qb.shape[0]
    t = min(FOX_T, s)
    nq = s // t

    def body(q_ref, k_ref, v_ref, ka_ref, o_ref, la_ref, do_ref, dq_ref, dk_ref, dv_ref, dc_ref, dkt, dvt):
        i = pl.program_id(1)

        @pl.when(i == 0)
        def _():
            dkt[...] = jnp.zeros_like(dkt)
            dvt[...] = jnp.zeros_like(dvt)
            dc_ref[...] = jnp.zeros_like(dc_ref)

        lane = lax.broadcasted_iota(jnp.int32, (t, LANES), 1)
        in_a = lane < FOX_D
        q, do, la = q_ref[...], do_ref[...], la_ref[...].astype(F32)
        zero = jnp.zeros_like(q)
        qh = [jnp.where(in_a, q, zero), jnp.where(in_a, zero, q)]
        doh = [jnp.where(in_a, do, zero), jnp.where(in_a, zero, do)]
        qt = [h.astype(F32).T.astype(BF16) for h in qh]
        dot_ = [h.astype(F32).T.astype(BF16) for h in doh]
        prod = do.astype(F32) * o_ref[...].astype(F32)
        qx, dox = [], []
        for z in range(2):
            delta = jnp.sum(jnp.where(in_a if z == 0 else ~in_a, prod, 0.0), axis=1, keepdims=True)
            c_ones = jnp.where((lane >= 3 * z) & (lane < 3 * z + 3), 1.0, 0.0)
            lse_lanes = (lane >= AUX_ONES + 3 * z) & (lane < AUX_ONES + 3 * z + 3)
            qx.append(jnp.concatenate([qh[z], (c_ones + jnp.where(lse_lanes, la, 0.0)).astype(BF16)], axis=1))
            dox.append(jnp.concatenate([doh[z], _lane_put(lane, _pieces(-delta), 3 * z).astype(BF16)], axis=1))
        v_ones = jnp.where(lane < 6, 1.0, 0.0).astype(BF16)
        dmask = _diag_mask(t)

        def step(j, carry, masked):
            rows = pl.ds(pl.multiple_of(j * t, t), t)
            kj, vj = k_ref[rows, :], v_ref[rows, :]
            kk = jnp.concatenate([kj, ka_ref[rows, :]], axis=1)
            vv = jnp.concatenate([vj, v_ones], axis=1)
            out = []
            dk_add, dv_add = None, None
            for z in range(2):
                dq, rsum = carry[z]
                e = lax.dot_general(qx[z], kk, (NT, ((), ())), preferred_element_type=F32)
                if masked:
                    e = jnp.where(dmask, e, -1e30)
                p = jnp.exp(e)
                ds = p * lax.dot_general(dox[z], vv, (NT, ((), ())), preferred_element_type=F32)
                dkz, dvz = _dot(qt[z], ds, NN), _dot(dot_[z], p, NN)
                dk_add = dkz if dk_add is None else dk_add + dkz
                dv_add = dvz if dv_add is None else dv_add + dvz
                dc_ref[0, z, j] += -jnp.sum(ds, axis=0, keepdims=True)
                out.append((dq + _dot(ds, kj, NN), rsum + jnp.sum(ds, axis=1, keepdims=True)))
            dkt[j] += dk_add
            dvt[j] += dv_add
            return tuple(out)

        init = tuple((jnp.zeros((t, LANES), F32), jnp.zeros((t, 1), F32)) for _ in range(2))
        (dq_a, rs_a), (dq_b, rs_b) = step(i, lax.fori_loop(0, i, lambda j, c: step(j, c, False), init), True)
        for z, rs in enumerate((rs_a, rs_b)):
            dc_ref[0, z, i] += jnp.transpose(jnp.broadcast_to(rs, (t, LANES)))[0:1]
        dq_ref[...] = (jnp.where(in_a, dq_a, dq_b) * 0.125).astype(dq_ref.dtype)

        @pl.when(i == nq - 1)
        def _():
            for jb in range(nq):
                dk_ref[jb * t:(jb + 1) * t, :] = dkt[jb].T.astype(dk_ref.dtype)
                dv_ref[jb * t:(jb + 1) * t, :] = dvt[jb].T.astype(dv_ref.dtype)

    blk = pl.BlockSpec((t, LANES), lambda p, i: (i, p))
    whole = pl.BlockSpec((s, LANES), lambda p, i: (0, p))
    return pl.pallas_call(
        body, name="fox_attn_bwd", grid=(FOX_H // 2, nq),
        in_specs=[blk, whole, whole, whole, blk, blk, blk],
        out_specs=[blk, whole, whole, pl.BlockSpec((1, 2, nq, 1, t), lambda p, i: (p, 0, 0, 0, 0))],
        out_shape=[jax.ShapeDtypeStruct((s, D), BF16)] * 3 + [jax.ShapeDtypeStruct((FOX_H // 2, 2, nq, 1, t), F32)],
        scratch_shapes=[pltpu.VMEM((nq, LANES, t), F32), pltpu.VMEM((nq, LANES, t), F32)],
        compiler_params=_cparams(("arbitrary", "arbitrary")),
    )(qb, kb, vb, ka, ob, laux, dob)


def _adamw(name, w, g, m, v, tm=None):
    rows, width = w.shape
    tm = rows if tm is None else tm
    c1 = 1.0 - ADAM_B1 ** ADAM_STEP
    c2 = 1.0 - ADAM_B2 ** ADAM_STEP

    def fn(wb, gb, mb, vb):
        m_new = ADAM_B1 * mb + (1.0 - ADAM_B1) * gb
        v_new = ADAM_B2 * vb + (1.0 - ADAM_B2) * (gb * gb)
        delta = -ADAM_LR * ((m_new / c1) / (jnp.sqrt(v_new / c2) + ADAM_EPS) + ADAM_WD * wb)
        return (delta, m_new, v_new), ()

    ins = [_rb(a, tm, width) for a in (w, g, m, v)]
    return _rows(name, fn, rows, tm, ins, [(width, F32)] * 3)


def _me():
    return lax.axis_index("x"), lax.axis_index("y"), lax.axis_index("c")


def _all_gather8(name, block):
    m, n = block.shape

    def body(x_ref, out_ref, send_sems, recv_sems):
        x, y, c = _me()
        me, sibling = (x, y, c), (x, y, 1 - c)
        chips = [(1 - x, y), (x, 1 - y), (1 - x, 1 - y)]

        def slot(px, py, pc):
            return out_ref.at[4 * px + 2 * py + pc]

        def copy(k, blk, to, src=None):
            return pltpu.make_async_remote_copy(
                src_ref=slot(*blk) if src is None else src, dst_ref=slot(*blk),
                send_sem=send_sems.at[k], recv_sem=recv_sems.at[k], device_id=to, device_id_type=MESH)

        first = [copy(0, me, sibling, src=x_ref)]
        first += [copy(1 + j, me, (*chip, c), src=x_ref) for j, chip in enumerate(chips)]
        for cp in first:
            cp.start()
        passed = [copy(4 + j, (*chip, c), sibling) for j, chip in enumerate(chips)]
        for j, chip in enumerate(chips):
            copy(1 + j, (*chip, c), me).wait_recv()
            passed[j].start()
        copy(0, sibling, me).wait_recv()
        back = copy(7, sibling, sibling)
        back.start()
        for j, chip in enumerate(chips):
            copy(4 + j, (*chip, 1 - c), me).wait_recv()
        copy(7, me, me).wait_recv()
        for cp in first + passed + [back]:
            cp.wait_send()

    return pl.pallas_call(
        body, name=name, in_specs=[ANY], out_specs=ANY,
        out_shape=jax.ShapeDtypeStruct((8, m, n), block.dtype),
        scratch_shapes=[pltpu.SemaphoreType.DMA((8,)), pltpu.SemaphoreType.DMA((8,))],
    )(block)


def _swap_halves(name, g):
    n, _, m, lanes = g.shape

    def body(g_ref, got_ref, send_sems, recv_sems):
        x, y, c = _me()
        copies = [pltpu.make_async_remote_copy(
            src_ref=g_ref.at[j, 1 - c], dst_ref=got_ref.at[j], send_sem=send_sems.at[j], recv_sem=recv_sems.at[j],
            device_id=(x, y, 1 - c), device_id_type=MESH) for j in range(n)]
        for cp in copies:
            cp.start()
        for cp in copies:
            cp.wait()

    return pl.pallas_call(
        body, name=name, in_specs=[ANY], out_specs=ANY, out_shape=jax.ShapeDtypeStruct((n, m, lanes), g.dtype),
        scratch_shapes=[pltpu.SemaphoreType.DMA((n,)), pltpu.SemaphoreType.DMA((n,))],
    )(g)


def _swap_sibling(name, mine):
    def body(m_ref, out_ref, send_sem, recv_sem):
        x, y, c = _me()
        cp = pltpu.make_async_remote_copy(src_ref=m_ref, dst_ref=out_ref, send_sem=send_sem, recv_sem=recv_sem,
                                          device_id=(x, y, 1 - c), device_id_type=MESH)
        cp.start()
        cp.wait()

    return pl.pallas_call(
        body, name=name, in_specs=[ANY], out_specs=ANY, out_shape=jax.ShapeDtypeStruct(mine.shape, mine.dtype),
        scratch_shapes=[pltpu.SemaphoreType.DMA, pltpu.SemaphoreType.DMA],
    )(mine)


def _chip_exchange(name, p):
    def body(p_ref, out_ref, bounce_ref, send_sems, recv_sems):
        x, y, c = _me()
        my_chip = 2 * x + y
        sibling = (x, y, 1 - c)
        chips = [(1 - x, y), (x, 1 - y), (1 - x, 1 - y)]

        def copy(k, src, dst, to):
            return pltpu.make_async_remote_copy(src_ref=src, dst_ref=dst, send_sem=send_sems.at[k],
                                                recv_sem=recv_sems.at[k], device_id=to, device_id_type=MESH)

        sends = [copy(k, p_ref.at[2 * px + py], out_ref.at[my_chip], (px, py, c)) for k, (px, py) in enumerate(chips)]
        sends.append(copy(3, p_ref.at[my_chip], bounce_ref, sibling))
        for cp in sends:
            cp.start()
        copy(3, p_ref.at[my_chip], bounce_ref, sibling).wait_recv()
        back = copy(4, bounce_ref, out_ref.at[my_chip], sibling)
        back.start()
        for k, (px, py) in enumerate(chips):
            copy(k, p_ref.at[my_chip], out_ref.at[2 * px + py], (px, py, c)).wait_recv()
        copy(4, bounce_ref, out_ref.at[my_chip], sibling).wait_recv()
        for cp in sends + [back]:
            cp.wait_send()

    piece = jax.ShapeDtypeStruct(p.shape[1:], p.dtype)
    return pl.pallas_call(
        body, name=name, in_specs=[ANY], out_specs=[ANY, ANY], out_shape=[jax.ShapeDtypeStruct(p.shape, p.dtype), piece],
        scratch_shapes=[pltpu.SemaphoreType.DMA((5,)), pltpu.SemaphoreType.DMA((5,))],
    )(p)[0]


def _all_reduce_small(name, block):
    r, n = block.shape

    def body(x_ref, sum_ref, gath, send_sems, recv_sems):
        x, y, c = _me()
        me = 4 * x + 2 * y + c
        gath[me] = x_ref[...]
        sends = []
        for k in range(1, 8):
            px = x ^ ((k >> 2) & 1)
            py = y ^ ((k >> 1) & 1)
            pc = c ^ (k & 1)
            sends.append(pltpu.make_async_remote_copy(
                src_ref=x_ref, dst_ref=gath.at[me], send_sem=send_sems.at[k - 1], recv_sem=recv_sems.at[k - 1],
                device_id=(px, py, pc), device_id_type=MESH))
        for cp in sends:
            cp.start()
        for k in range(1, 8):
            peer = me ^ k
            pltpu.make_async_remote_copy(
                src_ref=x_ref, dst_ref=gath.at[peer], send_sem=send_sems.at[k - 1], recv_sem=recv_sems.at[k - 1],
                device_id=(x, y, c), device_id_type=MESH).wait_recv()
        for cp in sends:
            cp.wait_send()
        acc = gath[0]
        for d in range(1, 8):
            acc = acc + gath[d]
        sum_ref[...] = acc

    vm = pl.BlockSpec(memory_space=pltpu.VMEM)
    return pl.pallas_call(
        body, name=name, in_specs=[vm], out_specs=vm, out_shape=jax.ShapeDtypeStruct((r, n), F32),
        scratch_shapes=[pltpu.VMEM((8, r, n), F32), pltpu.SemaphoreType.DMA((7,)), pltpu.SemaphoreType.DMA((7,))],
    )(block)


def _add2(name, a, b, tm):
    rows = a.shape[0]
    return _rows(name, lambda p, q: ((p + q,), ()), rows, tm, [_rb(a, tm, LANES), _rb(b, tm, LANES)], [(LANES, BF16)])[0]


def _add4(name, p, tm):
    m = p.shape[1]
    flat = p.reshape(4 * m, LANES)
    nb = m // tm
    ins = [(flat, (tm, LANES), (lambda i, j=j: (j * nb + i, 0))) for j in range(4)]
    f32 = lambda v: v.astype(F32)
    return _rows(name, lambda a, b, c, d: ((((f32(a) + f32(b)) + f32(c)) + f32(d),), ()), m, tm, ins, [(LANES, F32)])[0]


SEG_ROWS = (D * W_IN_SHARD // LANES, 256 * D // LANES, 256 * D // LANES, 256 * D // LANES,
            D * W_UP_SHARD // LANES, W_DOWN_SHARD * D // LANES)
GRAD_ROWS = sum(SEG_ROWS)
CONVW_ROWS = 3 * W_UP_SHARD * 2 // LANES
GATHER_ROWS = 41600


def _flat(a):
    return a.reshape(-1, LANES)


def _gather_weights(w_in, w_a, w_b, w_out, w_up, w_down, conv_w):
    c = lax.axis_index("c")
    bits = lax.bitcast_convert_type(conv_w, BF16)
    pieces = [_flat(t.astype(BF16)) for t in (w_in, w_a, w_b, w_out, w_up, w_down)] + [_flat(bits)]
    pad = GATHER_ROWS - GRAD_ROWS - CONVW_ROWS
    shard = jnp.concatenate(pieces + [jnp.zeros((pad, LANES), BF16)], axis=0)
    half = GATHER_ROWS // 2
    mine = lax.dynamic_slice_in_dim(shard, c * half, half, axis=0)
    full = _all_gather8("all_gather_weights", mine).reshape(N_CHIP, GATHER_ROWS, LANES)
    offs = [0]
    for r in SEG_ROWS:
        offs.append(offs[-1] + r)
    seg = lambda i: full[:, offs[i]:offs[i + 1]]
    wi = seg(0).reshape(N_CHIP, D, W_IN_SHARD).transpose(1, 0, 2).reshape(D, N_CHIP * W_IN_SHARD)
    w_main = jnp.concatenate([wi[:, :FF_COL], wi[:, FF_COL + FOX_H:]], axis=1)
    w_ff = jnp.pad(wi[:, FF_COL:FF_COL + FOX_H], ((0, 0), (0, LANES - FOX_H)))
    wa, wb, wo = (seg(i).reshape(D, D) for i in (1, 2, 3))
    wu = seg(4).reshape(N_CHIP, D, W_UP_SHARD).transpose(1, 0, 2).reshape(D, 2 * D_FF)
    wd = seg(5).reshape(D_FF, D)
    cw_bits = full[:, GRAD_ROWS:GRAD_ROWS + CONVW_ROWS].reshape(N_CHIP, 3, W_UP_SHARD, 2)
    cw = lax.bitcast_convert_type(cw_bits, F32).transpose(1, 0, 2).reshape(3, 2 * D_FF)
    return w_main, w_ff, wa, wb, wo, wu, wd, cw


def _reduce_scatter_grads(d_main, d_ff, d_a, d_b, d_o, d_u, d_d):
    c = lax.axis_index("c")
    d_in = jnp.concatenate(d_main[:7] + [d_ff[:, :FOX_H]] + d_main[7:], axis=1)
    per_chip = [
        d_in.reshape(D, N_CHIP, W_IN_SHARD).transpose(1, 0, 2).reshape(N_CHIP, -1, LANES),
        d_a.reshape(N_CHIP, -1, LANES), d_b.reshape(N_CHIP, -1, LANES), d_o.reshape(N_CHIP, -1, LANES),
        d_u.reshape(D, N_CHIP, W_UP_SHARD).transpose(1, 0, 2).reshape(N_CHIP, -1, LANES),
        d_d.reshape(N_CHIP, -1, LANES),
        jnp.zeros((N_CHIP, GATHER_ROWS - GRAD_ROWS, LANES), F32),
    ]
    half = GATHER_ROWS // 2
    g = jnp.concatenate(per_chip, axis=1).reshape(N_CHIP, 2, half, LANES)
    from_sibling = _swap_halves("grad_swap_halves", g)
    mine = lax.dynamic_index_in_dim(g, c, axis=1, keepdims=False)
    tm = half // 5
    chip_sum = _add2("grad_chip_sum", mine.reshape(-1, LANES), from_sibling.reshape(-1, LANES), tm)
    pieces = _chip_exchange("grad_chip_exchange", chip_sum.reshape(N_CHIP, half, LANES))
    mine_half = _add4("grad_sum_chips", pieces, tm)
    other_half = _swap_sibling("grad_share_half", mine_half)
    lo = jnp.where(c == 0, mine_half, other_half)
    hi = jnp.where(c == 0, other_half, mine_half)
    return jnp.concatenate([lo, hi], axis=0)


WD_EXT_ROWS = 736


def _remote(src, dst, send_sems, recv_sems, k, to):
    return pltpu.make_async_remote_copy(src_ref=src, dst_ref=dst, send_sem=send_sems.at[k], recv_sem=recv_sems.at[k],
                                        device_id=to, device_id_type=MESH)


def _all_gather8_multi(name, blocks):
    nt = len(blocks)

    def body(*refs):
        x_refs, out_refs, send_sems, recv_sems = refs[:nt], refs[nt:2 * nt], refs[-2], refs[-1]
        x, y, c = _me()
        me, sibling = (x, y, c), (x, y, 1 - c)
        chips = [(1 - x, y), (x, 1 - y), (1 - x, 1 - y)]
        slot = lambda q, p: out_refs[q].at[4 * p[0] + 2 * p[1] + p[2]]

        def copies(k, blk, to, from_input=False):
            return [_remote(x_refs[q] if from_input else slot(q, blk), slot(q, blk), send_sems, recv_sems, k * nt + q, to)
                    for q in range(nt)]

        first = copies(0, me, sibling, True)
        for j, chip in enumerate(chips):
            first += copies(1 + j, me, (*chip, c), True)
        for cp in first:
            cp.start()
        passed = []
        for j, chip in enumerate(chips):
            for cp in copies(1 + j, (*chip, c), me):
                cp.wait_recv()
            fwd = copies(4 + j, (*chip, c), sibling)
            for cp in fwd:
                cp.start()
            passed += fwd
        for cp in copies(0, sibling, me):
            cp.wait_recv()
        back = copies(7, sibling, sibling)
        for cp in back:
            cp.start()
        for j, chip in enumerate(chips):
            for cp in copies(4 + j, (*chip, 1 - c), me):
                cp.wait_recv()
        for cp in copies(7, me, me):
            cp.wait_recv()
        for cp in first + passed + back:
            cp.wait_send()

    return pl.pallas_call(
        body, name=name, in_specs=[ANY] * nt, out_specs=[ANY] * nt,
        out_shape=[jax.ShapeDtypeStruct((8,) + b.shape, b.dtype) for b in blocks],
        scratch_shapes=[pltpu.SemaphoreType.DMA((8 * nt,)), pltpu.SemaphoreType.DMA((8 * nt,))],
    )(*blocks)


def _swap_halves_multi(name, gs):
    nt = len(gs)
    n_chip = gs[0].shape[0]

    def body(*refs):
        g_refs, got_refs, send_sems, recv_sems = refs[:nt], refs[nt:2 * nt], refs[-2], refs[-1]
        x, y, c = _me()
        cps = [_remote(g_refs[q].at[j, 1 - c], got_refs[q].at[j], send_sems, recv_sems, q * n_chip + j, (x, y, 1 - c))
               for q in range(nt) for j in range(n_chip)]
        for cp in cps:
            cp.start()
        for cp in cps:
            cp.wait()

    return pl.pallas_call(
        body, name=name, in_specs=[ANY] * nt, out_specs=[ANY] * nt,
        out_shape=[jax.ShapeDtypeStruct((g.shape[0],) + g.shape[2:], g.dtype) for g in gs],
        scratch_shapes=[pltpu.SemaphoreType.DMA((nt * n_chip,)), pltpu.SemaphoreType.DMA((nt * n_chip,))],
    )(*gs)


def _swap_sibling_multi(name, xs):
    nt = len(xs)

    def body(*refs):
        x_refs, out_refs, send_sems, recv_sems = refs[:nt], refs[nt:2 * nt], refs[-2], refs[-1]
        x, y, c = _me()
        cps = [_remote(x_refs[q], out_refs[q], send_sems, recv_sems, q, (x, y, 1 - c)) for q in range(nt)]
        for cp in cps:
            cp.start()
        for cp in cps:
            cp.wait()

    return pl.pallas_call(
        body, name=name, in_specs=[ANY] * nt, out_specs=[ANY] * nt,
        out_shape=[jax.ShapeDtypeStruct(a.shape, a.dtype) for a in xs],
        scratch_shapes=[pltpu.SemaphoreType.DMA((nt,)), pltpu.SemaphoreType.DMA((nt,))],
    )(*xs)


def _chip_exchange_multi(name, ps):
    nt = len(ps)

    def body(*refs):
        p_refs, out_refs, bounce_refs = refs[:nt], refs[nt:2 * nt], refs[2 * nt:3 * nt]
        send_sems, recv_sems = refs[-2], refs[-1]
        x, y, c = _me()
        my_chip = 2 * x + y
        sibling = (x, y, 1 - c)
        chips = [(1 - x, y), (x, 1 - y), (1 - x, 1 - y)]
        cp = lambda k, q, src, dst, to: _remote(src, dst, send_sems, recv_sems, k * nt + q, to)
        sends = [cp(k, q, p_refs[q].at[2 * px + py], out_refs[q].at[my_chip], (px, py, c))
                 for k, (px, py) in enumerate(chips) for q in range(nt)]
        sends += [cp(3, q, p_refs[q].at[my_chip], bounce_refs[q], sibling) for q in range(nt)]
        for s_ in sends:
            s_.start()
        backs = []
        for q in range(nt):
            cp(3, q, p_refs[q].at[my_chip], bounce_refs[q], sibling).wait_recv()
            backs.append(cp(4, q, bounce_refs[q], out_refs[q].at[my_chip], sibling))
            backs[-1].start()
        for k, (px, py) in enumerate(chips):
            for q in range(nt):
                cp(k, q, p_refs[q].at[my_chip], out_refs[q].at[2 * px + py], (px, py, c)).wait_recv()
        for q in range(nt):
            cp(4, q, bounce_refs[q], out_refs[q].at[my_chip], sibling).wait_recv()
        for s_ in sends + backs:
            s_.wait_send()

    outs = pl.pallas_call(
        body, name=name, in_specs=[ANY] * nt, out_specs=[ANY] * (2 * nt),
        out_shape=[jax.ShapeDtypeStruct(p.shape, p.dtype) for p in ps]
        + [jax.ShapeDtypeStruct(p.shape[1:], p.dtype) for p in ps],
        scratch_shapes=[pltpu.SemaphoreType.DMA((5 * nt,)), pltpu.SemaphoreType.DMA((5 * nt,))],
    )(*ps)
    return outs[:nt]


def _row_tile(m):
    return m if m <= 384 else 128


def _add2_rows(name, a, b):
    n4, m, n = a.shape
    tm = _row_tile(m)
    out = _rows(name, lambda p, q: ((p + q,), ()), n4 * m, tm,
                [_rb(a.reshape(n4 * m, n), tm, n), _rb(b.reshape(n4 * m, n), tm, n)], [(n, BF16)])[0]
    return out.reshape(n4, m, n)


def _add4_rows(name, p):
    _, m, n = p.shape
    tm = _row_tile(m)
    nb = m // tm
    flat = p.reshape(4 * m, n)
    ins = [(flat, (tm, n), (lambda i, j=j: (j * nb + i, 0))) for j in range(4)]
    f32 = lambda v: v.astype(F32)
    return _rows(name, lambda a, b, c, d: ((((f32(a) + f32(b)) + f32(c)) + f32(d),), ()), m, tm, ins, [(n, F32)])[0]


def _gather_weights2(w_in, w_a, w_b, w_out, w_up, w_down, conv_w):
    c = lax.axis_index("c")
    bits = lax.bitcast_convert_type(conv_w, BF16).reshape(-1)
    extra = jnp.zeros(((WD_EXT_ROWS - W_DOWN_SHARD) * D,), BF16).at[:bits.shape[0]].set(bits)
    wd_ext = jnp.concatenate([w_down.astype(BF16), extra.reshape(-1, D)], axis=0)
    shards = [w_in.astype(BF16), w_a.astype(BF16), w_b.astype(BF16), w_out.astype(BF16), w_up.astype(BF16), wd_ext]
    halves = [lax.dynamic_slice_in_dim(t, c * (t.shape[0] // 2), t.shape[0] // 2, axis=0) for t in shards]
    full = [g.reshape((N_CHIP, 2 * g.shape[1]) + g.shape[2:]) for g in _all_gather8_multi("all_gather_weights", halves)]
    wi = full[0].transpose(1, 0, 2).reshape(D, N_CHIP * W_IN_SHARD)
    w_main = jnp.concatenate([wi[:, :FF_COL], wi[:, FF_COL + FOX_H:]], axis=1)
    w_ff = jnp.pad(wi[:, FF_COL:FF_COL + FOX_H], ((0, 0), (0, LANES - FOX_H)))
    wa, wb, wo = (full[i].reshape(D, D) for i in (1, 2, 3))
    wu = full[4].transpose(1, 0, 2).reshape(D, 2 * D_FF)
    wd = full[5][:, :W_DOWN_SHARD].reshape(D_FF, D)
    cw_bits = full[5][:, W_DOWN_SHARD:].reshape(N_CHIP, -1)[:, :bits.shape[0]].reshape(N_CHIP, 3, W_UP_SHARD, 2)
    cw = lax.bitcast_convert_type(cw_bits, F32).transpose(1, 0, 2).reshape(3, 2 * D_FF)
    return w_main, w_ff, wa, wb, wo, wu, wd, cw


def _reduce_scatter_grads2(d_main, d_ff, d_a, d_b, d_o, d_u, d_d):
    c = lax.axis_index("c")
    d_in = jnp.concatenate(d_main[:7] + [d_ff[:, :FOX_H]] + d_main[7:], axis=1)
    per_chip = [
        d_in.reshape(D, N_CHIP, W_IN_SHARD).transpose(1, 0, 2),
        d_a.reshape(N_CHIP, -1, D), d_b.reshape(N_CHIP, -1, D), d_o.reshape(N_CHIP, -1, D),
        d_u.reshape(D, N_CHIP, W_UP_SHARD).transpose(1, 0, 2),
        d_d.reshape(N_CHIP, -1, D),
    ]
    gs = [t.reshape(N_CHIP, 2, t.shape[1] // 2, t.shape[2]) for t in per_chip]
    got = _swap_halves_multi("grad_swap_halves", gs)
    sums = [_add2_rows("grad_chip_sum_%d" % q, lax.dynamic_index_in_dim(g, c, axis=1, keepdims=False), s_)
            for q, (g, s_) in enumerate(zip(gs, got))]
    pieces = _chip_exchange_multi("grad_chip_exchange", sums)
    mine = [_add4_rows("grad_sum_chips_%d" % q, p) for q, p in enumerate(pieces)]
    other = _swap_sibling_multi("grad_share_half", mine)
    return [jnp.concatenate([jnp.where(c == 0, a, b), jnp.where(c == 0, b, a)], axis=0) for a, b in zip(mine, other)]


def _local_step(x, target, norm_mix, fox_f_bias, hg_lb_logits, hg_norm, norm_ffn, conv_b, norm_final,
                w_main, w_ff, wa, wb, wo, wu, wd, conv_w):
    s = x.shape[0]
    bias = jnp.pad(fox_f_bias, ((0, 0), (0, LANES - FOX_H)))
    conv_w8 = jnp.pad(conv_w, ((0, 5), (0, 0)))
    t = min(FOX_T, s)

    n1, n1t = _rms_fwd("norm_mix_fwd", x, norm_mix)
    proj = _mm("in_proj", n1, w_main, "nn", F32, 1024, 1024, D)
    pff = _mm("in_proj_forget", n1, w_ff, "nn", F32, 1024, LANES, D)
    qb, kb, vb, ka = _fox_prep2(proj, pff, bias)
    o_b, laux = _fox_fwd2(qb, kb, vb, ka)
    o_raw, states = _hg_fwd(proj, hg_lb_logits)
    o_a = _hg_post_fwd(o_raw, proj, hg_norm)
    pa = _mm("branch_a", o_a, wa, "nn", F32, 1024, 1024, D)
    pb = _mm("branch_b", o_b, wb, "nn", F32, 1024, 1024, D)
    merged = _merge_fwd(pa, pb, proj)
    h1 = _mm("out_proj", merged, wo, "nn", F32, 1024, 1024, D, res=x)
    n2, n2t = _rms_fwd("norm_ffn_fwd", h1, norm_ffn)
    u = _mm("ffn_up", n2, wu, "nn", F32, 1024, W_UP_SHARD, D)
    act, gelu_gate, dact_dgate = _convglu_fwd(u, conv_w8, conv_b)
    h2 = _mm("ffn_down", act, wd, "nn", F32, 512, 1024, D_FF, res=h1)
    (dh2,), (d_norm_final, loss_row) = _final(h2, target, norm_final)

    dact = _mm("ffn_down_dx", dh2, wd, "nt", BF16, 1024, D_FF, D)
    d_wd = _mm("ffn_down_dw", act, dh2, "tn", F32, D_FF // 2, 1024, DW_TK // 2)
    (du,), (d_conv_w8, d_conv_b) = _convglu_bwd(u, dact, gelu_gate, dact_dgate, conv_w8)
    dn2 = _mm("ffn_up_dx", du, wu, "nt", F32, 1024, 1024, W_UP_SHARD)
    d_wu = _mm("ffn_up_dw", n2t, du, "nn", F32, 1024, W_UP_SHARD, DW_TK)
    (dh1,), (d_norm_ffn,) = _rms_bwd("norm_ffn_bwd", h1, norm_ffn, [dn2], dh2)

    dmerged = _mm("out_proj_dx", dh1, wo, "nt", F32, 1024, 1024, D)
    d_wo = _mm("out_proj_dw", merged, dh1, "tn", F32, 1024, 1024, DW_TK)
    dpa, dpb, dga, dgb = _merge_bwd(dmerged, pa, pb, proj)
    do_a = _mm("branch_a_dx", dpa, wa, "nt", F32, 1024, 1024, D)
    do_b = _mm("branch_b_dx", dpb, wb, "nt", BF16, 1024, 1024, D)
    d_wa = _mm("branch_a_dw", o_a, dpa, "tn", F32, 1024, 1024, DW_TK)
    d_wb = _mm("branch_b_dw", o_b, dpb, "tn", F32, 1024, 1024, DW_TK)

    (do_raw, dhg), (d_hg_norm,) = _hg_post_bwd(do_a, o_raw, proj, hg_norm)
    dhq, dhf, dhi, d_lb_logits = _hg_bwd(proj, hg_lb_logits, states, do_raw)

    dfq, dfk, dfv, dcrow = _fox_bwd2(qb, kb, vb, ka, o_b, laux, do_b)
    dct = jnp.pad(dcrow.reshape(FOX_H, s), ((0, LANES - FOX_H), (0, 0)))
    dff, d_bias = _fox_gate_bwd(dct, pff, bias)

    pieces = [dhq, dhf, dhi, dhg, dfq, dfk, dfv, dga, dgb]
    dn1 = _mm_sum_nt("in_proj_dx", pieces, w_main, (dff, w_ff), 1024, 1024)
    d_w_main = [_mm("in_proj_dw_%d" % i, n1t, p, "nn", F32, 1024, 1024, DW_TK) for i, p in enumerate(pieces)]
    d_w_ff = _mm("in_proj_forget_dw", n1t, dff, "nn", F32, 1024, LANES, DW_TK)
    (dx,), (d_norm_mix,) = _rms_bwd("norm_mix_bwd", x, norm_mix, [dn1], dh1)

    small = dict(norm_mix=d_norm_mix, fox_f_bias=d_bias[:, :FOX_H], hg_lb_logits=d_lb_logits, hg_norm=d_hg_norm,
                 norm_ffn=d_norm_ffn, conv_b=d_conv_b, norm_final=d_norm_final, conv_w=d_conv_w8[:3], loss=loss_row)
    big = (d_w_main, d_w_ff, d_wa, d_wb, d_wo, d_wu, d_wd)
    return dx, small, big


SMALL_KEYS = ("norm_mix", "fox_f_bias", "hg_lb_logits", "hg_norm", "norm_ffn", "conv_b", "norm_final")


def _pack_small(parts):
    rows, layout = [], []
    for key, arr in parts:
        flat = arr.reshape(-1)
        n = flat.shape[0]
        nr = -(-n // LANES)
        rows.append(jnp.pad(flat, (0, nr * LANES - n)).reshape(nr, LANES))
        layout.append((key, arr.shape, n, nr))
    packed = jnp.concatenate(rows, axis=0)
    pad = -packed.shape[0] % 8
    return jnp.pad(packed, ((0, pad), (0, 0))), layout


def _unpack_small(packed, layout):
    out, r0 = {}, 0
    for key, shape, n, nr in layout:
        out[key] = packed[r0:r0 + nr].reshape(-1)[:n].reshape(shape)
        r0 += nr
    return out


def kernel(x, norm_mix, w_in, fox_f_bias, hg_lb_logits, hg_norm, w_branch_a, w_branch_b, w_out, norm_ffn, w_up, conv_w, conv_b, w_down, norm_final, loss_target, m_norm_mix, m_w_in, m_fox_f_bias, m_hg_lb_logits, m_hg_norm, m_w_branch_a, m_w_branch_b, m_w_out, m_norm_ffn, m_w_up, m_conv_w, m_conv_b, m_w_down, m_norm_final, v_norm_mix, v_w_in, v_fox_f_bias, v_hg_lb_logits, v_hg_norm, v_w_branch_a, v_w_branch_b, v_w_out, v_norm_ffn, v_w_up, v_conv_w, v_conv_b, v_w_down, v_norm_final):
    chip = 2 * lax.axis_index("x") + lax.axis_index("y")
    w_main, w_ff, wa, wb, wo, wu, wd, cw = _gather_weights2(
        w_in[0], w_branch_a[0], w_branch_b[0], w_out[0], w_up[0], w_down[0], conv_w[0])
    dx, small, big = _local_step(
        x[0], loss_target[0], norm_mix, fox_f_bias, hg_lb_logits, hg_norm, norm_ffn, conv_b,
        norm_final.reshape(1, D), w_main, w_ff, wa, wb, wo, wu, wd, cw)

    packed, layout = _pack_small([(k, small[k]) for k in SMALL_KEYS + ("conv_w", "loss")])
    red = _unpack_small(_all_reduce_small("all_reduce_small", packed), layout)
    loss = red["loss"][0, 0]
    g_conv_w = lax.dynamic_slice_in_dim(red["conv_w"], chip * W_UP_SHARD, W_UP_SHARD, axis=1)

    g_big = _reduce_scatter_grads2(*big)

    names = ["norm_mix", "w_in", "fox_f_bias", "hg_lb_logits", "hg_norm", "w_branch_a", "w_branch_b", "w_out",
             "norm_ffn", "w_up", "conv_w", "conv_b", "w_down", "norm_final"]
    weights = dict(norm_mix=norm_mix, w_in=w_in, fox_f_bias=fox_f_bias, hg_lb_logits=hg_lb_logits, hg_norm=hg_norm,
                   w_branch_a=w_branch_a, w_branch_b=w_branch_b, w_out=w_out, norm_ffn=norm_ffn, w_up=w_up,
                   conv_w=conv_w, conv_b=conv_b, w_down=w_down, norm_final=norm_final)
    ms = dict(norm_mix=m_norm_mix, w_in=m_w_in, fox_f_bias=m_fox_f_bias, hg_lb_logits=m_hg_lb_logits,
              hg_norm=m_hg_norm, w_branch_a=m_w_branch_a, w_branch_b=m_w_branch_b, w_out=m_w_out,
              norm_ffn=m_norm_ffn, w_up=m_w_up, conv_w=m_conv_w, conv_b=m_conv_b, w_down=m_w_down,
              norm_final=m_norm_final)
    vs = dict(norm_mix=v_norm_mix, w_in=v_w_in, fox_f_bias=v_fox_f_bias, hg_lb_logits=v_hg_lb_logits,
              hg_norm=v_hg_norm, w_branch_a=v_w_branch_a, w_branch_b=v_w_branch_b, w_out=v_w_out,
              norm_ffn=v_norm_ffn, w_up=v_w_up, conv_w=v_conv_w, conv_b=v_conv_b, w_down=v_w_down,
              norm_final=v_norm_final)

    grads, deltas, new_m, new_v = {}, {}, {}, {}
    big_names = ["w_in", "w_branch_a", "w_branch_b", "w_out", "w_up", "w_down"]
    for name, g2 in zip(big_names, g_big):
        shape = weights[name].shape
        rows = g2.shape[0]
        d_, m_, v_ = _adamw("adamw_" + name, weights[name][0], g2, ms[name][0], vs[name][0], tm=rows // 8)
        grads[name], deltas[name], new_m[name], new_v[name] = (a.reshape(shape) for a in (g2, d_, m_, v_))
    shape = conv_w.shape
    d_, m_, v_ = _adamw("adamw_conv_w", conv_w[0], g_conv_w, m_conv_w[0], v_conv_w[0])
    grads["conv_w"], deltas["conv_w"], new_m["conv_w"], new_v["conv_w"] = (
        a.reshape(shape) for a in (g_conv_w, d_, m_, v_))
    gs = {k: red[k].reshape(weights[k].shape) for k in SMALL_KEYS}
    pw, lay = _pack_small([(k, weights[k]) for k in SMALL_KEYS])
    pg, _ = _pack_small([(k, gs[k]) for k in SMALL_KEYS])
    pm, _ = _pack_small([(k, ms[k]) for k in SMALL_KEYS])
    pv, _ = _pack_small([(k, vs[k]) for k in SMALL_KEYS])
    d_, m_, v_ = (_unpack_small(a, lay) for a in _adamw("adamw_small", pw, pg, pm, pv))
    for k in SMALL_KEYS:
        grads[k], deltas[k], new_m[k], new_v[k] = gs[k], d_[k], m_[k], v_[k]

    return (loss, dx[None], *[grads[n] for n in names], *[deltas[n] for n in names],
            *[new_m[n] for n in names], *[new_v[n] for n in names])
```

```python
import jax
import jax.numpy as jnp
from jax import lax
from jax.experimental import pallas as pl
from jax.experimental.pallas import tpu as pltpu

F32 = jnp.float32
BF16 = jnp.bfloat16

D = 1024
HG_H, HG_DK = 8, 128
FOX_H, FOX_D = 16, 64
D_FF = 2816
EPS = 1e-6
N_CHIP = 4
LANES = 128
W_IN_SHARD = 2308
W_UP_SHARD = 1408
W_DOWN_SHARD = 704
FF_COL = 7168
ADAM_LR, ADAM_B1, ADAM_B2, ADAM_EPS, ADAM_WD, ADAM_STEP = 0.001, 0.9, 0.999, 1e-08, 0.01, 10

HG_C = 16
HG_T = 512
HG_UNROLL = 16
HG_UNROLL_BWD = 4
FOX_T = 512
DW_TK = 2048
VMEM_LIMIT = 56 * 1024 * 1024
MESH = pl.DeviceIdType.MESH
ANY = pl.BlockSpec(memory_space=pl.ANY)


def _cparams(sem):
    return pltpu.CompilerParams(dimension_semantics=sem, vmem_limit_bytes=VMEM_LIMIT)


def _sigmoid(x):
    return 1.0 / (1.0 + jnp.exp(-x))


def _dot(a, b, dims):
    return lax.dot_general(a.astype(BF16), b.astype(BF16), (dims, ((), ())), preferred_element_type=F32)


NN = ((1,), (0,))
NT = ((1,), (1,))
TN = ((0,), (0,))


def _split_dot(tri, x, parts, dims=NN):
    acc = None
    r = x
    for _ in range(parts):
        p = r.astype(BF16)
        t = lax.dot_general(tri, p, (dims, ((), ())), preferred_element_type=F32)
        acc = t if acc is None else acc + t
        r = r - p.astype(F32)
    return acc


def _rb(arr, tm, width, cb=0):
    return (arr, (tm, width), lambda i: (i, cb))


def _cst(arr):
    return (arr, arr.shape, lambda i: (0,) * arr.ndim)


def _rows(name, fn, n_rows, tm, ins, outs, accs=()):
    n_in, n_out, n_acc = len(ins), len(outs), len(accs)
    nb = n_rows // tm

    def body(*refs):
        vals = [r[...] for r in refs[:n_in]]
        o, a = fn(*vals)
        for r, v in zip(refs[n_in:n_in + n_out], o):
            r[...] = v.astype(r.dtype)
        if n_acc:
            acc_refs = refs[n_in + n_out:]

            @pl.when(pl.program_id(0) == 0)
            def _():
                for r in acc_refs:
                    r[...] = jnp.zeros_like(r)

            for r, v in zip(acc_refs, a):
                r[...] += v

    in_specs = [pl.BlockSpec(bs, im) for (_, bs, im) in ins]
    out_specs = [pl.BlockSpec((tm, w), lambda i: (i, 0)) for (w, _) in outs]
    out_specs += [pl.BlockSpec((r, w), lambda i: (0, 0)) for (r, w) in accs]
    out_shape = [jax.ShapeDtypeStruct((n_rows, w), dt) for (w, dt) in outs]
    out_shape += [jax.ShapeDtypeStruct((r, w), F32) for (r, w) in accs]
    res = pl.pallas_call(
        body, name=name, grid=(nb,), in_specs=in_specs, out_specs=out_specs, out_shape=out_shape,
        compiler_params=_cparams(("arbitrary",)),
    )(*[a for a, _, _ in ins])
    return (res[:n_out], res[n_out:]) if n_acc else res


def _mm(name, a, b, mode, out_dtype, tm, tn, tk, res=None):
    if mode == "nn":
        (m, k), n = a.shape, b.shape[1]
    elif mode == "nt":
        (m, k), n = a.shape, b.shape[0]
    else:
        (k, m), n = a.shape, b.shape[1]
    tm, tn, tk = min(tm, m), min(tn, n), min(tk, k)
    assert m % tm == 0 and n % tn == 0 and k % tk == 0, (name, m, n, k, tm, tn, tk)
    if mode == "nn":
        a_spec = pl.BlockSpec((tm, tk), lambda i, j, kk: (i, kk))
        b_spec = pl.BlockSpec((tk, tn), lambda i, j, kk: (kk, j))
        dims = NN
    elif mode == "nt":
        a_spec = pl.BlockSpec((tm, tk), lambda i, j, kk: (i, kk))
        b_spec = pl.BlockSpec((tn, tk), lambda i, j, kk: (j, kk))
        dims = NT
    else:
        a_spec = pl.BlockSpec((tk, tm), lambda i, j, kk: (kk, i))
        b_spec = pl.BlockSpec((tk, tn), lambda i, j, kk: (kk, j))
        dims = TN
    nk = k // tk
    has_res = res is not None
    acc_in_out = out_dtype == F32 and not has_res

    def body(*refs):
        a_ref, b_ref = refs[0], refs[1]
        r_ref = refs[2] if has_res else None
        o_ref = refs[3] if has_res else refs[2]
        part = _dot(a_ref[...], b_ref[...], dims)

        def finish(val):
            if has_res:
                val = val + r_ref[...]
            o_ref[...] = val.astype(o_ref.dtype)

        if nk == 1:
            finish(part)
        elif acc_in_out:
            kk = pl.program_id(2)

            @pl.when(kk == 0)
            def _():
                o_ref[...] = part

            @pl.when(kk > 0)
            def _():
                o_ref[...] += part
        else:
            acc_ref = refs[-1]
            kk = pl.program_id(2)

            @pl.when(kk == 0)
            def _():
                acc_ref[...] = part

            @pl.when(kk > 0)
            def _():
                acc_ref[...] += part

            @pl.when(kk == nk - 1)
            def _():
                finish(acc_ref[...])

    in_specs = [a_spec, b_spec]
    args = [a, b]
    if has_res:
        in_specs.append(pl.BlockSpec((tm, tn), lambda i, j, kk: (i, j)))
        args.append(res)
    return pl.pallas_call(
        body, name=name, grid=(m // tm, n // tn, nk), in_specs=in_specs,
        out_specs=pl.BlockSpec((tm, tn), lambda i, j, kk: (i, j)),
        out_shape=jax.ShapeDtypeStruct((m, n), out_dtype),
        scratch_shapes=[pltpu.VMEM((tm, tn), F32)] if nk > 1 and not acc_in_out else [],
        compiler_params=_cparams(("arbitrary", "arbitrary", "arbitrary")),
    )(*args)


def _mm_sum_nt(name, pieces, w, extra, tm, tn):
    n_p = len(pieces)
    m, k = pieces[0].shape
    n = w.shape[0]
    xa, xb = extra
    ke = xa.shape[1]
    tm, tn = min(tm, m), min(tn, n)

    def body(*refs):
        p_refs, w_ref, xa_ref, xb_ref, o_ref = refs[:n_p], refs[n_p], refs[n_p + 1], refs[n_p + 2], refs[-1]
        kk = pl.program_id(2)

        @pl.when(kk == 0)
        def _():
            o_ref[...] = _dot(p_refs[0][...], w_ref[...], NT)

        for i in range(1, n_p):
            @pl.when(kk == i)
            def _(i=i):
                o_ref[...] += _dot(p_refs[i][...], w_ref[...], NT)

        @pl.when(kk == n_p)
        def _():
            o_ref[...] += _dot(xa_ref[...], xb_ref[...], NT)

    in_specs = [pl.BlockSpec((tm, k), lambda i, j, kk: (i, 0)) for _ in range(n_p)]
    in_specs.append(pl.BlockSpec((tn, k), lambda i, j, kk: (j, jnp.minimum(kk, n_p - 1))))
    in_specs += [pl.BlockSpec((tm, ke), lambda i, j, kk: (i, 0)), pl.BlockSpec((tn, ke), lambda i, j, kk: (j, 0))]
    return pl.pallas_call(
        body, name=name, grid=(m // tm, n // tn, n_p + 1), in_specs=in_specs,
        out_specs=pl.BlockSpec((tm, tn), lambda i, j, kk: (i, j)),
        out_shape=jax.ShapeDtypeStruct((m, n), F32),
        compiler_params=_cparams(("arbitrary", "arbitrary", "arbitrary")),
    )(*pieces, w, xa, xb)


def _rms_fwd(name, x, gain, tm=256):
    s = x.shape[0]

    def body(x_ref, g_ref, y_ref, yt_ref):
        xb = x_ref[...]
        y = xb * lax.rsqrt(jnp.mean(xb * xb, axis=-1, keepdims=True) + EPS) * g_ref[...]
        y_ref[...] = y.astype(BF16)
        yt_ref[...] = y.T.astype(BF16)

    return pl.pallas_call(
        body, name=name, grid=(s // tm,),
        in_specs=[pl.BlockSpec((tm, D), lambda i: (i, 0)), pl.BlockSpec((1, D), lambda i: (0, 0))],
        out_specs=[pl.BlockSpec((tm, D), lambda i: (i, 0)), pl.BlockSpec((D, tm), lambda i: (0, i))],
        out_shape=[jax.ShapeDtypeStruct((s, D), BF16), jax.ShapeDtypeStruct((D, s), BF16)],
        compiler_params=_cparams(("arbitrary",)),
    )(x, gain)


def _rms_bwd(name, x, gain, dns, dres, tm=256):
    s = x.shape[0]
    n_dn = len(dns)

    def fn(xb, g, *rest):
        dn = rest[0]
        for t in rest[1:n_dn]:
            dn = dn + t
        r = lax.rsqrt(jnp.mean(xb * xb, axis=-1, keepdims=True) + EPS)
        xhat = xb * r
        dxh = dn * g
        dx = r * (dxh - xhat * jnp.mean(dxh * xhat, axis=-1, keepdims=True)) + rest[n_dn]
        return (dx,), (jnp.sum(dn * xhat, axis=0, keepdims=True),)

    ins = [_rb(x, tm, D), _cst(gain)] + [_rb(t, tm, D) for t in dns] + [_rb(dres, tm, D)]
    return _rows(name, fn, s, tm, ins, [(D, F32)], [(1, D)])


def _final(h2, target, gain, tm=256):
    s = h2.shape[0]

    def fn(hb, tb, g):
        r = lax.rsqrt(jnp.mean(hb * hb, axis=-1, keepdims=True) + EPS)
        xhat = hb * r
        e = xhat * g - tb
        dy = e * (1.0 / D)
        dxh = dy * g
        dh = r * (dxh - xhat * jnp.mean(dxh * xhat, axis=-1, keepdims=True))
        lrow = 0.5 * jnp.sum(jnp.sum(e * e, axis=-1, keepdims=True) * (1.0 / D), axis=0, keepdims=True)
        return (dh,), (jnp.sum(dy * xhat, axis=0, keepdims=True), jnp.broadcast_to(lrow, (1, LANES)))

    return _rows("final_norm_loss", fn, s, tm, [_rb(h2, tm, D), _rb(target, tm, D), _cst(gain)],
                 [(D, F32)], [(1, D), (1, LANES)])


def _merge_fwd(pa, pb, proj, tm=256):
    s = pa.shape[0]

    def fn(a, b, ga, gb):
        return (_sigmoid(ga) * a + _sigmoid(gb) * b,), ()

    ins = [_rb(pa, tm, D), _rb(pb, tm, D), _rb(proj, tm, D, 7), _rb(proj, tm, D, 8)]
    return _rows("merge_fwd", fn, s, tm, ins, [(D, BF16)])[0]


def _merge_bwd(dmerged, pa, pb, proj, tm=256):
    s = pa.shape[0]

    def fn(dm, a, b, ga, gb):
        sa, sb = _sigmoid(ga), _sigmoid(gb)
        return (dm * sa, dm * sb, dm * a * sa * (1.0 - sa), dm * b * sb * (1.0 - sb)), ()

    ins = [_rb(dmerged, tm, D), _rb(pa, tm, D), _rb(pb, tm, D), _rb(proj, tm, D, 7), _rb(proj, tm, D, 8)]
    return _rows("merge_bwd", fn, s, tm, ins, [(D, BF16), (D, BF16), (D, BF16), (D, BF16)])


def _gelu_parts(x):
    cdf = 0.5 * (1.0 + lax.erf(x * 0.7071067811865476))
    pdf = 0.3989422804014327 * jnp.exp(-0.5 * x * x)
    return x * cdf, cdf + x * pdf


def _conv_taps(u_ext, n_out):
    cur = u_ext[8:8 + n_out]
    m1 = pltpu.roll(u_ext, 1, 0)[8:8 + n_out]
    m2 = pltpu.roll(u_ext, 2, 0)[8:8 + n_out]
    return m2, m1, cur


def _convglu_fwd(u, conv_w8, conv_b, tm=256):
    s, w = u.shape
    tb = tm // 8

    def fn(ub, up, cw, cb):
        i = pl.program_id(0)
        up = jnp.where(i == 0, 0.0, up)
        m2, m1, cur = _conv_taps(jnp.concatenate([up, ub], axis=0), tm)
        acc = cb + cw[0:1] * m2 + cw[1:2] * m1 + cw[2:3] * cur
        gl, dgl = _gelu_parts(acc[:, :D_FF])
        val = acc[:, D_FF:]
        return (gl * val, gl, val * dgl), ()

    ins = [_rb(u, tm, w), (u, (8, w), lambda i: (jnp.maximum(i * tb - 1, 0), 0)), _cst(conv_w8), _cst(conv_b)]
    return _rows("convglu_fwd", fn, s, tm, ins, [(D_FF, BF16)] * 3)


def _convglu_bwd(u, dact, gl, gd, conv_w8, tm=256):
    s, w = u.shape
    tb = tm // 8
    nb = s // tm

    def fn(ub, up, db, dn, glb, gln, gdb, gdn, cw):
        i = pl.program_id(0)
        up = jnp.where(i == 0, 0.0, up)
        dn = jnp.where(i == nb - 1, 0.0, dn.astype(F32))
        ne = tm + 8
        m2, m1, cur = _conv_taps(jnp.concatenate([up, ub], axis=0), tm)
        ext = lambda blk, nxt: jnp.concatenate([blk.astype(F32), nxt.astype(F32)], axis=0)
        de = ext(db, dn)
        dacc = jnp.concatenate([de * ext(gdb, gdn), de * ext(glb, gln)], axis=1)
        p1 = pltpu.roll(dacc, ne - 1, 0)[:tm]
        p2 = pltpu.roll(dacc, ne - 2, 0)[:tm]
        d0 = dacc[:tm]
        du = cw[2:3] * d0 + cw[1:2] * p1 + cw[0:1] * p2
        zero5 = jnp.zeros((5, w), F32)
        dcw = jnp.concatenate([
            jnp.sum(d0 * m2, axis=0, keepdims=True), jnp.sum(d0 * m1, axis=0, keepdims=True),
            jnp.sum(d0 * cur, axis=0, keepdims=True), zero5], axis=0)
        return (du,), (dcw, jnp.sum(d0, axis=0, keepdims=True))

    nxt = lambda arr: (arr, (8, D_FF), lambda i: (jnp.minimum((i + 1) * tb, s // 8 - 1), 0))
    ins = [_rb(u, tm, w), (u, (8, w), lambda i: (jnp.maximum(i * tb - 1, 0), 0)),
           _rb(dact, tm, D_FF), nxt(dact), _rb(gl, tm, D_FF), nxt(gl), _rb(gd, tm, D_FF), nxt(gd), _cst(conv_w8)]
    return _rows("convglu_bwd", fn, s, tm, ins, [(w, BF16)], [(8, w), (1, w)])


def _chunk_scan(x, t_iota, reverse):
    k = 1
    while k < HG_C:
        if reverse:
            x = x + jnp.where(t_iota < HG_C - k, pltpu.roll(x, HG_C - k, 0), 0.0)
        else:
            x = x + jnp.where(t_iota >= k, pltpu.roll(x, k, 0), 0.0)
        k *= 2
    return x


def _hg_gates(hq, hf, lb):
    sq = _sigmoid(hq)
    q = hq * sq
    sg = _sigmoid(hf)
    f = lb + (1.0 - lb) * sg
    return q, sq, sg, f, 1.0 - f, jnp.log(f)


def _lb_of(logits):
    l0, l1 = logits[0:1], logits[1:2]
    mx = jnp.maximum(l0, l1)
    e0, e1 = jnp.exp(l0 - mx), jnp.exp(l1 - mx)
    return e0 / (e0 + e1)


def _tri(n, lower):
    r = lax.broadcasted_iota(jnp.int32, (n, n), 0)
    c = lax.broadcasted_iota(jnp.int32, (n, n), 1)
    return jnp.where((r >= c) if lower else (r <= c), 1.0, 0.0).astype(BF16)


def _hg_intra_terms(q, kk, b, t_iota):
    ws, ps = [], []
    for s in range(HG_C):
        p = jnp.where(t_iota >= s, jnp.exp(b - b[s:s + 1]), 0.0)
        ps.append(p)
        ws.append(q * kk[s:s + 1] * p)
    return jnp.concatenate(ws, axis=0), ps


def _hg_fwd(proj, lb_logits):
    s = proj.shape[0]
    nt = s // HG_T
    nc = HG_T // HG_C

    def body(q_ref, f_ref, i_ref, l_ref, o_ref, st_ref, state):
        @pl.when(pl.program_id(1) == 0)
        def _():
            state[...] = jnp.zeros_like(state)

        st_ref[0, 0] = state[...]
        lb = _lb_of(l_ref[...])
        ones = jnp.ones((HG_DK, HG_DK), BF16)
        t_iota = lax.broadcasted_iota(jnp.int32, (HG_C, HG_DK), 0)
        cc = HG_C * HG_C

        def group(gi, st):
            units = []
            for u in range(HG_UNROLL):
                r = pl.ds(pl.multiple_of((gi * HG_UNROLL + u) * HG_C, HG_C), HG_C)
                q, _, _, _, kk, g = _hg_gates(q_ref[r, :], f_ref[r, :], lb)
                b = _chunk_scan(g, t_iota, False)
                b_end = b[HG_C - 1:HG_C]
                w_all, _ = _hg_intra_terms(q, kk, b, t_iota)
                units.append((r, i_ref[r, :], q * jnp.exp(b), jnp.exp(b_end), kk * jnp.exp(b_end - b), w_all))
            a_all = _dot(jnp.concatenate([un[5] for un in units], axis=0), ones, NN)
            kvs = [_dot(v, kd, TN) for (_, v, _, _, kd, _) in units]
            sts = [st]
            for (_, _, _, dec, _, _), kv in zip(units, kvs):
                sts.append(sts[-1] * dec + kv)
            for ui, (r, v, qd, _, _, _) in enumerate(units):
                o = _dot(qd, sts[ui], NT)
                for si in range(HG_C):
                    o = o + a_all[ui * cc + si * HG_C:ui * cc + (si + 1) * HG_C] * v[si:si + 1]
                o_ref[r, :] = o
            return sts[-1]

        state[...] = lax.fori_loop(0, nc // HG_UNROLL, group, state[...])

    col = lambda off: pl.BlockSpec((HG_T, HG_DK), lambda h, t: (t, off + h))
    return pl.pallas_call(
        body, name="hgrn2_fwd", grid=(HG_H, nt),
        in_specs=[col(0), col(8), col(16), pl.BlockSpec((2, HG_DK), lambda h, t: (0, h))],
        out_specs=[pl.BlockSpec((HG_T, HG_DK), lambda h, t: (t, h)),
                   pl.BlockSpec((1, 1, HG_DK, HG_DK), lambda h, t: (h, t, 0, 0))],
        out_shape=[jax.ShapeDtypeStruct((s, D), F32), jax.ShapeDtypeStruct((HG_H, nt, HG_DK, HG_DK), F32)],
        scratch_shapes=[pltpu.VMEM((HG_DK, HG_DK), F32)],
        compiler_params=_cparams(("arbitrary", "arbitrary")),
    )(proj, proj, proj, lb_logits)


def _hg_bwd(proj, lb_logits, states, do_raw):
    s = proj.shape[0]
    nt = s // HG_T
    nc = HG_T // HG_C

    def body(q_ref, f_ref, i_ref, l_ref, st_ref, do_ref, dq_ref, df_ref, di_ref, dl_ref, st_all, adj):
        tb = pl.program_id(1)

        @pl.when(tb == 0)
        def _():
            adj[...] = jnp.zeros_like(adj)
            dl_ref[...] = jnp.zeros_like(dl_ref)

        lb = _lb_of(l_ref[...])
        ones = jnp.ones((HG_DK, HG_DK), BF16)
        t_iota = lax.broadcasted_iota(jnp.int32, (HG_C, HG_DK), 0)
        cc = HG_C * HG_C

        def fwd_group(gi, st):
            terms = []
            for u in range(HG_UNROLL):
                ci = gi * HG_UNROLL + u
                r = pl.ds(pl.multiple_of(ci * HG_C, HG_C), HG_C)
                _, _, _, _, kk, g = _hg_gates(q_ref[r, :], f_ref[r, :], lb)
                b = _chunk_scan(g, t_iota, False)
                b_end = b[HG_C - 1:HG_C]
                terms.append((ci, jnp.exp(b_end), _dot(i_ref[r, :], kk * jnp.exp(b_end - b), TN)))
            for ci, dec, kv in terms:
                st_all[ci] = st
                st = st * dec + kv
            return st

        lax.fori_loop(0, nc // HG_UNROLL, fwd_group, st_ref[0, 0])

        def bwd_group(gj, dlb):
            units = []
            for u in range(HG_UNROLL_BWD):
                ci = nc - 1 - (gj * HG_UNROLL_BWD + u)
                r = pl.ds(pl.multiple_of(ci * HG_C, HG_C), HG_C)
                hq, hf, v, do = q_ref[r, :], f_ref[r, :], i_ref[r, :], do_ref[r, :]
                q, sq, sg, f, kk, g = _hg_gates(hq, hf, lb)
                b = _chunk_scan(g, t_iota, False)
                b_end = b[HG_C - 1:HG_C]
                e_b, e_be, dec = jnp.exp(b), jnp.exp(b_end - b), jnp.exp(b_end)
                w_all, ps = _hg_intra_terms(q, kk, b, t_iota)
                x_all = jnp.concatenate([do * v[si:si + 1] for si in range(HG_C)], axis=0)
                units.append(dict(ci=ci, r=r, hq=hq, v=v, do=do, q=q, sq=sq, sg=sg, f=f, kk=kk, e_b=e_b, e_be=e_be,
                                  dec=dec, kd=kk * e_be, w=w_all, ps=ps, x=x_all))
            both = _dot(jnp.concatenate([un["w"] for un in units] + [un["x"] for un in units], axis=0), ones, NN)
            st0s = [st_all[un["ci"]] for un in units]
            st_ends = [st0 * un["dec"] + _dot(un["v"], un["kd"], TN) for un, st0 in zip(units, st0s)]
            dqks = [_dot(un["do"], un["q"] * un["e_b"], TN) for un in units]
            es = [adj[...]]
            for un, dqk in zip(units, dqks):
                es.append(es[-1] * un["dec"] + dqk)
            adj[...] = es[-1]
            for ui, un in enumerate(units):
                e, q, kk, v, do = es[ui], un["q"], un["kk"], un["v"], un["do"]
                tail = jnp.sum(e * st_ends[ui], axis=0, keepdims=True)
                dq = un["e_b"] * _dot(do, st0s[ui], NN)
                dk = un["e_be"] * _dot(v, e, NN)
                dv = _dot(un["kd"], e, NT)
                a0 = ui * cc
                d0 = (HG_UNROLL_BWD + ui) * cc
                for si in range(HG_C):
                    da = both[d0 + si * HG_C:d0 + (si + 1) * HG_C]
                    aa = both[a0 + si * HG_C:a0 + (si + 1) * HG_C]
                    dap = da * un["ps"][si]
                    dq = dq + dap * kk[si:si + 1]
                    hit = t_iota == si
                    dk = dk + jnp.where(hit, jnp.sum(dap * q, axis=0, keepdims=True), 0.0)
                    dv = dv + jnp.where(hit, jnp.sum(aa * do, axis=0, keepdims=True), 0.0)
                dg = _chunk_scan(q * dq - kk * dk, t_iota, True) + tail
                dfg = dg / un["f"] - dk
                sq, sg, hq, r = un["sq"], un["sg"], un["hq"], un["r"]
                dq_ref[r, :] = (dq * sq * (1.0 + hq * (1.0 - sq))).astype(dq_ref.dtype)
                df_ref[r, :] = (dfg * (1.0 - lb) * sg * (1.0 - sg)).astype(df_ref.dtype)
                di_ref[r, :] = dv.astype(di_ref.dtype)
                dlb = dlb + jnp.sum(dfg * (1.0 - sg), axis=0, keepdims=True)
            return dlb

        dlb = lax.fori_loop(0, nc // HG_UNROLL_BWD, bwd_group, jnp.zeros((1, HG_DK), F32))
        dl0 = dlb * lb * (1.0 - lb)
        dl_ref[...] += jnp.concatenate([dl0, -dl0], axis=0)

    col = lambda off: pl.BlockSpec((HG_T, HG_DK), lambda h, t: (nt - 1 - t, off + h))
    out_col = pl.BlockSpec((HG_T, HG_DK), lambda h, t: (nt - 1 - t, h))
    return pl.pallas_call(
        body, name="hgrn2_bwd", grid=(HG_H, nt),
        in_specs=[col(0), col(8), col(16), pl.BlockSpec((2, HG_DK), lambda h, t: (0, h)),
                  pl.BlockSpec((1, 1, HG_DK, HG_DK), lambda h, t: (h, nt - 1 - t, 0, 0)), col(0)],
        out_specs=[out_col, out_col, out_col, pl.BlockSpec((2, HG_DK), lambda h, t: (0, h))],
        out_shape=[jax.ShapeDtypeStruct((s, D), BF16)] * 3 + [jax.ShapeDtypeStruct((2, D), F32)],
        scratch_shapes=[pltpu.VMEM((nc, HG_DK, HG_DK), F32), pltpu.VMEM((HG_DK, HG_DK), F32)],
        compiler_params=_cparams(("arbitrary", "arbitrary")),
    )(proj, proj, proj, lb_logits, states, do_raw)


def _hg_post_fwd(o_raw, proj, gnorm, tm=256):
    s = o_raw.shape[0]

    def fn(o, hg, gn):
        outs = []
        for h in range(HG_H):
            sl = slice(h * HG_DK, (h + 1) * HG_DK)
            oh, gh = o[:, sl], hg[:, sl]
            r = lax.rsqrt(jnp.mean(oh * oh, axis=-1, keepdims=True) + EPS)
            outs.append(oh * r * gn * (gh * _sigmoid(gh)))
        return (jnp.concatenate(outs, axis=1),), ()

    return _rows("hgrn2_out_fwd", fn, s, tm, [_rb(o_raw, tm, D), _rb(proj, tm, D, 3), _cst(gnorm)], [(D, BF16)])[0]


def _hg_post_bwd(do_a, o_raw, proj, gnorm, tm=256):
    s = o_raw.shape[0]

    def fn(da, o, hg, gn):
        dos, dhgs = [], []
        dgn = jnp.zeros((1, HG_DK), F32)
        for h in range(HG_H):
            sl = slice(h * HG_DK, (h + 1) * HG_DK)
            oh, gh, dh = o[:, sl], hg[:, sl], da[:, sl]
            r = lax.rsqrt(jnp.mean(oh * oh, axis=-1, keepdims=True) + EPS)
            xhat = oh * r
            sg = _sigmoid(gh)
            dy = dh * (gh * sg)
            dhgs.append(dh * xhat * gn * sg * (1.0 + gh * (1.0 - sg)))
            dgn = dgn + jnp.sum(dy * xhat, axis=0, keepdims=True)
            dxh = dy * gn
            dos.append(r * (dxh - xhat * jnp.mean(dxh * xhat, axis=-1, keepdims=True)))
        return (jnp.concatenate(dos, axis=1), jnp.concatenate(dhgs, axis=1)), (dgn,)

    ins = [_rb(do_a, tm, D), _rb(o_raw, tm, D), _rb(proj, tm, D, 3), _cst(gnorm)]
    return _rows("hgrn2_out_bwd", fn, s, tm, ins, [(D, F32), (D, BF16)], [(1, HG_DK)])


def _log_sigmoid(z):
    return jnp.minimum(z, 0.0) - jnp.log(1.0 + jnp.exp(-jnp.abs(z)))


def _fox_gate_bwd(dct, pff, bias, tm=256):
    s = pff.shape[0]
    nb = s // tm

    def body(d_ref, p_ref, b_ref, dff_ref, db_ref, carry):
        @pl.when(pl.program_id(0) == 0)
        def _():
            carry[...] = jnp.zeros_like(carry)
            db_ref[...] = jnp.zeros_like(db_ref)

        dc = d_ref[...].T
        dlf = _split_dot(_tri(tm, False), dc, 3) + carry[0:1]
        carry[...] = jnp.broadcast_to(dlf[0:1], carry.shape)
        dff = dlf * _sigmoid(-(p_ref[...] + b_ref[...]))
        dff_ref[...] = dff
        db_ref[...] += jnp.sum(dff, axis=0, keepdims=True)

    return pl.pallas_call(
        body, name="fox_gate_bwd", grid=(nb,),
        in_specs=[pl.BlockSpec((LANES, tm), lambda i: (0, nb - 1 - i)),
                  pl.BlockSpec((tm, LANES), lambda i: (nb - 1 - i, 0)), pl.BlockSpec((1, LANES), lambda i: (0, 0))],
        out_specs=[pl.BlockSpec((tm, LANES), lambda i: (nb - 1 - i, 0)), pl.BlockSpec((1, LANES), lambda i: (0, 0))],
        out_shape=[jax.ShapeDtypeStruct((s, LANES), F32), jax.ShapeDtypeStruct((1, LANES), F32)],
        scratch_shapes=[pltpu.VMEM((8, LANES), F32)],
        compiler_params=_cparams(("arbitrary",)),
    )(dct, pff, bias)


def _diag_mask(t):
    r = lax.broadcasted_iota(jnp.int32, (t, t), 0)
    c = lax.broadcasted_iota(jnp.int32, (t, t), 1)
    return r >= c


AUX_ONES = 6


def _pieces(x):
    h = x.astype(BF16)
    r = x - h.astype(F32)
    m = r.astype(BF16)
    return h, m, (r - m.astype(F32)).astype(BF16)


def _lane_put(lane, cols, base):
    out = None
    for i, col in enumerate(cols):
        term = jnp.where(lane == base + i, col.astype(F32), 0.0)
        out = term if out is None else out + term
    return out


def _fox_prep2(proj, pff, bias, tm=256):
    s = pff.shape[0]

    def body(q_ref, k_ref, v_ref, p_ref, b_ref, qb_ref, kb_ref, vb_ref, ka_ref, carry):
        @pl.when(pl.program_id(0) == 0)
        def _():
            carry[...] = jnp.zeros_like(carry)

        qb_ref[...] = (q_ref[...] * 0.125).astype(BF16)
        kb_ref[...] = k_ref[...].astype(BF16)
        vb_ref[...] = v_ref[...].astype(BF16)
        lf = _log_sigmoid(p_ref[...] + b_ref[...])
        c = _split_dot(_tri(tm, True), lf, 3) + carry[0:1]
        carry[...] = jnp.broadcast_to(c[tm - 1:tm], carry.shape)
        lane = lax.broadcasted_iota(jnp.int32, (tm, LANES), 1)
        ones = jnp.where((lane >= AUX_ONES) & (lane < AUX_ONES + 6), 1.0, 0.0)
        for p in range(FOX_H // 2):
            aux = ones
            for z in range(2):
                col = jnp.sum(jnp.where(lane == 2 * p + z, c, 0.0), axis=1, keepdims=True)
                aux = aux + _lane_put(lane, _pieces(-col), 3 * z)
            ka_ref[:, p * LANES:(p + 1) * LANES] = aux.astype(BF16)

    row = lambda cb: pl.BlockSpec((tm, D), lambda i: (i, cb))
    return pl.pallas_call(
        body, name="fox_prep", grid=(s // tm,),
        in_specs=[row(4), row(5), row(6), pl.BlockSpec((tm, LANES), lambda i: (i, 0)),
                  pl.BlockSpec((1, LANES), lambda i: (0, 0))],
        out_specs=[row(0)] * 4, out_shape=[jax.ShapeDtypeStruct((s, D), BF16)] * 4,
        scratch_shapes=[pltpu.VMEM((8, LANES), F32)],
        compiler_params=_cparams(("arbitrary",)),
    )(proj, proj, proj, pff, bias)


def _fox_fwd2(qb, kb, vb, ka):
    s = qb.shape[0]
    t = min(FOX_T, s)
    nq = s // t

    def body(q_ref, k_ref, v_ref, ka_ref, o_ref, la_ref):
        i = pl.program_id(1)
        lane = lax.broadcasted_iota(jnp.int32, (t, LANES), 1)
        in_a = lane < FOX_D
        q = q_ref[...]
        zero = jnp.zeros_like(q)
        qh = [jnp.where(in_a, q, zero), jnp.where(in_a, zero, q)]
        c_ones = [jnp.where((lane >= 3 * z) & (lane < 3 * z + 3), 1.0, 0.0) for z in range(2)]

        def keys(j):
            rows = pl.ds(pl.multiple_of(j * t, t), t)
            return jnp.concatenate([k_ref[rows, :], ka_ref[rows, :]], axis=1), rows

        dmask = _diag_mask(t)

        def logits(qx, kk, masked):
            e = lax.dot_general(qx, kk, (NT, ((), ())), preferred_element_type=F32)
            return jnp.where(dmask, e, -1e30) if masked else e

        qc = [jnp.concatenate([qh[z], c_ones[z].astype(BF16)], axis=1) for z in range(2)]

        def step(j, carry, masked):
            kk, rows = keys(j)
            vj = v_ref[rows, :]
            scores = [logits(qc[z], kk, masked) for z in range(2)]
            one = jnp.ones_like(vj)
            vh = [jnp.where(in_a, vj, one), jnp.where(in_a, one, vj)]
            out = []
            for z in range(2):
                m, acc = carry[z]
                m_new = jnp.maximum(m, jnp.max(scores[z], axis=1, keepdims=True))
                p = jnp.exp(scores[z] - m_new)
                out.append((m_new, jnp.exp(m - m_new) * acc + _dot(p, vh[z], NN)))
            return tuple(out)

        init = tuple((jnp.full((t, 1), -1e30, F32), jnp.zeros((t, LANES), F32)) for _ in range(2))
        (ma, acc_a), (mb, acc_b) = step(i, lax.fori_loop(0, i, lambda j, c: step(j, c, False), init), True)
        la = jnp.sum(jnp.where(lane == FOX_D, acc_a, 0.0), axis=1, keepdims=True)
        lb = jnp.sum(jnp.where(lane == 0, acc_b, 0.0), axis=1, keepdims=True)
        o_ref[...] = jnp.where(in_a, acc_a / la, acc_b / lb).astype(o_ref.dtype)
        la_ref[...] = (_lane_put(lane, _pieces(-(ma + jnp.log(la))), AUX_ONES)
                       + _lane_put(lane, _pieces(-(mb + jnp.log(lb))), AUX_ONES + 3)).astype(la_ref.dtype)

    blk = pl.BlockSpec((t, LANES), lambda p, i: (i, p))
    whole = pl.BlockSpec((s, LANES), lambda p, i: (0, p))
    return pl.pallas_call(
        body, name="fox_attn_fwd", grid=(FOX_H // 2, nq), in_specs=[blk, whole, whole, whole],
        out_specs=[blk, blk], out_shape=[jax.ShapeDtypeStruct((s, D), BF16)] * 2,
        compiler_params=_cparams(("arbitrary", "arbitrary")),
    )(qb, kb, vb, ka)


def _fox_bwd2(qb, kb, vb, ka, ob, laux, dob):
    s = qb.shape[0]
    t = min(FOX_T, s)
    nq = s // t

    def body(q_ref, k_ref, v_ref, ka_ref, o_ref, la_ref, do_ref, dq_ref, dk_ref, dv_ref, dc_ref, dkt, dvt):
        i = pl.program_id(1)

        @pl.when(i == 0)
        def _():
            dkt[...] = jnp.zeros_like(dkt)
            dvt[...] = jnp.zeros_like(dvt)
            dc_ref[...] = jnp.zeros_like(dc_ref)

        lane = lax.broadcasted_iota(jnp.int32, (t, LANES), 1)
        in_a = lane < FOX_D
        q, do, la = q_ref[...], do_ref[...], la_ref[...].astype(F32)
        zero = jnp.zeros_like(q)
        qh = [jnp.where(in_a, q, zero), jnp.where(in_a, zero, q)]
        doh = [jnp.where(in_a, do, zero), jnp.where(in_a, zero, do)]
        qt = [h.astype(F32).T.astype(BF16) for h in qh]
        dot_ = [h.astype(F32).T.astype(BF16) for h in doh]
        prod = do.astype(F32) * o_ref[...].astype(F32)
        qx, dox = [], []
        for z in range(2):
            delta = jnp.sum(jnp.where(in_a if z == 0 else ~in_a, prod, 0.0), axis=1, keepdims=True)
            c_ones = jnp.where((lane >= 3 * z) & (lane < 3 * z + 3), 1.0, 0.0)
            lse_lanes = (lane >= AUX_ONES + 3 * z) & (lane < AUX_ONES + 3 * z + 3)
            qx.append(jnp.concatenate([qh[z], (c_ones + jnp.where(lse_lanes, la, 0.0)).astype(BF16)], axis=1))
            dox.append(jnp.concatenate([doh[z], _lane_put(lane, _pieces(-delta), 3 * z).astype(BF16)], axis=1))
        v_ones = jnp.where(lane < 6, 1.0, 0.0).astype(BF16)
        dmask = _diag_mask(t)

        def step(j, carry, masked):
            rows = pl.ds(pl.multiple_of(j * t, t), t)
            kj, vj = k_ref[rows, :], v_ref[rows, :]
            kk = jnp.concatenate([kj, ka_ref[rows, :]], axis=1)
            vv = jnp.concatenate([vj, v_ones], axis=1)
            out = []
            dk_add, dv_add = None, None
            for z in range(2):
                dq, rsum = carry[z]
                e = lax.dot_general(qx[z], kk, (NT, ((), ())), preferred_element_type=F32)
                if masked:
                    e = jnp.where(dmask, e, -1e30)
                p = jnp.exp(e)
                ds = p * lax.dot_general(dox[z], vv, (NT, ((), ())), preferred_element_type=F32)
                dkz, dvz = _dot(qt[z], ds, NN), _dot(dot_[z], p, NN)
                dk_add = dkz if dk_add is None else dk_add + dkz
                dv_add = dvz if dv_add is None else dv_add + dvz
                dc_ref[0, z, j] += -jnp.sum(ds, axis=0, keepdims=True)
                out.append((dq + _dot(ds, kj, NN), rsum + jnp.sum(ds, axis=1, keepdims=True)))
            dkt[j] += dk_add
            dvt[j] += dv_add
            return tuple(out)

        init = tuple((jnp.zeros((t, LANES), F32), jnp.zeros((t, 1), F32)) for _ in range(2))
        (dq_a, rs_a), (dq_b, rs_b) = step(i, lax.fori_loop(0, i, lambda j, c: step(j, c, False), init), True)
        for z, rs in enumerate((rs_a, rs_b)):
            dc_ref[0, z, i] += jnp.transpose(jnp.broadcast_to(rs, (t, LANES)))[0:1]
        dq_ref[...] = (jnp.where(in_a, dq_a, dq_b) * 0.125).astype(dq_ref.dtype)

        @pl.when(i == nq - 1)
        def _():
            for jb in range(nq):
                dk_ref[jb * t:(jb + 1) * t, :] = dkt[jb].T.astype(dk_ref.dtype)
                dv_ref[jb * t:(jb + 1) * t, :] = dvt[jb].T.astype(dv_ref.dtype)

    blk = pl.BlockSpec((t, LANES), lambda p, i: (i, p))
    whole = pl.BlockSpec((s, LANES), lambda p, i: (0, p))
    return pl.pallas_call(
        body, name="fox_attn_bwd", grid=(FOX_H // 2, nq),
        in_specs=[blk, whole, whole, whole, blk, blk, blk],
        out_specs=[blk, whole, whole, pl.BlockSpec((1, 2, nq, 1, t), lambda p, i: (p, 0, 0, 0, 0))],
        out_shape=[jax.ShapeDtypeStruct((s, D), BF16)] * 3 + [jax.ShapeDtypeStruct((FOX_H // 2, 2, nq, 1, t), F32)],
        scratch_shapes=[pltpu.VMEM((nq, LANES, t), F32), pltpu.VMEM((nq, LANES, t), F32)],
        compiler_params=_cparams(("arbitrary", "arbitrary")),
    )(qb, kb, vb, ka, ob, laux, dob)


def _adamw(name, w, g, m, v, tm=None):
    rows, width = w.shape
    tm = rows if tm is None else tm
    c1 = 1.0 - ADAM_B1 ** ADAM_STEP
    c2 = 1.0 - ADAM_B2 ** ADAM_STEP

    def fn(wb, gb, mb, vb):
        m_new = ADAM_B1 * mb + (1.0 - ADAM_B1) * gb
        v_new = ADAM_B2 * vb + (1.0 - ADAM_B2) * (gb * gb)
        delta = -ADAM_LR * ((m_new / c1) / (jnp.sqrt(v_new / c2) + ADAM_EPS) + ADAM_WD * wb)
        return (delta, m_new, v_new), ()

    ins = [_rb(a, tm, width) for a in (w, g, m, v)]
    return _rows(name, fn, rows, tm, ins, [(width, F32)] * 3)


def _me():
    return lax.axis_index("x"), lax.axis_index("y"), lax.axis_index("c")


def _all_reduce_small(name, block):
    r, n = block.shape

    def body(x_ref, sum_ref, gath, send_sems, recv_sems):
        x, y, c = _me()
        me = 4 * x + 2 * y + c
        gath[me] = x_ref[...]
        sends = []
        for k in range(1, 8):
            px = x ^ ((k >> 2) & 1)
            py = y ^ ((k >> 1) & 1)
            pc = c ^ (k & 1)
            sends.append(pltpu.make_async_remote_copy(
                src_ref=x_ref, dst_ref=gath.at[me], send_sem=send_sems.at[k - 1], recv_sem=recv_sems.at[k - 1],
                device_id=(px, py, pc), device_id_type=MESH))
        for cp in sends:
            cp.start()
        for k in range(1, 8):
            peer = me ^ k
            pltpu.make_async_remote_copy(
                src_ref=x_ref, dst_ref=gath.at[peer], send_sem=send_sems.at[k - 1], recv_sem=recv_sems.at[k - 1],
                device_id=(x, y, c), device_id_type=MESH).wait_recv()
        for cp in sends:
            cp.wait_send()
        acc = gath[0]
        for d in range(1, 8):
            acc = acc + gath[d]
        sum_ref[...] = acc

    vm = pl.BlockSpec(memory_space=pltpu.VMEM)
    return pl.pallas_call(
        body, name=name, in_specs=[vm], out_specs=vm, out_shape=jax.ShapeDtypeStruct((r, n), F32),
        scratch_shapes=[pltpu.VMEM((8, r, n), F32), pltpu.SemaphoreType.DMA((7,)), pltpu.SemaphoreType.DMA((7,))],
    )(block)


WD_EXT_ROWS = 736


def _remote(src, dst, send_sems, recv_sems, k, to):
    return pltpu.make_async_remote_copy(src_ref=src, dst_ref=dst, send_sem=send_sems.at[k], recv_sem=recv_sems.at[k],
                                        device_id=to, device_id_type=MESH)


def _all_gather8_multi(name, blocks):
    nt = len(blocks)

    def body(*refs):
        x_refs, out_refs, send_sems, recv_sems = refs[:nt], refs[nt:2 * nt], refs[-2], refs[-1]
        x, y, c = _me()
        me, sibling = (x, y, c), (x, y, 1 - c)
        chips = [(1 - x, y), (x, 1 - y), (1 - x, 1 - y)]
        slot = lambda q, p: out_refs[q].at[4 * p[0] + 2 * p[1] + p[2]]

        def copies(k, blk, to, from_input=False):
            return [_remote(x_refs[q] if from_input else slot(q, blk), slot(q, blk), send_sems, recv_sems, k * nt + q, to)
                    for q in range(nt)]

        first = copies(0, me, sibling, True)
        for j, chip in enumerate(chips):
            first += copies(1 + j, me, (*chip, c), True)
        for cp in first:
            cp.start()
        passed = []
        for j, chip in enumerate(chips):
            for cp in copies(1 + j, (*chip, c), me):
                cp.wait_recv()
            fwd = copies(4 + j, (*chip, c), sibling)
            for cp in fwd:
                cp.start()
            passed += fwd
        for cp in copies(0, sibling, me):
            cp.wait_recv()
        back = copies(7, sibling, sibling)
        for cp in back:
            cp.start()
        for j, chip in enumerate(chips):
            for cp in copies(4 + j, (*chip, 1 - c), me):
                cp.wait_recv()
        for cp in copies(7, me, me):
            cp.wait_recv()
        for cp in first + passed + back:
            cp.wait_send()

    return pl.pallas_call(
        body, name=name, in_specs=[ANY] * nt, out_specs=[ANY] * nt,
        out_shape=[jax.ShapeDtypeStruct((8,) + b.shape, b.dtype) for b in blocks],
        scratch_shapes=[pltpu.SemaphoreType.DMA((8 * nt,)), pltpu.SemaphoreType.DMA((8 * nt,))],
    )(*blocks)


def _swap_halves_multi(name, gs):
    nt = len(gs)
    n_chip = gs[0].shape[0]

    def body(*refs):
        g_refs, got_refs, send_sems, recv_sems = refs[:nt], refs[nt:2 * nt], refs[-2], refs[-1]
        x, y, c = _me()
        cps = [_remote(g_refs[q].at[j, 1 - c], got_refs[q].at[j], send_sems, recv_sems, q * n_chip + j, (x, y, 1 - c))
               for q in range(nt) for j in range(n_chip)]
        for cp in cps:
            cp.start()
        for cp in cps:
            cp.wait()

    return pl.pallas_call(
        body, name=name, in_specs=[ANY] * nt, out_specs=[ANY] * nt,
        out_shape=[jax.ShapeDtypeStruct((g.shape[0],) + g.shape[2:], g.dtype) for g in gs],
        scratch_shapes=[pltpu.SemaphoreType.DMA((nt * n_chip,)), pltpu.SemaphoreType.DMA((nt * n_chip,))],
    )(*gs)


def _swap_sibling_multi(name, xs):
    nt = len(xs)

    def body(*refs):
        x_refs, out_refs, send_sems, recv_sems = refs[:nt], refs[nt:2 * nt], refs[-2], refs[-1]
        x, y, c = _me()
        cps = [_remote(x_refs[q], out_refs[q], send_sems, recv_sems, q, (x, y, 1 - c)) for q in range(nt)]
        for cp in cps:
            cp.start()
        for cp in cps:
            cp.wait()

    return pl.pallas_call(
        body, name=name, in_specs=[ANY] * nt, out_specs=[ANY] * nt,
        out_shape=[jax.ShapeDtypeStruct(a.shape, a.dtype) for a in xs],
        scratch_shapes=[pltpu.SemaphoreType.DMA((nt,)), pltpu.SemaphoreType.DMA((nt,))],
    )(*xs)


def _chip_exchange_multi(name, ps):
    nt = len(ps)

    def body(*refs):
        p_refs, out_refs, bounce_refs = refs[:nt], refs[nt:2 * nt], refs[2 * nt:3 * nt]
        send_sems, recv_sems = refs[-2], refs[-1]
        x, y, c = _me()
        my_chip = 2 * x + y
        sibling = (x, y, 1 - c)
        chips = [(1 - x, y), (x, 1 - y), (1 - x, 1 - y)]
        cp = lambda k, q, src, dst, to: _remote(src, dst, send_sems, recv_sems, k * nt + q, to)
        sends = [cp(k, q, p_refs[q].at[2 * px + py], out_refs[q].at[my_chip], (px, py, c))
                 for k, (px, py) in enumerate(chips) for q in range(nt)]
        sends += [cp(3, q, p_refs[q].at[my_chip], bounce_refs[q], sibling) for q in range(nt)]
        for s_ in sends:
            s_.start()
        backs = []
        for q in range(nt):
            cp(3, q, p_refs[q].at[my_chip], bounce_refs[q], sibling).wait_recv()
            backs.append(cp(4, q, bounce_refs[q], out_refs[q].at[my_chip], sibling))
            backs[-1].start()
        for k, (px, py) in enumerate(chips):
            for q in range(nt):
                cp(k, q, p_refs[q].at[my_chip], out_refs[q].at[2 * px + py], (px, py, c)).wait_recv()
        for q in range(nt):
            cp(4, q, bounce_refs[q], out_refs[q].at[my_chip], sibling).wait_recv()
        for s_ in sends + backs:
            s_.wait_send()

    outs = pl.pallas_call(
        body, name=name, in_specs=[ANY] * nt, out_specs=[ANY] * (2 * nt),
        out_shape=[jax.ShapeDtypeStruct(p.shape, p.dtype) for p in ps]
        + [jax.ShapeDtypeStruct(p.shape[1:], p.dtype) for p in ps],
        scratch_shapes=[pltpu.SemaphoreType.DMA((5 * nt,)), pltpu.SemaphoreType.DMA((5 * nt,))],
    )(*ps)
    return outs[:nt]


def _row_tile(m):
    return m if m <= 384 else 128


def _add2_rows(name, a, b):
    n4, m, n = a.shape
    tm = _row_tile(m)
    out = _rows(name, lambda p, q: ((p + q,), ()), n4 * m, tm,
                [_rb(a.reshape(n4 * m, n), tm, n), _rb(b.reshape(n4 * m, n), tm, n)], [(n, BF16)])[0]
    return out.reshape(n4, m, n)


def _add4_rows(name, p):
    _, m, n = p.shape
    tm = _row_tile(m)
    nb = m // tm
    flat = p.reshape(4 * m, n)
    ins = [(flat, (tm, n), (lambda i, j=j: (j * nb + i, 0))) for j in range(4)]
    f32 = lambda v: v.astype(F32)
    return _rows(name, lambda a, b, c, d: ((((f32(a) + f32(b)) + f32(c)) + f32(d),), ()), m, tm, ins, [(n, F32)])[0]


def _gather_weights2(w_in, w_a, w_b, w_out, w_up, w_down, conv_w):
    c = lax.axis_index("c")
    bits = lax.bitcast_convert_type(conv_w, BF16).reshape(-1)
    extra = jnp.zeros(((WD_EXT_ROWS - W_DOWN_SHARD) * D,), BF16).at[:bits.shape[0]].set(bits)
    wd_ext = jnp.concatenate([w_down.astype(BF16), extra.reshape(-1, D)], axis=0)
    shards = [w_in.astype(BF16), w_a.astype(BF16), w_b.astype(BF16), w_out.astype(BF16), w_up.astype(BF16), wd_ext]
    halves = [lax.dynamic_slice_in_dim(t, c * (t.shape[0] // 2), t.shape[0] // 2, axis=0) for t in shards]
    full = [g.reshape((N_CHIP, 2 * g.shape[1]) + g.shape[2:]) for g in _all_gather8_multi("all_gather_weights", halves)]
    wi = full[0].transpose(1, 0, 2).reshape(D, N_CHIP * W_IN_SHARD)
    w_main = jnp.concatenate([wi[:, :FF_COL], wi[:, FF_COL + FOX_H:]], axis=1)
    w_ff = jnp.pad(wi[:, FF_COL:FF_COL + FOX_H], ((0, 0), (0, LANES - FOX_H)))
    wa, wb, wo = (full[i].reshape(D, D) for i in (1, 2, 3))
    wu = full[4].transpose(1, 0, 2).reshape(D, 2 * D_FF)
    wd = full[5][:, :W_DOWN_SHARD].reshape(D_FF, D)
    cw_bits = full[5][:, W_DOWN_SHARD:].reshape(N_CHIP, -1)[:, :bits.shape[0]].reshape(N_CHIP, 3, W_UP_SHARD, 2)
    cw = lax.bitcast_convert_type(cw_bits, F32).transpose(1, 0, 2).reshape(3, 2 * D_FF)
    return w_main, w_ff, wa, wb, wo, wu, wd, cw


def _reduce_scatter_grads2(d_main, d_ff, d_a, d_b, d_o, d_u, d_d):
    c = lax.axis_index("c")
    d_in = jnp.concatenate(d_main[:7] + [d_ff[:, :FOX_H]] + d_main[7:], axis=1)
    per_chip = [
        d_in.reshape(D, N_CHIP, W_IN_SHARD).transpose(1, 0, 2),
        d_a.reshape(N_CHIP, -1, D), d_b.reshape(N_CHIP, -1, D), d_o.reshape(N_CHIP, -1, D),
        d_u.reshape(D, N_CHIP, W_UP_SHARD).transpose(1, 0, 2),
        d_d.reshape(N_CHIP, -1, D),
    ]
    gs = [t.reshape(N_CHIP, 2, t.shape[1] // 2, t.shape[2]) for t in per_chip]
    got = _swap_halves_multi("grad_swap_halves", gs)
    sums = [_add2_rows("grad_chip_sum_%d" % q, lax.dynamic_index_in_dim(g, c, axis=1, keepdims=False), s_)
            for q, (g, s_) in enumerate(zip(gs, got))]
    pieces = _chip_exchange_multi("grad_chip_exchange", sums)
    mine = [_add4_rows("grad_sum_chips_%d" % q, p) for q, p in enumerate(pieces)]
    other = _swap_sibling_multi("grad_share_half", mine)
    return [jnp.concatenate([jnp.where(c == 0, a, b), jnp.where(c == 0, b, a)], axis=0) for a, b in zip(mine, other)]


def _local_step(x, target, norm_mix, fox_f_bias, hg_lb_logits, hg_norm, norm_ffn, conv_b, norm_final,
                w_main, w_ff, wa, wb, wo, wu, wd, conv_w):
    s = x.shape[0]
    bias = jnp.pad(fox_f_bias, ((0, 0), (0, LANES - FOX_H)))
    conv_w8 = jnp.pad(conv_w, ((0, 5), (0, 0)))
    t = min(FOX_T, s)

    n1, n1t = _rms_fwd("norm_mix_fwd", x, norm_mix)
    proj = _mm("in_proj", n1, w_main, "nn", F32, 1024, 1024, D)
    pff = _mm("in_proj_forget", n1, w_ff, "nn", F32, 1024, LANES, D)
    qb, kb, vb, ka = _fox_prep2(proj, pff, bias)
    o_b, laux = _fox_fwd2(qb, kb, vb, ka)
    o_raw, states = _hg_fwd(proj, hg_lb_logits)
    o_a = _hg_post_fwd(o_raw, proj, hg_norm)
    pa = _mm("branch_a", o_a, wa, "nn", F32, 1024, 1024, D)
    pb = _mm("branch_b", o_b, wb, "nn", F32, 1024, 1024, D)
    merged = _merge_fwd(pa, pb, proj)
    h1 = _mm("out_proj", merged, wo, "nn", F32, 1024, 1024, D, res=x)
    n2, n2t = _rms_fwd("norm_ffn_fwd", h1, norm_ffn)
    u = _mm("ffn_up", n2, wu, "nn", F32, 1024, W_UP_SHARD, D)
    act, gelu_gate, dact_dgate = _convglu_fwd(u, conv_w8, conv_b)
    h2 = _mm("ffn_down", act, wd, "nn", F32, 512, 1024, D_FF, res=h1)
    (dh2,), (d_norm_final, loss_row) = _final(h2, target, norm_final)

    dact = _mm("ffn_down_dx", dh2, wd, "nt", BF16, 1024, D_FF, D)
    d_wd = _mm("ffn_down_dw", act, dh2, "tn", F32, D_FF // 2, 1024, DW_TK // 2)
    (du,), (d_conv_w8, d_conv_b) = _convglu_bwd(u, dact, gelu_gate, dact_dgate, conv_w8)
    dn2 = _mm("ffn_up_dx", du, wu, "nt", F32, 1024, 1024, W_UP_SHARD)
    d_wu = _mm("ffn_up_dw", n2t, du, "nn", F32, 1024, W_UP_SHARD, DW_TK)
    (dh1,), (d_norm_ffn,) = _rms_bwd("norm_ffn_bwd", h1, norm_ffn, [dn2], dh2)

    dmerged = _mm("out_proj_dx", dh1, wo, "nt", F32, 1024, 1024, D)
    d_wo = _mm("out_proj_dw", merged, dh1, "tn", F32, 1024, 1024, DW_TK)
    dpa, dpb, dga, dgb = _merge_bwd(dmerged, pa, pb, proj)
    do_a = _mm("branch_a_dx", dpa, wa, "nt", F32, 1024, 1024, D)
    do_b = _mm("branch_b_dx", dpb, wb, "nt", BF16, 1024, 1024, D)
    d_wa = _mm("branch_a_dw", o_a, dpa, "tn", F32, 1024, 1024, DW_TK)
    d_wb = _mm("branch_b_dw", o_b, dpb, "tn", F32, 1024, 1024, DW_TK)

    (do_raw, dhg), (d_hg_norm,) = _hg_post_bwd(do_a, o_raw, proj, hg_norm)
    dhq, dhf, dhi, d_lb_logits = _hg_bwd(proj, hg_lb_logits, states, do_raw)

    dfq, dfk, dfv, dcrow = _fox_bwd2(qb, kb, vb, ka, o_b, laux, do_b)
    dct = jnp.pad(dcrow.reshape(FOX_H, s), ((0, LANES - FOX_H), (0, 0)))
    dff, d_bias = _fox_gate_bwd(dct, pff, bias)

    pieces = [dhq, dhf, dhi, dhg, dfq, dfk, dfv, dga, dgb]
    dn1 = _mm_sum_nt("in_proj_dx", pieces, w_main, (dff, w_ff), 1024, 1024)
    d_w_main = [_mm("in_proj_dw_%d" % i, n1t, p, "nn", F32, 1024, 1024, DW_TK) for i, p in enumerate(pieces)]
    d_w_ff = _mm("in_proj_forget_dw", n1t, dff, "nn", F32, 1024, LANES, DW_TK)
    (dx,), (d_norm_mix,) = _rms_bwd("norm_mix_bwd", x, norm_mix, [dn1], dh1)

    small = dict(norm_mix=d_norm_mix, fox_f_bias=d_bias[:, :FOX_H], hg_lb_logits=d_lb_logits, hg_norm=d_hg_norm,
                 norm_ffn=d_norm_ffn, conv_b=d_conv_b, norm_final=d_norm_final, conv_w=d_conv_w8[:3], loss=loss_row)
    big = (d_w_main, d_w_ff, d_wa, d_wb, d_wo, d_wu, d_wd)
    return dx, small, big


SMALL_KEYS = ("norm_mix", "fox_f_bias", "hg_lb_logits", "hg_norm", "norm_ffn", "conv_b", "norm_final")


def _pack_small(parts):
    rows, layout = [], []
    for key, arr in parts:
        flat = arr.reshape(-1)
        n = flat.shape[0]
        nr = -(-n // LANES)
        rows.append(jnp.pad(flat, (0, nr * LANES - n)).reshape(nr, LANES))
        layout.append((key, arr.shape, n, nr))
    packed = jnp.concatenate(rows, axis=0)
    pad = -packed.shape[0] % 8
    return jnp.pad(packed, ((0, pad), (0, 0))), layout


def _unpack_small(packed, layout):
    out, r0 = {}, 0
    for key, shape, n, nr in layout:
        out[key] = packed[r0:r0 + nr].reshape(-1)[:n].reshape(shape)
        r0 += nr
    return out


def kernel(x, norm_mix, w_in, fox_f_bias, hg_lb_logits, hg_norm, w_branch_a, w_branch_b, w_out, norm_ffn, w_up, conv_w, conv_b, w_down, norm_final, loss_target, m_norm_mix, m_w_in, m_fox_f_bias, m_hg_lb_logits, m_hg_norm, m_w_branch_a, m_w_branch_b, m_w_out, m_norm_ffn, m_w_up, m_conv_w, m_conv_b, m_w_down, m_norm_final, v_norm_mix, v_w_in, v_fox_f_bias, v_hg_lb_logits, v_hg_norm, v_w_branch_a, v_w_branch_b, v_w_out, v_norm_ffn, v_w_up, v_conv_w, v_conv_b, v_w_down, v_norm_final):
    chip = 2 * lax.axis_index("x") + lax.axis_index("y")
    w_main, w_ff, wa, wb, wo, wu, wd, cw = _gather_weights2(
        w_in[0], w_branch_a[0], w_branch_b[0], w_out[0], w_up[0], w_down[0], conv_w[0])
    dx, small, big = _local_step(
        x[0], loss_target[0], norm_mix, fox_f_bias, hg_lb_logits, hg_norm, norm_ffn, conv_b,
        norm_final.reshape(1, D), w_main, w_ff, wa, wb, wo, wu, wd, cw)

    packed, layout = _pack_small([(k, small[k]) for k in SMALL_KEYS + ("conv_w", "loss")])
    red = _unpack_small(_all_reduce_small("all_reduce_small", packed), layout)
    loss = red["loss"][0, 0]
    g_conv_w = lax.dynamic_slice_in_dim(red["conv_w"], chip * W_UP_SHARD, W_UP_SHARD, axis=1)

    g_big = _reduce_scatter_grads2(*big)

    names = ["norm_mix", "w_in", "fox_f_bias", "hg_lb_logits", "hg_norm", "w_branch_a", "w_branch_b", "w_out",
             "norm_ffn", "w_up", "conv_w", "conv_b", "w_down", "norm_final"]
    weights = dict(norm_mix=norm_mix, w_in=w_in, fox_f_bias=fox_f_bias, hg_lb_logits=hg_lb_logits, hg_norm=hg_norm,
                   w_branch_a=w_branch_a, w_branch_b=w_branch_b, w_out=w_out, norm_ffn=norm_ffn, w_up=w_up,
                   conv_w=conv_w, conv_b=conv_b, w_down=w_down, norm_final=norm_final)
    ms = dict(norm_mix=m_norm_mix, w_in=m_w_in, fox_f_bias=m_fox_f_bias, hg_lb_logits=m_hg_lb_logits,
              hg_norm=m_hg_norm, w_branch_a=m_w_branch_a, w_branch_b=m_w_branch_b, w_out=m_w_out,
              norm_ffn=m_norm_ffn, w_up=m_w_up, conv_w=m_conv_w, conv_b=m_conv_b, w_down=m_w_down,
              norm_final=m_norm_final)
    vs = dict(norm_mix=v_norm_mix, w_in=v_w_in, fox_f_bias=v_fox_f_bias, hg_lb_logits=v_hg_lb_logits,
              hg_norm=v_hg_norm, w_branch_a=v_w_branch_a, w_branch_b=v_w_branch_b, w_out=v_w_out,
              norm_ffn=v_norm_ffn, w_up=v_w_up, conv_w=v_conv_w, conv_b=v_conv_b, w_down=v_w_down,
              norm_final=v_norm_final)

    grads, deltas, new_m, new_v = {}, {}, {}, {}
    big_names = ["w_in", "w_branch_a", "w_branch_b", "w_out", "w_up", "w_down"]
    for name, g2 in zip(big_names, g_big):
        shape = weights[name].shape
        rows = g2.shape[0]
        d_, m_, v_ = _adamw("adamw_" + name, weights[name][0], g2, ms[name][0], vs[name][0], tm=rows // 8)
        grads[name], deltas[name], new_m[name], new_v[name] = (a.reshape(shape) for a in (g2, d_, m_, v_))
    shape = conv_w.shape
    d_, m_, v_ = _adamw("adamw_conv_w", conv_w[0], g_conv_w, m_conv_w[0], v_conv_w[0])
    grads["conv_w"], deltas["conv_w"], new_m["conv_w"], new_v["conv_w"] = (
        a.reshape(shape) for a in (g_conv_w, d_, m_, v_))
    gs = {k: red[k].reshape(weights[k].shape) for k in SMALL_KEYS}
    pw, lay = _pack_small([(k, weights[k]) for k in SMALL_KEYS])
    pg, _ = _pack_small([(k, gs[k]) for k in SMALL_KEYS])
    pm, _ = _pack_small([(k, ms[k]) for k in SMALL_KEYS])
    pv, _ = _pack_small([(k, vs[k]) for k in SMALL_KEYS])
    d_, m_, v_ = (_unpack_small(a, lay) for a in _adamw("adamw_small", pw, pg, pm, pv))
    for k in SMALL_KEYS:
        grads[k], deltas[k], new_m[k], new_v[k] = gs[k], d_[k], m_[k], v_[k]

    return (loss, dx[None], *[grads[n] for n in names], *[deltas[n] for n in names],
            *[new_m[n] for n in names], *[new_v[n] for n in names])
```

```python
import jax
import jax.numpy as jnp
from jax import lax
from jax.experimental import pallas as pl
from jax.experimental.pallas import tpu as pltpu

F32 = jnp.float32
BF16 = jnp.bfloat16

D = 1024
HG_H, HG_DK = 8, 128
FOX_H, FOX_D = 16, 64
D_FF = 2816
EPS = 1e-6
N_CHIP = 4
LANES = 128
W_IN_SHARD = 2308
W_UP_SHARD = 1408
W_DOWN_SHARD = 704
FF_COL = 7168
ADAM_LR, ADAM_B1, ADAM_B2, ADAM_EPS, ADAM_WD, ADAM_STEP = 0.001, 0.9, 0.999, 1e-08, 0.01, 10

HG_C = 16
HG_T = 512
HG_UNROLL = 16
HG_UNROLL_BWD = 4
FOX_T = 512
DW_TK = 2048
VMEM_LIMIT = 56 * 1024 * 1024
MESH = pl.DeviceIdType.MESH
ANY = pl.BlockSpec(memory_space=pl.ANY)


def _cparams(sem):
    return pltpu.CompilerParams(dimension_semantics=sem, vmem_limit_bytes=VMEM_LIMIT)


def _sigmoid(x):
    return 1.0 / (1.0 + jnp.exp(-x))


def _dot(a, b, dims):
    return lax.dot_general(a.astype(BF16), b.astype(BF16), (dims, ((), ())), preferred_element_type=F32)


NN = ((1,), (0,))
NT = ((1,), (1,))
TN = ((0,), (0,))


def _split_dot(tri, x, parts, dims=NN):
    acc = None
    r = x
    for _ in range(parts):
        p = r.astype(BF16)
        t = lax.dot_general(tri, p, (dims, ((), ())), preferred_element_type=F32)
        acc = t if acc is None else acc + t
        r = r - p.astype(F32)
    return acc


def _rb(arr, tm, width, cb=0):
    return (arr, (tm, width), lambda i: (i, cb))


def _cst(arr):
    return (arr, arr.shape, lambda i: (0,) * arr.ndim)


def _rows(name, fn, n_rows, tm, ins, outs, accs=()):
    n_in, n_out, n_acc = len(ins), len(outs), len(accs)
    nb = n_rows // tm

    def body(*refs):
        vals = [r[...] for r in refs[:n_in]]
        o, a = fn(*vals)
        for r, v in zip(refs[n_in:n_in + n_out], o):
            r[...] = v.astype(r.dtype)
        if n_acc:
            acc_refs = refs[n_in + n_out:]

            @pl.when(pl.program_id(0) == 0)
            def _():
                for r in acc_refs:
                    r[...] = jnp.zeros_like(r)

            for r, v in zip(acc_refs, a):
                r[...] += v

    in_specs = [pl.BlockSpec(bs, im) for (_, bs, im) in ins]
    out_specs = [pl.BlockSpec((tm, w), lambda i: (i, 0)) for (w, _) in outs]
    out_specs += [pl.BlockSpec((r, w), lambda i: (0, 0)) for (r, w) in accs]
    out_shape = [jax.ShapeDtypeStruct((n_rows, w), dt) for (w, dt) in outs]
    out_shape += [jax.ShapeDtypeStruct((r, w), F32) for (r, w) in accs]
    res = pl.pallas_call(
        body, name=name, grid=(nb,), in_specs=in_specs, out_specs=out_specs, out_shape=out_shape,
        compiler_params=_cparams(("arbitrary",)),
    )(*[a for a, _, _ in ins])
    return (res[:n_out], res[n_out:]) if n_acc else res


def _mm(name, a, b, mode, out_dtype, tm, tn, tk, res=None):
    if mode == "nn":
        (m, k), n = a.shape, b.shape[1]
    elif mode == "nt":
        (m, k), n = a.shape, b.shape[0]
    else:
        (k, m), n = a.shape, b.shape[1]
    tm, tn, tk = min(tm, m), min(tn, n), min(tk, k)
    assert m % tm == 0 and n % tn == 0 and k % tk == 0, (name, m, n, k, tm, tn, tk)
    if mode == "nn":
        a_spec = pl.BlockSpec((tm, tk), lambda i, j, kk: (i, kk))
        b_spec = pl.BlockSpec((tk, tn), lambda i, j, kk: (kk, j))
        dims = NN
    elif mode == "nt":
        a_spec = pl.BlockSpec((tm, tk), lambda i, j, kk: (i, kk))
        b_spec = pl.BlockSpec((tn, tk), lambda i, j, kk: (j, kk))
        dims = NT
    else:
        a_spec = pl.BlockSpec((tk, tm), lambda i, j, kk: (kk, i))
        b_spec = pl.BlockSpec((tk, tn), lambda i, j, kk: (kk, j))
        dims = TN
    nk = k // tk
    has_res = res is not None
    acc_in_out = out_dtype == F32 and not has_res

    def body(*refs):
        a_ref, b_ref = refs[0], refs[1]
        r_ref = refs[2] if has_res else None
        o_ref = refs[3] if has_res else refs[2]
        part = _dot(a_ref[...], b_ref[...], dims)

        def finish(val):
            if has_res:
                val = val + r_ref[...]
            o_ref[...] = val.astype(o_ref.dtype)

        if nk == 1:
            finish(part)
        elif acc_in_out:
            kk = pl.program_id(2)

            @pl.when(kk == 0)
            def _():
                o_ref[...] = part

            @pl.when(kk > 0)
            def _():
                o_ref[...] += part
        else:
            acc_ref = refs[-1]
            kk = pl.program_id(2)

            @pl.when(kk == 0)
            def _():
                acc_ref[...] = part

            @pl.when(kk > 0)
            def _():
                acc_ref[...] += part

            @pl.when(kk == nk - 1)
            def _():
                finish(acc_ref[...])

    in_specs = [a_spec, b_spec]
    args = [a, b]
    if has_res:
        in_specs.append(pl.BlockSpec((tm, tn), lambda i, j, kk: (i, j)))
        args.append(res)
    return pl.pallas_call(
        body, name=name, grid=(m // tm, n // tn, nk), in_specs=in_specs,
        out_specs=pl.BlockSpec((tm, tn), lambda i, j, kk: (i, j)),
        out_shape=jax.ShapeDtypeStruct((m, n), out_dtype),
        scratch_shapes=[pltpu.VMEM((tm, tn), F32)] if nk > 1 and not acc_in_out else [],
        compiler_params=_cparams(("arbitrary", "arbitrary", "arbitrary")),
    )(*args)


def _mm_sum_nt(name, pieces, w, extra, tm, tn):
    n_p = len(pieces)
    m, k = pieces[0].shape
    n = w.shape[0]
    xa, xb = extra
    ke = xa.shape[1]
    tm, tn = min(tm, m), min(tn, n)

    def body(*refs):
        p_refs, w_ref, xa_ref, xb_ref, o_ref = refs[:n_p], refs[n_p], refs[n_p + 1], refs[n_p + 2], refs[-1]
        kk = pl.program_id(2)

        @pl.when(kk == 0)
        def _():
            o_ref[...] = _dot(p_refs[0][...], w_ref[...], NT)

        for i in range(1, n_p):
            @pl.when(kk == i)
            def _(i=i):
                o_ref[...] += _dot(p_refs[i][...], w_ref[...], NT)

        @pl.when(kk == n_p)
        def _():
            o_ref[...] += _dot(xa_ref[...], xb_ref[...], NT)

    in_specs = [pl.BlockSpec((tm, k), lambda i, j, kk: (i, 0)) for _ in range(n_p)]
    in_specs.append(pl.BlockSpec((tn, k), lambda i, j, kk: (j, jnp.minimum(kk, n_p - 1))))
    in_specs += [pl.BlockSpec((tm, ke), lambda i, j, kk: (i, 0)), pl.BlockSpec((tn, ke), lambda i, j, kk: (j, 0))]
    return pl.pallas_call(
        body, name=name, grid=(m // tm, n // tn, n_p + 1), in_specs=in_specs,
        out_specs=pl.BlockSpec((tm, tn), lambda i, j, kk: (i, j)),
        out_shape=jax.ShapeDtypeStruct((m, n), F32),
        compiler_params=_cparams(("arbitrary", "arbitrary", "arbitrary")),
    )(*pieces, w, xa, xb)


def _rms_fwd(name, x, gain, tm=256):
    s = x.shape[0]

    def body(x_ref, g_ref, y_ref, yt_ref):
        xb = x_ref[...]
        y = xb * lax.rsqrt(jnp.mean(xb * xb, axis=-1, keepdims=True) + EPS) * g_ref[...]
        y_ref[...] = y.astype(BF16)
        yt_ref[...] = y.T.astype(BF16)

    return pl.pallas_call(
        body, name=name, grid=(s // tm,),
        in_specs=[pl.BlockSpec((tm, D), lambda i: (i, 0)), pl.BlockSpec((1, D), lambda i: (0, 0))],
        out_specs=[pl.BlockSpec((tm, D), lambda i: (i, 0)), pl.BlockSpec((D, tm), lambda i: (0, i))],
        out_shape=[jax.ShapeDtypeStruct((s, D), BF16), jax.ShapeDtypeStruct((D, s), BF16)],
        compiler_params=_cparams(("arbitrary",)),
    )(x, gain)


def _rms_bwd(name, x, gain, dns, dres, tm=256):
    s = x.shape[0]
    n_dn = len(dns)

    def fn(xb, g, *rest):
        dn = rest[0]
        for t in rest[1:n_dn]:
            dn = dn + t
        r = lax.rsqrt(jnp.mean(xb * xb, axis=-1, keepdims=True) + EPS)
        xhat = xb * r
        dxh = dn * g
        dx = r * (dxh - xhat * jnp.mean(dxh * xhat, axis=-1, keepdims=True)) + rest[n_dn]
        return (dx,), (jnp.sum(dn * xhat, axis=0, keepdims=True),)

    ins = [_rb(x, tm, D), _cst(gain)] + [_rb(t, tm, D) for t in dns] + [_rb(dres, tm, D)]
    return _rows(name, fn, s, tm, ins, [(D, F32)], [(1, D)])


def _final(h2, target, gain, tm=256):
    s = h2.shape[0]

    def fn(hb, tb, g):
        r = lax.rsqrt(jnp.mean(hb * hb, axis=-1, keepdims=True) + EPS)
        xhat = hb * r
        e = xhat * g - tb
        dy = e * (1.0 / D)
        dxh = dy * g
        dh = r * (dxh - xhat * jnp.mean(dxh * xhat, axis=-1, keepdims=True))
        lrow = 0.5 * jnp.sum(jnp.sum(e * e, axis=-1, keepdims=True) * (1.0 / D), axis=0, keepdims=True)
        return (dh,), (jnp.sum(dy * xhat, axis=0, keepdims=True), jnp.broadcast_to(lrow, (1, LANES)))

    return _rows("final_norm_loss", fn, s, tm, [_rb(h2, tm, D), _rb(target, tm, D), _cst(gain)],
                 [(D, F32)], [(1, D), (1, LANES)])


def _merge_fwd(pa, pb, proj, tm=256):
    s = pa.shape[0]

    def fn(a, b, ga, gb):
        return (_sigmoid(ga) * a + _sigmoid(gb) * b,), ()

    ins = [_rb(pa, tm, D), _rb(pb, tm, D), _rb(proj, tm, D, 7), _rb(proj, tm, D, 8)]
    return _rows("merge_fwd", fn, s, tm, ins, [(D, BF16)])[0]


def _merge_bwd(dmerged, pa, pb, proj, tm=256):
    s = pa.shape[0]

    def fn(dm, a, b, ga, gb):
        sa, sb = _sigmoid(ga), _sigmoid(gb)
        return (dm * sa, dm * sb, dm * a * sa * (1.0 - sa), dm * b * sb * (1.0 - sb)), ()

    ins = [_rb(dmerged, tm, D), _rb(pa, tm, D), _rb(pb, tm, D), _rb(proj, tm, D, 7), _rb(proj, tm, D, 8)]
    return _rows("merge_bwd", fn, s, tm, ins, [(D, BF16), (D, BF16), (D, BF16), (D, BF16)])


def _gelu_parts(x):
    cdf = 0.5 * (1.0 + lax.erf(x * 0.7071067811865476))
    pdf = 0.3989422804014327 * jnp.exp(-0.5 * x * x)
    return x * cdf, cdf + x * pdf


def _conv_taps(u_ext, n_out):
    cur = u_ext[8:8 + n_out]
    m1 = pltpu.roll(u_ext, 1, 0)[8:8 + n_out]
    m2 = pltpu.roll(u_ext, 2, 0)[8:8 + n_out]
    return m2, m1, cur


def _convglu_fwd(u, conv_w8, conv_b, tm=256):
    s, w = u.shape
    tb = tm // 8

    def fn(ub, up, cw, cb):
        i = pl.program_id(0)
        up = jnp.where(i == 0, 0.0, up)
        m2, m1, cur = _conv_taps(jnp.concatenate([up, ub], axis=0), tm)
        acc = cb + cw[0:1] * m2 + cw[1:2] * m1 + cw[2:3] * cur
        gl, dgl = _gelu_parts(acc[:, :D_FF])
        val = acc[:, D_FF:]
        return (gl * val, gl, val * dgl), ()

    ins = [_rb(u, tm, w), (u, (8, w), lambda i: (jnp.maximum(i * tb - 1, 0), 0)), _cst(conv_w8), _cst(conv_b)]
    return _rows("convglu_fwd", fn, s, tm, ins, [(D_FF, BF16)] * 3)


def _convglu_bwd(u, dact, gl, gd, conv_w8, tm=256):
    s, w = u.shape
    tb = tm // 8
    nb = s // tm

    def fn(ub, up, db, dn, glb, gln, gdb, gdn, cw):
        i = pl.program_id(0)
        up = jnp.where(i == 0, 0.0, up)
        dn = jnp.where(i == nb - 1, 0.0, dn.astype(F32))
        ne = tm + 8
        m2, m1, cur = _conv_taps(jnp.concatenate([up, ub], axis=0), tm)
        ext = lambda blk, nxt: jnp.concatenate([blk.astype(F32), nxt.astype(F32)], axis=0)
        de = ext(db, dn)
        dacc = jnp.concatenate([de * ext(gdb, gdn), de * ext(glb, gln)], axis=1)
        p1 = pltpu.roll(dacc, ne - 1, 0)[:tm]
        p2 = pltpu.roll(dacc, ne - 2, 0)[:tm]
        d0 = dacc[:tm]
        du = cw[2:3] * d0 + cw[1:2] * p1 + cw[0:1] * p2
        zero5 = jnp.zeros((5, w), F32)
        dcw = jnp.concatenate([
            jnp.sum(d0 * m2, axis=0, keepdims=True), jnp.sum(d0 * m1, axis=0, keepdims=True),
            jnp.sum(d0 * cur, axis=0, keepdims=True), zero5], axis=0)
        return (du,), (dcw, jnp.sum(d0, axis=0, keepdims=True))

    nxt = lambda arr: (arr, (8, D_FF), lambda i: (jnp.minimum((i + 1) * tb, s // 8 - 1), 0))
    ins = [_rb(u, tm, w), (u, (8, w), lambda i: (jnp.maximum(i * tb - 1, 0), 0)),
           _rb(dact, tm, D_FF), nxt(dact), _rb(gl, tm, D_FF), nxt(gl), _rb(gd, tm, D_FF), nxt(gd), _cst(conv_w8)]
    return _rows("convglu_bwd", fn, s, tm, ins, [(w, BF16)], [(8, w), (1, w)])


def _chunk_scan(x, t_iota, reverse):
    k = 1
    while k < HG_C:
        if reverse:
            x = x + jnp.where(t_iota < HG_C - k, pltpu.roll(x, HG_C - k, 0), 0.0)
        else:
            x = x + jnp.where(t_iota >= k, pltpu.roll(x, k, 0), 0.0)
        k *= 2
    return x


def _hg_gates(hq, hf, lb):
    sq = _sigmoid(hq)
    q = hq * sq
    sg = _sigmoid(hf)
    f = lb + (1.0 - lb) * sg
    return q, sq, sg, f, 1.0 - f, jnp.log(f)


def _lb_of(logits):
    l0, l1 = logits[0:1], logits[1:2]
    mx = jnp.maximum(l0, l1)
    e0, e1 = jnp.exp(l0 - mx), jnp.exp(l1 - mx)
    return e0 / (e0 + e1)


def _tri(n, lower):
    r = lax.broadcasted_iota(jnp.int32, (n, n), 0)
    c = lax.broadcasted_iota(jnp.int32, (n, n), 1)
    return jnp.where((r >= c) if lower else (r <= c), 1.0, 0.0).astype(BF16)


def _hg_intra_terms(q, kk, b, t_iota):
    ws, ps = [], []
    for s in range(HG_C):
        p = jnp.where(t_iota >= s, jnp.exp(b - b[s:s + 1]), 0.0)
        ps.append(p)
        ws.append(q * kk[s:s + 1] * p)
    return jnp.concatenate(ws, axis=0), ps


def _hg_fwd(proj, lb_logits):
    s = proj.shape[0]
    nt = s // HG_T
    nc = HG_T // HG_C

    def body(q_ref, f_ref, i_ref, l_ref, o_ref, st_ref, state):
        @pl.when(pl.program_id(1) == 0)
        def _():
            state[...] = jnp.zeros_like(state)

        st_ref[0, 0] = state[...]
        lb = _lb_of(l_ref[...])
        ones = jnp.ones((HG_DK, HG_DK), BF16)
        t_iota = lax.broadcasted_iota(jnp.int32, (HG_C, HG_DK), 0)
        cc = HG_C * HG_C

        def group(gi, st):
            units = []
            for u in range(HG_UNROLL):
                r = pl.ds(pl.multiple_of((gi * HG_UNROLL + u) * HG_C, HG_C), HG_C)
                q, _, _, _, kk, g = _hg_gates(q_ref[r, :], f_ref[r, :], lb)
                b = _chunk_scan(g, t_iota, False)
                b_end = b[HG_C - 1:HG_C]
                w_all, _ = _hg_intra_terms(q, kk, b, t_iota)
                units.append((r, i_ref[r, :], q * jnp.exp(b), jnp.exp(b_end), kk * jnp.exp(b_end - b), w_all))
            a_all = _dot(jnp.concatenate([un[5] for un in units], axis=0), ones, NN)
            kvs = [_dot(v, kd, TN) for (_, v, _, _, kd, _) in units]
            sts = [st]
            for (_, _, _, dec, _, _), kv in zip(units, kvs):
                sts.append(sts[-1] * dec + kv)
            for ui, (r, v, qd, _, _, _) in enumerate(units):
                o = _dot(qd, sts[ui], NT)
                for si in range(HG_C):
                    o = o + a_all[ui * cc + si * HG_C:ui * cc + (si + 1) * HG_C] * v[si:si + 1]
                o_ref[r, :] = o
            return sts[-1]

        state[...] = lax.fori_loop(0, nc // HG_UNROLL, group, state[...])

    col = lambda off: pl.BlockSpec((HG_T, HG_DK), lambda h, t: (t, off + h))
    return pl.pallas_call(
        body, name="hgrn2_fwd", grid=(HG_H, nt),
        in_specs=[col(0), col(8), col(16), pl.BlockSpec((2, HG_DK), lambda h, t: (0, h))],
        out_specs=[pl.BlockSpec((HG_T, HG_DK), lambda h, t: (t, h)),
                   pl.BlockSpec((1, 1, HG_DK, HG_DK), lambda h, t: (h, t, 0, 0))],
        out_shape=[jax.ShapeDtypeStruct((s, D), F32), jax.ShapeDtypeStruct((HG_H, nt, HG_DK, HG_DK), F32)],
        scratch_shapes=[pltpu.VMEM((HG_DK, HG_DK), F32)],
        compiler_params=_cparams(("arbitrary", "arbitrary")),
    )(proj, proj, proj, lb_logits)


def _hg_bwd(proj, lb_logits, states, do_raw):
    s = proj.shape[0]
    nt = s // HG_T
    nc = HG_T // HG_C

    def body(q_ref, f_ref, i_ref, l_ref, st_ref, do_ref, dq_ref, df_ref, di_ref, dl_ref, st_all, adj):
        tb = pl.program_id(1)

        @pl.when(tb == 0)
        def _():
            adj[...] = jnp.zeros_like(adj)
            dl_ref[...] = jnp.zeros_like(dl_ref)

        lb = _lb_of(l_ref[...])
        ones = jnp.ones((HG_DK, HG_DK), BF16)
        t_iota = lax.broadcasted_iota(jnp.int32, (HG_C, HG_DK), 0)
        cc = HG_C * HG_C

        def fwd_group(gi, st):
            terms = []
            for u in range(HG_UNROLL):
                ci = gi * HG_UNROLL + u
                r = pl.ds(pl.multiple_of(ci * HG_C, HG_C), HG_C)
                _, _, _, _, kk, g = _hg_gates(q_ref[r, :], f_ref[r, :], lb)
                b = _chunk_scan(g, t_iota, False)
                b_end = b[HG_C - 1:HG_C]
                terms.append((ci, jnp.exp(b_end), _dot(i_ref[r, :], kk * jnp.exp(b_end - b), TN)))
            for ci, dec, kv in terms:
                st_all[ci] = st
                st = st * dec + kv
            return st

        lax.fori_loop(0, nc // HG_UNROLL, fwd_group, st_ref[0, 0])

        def bwd_group(gj, dlb):
            units = []
            for u in range(HG_UNROLL_BWD):
                ci = nc - 1 - (gj * HG_UNROLL_BWD + u)
                r = pl.ds(pl.multiple_of(ci * HG_C, HG_C), HG_C)
                hq, hf, v, do = q_ref[r, :], f_ref[r, :], i_ref[r, :], do_ref[r, :]
                q, sq, sg, f, kk, g = _hg_gates(hq, hf, lb)
                b = _chunk_scan(g, t_iota, False)
                b_end = b[HG_C - 1:HG_C]
                e_b, e_be, dec = jnp.exp(b), jnp.exp(b_end - b), jnp.exp(b_end)
                w_all, ps = _hg_intra_terms(q, kk, b, t_iota)
                x_all = jnp.concatenate([do * v[si:si + 1] for si in range(HG_C)], axis=0)
                units.append(dict(ci=ci, r=r, hq=hq, v=v, do=do, q=q, sq=sq, sg=sg, f=f, kk=kk, e_b=e_b, e_be=e_be,
                                  dec=dec, kd=kk * e_be, w=w_all, ps=ps, x=x_all))
            both = _dot(jnp.concatenate([un["w"] for un in units] + [un["x"] for un in units], axis=0), ones, NN)
            st0s = [st_all[un["ci"]] for un in units]
            st_ends = [st0 * un["dec"] + _dot(un["v"], un["kd"], TN) for un, st0 in zip(units, st0s)]
            dqks = [_dot(un["do"], un["q"] * un["e_b"], TN) for un in units]
            es = [adj[...]]
            for un, dqk in zip(units, dqks):
                es.append(es[-1] * un["dec"] + dqk)
            adj[...] = es[-1]
            for ui, un in enumerate(units):
                e, q, kk, v, do = es[ui], un["q"], un["kk"], un["v"], un["do"]
                tail = jnp.sum(e * st_ends[ui], axis=0, keepdims=True)
                dq = un["e_b"] * _dot(do, st0s[ui], NN)
                dk = un["e_be"] * _dot(v, e, NN)
                dv = _dot(un["kd"], e, NT)
                a0 = ui * cc
                d0 = (HG_UNROLL_BWD + ui) * cc
                for si in range(HG_C):
                    da = both[d0 + si * HG_C:d0 + (si + 1) * HG_C]
                    aa = both[a0 + si * HG_C:a0 + (si + 1) * HG_C]
                    dap = da * un["ps"][si]
                    dq = dq + dap * kk[si:si + 1]
                    hit = t_iota == si
                    dk = dk + jnp.where(hit, jnp.sum(dap * q, axis=0, keepdims=True), 0.0)
                    dv = dv + jnp.where(hit, jnp.sum(aa * do, axis=0, keepdims=True), 0.0)
                dg = _chunk_scan(q * dq - kk * dk, t_iota, True) + tail
                dfg = dg / un["f"] - dk
                sq, sg, hq, r = un["sq"], un["sg"], un["hq"], un["r"]
                dq_ref[r, :] = (dq * sq * (1.0 + hq * (1.0 - sq))).astype(dq_ref.dtype)
                df_ref[r, :] = (dfg * (1.0 - lb) * sg * (1.0 - sg)).astype(df_ref.dtype)
                di_ref[r, :] = dv.astype(di_ref.dtype)
                dlb = dlb + jnp.sum(dfg * (1.0 - sg), axis=0, keepdims=True)
            return dlb

        dlb = lax.fori_loop(0, nc // HG_UNROLL_BWD, bwd_group, jnp.zeros((1, HG_DK), F32))
        dl0 = dlb * lb * (1.0 - lb)
        dl_ref[...] += jnp.concatenate([dl0, -dl0], axis=0)

    col = lambda off: pl.BlockSpec((HG_T, HG_DK), lambda h, t: (nt - 1 - t, off + h))
    out_col = pl.BlockSpec((HG_T, HG_DK), lambda h, t: (nt - 1 - t, h))
    return pl.pallas_call(
        body, name="hgrn2_bwd", grid=(HG_H, nt),
        in_specs=[col(0), col(8), col(16), pl.BlockSpec((2, HG_DK), lambda h, t: (0, h)),
                  pl.BlockSpec((1, 1, HG_DK, HG_DK), lambda h, t: (h, nt - 1 - t, 0, 0)), col(0)],
        out_specs=[out_col, out_col, out_col, pl.BlockSpec((2, HG_DK), lambda h, t: (0, h))],
        out_shape=[jax.ShapeDtypeStruct((s, D), BF16)] * 3 + [jax.ShapeDtypeStruct((2, D), F32)],
        scratch_shapes=[pltpu.VMEM((nc, HG_DK, HG_DK), F32), pltpu.VMEM((HG_DK, HG_DK), F32)],
        compiler_params=_cparams(("arbitrary", "arbitrary")),
    )(proj, proj, proj, lb_logits, states, do_raw)


def _hg_post_fwd(o_raw, proj, gnorm, tm=256):
    s = o_raw.shape[0]

    def fn(o, hg, gn):
        outs = []
        for h in range(HG_H):
            sl = slice(h * HG_DK, (h + 1) * HG_DK)
            oh, gh = o[:, sl], hg[:, sl]
            r = lax.rsqrt(jnp.mean(oh * oh, axis=-1, keepdims=True) + EPS)
            outs.append(oh * r * gn * (gh * _sigmoid(gh)))
        return (jnp.concatenate(outs, axis=1),), ()

    return _rows("hgrn2_out_fwd", fn, s, tm, [_rb(o_raw, tm, D), _rb(proj, tm, D, 3), _cst(gnorm)], [(D, BF16)])[0]


def _hg_post_bwd(do_a, o_raw, proj, gnorm, tm=256):
    s = o_raw.shape[0]

    def fn(da, o, hg, gn):
        dos, dhgs = [], []
        dgn = jnp.zeros((1, HG_DK), F32)
        for h in range(HG_H):
            sl = slice(h * HG_DK, (h + 1) * HG_DK)
            oh, gh, dh = o[:, sl], hg[:, sl], da[:, sl]
            r = lax.rsqrt(jnp.mean(oh * oh, axis=-1, keepdims=True) + EPS)
            xhat = oh * r
            sg = _sigmoid(gh)
            dy = dh * (gh * sg)
            dhgs.append(dh * xhat * gn * sg * (1.0 + gh * (1.0 - sg)))
            dgn = dgn + jnp.sum(dy * xhat, axis=0, keepdims=True)
            dxh = dy * gn
            dos.append(r * (dxh - xhat * jnp.mean(dxh * xhat, axis=-1, keepdims=True)))
        return (jnp.concatenate(dos, axis=1), jnp.concatenate(dhgs, axis=1)), (dgn,)

    ins = [_rb(do_a, tm, D), _rb(o_raw, tm, D), _rb(proj, tm, D, 3), _cst(gnorm)]
    return _rows("hgrn2_out_bwd", fn, s, tm, ins, [(D, F32), (D, BF16)], [(1, HG_DK)])


def _log_sigmoid(z):
    return jnp.minimum(z, 0.0) - jnp.log(1.0 + jnp.exp(-jnp.abs(z)))


def _fox_gate_bwd(dct, pff, bias, tm=256):
    s = pff.shape[0]
    nb = s // tm

    def body(d_ref, p_ref, b_ref, dff_ref, db_ref, carry):
        @pl.when(pl.program_id(0) == 0)
        def _():
            carry[...] = jnp.zeros_like(carry)
            db_ref[...] = jnp.zeros_like(db_ref)

        dc = d_ref[...].T
        dlf = _split_dot(_tri(tm, False), dc, 3) + carry[0:1]
        carry[...] = jnp.broadcast_to(dlf[0:1], carry.shape)
        dff = dlf * _sigmoid(-(p_ref[...] + b_ref[...]))
        dff_ref[...] = dff
        db_ref[...] += jnp.sum(dff, axis=0, keepdims=True)

    return pl.pallas_call(
        body, name="fox_gate_bwd", grid=(nb,),
        in_specs=[pl.BlockSpec((LANES, tm), lambda i: (0, nb - 1 - i)),
                  pl.BlockSpec((tm, LANES), lambda i: (nb - 1 - i, 0)), pl.BlockSpec((1, LANES), lambda i: (0, 0))],
        out_specs=[pl.BlockSpec((tm, LANES), lambda i: (nb - 1 - i, 0)), pl.BlockSpec((1, LANES), lambda i: (0, 0))],
        out_shape=[jax.ShapeDtypeStruct((s, LANES), F32), jax.ShapeDtypeStruct((1, LANES), F32)],
        scratch_shapes=[pltpu.VMEM((8, LANES), F32)],
        compiler_params=_cparams(("arbitrary",)),
    )(dct, pff, bias)


def _diag_mask(t):
    r = lax.broadcasted_iota(jnp.int32, (t, t), 0)
    c = lax.broadcasted_iota(jnp.int32, (t, t), 1)
    return r >= c


AUX_ONES = 6


def _pieces(x):
    h = x.astype(BF16)
    r = x - h.astype(F32)
    m = r.astype(BF16)
    return h, m, (r - m.astype(F32)).astype(BF16)


def _lane_put(lane, cols, base):
    out = None
    for i, col in enumerate(cols):
        term = jnp.where(lane == base + i, col.astype(F32), 0.0)
        out = term if out is None else out + term
    return out


def _fox_prep2(proj, pff, bias, tm=256):
    s = pff.shape[0]

    def body(q_ref, k_ref, v_ref, p_ref, b_ref, qb_ref, kb_ref, vb_ref, ka_ref, carry):
        @pl.when(pl.program_id(0) == 0)
        def _():
            carry[...] = jnp.zeros_like(carry)

        qb_ref[...] = (q_ref[...] * 0.125).astype(BF16)
        kb_ref[...] = k_ref[...].astype(BF16)
        vb_ref[...] = v_ref[...].astype(BF16)
        lf = _log_sigmoid(p_ref[...] + b_ref[...])
        c = _split_dot(_tri(tm, True), lf, 3) + carry[0:1]
        carry[...] = jnp.broadcast_to(c[tm - 1:tm], carry.shape)
        lane = lax.broadcasted_iota(jnp.int32, (tm, LANES), 1)
        ones = jnp.where((lane >= AUX_ONES) & (lane < AUX_ONES + 6), 1.0, 0.0)
        for p in range(FOX_H // 2):
            aux = ones
            for z in range(2):
                col = jnp.sum(jnp.where(lane == 2 * p + z, c, 0.0), axis=1, keepdims=True)
                aux = aux + _lane_put(lane, _pieces(-col), 3 * z)
            ka_ref[:, p * LANES:(p + 1) * LANES] = aux.astype(BF16)

    row = lambda cb: pl.BlockSpec((tm, D), lambda i: (i, cb))
    return pl.pallas_call(
        body, name="fox_prep", grid=(s // tm,),
        in_specs=[row(4), row(5), row(6), pl.BlockSpec((tm, LANES), lambda i: (i, 0)),
                  pl.BlockSpec((1, LANES), lambda i: (0, 0))],
        out_specs=[row(0)] * 4, out_shape=[jax.ShapeDtypeStruct((s, D), BF16)] * 4,
        scratch_shapes=[pltpu.VMEM((8, LANES), F32)],
        compiler_params=_cparams(("arbitrary",)),
    )(proj, proj, proj, pff, bias)


def _fox_fwd2(qb, kb, vb, ka):
    s = qb.shape[0]
    t = min(FOX_T, s)
    nq = s // t

    def body(q_ref, k_ref, v_ref, ka_ref, o_ref, la_ref):
        i = pl.program_id(1)
        lane = lax.broadcasted_iota(jnp.int32, (t, LANES), 1)
        in_a = lane < FOX_D
        q = q_ref[...]
        zero = jnp.zeros_like(q)
        qh = [jnp.where(in_a, q, zero), jnp.where(in_a, zero, q)]
        c_ones = [jnp.where((lane >= 3 * z) & (lane < 3 * z + 3), 1.0, 0.0) for z in range(2)]

        def keys(j):
            rows = pl.ds(pl.multiple_of(j * t, t), t)
            return jnp.concatenate([k_ref[rows, :], ka_ref[rows, :]], axis=1), rows

        dmask = _diag_mask(t)

        def logits(qx, kk, masked):
            e = lax.dot_general(qx, kk, (NT, ((), ())), preferred_element_type=F32)
            return jnp.where(dmask, e, -1e30) if masked else e

        qc = [jnp.concatenate([qh[z], c_ones[z].astype(BF16)], axis=1) for z in range(2)]

        def step(j, carry, masked):
            kk, rows = keys(j)
            vj = v_ref[rows, :]
            scores = [logits(qc[z], kk, masked) for z in range(2)]
            one = jnp.ones_like(vj)
            vh = [jnp.where(in_a, vj, one), jnp.where(in_a, one, vj)]
            out = []
            for z in range(2):
                m, acc = carry[z]
                m_new = jnp.maximum(m, jnp.max(scores[z], axis=1, keepdims=True))
                p = jnp.exp(scores[z] - m_new)
                out.append((m_new, jnp.exp(m - m_new) * acc + _dot(p, vh[z], NN)))
            return tuple(out)

        init = tuple((jnp.full((t, 1), -1e30, F32), jnp.zeros((t, LANES), F32)) for _ in range(2))
        (ma, acc_a), (mb, acc_b) = step(i, lax.fori_loop(0, i, lambda j, c: step(j, c, False), init), True)
        la = jnp.sum(jnp.where(lane == FOX_D, acc_a, 0.0), axis=1, keepdims=True)
        lb = jnp.sum(jnp.where(lane == 0, acc_b, 0.0), axis=1, keepdims=True)
        o_ref[...] = jnp.where(in_a, acc_a / la, acc_b / lb).astype(o_ref.dtype)
        la_ref[...] = (_lane_put(lane, _pieces(-(ma + jnp.log(la))), AUX_ONES)
                       + _lane_put(lane, _pieces(-(mb + jnp.log(lb))), AUX_ONES + 3)).astype(la_ref.dtype)

    blk = pl.BlockSpec((t, LANES), lambda p, i: (i, p))
    whole = pl.BlockSpec((s, LANES), lambda p, i: (0, p))
    return pl.pallas_call(
        body, name="fox_attn_fwd", grid=(FOX_H // 2, nq), in_specs=[blk, whole, whole, whole],
        out_specs=[blk, blk], out_shape=[jax.ShapeDtypeStruct((s, D), BF16)] * 2,
        compiler_params=_cparams(("arbitrary", "arbitrary")),
    )(qb, kb, vb, ka)


def _fox_bwd2(qb, kb, vb, ka, ob, laux, dob):
    s = qb.shape[0]
    t = min(FOX_T, s)
    nq = s // t

    def body(q_ref, k_ref, v_ref, ka_ref, o_ref, la_ref, do_ref, dq_ref, dk_ref, dv_ref, dc_ref, dkt, dvt):
        i = pl.program_id(1)

        @pl.when(i == 0)
        def _():
            dkt[...] = jnp.zeros_like(dkt)
            dvt[...] = jnp.zeros_like(dvt)
            dc_ref[...] = jnp.zeros_like(dc_ref)

        lane = lax.broadcasted_iota(jnp.int32, (t, LANES), 1)
        in_a = lane < FOX_D
        q, do, la = q_ref[...], do_ref[...], la_ref[...].astype(F32)
        zero = jnp.zeros_like(q)
        qh = [jnp.where(in_a, q, zero), jnp.where(in_a, zero, q)]
        doh = [jnp.where(in_a, do, zero), jnp.where(in_a, zero, do)]
        qt = [h.astype(F32).T.astype(BF16) for h in qh]
        dot_ = [h.astype(F32).T.astype(BF16) for h in doh]
        prod = do.astype(F32) * o_ref[...].astype(F32)
        qx, dox = [], []
        for z in range(2):
            delta = jnp.sum(jnp.where(in_a if z == 0 else ~in_a, prod, 0.0), axis=1, keepdims=True)
            c_ones = jnp.where((lane >= 3 * z) & (lane < 3 * z + 3), 1.0, 0.0)
            lse_lanes = (lane >= AUX_ONES + 3 * z) & (lane < AUX_ONES + 3 * z + 3)
            qx.append(jnp.concatenate([qh[z], (c_ones + jnp.where(lse_lanes, la, 0.0)).astype(BF16)], axis=1))
            dox.append(jnp.concatenate([doh[z], _lane_put(lane, _pieces(-delta), 3 * z).astype(BF16)], axis=1))
        v_ones = jnp.where(lane < 6, 1.0, 0.0).astype(BF16)
        dmask = _diag_mask(t)

        def step(j, carry, masked):
            rows = pl.ds(pl.multiple_of(j * t, t), t)
            kj, vj = k_ref[rows, :], v_ref[rows, :]
            kk = jnp.concatenate([kj, ka_ref[rows, :]], axis=1)
            vv = jnp.concatenate([vj, v_ones], axis=1)
            out = []
            dk_add, dv_add = None, None
            for z in range(2):
                dq, rsum = carry[z]
                e = lax.dot_general(qx[z], kk, (NT, ((), ())), preferred_element_type=F32)
                if masked:
                    e = jnp.where(dmask, e, -1e30)
                p = jnp.exp(e)
                ds = p * lax.dot_general(dox[z], vv, (NT, ((), ())), preferred_element_type=F32)
                dkz, dvz = _dot(qt[z], ds, NN), _dot(dot_[z], p, NN)
                dk_add = dkz if dk_add is None else dk_add + dkz
                dv_add = dvz if dv_add is None else dv_add + dvz
                dc_ref[0, z, j] += -jnp.sum(ds, axis=0, keepdims=True)
                out.append((dq + _dot(ds, kj, NN), rsum + jnp.sum(ds, axis=1, keepdims=True)))
            dkt[j] += dk_add
            dvt[j] += dv_add
            return tuple(out)

        init = tuple((jnp.zeros((t, LANES), F32), jnp.zeros((t, 1), F32)) for _ in range(2))
        (dq_a, rs_a), (dq_b, rs_b) = step(i, lax.fori_loop(0, i, lambda j, c: step(j, c, False), init), True)
        for z, rs in enumerate((rs_a, rs_b)):
            dc_ref[0, z, i] += jnp.transpose(jnp.broadcast_to(rs, (t, LANES)))[0:1]
        dq_ref[...] = (jnp.where(in_a, dq_a, dq_b) * 0.125).astype(dq_ref.dtype)

        @pl.when(i == nq - 1)
        def _():
            for jb in range(nq):
                dk_ref[jb * t:(jb + 1) * t, :] = dkt[jb].T.astype(dk_ref.dtype)
                dv_ref[jb * t:(jb + 1) * t, :] = dvt[jb].T.astype(dv_ref.dtype)

    blk = pl.BlockSpec((t, LANES), lambda p, i: (i, p))
    whole = pl.BlockSpec((s, LANES), lambda p, i: (0, p))
    return pl.pallas_call(
        body, name="fox_attn_bwd", grid=(FOX_H // 2, nq),
        in_specs=[blk, whole, whole, whole, blk, blk, blk],
        out_specs=[blk, whole, whole, pl.BlockSpec((1, 2, nq, 1, t), lambda p, i: (p, 0, 0, 0, 0))],
        out_shape=[jax.ShapeDtypeStruct((s, D), BF16)] * 3 + [jax.ShapeDtypeStruct((FOX_H // 2, 2, nq, 1, t), F32)],
        scratch_shapes=[pltpu.VMEM((nq, LANES, t), F32), pltpu.VMEM((nq, LANES, t), F32)],
        compiler_params=_cparams(("arbitrary", "arbitrary")),
    )(qb, kb, vb, ka, ob, laux, dob)


def _adamw(name, w, g, m, v, tm=None):
    rows, width = w.shape
    tm = rows if tm is None else tm
    c1 = 1.0 - ADAM_B1 ** ADAM_STEP
    c2 = 1.0 - ADAM_B2 ** ADAM_STEP

    def fn(wb, gb, mb, vb):
        m_new = ADAM_B1 * mb + (1.0 - ADAM_B1) * gb
        v_new = ADAM_B2 * vb + (1.0 - ADAM_B2) * (gb * gb)
        delta = -ADAM_LR * ((m_new / c1) / (jnp.sqrt(v_new / c2) + ADAM_EPS) + ADAM_WD * wb)
        return (delta, m_new, v_new), ()

    ins = [_rb(a, tm, width) for a in (w, g, m, v)]
    return _rows(name, fn, rows, tm, ins, [(width, F32)] * 3)


def _me():
    return lax.axis_index("x"), lax.axis_index("y"), lax.axis_index("c")


def _all_reduce_small(name, block):
    r, n = block.shape

    def body(x_ref, sum_ref, gath, send_sems, recv_sems):
        x, y, c = _me()
        me = 4 * x + 2 * y + c
        gath[me] = x_ref[...]
        sends = []
        for k in range(1, 8):
            px = x ^ ((k >> 2) & 1)
            py = y ^ ((k >> 1) & 1)
            pc = c ^ (k & 1)
            sends.append(pltpu.make_async_remote_copy(
                src_ref=x_ref, dst_ref=gath.at[me], send_sem=send_sems.at[k - 1], recv_sem=recv_sems.at[k - 1],
                device_id=(px, py, pc), device_id_type=MESH))
        for cp in sends:
            cp.start()
        for k in range(1, 8):
            peer = me ^ k
            pltpu.make_async_remote_copy(
                src_ref=x_ref, dst_ref=gath.at[peer], send_sem=send_sems.at[k - 1], recv_sem=recv_sems.at[k - 1],
                device_id=(x, y, c), device_id_type=MESH).wait_recv()
        for cp in sends:
            cp.wait_send()
        acc = gath[0]
        for d in range(1, 8):
            acc = acc + gath[d]
        sum_ref[...] = acc

    vm = pl.BlockSpec(memory_space=pltpu.VMEM)
    return pl.pallas_call(
        body, name=name, in_specs=[vm], out_specs=vm, out_shape=jax.ShapeDtypeStruct((r, n), F32),
        scratch_shapes=[pltpu.VMEM((8, r, n), F32), pltpu.SemaphoreType.DMA((7,)), pltpu.SemaphoreType.DMA((7,))],
    )(block)


WD_EXT_ROWS = 736


def _remote(src, dst, send_sems, recv_sems, k, to):
    return pltpu.make_async_remote_copy(src_ref=src, dst_ref=dst, send_sem=send_sems.at[k], recv_sem=recv_sems.at[k],
                                        device_id=to, device_id_type=MESH)


def _all_gather8_multi(name, blocks):
    nt = len(blocks)

    def body(*refs):
        x_refs, out_refs, send_sems, recv_sems = refs[:nt], refs[nt:2 * nt], refs[-2], refs[-1]
        x, y, c = _me()
        me, sibling = (x, y, c), (x, y, 1 - c)
        chips = [(1 - x, y), (x, 1 - y), (1 - x, 1 - y)]
        slot = lambda q, p: out_refs[q].at[4 * p[0] + 2 * p[1] + p[2]]

        def copies(k, blk, to, from_input=False):
            return [_remote(x_refs[q] if from_input else slot(q, blk), slot(q, blk), send_sems, recv_sems, k * nt + q, to)
                    for q in range(nt)]

        first = copies(0, me, sibling, True)
        for j, chip in enumerate(chips):
            first += copies(1 + j, me, (*chip, c), True)
        for cp in first:
            cp.start()
        passed = []
        for j, chip in enumerate(chips):
            for cp in copies(1 + j, (*chip, c), me):
                cp.wait_recv()
            fwd = copies(4 + j, (*chip, c), sibling)
            for cp in fwd:
                cp.start()
            passed += fwd
        for cp in copies(0, sibling, me):
            cp.wait_recv()
        back = copies(7, sibling, sibling)
        for cp in back:
            cp.start()
        for j, chip in enumerate(chips):
            for cp in copies(4 + j, (*chip, 1 - c), me):
                cp.wait_recv()
        for cp in copies(7, me, me):
            cp.wait_recv()
        for cp in first + passed + back:
            cp.wait_send()

    return pl.pallas_call(
        body, name=name, in_specs=[ANY] * nt, out_specs=[ANY] * nt,
        out_shape=[jax.ShapeDtypeStruct((8,) + b.shape, b.dtype) for b in blocks],
        scratch_shapes=[pltpu.SemaphoreType.DMA((8 * nt,)), pltpu.SemaphoreType.DMA((8 * nt,))],
    )(*blocks)


def _swap_halves_multi(name, gs):
    nt = len(gs)
    n_chip = gs[0].shape[0]

    def body(*refs):
        g_refs, got_refs, send_sems, recv_sems = refs[:nt], refs[nt:2 * nt], refs[-2], refs[-1]
        x, y, c = _me()
        cps = [_remote(g_refs[q].at[j, 1 - c], got_refs[q].at[j], send_sems, recv_sems, q * n_chip + j, (x, y, 1 - c))
               for q in range(nt) for j in range(n_chip)]
        for cp in cps:
            cp.start()
        for cp in cps:
            cp.wait()

    return pl.pallas_call(
        body, name=name, in_specs=[ANY] * nt, out_specs=[ANY] * nt,
        out_shape=[jax.ShapeDtypeStruct((g.shape[0],) + g.shape[2:], g.dtype) for g in gs],
        scratch_shapes=[pltpu.SemaphoreType.DMA((nt * n_chip,)), pltpu.SemaphoreType.DMA((nt * n_chip,))],
    )(*gs)


def _swap_sibling_multi(name, xs):
    nt = len(xs)

    def body(*refs):
        x_refs, out_refs, send_sems, recv_sems = refs[:nt], refs[nt:2 * nt], refs[-2], refs[-1]
        x, y, c = _me()
        cps = [_remote(x_refs[q], out_refs[q], send_sems, recv_sems, q, (x, y, 1 - c)) for q in range(nt)]
        for cp in cps:
            cp.start()
        for cp in cps:
            cp.wait()

    return pl.pallas_call(
        body, name=name, in_specs=[ANY] * nt, out_specs=[ANY] * nt,
        out_shape=[jax.ShapeDtypeStruct(a.shape, a.dtype) for a in xs],
        scratch_shapes=[pltpu.SemaphoreType.DMA((nt,)), pltpu.SemaphoreType.DMA((nt,))],
    )(*xs)


def _chip_exchange_multi(name, ps):
    nt = len(ps)

    def body(*refs):
        p_refs, out_refs, bounce_refs = refs[:nt], refs[nt:2 * nt], refs[2 * nt:3 * nt]
        send_sems, recv_sems = refs[-2], refs[-1]
        x, y, c = _me()
        my_chip = 2 * x + y
        sibling = (x, y, 1 - c)
        chips = [(1 - x, y), (x, 1 - y), (1 - x, 1 - y)]
        cp = lambda k, q, src, dst, to: _remote(src, dst, send_sems, recv_sems, k * nt + q, to)
        sends = [cp(k, q, p_refs[q].at[2 * px + py], out_refs[q].at[my_chip], (px, py, c))
                 for k, (px, py) in enumerate(chips) for q in range(nt)]
        sends += [cp(3, q, p_refs[q].at[my_chip], bounce_refs[q], sibling) for q in range(nt)]
        for s_ in sends:
            s_.start()
        backs = []
        for q in range(nt):
            cp(3, q, p_refs[q].at[my_chip], bounce_refs[q], sibling).wait_recv()
            backs.append(cp(4, q, bounce_refs[q], out_refs[q].at[my_chip], sibling))
            backs[-1].start()
        for k, (px, py) in enumerate(chips):
            for q in range(nt):
                cp(k, q, p_refs[q].at[my_chip], out_refs[q].at[2 * px + py], (px, py, c)).wait_recv()
        for q in range(nt):
            cp(4, q, bounce_refs[q], out_refs[q].at[my_chip], sibling).wait_recv()
        for s_ in sends + backs:
            s_.wait_send()

    outs = pl.pallas_call(
        body, name=name, in_specs=[ANY] * nt, out_specs=[ANY] * (2 * nt),
        out_shape=[jax.ShapeDtypeStruct(p.shape, p.dtype) for p in ps]
        + [jax.ShapeDtypeStruct(p.shape[1:], p.dtype) for p in ps],
        scratch_shapes=[pltpu.SemaphoreType.DMA((5 * nt,)), pltpu.SemaphoreType.DMA((5 * nt,))],
    )(*ps)
    return outs[:nt]


def _row_tile(m):
    return m if m <= 384 else 128


def _add2_rows(name, a, b):
    n4, m, n = a.shape
    tm = _row_tile(m)
    out = _rows(name, lambda p, q: ((p + q,), ()), n4 * m, tm,
                [_rb(a.reshape(n4 * m, n), tm, n), _rb(b.reshape(n4 * m, n), tm, n)], [(n, BF16)])[0]
    return out.reshape(n4, m, n)


def _add4_rows(name, p):
    _, m, n = p.shape
    tm = _row_tile(m)
    nb = m // tm
    flat = p.reshape(4 * m, n)
    ins = [(flat, (tm, n), (lambda i, j=j: (j * nb + i, 0))) for j in range(4)]
    f32 = lambda v: v.astype(F32)
    return _rows(name, lambda a, b, c, d: ((((f32(a) + f32(b)) + f32(c)) + f32(d),), ()), m, tm, ins, [(n, F32)])[0]


def _in_proj_with_gather(n1, w_main, blocks):
    m, k = n1.shape
    n = w_main.shape[1]
    tm, tn = min(1024, m), 1024
    gi, gj = m // tm, n // tn
    last, mid = gi * gj - 1, (gi * gj) // 2
    nt = len(blocks)

    def body(*refs):
        a_ref, b_ref, x_refs, o_ref, out_refs = refs[0], refs[1], refs[2:2 + nt], refs[2 + nt], refs[3 + nt:3 + 2 * nt]
        send_sems, recv_sems = refs[-2], refs[-1]
        step = pl.program_id(0) * gj + pl.program_id(1)
        x, y, c = _me()
        me, sibling = (x, y, c), (x, y, 1 - c)
        chips = [(1 - x, y), (x, 1 - y), (1 - x, 1 - y)]
        slot = lambda q, p: out_refs[q].at[4 * p[0] + 2 * p[1] + p[2]]

        def copies(kk, blk, to, from_input=False):
            return [_remote(x_refs[q] if from_input else slot(q, blk), slot(q, blk), send_sems, recv_sems, kk * nt + q, to)
                    for q in range(nt)]

        def first():
            out = copies(0, me, sibling, True)
            for j, chip in enumerate(chips):
                out += copies(1 + j, me, (*chip, c), True)
            return out

        @pl.when(step == 0)
        def _():
            for cp in first():
                cp.start()

        @pl.when(step == mid)
        def _():
            for j, chip in enumerate(chips):
                for cp in copies(1 + j, (*chip, c), me):
                    cp.wait_recv()
                for cp in copies(4 + j, (*chip, c), sibling):
                    cp.start()
            for cp in copies(0, sibling, me):
                cp.wait_recv()
            for cp in copies(7, sibling, sibling):
                cp.start()

        o_ref[...] = _dot(a_ref[...], b_ref[...], NN)

        @pl.when(step == last)
        def _():
            for j, chip in enumerate(chips):
                for cp in copies(4 + j, (*chip, 1 - c), me):
                    cp.wait_recv()
            for cp in copies(7, me, me):
                cp.wait_recv()
            sent = first() + copies(7, sibling, sibling)
            for j, chip in enumerate(chips):
                sent += copies(4 + j, (*chip, c), sibling)
            for cp in sent:
                cp.wait_send()

    outs = pl.pallas_call(
        body, name="in_proj", grid=(gi, gj),
        in_specs=[pl.BlockSpec((tm, k), lambda i, j: (i, 0)), pl.BlockSpec((k, tn), lambda i, j: (0, j))] + [ANY] * nt,
        out_specs=[pl.BlockSpec((tm, tn), lambda i, j: (i, j))] + [ANY] * nt,
        out_shape=[jax.ShapeDtypeStruct((m, n), F32)] + [jax.ShapeDtypeStruct((8,) + b.shape, b.dtype) for b in blocks],
        scratch_shapes=[pltpu.SemaphoreType.DMA((8 * nt,)), pltpu.SemaphoreType.DMA((8 * nt,))],
        compiler_params=_cparams(("arbitrary", "arbitrary")),
    )(n1, w_main, *blocks)
    return outs[0], outs[1:]


def _weight_halves(w_in, w_a, w_b, w_out, w_up, w_down, conv_w):
    c = lax.axis_index("c")
    bits = lax.bitcast_convert_type(conv_w, BF16).reshape(-1)
    extra = jnp.zeros(((WD_EXT_ROWS - W_DOWN_SHARD) * D,), BF16).at[:bits.shape[0]].set(bits)
    wd_ext = jnp.concatenate([w_down.astype(BF16), extra.reshape(-1, D)], axis=0)
    shards = [w_in.astype(BF16), w_a.astype(BF16), w_b.astype(BF16), w_out.astype(BF16), w_up.astype(BF16), wd_ext]
    return [lax.dynamic_slice_in_dim(t, c * (t.shape[0] // 2), t.shape[0] // 2, axis=0) for t in shards]


def _unpack_w_in(gathered):
    wi = gathered.reshape(N_CHIP, D, W_IN_SHARD).transpose(1, 0, 2).reshape(D, N_CHIP * W_IN_SHARD)
    w_main = jnp.concatenate([wi[:, :FF_COL], wi[:, FF_COL + FOX_H:]], axis=1)
    return w_main, jnp.pad(wi[:, FF_COL:FF_COL + FOX_H], ((0, 0), (0, LANES - FOX_H)))


def _unpack_later_weights(gathered):
    full = [g.reshape((N_CHIP, 2 * g.shape[1]) + g.shape[2:]) for g in gathered]
    wa, wb, wo = (full[i].reshape(D, D) for i in (0, 1, 2))
    wu = full[3].transpose(1, 0, 2).reshape(D, 2 * D_FF)
    wd = full[4][:, :W_DOWN_SHARD].reshape(D_FF, D)
    n_bits = 3 * W_UP_SHARD * 2
    cw_bits = full[4][:, W_DOWN_SHARD:].reshape(N_CHIP, -1)[:, :n_bits].reshape(N_CHIP, 3, W_UP_SHARD, 2)
    cw = lax.bitcast_convert_type(cw_bits, F32).transpose(1, 0, 2).reshape(3, 2 * D_FF)
    return wa, wb, wo, wu, wd, cw


def _reduce_scatter_grads2(d_main, d_ff, d_a, d_b, d_o, d_u, d_d):
    c = lax.axis_index("c")
    d_in = jnp.concatenate(d_main[:7] + [d_ff[:, :FOX_H]] + d_main[7:], axis=1)
    per_chip = [
        d_in.reshape(D, N_CHIP, W_IN_SHARD).transpose(1, 0, 2),
        d_a.reshape(N_CHIP, -1, D), d_b.reshape(N_CHIP, -1, D), d_o.reshape(N_CHIP, -1, D),
        d_u.reshape(D, N_CHIP, W_UP_SHARD).transpose(1, 0, 2),
        d_d.reshape(N_CHIP, -1, D),
    ]
    gs = [t.reshape(N_CHIP, 2, t.shape[1] // 2, t.shape[2]) for t in per_chip]
    got = _swap_halves_multi("grad_swap_halves", gs)
    sums = [_add2_rows("grad_chip_sum_%d" % q, lax.dynamic_index_in_dim(g, c, axis=1, keepdims=False), s_)
            for q, (g, s_) in enumerate(zip(gs, got))]
    pieces = _chip_exchange_multi("grad_chip_exchange", sums)
    mine = [_add4_rows("grad_sum_chips_%d" % q, p) for q, p in enumerate(pieces)]
    other = _swap_sibling_multi("grad_share_half", mine)
    return [jnp.concatenate([jnp.where(c == 0, a, b), jnp.where(c == 0, b, a)], axis=0) for a, b in zip(mine, other)]


def _local_step(x, target, norm_mix, fox_f_bias, hg_lb_logits, hg_norm, norm_ffn, conv_b, norm_final,
                w_main, w_ff, later):
    bias = jnp.pad(fox_f_bias, ((0, 0), (0, LANES - FOX_H)))

    n1, n1t = _rms_fwd("norm_mix_fwd", x, norm_mix)
    if len(later) == 5:
        proj, gathered = _in_proj_with_gather(n1, w_main, later)
        wa, wb, wo, wu, wd, conv_w = _unpack_later_weights(gathered)
    else:
        proj = _mm("in_proj", n1, w_main, "nn", F32, 1024, 1024, D)
        wa, wb, wo, wu, wd, conv_w = later
    conv_w8 = jnp.pad(conv_w, ((0, 5), (0, 0)))
    pff = _mm("in_proj_forget", n1, w_ff, "nn", F32, 1024, LANES, D)
    qb, kb, vb, ka = _fox_prep2(proj, pff, bias)
    o_b, laux = _fox_fwd2(qb, kb, vb, ka)
    o_raw, states = _hg_fwd(proj, hg_lb_logits)
    o_a = _hg_post_fwd(o_raw, proj, hg_norm)
    pa = _mm("branch_a", o_a, wa, "nn", F32, 1024, 1024, D)
    pb = _mm("branch_b", o_b, wb, "nn", F32, 1024, 1024, D)
    merged = _merge_fwd(pa, pb, proj)
    h1 = _mm("out_proj", merged, wo, "nn", F32, 1024, 1024, D, res=x)
    n2, n2t = _rms_fwd("norm_ffn_fwd", h1, norm_ffn)
    u = _mm("ffn_up", n2, wu, "nn", F32, 1024, W_UP_SHARD, D)
    act, gelu_gate, dact_dgate = _convglu_fwd(u, conv_w8, conv_b)
    h2 = _mm("ffn_down", act, wd, "nn", F32, 512, 1024, D_FF, res=h1)
    (dh2,), (d_norm_final, loss_row) = _final(h2, target, norm_final)

    dact = _mm("ffn_down_dx", dh2, wd, "nt", BF16, 1024, D_FF, D)
    d_wd = _mm("ffn_down_dw", act, dh2, "tn", F32, D_FF // 2, 1024, DW_TK // 2)
    (du,), (d_conv_w8, d_conv_b) = _convglu_bwd(u, dact, gelu_gate, dact_dgate, conv_w8)
    dn2 = _mm("ffn_up_dx", du, wu, "nt", F32, 1024, 1024, W_UP_SHARD)
    d_wu = _mm("ffn_up_dw", n2t, du, "nn", F32, 1024, W_UP_SHARD, DW_TK)
    (dh1,), (d_norm_ffn,) = _rms_bwd("norm_ffn_bwd", h1, norm_ffn, [dn2], dh2)

    dmerged = _mm("out_proj_dx", dh1, wo, "nt", F32, 1024, 1024, D)
    d_wo = _mm("out_proj_dw", merged, dh1, "tn", F32, 1024, 1024, DW_TK)
    dpa, dpb, dga, dgb = _merge_bwd(dmerged, pa, pb, proj)
    do_a = _mm("branch_a_dx", dpa, wa, "nt", F32, 1024, 1024, D)
    do_b = _mm("branch_b_dx", dpb, wb, "nt", BF16, 1024, 1024, D)
    d_wa = _mm("branch_a_dw", o_a, dpa, "tn", F32, 1024, 1024, DW_TK)
    d_wb = _mm("branch_b_dw", o_b, dpb, "tn", F32, 1024, 1024, DW_TK)

    (do_raw, dhg), (d_hg_norm,) = _hg_post_bwd(do_a, o_raw, proj, hg_norm)
    dhq, dhf, dhi, d_lb_logits = _hg_bwd(proj, hg_lb_logits, states, do_raw)

    dfq, dfk, dfv, dcrow = _fox_bwd2(qb, kb, vb, ka, o_b, laux, do_b)
    dct = jnp.pad(dcrow.reshape(FOX_H, x.shape[0]), ((0, LANES - FOX_H), (0, 0)))
    dff, d_bias = _fox_gate_bwd(dct, pff, bias)

    pieces = [dhq, dhf, dhi, dhg, dfq, dfk, dfv, dga, dgb]
    dn1 = _mm_sum_nt("in_proj_dx", pieces, w_main, (dff, w_ff), 1024, 1024)
    d_w_main = [_mm("in_proj_dw_%d" % i, n1t, p, "nn", F32, 1024, 1024, DW_TK) for i, p in enumerate(pieces)]
    d_w_ff = _mm("in_proj_forget_dw", n1t, dff, "nn", F32, 1024, LANES, DW_TK)
    (dx,), (d_norm_mix,) = _rms_bwd("norm_mix_bwd", x, norm_mix, [dn1], dh1)

    small = dict(norm_mix=d_norm_mix, fox_f_bias=d_bias[:, :FOX_H], hg_lb_logits=d_lb_logits, hg_norm=d_hg_norm,
                 norm_ffn=d_norm_ffn, conv_b=d_conv_b, norm_final=d_norm_final, conv_w=d_conv_w8[:3], loss=loss_row)
    big = (d_w_main, d_w_ff, d_wa, d_wb, d_wo, d_wu, d_wd)
    return dx, small, big


SMALL_KEYS = ("norm_mix", "fox_f_bias", "hg_lb_logits", "hg_norm", "norm_ffn", "conv_b", "norm_final")


def _pack_small(parts):
    rows, layout = [], []
    for key, arr in parts:
        flat = arr.reshape(-1)
        n = flat.shape[0]
        nr = -(-n // LANES)
        rows.append(jnp.pad(flat, (0, nr * LANES - n)).reshape(nr, LANES))
        layout.append((key, arr.shape, n, nr))
    packed = jnp.concatenate(rows, axis=0)
    pad = -packed.shape[0] % 8
    return jnp.pad(packed, ((0, pad), (0, 0))), layout


def _unpack_small(packed, layout):
    out, r0 = {}, 0
    for key, shape, n, nr in layout:
        out[key] = packed[r0:r0 + nr].reshape(-1)[:n].reshape(shape)
        r0 += nr
    return out


def kernel(x, norm_mix, w_in, fox_f_bias, hg_lb_logits, hg_norm, w_branch_a, w_branch_b, w_out, norm_ffn, w_up, conv_w, conv_b, w_down, norm_final, loss_target, m_norm_mix, m_w_in, m_fox_f_bias, m_hg_lb_logits, m_hg_norm, m_w_branch_a, m_w_branch_b, m_w_out, m_norm_ffn, m_w_up, m_conv_w, m_conv_b, m_w_down, m_norm_final, v_norm_mix, v_w_in, v_fox_f_bias, v_hg_lb_logits, v_hg_norm, v_w_branch_a, v_w_branch_b, v_w_out, v_norm_ffn, v_w_up, v_conv_w, v_conv_b, v_w_down, v_norm_final):
    chip = 2 * lax.axis_index("x") + lax.axis_index("y")
    halves = _weight_halves(w_in[0], w_branch_a[0], w_branch_b[0], w_out[0], w_up[0], w_down[0], conv_w[0])
    w_main, w_ff = _unpack_w_in(_all_gather8_multi("all_gather_w_in", halves[:1])[0])
    dx, small, big = _local_step(
        x[0], loss_target[0], norm_mix, fox_f_bias, hg_lb_logits, hg_norm, norm_ffn, conv_b,
        norm_final.reshape(1, D), w_main, w_ff, halves[1:])

    packed, layout = _pack_small([(k, small[k]) for k in SMALL_KEYS + ("conv_w", "loss")])
    red = _unpack_small(_all_reduce_small("all_reduce_small", packed), layout)
    loss = red["loss"][0, 0]
    g_conv_w = lax.dynamic_slice_in_dim(red["conv_w"], chip * W_UP_SHARD, W_UP_SHARD, axis=1)

    g_big = _reduce_scatter_grads2(*big)

    names = ["norm_mix", "w_in", "fox_f_bias", "hg_lb_logits", "hg_norm", "w_branch_a", "w_branch_b", "w_out",
             "norm_ffn", "w_up", "conv_w", "conv_b", "w_down", "norm_final"]
    weights = dict(norm_mix=norm_mix, w_in=w_in, fox_f_bias=fox_f_bias, hg_lb_logits=hg_lb_logits, hg_norm=hg_norm,
                   w_branch_a=w_branch_a, w_branch_b=w_branch_b, w_out=w_out, norm_ffn=norm_ffn, w_up=w_up,
                   conv_w=conv_w, conv_b=conv_b, w_down=w_down, norm_final=norm_final)
    ms = dict(norm_mix=m_norm_mix, w_in=m_w_in, fox_f_bias=m_fox_f_bias, hg_lb_logits=m_hg_lb_logits,
              hg_norm=m_hg_norm, w_branch_a=m_w_branch_a, w_branch_b=m_w_branch_b, w_out=m_w_out,
              norm_ffn=m_norm_ffn, w_up=m_w_up, conv_w=m_conv_w, conv_b=m_conv_b, w_down=m_w_down,
              norm_final=m_norm_final)
    vs = dict(norm_mix=v_norm_mix, w_in=v_w_in, fox_f_bias=v_fox_f_bias, hg_lb_logits=v_hg_lb_logits,
              hg_norm=v_hg_norm, w_branch_a=v_w_branch_a, w_branch_b=v_w_branch_b, w_out=v_w_out,
              norm_ffn=v_norm_ffn, w_up=v_w_up, conv_w=v_conv_w, conv_b=v_conv_b, w_down=v_w_down,
              norm_final=v_norm_final)

    grads, deltas, new_m, new_v = {}, {}, {}, {}
    big_names = ["w_in", "w_branch_a", "w_branch_b", "w_out", "w_up", "w_down"]
    for name, g2 in zip(big_names, g_big):
        shape = weights[name].shape
        rows = g2.shape[0]
        d_, m_, v_ = _adamw("adamw_" + name, weights[name][0], g2, ms[name][0], vs[name][0], tm=rows // 8)
        grads[name], deltas[name], new_m[name], new_v[name] = (a.reshape(shape) for a in (g2, d_, m_, v_))
    shape = conv_w.shape
    d_, m_, v_ = _adamw("adamw_conv_w", conv_w[0], g_conv_w, m_conv_w[0], v_conv_w[0])
    grads["conv_w"], deltas["conv_w"], new_m["conv_w"], new_v["conv_w"] = (
        a.reshape(shape) for a in (g_conv_w, d_, m_, v_))
    gs = {k: red[k].reshape(weights[k].shape) for k in SMALL_KEYS}
    pw, lay = _pack_small([(k, weights[k]) for k in SMALL_KEYS])
    pg, _ = _pack_small([(k, gs[k]) for k in SMALL_KEYS])
    pm, _ = _pack_small([(k, ms[k]) for k in SMALL_KEYS])
    pv, _ = _pack_small([(k, vs[k]) for k in SMALL_KEYS])
    d_, m_, v_ = (_unpack_small(a, lay) for a in _adamw("adamw_small", pw, pg, pm, pv))
    for k in SMALL_KEYS:
        grads[k], deltas[k], new_m[k], new_v[k] = gs[k], d_[k], m_[k], v_[k]

    return (loss, dx[None], *[grads[n] for n in names], *[deltas[n] for n in names],
            *[new_m[n] for n in names], *[new_v[n] for n in names])
```

```python
import jax
import jax.numpy as jnp
from jax import lax
from jax.experimental import pallas as pl
from jax.experimental.pallas import tpu as pltpu

F32 = jnp.float32
BF16 = jnp.bfloat16

D = 1024
HG_H, HG_DK = 8, 128
FOX_H, FOX_D = 16, 64
D_FF = 2816
EPS = 1e-6
N_CHIP = 4
LANES = 128
W_IN_SHARD = 2308
W_UP_SHARD = 1408
W_DOWN_SHARD = 704
FF_COL = 7168
ADAM_LR, ADAM_B1, ADAM_B2, ADAM_EPS, ADAM_WD, ADAM_STEP = 0.001, 0.9, 0.999, 1e-08, 0.01, 10

HG_C = 16
HG_T = 512
HG_UNROLL = 16
HG_UNROLL_BWD = 4
FOX_T = 512
DW_TK = 2048
VMEM_LIMIT = 56 * 1024 * 1024
MESH = pl.DeviceIdType.MESH
ANY = pl.BlockSpec(memory_space=pl.ANY)


def _cparams(sem):
    return pltpu.CompilerParams(dimension_semantics=sem, vmem_limit_bytes=VMEM_LIMIT)


def _sigmoid(x):
    return 1.0 / (1.0 + jnp.exp(-x))


def _dot(a, b, dims):
    return lax.dot_general(a.astype(BF16), b.astype(BF16), (dims, ((), ())), preferred_element_type=F32)


NN = ((1,), (0,))
NT = ((1,), (1,))
TN = ((0,), (0,))


def _split_dot(tri, x, parts, dims=NN):
    acc = None
    r = x
    for _ in range(parts):
        p = r.astype(BF16)
        t = lax.dot_general(tri, p, (dims, ((), ())), preferred_element_type=F32)
        acc = t if acc is None else acc + t
        r = r - p.astype(F32)
    return acc


def _rb(arr, tm, width, cb=0):
    return (arr, (tm, width), lambda i: (i, cb))


def _cst(arr):
    return (arr, arr.shape, lambda i: (0,) * arr.ndim)


def _rows(name, fn, n_rows, tm, ins, outs, accs=()):
    n_in, n_out, n_acc = len(ins), len(outs), len(accs)
    nb = n_rows // tm

    def body(*refs):
        vals = [r[...] for r in refs[:n_in]]
        o, a = fn(*vals)
        for r, v in zip(refs[n_in:n_in + n_out], o):
            r[...] = v.astype(r.dtype)
        if n_acc:
            acc_refs = refs[n_in + n_out:]

            @pl.when(pl.program_id(0) == 0)
            def _():
                for r in acc_refs:
                    r[...] = jnp.zeros_like(r)

            for r, v in zip(acc_refs, a):
                r[...] += v

    in_specs = [pl.BlockSpec(bs, im) for (_, bs, im) in ins]
    out_specs = [pl.BlockSpec((tm, w), lambda i: (i, 0)) for (w, _) in outs]
    out_specs += [pl.BlockSpec((r, w), lambda i: (0, 0)) for (r, w) in accs]
    out_shape = [jax.ShapeDtypeStruct((n_rows, w), dt) for (w, dt) in outs]
    out_shape += [jax.ShapeDtypeStruct((r, w), F32) for (r, w) in accs]
    res = pl.pallas_call(
        body, name=name, grid=(nb,), in_specs=in_specs, out_specs=out_specs, out_shape=out_shape,
        compiler_params=_cparams(("arbitrary",)),
    )(*[a for a, _, _ in ins])
    return (res[:n_out], res[n_out:]) if n_acc else res


def _mm(name, a, b, mode, out_dtype, tm, tn, tk, res=None):
    if mode == "nn":
        (m, k), n = a.shape, b.shape[1]
    elif mode == "nt":
        (m, k), n = a.shape, b.shape[0]
    else:
        (k, m), n = a.shape, b.shape[1]
    tm, tn, tk = min(tm, m), min(tn, n), min(tk, k)
    assert m % tm == 0 and n % tn == 0 and k % tk == 0, (name, m, n, k, tm, tn, tk)
    if mode == "nn":
        a_spec = pl.BlockSpec((tm, tk), lambda i, j, kk: (i, kk))
        b_spec = pl.BlockSpec((tk, tn), lambda i, j, kk: (kk, j))
        dims = NN
    elif mode == "nt":
        a_spec = pl.BlockSpec((tm, tk), lambda i, j, kk: (i, kk))
        b_spec = pl.BlockSpec((tn, tk), lambda i, j, kk: (j, kk))
        dims = NT
    else:
        a_spec = pl.BlockSpec((tk, tm), lambda i, j, kk: (kk, i))
        b_spec = pl.BlockSpec((tk, tn), lambda i, j, kk: (kk, j))
        dims = TN
    nk = k // tk
    has_res = res is not None
    acc_in_out = out_dtype == F32 and not has_res

    def body(*refs):
        a_ref, b_ref = refs[0], refs[1]
        r_ref = refs[2] if has_res else None
        o_ref = refs[3] if has_res else refs[2]
        part = _dot(a_ref[...], b_ref[...], dims)

        def finish(val):
            if has_res:
                val = val + r_ref[...]
            o_ref[...] = val.astype(o_ref.dtype)

        if nk == 1:
            finish(part)
        elif acc_in_out:
            kk = pl.program_id(2)

            @pl.when(kk == 0)
            def _():
                o_ref[...] = part

            @pl.when(kk > 0)
            def _():
                o_ref[...] += part
        else:
            acc_ref = refs[-1]
            kk = pl.program_id(2)

            @pl.when(kk == 0)
            def _():
                acc_ref[...] = part

            @pl.when(kk > 0)
            def _():
                acc_ref[...] += part

            @pl.when(kk == nk - 1)
            def _():
                finish(acc_ref[...])

    in_specs = [a_spec, b_spec]
    args = [a, b]
    if has_res:
        in_specs.append(pl.BlockSpec((tm, tn), lambda i, j, kk: (i, j)))
        args.append(res)
    return pl.pallas_call(
        body, name=name, grid=(m // tm, n // tn, nk), in_specs=in_specs,
        out_specs=pl.BlockSpec((tm, tn), lambda i, j, kk: (i, j)),
        out_shape=jax.ShapeDtypeStruct((m, n), out_dtype),
        scratch_shapes=[pltpu.VMEM((tm, tn), F32)] if nk > 1 and not acc_in_out else [],
        compiler_params=_cparams(("arbitrary", "arbitrary", "arbitrary")),
    )(*args)


def _mm_sum_nt(name, pieces, w, extra, tm, tn, exchange=()):
    n_p = len(pieces)
    m, k = pieces[0].shape
    n = w.shape[0]
    xa, xb = extra
    ke = xa.shape[1]
    tm, tn = min(tm, m), min(tn, n)
    nx = len(exchange)
    n_steps = (m // tm) * (n // tn) * (n_p + 1)

    def exchange_steps(refs):
        e_refs = refs[n_p + 3:n_p + 3 + nx]
        out_refs, bounce_refs = refs[n_p + 4 + nx:n_p + 4 + 2 * nx], refs[n_p + 4 + 2 * nx:n_p + 4 + 3 * nx]
        send_sems, recv_sems = refs[-2], refs[-1]
        step = (pl.program_id(0) * (n // tn) + pl.program_id(1)) * (n_p + 1) + pl.program_id(2)
        x, y, c = _me()
        my_chip = 2 * x + y
        sibling = (x, y, 1 - c)
        chips = [(1 - x, y), (x, 1 - y), (1 - x, 1 - y)]
        cp = lambda kx, q, src, dst, to: _remote(src, dst, send_sems, recv_sems, kx * nx + q, to)
        sends = lambda: ([cp(kx, q, e_refs[q].at[2 * px + py], out_refs[q].at[my_chip], (px, py, c))
                          for kx, (px, py) in enumerate(chips) for q in range(nx)]
                         + [cp(3, q, e_refs[q].at[my_chip], bounce_refs[q], sibling) for q in range(nx)])
        backs = lambda: [cp(4, q, bounce_refs[q], out_refs[q].at[my_chip], sibling) for q in range(nx)]

        @pl.when(step == 0)
        def _():
            for s_ in sends():
                s_.start()

        @pl.when(step == n_steps // 2)
        def _():
            for q in range(nx):
                cp(3, q, e_refs[q].at[my_chip], bounce_refs[q], sibling).wait_recv()
            for s_ in backs():
                s_.start()

        @pl.when(step == n_steps - 1)
        def _():
            for kx, (px, py) in enumerate(chips):
                for q in range(nx):
                    cp(kx, q, e_refs[q].at[my_chip], out_refs[q].at[2 * px + py], (px, py, c)).wait_recv()
            for q in range(nx):
                cp(4, q, bounce_refs[q], out_refs[q].at[my_chip], sibling).wait_recv()
            for s_ in sends() + backs():
                s_.wait_send()

    def body(*refs):
        p_refs, w_ref, xa_ref, xb_ref, o_ref = refs[:n_p], refs[n_p], refs[n_p + 1], refs[n_p + 2], refs[n_p + 3 + nx]
        if nx:
            exchange_steps(refs)
        kk = pl.program_id(2)

        @pl.when(kk == 0)
        def _():
            o_ref[...] = _dot(p_refs[0][...], w_ref[...], NT)

        for i in range(1, n_p):
            @pl.when(kk == i)
            def _(i=i):
                o_ref[...] += _dot(p_refs[i][...], w_ref[...], NT)

        @pl.when(kk == n_p)
        def _():
            o_ref[...] += _dot(xa_ref[...], xb_ref[...], NT)

    in_specs = [pl.BlockSpec((tm, k), lambda i, j, kk: (i, 0)) for _ in range(n_p)]
    in_specs.append(pl.BlockSpec((tn, k), lambda i, j, kk: (j, jnp.minimum(kk, n_p - 1))))
    in_specs += [pl.BlockSpec((tm, ke), lambda i, j, kk: (i, 0)), pl.BlockSpec((tn, ke), lambda i, j, kk: (j, 0))]
    outs = pl.pallas_call(
        body, name=name, grid=(m // tm, n // tn, n_p + 1), in_specs=in_specs + [ANY] * nx,
        out_specs=[pl.BlockSpec((tm, tn), lambda i, j, kk: (i, j))] + [ANY] * (2 * nx),
        out_shape=[jax.ShapeDtypeStruct((m, n), F32)] + [jax.ShapeDtypeStruct(e.shape, e.dtype) for e in exchange]
        + [jax.ShapeDtypeStruct(e.shape[1:], e.dtype) for e in exchange],
        scratch_shapes=[pltpu.SemaphoreType.DMA((5 * nx,)), pltpu.SemaphoreType.DMA((5 * nx,))] if nx else [],
        compiler_params=_cparams(("arbitrary", "arbitrary", "arbitrary")),
    )(*pieces, w, xa, xb, *exchange)
    return (outs[0], outs[1:1 + nx]) if nx else outs[0]


def _rms_fwd(name, x, gain, tm=256):
    s = x.shape[0]

    def body(x_ref, g_ref, y_ref, yt_ref):
        xb = x_ref[...]
        y = xb * lax.rsqrt(jnp.mean(xb * xb, axis=-1, keepdims=True) + EPS) * g_ref[...]
        y_ref[...] = y.astype(BF16)
        yt_ref[...] = y.T.astype(BF16)

    return pl.pallas_call(
        body, name=name, grid=(s // tm,),
        in_specs=[pl.BlockSpec((tm, D), lambda i: (i, 0)), pl.BlockSpec((1, D), lambda i: (0, 0))],
        out_specs=[pl.BlockSpec((tm, D), lambda i: (i, 0)), pl.BlockSpec((D, tm), lambda i: (0, i))],
        out_shape=[jax.ShapeDtypeStruct((s, D), BF16), jax.ShapeDtypeStruct((D, s), BF16)],
        compiler_params=_cparams(("arbitrary",)),
    )(x, gain)


def _rms_bwd(name, x, gain, dns, dres, tm=256):
    s = x.shape[0]
    n_dn = len(dns)

    def fn(xb, g, *rest):
        dn = rest[0]
        for t in rest[1:n_dn]:
            dn = dn + t
        r = lax.rsqrt(jnp.mean(xb * xb, axis=-1, keepdims=True) + EPS)
        xhat = xb * r
        dxh = dn * g
        dx = r * (dxh - xhat * jnp.mean(dxh * xhat, axis=-1, keepdims=True)) + rest[n_dn]
        return (dx,), (jnp.sum(dn * xhat, axis=0, keepdims=True),)

    ins = [_rb(x, tm, D), _cst(gain)] + [_rb(t, tm, D) for t in dns] + [_rb(dres, tm, D)]
    return _rows(name, fn, s, tm, ins, [(D, F32)], [(1, D)])


def _final(h2, target, gain, tm=256):
    s = h2.shape[0]

    def fn(hb, tb, g):
        r = lax.rsqrt(jnp.mean(hb * hb, axis=-1, keepdims=True) + EPS)
        xhat = hb * r
        e = xhat * g - tb
        dy = e * (1.0 / D)
        dxh = dy * g
        dh = r * (dxh - xhat * jnp.mean(dxh * xhat, axis=-1, keepdims=True))
        lrow = 0.5 * jnp.sum(jnp.sum(e * e, axis=-1, keepdims=True) * (1.0 / D), axis=0, keepdims=True)
        return (dh,), (jnp.sum(dy * xhat, axis=0, keepdims=True), jnp.broadcast_to(lrow, (1, LANES)))

    return _rows("final_norm_loss", fn, s, tm, [_rb(h2, tm, D), _rb(target, tm, D), _cst(gain)],
                 [(D, F32)], [(1, D), (1, LANES)])


def _merge_fwd(pa, pb, proj, tm=256):
    s = pa.shape[0]

    def fn(a, b, ga, gb):
        return (_sigmoid(ga) * a + _sigmoid(gb) * b,), ()

    ins = [_rb(pa, tm, D), _rb(pb, tm, D), _rb(proj, tm, D, 7), _rb(proj, tm, D, 8)]
    return _rows("merge_fwd", fn, s, tm, ins, [(D, BF16)])[0]


def _merge_bwd(dmerged, pa, pb, proj, tm=256):
    s = pa.shape[0]

    def fn(dm, a, b, ga, gb):
        sa, sb = _sigmoid(ga), _sigmoid(gb)
        return (dm * sa, dm * sb, dm * a * sa * (1.0 - sa), dm * b * sb * (1.0 - sb)), ()

    ins = [_rb(dmerged, tm, D), _rb(pa, tm, D), _rb(pb, tm, D), _rb(proj, tm, D, 7), _rb(proj, tm, D, 8)]
    return _rows("merge_bwd", fn, s, tm, ins, [(D, BF16), (D, BF16), (D, BF16), (D, BF16)])


def _gelu_parts(x):
    cdf = 0.5 * (1.0 + lax.erf(x * 0.7071067811865476))
    pdf = 0.3989422804014327 * jnp.exp(-0.5 * x * x)
    return x * cdf, cdf + x * pdf


def _conv_taps(u_ext, n_out):
    cur = u_ext[8:8 + n_out]
    m1 = pltpu.roll(u_ext, 1, 0)[8:8 + n_out]
    m2 = pltpu.roll(u_ext, 2, 0)[8:8 + n_out]
    return m2, m1, cur


def _convglu_fwd(u, conv_w8, conv_b, tm=256):
    s, w = u.shape
    tb = tm // 8

    def fn(ub, up, cw, cb):
        i = pl.program_id(0)
        up = jnp.where(i == 0, 0.0, up)
        m2, m1, cur = _conv_taps(jnp.concatenate([up, ub], axis=0), tm)
        acc = cb + cw[0:1] * m2 + cw[1:2] * m1 + cw[2:3] * cur
        gl, dgl = _gelu_parts(acc[:, :D_FF])
        val = acc[:, D_FF:]
        return (gl * val, gl, val * dgl), ()

    ins = [_rb(u, tm, w), (u, (8, w), lambda i: (jnp.maximum(i * tb - 1, 0), 0)), _cst(conv_w8), _cst(conv_b)]
    return _rows("convglu_fwd", fn, s, tm, ins, [(D_FF, BF16)] * 3)


def _convglu_bwd(u, dact, gl, gd, conv_w8, tm=256):
    s, w = u.shape
    tb = tm // 8
    nb = s // tm

    def fn(ub, up, db, dn, glb, gln, gdb, gdn, cw):
        i = pl.program_id(0)
        up = jnp.where(i == 0, 0.0, up)
        dn = jnp.where(i == nb - 1, 0.0, dn.astype(F32))
        ne = tm + 8
        m2, m1, cur = _conv_taps(jnp.concatenate([up, ub], axis=0), tm)
        ext = lambda blk, nxt: jnp.concatenate([blk.astype(F32), nxt.astype(F32)], axis=0)
        de = ext(db, dn)
        dacc = jnp.concatenate([de * ext(gdb, gdn), de * ext(glb, gln)], axis=1)
        p1 = pltpu.roll(dacc, ne - 1, 0)[:tm]
        p2 = pltpu.roll(dacc, ne - 2, 0)[:tm]
        d0 = dacc[:tm]
        du = cw[2:3] * d0 + cw[1:2] * p1 + cw[0:1] * p2
        zero5 = jnp.zeros((5, w), F32)
        dcw = jnp.concatenate([
            jnp.sum(d0 * m2, axis=0, keepdims=True), jnp.sum(d0 * m1, axis=0, keepdims=True),
            jnp.sum(d0 * cur, axis=0, keepdims=True), zero5], axis=0)
        return (du,), (dcw, jnp.sum(d0, axis=0, keepdims=True))

    nxt = lambda arr: (arr, (8, D_FF), lambda i: (jnp.minimum((i + 1) * tb, s // 8 - 1), 0))
    ins = [_rb(u, tm, w), (u, (8, w), lambda i: (jnp.maximum(i * tb - 1, 0), 0)),
           _rb(dact, tm, D_FF), nxt(dact), _rb(gl, tm, D_FF), nxt(gl), _rb(gd, tm, D_FF), nxt(gd), _cst(conv_w8)]
    return _rows("convglu_bwd", fn, s, tm, ins, [(w, BF16)], [(8, w), (1, w)])


def _chunk_scan(x, t_iota, reverse):
    k = 1
    while k < HG_C:
        if reverse:
            x = x + jnp.where(t_iota < HG_C - k, pltpu.roll(x, HG_C - k, 0), 0.0)
        else:
            x = x + jnp.where(t_iota >= k, pltpu.roll(x, k, 0), 0.0)
        k *= 2
    return x


def _hg_gates(hq, hf, lb):
    sq = _sigmoid(hq)
    q = hq * sq
    sg = _sigmoid(hf)
    f = lb + (1.0 - lb) * sg
    return q, sq, sg, f, 1.0 - f, jnp.log(f)


def _lb_of(logits):
    l0, l1 = logits[0:1], logits[1:2]
    mx = jnp.maximum(l0, l1)
    e0, e1 = jnp.exp(l0 - mx), jnp.exp(l1 - mx)
    return e0 / (e0 + e1)


def _tri(n, lower):
    r = lax.broadcasted_iota(jnp.int32, (n, n), 0)
    c = lax.broadcasted_iota(jnp.int32, (n, n), 1)
    return jnp.where((r >= c) if lower else (r <= c), 1.0, 0.0).astype(BF16)


def _hg_intra_terms(q, kk, b, t_iota):
    ws, ps = [], []
    for s in range(HG_C):
        p = jnp.where(t_iota >= s, jnp.exp(b - b[s:s + 1]), 0.0)
        ps.append(p)
        ws.append(q * kk[s:s + 1] * p)
    return jnp.concatenate(ws, axis=0), ps


def _hg_fwd(proj, lb_logits):
    s = proj.shape[0]
    nt = s // HG_T
    nc = HG_T // HG_C

    def body(q_ref, f_ref, i_ref, l_ref, o_ref, st_ref, state):
        @pl.when(pl.program_id(1) == 0)
        def _():
            state[...] = jnp.zeros_like(state)

        st_ref[0, 0] = state[...]
        lb = _lb_of(l_ref[...])
        ones = jnp.ones((HG_DK, HG_DK), BF16)
        t_iota = lax.broadcasted_iota(jnp.int32, (HG_C, HG_DK), 0)
        cc = HG_C * HG_C

        def group(gi, st):
            units = []
            for u in range(HG_UNROLL):
                r = pl.ds(pl.multiple_of((gi * HG_UNROLL + u) * HG_C, HG_C), HG_C)
                q, _, _, _, kk, g = _hg_gates(q_ref[r, :], f_ref[r, :], lb)
                b = _chunk_scan(g, t_iota, False)
                b_end = b[HG_C - 1:HG_C]
                w_all, _ = _hg_intra_terms(q, kk, b, t_iota)
                units.append((r, i_ref[r, :], q * jnp.exp(b), jnp.exp(b_end), kk * jnp.exp(b_end - b), w_all))
            a_all = _dot(jnp.concatenate([un[5] for un in units], axis=0), ones, NN)
            kvs = [_dot(v, kd, TN) for (_, v, _, _, kd, _) in units]
            sts = [st]
            for (_, _, _, dec, _, _), kv in zip(units, kvs):
                sts.append(sts[-1] * dec + kv)
            for ui, (r, v, qd, _, _, _) in enumerate(units):
                o = _dot(qd, sts[ui], NT)
                for si in range(HG_C):
                    o = o + a_all[ui * cc + si * HG_C:ui * cc + (si + 1) * HG_C] * v[si:si + 1]
                o_ref[r, :] = o
            return sts[-1]

        state[...] = lax.fori_loop(0, nc // HG_UNROLL, group, state[...])

    col = lambda off: pl.BlockSpec((HG_T, HG_DK), lambda h, t: (t, off + h))
    return pl.pallas_call(
        body, name="hgrn2_fwd", grid=(HG_H, nt),
        in_specs=[col(0), col(8), col(16), pl.BlockSpec((2, HG_DK), lambda h, t: (0, h))],
        out_specs=[pl.BlockSpec((HG_T, HG_DK), lambda h, t: (t, h)),
                   pl.BlockSpec((1, 1, HG_DK, HG_DK), lambda h, t: (h, t, 0, 0))],
        out_shape=[jax.ShapeDtypeStruct((s, D), F32), jax.ShapeDtypeStruct((HG_H, nt, HG_DK, HG_DK), F32)],
        scratch_shapes=[pltpu.VMEM((HG_DK, HG_DK), F32)],
        compiler_params=_cparams(("arbitrary", "arbitrary")),
    )(proj, proj, proj, lb_logits)


def _hg_bwd(proj, lb_logits, states, do_raw):
    s = proj.shape[0]
    nt = s // HG_T
    nc = HG_T // HG_C

    def body(q_ref, f_ref, i_ref, l_ref, st_ref, do_ref, dq_ref, df_ref, di_ref, dl_ref, st_all, adj):
        tb = pl.program_id(1)

        @pl.when(tb == 0)
        def _():
            adj[...] = jnp.zeros_like(adj)
            dl_ref[...] = jnp.zeros_like(dl_ref)

        lb = _lb_of(l_ref[...])
        ones = jnp.ones((HG_DK, HG_DK), BF16)
        t_iota = lax.broadcasted_iota(jnp.int32, (HG_C, HG_DK), 0)
        cc = HG_C * HG_C

        def fwd_group(gi, st):
            terms = []
            for u in range(HG_UNROLL):
                ci = gi * HG_UNROLL + u
                r = pl.ds(pl.multiple_of(ci * HG_C, HG_C), HG_C)
                _, _, _, _, kk, g = _hg_gates(q_ref[r, :], f_ref[r, :], lb)
                b = _chunk_scan(g, t_iota, False)
                b_end = b[HG_C - 1:HG_C]
                terms.append((ci, jnp.exp(b_end), _dot(i_ref[r, :], kk * jnp.exp(b_end - b), TN)))
            for ci, dec, kv in terms:
                st_all[ci] = st
                st = st * dec + kv
            return st

        lax.fori_loop(0, nc // HG_UNROLL, fwd_group, st_ref[0, 0])

        def bwd_group(gj, dlb):
            units = []
            for u in range(HG_UNROLL_BWD):
                ci = nc - 1 - (gj * HG_UNROLL_BWD + u)
                r = pl.ds(pl.multiple_of(ci * HG_C, HG_C), HG_C)
                hq, hf, v, do = q_ref[r, :], f_ref[r, :], i_ref[r, :], do_ref[r, :]
                q, sq, sg, f, kk, g = _hg_gates(hq, hf, lb)
                b = _chunk_scan(g, t_iota, False)
                b_end = b[HG_C - 1:HG_C]
                e_b, e_be, dec = jnp.exp(b), jnp.exp(b_end - b), jnp.exp(b_end)
                w_all, ps = _hg_intra_terms(q, kk, b, t_iota)
                x_all = jnp.concatenate([do * v[si:si + 1] for si in range(HG_C)], axis=0)
                units.append(dict(ci=ci, r=r, hq=hq, v=v, do=do, q=q, sq=sq, sg=sg, f=f, kk=kk, e_b=e_b, e_be=e_be,
                                  dec=dec, kd=kk * e_be, w=w_all, ps=ps, x=x_all))
            both = _dot(jnp.concatenate([un["w"] for un in units] + [un["x"] for un in units], axis=0), ones, NN)
            st0s = [st_all[un["ci"]] for un in units]
            st_ends = [st0 * un["dec"] + _dot(un["v"], un["kd"], TN) for un, st0 in zip(units, st0s)]
            dqks = [_dot(un["do"], un["q"] * un["e_b"], TN) for un in units]
            es = [adj[...]]
            for un, dqk in zip(units, dqks):
                es.append(es[-1] * un["dec"] + dqk)
            adj[...] = es[-1]
            for ui, un in enumerate(units):
                e, q, kk, v, do = es[ui], un["q"], un["kk"], un["v"], un["do"]
                tail = jnp.sum(e * st_ends[ui], axis=0, keepdims=True)
                dq = un["e_b"] * _dot(do, st0s[ui], NN)
                dk = un["e_be"] * _dot(v, e, NN)
                dv = _dot(un["kd"], e, NT)
                a0 = ui * cc
                d0 = (HG_UNROLL_BWD + ui) * cc
                for si in range(HG_C):
                    da = both[d0 + si * HG_C:d0 + (si + 1) * HG_C]
                    aa = both[a0 + si * HG_C:a0 + (si + 1) * HG_C]
                    dap = da * un["ps"][si]
                    dq = dq + dap * kk[si:si + 1]
                    hit = t_iota == si
                    dk = dk + jnp.where(hit, jnp.sum(dap * q, axis=0, keepdims=True), 0.0)
                    dv = dv + jnp.where(hit, jnp.sum(aa * do, axis=0, keepdims=True), 0.0)
                dg = _chunk_scan(q * dq - kk * dk, t_iota, True) + tail
                dfg = dg / un["f"] - dk
                sq, sg, hq, r = un["sq"], un["sg"], un["hq"], un["r"]
                dq_ref[r, :] = (dq * sq * (1.0 + hq * (1.0 - sq))).astype(dq_ref.dtype)
                df_ref[r, :] = (dfg * (1.0 - lb) * sg * (1.0 - sg)).astype(df_ref.dtype)
                di_ref[r, :] = dv.astype(di_ref.dtype)
                dlb = dlb + jnp.sum(dfg * (1.0 - sg), axis=0, keepdims=True)
            return dlb

        dlb = lax.fori_loop(0, nc // HG_UNROLL_BWD, bwd_group, jnp.zeros((1, HG_DK), F32))
        dl0 = dlb * lb * (1.0 - lb)
        dl_ref[...] += jnp.concatenate([dl0, -dl0], axis=0)

    col = lambda off: pl.BlockSpec((HG_T, HG_DK), lambda h, t: (nt - 1 - t, off + h))
    out_col = pl.BlockSpec((HG_T, HG_DK), lambda h, t: (nt - 1 - t, h))
    return pl.pallas_call(
        body, name="hgrn2_bwd", grid=(HG_H, nt),
        in_specs=[col(0), col(8), col(16), pl.BlockSpec((2, HG_DK), lambda h, t: (0, h)),
                  pl.BlockSpec((1, 1, HG_DK, HG_DK), lambda h, t: (h, nt - 1 - t, 0, 0)), col(0)],
        out_specs=[out_col, out_col, out_col, pl.BlockSpec((2, HG_DK), lambda h, t: (0, h))],
        out_shape=[jax.ShapeDtypeStruct((s, D), BF16)] * 3 + [jax.ShapeDtypeStruct((2, D), F32)],
        scratch_shapes=[pltpu.VMEM((nc, HG_DK, HG_DK), F32), pltpu.VMEM((HG_DK, HG_DK), F32)],
        compiler_params=_cparams(("arbitrary", "arbitrary")),
    )(proj, proj, proj, lb_logits, states, do_raw)


def _hg_post_fwd(o_raw, proj, gnorm, tm=256):
    s = o_raw.shape[0]

    def fn(o, hg, gn):
        outs = []
        for h in range(HG_H):
            sl = slice(h * HG_DK, (h + 1) * HG_DK)
            oh, gh = o[:, sl], hg[:, sl]
            r = lax.rsqrt(jnp.mean(oh * oh, axis=-1, keepdims=True) + EPS)
            outs.append(oh * r * gn * (gh * _sigmoid(gh)))
        return (jnp.concatenate(outs, axis=1),), ()

    return _rows("hgrn2_out_fwd", fn, s, tm, [_rb(o_raw, tm, D), _rb(proj, tm, D, 3), _cst(gnorm)], [(D, BF16)])[0]


def _hg_post_bwd(do_a, o_raw, proj, gnorm, tm=256):
    s = o_raw.shape[0]

    def fn(da, o, hg, gn):
        dos, dhgs = [], []
        dgn = jnp.zeros((1, HG_DK), F32)
        for h in range(HG_H):
            sl = slice(h * HG_DK, (h + 1) * HG_DK)
            oh, gh, dh = o[:, sl], hg[:, sl], da[:, sl]
            r = lax.rsqrt(jnp.mean(oh * oh, axis=-1, keepdims=True) + EPS)
            xhat = oh * r
            sg = _sigmoid(gh)
            dy = dh * (gh * sg)
            dhgs.append(dh * xhat * gn * sg * (1.0 + gh * (1.0 - sg)))
            dgn = dgn + jnp.sum(dy * xhat, axis=0, keepdims=True)
            dxh = dy * gn
            dos.append(r * (dxh - xhat * jnp.mean(dxh * xhat, axis=-1, keepdims=True)))
        return (jnp.concatenate(dos, axis=1), jnp.concatenate(dhgs, axis=1)), (dgn,)

    ins = [_rb(do_a, tm, D), _rb(o_raw, tm, D), _rb(proj, tm, D, 3), _cst(gnorm)]
    return _rows("hgrn2_out_bwd", fn, s, tm, ins, [(D, F32), (D, BF16)], [(1, HG_DK)])


def _log_sigmoid(z):
    return jnp.minimum(z, 0.0) - jnp.log(1.0 + jnp.exp(-jnp.abs(z)))


def _fox_gate_bwd(dct, pff, bias, tm=256):
    s = pff.shape[0]
    nb = s // tm

    def body(d_ref, p_ref, b_ref, dff_ref, db_ref, carry):
        @pl.when(pl.program_id(0) == 0)
        def _():
            carry[...] = jnp.zeros_like(carry)
            db_ref[...] = jnp.zeros_like(db_ref)

        dc = d_ref[...].T
        dlf = _split_dot(_tri(tm, False), dc, 3) + carry[0:1]
        carry[...] = jnp.broadcast_to(dlf[0:1], carry.shape)
        dff = dlf * _sigmoid(-(p_ref[...] + b_ref[...]))
        dff_ref[...] = dff
        db_ref[...] += jnp.sum(dff, axis=0, keepdims=True)

    return pl.pallas_call(
        body, name="fox_gate_bwd", grid=(nb,),
        in_specs=[pl.BlockSpec((LANES, tm), lambda i: (0, nb - 1 - i)),
                  pl.BlockSpec((tm, LANES), lambda i: (nb - 1 - i, 0)), pl.BlockSpec((1, LANES), lambda i: (0, 0))],
        out_specs=[pl.BlockSpec((tm, LANES), lambda i: (nb - 1 - i, 0)), pl.BlockSpec((1, LANES), lambda i: (0, 0))],
        out_shape=[jax.ShapeDtypeStruct((s, LANES), F32), jax.ShapeDtypeStruct((1, LANES), F32)],
        scratch_shapes=[pltpu.VMEM((8, LANES), F32)],
        compiler_params=_cparams(("arbitrary",)),
    )(dct, pff, bias)


def _diag_mask(t):
    r = lax.broadcasted_iota(jnp.int32, (t, t), 0)
    c = lax.broadcasted_iota(jnp.int32, (t, t), 1)
    return r >= c


AUX_ONES = 6


def _pieces(x):
    h = x.astype(BF16)
    r = x - h.astype(F32)
    m = r.astype(BF16)
    return h, m, (r - m.astype(F32)).astype(BF16)


def _lane_put(lane, cols, base):
    out = None
    for i, col in enumerate(cols):
        term = jnp.where(lane == base + i, col.astype(F32), 0.0)
        out = term if out is None else out + term
    return out


def _fox_prep2(proj, pff, bias, tm=256):
    s = pff.shape[0]

    def body(q_ref, k_ref, v_ref, p_ref, b_ref, qb_ref, kb_ref, vb_ref, ka_ref, carry):
        @pl.when(pl.program_id(0) == 0)
        def _():
            carry[...] = jnp.zeros_like(carry)

        qb_ref[...] = (q_ref[...] * 0.125).astype(BF16)
        kb_ref[...] = k_ref[...].astype(BF16)
        vb_ref[...] = v_ref[...].astype(BF16)
        lf = _log_sigmoid(p_ref[...] + b_ref[...])
        c = _split_dot(_tri(tm, True), lf, 3) + carry[0:1]
        carry[...] = jnp.broadcast_to(c[tm - 1:tm], carry.shape)
        lane = lax.broadcasted_iota(jnp.int32, (tm, LANES), 1)
        ones = jnp.where((lane >= AUX_ONES) & (lane < AUX_ONES + 6), 1.0, 0.0)
        for p in range(FOX_H // 2):
            aux = ones
            for z in range(2):
                col = jnp.sum(jnp.where(lane == 2 * p + z, c, 0.0), axis=1, keepdims=True)
                aux = aux + _lane_put(lane, _pieces(-col), 3 * z)
            ka_ref[:, p * LANES:(p + 1) * LANES] = aux.astype(BF16)

    row = lambda cb: pl.BlockSpec((tm, D), lambda i: (i, cb))
    return pl.pallas_call(
        body, name="fox_prep", grid=(s // tm,),
        in_specs=[row(4), row(5), row(6), pl.BlockSpec((tm, LANES), lambda i: (i, 0)),
                  pl.BlockSpec((1, LANES), lambda i: (0, 0))],
        out_specs=[row(0)] * 4, out_shape=[jax.ShapeDtypeStruct((s, D), BF16)] * 4,
        scratch_shapes=[pltpu.VMEM((8, LANES), F32)],
        compiler_params=_cparams(("arbitrary",)),
    )(proj, proj, proj, pff, bias)


def _fox_fwd2(qb, kb, vb, ka):
    s = qb.shape[0]
    t = min(FOX_T, s)
    nq = s // t

    def body(q_ref, k_ref, v_ref, ka_ref, o_ref, la_ref):
        i = pl.program_id(1)
        lane = lax.broadcasted_iota(jnp.int32, (t, LANES), 1)
        in_a = lane < FOX_D
        q = q_ref[...]
        zero = jnp.zeros_like(q)
        qh = [jnp.where(in_a, q, zero), jnp.where(in_a, zero, q)]
        c_ones = [jnp.where((lane >= 3 * z) & (lane < 3 * z + 3), 1.0, 0.0) for z in range(2)]

        def keys(j):
            rows = pl.ds(pl.multiple_of(j * t, t), t)
            return jnp.concatenate([k_ref[rows, :], ka_ref[rows, :]], axis=1), rows

        dmask = _diag_mask(t)

        def logits(qx, kk, masked):
            e = lax.dot_general(qx, kk, (NT, ((), ())), preferred_element_type=F32)
            return jnp.where(dmask, e, -1e30) if masked else e

        qc = [jnp.concatenate([qh[z], c_ones[z].astype(BF16)], axis=1) for z in range(2)]

        def step(j, carry, masked):
            kk, rows = keys(j)
            vj = v_ref[rows, :]
            scores = [logits(qc[z], kk, masked) for z in range(2)]
            one = jnp.ones_like(vj)
            vh = [jnp.where(in_a, vj, one), jnp.where(in_a, one, vj)]
            out = []
            for z in range(2):
                m, acc = carry[z]
                m_new = jnp.maximum(m, jnp.max(scores[z], axis=1, keepdims=True))
                p = jnp.exp(scores[z] - m_new)
                out.append((m_new, jnp.exp(m - m_new) * acc + _dot(p, vh[z], NN)))
            return tuple(out)

        init = tuple((jnp.full((t, 1), -1e30, F32), jnp.zeros((t, LANES), F32)) for _ in range(2))
        (ma, acc_a), (mb, acc_b) = step(i, lax.fori_loop(0, i, lambda j, c: step(j, c, False), init), True)
        la = jnp.sum(jnp.where(lane == FOX_D, acc_a, 0.0), axis=1, keepdims=True)
        lb = jnp.sum(jnp.where(lane == 0, acc_b, 0.0), axis=1, keepdims=True)
        o_ref[...] = jnp.where(in_a, acc_a / la, acc_b / lb).astype(o_ref.dtype)
        la_ref[...] = (_lane_put(lane, _pieces(-(ma + jnp.log(la))), AUX_ONES)
                       + _lane_put(lane, _pieces(-(mb + jnp.log(lb))), AUX_ONES + 3)).astype(la_ref.dtype)

    blk = pl.BlockSpec((t, LANES), lambda p, i: (i, p))
    whole = pl.BlockSpec((s, LANES), lambda p, i: (0, p))
    return pl.pallas_call(
        body, name="fox_attn_fwd", grid=(FOX_H // 2, nq), in_specs=[blk, whole, whole, whole],
        out_specs=[blk, blk], out_shape=[jax.ShapeDtypeStruct((s, D), BF16)] * 2,
        compiler_params=_cparams(("arbitrary", "arbitrary")),
    )(qb, kb, vb, ka)


def _fox_bwd2(qb, kb, vb, ka, ob, laux, dob):
    s = qb.shape[0]
    t = min(FOX_T, s)
    nq = s // t

    def body(q_ref, k_ref, v_ref, ka_ref, o_ref, la_ref, do_ref, dq_ref, dk_ref, dv_ref, dc_ref, dkt, dvt):
        i = pl.program_id(1)

        @pl.when(i == 0)
        def _():
            dkt[...] = jnp.zeros_like(dkt)
            dvt[...] = jnp.zeros_like(dvt)
            dc_ref[...] = jnp.zeros_like(dc_ref)

        lane = lax.broadcasted_iota(jnp.int32, (t, LANES), 1)
        in_a = lane < FOX_D
        q, do, la = q_ref[...], do_ref[...], la_ref[...].astype(F32)
        zero = jnp.zeros_like(q)
        qh = [jnp.where(in_a, q, zero), jnp.where(in_a, zero, q)]
        doh = [jnp.where(in_a, do, zero), jnp.where(in_a, zero, do)]
        qt = [h.astype(F32).T.astype(BF16) for h in qh]
        dot_ = [h.astype(F32).T.astype(BF16) for h in doh]
        prod = do.astype(F32) * o_ref[...].astype(F32)
        qx, dox = [], []
        for z in range(2):
            delta = jnp.sum(jnp.where(in_a if z == 0 else ~in_a, prod, 0.0), axis=1, keepdims=True)
            c_ones = jnp.where((lane >= 3 * z) & (lane < 3 * z + 3), 1.0, 0.0)
            lse_lanes = (lane >= AUX_ONES + 3 * z) & (lane < AUX_ONES + 3 * z + 3)
            qx.append(jnp.concatenate([qh[z], (c_ones + jnp.where(lse_lanes, la, 0.0)).astype(BF16)], axis=1))
            dox.append(jnp.concatenate([doh[z], _lane_put(lane, _pieces(-delta), 3 * z).astype(BF16)], axis=1))
        v_ones = jnp.where(lane < 6, 1.0, 0.0).astype(BF16)
        dmask = _diag_mask(t)

        def step(j, carry, masked):
            rows = pl.ds(pl.multiple_of(j * t, t), t)
            kj, vj = k_ref[rows, :], v_ref[rows, :]
            kk = jnp.concatenate([kj, ka_ref[rows, :]], axis=1)
            vv = jnp.concatenate([vj, v_ones], axis=1)
            out = []
            dk_add, dv_add = None, None
            for z in range(2):
                dq, rsum = carry[z]
                e = lax.dot_general(qx[z], kk, (NT, ((), ())), preferred_element_type=F32)
                if masked:
                    e = jnp.where(dmask, e, -1e30)
                p = jnp.exp(e)
                ds = p * lax.dot_general(dox[z], vv, (NT, ((), ())), preferred_element_type=F32)
                dkz, dvz = _dot(qt[z], ds, NN), _dot(dot_[z], p, NN)
                dk_add = dkz if dk_add is None else dk_add + dkz
                dv_add = dvz if dv_add is None else dv_add + dvz
                dc_ref[0, z, j] += -jnp.sum(ds, axis=0, keepdims=True)
                out.append((dq + _dot(ds, kj, NN), rsum + jnp.sum(ds, axis=1, keepdims=True)))
            dkt[j] += dk_add
            dvt[j] += dv_add
            return tuple(out)

        init = tuple((jnp.zeros((t, LANES), F32), jnp.zeros((t, 1), F32)) for _ in range(2))
        (dq_a, rs_a), (dq_b, rs_b) = step(i, lax.fori_loop(0, i, lambda j, c: step(j, c, False), init), True)
        for z, rs in enumerate((rs_a, rs_b)):
            dc_ref[0, z, i] += jnp.transpose(jnp.broadcast_to(rs, (t, LANES)))[0:1]
        dq_ref[...] = (jnp.where(in_a, dq_a, dq_b) * 0.125).astype(dq_ref.dtype)

        @pl.when(i == nq - 1)
        def _():
            for jb in range(nq):
                dk_ref[jb * t:(jb + 1) * t, :] = dkt[jb].T.astype(dk_ref.dtype)
                dv_ref[jb * t:(jb + 1) * t, :] = dvt[jb].T.astype(dv_ref.dtype)

    blk = pl.BlockSpec((t, LANES), lambda p, i: (i, p))
    whole = pl.BlockSpec((s, LANES), lambda p, i: (0, p))
    return pl.pallas_call(
        body, name="fox_attn_bwd", grid=(FOX_H // 2, nq),
        in_specs=[blk, whole, whole, whole, blk, blk, blk],
        out_specs=[blk, whole, whole, pl.BlockSpec((1, 2, nq, 1, t), lambda p, i: (p, 0, 0, 0, 0))],
        out_shape=[jax.ShapeDtypeStruct((s, D), BF16)] * 3 + [jax.ShapeDtypeStruct((FOX_H // 2, 2, nq, 1, t), F32)],
        scratch_shapes=[pltpu.VMEM((nq, LANES, t), F32), pltpu.VMEM((nq, LANES, t), F32)],
        compiler_params=_cparams(("arbitrary", "arbitrary")),
    )(qb, kb, vb, ka, ob, laux, dob)


def _adamw(name, w, g, m, v, tm=None):
    rows, width = w.shape
    tm = rows if tm is None else tm
    c1 = 1.0 - ADAM_B1 ** ADAM_STEP
    c2 = 1.0 - ADAM_B2 ** ADAM_STEP

    def fn(wb, gb, mb, vb):
        m_new = ADAM_B1 * mb + (1.0 - ADAM_B1) * gb
        v_new = ADAM_B2 * vb + (1.0 - ADAM_B2) * (gb * gb)
        delta = -ADAM_LR * ((m_new / c1) / (jnp.sqrt(v_new / c2) + ADAM_EPS) + ADAM_WD * wb)
        return (delta, m_new, v_new), ()

    ins = [_rb(a, tm, width) for a in (w, g, m, v)]
    return _rows(name, fn, rows, tm, ins, [(width, F32)] * 3)


def _me():
    return lax.axis_index("x"), lax.axis_index("y"), lax.axis_index("c")


def _all_reduce_small(name, block):
    r, n = block.shape

    def body(x_ref, sum_ref, gath, send_sems, recv_sems):
        x, y, c = _me()
        me = 4 * x + 2 * y + c
        gath[me] = x_ref[...]
        sends = []
        for k in range(1, 8):
            px = x ^ ((k >> 2) & 1)
            py = y ^ ((k >> 1) & 1)
            pc = c ^ (k & 1)
            sends.append(pltpu.make_async_remote_copy(
                src_ref=x_ref, dst_ref=gath.at[me], send_sem=send_sems.at[k - 1], recv_sem=recv_sems.at[k - 1],
                device_id=(px, py, pc), device_id_type=MESH))
        for cp in sends:
            cp.start()
        for k in range(1, 8):
            peer = me ^ k
            pltpu.make_async_remote_copy(
                src_ref=x_ref, dst_ref=gath.at[peer], send_sem=send_sems.at[k - 1], recv_sem=recv_sems.at[k - 1],
                device_id=(x, y, c), device_id_type=MESH).wait_recv()
        for cp in sends:
            cp.wait_send()
        acc = gath[0]
        for d in range(1, 8):
            acc = acc + gath[d]
        sum_ref[...] = acc

    vm = pl.BlockSpec(memory_space=pltpu.VMEM)
    return pl.pallas_call(
        body, name=name, in_specs=[vm], out_specs=vm, out_shape=jax.ShapeDtypeStruct((r, n), F32),
        scratch_shapes=[pltpu.VMEM((8, r, n), F32), pltpu.SemaphoreType.DMA((7,)), pltpu.SemaphoreType.DMA((7,))],
    )(block)


WD_EXT_ROWS = 736


def _remote(src, dst, send_sems, recv_sems, k, to):
    return pltpu.make_async_remote_copy(src_ref=src, dst_ref=dst, send_sem=send_sems.at[k], recv_sem=recv_sems.at[k],
                                        device_id=to, device_id_type=MESH)


def _all_gather8_multi(name, blocks):
    nt = len(blocks)

    def body(*refs):
        x_refs, out_refs, send_sems, recv_sems = refs[:nt], refs[nt:2 * nt], refs[-2], refs[-1]
        x, y, c = _me()
        me, sibling = (x, y, c), (x, y, 1 - c)
        chips = [(1 - x, y), (x, 1 - y), (1 - x, 1 - y)]
        slot = lambda q, p: out_refs[q].at[4 * p[0] + 2 * p[1] + p[2]]

        def copies(k, blk, to, from_input=False):
            return [_remote(x_refs[q] if from_input else slot(q, blk), slot(q, blk), send_sems, recv_sems, k * nt + q, to)
                    for q in range(nt)]

        first = copies(0, me, sibling, True)
        for j, chip in enumerate(chips):
            first += copies(1 + j, me, (*chip, c), True)
        for cp in first:
            cp.start()
        passed = []
        for j, chip in enumerate(chips):
            for cp in copies(1 + j, (*chip, c), me):
                cp.wait_recv()
            fwd = copies(4 + j, (*chip, c), sibling)
            for cp in fwd:
                cp.start()
            passed += fwd
        for cp in copies(0, sibling, me):
            cp.wait_recv()
        back = copies(7, sibling, sibling)
        for cp in back:
            cp.start()
        for j, chip in enumerate(chips):
            for cp in copies(4 + j, (*chip, 1 - c), me):
                cp.wait_recv()
        for cp in copies(7, me, me):
            cp.wait_recv()
        for cp in first + passed + back:
            cp.wait_send()

    return pl.pallas_call(
        body, name=name, in_specs=[ANY] * nt, out_specs=[ANY] * nt,
        out_shape=[jax.ShapeDtypeStruct((8,) + b.shape, b.dtype) for b in blocks],
        scratch_shapes=[pltpu.SemaphoreType.DMA((8 * nt,)), pltpu.SemaphoreType.DMA((8 * nt,))],
    )(*blocks)


def _swap_halves_multi(name, gs):
    nt = len(gs)
    n_chip = gs[0].shape[0]

    def body(*refs):
        g_refs, got_refs, send_sems, recv_sems = refs[:nt], refs[nt:2 * nt], refs[-2], refs[-1]
        x, y, c = _me()
        cps = [_remote(g_refs[q].at[j, 1 - c], got_refs[q].at[j], send_sems, recv_sems, q * n_chip + j, (x, y, 1 - c))
               for q in range(nt) for j in range(n_chip)]
        for cp in cps:
            cp.start()
        for cp in cps:
            cp.wait()

    return pl.pallas_call(
        body, name=name, in_specs=[ANY] * nt, out_specs=[ANY] * nt,
        out_shape=[jax.ShapeDtypeStruct((g.shape[0],) + g.shape[2:], g.dtype) for g in gs],
        scratch_shapes=[pltpu.SemaphoreType.DMA((nt * n_chip,)), pltpu.SemaphoreType.DMA((nt * n_chip,))],
    )(*gs)


def _swap_sibling_multi(name, xs):
    nt = len(xs)

    def body(*refs):
        x_refs, out_refs, send_sems, recv_sems = refs[:nt], refs[nt:2 * nt], refs[-2], refs[-1]
        x, y, c = _me()
        cps = [_remote(x_refs[q], out_refs[q], send_sems, recv_sems, q, (x, y, 1 - c)) for q in range(nt)]
        for cp in cps:
            cp.start()
        for cp in cps:
            cp.wait()

    return pl.pallas_call(
        body, name=name, in_specs=[ANY] * nt, out_specs=[ANY] * nt,
        out_shape=[jax.ShapeDtypeStruct(a.shape, a.dtype) for a in xs],
        scratch_shapes=[pltpu.SemaphoreType.DMA((nt,)), pltpu.SemaphoreType.DMA((nt,))],
    )(*xs)


def _chip_exchange_multi(name, ps):
    nt = len(ps)

    def body(*refs):
        p_refs, out_refs, bounce_refs = refs[:nt], refs[nt:2 * nt], refs[2 * nt:3 * nt]
        send_sems, recv_sems = refs[-2], refs[-1]
        x, y, c = _me()
        my_chip = 2 * x + y
        sibling = (x, y, 1 - c)
        chips = [(1 - x, y), (x, 1 - y), (1 - x, 1 - y)]
        cp = lambda k, q, src, dst, to: _remote(src, dst, send_sems, recv_sems, k * nt + q, to)
        sends = [cp(k, q, p_refs[q].at[2 * px + py], out_refs[q].at[my_chip], (px, py, c))
                 for k, (px, py) in enumerate(chips) for q in range(nt)]
        sends += [cp(3, q, p_refs[q].at[my_chip], bounce_refs[q], sibling) for q in range(nt)]
        for s_ in sends:
            s_.start()
        backs = []
        for q in range(nt):
            cp(3, q, p_refs[q].at[my_chip], bounce_refs[q], sibling).wait_recv()
            backs.append(cp(4, q, bounce_refs[q], out_refs[q].at[my_chip], sibling))
            backs[-1].start()
        for k, (px, py) in enumerate(chips):
            for q in range(nt):
                cp(k, q, p_refs[q].at[my_chip], out_refs[q].at[2 * px + py], (px, py, c)).wait_recv()
        for q in range(nt):
            cp(4, q, bounce_refs[q], out_refs[q].at[my_chip], sibling).wait_recv()
        for s_ in sends + backs:
            s_.wait_send()

    outs = pl.pallas_call(
        body, name=name, in_specs=[ANY] * nt, out_specs=[ANY] * (2 * nt),
        out_shape=[jax.ShapeDtypeStruct(p.shape, p.dtype) for p in ps]
        + [jax.ShapeDtypeStruct(p.shape[1:], p.dtype) for p in ps],
        scratch_shapes=[pltpu.SemaphoreType.DMA((5 * nt,)), pltpu.SemaphoreType.DMA((5 * nt,))],
    )(*ps)
    return outs[:nt]


def _row_tile(m):
    return m if m <= 384 else 128


def _add2_rows(name, a, b):
    n4, m, n = a.shape
    tm = _row_tile(m)
    out = _rows(name, lambda p, q: ((p + q,), ()), n4 * m, tm,
                [_rb(a.reshape(n4 * m, n), tm, n), _rb(b.reshape(n4 * m, n), tm, n)], [(n, BF16)])[0]
    return out.reshape(n4, m, n)


def _add4_rows(name, p):
    _, m, n = p.shape
    tm = _row_tile(m)
    nb = m // tm
    flat = p.reshape(4 * m, n)
    ins = [(flat, (tm, n), (lambda i, j=j: (j * nb + i, 0))) for j in range(4)]
    f32 = lambda v: v.astype(F32)
    return _rows(name, lambda a, b, c, d: ((((f32(a) + f32(b)) + f32(c)) + f32(d),), ()), m, tm, ins, [(n, F32)])[0]


def _in_proj_with_gather(n1, w_main, blocks):
    m, k = n1.shape
    n = w_main.shape[1]
    tm, tn = min(1024, m), 1024
    gi, gj = m // tm, n // tn
    last, mid = gi * gj - 1, (gi * gj) // 2
    nt = len(blocks)

    def body(*refs):
        a_ref, b_ref, x_refs, o_ref, out_refs = refs[0], refs[1], refs[2:2 + nt], refs[2 + nt], refs[3 + nt:3 + 2 * nt]
        send_sems, recv_sems = refs[-2], refs[-1]
        step = pl.program_id(0) * gj + pl.program_id(1)
        x, y, c = _me()
        me, sibling = (x, y, c), (x, y, 1 - c)
        chips = [(1 - x, y), (x, 1 - y), (1 - x, 1 - y)]
        slot = lambda q, p: out_refs[q].at[4 * p[0] + 2 * p[1] + p[2]]

        def copies(kk, blk, to, from_input=False):
            return [_remote(x_refs[q] if from_input else slot(q, blk), slot(q, blk), send_sems, recv_sems, kk * nt + q, to)
                    for q in range(nt)]

        def first():
            out = copies(0, me, sibling, True)
            for j, chip in enumerate(chips):
                out += copies(1 + j, me, (*chip, c), True)
            return out

        @pl.when(step == 0)
        def _():
            for cp in first():
                cp.start()

        @pl.when(step == mid)
        def _():
            for j, chip in enumerate(chips):
                for cp in copies(1 + j, (*chip, c), me):
                    cp.wait_recv()
                for cp in copies(4 + j, (*chip, c), sibling):
                    cp.start()
            for cp in copies(0, sibling, me):
                cp.wait_recv()
            for cp in copies(7, sibling, sibling):
                cp.start()

        o_ref[...] = _dot(a_ref[...], b_ref[...], NN)

        @pl.when(step == last)
        def _():
            for j, chip in enumerate(chips):
                for cp in copies(4 + j, (*chip, 1 - c), me):
                    cp.wait_recv()
            for cp in copies(7, me, me):
                cp.wait_recv()
            sent = first() + copies(7, sibling, sibling)
            for j, chip in enumerate(chips):
                sent += copies(4 + j, (*chip, c), sibling)
            for cp in sent:
                cp.wait_send()

    outs = pl.pallas_call(
        body, name="in_proj", grid=(gi, gj),
        in_specs=[pl.BlockSpec((tm, k), lambda i, j: (i, 0)), pl.BlockSpec((k, tn), lambda i, j: (0, j))] + [ANY] * nt,
        out_specs=[pl.BlockSpec((tm, tn), lambda i, j: (i, j))] + [ANY] * nt,
        out_shape=[jax.ShapeDtypeStruct((m, n), F32)] + [jax.ShapeDtypeStruct((8,) + b.shape, b.dtype) for b in blocks],
        scratch_shapes=[pltpu.SemaphoreType.DMA((8 * nt,)), pltpu.SemaphoreType.DMA((8 * nt,))],
        compiler_params=_cparams(("arbitrary", "arbitrary")),
    )(n1, w_main, *blocks)
    return outs[0], outs[1:]


def _weight_halves(w_in, w_a, w_b, w_out, w_up, w_down, conv_w):
    c = lax.axis_index("c")
    bits = lax.bitcast_convert_type(conv_w, BF16).reshape(-1)
    extra = jnp.zeros(((WD_EXT_ROWS - W_DOWN_SHARD) * D,), BF16).at[:bits.shape[0]].set(bits)
    wd_ext = jnp.concatenate([w_down.astype(BF16), extra.reshape(-1, D)], axis=0)
    shards = [w_in.astype(BF16), w_a.astype(BF16), w_b.astype(BF16), w_out.astype(BF16), w_up.astype(BF16), wd_ext]
    return [lax.dynamic_slice_in_dim(t, c * (t.shape[0] // 2), t.shape[0] // 2, axis=0) for t in shards]


def _unpack_w_in(gathered):
    wi = gathered.reshape(N_CHIP, D, W_IN_SHARD).transpose(1, 0, 2).reshape(D, N_CHIP * W_IN_SHARD)
    w_main = jnp.concatenate([wi[:, :FF_COL], wi[:, FF_COL + FOX_H:]], axis=1)
    return w_main, jnp.pad(wi[:, FF_COL:FF_COL + FOX_H], ((0, 0), (0, LANES - FOX_H)))


def _unpack_later_weights(gathered):
    full = [g.reshape((N_CHIP, 2 * g.shape[1]) + g.shape[2:]) for g in gathered]
    wa, wb, wo = (full[i].reshape(D, D) for i in (0, 1, 2))
    wu = full[3].transpose(1, 0, 2).reshape(D, 2 * D_FF)
    wd = full[4][:, :W_DOWN_SHARD].reshape(D_FF, D)
    n_bits = 3 * W_UP_SHARD * 2
    cw_bits = full[4][:, W_DOWN_SHARD:].reshape(N_CHIP, -1)[:, :n_bits].reshape(N_CHIP, 3, W_UP_SHARD, 2)
    cw = lax.bitcast_convert_type(cw_bits, F32).transpose(1, 0, 2).reshape(3, 2 * D_FF)
    return wa, wb, wo, wu, wd, cw


def _chip_sums(tag, per_chip):
    c = lax.axis_index("c")
    gs = [t.reshape(N_CHIP, 2, t.shape[1] // 2, t.shape[2]) for t in per_chip]
    got = _swap_halves_multi("grad_swap_halves_" + tag, gs)
    return [_add2_rows("grad_chip_sum_%s%d" % (tag, q), lax.dynamic_index_in_dim(g, c, axis=1, keepdims=False), s_)
            for q, (g, s_) in enumerate(zip(gs, got))]


def _late_weight_chip_sums(d_a, d_b, d_o, d_u, d_d):
    return _chip_sums("late", [d_a.reshape(N_CHIP, -1, D), d_b.reshape(N_CHIP, -1, D), d_o.reshape(N_CHIP, -1, D),
                               d_u.reshape(D, N_CHIP, W_UP_SHARD).transpose(1, 0, 2), d_d.reshape(N_CHIP, -1, D)])


def _finish_grads(d_main, d_ff, late_pieces):
    c = lax.axis_index("c")
    d_in = jnp.concatenate(d_main[:7] + [d_ff[:, :FOX_H]] + d_main[7:], axis=1)
    sums = _chip_sums("w_in", [d_in.reshape(D, N_CHIP, W_IN_SHARD).transpose(1, 0, 2)])
    pieces = list(_chip_exchange_multi("grad_chip_exchange", sums)) + list(late_pieces)
    mine = [_add4_rows("grad_sum_chips_%d" % q, p) for q, p in enumerate(pieces)]
    other = _swap_sibling_multi("grad_share_half", mine)
    return [jnp.concatenate([jnp.where(c == 0, a, b), jnp.where(c == 0, b, a)], axis=0) for a, b in zip(mine, other)]


def _local_step(x, target, norm_mix, fox_f_bias, hg_lb_logits, hg_norm, norm_ffn, conv_b, norm_final,
                w_main, w_ff, later):
    bias = jnp.pad(fox_f_bias, ((0, 0), (0, LANES - FOX_H)))

    n1, n1t = _rms_fwd("norm_mix_fwd", x, norm_mix)
    if len(later) == 5:
        proj, gathered = _in_proj_with_gather(n1, w_main, later)
        wa, wb, wo, wu, wd, conv_w = _unpack_later_weights(gathered)
    else:
        proj = _mm("in_proj", n1, w_main, "nn", F32, 1024, 1024, D)
        wa, wb, wo, wu, wd, conv_w = later
    conv_w8 = jnp.pad(conv_w, ((0, 5), (0, 0)))
    pff = _mm("in_proj_forget", n1, w_ff, "nn", F32, 1024, LANES, D)
    qb, kb, vb, ka = _fox_prep2(proj, pff, bias)
    o_b, laux = _fox_fwd2(qb, kb, vb, ka)
    o_raw, states = _hg_fwd(proj, hg_lb_logits)
    o_a = _hg_post_fwd(o_raw, proj, hg_norm)
    pa = _mm("branch_a", o_a, wa, "nn", F32, 1024, 1024, D)
    pb = _mm("branch_b", o_b, wb, "nn", F32, 1024, 1024, D)
    merged = _merge_fwd(pa, pb, proj)
    h1 = _mm("out_proj", merged, wo, "nn", F32, 1024, 1024, D, res=x)
    n2, n2t = _rms_fwd("norm_ffn_fwd", h1, norm_ffn)
    u = _mm("ffn_up", n2, wu, "nn", F32, 1024, W_UP_SHARD, D)
    act, gelu_gate, dact_dgate = _convglu_fwd(u, conv_w8, conv_b)
    h2 = _mm("ffn_down", act, wd, "nn", F32, 512, 1024, D_FF, res=h1)
    (dh2,), (d_norm_final, loss_row) = _final(h2, target, norm_final)

    dact = _mm("ffn_down_dx", dh2, wd, "nt", BF16, 1024, D_FF, D)
    d_wd = _mm("ffn_down_dw", act, dh2, "tn", F32, D_FF // 2, 1024, DW_TK // 2)
    (du,), (d_conv_w8, d_conv_b) = _convglu_bwd(u, dact, gelu_gate, dact_dgate, conv_w8)
    dn2 = _mm("ffn_up_dx", du, wu, "nt", F32, 1024, 1024, W_UP_SHARD)
    d_wu = _mm("ffn_up_dw", n2t, du, "nn", F32, 1024, W_UP_SHARD, DW_TK)
    (dh1,), (d_norm_ffn,) = _rms_bwd("norm_ffn_bwd", h1, norm_ffn, [dn2], dh2)

    dmerged = _mm("out_proj_dx", dh1, wo, "nt", F32, 1024, 1024, D)
    d_wo = _mm("out_proj_dw", merged, dh1, "tn", F32, 1024, 1024, DW_TK)
    dpa, dpb, dga, dgb = _merge_bwd(dmerged, pa, pb, proj)
    do_a = _mm("branch_a_dx", dpa, wa, "nt", F32, 1024, 1024, D)
    do_b = _mm("branch_b_dx", dpb, wb, "nt", BF16, 1024, 1024, D)
    d_wa = _mm("branch_a_dw", o_a, dpa, "tn", F32, 1024, 1024, DW_TK)
    d_wb = _mm("branch_b_dw", o_b, dpb, "tn", F32, 1024, 1024, DW_TK)

    (do_raw, dhg), (d_hg_norm,) = _hg_post_bwd(do_a, o_raw, proj, hg_norm)
    dhq, dhf, dhi, d_lb_logits = _hg_bwd(proj, hg_lb_logits, states, do_raw)

    dfq, dfk, dfv, dcrow = _fox_bwd2(qb, kb, vb, ka, o_b, laux, do_b)
    dct = jnp.pad(dcrow.reshape(FOX_H, x.shape[0]), ((0, LANES - FOX_H), (0, 0)))
    dff, d_bias = _fox_gate_bwd(dct, pff, bias)

    pieces = [dhq, dhf, dhi, dhg, dfq, dfk, dfv, dga, dgb]
    if len(later) == 5:
        dn1, late = _mm_sum_nt("in_proj_dx", pieces, w_main, (dff, w_ff), 1024, 1024,
                               exchange=_late_weight_chip_sums(d_wa, d_wb, d_wo, d_wu, d_wd))
    else:
        dn1, late = _mm_sum_nt("in_proj_dx", pieces, w_main, (dff, w_ff), 1024, 1024), (d_wa, d_wb, d_wo, d_wu, d_wd)
    d_w_main = [_mm("in_proj_dw_%d" % i, n1t, p, "nn", F32, 1024, 1024, DW_TK) for i, p in enumerate(pieces)]
    d_w_ff = _mm("in_proj_forget_dw", n1t, dff, "nn", F32, 1024, LANES, DW_TK)
    (dx,), (d_norm_mix,) = _rms_bwd("norm_mix_bwd", x, norm_mix, [dn1], dh1)

    small = dict(norm_mix=d_norm_mix, fox_f_bias=d_bias[:, :FOX_H], hg_lb_logits=d_lb_logits, hg_norm=d_hg_norm,
                 norm_ffn=d_norm_ffn, conv_b=d_conv_b, norm_final=d_norm_final, conv_w=d_conv_w8[:3], loss=loss_row)
    big = (d_w_main, d_w_ff) + tuple(late)
    return dx, small, big


SMALL_KEYS = ("norm_mix", "fox_f_bias", "hg_lb_logits", "hg_norm", "norm_ffn", "conv_b", "norm_final")


def _pack_small(parts):
    rows, layout = [], []
    for key, arr in parts:
        flat = arr.reshape(-1)
        n = flat.shape[0]
        nr = -(-n // LANES)
        rows.append(jnp.pad(flat, (0, nr * LANES - n)).reshape(nr, LANES))
        layout.append((key, arr.shape, n, nr))
    packed = jnp.concatenate(rows, axis=0)
    pad = -packed.shape[0] % 8
    return jnp.pad(packed, ((0, pad), (0, 0))), layout


def _unpack_small(packed, layout):
    out, r0 = {}, 0
    for key, shape, n, nr in layout:
        out[key] = packed[r0:r0 + nr].reshape(-1)[:n].reshape(shape)
        r0 += nr
    return out


def kernel(x, norm_mix, w_in, fox_f_bias, hg_lb_logits, hg_norm, w_branch_a, w_branch_b, w_out, norm_ffn, w_up, conv_w, conv_b, w_down, norm_final, loss_target, m_norm_mix, m_w_in, m_fox_f_bias, m_hg_lb_logits, m_hg_norm, m_w_branch_a, m_w_branch_b, m_w_out, m_norm_ffn, m_w_up, m_conv_w, m_conv_b, m_w_down, m_norm_final, v_norm_mix, v_w_in, v_fox_f_bias, v_hg_lb_logits, v_hg_norm, v_w_branch_a, v_w_branch_b, v_w_out, v_norm_ffn, v_w_up, v_conv_w, v_conv_b, v_w_down, v_norm_final):
    chip = 2 * lax.axis_index("x") + lax.axis_index("y")
    halves = _weight_halves(w_in[0], w_branch_a[0], w_branch_b[0], w_out[0], w_up[0], w_down[0], conv_w[0])
    w_main, w_ff = _unpack_w_in(_all_gather8_multi("all_gather_w_in", halves[:1])[0])
    dx, small, big = _local_step(
        x[0], loss_target[0], norm_mix, fox_f_bias, hg_lb_logits, hg_norm, norm_ffn, conv_b,
        norm_final.reshape(1, D), w_main, w_ff, halves[1:])

    packed, layout = _pack_small([(k, small[k]) for k in SMALL_KEYS + ("conv_w", "loss")])
    red = _unpack_small(_all_reduce_small("all_reduce_small", packed), layout)
    loss = red["loss"][0, 0]
    g_conv_w = lax.dynamic_slice_in_dim(red["conv_w"], chip * W_UP_SHARD, W_UP_SHARD, axis=1)

    g_big = _finish_grads(big[0], big[1], big[2:])

    names = ["norm_mix", "w_in", "fox_f_bias", "hg_lb_logits", "hg_norm", "w_branch_a", "w_branch_b", "w_out",
             "norm_ffn", "w_up", "conv_w", "conv_b", "w_down", "norm_final"]
    weights = dict(norm_mix=norm_mix, w_in=w_in, fox_f_bias=fox_f_bias, hg_lb_logits=hg_lb_logits, hg_norm=hg_norm,
                   w_branch_a=w_branch_a, w_branch_b=w_branch_b, w_out=w_out, norm_ffn=norm_ffn, w_up=w_up,
                   conv_w=conv_w, conv_b=conv_b, w_down=w_down, norm_final=norm_final)
    ms = dict(norm_mix=m_norm_mix, w_in=m_w_in, fox_f_bias=m_fox_f_bias, hg_lb_logits=m_hg_lb_logits,
              hg_norm=m_hg_norm, w_branch_a=m_w_branch_a, w_branch_b=m_w_branch_b, w_out=m_w_out,
              norm_ffn=m_norm_ffn, w_up=m_w_up, conv_w=m_conv_w, conv_b=m_conv_b, w_down=m_w_down,
              norm_final=m_norm_final)
    vs = dict(norm_mix=v_norm_mix, w_in=v_w_in, fox_f_bias=v_fox_f_bias, hg_lb_logits=v_hg_lb_logits,
              hg_norm=v_hg_norm, w_branch_a=v_w_branch_a, w_branch_b=v_w_branch_b, w_out=v_w_out,
              norm_ffn=v_norm_ffn, w_up=v_w_up, conv_w=v_conv_w, conv_b=v_conv_b, w_down=v_w_down,
              norm_final=v_norm_final)

    grads, deltas, new_m, new_v = {}, {}, {}, {}
    big_names = ["w_in", "w_branch_a", "w_branch_b", "w_out", "w_up", "w_down"]
    for name, g2 in zip(big_names, g_big):
        shape = weights[name].shape
        rows = g2.shape[0]
        d_, m_, v_ = _adamw("adamw_" + name, weights[name][0], g2, ms[name][0], vs[name][0], tm=rows // 8)
        grads[name], deltas[name], new_m[name], new_v[name] = (a.reshape(shape) for a in (g2, d_, m_, v_))
    shape = conv_w.shape
    d_, m_, v_ = _adamw("adamw_conv_w", conv_w[0], g_conv_w, m_conv_w[0], v_conv_w[0])
    grads["conv_w"], deltas["conv_w"], new_m["conv_w"], new_v["conv_w"] = (
        a.reshape(shape) for a in (g_conv_w, d_, m_, v_))
    gs = {k: red[k].reshape(weights[k].shape) for k in SMALL_KEYS}
    pw, lay = _pack_small([(k, weights[k]) for k in SMALL_KEYS])
    pg, _ = _pack_small([(k, gs[k]) for k in SMALL_KEYS])
    pm, _ = _pack_small([(k, ms[k]) for k in SMALL_KEYS])
    pv, _ = _pack_small([(k, vs[k]) for k in SMALL_KEYS])
    d_, m_, v_ = (_unpack_small(a, lay) for a in _adamw("adamw_small", pw, pg, pm, pv))
    for k in SMALL_KEYS:
        grads[k], deltas[k], new_m[k], new_v[k] = gs[k], d_[k], m_[k], v_[k]

    return (loss, dx[None], *[grads[n] for n in names], *[deltas[n] for n in names],
            *[new_m[n] for n in names], *[new_v[n] for n in names])
```

```python
import jax
import jax.numpy as jnp
from jax import lax
from jax.experimental import pallas as pl
from jax.experimental.pallas import tpu as pltpu

F32 = jnp.float32
BF16 = jnp.bfloat16

D = 1024
HG_H, HG_DK = 8, 128
FOX_H, FOX_D = 16, 64
D_FF = 2816
EPS = 1e-6
N_CHIP = 4
LANES = 128
W_IN_SHARD = 2308
W_UP_SHARD = 1408
W_DOWN_SHARD = 704
FF_COL = 7168
ADAM_LR, ADAM_B1, ADAM_B2, ADAM_EPS, ADAM_WD, ADAM_STEP = 0.001, 0.9, 0.999, 1e-08, 0.01, 10

HG_C = 16
HG_T = 512
HG_UNROLL = 16
HG_UNROLL_BWD = 4
FOX_T = 512
DW_TK = 2048
VMEM_LIMIT = 56 * 1024 * 1024
MESH = pl.DeviceIdType.MESH
ANY = pl.BlockSpec(memory_space=pl.ANY)


def _cparams(sem):
    return pltpu.CompilerParams(dimension_semantics=sem, vmem_limit_bytes=VMEM_LIMIT)


def _sigmoid(x):
    return 1.0 / (1.0 + jnp.exp(-x))


def _dot(a, b, dims):
    return lax.dot_general(a.astype(BF16), b.astype(BF16), (dims, ((), ())), preferred_element_type=F32)


NN = ((1,), (0,))
NT = ((1,), (1,))
TN = ((0,), (0,))


def _split_dot(tri, x, parts, dims=NN):
    acc = None
    r = x
    for _ in range(parts):
        p = r.astype(BF16)
        t = lax.dot_general(tri, p, (dims, ((), ())), preferred_element_type=F32)
        acc = t if acc is None else acc + t
        r = r - p.astype(F32)
    return acc


def _rb(arr, tm, width, cb=0):
    return (arr, (tm, width), lambda i: (i, cb))


def _cst(arr):
    return (arr, arr.shape, lambda i: (0,) * arr.ndim)


def _rows(name, fn, n_rows, tm, ins, outs, accs=()):
    n_in, n_out, n_acc = len(ins), len(outs), len(accs)
    nb = n_rows // tm

    def body(*refs):
        vals = [r[...] for r in refs[:n_in]]
        o, a = fn(*vals)
        for r, v in zip(refs[n_in:n_in + n_out], o):
            r[...] = v.astype(r.dtype)
        if n_acc:
            acc_refs = refs[n_in + n_out:]

            @pl.when(pl.program_id(0) == 0)
            def _():
                for r in acc_refs:
                    r[...] = jnp.zeros_like(r)

            for r, v in zip(acc_refs, a):
                r[...] += v

    in_specs = [pl.BlockSpec(bs, im) for (_, bs, im) in ins]
    out_specs = [pl.BlockSpec((tm, w), lambda i: (i, 0)) for (w, _) in outs]
    out_specs += [pl.BlockSpec((r, w), lambda i: (0, 0)) for (r, w) in accs]
    out_shape = [jax.ShapeDtypeStruct((n_rows, w), dt) for (w, dt) in outs]
    out_shape += [jax.ShapeDtypeStruct((r, w), F32) for (r, w) in accs]
    res = pl.pallas_call(
        body, name=name, grid=(nb,), in_specs=in_specs, out_specs=out_specs, out_shape=out_shape,
        compiler_params=_cparams(("arbitrary",)),
    )(*[a for a, _, _ in ins])
    return (res[:n_out], res[n_out:]) if n_acc else res


def _mm(name, a, b, mode, out_dtype, tm, tn, tk, res=None):
    if mode == "nn":
        (m, k), n = a.shape, b.shape[1]
    elif mode == "nt":
        (m, k), n = a.shape, b.shape[0]
    else:
        (k, m), n = a.shape, b.shape[1]
    tm, tn, tk = min(tm, m), min(tn, n), min(tk, k)
    assert m % tm == 0 and n % tn == 0 and k % tk == 0, (name, m, n, k, tm, tn, tk)
    if mode == "nn":
        a_spec = pl.BlockSpec((tm, tk), lambda i, j, kk: (i, kk))
        b_spec = pl.BlockSpec((tk, tn), lambda i, j, kk: (kk, j))
        dims = NN
    elif mode == "nt":
        a_spec = pl.BlockSpec((tm, tk), lambda i, j, kk: (i, kk))
        b_spec = pl.BlockSpec((tn, tk), lambda i, j, kk: (j, kk))
        dims = NT
    else:
        a_spec = pl.BlockSpec((tk, tm), lambda i, j, kk: (kk, i))
        b_spec = pl.BlockSpec((tk, tn), lambda i, j, kk: (kk, j))
        dims = TN
    nk = k // tk
    has_res = res is not None
    acc_in_out = out_dtype == F32 and not has_res

    def body(*refs):
        a_ref, b_ref = refs[0], refs[1]
        r_ref = refs[2] if has_res else None
        o_ref = refs[3] if has_res else refs[2]
        part = _dot(a_ref[...], b_ref[...], dims)

        def finish(val):
            if has_res:
                val = val + r_ref[...]
            o_ref[...] = val.astype(o_ref.dtype)

        if nk == 1:
            finish(part)
        elif acc_in_out:
            kk = pl.program_id(2)

            @pl.when(kk == 0)
            def _():
                o_ref[...] = part

            @pl.when(kk > 0)
            def _():
                o_ref[...] += part
        else:
            acc_ref = refs[-1]
            kk = pl.program_id(2)

            @pl.when(kk == 0)
            def _():
                acc_ref[...] = part

            @pl.when(kk > 0)
            def _():
                acc_ref[...] += part

            @pl.when(kk == nk - 1)
            def _():
                finish(acc_ref[...])

    in_specs = [a_spec, b_spec]
    args = [a, b]
    if has_res:
        in_specs.append(pl.BlockSpec((tm, tn), lambda i, j, kk: (i, j)))
        args.append(res)
    return pl.pallas_call(
        body, name=name, grid=(m // tm, n // tn, nk), in_specs=in_specs,
        out_specs=pl.BlockSpec((tm, tn), lambda i, j, kk: (i, j)),
        out_shape=jax.ShapeDtypeStruct((m, n), out_dtype),
        scratch_shapes=[pltpu.VMEM((tm, tn), F32)] if nk > 1 and not acc_in_out else [],
        compiler_params=_cparams(("arbitrary", "arbitrary", "arbitrary")),
    )(*args)


def _mm_sum_nt(name, pieces, w, extra, tm, tn, exchange=()):
    n_p = len(pieces)
    m, k = pieces[0].shape
    n = w.shape[0]
    xa, xb = extra
    ke = xa.shape[1]
    tm, tn = min(tm, m), min(tn, n)
    nx = len(exchange)
    n_steps = (m // tm) * (n // tn) * (n_p + 1)

    def exchange_steps(refs):
        e_refs = refs[n_p + 3:n_p + 3 + nx]
        out_refs, bounce_refs = refs[n_p + 4 + nx:n_p + 4 + 2 * nx], refs[n_p + 4 + 2 * nx:n_p + 4 + 3 * nx]
        send_sems, recv_sems = refs[-2], refs[-1]
        step = (pl.program_id(0) * (n // tn) + pl.program_id(1)) * (n_p + 1) + pl.program_id(2)
        x, y, c = _me()
        my_chip = 2 * x + y
        sibling = (x, y, 1 - c)
        chips = [(1 - x, y), (x, 1 - y), (1 - x, 1 - y)]
        cp = lambda kx, q, src, dst, to: _remote(src, dst, send_sems, recv_sems, kx * nx + q, to)
        sends = lambda: ([cp(kx, q, e_refs[q].at[2 * px + py], out_refs[q].at[my_chip], (px, py, c))
                          for kx, (px, py) in enumerate(chips) for q in range(nx)]
                         + [cp(3, q, e_refs[q].at[my_chip], bounce_refs[q], sibling) for q in range(nx)])
        backs = lambda: [cp(4, q, bounce_refs[q], out_refs[q].at[my_chip], sibling) for q in range(nx)]

        @pl.when(step == 0)
        def _():
            for s_ in sends():
                s_.start()

        @pl.when(step == n_steps // 2)
        def _():
            for q in range(nx):
                cp(3, q, e_refs[q].at[my_chip], bounce_refs[q], sibling).wait_recv()
            for s_ in backs():
                s_.start()

        @pl.when(step == n_steps - 1)
        def _():
            for kx, (px, py) in enumerate(chips):
                for q in range(nx):
                    cp(kx, q, e_refs[q].at[my_chip], out_refs[q].at[2 * px + py], (px, py, c)).wait_recv()
            for q in range(nx):
                cp(4, q, bounce_refs[q], out_refs[q].at[my_chip], sibling).wait_recv()
            for s_ in sends() + backs():
                s_.wait_send()

    def body(*refs):
        p_refs, w_ref, xa_ref, xb_ref, o_ref = refs[:n_p], refs[n_p], refs[n_p + 1], refs[n_p + 2], refs[n_p + 3 + nx]
        if nx:
            exchange_steps(refs)
        kk = pl.program_id(2)

        @pl.when(kk == 0)
        def _():
            o_ref[...] = _dot(p_refs[0][...], w_ref[...], NT)

        for i in range(1, n_p):
            @pl.when(kk == i)
            def _(i=i):
                o_ref[...] += _dot(p_refs[i][...], w_ref[...], NT)

        @pl.when(kk == n_p)
        def _():
            o_ref[...] += _dot(xa_ref[...], xb_ref[...], NT)

    in_specs = [pl.BlockSpec((tm, k), lambda i, j, kk: (i, 0)) for _ in range(n_p)]
    in_specs.append(pl.BlockSpec((tn, k), lambda i, j, kk: (j, jnp.minimum(kk, n_p - 1))))
    in_specs += [pl.BlockSpec((tm, ke), lambda i, j, kk: (i, 0)), pl.BlockSpec((tn, ke), lambda i, j, kk: (j, 0))]
    outs = pl.pallas_call(
        body, name=name, grid=(m // tm, n // tn, n_p + 1), in_specs=in_specs + [ANY] * nx,
        out_specs=[pl.BlockSpec((tm, tn), lambda i, j, kk: (i, j))] + [ANY] * (2 * nx),
        out_shape=[jax.ShapeDtypeStruct((m, n), F32)] + [jax.ShapeDtypeStruct(e.shape, e.dtype) for e in exchange]
        + [jax.ShapeDtypeStruct(e.shape[1:], e.dtype) for e in exchange],
        scratch_shapes=[pltpu.SemaphoreType.DMA((5 * nx,)), pltpu.SemaphoreType.DMA((5 * nx,))] if nx else [],
        compiler_params=_cparams(("arbitrary", "arbitrary", "arbitrary")),
    )(*pieces, w, xa, xb, *exchange)
    return (outs[0], outs[1:1 + nx]) if nx else outs[0]


def _rms_fwd(name, x, gain, tm=256):
    s = x.shape[0]

    def body(x_ref, g_ref, y_ref, yt_ref):
        xb = x_ref[...]
        y = xb * lax.rsqrt(jnp.mean(xb * xb, axis=-1, keepdims=True) + EPS) * g_ref[...]
        y_ref[...] = y.astype(BF16)
        yt_ref[...] = y.T.astype(BF16)

    return pl.pallas_call(
        body, name=name, grid=(s // tm,),
        in_specs=[pl.BlockSpec((tm, D), lambda i: (i, 0)), pl.BlockSpec((1, D), lambda i: (0, 0))],
        out_specs=[pl.BlockSpec((tm, D), lambda i: (i, 0)), pl.BlockSpec((D, tm), lambda i: (0, i))],
        out_shape=[jax.ShapeDtypeStruct((s, D), BF16), jax.ShapeDtypeStruct((D, s), BF16)],
        compiler_params=_cparams(("arbitrary",)),
    )(x, gain)


def _rms_bwd(name, x, gain, dns, dres, tm=256):
    s = x.shape[0]
    n_dn = len(dns)

    def fn(xb, g, *rest):
        dn = rest[0]
        for t in rest[1:n_dn]:
            dn = dn + t
        r = lax.rsqrt(jnp.mean(xb * xb, axis=-1, keepdims=True) + EPS)
        xhat = xb * r
        dxh = dn * g
        dx = r * (dxh - xhat * jnp.mean(dxh * xhat, axis=-1, keepdims=True)) + rest[n_dn]
        return (dx,), (jnp.sum(dn * xhat, axis=0, keepdims=True),)

    ins = [_rb(x, tm, D), _cst(gain)] + [_rb(t, tm, D) for t in dns] + [_rb(dres, tm, D)]
    return _rows(name, fn, s, tm, ins, [(D, F32)], [(1, D)])


def _final(h2, target, gain, tm=256):
    s = h2.shape[0]

    def fn(hb, tb, g):
        r = lax.rsqrt(jnp.mean(hb * hb, axis=-1, keepdims=True) + EPS)
        xhat = hb * r
        e = xhat * g - tb
        dy = e * (1.0 / D)
        dxh = dy * g
        dh = r * (dxh - xhat * jnp.mean(dxh * xhat, axis=-1, keepdims=True))
        lrow = 0.5 * jnp.sum(jnp.sum(e * e, axis=-1, keepdims=True) * (1.0 / D), axis=0, keepdims=True)
        return (dh,), (jnp.sum(dy * xhat, axis=0, keepdims=True), jnp.broadcast_to(lrow, (1, LANES)))

    return _rows("final_norm_loss", fn, s, tm, [_rb(h2, tm, D), _rb(target, tm, D), _cst(gain)],
                 [(D, F32)], [(1, D), (1, LANES)])


def _merge_fwd(pa, pb, proj, tm=256):
    s = pa.shape[0]

    def fn(a, b, ga, gb):
        return (_sigmoid(ga) * a + _sigmoid(gb) * b,), ()

    ins = [_rb(pa, tm, D), _rb(pb, tm, D), _rb(proj, tm, D, 7), _rb(proj, tm, D, 8)]
    return _rows("merge_fwd", fn, s, tm, ins, [(D, BF16)])[0]


def _merge_bwd(dmerged, pa, pb, proj, tm=256):
    s = pa.shape[0]

    def fn(dm, a, b, ga, gb):
        sa, sb = _sigmoid(ga), _sigmoid(gb)
        return (dm * sa, dm * sb, dm * a * sa * (1.0 - sa), dm * b * sb * (1.0 - sb)), ()

    ins = [_rb(dmerged, tm, D), _rb(pa, tm, D), _rb(pb, tm, D), _rb(proj, tm, D, 7), _rb(proj, tm, D, 8)]
    return _rows("merge_bwd", fn, s, tm, ins, [(D, BF16), (D, BF16), (D, BF16), (D, BF16)])


def _gelu_parts(x):
    cdf = 0.5 * (1.0 + lax.erf(x * 0.7071067811865476))
    pdf = 0.3989422804014327 * jnp.exp(-0.5 * x * x)
    return x * cdf, cdf + x * pdf


def _conv_taps(u_ext, n_out):
    cur = u_ext[8:8 + n_out]
    m1 = pltpu.roll(u_ext, 1, 0)[8:8 + n_out]
    m2 = pltpu.roll(u_ext, 2, 0)[8:8 + n_out]
    return m2, m1, cur


def _convglu_fwd(u, conv_w8, conv_b, tm=256):
    s, w = u.shape
    tb = tm // 8

    def fn(ub, up, cw, cb):
        i = pl.program_id(0)
        up = jnp.where(i == 0, 0.0, up)
        m2, m1, cur = _conv_taps(jnp.concatenate([up, ub], axis=0), tm)
        acc = cb + cw[0:1] * m2 + cw[1:2] * m1 + cw[2:3] * cur
        gl, dgl = _gelu_parts(acc[:, :D_FF])
        val = acc[:, D_FF:]
        return (gl * val, gl, val * dgl), ()

    ins = [_rb(u, tm, w), (u, (8, w), lambda i: (jnp.maximum(i * tb - 1, 0), 0)), _cst(conv_w8), _cst(conv_b)]
    return _rows("convglu_fwd", fn, s, tm, ins, [(D_FF, BF16)] * 3)


def _convglu_bwd(u, dact, gl, gd, conv_w8, tm=256):
    s, w = u.shape
    tb = tm // 8
    nb = s // tm

    def fn(ub, up, db, dn, glb, gln, gdb, gdn, cw):
        i = pl.program_id(0)
        up = jnp.where(i == 0, 0.0, up)
        dn = jnp.where(i == nb - 1, 0.0, dn.astype(F32))
        ne = tm + 8
        m2, m1, cur = _conv_taps(jnp.concatenate([up, ub], axis=0), tm)
        ext = lambda blk, nxt: jnp.concatenate([blk.astype(F32), nxt.astype(F32)], axis=0)
        de = ext(db, dn)
        dacc = jnp.concatenate([de * ext(gdb, gdn), de * ext(glb, gln)], axis=1)
        p1 = pltpu.roll(dacc, ne - 1, 0)[:tm]
        p2 = pltpu.roll(dacc, ne - 2, 0)[:tm]
        d0 = dacc[:tm]
        du = cw[2:3] * d0 + cw[1:2] * p1 + cw[0:1] * p2
        zero5 = jnp.zeros((5, w), F32)
        dcw = jnp.concatenate([
            jnp.sum(d0 * m2, axis=0, keepdims=True), jnp.sum(d0 * m1, axis=0, keepdims=True),
            jnp.sum(d0 * cur, axis=0, keepdims=True), zero5], axis=0)
        return (du,), (dcw, jnp.sum(d0, axis=0, keepdims=True))

    nxt = lambda arr: (arr, (8, D_FF), lambda i: (jnp.minimum((i + 1) * tb, s // 8 - 1), 0))
    ins = [_rb(u, tm, w), (u, (8, w), lambda i: (jnp.maximum(i * tb - 1, 0), 0)),
           _rb(dact, tm, D_FF), nxt(dact), _rb(gl, tm, D_FF), nxt(gl), _rb(gd, tm, D_FF), nxt(gd), _cst(conv_w8)]
    return _rows("convglu_bwd", fn, s, tm, ins, [(w, BF16)], [(8, w), (1, w)])


def _chunk_scan(x, t_iota, reverse):
    k = 1
    while k < HG_C:
        if reverse:
            x = x + jnp.where(t_iota < HG_C - k, pltpu.roll(x, HG_C - k, 0), 0.0)
        else:
            x = x + jnp.where(t_iota >= k, pltpu.roll(x, k, 0), 0.0)
        k *= 2
    return x


def _hg_gates(hq, hf, lb):
    sq = _sigmoid(hq)
    q = hq * sq
    sg = _sigmoid(hf)
    f = lb + (1.0 - lb) * sg
    return q, sq, sg, f, 1.0 - f, jnp.log(f)


def _lb_of(logits):
    l0, l1 = logits[0:1], logits[1:2]
    mx = jnp.maximum(l0, l1)
    e0, e1 = jnp.exp(l0 - mx), jnp.exp(l1 - mx)
    return e0 / (e0 + e1)


def _tri(n, lower):
    r = lax.broadcasted_iota(jnp.int32, (n, n), 0)
    c = lax.broadcasted_iota(jnp.int32, (n, n), 1)
    return jnp.where((r >= c) if lower else (r <= c), 1.0, 0.0).astype(BF16)


def _hg_intra_terms(q, kk, b, t_iota):
    ws, ps = [], []
    for s in range(HG_C):
        p = jnp.where(t_iota >= s, jnp.exp(b - b[s:s + 1]), 0.0)
        ps.append(p)
        ws.append(q * kk[s:s + 1] * p)
    return jnp.concatenate(ws, axis=0), ps


def _hg_fwd(proj, lb_logits):
    s = proj.shape[0]
    nt = s // HG_T
    nc = HG_T // HG_C

    def body(q_ref, f_ref, i_ref, l_ref, o_ref, st_ref, state):
        @pl.when(pl.program_id(1) == 0)
        def _():
            state[...] = jnp.zeros_like(state)

        st_ref[0, 0] = state[...]
        lb = _lb_of(l_ref[...])
        ones = jnp.ones((HG_DK, HG_DK), BF16)
        t_iota = lax.broadcasted_iota(jnp.int32, (HG_C, HG_DK), 0)
        cc = HG_C * HG_C

        def group(gi, st):
            units = []
            for u in range(HG_UNROLL):
                r = pl.ds(pl.multiple_of((gi * HG_UNROLL + u) * HG_C, HG_C), HG_C)
                q, _, _, _, kk, g = _hg_gates(q_ref[r, :], f_ref[r, :], lb)
                b = _chunk_scan(g, t_iota, False)
                b_end = b[HG_C - 1:HG_C]
                w_all, _ = _hg_intra_terms(q, kk, b, t_iota)
                units.append((r, i_ref[r, :], q * jnp.exp(b), jnp.exp(b_end), kk * jnp.exp(b_end - b), w_all))
            a_all = _dot(jnp.concatenate([un[5] for un in units], axis=0), ones, NN)
            kvs = [_dot(v, kd, TN) for (_, v, _, _, kd, _) in units]
            sts = [st]
            for (_, _, _, dec, _, _), kv in zip(units, kvs):
                sts.append(sts[-1] * dec + kv)
            for ui, (r, v, qd, _, _, _) in enumerate(units):
                o = _dot(qd, sts[ui], NT)
                for si in range(HG_C):
                    o = o + a_all[ui * cc + si * HG_C:ui * cc + (si + 1) * HG_C] * v[si:si + 1]
                o_ref[r, :] = o
            return sts[-1]

        state[...] = lax.fori_loop(0, nc // HG_UNROLL, group, state[...])

    col = lambda off: pl.BlockSpec((HG_T, HG_DK), lambda h, t: (t, off + h))
    return pl.pallas_call(
        body, name="hgrn2_fwd", grid=(HG_H, nt),
        in_specs=[col(0), col(8), col(16), pl.BlockSpec((2, HG_DK), lambda h, t: (0, h))],
        out_specs=[pl.BlockSpec((HG_T, HG_DK), lambda h, t: (t, h)),
                   pl.BlockSpec((1, 1, HG_DK, HG_DK), lambda h, t: (h, t, 0, 0))],
        out_shape=[jax.ShapeDtypeStruct((s, D), F32), jax.ShapeDtypeStruct((HG_H, nt, HG_DK, HG_DK), F32)],
        scratch_shapes=[pltpu.VMEM((HG_DK, HG_DK), F32)],
        compiler_params=_cparams(("arbitrary", "arbitrary")),
    )(proj, proj, proj, lb_logits)


def _hg_bwd(proj, lb_logits, states, do_raw):
    s = proj.shape[0]
    nt = s // HG_T
    nc = HG_T // HG_C

    def body(q_ref, f_ref, i_ref, l_ref, st_ref, do_ref, dq_ref, df_ref, di_ref, dl_ref, st_all, adj):
        tb = pl.program_id(1)

        @pl.when(tb == 0)
        def _():
            adj[...] = jnp.zeros_like(adj)
            dl_ref[...] = jnp.zeros_like(dl_ref)

        lb = _lb_of(l_ref[...])
        ones = jnp.ones((HG_DK, HG_DK), BF16)
        t_iota = lax.broadcasted_iota(jnp.int32, (HG_C, HG_DK), 0)
        cc = HG_C * HG_C

        def fwd_group(gi, st):
            terms = []
            for u in range(HG_UNROLL):
                ci = gi * HG_UNROLL + u
                r = pl.ds(pl.multiple_of(ci * HG_C, HG_C), HG_C)
                _, _, _, _, kk, g = _hg_gates(q_ref[r, :], f_ref[r, :], lb)
                b = _chunk_scan(g, t_iota, False)
                b_end = b[HG_C - 1:HG_C]
                terms.append((ci, jnp.exp(b_end), _dot(i_ref[r, :], kk * jnp.exp(b_end - b), TN)))
            for ci, dec, kv in terms:
                st_all[ci] = st
                st = st * dec + kv
            return st

        lax.fori_loop(0, nc // HG_UNROLL, fwd_group, st_ref[0, 0])

        def bwd_group(gj, dlb):
            units = []
            for u in range(HG_UNROLL_BWD):
                ci = nc - 1 - (gj * HG_UNROLL_BWD + u)
                r = pl.ds(pl.multiple_of(ci * HG_C, HG_C), HG_C)
                hq, hf, v, do = q_ref[r, :], f_ref[r, :], i_ref[r, :], do_ref[r, :]
                q, sq, sg, f, kk, g = _hg_gates(hq, hf, lb)
                b = _chunk_scan(g, t_iota, False)
                b_end = b[HG_C - 1:HG_C]
                e_b, e_be, dec = jnp.exp(b), jnp.exp(b_end - b), jnp.exp(b_end)
                w_all, ps = _hg_intra_terms(q, kk, b, t_iota)
                x_all = jnp.concatenate([do * v[si:si + 1] for si in range(HG_C)], axis=0)
                units.append(dict(ci=ci, r=r, hq=hq, v=v, do=do, q=q, sq=sq, sg=sg, f=f, kk=kk, e_b=e_b, e_be=e_be,
                                  dec=dec, kd=kk * e_be, w=w_all, ps=ps, x=x_all))
            both = _dot(jnp.concatenate([un["w"] for un in units] + [un["x"] for un in units], axis=0), ones, NN)
            st0s = [st_all[un["ci"]] for un in units]
            st_ends = [st0 * un["dec"] + _dot(un["v"], un["kd"], TN) for un, st0 in zip(units, st0s)]
            dqks = [_dot(un["do"], un["q"] * un["e_b"], TN) for un in units]
            es = [adj[...]]
            for un, dqk in zip(units, dqks):
                es.append(es[-1] * un["dec"] + dqk)
            adj[...] = es[-1]
            for ui, un in enumerate(units):
                e, q, kk, v, do = es[ui], un["q"], un["kk"], un["v"], un["do"]
                tail = jnp.sum(e * st_ends[ui], axis=0, keepdims=True)
                dq = un["e_b"] * _dot(do, st0s[ui], NN)
                dk = un["e_be"] * _dot(v, e, NN)
                dv = _dot(un["kd"], e, NT)
                a0 = ui * cc
                d0 = (HG_UNROLL_BWD + ui) * cc
                for si in range(HG_C):
                    da = both[d0 + si * HG_C:d0 + (si + 1) * HG_C]
                    aa = both[a0 + si * HG_C:a0 + (si + 1) * HG_C]
                    dap = da * un["ps"][si]
                    dq = dq + dap * kk[si:si + 1]
                    hit = t_iota == si
                    dk = dk + jnp.where(hit, jnp.sum(dap * q, axis=0, keepdims=True), 0.0)
                    dv = dv + jnp.where(hit, jnp.sum(aa * do, axis=0, keepdims=True), 0.0)
                dg = _chunk_scan(q * dq - kk * dk, t_iota, True) + tail
                dfg = dg / un["f"] - dk
                sq, sg, hq, r = un["sq"], un["sg"], un["hq"], un["r"]
                dq_ref[r, :] = (dq * sq * (1.0 + hq * (1.0 - sq))).astype(dq_ref.dtype)
                df_ref[r, :] = (dfg * (1.0 - lb) * sg * (1.0 - sg)).astype(df_ref.dtype)
                di_ref[r, :] = dv.astype(di_ref.dtype)
                dlb = dlb + jnp.sum(dfg * (1.0 - sg), axis=0, keepdims=True)
            return dlb

        dlb = lax.fori_loop(0, nc // HG_UNROLL_BWD, bwd_group, jnp.zeros((1, HG_DK), F32))
        dl0 = dlb * lb * (1.0 - lb)
        dl_ref[...] += jnp.concatenate([dl0, -dl0], axis=0)

    col = lambda off: pl.BlockSpec((HG_T, HG_DK), lambda h, t: (nt - 1 - t, off + h))
    out_col = pl.BlockSpec((HG_T, HG_DK), lambda h, t: (nt - 1 - t, h))
    return pl.pallas_call(
        body, name="hgrn2_bwd", grid=(HG_H, nt),
        in_specs=[col(0), col(8), col(16), pl.BlockSpec((2, HG_DK), lambda h, t: (0, h)),
                  pl.BlockSpec((1, 1, HG_DK, HG_DK), lambda h, t: (h, nt - 1 - t, 0, 0)), col(0)],
        out_specs=[out_col, out_col, out_col, pl.BlockSpec((2, HG_DK), lambda h, t: (0, h))],
        out_shape=[jax.ShapeDtypeStruct((s, D), BF16)] * 3 + [jax.ShapeDtypeStruct((2, D), F32)],
        scratch_shapes=[pltpu.VMEM((nc, HG_DK, HG_DK), F32), pltpu.VMEM((HG_DK, HG_DK), F32)],
        compiler_params=_cparams(("arbitrary", "arbitrary")),
    )(proj, proj, proj, lb_logits, states, do_raw)


def _hg_post_fwd(o_raw, proj, gnorm, tm=256):
    s = o_raw.shape[0]

    def fn(o, hg, gn):
        outs = []
        for h in range(HG_H):
            sl = slice(h * HG_DK, (h + 1) * HG_DK)
            oh, gh = o[:, sl], hg[:, sl]
            r = lax.rsqrt(jnp.mean(oh * oh, axis=-1, keepdims=True) + EPS)
            outs.append(oh * r * gn * (gh * _sigmoid(gh)))
        return (jnp.concatenate(outs, axis=1),), ()

    return _rows("hgrn2_out_fwd", fn, s, tm, [_rb(o_raw, tm, D), _rb(proj, tm, D, 3), _cst(gnorm)], [(D, BF16)])[0]


def _hg_post_bwd(do_a, o_raw, proj, gnorm, tm=256):
    s = o_raw.shape[0]

    def fn(da, o, hg, gn):
        dos, dhgs = [], []
        dgn = jnp.zeros((1, HG_DK), F32)
        for h in range(HG_H):
            sl = slice(h * HG_DK, (h + 1) * HG_DK)
            oh, gh, dh = o[:, sl], hg[:, sl], da[:, sl]
            r = lax.rsqrt(jnp.mean(oh * oh, axis=-1, keepdims=True) + EPS)
            xhat = oh * r
            sg = _sigmoid(gh)
            dy = dh * (gh * sg)
            dhgs.append(dh * xhat * gn * sg * (1.0 + gh * (1.0 - sg)))
            dgn = dgn + jnp.sum(dy * xhat, axis=0, keepdims=True)
            dxh = dy * gn
            dos.append(r * (dxh - xhat * jnp.mean(dxh * xhat, axis=-1, keepdims=True)))
        return (jnp.concatenate(dos, axis=1), jnp.concatenate(dhgs, axis=1)), (dgn,)

    ins = [_rb(do_a, tm, D), _rb(o_raw, tm, D), _rb(proj, tm, D, 3), _cst(gnorm)]
    return _rows("hgrn2_out_bwd", fn, s, tm, ins, [(D, F32), (D, BF16)], [(1, HG_DK)])


def _log_sigmoid(z):
    return jnp.minimum(z, 0.0) - jnp.log(1.0 + jnp.exp(-jnp.abs(z)))


def _fox_gate_bwd(dct, pff, bias, tm=256):
    s = pff.shape[0]
    nb = s // tm

    def body(d_ref, p_ref, b_ref, dff_ref, db_ref, carry):
        @pl.when(pl.program_id(0) == 0)
        def _():
            carry[...] = jnp.zeros_like(carry)
            db_ref[...] = jnp.zeros_like(db_ref)

        dc = d_ref[...].T
        dlf = _split_dot(_tri(tm, False), dc, 3) + carry[0:1]
        carry[...] = jnp.broadcast_to(dlf[0:1], carry.shape)
        dff = dlf * _sigmoid(-(p_ref[...] + b_ref[...]))
        dff_ref[...] = dff
        db_ref[...] += jnp.sum(dff, axis=0, keepdims=True)

    return pl.pallas_call(
        body, name="fox_gate_bwd", grid=(nb,),
        in_specs=[pl.BlockSpec((LANES, tm), lambda i: (0, nb - 1 - i)),
                  pl.BlockSpec((tm, LANES), lambda i: (nb - 1 - i, 0)), pl.BlockSpec((1, LANES), lambda i: (0, 0))],
        out_specs=[pl.BlockSpec((tm, LANES), lambda i: (nb - 1 - i, 0)), pl.BlockSpec((1, LANES), lambda i: (0, 0))],
        out_shape=[jax.ShapeDtypeStruct((s, LANES), F32), jax.ShapeDtypeStruct((1, LANES), F32)],
        scratch_shapes=[pltpu.VMEM((8, LANES), F32)],
        compiler_params=_cparams(("arbitrary",)),
    )(dct, pff, bias)


def _diag_mask(t):
    r = lax.broadcasted_iota(jnp.int32, (t, t), 0)
    c = lax.broadcasted_iota(jnp.int32, (t, t), 1)
    return r >= c


AUX_ONES = 6


def _pieces(x):
    h = x.astype(BF16)
    r = x - h.astype(F32)
    m = r.astype(BF16)
    return h, m, (r - m.astype(F32)).astype(BF16)


def _lane_put(lane, cols, base):
    out = None
    for i, col in enumerate(cols):
        term = jnp.where(lane == base + i, col.astype(F32), 0.0)
        out = term if out is None else out + term
    return out


def _fox_prep(proj, pff, bias, tm=256):
    s = pff.shape[0]

    def body(q_ref, k_ref, v_ref, p_ref, b_ref, qb_ref, kb_ref, vb_ref, ka_ref, carry):
        @pl.when(pl.program_id(0) == 0)
        def _():
            carry[...] = jnp.zeros_like(carry)

        qb_ref[...] = (q_ref[...] * 0.125).astype(BF16)
        kb_ref[...] = k_ref[...].astype(BF16)
        vb_ref[...] = v_ref[...].astype(BF16)
        lf = _log_sigmoid(p_ref[...] + b_ref[...])
        c = _split_dot(_tri(tm, True), lf, 3) + carry[0:1]
        carry[...] = jnp.broadcast_to(c[tm - 1:tm], carry.shape)
        lane = lax.broadcasted_iota(jnp.int32, (tm, LANES), 1)
        ones = jnp.where((lane >= AUX_ONES) & (lane < AUX_ONES + 6), 1.0, 0.0)
        for p in range(FOX_H // 2):
            aux = ones
            for z in range(2):
                col = jnp.sum(jnp.where(lane == 2 * p + z, c, 0.0), axis=1, keepdims=True)
                aux = aux + _lane_put(lane, _pieces(-col), 3 * z)
            ka_ref[:, p * LANES:(p + 1) * LANES] = aux.astype(BF16)

    row = lambda cb: pl.BlockSpec((tm, D), lambda i: (i, cb))
    return pl.pallas_call(
        body, name="fox_prep", grid=(s // tm,),
        in_specs=[row(4), row(5), row(6), pl.BlockSpec((tm, LANES), lambda i: (i, 0)),
                  pl.BlockSpec((1, LANES), lambda i: (0, 0))],
        out_specs=[row(0)] * 4, out_shape=[jax.ShapeDtypeStruct((s, D), BF16)] * 4,
        scratch_shapes=[pltpu.VMEM((8, LANES), F32)],
        compiler_params=_cparams(("arbitrary",)),
    )(proj, proj, proj, pff, bias)


def _fox_fwd(qb, kb, vb, ka):
    s = qb.shape[0]
    t = min(FOX_T, s)
    nq = s // t

    def body(q_ref, k_ref, v_ref, ka_ref, o_ref, la_ref):
        i = pl.program_id(1)
        lane = lax.broadcasted_iota(jnp.int32, (t, LANES), 1)
        in_a = lane < FOX_D
        q = q_ref[...]
        zero = jnp.zeros_like(q)
        qh = [jnp.where(in_a, q, zero), jnp.where(in_a, zero, q)]
        c_ones = [jnp.where((lane >= 3 * z) & (lane < 3 * z + 3), 1.0, 0.0) for z in range(2)]

        def keys(j):
            rows = pl.ds(pl.multiple_of(j * t, t), t)
            return jnp.concatenate([k_ref[rows, :], ka_ref[rows, :]], axis=1), rows

        dmask = _diag_mask(t)

        def logits(qx, kk, masked):
            e = lax.dot_general(qx, kk, (NT, ((), ())), preferred_element_type=F32)
            return jnp.where(dmask, e, -1e30) if masked else e

        qc = [jnp.concatenate([qh[z], c_ones[z].astype(BF16)], axis=1) for z in range(2)]

        def step(j, carry, masked):
            kk, rows = keys(j)
            vj = v_ref[rows, :]
            scores = [logits(qc[z], kk, masked) for z in range(2)]
            one = jnp.ones_like(vj)
            vh = [jnp.where(in_a, vj, one), jnp.where(in_a, one, vj)]
            out = []
            for z in range(2):
                m, acc = carry[z]
                m_new = jnp.maximum(m, jnp.max(scores[z], axis=1, keepdims=True))
                p = jnp.exp(scores[z] - m_new)
                out.append((m_new, jnp.exp(m - m_new) * acc + _dot(p, vh[z], NN)))
            return tuple(out)

        init = tuple((jnp.full((t, 1), -1e30, F32), jnp.zeros((t, LANES), F32)) for _ in range(2))
        (ma, acc_a), (mb, acc_b) = step(i, lax.fori_loop(0, i, lambda j, c: step(j, c, False), init), True)
        la = jnp.sum(jnp.where(lane == FOX_D, acc_a, 0.0), axis=1, keepdims=True)
        lb = jnp.sum(jnp.where(lane == 0, acc_b, 0.0), axis=1, keepdims=True)
        o_ref[...] = jnp.where(in_a, acc_a / la, acc_b / lb).astype(o_ref.dtype)
        la_ref[...] = (_lane_put(lane, _pieces(-(ma + jnp.log(la))), AUX_ONES)
                       + _lane_put(lane, _pieces(-(mb + jnp.log(lb))), AUX_ONES + 3)).astype(la_ref.dtype)

    blk = pl.BlockSpec((t, LANES), lambda p, i: (i, p))
    whole = pl.BlockSpec((s, LANES), lambda p, i: (0, p))
    return pl.pallas_call(
        body, name="fox_attn_fwd", grid=(FOX_H // 2, nq), in_specs=[blk, whole, whole, whole],
        out_specs=[blk, blk], out_shape=[jax.ShapeDtypeStruct((s, D), BF16)] * 2,
        compiler_params=_cparams(("arbitrary", "arbitrary")),
    )(qb, kb, vb, ka)


def _fox_bwd(qb, kb, vb, ka, ob, laux, dob):
    s = qb.shape[0]
    t = min(FOX_T, s)
    nq = s // t

    def body(q_ref, k_ref, v_ref, ka_ref, o_ref, la_ref, do_ref, dq_ref, dk_ref, dv_ref, dc_ref, dkt, dvt):
        i = pl.program_id(1)

        @pl.when(i == 0)
        def _():
            dkt[...] = jnp.zeros_like(dkt)
            dvt[...] = jnp.zeros_like(dvt)
            dc_ref[...] = jnp.zeros_like(dc_ref)

        lane = lax.broadcasted_iota(jnp.int32, (t, LANES), 1)
        in_a = lane < FOX_D
        q, do, la = q_ref[...], do_ref[...], la_ref[...].astype(F32)
        zero = jnp.zeros_like(q)
        qh = [jnp.where(in_a, q, zero), jnp.where(in_a, zero, q)]
        doh = [jnp.where(in_a, do, zero), jnp.where(in_a, zero, do)]
        qt = [h.astype(F32).T.astype(BF16) for h in qh]
        dot_ = [h.astype(F32).T.astype(BF16) for h in doh]
        prod = do.astype(F32) * o_ref[...].astype(F32)
        qx, dox = [], []
        for z in range(2):
            delta = jnp.sum(jnp.where(in_a if z == 0 else ~in_a, prod, 0.0), axis=1, keepdims=True)
            c_ones = jnp.where((lane >= 3 * z) & (lane < 3 * z + 3), 1.0, 0.0)
            lse_lanes = (lane >= AUX_ONES + 3 * z) & (lane < AUX_ONES + 3 * z + 3)
            qx.append(jnp.concatenate([qh[z], (c_ones + jnp.where(lse_lanes, la, 0.0)).astype(BF16)], axis=1))
            dox.append(jnp.concatenate([doh[z], _lane_put(lane, _pieces(-delta), 3 * z).astype(BF16)], axis=1))
        v_ones = jnp.where(lane < 6, 1.0, 0.0).astype(BF16)
        dmask = _diag_mask(t)

        def step(j, carry, masked):
            rows = pl.ds(pl.multiple_of(j * t, t), t)
            kj, vj = k_ref[rows, :], v_ref[rows, :]
            kk = jnp.concatenate([kj, ka_ref[rows, :]], axis=1)
            vv = jnp.concatenate([vj, v_ones], axis=1)
            out = []
            dk_add, dv_add = None, None
            for z in range(2):
                dq, rsum = carry[z]
                e = lax.dot_general(qx[z], kk, (NT, ((), ())), preferred_element_type=F32)
                if masked:
                    e = jnp.where(dmask, e, -1e30)
                p = jnp.exp(e)
                ds = p * lax.dot_general(dox[z], vv, (NT, ((), ())), preferred_element_type=F32)
                dkz, dvz = _dot(qt[z], ds, NN), _dot(dot_[z], p, NN)
                dk_add = dkz if dk_add is None else dk_add + dkz
                dv_add = dvz if dv_add is None else dv_add + dvz
                dc_ref[0, z, j] += -jnp.sum(ds, axis=0, keepdims=True)
                out.append((dq + _dot(ds, kj, NN), rsum + jnp.sum(ds, axis=1, keepdims=True)))
            dkt[j] += dk_add
            dvt[j] += dv_add
            return tuple(out)

        init = tuple((jnp.zeros((t, LANES), F32), jnp.zeros((t, 1), F32)) for _ in range(2))
        (dq_a, rs_a), (dq_b, rs_b) = step(i, lax.fori_loop(0, i, lambda j, c: step(j, c, False), init), True)
        for z, rs in enumerate((rs_a, rs_b)):
            dc_ref[0, z, i] += jnp.transpose(jnp.broadcast_to(rs, (t, LANES)))[0:1]
        dq_ref[...] = (jnp.where(in_a, dq_a, dq_b) * 0.125).astype(dq_ref.dtype)

        @pl.when(i == nq - 1)
        def _():
            for jb in range(nq):
                dk_ref[jb * t:(jb + 1) * t, :] = dkt[jb].T.astype(dk_ref.dtype)
                dv_ref[jb * t:(jb + 1) * t, :] = dvt[jb].T.astype(dv_ref.dtype)

    blk = pl.BlockSpec((t, LANES), lambda p, i: (i, p))
    whole = pl.BlockSpec((s, LANES), lambda p, i: (0, p))
    return pl.pallas_call(
        body, name="fox_attn_bwd", grid=(FOX_H // 2, nq),
        in_specs=[blk, whole, whole, whole, blk, blk, blk],
        out_specs=[blk, whole, whole, pl.BlockSpec((1, 2, nq, 1, t), lambda p, i: (p, 0, 0, 0, 0))],
        out_shape=[jax.ShapeDtypeStruct((s, D), BF16)] * 3 + [jax.ShapeDtypeStruct((FOX_H // 2, 2, nq, 1, t), F32)],
        scratch_shapes=[pltpu.VMEM((nq, LANES, t), F32), pltpu.VMEM((nq, LANES, t), F32)],
        compiler_params=_cparams(("arbitrary", "arbitrary")),
    )(qb, kb, vb, ka, ob, laux, dob)


def _adamw(name, w, g, m, v, tm=None):
    rows, width = w.shape
    tm = rows if tm is None else tm
    c1 = 1.0 - ADAM_B1 ** ADAM_STEP
    c2 = 1.0 - ADAM_B2 ** ADAM_STEP

    def fn(wb, gb, mb, vb):
        m_new = ADAM_B1 * mb + (1.0 - ADAM_B1) * gb
        v_new = ADAM_B2 * vb + (1.0 - ADAM_B2) * (gb * gb)
        delta = -ADAM_LR * ((m_new / c1) / (jnp.sqrt(v_new / c2) + ADAM_EPS) + ADAM_WD * wb)
        return (delta, m_new, v_new), ()

    ins = [_rb(a, tm, width) for a in (w, g, m, v)]
    return _rows(name, fn, rows, tm, ins, [(width, F32)] * 3)


def _me():
    return lax.axis_index("x"), lax.axis_index("y"), lax.axis_index("c")


def _all_reduce_small(name, block):
    r, n = block.shape

    def body(x_ref, sum_ref, gath, send_sems, recv_sems):
        x, y, c = _me()
        me = 4 * x + 2 * y + c
        gath[me] = x_ref[...]
        sends = []
        for k in range(1, 8):
            px = x ^ ((k >> 2) & 1)
            py = y ^ ((k >> 1) & 1)
            pc = c ^ (k & 1)
            sends.append(pltpu.make_async_remote_copy(
                src_ref=x_ref, dst_ref=gath.at[me], send_sem=send_sems.at[k - 1], recv_sem=recv_sems.at[k - 1],
                device_id=(px, py, pc), device_id_type=MESH))
        for cp in sends:
            cp.start()
        for k in range(1, 8):
            peer = me ^ k
            pltpu.make_async_remote_copy(
                src_ref=x_ref, dst_ref=gath.at[peer], send_sem=send_sems.at[k - 1], recv_sem=recv_sems.at[k - 1],
                device_id=(x, y, c), device_id_type=MESH).wait_recv()
        for cp in sends:
            cp.wait_send()
        acc = gath[0]
        for d in range(1, 8):
            acc = acc + gath[d]
        sum_ref[...] = acc

    vm = pl.BlockSpec(memory_space=pltpu.VMEM)
    return pl.pallas_call(
        body, name=name, in_specs=[vm], out_specs=vm, out_shape=jax.ShapeDtypeStruct((r, n), F32),
        scratch_shapes=[pltpu.VMEM((8, r, n), F32), pltpu.SemaphoreType.DMA((7,)), pltpu.SemaphoreType.DMA((7,))],
    )(block)


WD_EXT_ROWS = 736


def _remote(src, dst, send_sems, recv_sems, k, to):
    return pltpu.make_async_remote_copy(src_ref=src, dst_ref=dst, send_sem=send_sems.at[k], recv_sem=recv_sems.at[k],
                                        device_id=to, device_id_type=MESH)


def _all_gather8_multi(name, blocks):
    nt = len(blocks)

    def body(*refs):
        x_refs, out_refs, send_sems, recv_sems = refs[:nt], refs[nt:2 * nt], refs[-2], refs[-1]
        x, y, c = _me()
        me, sibling = (x, y, c), (x, y, 1 - c)
        chips = [(1 - x, y), (x, 1 - y), (1 - x, 1 - y)]
        slot = lambda q, p: out_refs[q].at[4 * p[0] + 2 * p[1] + p[2]]

        def copies(k, blk, to, from_input=False):
            return [_remote(x_refs[q] if from_input else slot(q, blk), slot(q, blk), send_sems, recv_sems, k * nt + q, to)
                    for q in range(nt)]

        first = copies(0, me, sibling, True)
        for j, chip in enumerate(chips):
            first += copies(1 + j, me, (*chip, c), True)
        for cp in first:
            cp.start()
        passed = []
        for j, chip in enumerate(chips):
            for cp in copies(1 + j, (*chip, c), me):
                cp.wait_recv()
            fwd = copies(4 + j, (*chip, c), sibling)
            for cp in fwd:
                cp.start()
            passed += fwd
        for cp in copies(0, sibling, me):
            cp.wait_recv()
        back = copies(7, sibling, sibling)
        for cp in back:
            cp.start()
        for j, chip in enumerate(chips):
            for cp in copies(4 + j, (*chip, 1 - c), me):
                cp.wait_recv()
        for cp in copies(7, me, me):
            cp.wait_recv()
        for cp in first + passed + back:
            cp.wait_send()

    return pl.pallas_call(
        body, name=name, in_specs=[ANY] * nt, out_specs=[ANY] * nt,
        out_shape=[jax.ShapeDtypeStruct((8,) + b.shape, b.dtype) for b in blocks],
        scratch_shapes=[pltpu.SemaphoreType.DMA((8 * nt,)), pltpu.SemaphoreType.DMA((8 * nt,))],
    )(*blocks)


def _swap_halves_multi(name, gs):
    nt = len(gs)
    n_chip = gs[0].shape[0]

    def body(*refs):
        g_refs, got_refs, send_sems, recv_sems = refs[:nt], refs[nt:2 * nt], refs[-2], refs[-1]
        x, y, c = _me()
        cps = [_remote(g_refs[q].at[j, 1 - c], got_refs[q].at[j], send_sems, recv_sems, q * n_chip + j, (x, y, 1 - c))
               for q in range(nt) for j in range(n_chip)]
        for cp in cps:
            cp.start()
        for cp in cps:
            cp.wait()

    return pl.pallas_call(
        body, name=name, in_specs=[ANY] * nt, out_specs=[ANY] * nt,
        out_shape=[jax.ShapeDtypeStruct((g.shape[0],) + g.shape[2:], g.dtype) for g in gs],
        scratch_shapes=[pltpu.SemaphoreType.DMA((nt * n_chip,)), pltpu.SemaphoreType.DMA((nt * n_chip,))],
    )(*gs)


def _swap_sibling_multi(name, xs):
    nt = len(xs)

    def body(*refs):
        x_refs, out_refs, send_sems, recv_sems = refs[:nt], refs[nt:2 * nt], refs[-2], refs[-1]
        x, y, c = _me()
        cps = [_remote(x_refs[q], out_refs[q], send_sems, recv_sems, q, (x, y, 1 - c)) for q in range(nt)]
        for cp in cps:
            cp.start()
        for cp in cps:
            cp.wait()

    return pl.pallas_call(
        body, name=name, in_specs=[ANY] * nt, out_specs=[ANY] * nt,
        out_shape=[jax.ShapeDtypeStruct(a.shape, a.dtype) for a in xs],
        scratch_shapes=[pltpu.SemaphoreType.DMA((nt,)), pltpu.SemaphoreType.DMA((nt,))],
    )(*xs)


def _chip_exchange_multi(name, ps):
    nt = len(ps)

    def body(*refs):
        p_refs, out_refs, bounce_refs = refs[:nt], refs[nt:2 * nt], refs[2 * nt:3 * nt]
        send_sems, recv_sems = refs[-2], refs[-1]
        x, y, c = _me()
        my_chip = 2 * x + y
        sibling = (x, y, 1 - c)
        chips = [(1 - x, y), (x, 1 - y), (1 - x, 1 - y)]
        cp = lambda k, q, src, dst, to: _remote(src, dst, send_sems, recv_sems, k * nt + q, to)
        sends = [cp(k, q, p_refs[q].at[2 * px + py], out_refs[q].at[my_chip], (px, py, c))
                 for k, (px, py) in enumerate(chips) for q in range(nt)]
        sends += [cp(3, q, p_refs[q].at[my_chip], bounce_refs[q], sibling) for q in range(nt)]
        for s_ in sends:
            s_.start()
        backs = []
        for q in range(nt):
            cp(3, q, p_refs[q].at[my_chip], bounce_refs[q], sibling).wait_recv()
            backs.append(cp(4, q, bounce_refs[q], out_refs[q].at[my_chip], sibling))
            backs[-1].start()
        for k, (px, py) in enumerate(chips):
            for q in range(nt):
                cp(k, q, p_refs[q].at[my_chip], out_refs[q].at[2 * px + py], (px, py, c)).wait_recv()
        for q in range(nt):
            cp(4, q, bounce_refs[q], out_refs[q].at[my_chip], sibling).wait_recv()
        for s_ in sends + backs:
            s_.wait_send()

    outs = pl.pallas_call(
        body, name=name, in_specs=[ANY] * nt, out_specs=[ANY] * (2 * nt),
        out_shape=[jax.ShapeDtypeStruct(p.shape, p.dtype) for p in ps]
        + [jax.ShapeDtypeStruct(p.shape[1:], p.dtype) for p in ps],
        scratch_shapes=[pltpu.SemaphoreType.DMA((5 * nt,)), pltpu.SemaphoreType.DMA((5 * nt,))],
    )(*ps)
    return outs[:nt]


def _row_tile(m):
    return m if m <= 384 else 128


def _add2_rows(name, a, b):
    n4, m, n = a.shape
    tm = _row_tile(m)
    out = _rows(name, lambda p, q: ((p + q,), ()), n4 * m, tm,
                [_rb(a.reshape(n4 * m, n), tm, n), _rb(b.reshape(n4 * m, n), tm, n)], [(n, BF16)])[0]
    return out.reshape(n4, m, n)


def _add4_rows(name, p):
    _, m, n = p.shape
    tm = _row_tile(m)
    nb = m // tm
    flat = p.reshape(4 * m, n)
    ins = [(flat, (tm, n), (lambda i, j=j: (j * nb + i, 0))) for j in range(4)]
    f32 = lambda v: v.astype(F32)
    return _rows(name, lambda a, b, c, d: ((((f32(a) + f32(b)) + f32(c)) + f32(d),), ()), m, tm, ins, [(n, F32)])[0]


def _in_proj_with_gather(n1, w_main, blocks):
    m, k = n1.shape
    n = w_main.shape[1]
    tm, tn = min(1024, m), 1024
    gi, gj = m // tm, n // tn
    last, mid = gi * gj - 1, (3 * gi * gj) // 4
    nt = len(blocks)

    def body(*refs):
        a_ref, b_ref, x_refs, o_ref, out_refs = refs[0], refs[1], refs[2:2 + nt], refs[2 + nt], refs[3 + nt:3 + 2 * nt]
        send_sems, recv_sems = refs[-2], refs[-1]
        step = pl.program_id(0) * gj + pl.program_id(1)
        x, y, c = _me()
        me, sibling = (x, y, c), (x, y, 1 - c)
        chips = [(1 - x, y), (x, 1 - y), (1 - x, 1 - y)]
        slot = lambda q, p: out_refs[q].at[4 * p[0] + 2 * p[1] + p[2]]

        def copies(kk, blk, to, from_input=False):
            return [_remote(x_refs[q] if from_input else slot(q, blk), slot(q, blk), send_sems, recv_sems, kk * nt + q, to)
                    for q in range(nt)]

        def first():
            out = copies(0, me, sibling, True)
            for j, chip in enumerate(chips):
                out += copies(1 + j, me, (*chip, c), True)
            return out

        @pl.when(step == 0)
        def _():
            for cp in first():
                cp.start()

        @pl.when(step == mid)
        def _():
            for j, chip in enumerate(chips):
                for cp in copies(1 + j, (*chip, c), me):
                    cp.wait_recv()
                for cp in copies(4 + j, (*chip, c), sibling):
                    cp.start()
            for cp in copies(0, sibling, me):
                cp.wait_recv()
            for cp in copies(7, sibling, sibling):
                cp.start()

        o_ref[...] = _dot(a_ref[...], b_ref[...], NN)

        @pl.when(step == last)
        def _():
            for j, chip in enumerate(chips):
                for cp in copies(4 + j, (*chip, 1 - c), me):
                    cp.wait_recv()
            for cp in copies(7, me, me):
                cp.wait_recv()
            sent = first() + copies(7, sibling, sibling)
            for j, chip in enumerate(chips):
                sent += copies(4 + j, (*chip, c), sibling)
            for cp in sent:
                cp.wait_send()

    outs = pl.pallas_call(
        body, name="in_proj", grid=(gi, gj),
        in_specs=[pl.BlockSpec((tm, k), lambda i, j: (i, 0)), pl.BlockSpec((k, tn), lambda i, j: (0, j))] + [ANY] * nt,
        out_specs=[pl.BlockSpec((tm, tn), lambda i, j: (i, j))] + [ANY] * nt,
        out_shape=[jax.ShapeDtypeStruct((m, n), F32)] + [jax.ShapeDtypeStruct((8,) + b.shape, b.dtype) for b in blocks],
        scratch_shapes=[pltpu.SemaphoreType.DMA((8 * nt,)), pltpu.SemaphoreType.DMA((8 * nt,))],
        compiler_params=_cparams(("arbitrary", "arbitrary")),
    )(n1, w_main, *blocks)
    return outs[0], outs[1:]


def _weight_halves(w_in, w_a, w_b, w_out, w_up, w_down, conv_w):
    c = lax.axis_index("c")
    bits = lax.bitcast_convert_type(conv_w, BF16).reshape(-1)
    extra = jnp.zeros(((WD_EXT_ROWS - W_DOWN_SHARD) * D,), BF16).at[:bits.shape[0]].set(bits)
    wd_ext = jnp.concatenate([w_down.astype(BF16), extra.reshape(-1, D)], axis=0)
    shards = [w_in.astype(BF16), w_a.astype(BF16), w_b.astype(BF16), w_out.astype(BF16), w_up.astype(BF16), wd_ext]
    return [lax.dynamic_slice_in_dim(t, c * (t.shape[0] // 2), t.shape[0] // 2, axis=0) for t in shards]


def _unpack_w_in(gathered):
    wi = gathered.reshape(N_CHIP, D, W_IN_SHARD).transpose(1, 0, 2).reshape(D, N_CHIP * W_IN_SHARD)
    w_main = jnp.concatenate([wi[:, :FF_COL], wi[:, FF_COL + FOX_H:]], axis=1)
    return w_main, jnp.pad(wi[:, FF_COL:FF_COL + FOX_H], ((0, 0), (0, LANES - FOX_H)))


def _unpack_later_weights(gathered):
    full = [g.reshape((N_CHIP, 2 * g.shape[1]) + g.shape[2:]) for g in gathered]
    wa, wb, wo = (full[i].reshape(D, D) for i in (0, 1, 2))
    wu = full[3].transpose(1, 0, 2).reshape(D, 2 * D_FF)
    wd = full[4][:, :W_DOWN_SHARD].reshape(D_FF, D)
    n_bits = 3 * W_UP_SHARD * 2
    cw_bits = full[4][:, W_DOWN_SHARD:].reshape(N_CHIP, -1)[:, :n_bits].reshape(N_CHIP, 3, W_UP_SHARD, 2)
    cw = lax.bitcast_convert_type(cw_bits, F32).transpose(1, 0, 2).reshape(3, 2 * D_FF)
    return wa, wb, wo, wu, wd, cw


def _chip_sums(tag, per_chip):
    c = lax.axis_index("c")
    gs = [t.reshape(N_CHIP, 2, t.shape[1] // 2, t.shape[2]) for t in per_chip]
    got = _swap_halves_multi("grad_swap_halves_" + tag, gs)
    return [_add2_rows("grad_chip_sum_%s%d" % (tag, q), lax.dynamic_index_in_dim(g, c, axis=1, keepdims=False), s_)
            for q, (g, s_) in enumerate(zip(gs, got))]


def _late_weight_chip_sums(d_a, d_b, d_o, d_u, d_d):
    return _chip_sums("late", [d_a.reshape(N_CHIP, -1, D), d_b.reshape(N_CHIP, -1, D), d_o.reshape(N_CHIP, -1, D),
                               d_u.reshape(D, N_CHIP, W_UP_SHARD).transpose(1, 0, 2), d_d.reshape(N_CHIP, -1, D)])


def _finish_grads(d_main, d_ff, late_pieces):
    c = lax.axis_index("c")
    d_in = jnp.concatenate(d_main[:7] + [d_ff[:, :FOX_H]] + d_main[7:], axis=1)
    sums = _chip_sums("w_in", [d_in.reshape(D, N_CHIP, W_IN_SHARD).transpose(1, 0, 2)])
    pieces = list(_chip_exchange_multi("grad_chip_exchange", sums)) + list(late_pieces)
    mine = [_add4_rows("grad_sum_chips_%d" % q, p) for q, p in enumerate(pieces)]
    other = _swap_sibling_multi("grad_share_half", mine)
    return [jnp.concatenate([jnp.where(c == 0, a, b), jnp.where(c == 0, b, a)], axis=0) for a, b in zip(mine, other)]


def _local_step(x, target, norm_mix, fox_f_bias, hg_lb_logits, hg_norm, norm_ffn, conv_b, norm_final,
                w_main, w_ff, later):
    bias = jnp.pad(fox_f_bias, ((0, 0), (0, LANES - FOX_H)))

    n1, n1t = _rms_fwd("norm_mix_fwd", x, norm_mix)
    if len(later) == 5:
        proj, gathered = _in_proj_with_gather(n1, w_main, later)
        wa, wb, wo, wu, wd, conv_w = _unpack_later_weights(gathered)
    else:
        proj = _mm("in_proj", n1, w_main, "nn", F32, 1024, 1024, D)
        wa, wb, wo, wu, wd, conv_w = later
    conv_w8 = jnp.pad(conv_w, ((0, 5), (0, 0)))
    pff = _mm("in_proj_forget", n1, w_ff, "nn", F32, 1024, LANES, D)
    qb, kb, vb, ka = _fox_prep(proj, pff, bias)
    o_b, laux = _fox_fwd(qb, kb, vb, ka)
    o_raw, states = _hg_fwd(proj, hg_lb_logits)
    o_a = _hg_post_fwd(o_raw, proj, hg_norm)
    pa = _mm("branch_a", o_a, wa, "nn", F32, 1024, 1024, D)
    pb = _mm("branch_b", o_b, wb, "nn", F32, 1024, 1024, D)
    merged = _merge_fwd(pa, pb, proj)
    h1 = _mm("out_proj", merged, wo, "nn", F32, 1024, 1024, D, res=x)
    n2, n2t = _rms_fwd("norm_ffn_fwd", h1, norm_ffn)
    u = _mm("ffn_up", n2, wu, "nn", F32, 1024, W_UP_SHARD, D)
    act, gelu_gate, dact_dgate = _convglu_fwd(u, conv_w8, conv_b)
    h2 = _mm("ffn_down", act, wd, "nn", F32, 512, 1024, D_FF, res=h1)
    (dh2,), (d_norm_final, loss_row) = _final(h2, target, norm_final)

    dact = _mm("ffn_down_dx", dh2, wd, "nt", BF16, 1024, D_FF, D)
    d_wd = _mm("ffn_down_dw", act, dh2, "tn", F32, D_FF // 2, 1024, DW_TK // 2)
    (du,), (d_conv_w8, d_conv_b) = _convglu_bwd(u, dact, gelu_gate, dact_dgate, conv_w8)
    dn2 = _mm("ffn_up_dx", du, wu, "nt", F32, 1024, 1024, W_UP_SHARD)
    d_wu = _mm("ffn_up_dw", n2t, du, "nn", F32, 1024, W_UP_SHARD, DW_TK)
    (dh1,), (d_norm_ffn,) = _rms_bwd("norm_ffn_bwd", h1, norm_ffn, [dn2], dh2)

    dmerged = _mm("out_proj_dx", dh1, wo, "nt", F32, 1024, 1024, D)
    d_wo = _mm("out_proj_dw", merged, dh1, "tn", F32, 1024, 1024, DW_TK)
    dpa, dpb, dga, dgb = _merge_bwd(dmerged, pa, pb, proj)
    do_a = _mm("branch_a_dx", dpa, wa, "nt", F32, 1024, 1024, D)
    do_b = _mm("branch_b_dx", dpb, wb, "nt", BF16, 1024, 1024, D)
    d_wa = _mm("branch_a_dw", o_a, dpa, "tn", F32, 1024, 1024, DW_TK)
    d_wb = _mm("branch_b_dw", o_b, dpb, "tn", F32, 1024, 1024, DW_TK)

    (do_raw, dhg), (d_hg_norm,) = _hg_post_bwd(do_a, o_raw, proj, hg_norm)
    dhq, dhf, dhi, d_lb_logits = _hg_bwd(proj, hg_lb_logits, states, do_raw)

    dfq, dfk, dfv, dcrow = _fox_bwd(qb, kb, vb, ka, o_b, laux, do_b)
    dct = jnp.pad(dcrow.reshape(FOX_H, x.shape[0]), ((0, LANES - FOX_H), (0, 0)))
    dff, d_bias = _fox_gate_bwd(dct, pff, bias)

    pieces = [dhq, dhf, dhi, dhg, dfq, dfk, dfv, dga, dgb]
    if len(later) == 5:
        dn1, late = _mm_sum_nt("in_proj_dx", pieces, w_main, (dff, w_ff), 1024, 1024,
                               exchange=_late_weight_chip_sums(d_wa, d_wb, d_wo, d_wu, d_wd))
    else:
        dn1, late = _mm_sum_nt("in_proj_dx", pieces, w_main, (dff, w_ff), 1024, 1024), (d_wa, d_wb, d_wo, d_wu, d_wd)
    d_w_main = [_mm("in_proj_dw_%d" % i, n1t, p, "nn", F32, 1024, 1024, DW_TK) for i, p in enumerate(pieces)]
    d_w_ff = _mm("in_proj_forget_dw", n1t, dff, "nn", F32, 1024, LANES, DW_TK)
    (dx,), (d_norm_mix,) = _rms_bwd("norm_mix_bwd", x, norm_mix, [dn1], dh1)

    small = dict(norm_mix=d_norm_mix, fox_f_bias=d_bias[:, :FOX_H], hg_lb_logits=d_lb_logits, hg_norm=d_hg_norm,
                 norm_ffn=d_norm_ffn, conv_b=d_conv_b, norm_final=d_norm_final, conv_w=d_conv_w8[:3], loss=loss_row)
    big = (d_w_main, d_w_ff) + tuple(late)
    return dx, small, big


SMALL_KEYS = ("norm_mix", "fox_f_bias", "hg_lb_logits", "hg_norm", "norm_ffn", "conv_b", "norm_final")


def _pack_small(parts):
    rows, layout = [], []
    for key, arr in parts:
        flat = arr.reshape(-1)
        n = flat.shape[0]
        nr = -(-n // LANES)
        rows.append(jnp.pad(flat, (0, nr * LANES - n)).reshape(nr, LANES))
        layout.append((key, arr.shape, n, nr))
    packed = jnp.concatenate(rows, axis=0)
    pad = -packed.shape[0] % 8
    return jnp.pad(packed, ((0, pad), (0, 0))), layout


def _unpack_small(packed, layout):
    out, r0 = {}, 0
    for key, shape, n, nr in layout:
        out[key] = packed[r0:r0 + nr].reshape(-1)[:n].reshape(shape)
        r0 += nr
    return out


def kernel(x, norm_mix, w_in, fox_f_bias, hg_lb_logits, hg_norm, w_branch_a, w_branch_b, w_out, norm_ffn, w_up, conv_w, conv_b, w_down, norm_final, loss_target, m_norm_mix, m_w_in, m_fox_f_bias, m_hg_lb_logits, m_hg_norm, m_w_branch_a, m_w_branch_b, m_w_out, m_norm_ffn, m_w_up, m_conv_w, m_conv_b, m_w_down, m_norm_final, v_norm_mix, v_w_in, v_fox_f_bias, v_hg_lb_logits, v_hg_norm, v_w_branch_a, v_w_branch_b, v_w_out, v_norm_ffn, v_w_up, v_conv_w, v_conv_b, v_w_down, v_norm_final):
    chip = 2 * lax.axis_index("x") + lax.axis_index("y")
    halves = _weight_halves(w_in[0], w_branch_a[0], w_branch_b[0], w_out[0], w_up[0], w_down[0], conv_w[0])
    w_main, w_ff = _unpack_w_in(_all_gather8_multi("all_gather_w_in", halves[:1])[0])
    dx, small, big = _local_step(
        x[0], loss_target[0], norm_mix, fox_f_bias, hg_lb_logits, hg_norm, norm_ffn, conv_b,
        norm_final.reshape(1, D), w_main, w_ff, halves[1:])

    packed, layout = _pack_small([(k, small[k]) for k in SMALL_KEYS + ("conv_w", "loss")])
    red = _unpack_small(_all_reduce_small("all_reduce_small", packed), layout)
    loss = red["loss"][0, 0]
    g_conv_w = lax.dynamic_slice_in_dim(red["conv_w"], chip * W_UP_SHARD, W_UP_SHARD, axis=1)

    g_big = _finish_grads(big[0], big[1], big[2:])

    names = ["norm_mix", "w_in", "fox_f_bias", "hg_lb_logits", "hg_norm", "w_branch_a", "w_branch_b", "w_out",
             "norm_ffn", "w_up", "conv_w", "conv_b", "w_down", "norm_final"]
    weights = dict(norm_mix=norm_mix, w_in=w_in, fox_f_bias=fox_f_bias, hg_lb_logits=hg_lb_logits, hg_norm=hg_norm,
                   w_branch_a=w_branch_a, w_branch_b=w_branch_b, w_out=w_out, norm_ffn=norm_ffn, w_up=w_up,
                   conv_w=conv_w, conv_b=conv_b, w_down=w_down, norm_final=norm_final)
    ms = dict(norm_mix=m_norm_mix, w_in=m_w_in, fox_f_bias=m_fox_f_bias, hg_lb_logits=m_hg_lb_logits,
              hg_norm=m_hg_norm, w_branch_a=m_w_branch_a, w_branch_b=m_w_branch_b, w_out=m_w_out,
              norm_ffn=m_norm_ffn, w_up=m_w_up, conv_w=m_conv_w, conv_b=m_conv_b, w_down=m_w_down,
              norm_final=m_norm_final)
    vs = dict(norm_mix=v_norm_mix, w_in=v_w_in, fox_f_bias=v_fox_f_bias, hg_lb_logits=v_hg_lb_logits,
              hg_norm=v_hg_norm, w_branch_a=v_w_branch_a, w_branch_b=v_w_branch_b, w_out=v_w_out,
              norm_ffn=v_norm_ffn, w_up=v_w_up, conv_w=v_conv_w, conv_b=v_conv_b, w_down=v_w_down,
              norm_final=v_norm_final)

    grads, deltas, new_m, new_v = {}, {}, {}, {}
    big_names = ["w_in", "w_branch_a", "w_branch_b", "w_out", "w_up", "w_down"]
    for name, g2 in zip(big_names, g_big):
        shape = weights[name].shape
        rows = g2.shape[0]
        d_, m_, v_ = _adamw("adamw_" + name, weights[name][0], g2, ms[name][0], vs[name][0], tm=rows // 8)
        grads[name], deltas[name], new_m[name], new_v[name] = (a.reshape(shape) for a in (g2, d_, m_, v_))
    shape = conv_w.shape
    d_, m_, v_ = _adamw("adamw_conv_w", conv_w[0], g_conv_w, m_conv_w[0], v_conv_w[0])
    grads["conv_w"], deltas["conv_w"], new_m["conv_w"], new_v["conv_w"] = (
        a.reshape(shape) for a in (g_conv_w, d_, m_, v_))
    gs = {k: red[k].reshape(weights[k].shape) for k in SMALL_KEYS}
    pw, lay = _pack_small([(k, weights[k]) for k in SMALL_KEYS])
    pg, _ = _pack_small([(k, gs[k]) for k in SMALL_KEYS])
    pm, _ = _pack_small([(k, ms[k]) for k in SMALL_KEYS])
    pv, _ = _pack_small([(k, vs[k]) for k in SMALL_KEYS])
    d_, m_, v_ = (_unpack_small(a, lay) for a in _adamw("adamw_small", pw, pg, pm, pv))
    for k in SMALL_KEYS:
        grads[k], deltas[k], new_m[k], new_v[k] = gs[k], d_[k], m_[k], v_[k]

    return (loss, dx[None], *[grads[n] for n in names], *[deltas[n] for n in names],
            *[new_m[n] for n in names], *[new_v[n] for n in names])
```

```python
import jax
import jax.numpy as jnp
from jax import lax
from jax.experimental import pallas as pl
from jax.experimental.pallas import tpu as pltpu

F32 = jnp.float32
BF16 = jnp.bfloat16

D = 1024
HG_H, HG_DK = 8, 128
FOX_H, FOX_D = 16, 64
D_FF = 2816
EPS = 1e-6
N_CHIP = 4
LANES = 128
W_IN_SHARD = 2308
W_UP_SHARD = 1408
W_DOWN_SHARD = 704
FF_COL = 7168
ADAM_LR, ADAM_B1, ADAM_B2, ADAM_EPS, ADAM_WD, ADAM_STEP = 0.001, 0.9, 0.999, 1e-08, 0.01, 10

HG_C = 16
HG_T = 512
HG_UNROLL = 16
HG_UNROLL_BWD = 4
FOX_T = 1024
DW_TK = 2048
VMEM_LIMIT = 56 * 1024 * 1024
MESH = pl.DeviceIdType.MESH
ANY = pl.BlockSpec(memory_space=pl.ANY)


def _cparams(sem):
    return pltpu.CompilerParams(dimension_semantics=sem, vmem_limit_bytes=VMEM_LIMIT)


def _sigmoid(x):
    return 1.0 / (1.0 + jnp.exp(-x))


def _dot(a, b, dims):
    return lax.dot_general(a.astype(BF16), b.astype(BF16), (dims, ((), ())), preferred_element_type=F32)


NN = ((1,), (0,))
NT = ((1,), (1,))
TN = ((0,), (0,))


def _split_dot(tri, x, parts, dims=NN):
    acc = None
    r = x
    for _ in range(parts):
        p = r.astype(BF16)
        t = lax.dot_general(tri, p, (dims, ((), ())), preferred_element_type=F32)
        acc = t if acc is None else acc + t
        r = r - p.astype(F32)
    return acc


def _rb(arr, tm, width, cb=0):
    return (arr, (tm, width), lambda i: (i, cb))


def _cst(arr):
    return (arr, arr.shape, lambda i: (0,) * arr.ndim)


def _rows(name, fn, n_rows, tm, ins, outs, accs=()):
    n_in, n_out, n_acc = len(ins), len(outs), len(accs)
    nb = n_rows // tm

    def body(*refs):
        vals = [r[...] for r in refs[:n_in]]
        o, a = fn(*vals)
        for r, v in zip(refs[n_in:n_in + n_out], o):
            r[...] = v.astype(r.dtype)
        if n_acc:
            acc_refs = refs[n_in + n_out:]

            @pl.when(pl.program_id(0) == 0)
            def _():
                for r in acc_refs:
                    r[...] = jnp.zeros_like(r)

            for r, v in zip(acc_refs, a):
                r[...] += v

    in_specs = [pl.BlockSpec(bs, im) for (_, bs, im) in ins]
    out_specs = [pl.BlockSpec((tm, w), lambda i: (i, 0)) for (w, _) in outs]
    out_specs += [pl.BlockSpec((r, w), lambda i: (0, 0)) for (r, w) in accs]
    out_shape = [jax.ShapeDtypeStruct((n_rows, w), dt) for (w, dt) in outs]
    out_shape += [jax.ShapeDtypeStruct((r, w), F32) for (r, w) in accs]
    res = pl.pallas_call(
        body, name=name, grid=(nb,), in_specs=in_specs, out_specs=out_specs, out_shape=out_shape,
        compiler_params=_cparams(("arbitrary",)),
    )(*[a for a, _, _ in ins])
    return (res[:n_out], res[n_out:]) if n_acc else res


def _mm(name, a, b, mode, out_dtype, tm, tn, tk, res=None):
    if mode == "nn":
        (m, k), n = a.shape, b.shape[1]
    elif mode == "nt":
        (m, k), n = a.shape, b.shape[0]
    else:
        (k, m), n = a.shape, b.shape[1]
    tm, tn, tk = min(tm, m), min(tn, n), min(tk, k)
    assert m % tm == 0 and n % tn == 0 and k % tk == 0, (name, m, n, k, tm, tn, tk)
    if mode == "nn":
        a_spec = pl.BlockSpec((tm, tk), lambda i, j, kk: (i, kk))
        b_spec = pl.BlockSpec((tk, tn), lambda i, j, kk: (kk, j))
        dims = NN
    elif mode == "nt":
        a_spec = pl.BlockSpec((tm, tk), lambda i, j, kk: (i, kk))
        b_spec = pl.BlockSpec((tn, tk), lambda i, j, kk: (j, kk))
        dims = NT
    else:
        a_spec = pl.BlockSpec((tk, tm), lambda i, j, kk: (kk, i))
        b_spec = pl.BlockSpec((tk, tn), lambda i, j, kk: (kk, j))
        dims = TN
    nk = k // tk
    has_res = res is not None
    acc_in_out = out_dtype == F32 and not has_res

    def body(*refs):
        a_ref, b_ref = refs[0], refs[1]
        r_ref = refs[2] if has_res else None
        o_ref = refs[3] if has_res else refs[2]
        part = _dot(a_ref[...], b_ref[...], dims)

        def finish(val):
            if has_res:
                val = val + r_ref[...]
            o_ref[...] = val.astype(o_ref.dtype)

        if nk == 1:
            finish(part)
        elif acc_in_out:
            kk = pl.program_id(2)

            @pl.when(kk == 0)
            def _():
                o_ref[...] = part

            @pl.when(kk > 0)
            def _():
                o_ref[...] += part
        else:
            acc_ref = refs[-1]
            kk = pl.program_id(2)

            @pl.when(kk == 0)
            def _():
                acc_ref[...] = part

            @pl.when(kk > 0)
            def _():
                acc_ref[...] += part

            @pl.when(kk == nk - 1)
            def _():
                finish(acc_ref[...])

    in_specs = [a_spec, b_spec]
    args = [a, b]
    if has_res:
        in_specs.append(pl.BlockSpec((tm, tn), lambda i, j, kk: (i, j)))
        args.append(res)
    return pl.pallas_call(
        body, name=name, grid=(m // tm, n // tn, nk), in_specs=in_specs,
        out_specs=pl.BlockSpec((tm, tn), lambda i, j, kk: (i, j)),
        out_shape=jax.ShapeDtypeStruct((m, n), out_dtype),
        scratch_shapes=[pltpu.VMEM((tm, tn), F32)] if nk > 1 and not acc_in_out else [],
        compiler_params=_cparams(("arbitrary", "arbitrary", "arbitrary")),
    )(*args)


def _mm_sum_nt(name, pieces, w, extra, tm, tn, exchange=()):
    n_p = len(pieces)
    m, k = pieces[0].shape
    n = w.shape[0]
    xa, xb = extra
    ke = xa.shape[1]
    tm, tn = min(tm, m), min(tn, n)
    nx = len(exchange)
    n_steps = (m // tm) * (n // tn) * (n_p + 1)

    def exchange_steps(refs):
        e_refs = refs[n_p + 3:n_p + 3 + nx]
        out_refs, bounce_refs = refs[n_p + 4 + nx:n_p + 4 + 2 * nx], refs[n_p + 4 + 2 * nx:n_p + 4 + 3 * nx]
        send_sems, recv_sems = refs[-2], refs[-1]
        step = (pl.program_id(0) * (n // tn) + pl.program_id(1)) * (n_p + 1) + pl.program_id(2)
        x, y, c = _me()
        my_chip = 2 * x + y
        sibling = (x, y, 1 - c)
        chips = [(1 - x, y), (x, 1 - y), (1 - x, 1 - y)]
        cp = lambda kx, q, src, dst, to: _remote(src, dst, send_sems, recv_sems, kx * nx + q, to)
        sends = lambda: ([cp(kx, q, e_refs[q].at[2 * px + py], out_refs[q].at[my_chip], (px, py, c))
                          for kx, (px, py) in enumerate(chips) for q in range(nx)]
                         + [cp(3, q, e_refs[q].at[my_chip], bounce_refs[q], sibling) for q in range(nx)])
        backs = lambda: [cp(4, q, bounce_refs[q], out_refs[q].at[my_chip], sibling) for q in range(nx)]

        @pl.when(step == 0)
        def _():
            for s_ in sends():
                s_.start()

        @pl.when(step == n_steps // 2)
        def _():
            for q in range(nx):
                cp(3, q, e_refs[q].at[my_chip], bounce_refs[q], sibling).wait_recv()
            for s_ in backs():
                s_.start()

        @pl.when(step == n_steps - 1)
        def _():
            for kx, (px, py) in enumerate(chips):
                for q in range(nx):
                    cp(kx, q, e_refs[q].at[my_chip], out_refs[q].at[2 * px + py], (px, py, c)).wait_recv()
            for q in range(nx):
                cp(4, q, bounce_refs[q], out_refs[q].at[my_chip], sibling).wait_recv()
            for s_ in sends() + backs():
                s_.wait_send()

    def body(*refs):
        p_refs, w_ref, xa_ref, xb_ref, o_ref = refs[:n_p], refs[n_p], refs[n_p + 1], refs[n_p + 2], refs[n_p + 3 + nx]
        if nx:
            exchange_steps(refs)
        kk = pl.program_id(2)

        @pl.when(kk == 0)
        def _():
            o_ref[...] = _dot(p_refs[0][...], w_ref[...], NT)

        for i in range(1, n_p):
            @pl.when(kk == i)
            def _(i=i):
                o_ref[...] += _dot(p_refs[i][...], w_ref[...], NT)

        @pl.when(kk == n_p)
        def _():
            o_ref[...] += _dot(xa_ref[...], xb_ref[...], NT)

    in_specs = [pl.BlockSpec((tm, k), lambda i, j, kk: (i, 0)) for _ in range(n_p)]
    in_specs.append(pl.BlockSpec((tn, k), lambda i, j, kk: (j, jnp.minimum(kk, n_p - 1))))
    in_specs += [pl.BlockSpec((tm, ke), lambda i, j, kk: (i, 0)), pl.BlockSpec((tn, ke), lambda i, j, kk: (j, 0))]
    outs = pl.pallas_call(
        body, name=name, grid=(m // tm, n // tn, n_p + 1), in_specs=in_specs + [ANY] * nx,
        out_specs=[pl.BlockSpec((tm, tn), lambda i, j, kk: (i, j))] + [ANY] * (2 * nx),
        out_shape=[jax.ShapeDtypeStruct((m, n), F32)] + [jax.ShapeDtypeStruct(e.shape, e.dtype) for e in exchange]
        + [jax.ShapeDtypeStruct(e.shape[1:], e.dtype) for e in exchange],
        scratch_shapes=[pltpu.SemaphoreType.DMA((5 * nx,)), pltpu.SemaphoreType.DMA((5 * nx,))] if nx else [],
        compiler_params=_cparams(("arbitrary", "arbitrary", "arbitrary")),
    )(*pieces, w, xa, xb, *exchange)
    return (outs[0], outs[1:1 + nx]) if nx else outs[0]


def _rms_fwd(name, x, gain, tm=256):
    s = x.shape[0]

    def body(x_ref, g_ref, y_ref, yt_ref):
        xb = x_ref[...]
        y = xb * lax.rsqrt(jnp.mean(xb * xb, axis=-1, keepdims=True) + EPS) * g_ref[...]
        y_ref[...] = y.astype(BF16)
        yt_ref[...] = y.T.astype(BF16)

    return pl.pallas_call(
        body, name=name, grid=(s // tm,),
        in_specs=[pl.BlockSpec((tm, D), lambda i: (i, 0)), pl.BlockSpec((1, D), lambda i: (0, 0))],
        out_specs=[pl.BlockSpec((tm, D), lambda i: (i, 0)), pl.BlockSpec((D, tm), lambda i: (0, i))],
        out_shape=[jax.ShapeDtypeStruct((s, D), BF16), jax.ShapeDtypeStruct((D, s), BF16)],
        compiler_params=_cparams(("arbitrary",)),
    )(x, gain)


def _rms_bwd(name, x, gain, dns, dres, tm=256):
    s = x.shape[0]
    n_dn = len(dns)

    def fn(xb, g, *rest):
        dn = rest[0]
        for t in rest[1:n_dn]:
            dn = dn + t
        r = lax.rsqrt(jnp.mean(xb * xb, axis=-1, keepdims=True) + EPS)
        xhat = xb * r
        dxh = dn * g
        dx = r * (dxh - xhat * jnp.mean(dxh * xhat, axis=-1, keepdims=True)) + rest[n_dn]
        return (dx,), (jnp.sum(dn * xhat, axis=0, keepdims=True),)

    ins = [_rb(x, tm, D), _cst(gain)] + [_rb(t, tm, D) for t in dns] + [_rb(dres, tm, D)]
    return _rows(name, fn, s, tm, ins, [(D, F32)], [(1, D)])


def _final(h2, target, gain, tm=256):
    s = h2.shape[0]

    def fn(hb, tb, g):
        r = lax.rsqrt(jnp.mean(hb * hb, axis=-1, keepdims=True) + EPS)
        xhat = hb * r
        e = xhat * g - tb
        dy = e * (1.0 / D)
        dxh = dy * g
        dh = r * (dxh - xhat * jnp.mean(dxh * xhat, axis=-1, keepdims=True))
        lrow = 0.5 * jnp.sum(jnp.sum(e * e, axis=-1, keepdims=True) * (1.0 / D), axis=0, keepdims=True)
        return (dh,), (jnp.sum(dy * xhat, axis=0, keepdims=True), jnp.broadcast_to(lrow, (1, LANES)))

    return _rows("final_norm_loss", fn, s, tm, [_rb(h2, tm, D), _rb(target, tm, D), _cst(gain)],
                 [(D, F32)], [(1, D), (1, LANES)])


def _merge_fwd(pa, pb, proj, tm=256):
    s = pa.shape[0]

    def fn(a, b, ga, gb):
        return (_sigmoid(ga) * a + _sigmoid(gb) * b,), ()

    ins = [_rb(pa, tm, D), _rb(pb, tm, D), _rb(proj, tm, D, 7), _rb(proj, tm, D, 8)]
    return _rows("merge_fwd", fn, s, tm, ins, [(D, BF16)])[0]


def _merge_bwd(dmerged, pa, pb, proj, tm=256):
    s = pa.shape[0]

    def fn(dm, a, b, ga, gb):
        sa, sb = _sigmoid(ga), _sigmoid(gb)
        return (dm * sa, dm * sb, dm * a * sa * (1.0 - sa), dm * b * sb * (1.0 - sb)), ()

    ins = [_rb(dmerged, tm, D), _rb(pa, tm, D), _rb(pb, tm, D), _rb(proj, tm, D, 7), _rb(proj, tm, D, 8)]
    return _rows("merge_bwd", fn, s, tm, ins, [(D, BF16), (D, BF16), (D, BF16), (D, BF16)])


def _gelu_parts(x):
    cdf = 0.5 * (1.0 + lax.erf(x * 0.7071067811865476))
    pdf = 0.3989422804014327 * jnp.exp(-0.5 * x * x)
    return x * cdf, cdf + x * pdf


def _conv_taps(u_ext, n_out):
    cur = u_ext[8:8 + n_out]
    m1 = pltpu.roll(u_ext, 1, 0)[8:8 + n_out]
    m2 = pltpu.roll(u_ext, 2, 0)[8:8 + n_out]
    return m2, m1, cur


def _convglu_fwd(u, conv_w8, conv_b, tm=256):
    s, w = u.shape
    tb = tm // 8

    def fn(ub, up, cw, cb):
        i = pl.program_id(0)
        up = jnp.where(i == 0, 0.0, up)
        m2, m1, cur = _conv_taps(jnp.concatenate([up, ub], axis=0), tm)
        acc = cb + cw[0:1] * m2 + cw[1:2] * m1 + cw[2:3] * cur
        gl, dgl = _gelu_parts(acc[:, :D_FF])
        val = acc[:, D_FF:]
        return (gl * val, gl, val * dgl), ()

    ins = [_rb(u, tm, w), (u, (8, w), lambda i: (jnp.maximum(i * tb - 1, 0), 0)), _cst(conv_w8), _cst(conv_b)]
    return _rows("convglu_fwd", fn, s, tm, ins, [(D_FF, BF16)] * 3)


def _convglu_bwd(u, dact, gl, gd, conv_w8, tm=256):
    s, w = u.shape
    tb = tm // 8
    nb = s // tm

    def fn(ub, up, db, dn, glb, gln, gdb, gdn, cw):
        i = pl.program_id(0)
        up = jnp.where(i == 0, 0.0, up)
        dn = jnp.where(i == nb - 1, 0.0, dn.astype(F32))
        ne = tm + 8
        m2, m1, cur = _conv_taps(jnp.concatenate([up, ub], axis=0), tm)
        ext = lambda blk, nxt: jnp.concatenate([blk.astype(F32), nxt.astype(F32)], axis=0)
        de = ext(db, dn)
        dacc = jnp.concatenate([de * ext(gdb, gdn), de * ext(glb, gln)], axis=1)
        p1 = pltpu.roll(dacc, ne - 1, 0)[:tm]
        p2 = pltpu.roll(dacc, ne - 2, 0)[:tm]
        d0 = dacc[:tm]
        du = cw[2:3] * d0 + cw[1:2] * p1 + cw[0:1] * p2
        zero5 = jnp.zeros((5, w), F32)
        dcw = jnp.concatenate([
            jnp.sum(d0 * m2, axis=0, keepdims=True), jnp.sum(d0 * m1, axis=0, keepdims=True),
            jnp.sum(d0 * cur, axis=0, keepdims=True), zero5], axis=0)
        return (du,), (dcw, jnp.sum(d0, axis=0, keepdims=True))

    nxt = lambda arr: (arr, (8, D_FF), lambda i: (jnp.minimum((i + 1) * tb, s // 8 - 1), 0))
    ins = [_rb(u, tm, w), (u, (8, w), lambda i: (jnp.maximum(i * tb - 1, 0), 0)),
           _rb(dact, tm, D_FF), nxt(dact), _rb(gl, tm, D_FF), nxt(gl), _rb(gd, tm, D_FF), nxt(gd), _cst(conv_w8)]
    return _rows("convglu_bwd", fn, s, tm, ins, [(w, BF16)], [(8, w), (1, w)])


def _chunk_scan(x, t_iota, reverse):
    k = 1
    while k < HG_C:
        if reverse:
            x = x + jnp.where(t_iota < HG_C - k, pltpu.roll(x, HG_C - k, 0), 0.0)
        else:
            x = x + jnp.where(t_iota >= k, pltpu.roll(x, k, 0), 0.0)
        k *= 2
    return x


def _hg_gates(hq, hf, lb):
    sq = _sigmoid(hq)
    q = hq * sq
    sg = _sigmoid(hf)
    f = lb + (1.0 - lb) * sg
    return q, sq, sg, f, 1.0 - f, jnp.log(f)


def _lb_of(logits):
    l0, l1 = logits[0:1], logits[1:2]
    mx = jnp.maximum(l0, l1)
    e0, e1 = jnp.exp(l0 - mx), jnp.exp(l1 - mx)
    return e0 / (e0 + e1)


def _tri(n, lower):
    r = lax.broadcasted_iota(jnp.int32, (n, n), 0)
    c = lax.broadcasted_iota(jnp.int32, (n, n), 1)
    return jnp.where((r >= c) if lower else (r <= c), 1.0, 0.0).astype(BF16)


def _hg_intra_terms(q, kk, b, t_iota):
    ws, ps = [], []
    for s in range(HG_C):
        p = jnp.where(t_iota >= s, jnp.exp(b - b[s:s + 1]), 0.0)
        ps.append(p)
        ws.append(q * kk[s:s + 1] * p)
    return jnp.concatenate(ws, axis=0), ps


def _hg_fwd(proj, lb_logits):
    s = proj.shape[0]
    nt = s // HG_T
    nc = HG_T // HG_C

    def body(q_ref, f_ref, i_ref, l_ref, o_ref, st_ref, state):
        @pl.when(pl.program_id(1) == 0)
        def _():
            state[...] = jnp.zeros_like(state)

        st_ref[0, 0] = state[...]
        lb = _lb_of(l_ref[...])
        ones = jnp.ones((HG_DK, HG_DK), BF16)
        t_iota = lax.broadcasted_iota(jnp.int32, (HG_C, HG_DK), 0)
        cc = HG_C * HG_C

        def group(gi, st):
            units = []
            for u in range(HG_UNROLL):
                r = pl.ds(pl.multiple_of((gi * HG_UNROLL + u) * HG_C, HG_C), HG_C)
                q, _, _, _, kk, g = _hg_gates(q_ref[r, :], f_ref[r, :], lb)
                b = _chunk_scan(g, t_iota, False)
                b_end = b[HG_C - 1:HG_C]
                w_all, _ = _hg_intra_terms(q, kk, b, t_iota)
                units.append((r, i_ref[r, :], q * jnp.exp(b), jnp.exp(b_end), kk * jnp.exp(b_end - b), w_all))
            a_all = _dot(jnp.concatenate([un[5] for un in units], axis=0), ones, NN)
            kvs = [_dot(v, kd, TN) for (_, v, _, _, kd, _) in units]
            sts = [st]
            for (_, _, _, dec, _, _), kv in zip(units, kvs):
                sts.append(sts[-1] * dec + kv)
            for ui, (r, v, qd, _, _, _) in enumerate(units):
                o = _dot(qd, sts[ui], NT)
                for si in range(HG_C):
                    o = o + a_all[ui * cc + si * HG_C:ui * cc + (si + 1) * HG_C] * v[si:si + 1]
                o_ref[r, :] = o
            return sts[-1]

        state[...] = lax.fori_loop(0, nc // HG_UNROLL, group, state[...])

    col = lambda off: pl.BlockSpec((HG_T, HG_DK), lambda h, t: (t, off + h))
    return pl.pallas_call(
        body, name="hgrn2_fwd", grid=(HG_H, nt),
        in_specs=[col(0), col(8), col(16), pl.BlockSpec((2, HG_DK), lambda h, t: (0, h))],
        out_specs=[pl.BlockSpec((HG_T, HG_DK), lambda h, t: (t, h)),
                   pl.BlockSpec((1, 1, HG_DK, HG_DK), lambda h, t: (h, t, 0, 0))],
        out_shape=[jax.ShapeDtypeStruct((s, D), F32), jax.ShapeDtypeStruct((HG_H, nt, HG_DK, HG_DK), F32)],
        scratch_shapes=[pltpu.VMEM((HG_DK, HG_DK), F32)],
        compiler_params=_cparams(("arbitrary", "arbitrary")),
    )(proj, proj, proj, lb_logits)


def _hg_bwd(proj, lb_logits, states, do_raw):
    s = proj.shape[0]
    nt = s // HG_T
    nc = HG_T // HG_C

    def body(q_ref, f_ref, i_ref, l_ref, st_ref, do_ref, dq_ref, df_ref, di_ref, dl_ref, st_all, adj):
        tb = pl.program_id(1)

        @pl.when(tb == 0)
        def _():
            adj[...] = jnp.zeros_like(adj)
            dl_ref[...] = jnp.zeros_like(dl_ref)

        lb = _lb_of(l_ref[...])
        ones = jnp.ones((HG_DK, HG_DK), BF16)
        t_iota = lax.broadcasted_iota(jnp.int32, (HG_C, HG_DK), 0)
        cc = HG_C * HG_C

        def fwd_group(gi, st):
            terms = []
            for u in range(HG_UNROLL):
                ci = gi * HG_UNROLL + u
                r = pl.ds(pl.multiple_of(ci * HG_C, HG_C), HG_C)
                _, _, _, _, kk, g = _hg_gates(q_ref[r, :], f_ref[r, :], lb)
                b = _chunk_scan(g, t_iota, False)
                b_end = b[HG_C - 1:HG_C]
                terms.append((ci, jnp.exp(b_end), _dot(i_ref[r, :], kk * jnp.exp(b_end - b), TN)))
            for ci, dec, kv in terms:
                st_all[ci] = st
                st = st * dec + kv
            return st

        lax.fori_loop(0, nc // HG_UNROLL, fwd_group, st_ref[0, 0])

        def bwd_group(gj, dlb):
            units = []
            for u in range(HG_UNROLL_BWD):
                ci = nc - 1 - (gj * HG_UNROLL_BWD + u)
                r = pl.ds(pl.multiple_of(ci * HG_C, HG_C), HG_C)
                hq, hf, v, do = q_ref[r, :], f_ref[r, :], i_ref[r, :], do_ref[r, :]
                q, sq, sg, f, kk, g = _hg_gates(hq, hf, lb)
                b = _chunk_scan(g, t_iota, False)
                b_end = b[HG_C - 1:HG_C]
                e_b, e_be, dec = jnp.exp(b), jnp.exp(b_end - b), jnp.exp(b_end)
                w_all, ps = _hg_intra_terms(q, kk, b, t_iota)
                x_all = jnp.concatenate([do * v[si:si + 1] for si in range(HG_C)], axis=0)
                units.append(dict(ci=ci, r=r, hq=hq, v=v, do=do, q=q, sq=sq, sg=sg, f=f, kk=kk, e_b=e_b, e_be=e_be,
                                  dec=dec, kd=kk * e_be, w=w_all, ps=ps, x=x_all))
            both = _dot(jnp.concatenate([un["w"] for un in units] + [un["x"] for un in units], axis=0), ones, NN)
            st0s = [st_all[un["ci"]] for un in units]
            st_ends = [st0 * un["dec"] + _dot(un["v"], un["kd"], TN) for un, st0 in zip(units, st0s)]
            dqks = [_dot(un["do"], un["q"] * un["e_b"], TN) for un in units]
            es = [adj[...]]
            for un, dqk in zip(units, dqks):
                es.append(es[-1] * un["dec"] + dqk)
            adj[...] = es[-1]
            for ui, un in enumerate(units):
                e, q, kk, v, do = es[ui], un["q"], un["kk"], un["v"], un["do"]
                tail = jnp.sum(e * st_ends[ui], axis=0, keepdims=True)
                dq = un["e_b"] * _dot(do, st0s[ui], NN)
                dk = un["e_be"] * _dot(v, e, NN)
                dv = _dot(un["kd"], e, NT)
                a0 = ui * cc
                d0 = (HG_UNROLL_BWD + ui) * cc
                for si in range(HG_C):
                    da = both[d0 + si * HG_C:d0 + (si + 1) * HG_C]
                    aa = both[a0 + si * HG_C:a0 + (si + 1) * HG_C]
                    dap = da * un["ps"][si]
                    dq = dq + dap * kk[si:si + 1]
                    hit = t_iota == si
                    dk = dk + jnp.where(hit, jnp.sum(dap * q, axis=0, keepdims=True), 0.0)
                    dv = dv + jnp.where(hit, jnp.sum(aa * do, axis=0, keepdims=True), 0.0)
                dg = _chunk_scan(q * dq - kk * dk, t_iota, True) + tail
                dfg = dg / un["f"] - dk
                sq, sg, hq, r = un["sq"], un["sg"], un["hq"], un["r"]
                dq_ref[r, :] = (dq * sq * (1.0 + hq * (1.0 - sq))).astype(dq_ref.dtype)
                df_ref[r, :] = (dfg * (1.0 - lb) * sg * (1.0 - sg)).astype(df_ref.dtype)
                di_ref[r, :] = dv.astype(di_ref.dtype)
                dlb = dlb + jnp.sum(dfg * (1.0 - sg), axis=0, keepdims=True)
            return dlb

        dlb = lax.fori_loop(0, nc // HG_UNROLL_BWD, bwd_group, jnp.zeros((1, HG_DK), F32))
        dl0 = dlb * lb * (1.0 - lb)
        dl_ref[...] += jnp.concatenate([dl0, -dl0], axis=0)

    col = lambda off: pl.BlockSpec((HG_T, HG_DK), lambda h, t: (nt - 1 - t, off + h))
    out_col = pl.BlockSpec((HG_T, HG_DK), lambda h, t: (nt - 1 - t, h))
    return pl.pallas_call(
        body, name="hgrn2_bwd", grid=(HG_H, nt),
        in_specs=[col(0), col(8), col(16), pl.BlockSpec((2, HG_DK), lambda h, t: (0, h)),
                  pl.BlockSpec((1, 1, HG_DK, HG_DK), lambda h, t: (h, nt - 1 - t, 0, 0)), col(0)],
        out_specs=[out_col, out_col, out_col, pl.BlockSpec((2, HG_DK), lambda h, t: (0, h))],
        out_shape=[jax.ShapeDtypeStruct((s, D), BF16)] * 3 + [jax.ShapeDtypeStruct((2, D), F32)],
        scratch_shapes=[pltpu.VMEM((nc, HG_DK, HG_DK), F32), pltpu.VMEM((HG_DK, HG_DK), F32)],
        compiler_params=_cparams(("arbitrary", "arbitrary")),
    )(proj, proj, proj, lb_logits, states, do_raw)


def _hg_post_fwd(o_raw, proj, gnorm, tm=256):
    s = o_raw.shape[0]

    def fn(o, hg, gn):
        outs = []
        for h in range(HG_H):
            sl = slice(h * HG_DK, (h + 1) * HG_DK)
            oh, gh = o[:, sl], hg[:, sl]
            r = lax.rsqrt(jnp.mean(oh * oh, axis=-1, keepdims=True) + EPS)
            outs.append(oh * r * gn * (gh * _sigmoid(gh)))
        return (jnp.concatenate(outs, axis=1),), ()

    return _rows("hgrn2_out_fwd", fn, s, tm, [_rb(o_raw, tm, D), _rb(proj, tm, D, 3), _cst(gnorm)], [(D, BF16)])[0]


def _hg_post_bwd(do_a, o_raw, proj, gnorm, tm=256):
    s = o_raw.shape[0]

    def fn(da, o, hg, gn):
        dos, dhgs = [], []
        dgn = jnp.zeros((1, HG_DK), F32)
        for h in range(HG_H):
            sl = slice(h * HG_DK, (h + 1) * HG_DK)
            oh, gh, dh = o[:, sl], hg[:, sl], da[:, sl]
            r = lax.rsqrt(jnp.mean(oh * oh, axis=-1, keepdims=True) + EPS)
            xhat = oh * r
            sg = _sigmoid(gh)
            dy = dh * (gh * sg)
            dhgs.append(dh * xhat * gn * sg * (1.0 + gh * (1.0 - sg)))
            dgn = dgn + jnp.sum(dy * xhat, axis=0, keepdims=True)
            dxh = dy * gn
            dos.append(r * (dxh - xhat * jnp.mean(dxh * xhat, axis=-1, keepdims=True)))
        return (jnp.concatenate(dos, axis=1), jnp.concatenate(dhgs, axis=1)), (dgn,)

    ins = [_rb(do_a, tm, D), _rb(o_raw, tm, D), _rb(proj, tm, D, 3), _cst(gnorm)]
    return _rows("hgrn2_out_bwd", fn, s, tm, ins, [(D, F32), (D, BF16)], [(1, HG_DK)])


def _log_sigmoid(z):
    return jnp.minimum(z, 0.0) - jnp.log(1.0 + jnp.exp(-jnp.abs(z)))


def _fox_gate_bwd(dct, pff, bias, tm=256):
    s = pff.shape[0]
    nb = s // tm

    def body(d_ref, p_ref, b_ref, dff_ref, db_ref, carry):
        @pl.when(pl.program_id(0) == 0)
        def _():
            carry[...] = jnp.zeros_like(carry)
            db_ref[...] = jnp.zeros_like(db_ref)

        dc = d_ref[...].T
        dlf = _split_dot(_tri(tm, False), dc, 3) + carry[0:1]
        carry[...] = jnp.broadcast_to(dlf[0:1], carry.shape)
        dff = dlf * _sigmoid(-(p_ref[...] + b_ref[...]))
        dff_ref[...] = dff
        db_ref[...] += jnp.sum(dff, axis=0, keepdims=True)

    return pl.pallas_call(
        body, name="fox_gate_bwd", grid=(nb,),
        in_specs=[pl.BlockSpec((LANES, tm), lambda i: (0, nb - 1 - i)),
                  pl.BlockSpec((tm, LANES), lambda i: (nb - 1 - i, 0)), pl.BlockSpec((1, LANES), lambda i: (0, 0))],
        out_specs=[pl.BlockSpec((tm, LANES), lambda i: (nb - 1 - i, 0)), pl.BlockSpec((1, LANES), lambda i: (0, 0))],
        out_shape=[jax.ShapeDtypeStruct((s, LANES), F32), jax.ShapeDtypeStruct((1, LANES), F32)],
        scratch_shapes=[pltpu.VMEM((8, LANES), F32)],
        compiler_params=_cparams(("arbitrary",)),
    )(dct, pff, bias)


def _diag_mask(t):
    r = lax.broadcasted_iota(jnp.int32, (t, t), 0)
    c = lax.broadcasted_iota(jnp.int32, (t, t), 1)
    return r >= c


AUX_ONES = 6


def _pieces(x):
    h = x.astype(BF16)
    r = x - h.astype(F32)
    m = r.astype(BF16)
    return h, m, (r - m.astype(F32)).astype(BF16)


def _lane_put(lane, cols, base):
    out = None
    for i, col in enumerate(cols):
        term = jnp.where(lane == base + i, col.astype(F32), 0.0)
        out = term if out is None else out + term
    return out


def _fox_prep(proj, pff, bias, tm=256):
    s = pff.shape[0]

    def body(q_ref, k_ref, v_ref, p_ref, b_ref, qb_ref, kb_ref, vb_ref, ka_ref, carry):
        @pl.when(pl.program_id(0) == 0)
        def _():
            carry[...] = jnp.zeros_like(carry)

        qb_ref[...] = (q_ref[...] * 0.125).astype(BF16)
        kb_ref[...] = k_ref[...].astype(BF16)
        vb_ref[...] = v_ref[...].astype(BF16)
        lf = _log_sigmoid(p_ref[...] + b_ref[...])
        c = _split_dot(_tri(tm, True), lf, 3) + carry[0:1]
        carry[...] = jnp.broadcast_to(c[tm - 1:tm], carry.shape)
        lane = lax.broadcasted_iota(jnp.int32, (tm, LANES), 1)
        ones = jnp.where((lane >= AUX_ONES) & (lane < AUX_ONES + 6), 1.0, 0.0)
        for p in range(FOX_H // 2):
            aux = ones
            for z in range(2):
                col = jnp.sum(jnp.where(lane == 2 * p + z, c, 0.0), axis=1, keepdims=True)
                aux = aux + _lane_put(lane, _pieces(-col), 3 * z)
            ka_ref[:, p * LANES:(p + 1) * LANES] = aux.astype(BF16)

    row = lambda cb: pl.BlockSpec((tm, D), lambda i: (i, cb))
    return pl.pallas_call(
        body, name="fox_prep", grid=(s // tm,),
        in_specs=[row(4), row(5), row(6), pl.BlockSpec((tm, LANES), lambda i: (i, 0)),
                  pl.BlockSpec((1, LANES), lambda i: (0, 0))],
        out_specs=[row(0)] * 4, out_shape=[jax.ShapeDtypeStruct((s, D), BF16)] * 4,
        scratch_shapes=[pltpu.VMEM((8, LANES), F32)],
        compiler_params=_cparams(("arbitrary",)),
    )(proj, proj, proj, pff, bias)


def _fox_fwd(qb, kb, vb, ka):
    s = qb.shape[0]
    t = min(FOX_T, s)
    nq = s // t

    def body(q_ref, k_ref, v_ref, ka_ref, o_ref, la_ref):
        i = pl.program_id(1)
        lane = lax.broadcasted_iota(jnp.int32, (t, LANES), 1)
        in_a = lane < FOX_D
        q = q_ref[...]
        zero = jnp.zeros_like(q)
        qh = [jnp.where(in_a, q, zero), jnp.where(in_a, zero, q)]
        c_ones = [jnp.where((lane >= 3 * z) & (lane < 3 * z + 3), 1.0, 0.0) for z in range(2)]

        def keys(j):
            rows = pl.ds(pl.multiple_of(j * t, t), t)
            return jnp.concatenate([k_ref[rows, :], ka_ref[rows, :]], axis=1), rows

        dmask = _diag_mask(t)

        def logits(qx, kk, masked):
            e = lax.dot_general(qx, kk, (NT, ((), ())), preferred_element_type=F32)
            return jnp.where(dmask, e, -1e30) if masked else e

        qc = [jnp.concatenate([qh[z], c_ones[z].astype(BF16)], axis=1) for z in range(2)]

        def step(j, carry, masked):
            kk, rows = keys(j)
            vj = v_ref[rows, :]
            scores = [logits(qc[z], kk, masked) for z in range(2)]
            one = jnp.ones_like(vj)
            vh = [jnp.where(in_a, vj, one), jnp.where(in_a, one, vj)]
            out = []
            for z in range(2):
                m, acc = carry[z]
                m_new = jnp.maximum(m, jnp.max(scores[z], axis=1, keepdims=True))
                p = jnp.exp(scores[z] - m_new)
                out.append((m_new, jnp.exp(m - m_new) * acc + _dot(p, vh[z], NN)))
            return tuple(out)

        init = tuple((jnp.full((t, 1), -1e30, F32), jnp.zeros((t, LANES), F32)) for _ in range(2))
        (ma, acc_a), (mb, acc_b) = step(i, lax.fori_loop(0, i, lambda j, c: step(j, c, False), init), True)
        la = jnp.sum(jnp.where(lane == FOX_D, acc_a, 0.0), axis=1, keepdims=True)
        lb = jnp.sum(jnp.where(lane == 0, acc_b, 0.0), axis=1, keepdims=True)
        o_ref[...] = jnp.where(in_a, acc_a / la, acc_b / lb).astype(o_ref.dtype)
        la_ref[...] = (_lane_put(lane, _pieces(-(ma + jnp.log(la))), AUX_ONES)
                       + _lane_put(lane, _pieces(-(mb + jnp.log(lb))), AUX_ONES + 3)).astype(la_ref.dtype)

    blk = pl.BlockSpec((t, LANES), lambda p, i: (i, p))
    whole = pl.BlockSpec((s, LANES), lambda p, i: (0, p))
    return pl.pallas_call(
        body, name="fox_attn_fwd", grid=(FOX_H // 2, nq), in_specs=[blk, whole, whole, whole],
        out_specs=[blk, blk], out_shape=[jax.ShapeDtypeStruct((s, D), BF16)] * 2,
        compiler_params=_cparams(("arbitrary", "arbitrary")),
    )(qb, kb, vb, ka)


def _fox_bwd(qb, kb, vb, ka, ob, laux, dob):
    s = qb.shape[0]
    t = min(FOX_T, s)
    nq = s // t

    def body(q_ref, k_ref, v_ref, ka_ref, o_ref, la_ref, do_ref, dq_ref, dk_ref, dv_ref, dc_ref, dkt, dvt):
        i = pl.program_id(1)

        @pl.when(i == 0)
        def _():
            dkt[...] = jnp.zeros_like(dkt)
            dvt[...] = jnp.zeros_like(dvt)
            dc_ref[...] = jnp.zeros_like(dc_ref)

        lane = lax.broadcasted_iota(jnp.int32, (t, LANES), 1)
        in_a = lane < FOX_D
        q, do, la = q_ref[...], do_ref[...], la_ref[...].astype(F32)
        zero = jnp.zeros_like(q)
        qh = [jnp.where(in_a, q, zero), jnp.where(in_a, zero, q)]
        doh = [jnp.where(in_a, do, zero), jnp.where(in_a, zero, do)]
        qt = [h.astype(F32).T.astype(BF16) for h in qh]
        dot_ = [h.astype(F32).T.astype(BF16) for h in doh]
        prod = do.astype(F32) * o_ref[...].astype(F32)
        qx, dox = [], []
        for z in range(2):
            delta = jnp.sum(jnp.where(in_a if z == 0 else ~in_a, prod, 0.0), axis=1, keepdims=True)
            c_ones = jnp.where((lane >= 3 * z) & (lane < 3 * z + 3), 1.0, 0.0)
            lse_lanes = (lane >= AUX_ONES + 3 * z) & (lane < AUX_ONES + 3 * z + 3)
            qx.append(jnp.concatenate([qh[z], (c_ones + jnp.where(lse_lanes, la, 0.0)).astype(BF16)], axis=1))
            dox.append(jnp.concatenate([doh[z], _lane_put(lane, _pieces(-delta), 3 * z).astype(BF16)], axis=1))
        v_ones = jnp.where(lane < 6, 1.0, 0.0).astype(BF16)
        dmask = _diag_mask(t)

        def step(j, carry, masked):
            rows = pl.ds(pl.multiple_of(j * t, t), t)
            kj, vj = k_ref[rows, :], v_ref[rows, :]
            kk = jnp.concatenate([kj, ka_ref[rows, :]], axis=1)
            vv = jnp.concatenate([vj, v_ones], axis=1)
            out = []
            dk_add, dv_add = None, None
            for z in range(2):
                dq, rsum = carry[z]
                e = lax.dot_general(qx[z], kk, (NT, ((), ())), preferred_element_type=F32)
                if masked:
                    e = jnp.where(dmask, e, -1e30)
                p = jnp.exp(e)
                ds = p * lax.dot_general(dox[z], vv, (NT, ((), ())), preferred_element_type=F32)
                dkz, dvz = _dot(qt[z], ds, NN), _dot(dot_[z], p, NN)
                dk_add = dkz if dk_add is None else dk_add + dkz
                dv_add = dvz if dv_add is None else dv_add + dvz
                dc_ref[0, z, j] += -jnp.sum(ds, axis=0, keepdims=True)
                out.append((dq + _dot(ds, kj, NN), rsum + jnp.sum(ds, axis=1, keepdims=True)))
            dkt[j] += dk_add
            dvt[j] += dv_add
            return tuple(out)

        init = tuple((jnp.zeros((t, LANES), F32), jnp.zeros((t, 1), F32)) for _ in range(2))
        (dq_a, rs_a), (dq_b, rs_b) = step(i, lax.fori_loop(0, i, lambda j, c: step(j, c, False), init), True)
        for z, rs in enumerate((rs_a, rs_b)):
            dc_ref[0, z, i] += jnp.transpose(jnp.broadcast_to(rs, (t, LANES)))[0:1]
        dq_ref[...] = (jnp.where(in_a, dq_a, dq_b) * 0.125).astype(dq_ref.dtype)

        @pl.when(i == nq - 1)
        def _():
            for jb in range(nq):
                dk_ref[jb * t:(jb + 1) * t, :] = dkt[jb].T.astype(dk_ref.dtype)
                dv_ref[jb * t:(jb + 1) * t, :] = dvt[jb].T.astype(dv_ref.dtype)

    blk = pl.BlockSpec((t, LANES), lambda p, i: (i, p))
    whole = pl.BlockSpec((s, LANES), lambda p, i: (0, p))
    return pl.pallas_call(
        body, name="fox_attn_bwd", grid=(FOX_H // 2, nq),
        in_specs=[blk, whole, whole, whole, blk, blk, blk],
        out_specs=[blk, whole, whole, pl.BlockSpec((1, 2, nq, 1, t), lambda p, i: (p, 0, 0, 0, 0))],
        out_shape=[jax.ShapeDtypeStruct((s, D), BF16)] * 3 + [jax.ShapeDtypeStruct((FOX_H // 2, 2, nq, 1, t), F32)],
        scratch_shapes=[pltpu.VMEM((nq, LANES, t), F32), pltpu.VMEM((nq, LANES, t), F32)],
        compiler_params=_cparams(("arbitrary", "arbitrary")),
    )(qb, kb, vb, ka, ob, laux, dob)


def _adamw(name, w, g, m, v, tm=None):
    rows, width = w.shape
    tm = rows if tm is None else tm
    c1 = 1.0 - ADAM_B1 ** ADAM_STEP
    c2 = 1.0 - ADAM_B2 ** ADAM_STEP

    def fn(wb, gb, mb, vb):
        m_new = ADAM_B1 * mb + (1.0 - ADAM_B1) * gb
        v_new = ADAM_B2 * vb + (1.0 - ADAM_B2) * (gb * gb)
        delta = -ADAM_LR * ((m_new / c1) / (jnp.sqrt(v_new / c2) + ADAM_EPS) + ADAM_WD * wb)
        return (delta, m_new, v_new), ()

    ins = [_rb(a, tm, width) for a in (w, g, m, v)]
    return _rows(name, fn, rows, tm, ins, [(width, F32)] * 3)


def _me():
    return lax.axis_index("x"), lax.axis_index("y"), lax.axis_index("c")


def _all_reduce_small(name, block):
    r, n = block.shape

    def body(x_ref, sum_ref, gath, send_sems, recv_sems):
        x, y, c = _me()
        me = 4 * x + 2 * y + c
        gath[me] = x_ref[...]
        sends = []
        for k in range(1, 8):
            px = x ^ ((k >> 2) & 1)
            py = y ^ ((k >> 1) & 1)
            pc = c ^ (k & 1)
            sends.append(pltpu.make_async_remote_copy(
                src_ref=x_ref, dst_ref=gath.at[me], send_sem=send_sems.at[k - 1], recv_sem=recv_sems.at[k - 1],
                device_id=(px, py, pc), device_id_type=MESH))
        for cp in sends:
            cp.start()
        for k in range(1, 8):
            peer = me ^ k
            pltpu.make_async_remote_copy(
                src_ref=x_ref, dst_ref=gath.at[peer], send_sem=send_sems.at[k - 1], recv_sem=recv_sems.at[k - 1],
                device_id=(x, y, c), device_id_type=MESH).wait_recv()
        for cp in sends:
            cp.wait_send()
        acc = gath[0]
        for d in range(1, 8):
            acc = acc + gath[d]
        sum_ref[...] = acc

    vm = pl.BlockSpec(memory_space=pltpu.VMEM)
    return pl.pallas_call(
        body, name=name, in_specs=[vm], out_specs=vm, out_shape=jax.ShapeDtypeStruct((r, n), F32),
        scratch_shapes=[pltpu.VMEM((8, r, n), F32), pltpu.SemaphoreType.DMA((7,)), pltpu.SemaphoreType.DMA((7,))],
    )(block)


WD_EXT_ROWS = 736


def _remote(src, dst, send_sems, recv_sems, k, to):
    return pltpu.make_async_remote_copy(src_ref=src, dst_ref=dst, send_sem=send_sems.at[k], recv_sem=recv_sems.at[k],
                                        device_id=to, device_id_type=MESH)


def _all_gather8_multi(name, blocks):
    nt = len(blocks)

    def body(*refs):
        x_refs, out_refs, send_sems, recv_sems = refs[:nt], refs[nt:2 * nt], refs[-2], refs[-1]
        x, y, c = _me()
        me, sibling = (x, y, c), (x, y, 1 - c)
        chips = [(1 - x, y), (x, 1 - y), (1 - x, 1 - y)]
        slot = lambda q, p: out_refs[q].at[4 * p[0] + 2 * p[1] + p[2]]

        def copies(k, blk, to, from_input=False):
            return [_remote(x_refs[q] if from_input else slot(q, blk), slot(q, blk), send_sems, recv_sems, k * nt + q, to)
                    for q in range(nt)]

        first = copies(0, me, sibling, True)
        for j, chip in enumerate(chips):
            first += copies(1 + j, me, (*chip, c), True)
        for cp in first:
            cp.start()
        passed = []
        for j, chip in enumerate(chips):
            for cp in copies(1 + j, (*chip, c), me):
                cp.wait_recv()
            fwd = copies(4 + j, (*chip, c), sibling)
            for cp in fwd:
                cp.start()
            passed += fwd
        for cp in copies(0, sibling, me):
            cp.wait_recv()
        back = copies(7, sibling, sibling)
        for cp in back:
            cp.start()
        for j, chip in enumerate(chips):
            for cp in copies(4 + j, (*chip, 1 - c), me):
                cp.wait_recv()
        for cp in copies(7, me, me):
            cp.wait_recv()
        for cp in first + passed + back:
            cp.wait_send()

    return pl.pallas_call(
        body, name=name, in_specs=[ANY] * nt, out_specs=[ANY] * nt,
        out_shape=[jax.ShapeDtypeStruct((8,) + b.shape, b.dtype) for b in blocks],
        scratch_shapes=[pltpu.SemaphoreType.DMA((8 * nt,)), pltpu.SemaphoreType.DMA((8 * nt,))],
    )(*blocks)


def _swap_halves_multi(name, gs):
    nt = len(gs)
    n_chip = gs[0].shape[0]

    def body(*refs):
        g_refs, got_refs, send_sems, recv_sems = refs[:nt], refs[nt:2 * nt], refs[-2], refs[-1]
        x, y, c = _me()
        cps = [_remote(g_refs[q].at[j, 1 - c], got_refs[q].at[j], send_sems, recv_sems, q * n_chip + j, (x, y, 1 - c))
               for q in range(nt) for j in range(n_chip)]
        for cp in cps:
            cp.start()
        for cp in cps:
            cp.wait()

    return pl.pallas_call(
        body, name=name, in_specs=[ANY] * nt, out_specs=[ANY] * nt,
        out_shape=[jax.ShapeDtypeStruct((g.shape[0],) + g.shape[2:], g.dtype) for g in gs],
        scratch_shapes=[pltpu.SemaphoreType.DMA((nt * n_chip,)), pltpu.SemaphoreType.DMA((nt * n_chip,))],
    )(*gs)


def _swap_sibling_multi(name, xs):
    nt = len(xs)

    def body(*refs):
        x_refs, out_refs, send_sems, recv_sems = refs[:nt], refs[nt:2 * nt], refs[-2], refs[-1]
        x, y, c = _me()
        cps = [_remote(x_refs[q], out_refs[q], send_sems, recv_sems, q, (x, y, 1 - c)) for q in range(nt)]
        for cp in cps:
            cp.start()
        for cp in cps:
            cp.wait()

    return pl.pallas_call(
        body, name=name, in_specs=[ANY] * nt, out_specs=[ANY] * nt,
        out_shape=[jax.ShapeDtypeStruct(a.shape, a.dtype) for a in xs],
        scratch_shapes=[pltpu.SemaphoreType.DMA((nt,)), pltpu.SemaphoreType.DMA((nt,))],
    )(*xs)


def _chip_exchange_multi(name, ps):
    nt = len(ps)

    def body(*refs):
        p_refs, out_refs, bounce_refs = refs[:nt], refs[nt:2 * nt], refs[2 * nt:3 * nt]
        send_sems, recv_sems = refs[-2], refs[-1]
        x, y, c = _me()
        my_chip = 2 * x + y
        sibling = (x, y, 1 - c)
        chips = [(1 - x, y), (x, 1 - y), (1 - x, 1 - y)]
        cp = lambda k, q, src, dst, to: _remote(src, dst, send_sems, recv_sems, k * nt + q, to)
        sends = [cp(k, q, p_refs[q].at[2 * px + py], out_refs[q].at[my_chip], (px, py, c))
                 for k, (px, py) in enumerate(chips) for q in range(nt)]
        sends += [cp(3, q, p_refs[q].at[my_chip], bounce_refs[q], sibling) for q in range(nt)]
        for s_ in sends:
            s_.start()
        backs = []
        for q in range(nt):
            cp(3, q, p_refs[q].at[my_chip], bounce_refs[q], sibling).wait_recv()
            backs.append(cp(4, q, bounce_refs[q], out_refs[q].at[my_chip], sibling))
            backs[-1].start()
        for k, (px, py) in enumerate(chips):
            for q in range(nt):
                cp(k, q, p_refs[q].at[my_chip], out_refs[q].at[2 * px + py], (px, py, c)).wait_recv()
        for q in range(nt):
            cp(4, q, bounce_refs[q], out_refs[q].at[my_chip], sibling).wait_recv()
        for s_ in sends + backs:
            s_.wait_send()

    outs = pl.pallas_call(
        body, name=name, in_specs=[ANY] * nt, out_specs=[ANY] * (2 * nt),
        out_shape=[jax.ShapeDtypeStruct(p.shape, p.dtype) for p in ps]
        + [jax.ShapeDtypeStruct(p.shape[1:], p.dtype) for p in ps],
        scratch_shapes=[pltpu.SemaphoreType.DMA((5 * nt,)), pltpu.SemaphoreType.DMA((5 * nt,))],
    )(*ps)
    return outs[:nt]


def _row_tile(m):
    return m if m <= 384 else 128


def _add2_rows(name, a, b):
    n4, m, n = a.shape
    tm = _row_tile(m)
    out = _rows(name, lambda p, q: ((p + q,), ()), n4 * m, tm,
                [_rb(a.reshape(n4 * m, n), tm, n), _rb(b.reshape(n4 * m, n), tm, n)], [(n, BF16)])[0]
    return out.reshape(n4, m, n)


def _add4_rows(name, p):
    _, m, n = p.shape
    tm = _row_tile(m)
    nb = m // tm
    flat = p.reshape(4 * m, n)
    ins = [(flat, (tm, n), (lambda i, j=j: (j * nb + i, 0))) for j in range(4)]
    f32 = lambda v: v.astype(F32)
    return _rows(name, lambda a, b, c, d: ((((f32(a) + f32(b)) + f32(c)) + f32(d),), ()), m, tm, ins, [(n, F32)])[0]


def _in_proj_with_gather(n1, w_main, blocks):
    m, k = n1.shape
    n = w_main.shape[1]
    tm, tn = min(1024, m), 1024
    gi, gj = m // tm, n // tn
    last, mid = gi * gj - 1, (3 * gi * gj) // 4
    nt = len(blocks)

    def body(*refs):
        a_ref, b_ref, x_refs, o_ref, out_refs = refs[0], refs[1], refs[2:2 + nt], refs[2 + nt], refs[3 + nt:3 + 2 * nt]
        send_sems, recv_sems = refs[-2], refs[-1]
        step = pl.program_id(0) * gj + pl.program_id(1)
        x, y, c = _me()
        me, sibling = (x, y, c), (x, y, 1 - c)
        chips = [(1 - x, y), (x, 1 - y), (1 - x, 1 - y)]
        slot = lambda q, p: out_refs[q].at[4 * p[0] + 2 * p[1] + p[2]]

        def copies(kk, blk, to, from_input=False):
            return [_remote(x_refs[q] if from_input else slot(q, blk), slot(q, blk), send_sems, recv_sems, kk * nt + q, to)
                    for q in range(nt)]

        def first():
            out = copies(0, me, sibling, True)
            for j, chip in enumerate(chips):
                out += copies(1 + j, me, (*chip, c), True)
            return out

        @pl.when(step == 0)
        def _():
            for cp in first():
                cp.start()

        @pl.when(step == mid)
        def _():
            for j, chip in enumerate(chips):
                for cp in copies(1 + j, (*chip, c), me):
                    cp.wait_recv()
                for cp in copies(4 + j, (*chip, c), sibling):
                    cp.start()
            for cp in copies(0, sibling, me):
                cp.wait_recv()
            for cp in copies(7, sibling, sibling):
                cp.start()

        o_ref[...] = _dot(a_ref[...], b_ref[...], NN)

        @pl.when(step == last)
        def _():
            for j, chip in enumerate(chips):
                for cp in copies(4 + j, (*chip, 1 - c), me):
                    cp.wait_recv()
            for cp in copies(7, me, me):
                cp.wait_recv()
            sent = first() + copies(7, sibling, sibling)
            for j, chip in enumerate(chips):
                sent += copies(4 + j, (*chip, c), sibling)
            for cp in sent:
                cp.wait_send()

    outs = pl.pallas_call(
        body, name="in_proj", grid=(gi, gj),
        in_specs=[pl.BlockSpec((tm, k), lambda i, j: (i, 0)), pl.BlockSpec((k, tn), lambda i, j: (0, j))] + [ANY] * nt,
        out_specs=[pl.BlockSpec((tm, tn), lambda i, j: (i, j))] + [ANY] * nt,
        out_shape=[jax.ShapeDtypeStruct((m, n), F32)] + [jax.ShapeDtypeStruct((8,) + b.shape, b.dtype) for b in blocks],
        scratch_shapes=[pltpu.SemaphoreType.DMA((8 * nt,)), pltpu.SemaphoreType.DMA((8 * nt,))],
        compiler_params=_cparams(("arbitrary", "arbitrary")),
    )(n1, w_main, *blocks)
    return outs[0], outs[1:]


def _weight_halves(w_in, w_a, w_b, w_out, w_up, w_down, conv_w):
    c = lax.axis_index("c")
    bits = lax.bitcast_convert_type(conv_w, BF16).reshape(-1)
    extra = jnp.zeros(((WD_EXT_ROWS - W_DOWN_SHARD) * D,), BF16).at[:bits.shape[0]].set(bits)
    wd_ext = jnp.concatenate([w_down.astype(BF16), extra.reshape(-1, D)], axis=0)
    shards = [w_in.astype(BF16), w_a.astype(BF16), w_b.astype(BF16), w_out.astype(BF16), w_up.astype(BF16), wd_ext]
    return [lax.dynamic_slice_in_dim(t, c * (t.shape[0] // 2), t.shape[0] // 2, axis=0) for t in shards]


def _unpack_w_in(gathered):
    wi = gathered.reshape(N_CHIP, D, W_IN_SHARD).transpose(1, 0, 2).reshape(D, N_CHIP * W_IN_SHARD)
    w_main = jnp.concatenate([wi[:, :FF_COL], wi[:, FF_COL + FOX_H:]], axis=1)
    return w_main, jnp.pad(wi[:, FF_COL:FF_COL + FOX_H], ((0, 0), (0, LANES - FOX_H)))


def _unpack_later_weights(gathered):
    full = [g.reshape((N_CHIP, 2 * g.shape[1]) + g.shape[2:]) for g in gathered]
    wa, wb, wo = (full[i].reshape(D, D) for i in (0, 1, 2))
    wu = full[3].transpose(1, 0, 2).reshape(D, 2 * D_FF)
    wd = full[4][:, :W_DOWN_SHARD].reshape(D_FF, D)
    n_bits = 3 * W_UP_SHARD * 2
    cw_bits = full[4][:, W_DOWN_SHARD:].reshape(N_CHIP, -1)[:, :n_bits].reshape(N_CHIP, 3, W_UP_SHARD, 2)
    cw = lax.bitcast_convert_type(cw_bits, F32).transpose(1, 0, 2).reshape(3, 2 * D_FF)
    return wa, wb, wo, wu, wd, cw


def _chip_sums(tag, per_chip):
    c = lax.axis_index("c")
    gs = [t.reshape(N_CHIP, 2, t.shape[1] // 2, t.shape[2]) for t in per_chip]
    got = _swap_halves_multi("grad_swap_halves_" + tag, gs)
    return [_add2_rows("grad_chip_sum_%s%d" % (tag, q), lax.dynamic_index_in_dim(g, c, axis=1, keepdims=False), s_)
            for q, (g, s_) in enumerate(zip(gs, got))]


def _late_weight_chip_sums(d_a, d_b, d_o, d_u, d_d):
    return _chip_sums("late", [d_a.reshape(N_CHIP, -1, D), d_b.reshape(N_CHIP, -1, D), d_o.reshape(N_CHIP, -1, D),
                               d_u.reshape(D, N_CHIP, W_UP_SHARD).transpose(1, 0, 2), d_d.reshape(N_CHIP, -1, D)])


def _finish_grads(d_main, d_ff, late_pieces):
    c = lax.axis_index("c")
    d_in = jnp.concatenate(d_main[:7] + [d_ff[:, :FOX_H]] + d_main[7:], axis=1)
    sums = _chip_sums("w_in", [d_in.reshape(D, N_CHIP, W_IN_SHARD).transpose(1, 0, 2)])
    pieces = list(_chip_exchange_multi("grad_chip_exchange", sums)) + list(late_pieces)
    mine = [_add4_rows("grad_sum_chips_%d" % q, p) for q, p in enumerate(pieces)]
    other = _swap_sibling_multi("grad_share_half", mine)
    return [jnp.concatenate([jnp.where(c == 0, a, b), jnp.where(c == 0, b, a)], axis=0) for a, b in zip(mine, other)]


def _local_step(x, target, norm_mix, fox_f_bias, hg_lb_logits, hg_norm, norm_ffn, conv_b, norm_final,
                w_main, w_ff, later):
    bias = jnp.pad(fox_f_bias, ((0, 0), (0, LANES - FOX_H)))

    n1, n1t = _rms_fwd("norm_mix_fwd", x, norm_mix)
    if len(later) == 5:
        proj, gathered = _in_proj_with_gather(n1, w_main, later)
        wa, wb, wo, wu, wd, conv_w = _unpack_later_weights(gathered)
    else:
        proj = _mm("in_proj", n1, w_main, "nn", F32, 1024, 1024, D)
        wa, wb, wo, wu, wd, conv_w = later
    conv_w8 = jnp.pad(conv_w, ((0, 5), (0, 0)))
    pff = _mm("in_proj_forget", n1, w_ff, "nn", F32, 1024, LANES, D)
    qb, kb, vb, ka = _fox_prep(proj, pff, bias)
    o_b, laux = _fox_fwd(qb, kb, vb, ka)
    o_raw, states = _hg_fwd(proj, hg_lb_logits)
    o_a = _hg_post_fwd(o_raw, proj, hg_norm)
    pa = _mm("branch_a", o_a, wa, "nn", F32, 1024, 1024, D)
    pb = _mm("branch_b", o_b, wb, "nn", F32, 1024, 1024, D)
    merged = _merge_fwd(pa, pb, proj)
    h1 = _mm("out_proj", merged, wo, "nn", F32, 1024, 1024, D, res=x)
    n2, n2t = _rms_fwd("norm_ffn_fwd", h1, norm_ffn)
    u = _mm("ffn_up", n2, wu, "nn", F32, 1024, W_UP_SHARD, D)
    act, gelu_gate, dact_dgate = _convglu_fwd(u, conv_w8, conv_b)
    h2 = _mm("ffn_down", act, wd, "nn", F32, 512, 1024, D_FF, res=h1)
    (dh2,), (d_norm_final, loss_row) = _final(h2, target, norm_final)

    dact = _mm("ffn_down_dx", dh2, wd, "nt", BF16, 1024, D_FF, D)
    d_wd = _mm("ffn_down_dw", act, dh2, "tn", F32, D_FF // 2, 1024, DW_TK // 2)
    (du,), (d_conv_w8, d_conv_b) = _convglu_bwd(u, dact, gelu_gate, dact_dgate, conv_w8)
    dn2 = _mm("ffn_up_dx", du, wu, "nt", F32, 1024, 1024, W_UP_SHARD)
    d_wu = _mm("ffn_up_dw", n2t, du, "nn", F32, 1024, W_UP_SHARD, DW_TK)
    (dh1,), (d_norm_ffn,) = _rms_bwd("norm_ffn_bwd", h1, norm_ffn, [dn2], dh2)

    dmerged = _mm("out_proj_dx", dh1, wo, "nt", F32, 1024, 1024, D)
    d_wo = _mm("out_proj_dw", merged, dh1, "tn", F32, 1024, 1024, DW_TK)
    dpa, dpb, dga, dgb = _merge_bwd(dmerged, pa, pb, proj)
    do_a = _mm("branch_a_dx", dpa, wa, "nt", F32, 1024, 1024, D)
    do_b = _mm("branch_b_dx", dpb, wb, "nt", BF16, 1024, 1024, D)
    d_wa = _mm("branch_a_dw", o_a, dpa, "tn", F32, 1024, 1024, DW_TK)
    d_wb = _mm("branch_b_dw", o_b, dpb, "tn", F32, 1024, 1024, DW_TK)

    (do_raw, dhg), (d_hg_norm,) = _hg_post_bwd(do_a, o_raw, proj, hg_norm)
    dhq, dhf, dhi, d_lb_logits = _hg_bwd(proj, hg_lb_logits, states, do_raw)

    dfq, dfk, dfv, dcrow = _fox_bwd(qb, kb, vb, ka, o_b, laux, do_b)
    dct = jnp.pad(dcrow.reshape(FOX_H, x.shape[0]), ((0, LANES - FOX_H), (0, 0)))
    dff, d_bias = _fox_gate_bwd(dct, pff, bias)

    pieces = [dhq, dhf, dhi, dhg, dfq, dfk, dfv, dga, dgb]
    if len(later) == 5:
        dn1, late = _mm_sum_nt("in_proj_dx", pieces, w_main, (dff, w_ff), 1024, 1024,
                               exchange=_late_weight_chip_sums(d_wa, d_wb, d_wo, d_wu, d_wd))
    else:
        dn1, late = _mm_sum_nt("in_proj_dx", pieces, w_main, (dff, w_ff), 1024, 1024), (d_wa, d_wb, d_wo, d_wu, d_wd)
    d_w_main = [_mm("in_proj_dw_%d" % i, n1t, p, "nn", F32, 1024, 1024, DW_TK) for i, p in enumerate(pieces)]
    d_w_ff = _mm("in_proj_forget_dw", n1t, dff, "nn", F32, 1024, LANES, DW_TK)
    (dx,), (d_norm_mix,) = _rms_bwd("norm_mix_bwd", x, norm_mix, [dn1], dh1)

    small = dict(norm_mix=d_norm_mix, fox_f_bias=d_bias[:, :FOX_H], hg_lb_logits=d_lb_logits, hg_norm=d_hg_norm,
                 norm_ffn=d_norm_ffn, conv_b=d_conv_b, norm_final=d_norm_final, conv_w=d_conv_w8[:3], loss=loss_row)
    big = (d_w_main, d_w_ff) + tuple(late)
    return dx, small, big


SMALL_KEYS = ("norm_mix", "fox_f_bias", "hg_lb_logits", "hg_norm", "norm_ffn", "conv_b", "norm_final")


def _pack_small(parts):
    rows, layout = [], []
    for key, arr in parts:
        flat = arr.reshape(-1)
        n = flat.shape[0]
        nr = -(-n // LANES)
        rows.append(jnp.pad(flat, (0, nr * LANES - n)).reshape(nr, LANES))
        layout.append((key, arr.shape, n, nr))
    packed = jnp.concatenate(rows, axis=0)
    pad = -packed.shape[0] % 8
    return jnp.pad(packed, ((0, pad), (0, 0))), layout


def _unpack_small(packed, layout):
    out, r0 = {}, 0
    for key, shape, n, nr in layout:
        out[key] = packed[r0:r0 + nr].reshape(-1)[:n].reshape(shape)
        r0 += nr
    return out


def kernel(x, norm_mix, w_in, fox_f_bias, hg_lb_logits, hg_norm, w_branch_a, w_branch_b, w_out, norm_ffn, w_up, conv_w, conv_b, w_down, norm_final, loss_target, m_norm_mix, m_w_in, m_fox_f_bias, m_hg_lb_logits, m_hg_norm, m_w_branch_a, m_w_branch_b, m_w_out, m_norm_ffn, m_w_up, m_conv_w, m_conv_b, m_w_down, m_norm_final, v_norm_mix, v_w_in, v_fox_f_bias, v_hg_lb_logits, v_hg_norm, v_w_branch_a, v_w_branch_b, v_w_out, v_norm_ffn, v_w_up, v_conv_w, v_conv_b, v_w_down, v_norm_final):
    chip = 2 * lax.axis_index("x") + lax.axis_index("y")
    halves = _weight_halves(w_in[0], w_branch_a[0], w_branch_b[0], w_out[0], w_up[0], w_down[0], conv_w[0])
    w_main, w_ff = _unpack_w_in(_all_gather8_multi("all_gather_w_in", halves[:1])[0])
    dx, small, big = _local_step(
        x[0], loss_target[0], norm_mix, fox_f_bias, hg_lb_logits, hg_norm, norm_ffn, conv_b,
        norm_final.reshape(1, D), w_main, w_ff, halves[1:])

    packed, layout = _pack_small([(k, small[k]) for k in SMALL_KEYS + ("conv_w", "loss")])
    red = _unpack_small(_all_reduce_small("all_reduce_small", packed), layout)
    loss = red["loss"][0, 0]
    g_conv_w = lax.dynamic_slice_in_dim(red["conv_w"], chip * W_UP_SHARD, W_UP_SHARD, axis=1)

    g_big = _finish_grads(big[0], big[1], big[2:])

    names = ["norm_mix", "w_in", "fox_f_bias", "hg_lb_logits", "hg_norm", "w_branch_a", "w_branch_b", "w_out",
             "norm_ffn", "w_up", "conv_w", "conv_b", "w_down", "norm_final"]
    weights = dict(norm_mix=norm_mix, w_in=w_in, fox_f_bias=fox_f_bias, hg_lb_logits=hg_lb_logits, hg_norm=hg_norm,
                   w_branch_a=w_branch_a, w_branch_b=w_branch_b, w_out=w_out, norm_ffn=norm_ffn, w_up=w_up,
                   conv_w=conv_w, conv_b=conv_b, w_down=w_down, norm_final=norm_final)
    ms = dict(norm_mix=m_norm_mix, w_in=m_w_in, fox_f_bias=m_fox_f_bias, hg_lb_logits=m_hg_lb_logits,
              hg_norm=m_hg_norm, w_branch_a=m_w_branch_a, w_branch_b=m_w_branch_b, w_out=m_w_out,
              norm_ffn=m_norm_ffn, w_up=m_w_up, conv_w=m_conv_w, conv_b=m_conv_b, w_down=m_w_down,
              norm_final=m_norm_final)
    vs = dict(norm_mix=v_norm_mix, w_in=v_w_in, fox_f_bias=v_fox_f_bias, hg_lb_logits=v_hg_lb_logits,
              hg_norm=v_hg_norm, w_branch_a=v_w_branch_a, w_branch_b=v_w_branch_b, w_out=v_w_out,
              norm_ffn=v_norm_ffn, w_up=v_w_up, conv_w=v_conv_w, conv_b=v_conv_b, w_down=v_w_down,
              norm_final=v_norm_final)

    grads, deltas, new_m, new_v = {}, {}, {}, {}
    big_names = ["w_in", "w_branch_a", "w_branch_b", "w_out", "w_up", "w_down"]
    for name, g2 in zip(big_names, g_big):
        shape = weights[name].shape
        rows = g2.shape[0]
        d_, m_, v_ = _adamw("adamw_" + name, weights[name][0], g2, ms[name][0], vs[name][0], tm=rows // 8)
        grads[name], deltas[name], new_m[name], new_v[name] = (a.reshape(shape) for a in (g2, d_, m_, v_))
    shape = conv_w.shape
    d_, m_, v_ = _adamw("adamw_conv_w", conv_w[0], g_conv_w, m_conv_w[0], v_conv_w[0])
    grads["conv_w"], deltas["conv_w"], new_m["conv_w"], new_v["conv_w"] = (
        a.reshape(shape) for a in (g_conv_w, d_, m_, v_))
    gs = {k: red[k].reshape(weights[k].shape) for k in SMALL_KEYS}
    pw, lay = _pack_small([(k, weights[k]) for k in SMALL_KEYS])
    pg, _ = _pack_small([(k, gs[k]) for k in SMALL_KEYS])
    pm, _ = _pack_small([(k, ms[k]) for k in SMALL_KEYS])
    pv, _ = _pack_small([(k, vs[k]) for k in SMALL_KEYS])
    d_, m_, v_ = (_unpack_small(a, lay) for a in _adamw("adamw_small", pw, pg, pm, pv))
    for k in SMALL_KEYS:
        grads[k], deltas[k], new_m[k], new_v[k] = gs[k], d_[k], m_[k], v_[k]

    return (loss, dx[None], *[grads[n] for n in names], *[deltas[n] for n in names],
            *[new_m[n] for n in names], *[new_v[n] for n in names])
```

```python
import jax
import jax.numpy as jnp
from jax import lax
from jax.experimental import pallas as pl
from jax.experimental.pallas import tpu as pltpu

F32 = jnp.float32
BF16 = jnp.bfloat16

D = 1024
HG_H, HG_DK = 8, 128
FOX_H, FOX_D = 16, 64
D_FF = 2816
EPS = 1e-6
N_CHIP = 4
LANES = 128
W_IN_SHARD = 2308
W_UP_SHARD = 1408
W_DOWN_SHARD = 704
FF_COL = 7168
ADAM_LR, ADAM_B1, ADAM_B2, ADAM_EPS, ADAM_WD, ADAM_STEP = 0.001, 0.9, 0.999, 1e-08, 0.01, 10

HG_C = 16
HG_T = 1024
HG_UNROLL = 16
HG_UNROLL_BWD = 4
FOX_T = 1024
DW_TK = 2048
VMEM_LIMIT = 56 * 1024 * 1024
MESH = pl.DeviceIdType.MESH
ANY = pl.BlockSpec(memory_space=pl.ANY)


def _cparams(sem):
    return pltpu.CompilerParams(dimension_semantics=sem, vmem_limit_bytes=VMEM_LIMIT)


def _sigmoid(x):
    return 1.0 / (1.0 + jnp.exp(-x))


def _dot(a, b, dims):
    return lax.dot_general(a.astype(BF16), b.astype(BF16), (dims, ((), ())), preferred_element_type=F32)


NN = ((1,), (0,))
NT = ((1,), (1,))
TN = ((0,), (0,))


def _split_dot(tri, x, parts, dims=NN):
    acc = None
    r = x
    for _ in range(parts):
        p = r.astype(BF16)
        t = lax.dot_general(tri, p, (dims, ((), ())), preferred_element_type=F32)
        acc = t if acc is None else acc + t
        r = r - p.astype(F32)
    return acc


def _rb(arr, tm, width, cb=0):
    return (arr, (tm, width), lambda i: (i, cb))


def _cst(arr):
    return (arr, arr.shape, lambda i: (0,) * arr.ndim)


def _rows(name, fn, n_rows, tm, ins, outs, accs=()):
    n_in, n_out, n_acc = len(ins), len(outs), len(accs)
    nb = n_rows // tm

    def body(*refs):
        vals = [r[...] for r in refs[:n_in]]
        o, a = fn(*vals)
        for r, v in zip(refs[n_in:n_in + n_out], o):
            r[...] = v.astype(r.dtype)
        if n_acc:
            acc_refs = refs[n_in + n_out:]

            @pl.when(pl.program_id(0) == 0)
            def _():
                for r in acc_refs:
                    r[...] = jnp.zeros_like(r)

            for r, v in zip(acc_refs, a):
                r[...] += v

    in_specs = [pl.BlockSpec(bs, im) for (_, bs, im) in ins]
    out_specs = [pl.BlockSpec((tm, w), lambda i: (i, 0)) for (w, _) in outs]
    out_specs += [pl.BlockSpec((r, w), lambda i: (0, 0)) for (r, w) in accs]
    out_shape = [jax.ShapeDtypeStruct((n_rows, w), dt) for (w, dt) in outs]
    out_shape += [jax.ShapeDtypeStruct((r, w), F32) for (r, w) in accs]
    res = pl.pallas_call(
        body, name=name, grid=(nb,), in_specs=in_specs, out_specs=out_specs, out_shape=out_shape,
        compiler_params=_cparams(("arbitrary",)),
    )(*[a for a, _, _ in ins])
    return (res[:n_out], res[n_out:]) if n_acc else res


def _mm(name, a, b, mode, out_dtype, tm, tn, tk, res=None):
    if mode == "nn":
        (m, k), n = a.shape, b.shape[1]
    elif mode == "nt":
        (m, k), n = a.shape, b.shape[0]
    else:
        (k, m), n = a.shape, b.shape[1]
    tm, tn, tk = min(tm, m), min(tn, n), min(tk, k)
    assert m % tm == 0 and n % tn == 0 and k % tk == 0, (name, m, n, k, tm, tn, tk)
    if mode == "nn":
        a_spec = pl.BlockSpec((tm, tk), lambda i, j, kk: (i, kk))
        b_spec = pl.BlockSpec((tk, tn), lambda i, j, kk: (kk, j))
        dims = NN
    elif mode == "nt":
        a_spec = pl.BlockSpec((tm, tk), lambda i, j, kk: (i, kk))
        b_spec = pl.BlockSpec((tn, tk), lambda i, j, kk: (j, kk))
        dims = NT
    else:
        a_spec = pl.BlockSpec((tk, tm), lambda i, j, kk: (kk, i))
        b_spec = pl.BlockSpec((tk, tn), lambda i, j, kk: (kk, j))
        dims = TN
    nk = k // tk
    has_res = res is not None
    acc_in_out = out_dtype == F32 and not has_res

    def body(*refs):
        a_ref, b_ref = refs[0], refs[1]
        r_ref = refs[2] if has_res else None
        o_ref = refs[3] if has_res else refs[2]
        part = _dot(a_ref[...], b_ref[...], dims)

        def finish(val):
            if has_res:
                val = val + r_ref[...]
            o_ref[...] = val.astype(o_ref.dtype)

        if nk == 1:
            finish(part)
        elif acc_in_out:
            kk = pl.program_id(2)

            @pl.when(kk == 0)
            def _():
                o_ref[...] = part

            @pl.when(kk > 0)
            def _():
                o_ref[...] += part
        else:
            acc_ref = refs[-1]
            kk = pl.program_id(2)

            @pl.when(kk == 0)
            def _():
                acc_ref[...] = part

            @pl.when(kk > 0)
            def _():
                acc_ref[...] += part

            @pl.when(kk == nk - 1)
            def _():
                finish(acc_ref[...])

    in_specs = [a_spec, b_spec]
    args = [a, b]
    if has_res:
        in_specs.append(pl.BlockSpec((tm, tn), lambda i, j, kk: (i, j)))
        args.append(res)
    return pl.pallas_call(
        body, name=name, grid=(m // tm, n // tn, nk), in_specs=in_specs,
        out_specs=pl.BlockSpec((tm, tn), lambda i, j, kk: (i, j)),
        out_shape=jax.ShapeDtypeStruct((m, n), out_dtype),
        scratch_shapes=[pltpu.VMEM((tm, tn), F32)] if nk > 1 and not acc_in_out else [],
        compiler_params=_cparams(("arbitrary", "arbitrary", "arbitrary")),
    )(*args)


def _mm_sum_nt(name, pieces, w, extra, tm, tn, exchange=()):
    n_p = len(pieces)
    m, k = pieces[0].shape
    n = w.shape[0]
    xa, xb = extra
    ke = xa.shape[1]
    tm, tn = min(tm, m), min(tn, n)
    nx = len(exchange)
    n_steps = (m // tm) * (n // tn) * (n_p + 1)

    def exchange_steps(refs):
        e_refs = refs[n_p + 3:n_p + 3 + nx]
        out_refs, bounce_refs = refs[n_p + 4 + nx:n_p + 4 + 2 * nx], refs[n_p + 4 + 2 * nx:n_p + 4 + 3 * nx]
        send_sems, recv_sems = refs[-2], refs[-1]
        step = (pl.program_id(0) * (n // tn) + pl.program_id(1)) * (n_p + 1) + pl.program_id(2)
        x, y, c = _me()
        my_chip = 2 * x + y
        sibling = (x, y, 1 - c)
        chips = [(1 - x, y), (x, 1 - y), (1 - x, 1 - y)]
        cp = lambda kx, q, src, dst, to: _remote(src, dst, send_sems, recv_sems, kx * nx + q, to)
        sends = lambda: ([cp(kx, q, e_refs[q].at[2 * px + py], out_refs[q].at[my_chip], (px, py, c))
                          for kx, (px, py) in enumerate(chips) for q in range(nx)]
                         + [cp(3, q, e_refs[q].at[my_chip], bounce_refs[q], sibling) for q in range(nx)])
        backs = lambda: [cp(4, q, bounce_refs[q], out_refs[q].at[my_chip], sibling) for q in range(nx)]

        @pl.when(step == 0)
        def _():
            for s_ in sends():
                s_.start()

        @pl.when(step == n_steps // 2)
        def _():
            for q in range(nx):
                cp(3, q, e_refs[q].at[my_chip], bounce_refs[q], sibling).wait_recv()
            for s_ in backs():
                s_.start()

        @pl.when(step == n_steps - 1)
        def _():
            for kx, (px, py) in enumerate(chips):
                for q in range(nx):
                    cp(kx, q, e_refs[q].at[my_chip], out_refs[q].at[2 * px + py], (px, py, c)).wait_recv()
            for q in range(nx):
                cp(4, q, bounce_refs[q], out_refs[q].at[my_chip], sibling).wait_recv()
            for s_ in sends() + backs():
                s_.wait_send()

    def body(*refs):
        p_refs, w_ref, xa_ref, xb_ref, o_ref = refs[:n_p], refs[n_p], refs[n_p + 1], refs[n_p + 2], refs[n_p + 3 + nx]
        if nx:
            exchange_steps(refs)
        kk = pl.program_id(2)

        @pl.when(kk == 0)
        def _():
            o_ref[...] = _dot(p_refs[0][...], w_ref[...], NT)

        for i in range(1, n_p):
            @pl.when(kk == i)
            def _(i=i):
                o_ref[...] += _dot(p_refs[i][...], w_ref[...], NT)

        @pl.when(kk == n_p)
        def _():
            o_ref[...] += _dot(xa_ref[...], xb_ref[...], NT)

    in_specs = [pl.BlockSpec((tm, k), lambda i, j, kk: (i, 0)) for _ in range(n_p)]
    in_specs.append(pl.BlockSpec((tn, k), lambda i, j, kk: (j, jnp.minimum(kk, n_p - 1))))
    in_specs += [pl.BlockSpec((tm, ke), lambda i, j, kk: (i, 0)), pl.BlockSpec((tn, ke), lambda i, j, kk: (j, 0))]
    outs = pl.pallas_call(
        body, name=name, grid=(m // tm, n // tn, n_p + 1), in_specs=in_specs + [ANY] * nx,
        out_specs=[pl.BlockSpec((tm, tn), lambda i, j, kk: (i, j))] + [ANY] * (2 * nx),
        out_shape=[jax.ShapeDtypeStruct((m, n), F32)] + [jax.ShapeDtypeStruct(e.shape, e.dtype) for e in exchange]
        + [jax.ShapeDtypeStruct(e.shape[1:], e.dtype) for e in exchange],
        scratch_shapes=[pltpu.SemaphoreType.DMA((5 * nx,)), pltpu.SemaphoreType.DMA((5 * nx,))] if nx else [],
        compiler_params=_cparams(("arbitrary", "arbitrary", "arbitrary")),
    )(*pieces, w, xa, xb, *exchange)
    return (outs[0], outs[1:1 + nx]) if nx else outs[0]


def _rms_fwd(name, x, gain, tm=256):
    s = x.shape[0]

    def body(x_ref, g_ref, y_ref, yt_ref):
        xb = x_ref[...]
        y = xb * lax.rsqrt(jnp.mean(xb * xb, axis=-1, keepdims=True) + EPS) * g_ref[...]
        y_ref[...] = y.astype(BF16)
        yt_ref[...] = y.T.astype(BF16)

    return pl.pallas_call(
        body, name=name, grid=(s // tm,),
        in_specs=[pl.BlockSpec((tm, D), lambda i: (i, 0)), pl.BlockSpec((1, D), lambda i: (0, 0))],
        out_specs=[pl.BlockSpec((tm, D), lambda i: (i, 0)), pl.BlockSpec((D, tm), lambda i: (0, i))],
        out_shape=[jax.ShapeDtypeStruct((s, D), BF16), jax.ShapeDtypeStruct((D, s), BF16)],
        compiler_params=_cparams(("arbitrary",)),
    )(x, gain)


def _rms_bwd(name, x, gain, dns, dres, tm=256):
    s = x.shape[0]
    n_dn = len(dns)

    def fn(xb, g, *rest):
        dn = rest[0]
        for t in rest[1:n_dn]:
            dn = dn + t
        r = lax.rsqrt(jnp.mean(xb * xb, axis=-1, keepdims=True) + EPS)
        xhat = xb * r
        dxh = dn * g
        dx = r * (dxh - xhat * jnp.mean(dxh * xhat, axis=-1, keepdims=True)) + rest[n_dn]
        return (dx,), (jnp.sum(dn * xhat, axis=0, keepdims=True),)

    ins = [_rb(x, tm, D), _cst(gain)] + [_rb(t, tm, D) for t in dns] + [_rb(dres, tm, D)]
    return _rows(name, fn, s, tm, ins, [(D, F32)], [(1, D)])


def _final(h2, target, gain, tm=256):
    s = h2.shape[0]

    def fn(hb, tb, g):
        r = lax.rsqrt(jnp.mean(hb * hb, axis=-1, keepdims=True) + EPS)
        xhat = hb * r
        e = xhat * g - tb
        dy = e * (1.0 / D)
        dxh = dy * g
        dh = r * (dxh - xhat * jnp.mean(dxh * xhat, axis=-1, keepdims=True))
        lrow = 0.5 * jnp.sum(jnp.sum(e * e, axis=-1, keepdims=True) * (1.0 / D), axis=0, keepdims=True)
        return (dh,), (jnp.sum(dy * xhat, axis=0, keepdims=True), jnp.broadcast_to(lrow, (1, LANES)))

    return _rows("final_norm_loss", fn, s, tm, [_rb(h2, tm, D), _rb(target, tm, D), _cst(gain)],
                 [(D, F32)], [(1, D), (1, LANES)])


def _merge_fwd(pa, pb, proj, tm=256):
    s = pa.shape[0]

    def fn(a, b, ga, gb):
        return (_sigmoid(ga) * a + _sigmoid(gb) * b,), ()

    ins = [_rb(pa, tm, D), _rb(pb, tm, D), _rb(proj, tm, D, 7), _rb(proj, tm, D, 8)]
    return _rows("merge_fwd", fn, s, tm, ins, [(D, BF16)])[0]


def _merge_bwd(dmerged, pa, pb, proj, tm=256):
    s = pa.shape[0]

    def fn(dm, a, b, ga, gb):
        sa, sb = _sigmoid(ga), _sigmoid(gb)
        return (dm * sa, dm * sb, dm * a * sa * (1.0 - sa), dm * b * sb * (1.0 - sb)), ()

    ins = [_rb(dmerged, tm, D), _rb(pa, tm, D), _rb(pb, tm, D), _rb(proj, tm, D, 7), _rb(proj, tm, D, 8)]
    return _rows("merge_bwd", fn, s, tm, ins, [(D, BF16), (D, BF16), (D, BF16), (D, BF16)])


def _gelu_parts(x):
    cdf = 0.5 * (1.0 + lax.erf(x * 0.7071067811865476))
    pdf = 0.3989422804014327 * jnp.exp(-0.5 * x * x)
    return x * cdf, cdf + x * pdf


def _conv_taps(u_ext, n_out):
    cur = u_ext[8:8 + n_out]
    m1 = pltpu.roll(u_ext, 1, 0)[8:8 + n_out]
    m2 = pltpu.roll(u_ext, 2, 0)[8:8 + n_out]
    return m2, m1, cur


def _convglu_fwd(u, conv_w8, conv_b, tm=256):
    s, w = u.shape
    tb = tm // 8

    def fn(ub, up, cw, cb):
        i = pl.program_id(0)
        up = jnp.where(i == 0, 0.0, up)
        m2, m1, cur = _conv_taps(jnp.concatenate([up, ub], axis=0), tm)
        acc = cb + cw[0:1] * m2 + cw[1:2] * m1 + cw[2:3] * cur
        gl, dgl = _gelu_parts(acc[:, :D_FF])
        val = acc[:, D_FF:]
        return (gl * val, gl, val * dgl), ()

    ins = [_rb(u, tm, w), (u, (8, w), lambda i: (jnp.maximum(i * tb - 1, 0), 0)), _cst(conv_w8), _cst(conv_b)]
    return _rows("convglu_fwd", fn, s, tm, ins, [(D_FF, BF16)] * 3)


def _convglu_bwd(u, dact, gl, gd, conv_w8, tm=256):
    s, w = u.shape
    tb = tm // 8
    nb = s // tm

    def fn(ub, up, db, dn, glb, gln, gdb, gdn, cw):
        i = pl.program_id(0)
        up = jnp.where(i == 0, 0.0, up)
        dn = jnp.where(i == nb - 1, 0.0, dn.astype(F32))
        ne = tm + 8
        m2, m1, cur = _conv_taps(jnp.concatenate([up, ub], axis=0), tm)
        ext = lambda blk, nxt: jnp.concatenate([blk.astype(F32), nxt.astype(F32)], axis=0)
        de = ext(db, dn)
        dacc = jnp.concatenate([de * ext(gdb, gdn), de * ext(glb, gln)], axis=1)
        p1 = pltpu.roll(dacc, ne - 1, 0)[:tm]
        p2 = pltpu.roll(dacc, ne - 2, 0)[:tm]
        d0 = dacc[:tm]
        du = cw[2:3] * d0 + cw[1:2] * p1 + cw[0:1] * p2
        zero5 = jnp.zeros((5, w), F32)
        dcw = jnp.concatenate([
            jnp.sum(d0 * m2, axis=0, keepdims=True), jnp.sum(d0 * m1, axis=0, keepdims=True),
            jnp.sum(d0 * cur, axis=0, keepdims=True), zero5], axis=0)
        return (du,), (dcw, jnp.sum(d0, axis=0, keepdims=True))

    nxt = lambda arr: (arr, (8, D_FF), lambda i: (jnp.minimum((i + 1) * tb, s // 8 - 1), 0))
    ins = [_rb(u, tm, w), (u, (8, w), lambda i: (jnp.maximum(i * tb - 1, 0), 0)),
           _rb(dact, tm, D_FF), nxt(dact), _rb(gl, tm, D_FF), nxt(gl), _rb(gd, tm, D_FF), nxt(gd), _cst(conv_w8)]
    return _rows("convglu_bwd", fn, s, tm, ins, [(w, BF16)], [(8, w), (1, w)])


def _chunk_scan(x, t_iota, reverse):
    k = 1
    while k < HG_C:
        if reverse:
            x = x + jnp.where(t_iota < HG_C - k, pltpu.roll(x, HG_C - k, 0), 0.0)
        else:
            x = x + jnp.where(t_iota >= k, pltpu.roll(x, k, 0), 0.0)
        k *= 2
    return x


def _hg_gates(hq, hf, lb):
    sq = _sigmoid(hq)
    q = hq * sq
    sg = _sigmoid(hf)
    f = lb + (1.0 - lb) * sg
    return q, sq, sg, f, 1.0 - f, jnp.log(f)


def _lb_of(logits):
    l0, l1 = logits[0:1], logits[1:2]
    mx = jnp.maximum(l0, l1)
    e0, e1 = jnp.exp(l0 - mx), jnp.exp(l1 - mx)
    return e0 / (e0 + e1)


def _tri(n, lower):
    r = lax.broadcasted_iota(jnp.int32, (n, n), 0)
    c = lax.broadcasted_iota(jnp.int32, (n, n), 1)
    return jnp.where((r >= c) if lower else (r <= c), 1.0, 0.0).astype(BF16)


def _hg_intra_terms(q, kk, b, t_iota):
    ws, ps = [], []
    for s in range(HG_C):
        p = jnp.where(t_iota >= s, jnp.exp(b - b[s:s + 1]), 0.0)
        ps.append(p)
        ws.append(q * kk[s:s + 1] * p)
    return jnp.concatenate(ws, axis=0), ps


def _hg_fwd(proj, lb_logits):
    s = proj.shape[0]
    nt = s // HG_T
    nc = HG_T // HG_C

    def body(q_ref, f_ref, i_ref, l_ref, o_ref, st_ref, state):
        @pl.when(pl.program_id(1) == 0)
        def _():
            state[...] = jnp.zeros_like(state)

        st_ref[0, 0] = state[...]
        lb = _lb_of(l_ref[...])
        ones = jnp.ones((HG_DK, HG_DK), BF16)
        t_iota = lax.broadcasted_iota(jnp.int32, (HG_C, HG_DK), 0)
        cc = HG_C * HG_C

        def group(gi, st):
            units = []
            for u in range(HG_UNROLL):
                r = pl.ds(pl.multiple_of((gi * HG_UNROLL + u) * HG_C, HG_C), HG_C)
                q, _, _, _, kk, g = _hg_gates(q_ref[r, :], f_ref[r, :], lb)
                b = _chunk_scan(g, t_iota, False)
                b_end = b[HG_C - 1:HG_C]
                w_all, _ = _hg_intra_terms(q, kk, b, t_iota)
                units.append((r, i_ref[r, :], q * jnp.exp(b), jnp.exp(b_end), kk * jnp.exp(b_end - b), w_all))
            a_all = _dot(jnp.concatenate([un[5] for un in units], axis=0), ones, NN)
            kvs = [_dot(v, kd, TN) for (_, v, _, _, kd, _) in units]
            sts = [st]
            for (_, _, _, dec, _, _), kv in zip(units, kvs):
                sts.append(sts[-1] * dec + kv)
            for ui, (r, v, qd, _, _, _) in enumerate(units):
                o = _dot(qd, sts[ui], NT)
                for si in range(HG_C):
                    o = o + a_all[ui * cc + si * HG_C:ui * cc + (si + 1) * HG_C] * v[si:si + 1]
                o_ref[r, :] = o
            return sts[-1]

        state[...] = lax.fori_loop(0, nc // HG_UNROLL, group, state[...])

    col = lambda off: pl.BlockSpec((HG_T, HG_DK), lambda h, t: (t, off + h))
    return pl.pallas_call(
        body, name="hgrn2_fwd", grid=(HG_H, nt),
        in_specs=[col(0), col(8), col(16), pl.BlockSpec((2, HG_DK), lambda h, t: (0, h))],
        out_specs=[pl.BlockSpec((HG_T, HG_DK), lambda h, t: (t, h)),
                   pl.BlockSpec((1, 1, HG_DK, HG_DK), lambda h, t: (h, t, 0, 0))],
        out_shape=[jax.ShapeDtypeStruct((s, D), F32), jax.ShapeDtypeStruct((HG_H, nt, HG_DK, HG_DK), F32)],
        scratch_shapes=[pltpu.VMEM((HG_DK, HG_DK), F32)],
        compiler_params=_cparams(("arbitrary", "arbitrary")),
    )(proj, proj, proj, lb_logits)


def _hg_bwd(proj, lb_logits, states, do_raw):
    s = proj.shape[0]
    nt = s // HG_T
    nc = HG_T // HG_C

    def body(q_ref, f_ref, i_ref, l_ref, st_ref, do_ref, dq_ref, df_ref, di_ref, dl_ref, st_all, adj):
        tb = pl.program_id(1)

        @pl.when(tb == 0)
        def _():
            adj[...] = jnp.zeros_like(adj)
            dl_ref[...] = jnp.zeros_like(dl_ref)

        lb = _lb_of(l_ref[...])
        ones = jnp.ones((HG_DK, HG_DK), BF16)
        t_iota = lax.broadcasted_iota(jnp.int32, (HG_C, HG_DK), 0)
        cc = HG_C * HG_C

        def fwd_group(gi, st):
            terms = []
            for u in range(HG_UNROLL):
                ci = gi * HG_UNROLL + u
                r = pl.ds(pl.multiple_of(ci * HG_C, HG_C), HG_C)
                _, _, _, _, kk, g = _hg_gates(q_ref[r, :], f_ref[r, :], lb)
                b = _chunk_scan(g, t_iota, False)
                b_end = b[HG_C - 1:HG_C]
                terms.append((ci, jnp.exp(b_end), _dot(i_ref[r, :], kk * jnp.exp(b_end - b), TN)))
            for ci, dec, kv in terms:
                st_all[ci] = st
                st = st * dec + kv
            return st

        lax.fori_loop(0, nc // HG_UNROLL, fwd_group, st_ref[0, 0])

        def bwd_group(gj, dlb):
            units = []
            for u in range(HG_UNROLL_BWD):
                ci = nc - 1 - (gj * HG_UNROLL_BWD + u)
                r = pl.ds(pl.multiple_of(ci * HG_C, HG_C), HG_C)
                hq, hf, v, do = q_ref[r, :], f_ref[r, :], i_ref[r, :], do_ref[r, :]
                q, sq, sg, f, kk, g = _hg_gates(hq, hf, lb)
                b = _chunk_scan(g, t_iota, False)
                b_end = b[HG_C - 1:HG_C]
                e_b, e_be, dec = jnp.exp(b), jnp.exp(b_end - b), jnp.exp(b_end)
                w_all, ps = _hg_intra_terms(q, kk, b, t_iota)
                x_all = jnp.concatenate([do * v[si:si + 1] for si in range(HG_C)], axis=0)
                units.append(dict(ci=ci, r=r, hq=hq, v=v, do=do, q=q, sq=sq, sg=sg, f=f, kk=kk, e_b=e_b, e_be=e_be,
                                  dec=dec, kd=kk * e_be, w=w_all, ps=ps, x=x_all))
            both = _dot(jnp.concatenate([un["w"] for un in units] + [un["x"] for un in units], axis=0), ones, NN)
            st0s = [st_all[un["ci"]] for un in units]
            st_ends = [st0 * un["dec"] + _dot(un["v"], un["kd"], TN) for un, st0 in zip(units, st0s)]
            dqks = [_dot(un["do"], un["q"] * un["e_b"], TN) for un in units]
            es = [adj[...]]
            for un, dqk in zip(units, dqks):
                es.append(es[-1] * un["dec"] + dqk)
            adj[...] = es[-1]
            for ui, un in enumerate(units):
                e, q, kk, v, do = es[ui], un["q"], un["kk"], un["v"], un["do"]
                tail = jnp.sum(e * st_ends[ui], axis=0, keepdims=True)
                dq = un["e_b"] * _dot(do, st0s[ui], NN)
                dk = un["e_be"] * _dot(v, e, NN)
                dv = _dot(un["kd"], e, NT)
                a0 = ui * cc
                d0 = (HG_UNROLL_BWD + ui) * cc
                for si in range(HG_C):
                    da = both[d0 + si * HG_C:d0 + (si + 1) * HG_C]
                    aa = both[a0 + si * HG_C:a0 + (si + 1) * HG_C]
                    dap = da * un["ps"][si]
                    dq = dq + dap * kk[si:si + 1]
                    hit = t_iota == si
                    dk = dk + jnp.where(hit, jnp.sum(dap * q, axis=0, keepdims=True), 0.0)
                    dv = dv + jnp.where(hit, jnp.sum(aa * do, axis=0, keepdims=True), 0.0)
                dg = _chunk_scan(q * dq - kk * dk, t_iota, True) + tail
                dfg = dg / un["f"] - dk
                sq, sg, hq, r = un["sq"], un["sg"], un["hq"], un["r"]
                dq_ref[r, :] = (dq * sq * (1.0 + hq * (1.0 - sq))).astype(dq_ref.dtype)
                df_ref[r, :] = (dfg * (1.0 - lb) * sg * (1.0 - sg)).astype(df_ref.dtype)
                di_ref[r, :] = dv.astype(di_ref.dtype)
                dlb = dlb + jnp.sum(dfg * (1.0 - sg), axis=0, keepdims=True)
            return dlb

        dlb = lax.fori_loop(0, nc // HG_UNROLL_BWD, bwd_group, jnp.zeros((1, HG_DK), F32))
        dl0 = dlb * lb * (1.0 - lb)
        dl_ref[...] += jnp.concatenate([dl0, -dl0], axis=0)

    col = lambda off: pl.BlockSpec((HG_T, HG_DK), lambda h, t: (nt - 1 - t, off + h))
    out_col = pl.BlockSpec((HG_T, HG_DK), lambda h, t: (nt - 1 - t, h))
    return pl.pallas_call(
        body, name="hgrn2_bwd", grid=(HG_H, nt),
        in_specs=[col(0), col(8), col(16), pl.BlockSpec((2, HG_DK), lambda h, t: (0, h)),
                  pl.BlockSpec((1, 1, HG_DK, HG_DK), lambda h, t: (h, nt - 1 - t, 0, 0)), col(0)],
        out_specs=[out_col, out_col, out_col, pl.BlockSpec((2, HG_DK), lambda h, t: (0, h))],
        out_shape=[jax.ShapeDtypeStruct((s, D), BF16)] * 3 + [jax.ShapeDtypeStruct((2, D), F32)],
        scratch_shapes=[pltpu.VMEM((nc, HG_DK, HG_DK), F32), pltpu.VMEM((HG_DK, HG_DK), F32)],
        compiler_params=_cparams(("arbitrary", "arbitrary")),
    )(proj, proj, proj, lb_logits, states, do_raw)


def _hg_post_fwd(o_raw, proj, gnorm, tm=256):
    s = o_raw.shape[0]

    def fn(o, hg, gn):
        outs = []
        for h in range(HG_H):
            sl = slice(h * HG_DK, (h + 1) * HG_DK)
            oh, gh = o[:, sl], hg[:, sl]
            r = lax.rsqrt(jnp.mean(oh * oh, axis=-1, keepdims=True) + EPS)
            outs.append(oh * r * gn * (gh * _sigmoid(gh)))
        return (jnp.concatenate(outs, axis=1),), ()

    return _rows("hgrn2_out_fwd", fn, s, tm, [_rb(o_raw, tm, D), _rb(proj, tm, D, 3), _cst(gnorm)], [(D, BF16)])[0]


def _hg_post_bwd(do_a, o_raw, proj, gnorm, tm=256):
    s = o_raw.shape[0]

    def fn(da, o, hg, gn):
        dos, dhgs = [], []
        dgn = jnp.zeros((1, HG_DK), F32)
        for h in range(HG_H):
            sl = slice(h * HG_DK, (h + 1) * HG_DK)
            oh, gh, dh = o[:, sl], hg[:, sl], da[:, sl]
            r = lax.rsqrt(jnp.mean(oh * oh, axis=-1, keepdims=True) + EPS)
            xhat = oh * r
            sg = _sigmoid(gh)
            dy = dh * (gh * sg)
            dhgs.append(dh * xhat * gn * sg * (1.0 + gh * (1.0 - sg)))
            dgn = dgn + jnp.sum(dy * xhat, axis=0, keepdims=True)
            dxh = dy * gn
            dos.append(r * (dxh - xhat * jnp.mean(dxh * xhat, axis=-1, keepdims=True)))
        return (jnp.concatenate(dos, axis=1), jnp.concatenate(dhgs, axis=1)), (dgn,)

    ins = [_rb(do_a, tm, D), _rb(o_raw, tm, D), _rb(proj, tm, D, 3), _cst(gnorm)]
    return _rows("hgrn2_out_bwd", fn, s, tm, ins, [(D, F32), (D, BF16)], [(1, HG_DK)])


def _log_sigmoid(z):
    return jnp.minimum(z, 0.0) - jnp.log(1.0 + jnp.exp(-jnp.abs(z)))


def _fox_gate_bwd(dct, pff, bias, tm=256):
    s = pff.shape[0]
    nb = s // tm

    def body(d_ref, p_ref, b_ref, dff_ref, db_ref, carry):
        @pl.when(pl.program_id(0) == 0)
        def _():
            carry[...] = jnp.zeros_like(carry)
            db_ref[...] = jnp.zeros_like(db_ref)

        dc = d_ref[...].T
        dlf = _split_dot(_tri(tm, False), dc, 3) + carry[0:1]
        carry[...] = jnp.broadcast_to(dlf[0:1], carry.shape)
        dff = dlf * _sigmoid(-(p_ref[...] + b_ref[...]))
        dff_ref[...] = dff
        db_ref[...] += jnp.sum(dff, axis=0, keepdims=True)

    return pl.pallas_call(
        body, name="fox_gate_bwd", grid=(nb,),
        in_specs=[pl.BlockSpec((LANES, tm), lambda i: (0, nb - 1 - i)),
                  pl.BlockSpec((tm, LANES), lambda i: (nb - 1 - i, 0)), pl.BlockSpec((1, LANES), lambda i: (0, 0))],
        out_specs=[pl.BlockSpec((tm, LANES), lambda i: (nb - 1 - i, 0)), pl.BlockSpec((1, LANES), lambda i: (0, 0))],
        out_shape=[jax.ShapeDtypeStruct((s, LANES), F32), jax.ShapeDtypeStruct((1, LANES), F32)],
        scratch_shapes=[pltpu.VMEM((8, LANES), F32)],
        compiler_params=_cparams(("arbitrary",)),
    )(dct, pff, bias)


def _diag_mask(t):
    r = lax.broadcasted_iota(jnp.int32, (t, t), 0)
    c = lax.broadcasted_iota(jnp.int32, (t, t), 1)
    return r >= c


AUX_ONES = 6


def _pieces(x):
    h = x.astype(BF16)
    r = x - h.astype(F32)
    m = r.astype(BF16)
    return h, m, (r - m.astype(F32)).astype(BF16)


def _lane_put(lane, cols, base):
    out = None
    for i, col in enumerate(cols):
        term = jnp.where(lane == base + i, col.astype(F32), 0.0)
        out = term if out is None else out + term
    return out


def _fox_prep(proj, pff, bias, tm=256):
    s = pff.shape[0]

    def body(q_ref, k_ref, v_ref, p_ref, b_ref, qb_ref, kb_ref, vb_ref, ka_ref, carry):
        @pl.when(pl.program_id(0) == 0)
        def _():
            carry[...] = jnp.zeros_like(carry)

        qb_ref[...] = (q_ref[...] * 0.125).astype(BF16)
        kb_ref[...] = k_ref[...].astype(BF16)
        vb_ref[...] = v_ref[...].astype(BF16)
        lf = _log_sigmoid(p_ref[...] + b_ref[...])
        c = _split_dot(_tri(tm, True), lf, 3) + carry[0:1]
        carry[...] = jnp.broadcast_to(c[tm - 1:tm], carry.shape)
        lane = lax.broadcasted_iota(jnp.int32, (tm, LANES), 1)
        ones = jnp.where((lane >= AUX_ONES) & (lane < AUX_ONES + 6), 1.0, 0.0)
        for p in range(FOX_H // 2):
            aux = ones
            for z in range(2):
                col = jnp.sum(jnp.where(lane == 2 * p + z, c, 0.0), axis=1, keepdims=True)
                aux = aux + _lane_put(lane, _pieces(-col), 3 * z)
            ka_ref[:, p * LANES:(p + 1) * LANES] = aux.astype(BF16)

    row = lambda cb: pl.BlockSpec((tm, D), lambda i: (i, cb))
    return pl.pallas_call(
        body, name="fox_prep", grid=(s // tm,),
        in_specs=[row(4), row(5), row(6), pl.BlockSpec((tm, LANES), lambda i: (i, 0)),
                  pl.BlockSpec((1, LANES), lambda i: (0, 0))],
        out_specs=[row(0)] * 4, out_shape=[jax.ShapeDtypeStruct((s, D), BF16)] * 4,
        scratch_shapes=[pltpu.VMEM((8, LANES), F32)],
        compiler_params=_cparams(("arbitrary",)),
    )(proj, proj, proj, pff, bias)


def _fox_fwd(qb, kb, vb, ka):
    s = qb.shape[0]
    t = min(FOX_T, s)
    nq = s // t

    def body(q_ref, k_ref, v_ref, ka_ref, o_ref, la_ref):
        i = pl.program_id(1)
        lane = lax.broadcasted_iota(jnp.int32, (t, LANES), 1)
        in_a = lane < FOX_D
        q = q_ref[...]
        zero = jnp.zeros_like(q)
        qh = [jnp.where(in_a, q, zero), jnp.where(in_a, zero, q)]
        c_ones = [jnp.where((lane >= 3 * z) & (lane < 3 * z + 3), 1.0, 0.0) for z in range(2)]

        def keys(j):
            rows = pl.ds(pl.multiple_of(j * t, t), t)
            return jnp.concatenate([k_ref[rows, :], ka_ref[rows, :]], axis=1), rows

        dmask = _diag_mask(t)

        def logits(qx, kk, masked):
            e = lax.dot_general(qx, kk, (NT, ((), ())), preferred_element_type=F32)
            return jnp.where(dmask, e, -1e30) if masked else e

        qc = [jnp.concatenate([qh[z], c_ones[z].astype(BF16)], axis=1) for z in range(2)]

        def step(j, carry, masked):
            kk, rows = keys(j)
            vj = v_ref[rows, :]
            scores = [logits(qc[z], kk, masked) for z in range(2)]
            one = jnp.ones_like(vj)
            vh = [jnp.where(in_a, vj, one), jnp.where(in_a, one, vj)]
            out = []
            for z in range(2):
                m, acc = carry[z]
                m_new = jnp.maximum(m, jnp.max(scores[z], axis=1, keepdims=True))
                p = jnp.exp(scores[z] - m_new)
                out.append((m_new, jnp.exp(m - m_new) * acc + _dot(p, vh[z], NN)))
            return tuple(out)

        init = tuple((jnp.full((t, 1), -1e30, F32), jnp.zeros((t, LANES), F32)) for _ in range(2))
        (ma, acc_a), (mb, acc_b) = step(i, lax.fori_loop(0, i, lambda j, c: step(j, c, False), init), True)
        la = jnp.sum(jnp.where(lane == FOX_D, acc_a, 0.0), axis=1, keepdims=True)
        lb = jnp.sum(jnp.where(lane == 0, acc_b, 0.0), axis=1, keepdims=True)
        o_ref[...] = jnp.where(in_a, acc_a / la, acc_b / lb).astype(o_ref.dtype)
        la_ref[...] = (_lane_put(lane, _pieces(-(ma + jnp.log(la))), AUX_ONES)
                       + _lane_put(lane, _pieces(-(mb + jnp.log(lb))), AUX_ONES + 3)).astype(la_ref.dtype)

    blk = pl.BlockSpec((t, LANES), lambda p, i: (i, p))
    whole = pl.BlockSpec((s, LANES), lambda p, i: (0, p))
    return pl.pallas_call(
        body, name="fox_attn_fwd", grid=(FOX_H // 2, nq), in_specs=[blk, whole, whole, whole],
        out_specs=[blk, blk], out_shape=[jax.ShapeDtypeStruct((s, D), BF16)] * 2,
        compiler_params=_cparams(("arbitrary", "arbitrary")),
    )(qb, kb, vb, ka)


def _fox_bwd(qb, kb, vb, ka, ob, laux, dob):
    s = qb.shape[0]
    t = min(FOX_T, s)
    nq = s // t

    def body(q_ref, k_ref, v_ref, ka_ref, o_ref, la_ref, do_ref, dq_ref, dk_ref, dv_ref, dc_ref, dkt, dvt):
        i = pl.program_id(1)

        @pl.when(i == 0)
        def _():
            dkt[...] = jnp.zeros_like(dkt)
            dvt[...] = jnp.zeros_like(dvt)
            dc_ref[...] = jnp.zeros_like(dc_ref)

        lane = lax.broadcasted_iota(jnp.int32, (t, LANES), 1)
        in_a = lane < FOX_D
        q, do, la = q_ref[...], do_ref[...], la_ref[...].astype(F32)
        zero = jnp.zeros_like(q)
        qh = [jnp.where(in_a, q, zero), jnp.where(in_a, zero, q)]
        doh = [jnp.where(in_a, do, zero), jnp.where(in_a, zero, do)]
        qt = [h.astype(F32).T.astype(BF16) for h in qh]
        dot_ = [h.astype(F32).T.astype(BF16) for h in doh]
        prod = do.astype(F32) * o_ref[...].astype(F32)
        qx, dox = [], []
        for z in range(2):
            delta = jnp.sum(jnp.where(in_a if z == 0 else ~in_a, prod, 0.0), axis=1, keepdims=True)
            c_ones = jnp.where((lane >= 3 * z) & (lane < 3 * z + 3), 1.0, 0.0)
            lse_lanes = (lane >= AUX_ONES + 3 * z) & (lane < AUX_ONES + 3 * z + 3)
            qx.append(jnp.concatenate([qh[z], (c_ones + jnp.where(lse_lanes, la, 0.0)).astype(BF16)], axis=1))
            dox.append(jnp.concatenate([doh[z], _lane_put(lane, _pieces(-delta), 3 * z).astype(BF16)], axis=1))
        v_ones = jnp.where(lane < 6, 1.0, 0.0).astype(BF16)
        dmask = _diag_mask(t)

        def step(j, carry, masked):
            rows = pl.ds(pl.multiple_of(j * t, t), t)
            kj, vj = k_ref[rows, :], v_ref[rows, :]
            kk = jnp.concatenate([kj, ka_ref[rows, :]], axis=1)
            vv = jnp.concatenate([vj, v_ones], axis=1)
            out = []
            dk_add, dv_add = None, None
            for z in range(2):
                dq, rsum = carry[z]
                e = lax.dot_general(qx[z], kk, (NT, ((), ())), preferred_element_type=F32)
                if masked:
                    e = jnp.where(dmask, e, -1e30)
                p = jnp.exp(e)
                ds = p * lax.dot_general(dox[z], vv, (NT, ((), ())), preferred_element_type=F32)
                dkz, dvz = _dot(qt[z], ds, NN), _dot(dot_[z], p, NN)
                dk_add = dkz if dk_add is None else dk_add + dkz
                dv_add = dvz if dv_add is None else dv_add + dvz
                dc_ref[0, z, j] += -jnp.sum(ds, axis=0, keepdims=True)
                out.append((dq + _dot(ds, kj, NN), rsum + jnp.sum(ds, axis=1, keepdims=True)))
            dkt[j] += dk_add
            dvt[j] += dv_add
            return tuple(out)

        init = tuple((jnp.zeros((t, LANES), F32), jnp.zeros((t, 1), F32)) for _ in range(2))
        (dq_a, rs_a), (dq_b, rs_b) = step(i, lax.fori_loop(0, i, lambda j, c: step(j, c, False), init), True)
        for z, rs in enumerate((rs_a, rs_b)):
            dc_ref[0, z, i] += jnp.transpose(jnp.broadcast_to(rs, (t, LANES)))[0:1]
        dq_ref[...] = (jnp.where(in_a, dq_a, dq_b) * 0.125).astype(dq_ref.dtype)

        @pl.when(i == nq - 1)
        def _():
            for jb in range(nq):
                dk_ref[jb * t:(jb + 1) * t, :] = dkt[jb].T.astype(dk_ref.dtype)
                dv_ref[jb * t:(jb + 1) * t, :] = dvt[jb].T.astype(dv_ref.dtype)

    blk = pl.BlockSpec((t, LANES), lambda p, i: (i, p))
    whole = pl.BlockSpec((s, LANES), lambda p, i: (0, p))
    return pl.pallas_call(
        body, name="fox_attn_bwd", grid=(FOX_H // 2, nq),
        in_specs=[blk, whole, whole, whole, blk, blk, blk],
        out_specs=[blk, whole, whole, pl.BlockSpec((1, 2, nq, 1, t), lambda p, i: (p, 0, 0, 0, 0))],
        out_shape=[jax.ShapeDtypeStruct((s, D), BF16)] * 3 + [jax.ShapeDtypeStruct((FOX_H // 2, 2, nq, 1, t), F32)],
        scratch_shapes=[pltpu.VMEM((nq, LANES, t), F32), pltpu.VMEM((nq, LANES, t), F32)],
        compiler_params=_cparams(("arbitrary", "arbitrary")),
    )(qb, kb, vb, ka, ob, laux, dob)


def _adamw(name, w, g, m, v, tm=None):
    rows, width = w.shape
    tm = rows if tm is None else tm
    c1 = 1.0 - ADAM_B1 ** ADAM_STEP
    c2 = 1.0 - ADAM_B2 ** ADAM_STEP

    def fn(wb, gb, mb, vb):
        m_new = ADAM_B1 * mb + (1.0 - ADAM_B1) * gb
        v_new = ADAM_B2 * vb + (1.0 - ADAM_B2) * (gb * gb)
        delta = -ADAM_LR * ((m_new / c1) / (jnp.sqrt(v_new / c2) + ADAM_EPS) + ADAM_WD * wb)
        return (delta, m_new, v_new), ()

    ins = [_rb(a, tm, width) for a in (w, g, m, v)]
    return _rows(name, fn, rows, tm, ins, [(width, F32)] * 3)


def _me():
    return lax.axis_index("x"), lax.axis_index("y"), lax.axis_index("c")


def _all_reduce_small(name, block):
    r, n = block.shape

    def body(x_ref, sum_ref, gath, send_sems, recv_sems):
        x, y, c = _me()
        me = 4 * x + 2 * y + c
        gath[me] = x_ref[...]
        sends = []
        for k in range(1, 8):
            px = x ^ ((k >> 2) & 1)
            py = y ^ ((k >> 1) & 1)
            pc = c ^ (k & 1)
            sends.append(pltpu.make_async_remote_copy(
                src_ref=x_ref, dst_ref=gath.at[me], send_sem=send_sems.at[k - 1], recv_sem=recv_sems.at[k - 1],
                device_id=(px, py, pc), device_id_type=MESH))
        for cp in sends:
            cp.start()
        for k in range(1, 8):
            peer = me ^ k
            pltpu.make_async_remote_copy(
                src_ref=x_ref, dst_ref=gath.at[peer], send_sem=send_sems.at[k - 1], recv_sem=recv_sems.at[k - 1],
                device_id=(x, y, c), device_id_type=MESH).wait_recv()
        for cp in sends:
            cp.wait_send()
        acc = gath[0]
        for d in range(1, 8):
            acc = acc + gath[d]
        sum_ref[...] = acc

    vm = pl.BlockSpec(memory_space=pltpu.VMEM)
    return pl.pallas_call(
        body, name=name, in_specs=[vm], out_specs=vm, out_shape=jax.ShapeDtypeStruct((r, n), F32),
        scratch_shapes=[pltpu.VMEM((8, r, n), F32), pltpu.SemaphoreType.DMA((7,)), pltpu.SemaphoreType.DMA((7,))],
    )(block)


WD_EXT_ROWS = 736


def _remote(src, dst, send_sems, recv_sems, k, to):
    return pltpu.make_async_remote_copy(src_ref=src, dst_ref=dst, send_sem=send_sems.at[k], recv_sem=recv_sems.at[k],
                                        device_id=to, device_id_type=MESH)


def _all_gather8_multi(name, blocks):
    nt = len(blocks)

    def body(*refs):
        x_refs, out_refs, send_sems, recv_sems = refs[:nt], refs[nt:2 * nt], refs[-2], refs[-1]
        x, y, c = _me()
        me, sibling = (x, y, c), (x, y, 1 - c)
        chips = [(1 - x, y), (x, 1 - y), (1 - x, 1 - y)]
        slot = lambda q, p: out_refs[q].at[4 * p[0] + 2 * p[1] + p[2]]

        def copies(k, blk, to, from_input=False):
            return [_remote(x_refs[q] if from_input else slot(q, blk), slot(q, blk), send_sems, recv_sems, k * nt + q, to)
                    for q in range(nt)]

        first = copies(0, me, sibling, True)
        for j, chip in enumerate(chips):
            first += copies(1 + j, me, (*chip, c), True)
        for cp in first:
            cp.start()
        passed = []
        for j, chip in enumerate(chips):
            for cp in copies(1 + j, (*chip, c), me):
                cp.wait_recv()
            fwd = copies(4 + j, (*chip, c), sibling)
            for cp in fwd:
                cp.start()
            passed += fwd
        for cp in copies(0, sibling, me):
            cp.wait_recv()
        back = copies(7, sibling, sibling)
        for cp in back:
            cp.start()
        for j, chip in enumerate(chips):
            for cp in copies(4 + j, (*chip, 1 - c), me):
                cp.wait_recv()
        for cp in copies(7, me, me):
            cp.wait_recv()
        for cp in first + passed + back:
            cp.wait_send()

    return pl.pallas_call(
        body, name=name, in_specs=[ANY] * nt, out_specs=[ANY] * nt,
        out_shape=[jax.ShapeDtypeStruct((8,) + b.shape, b.dtype) for b in blocks],
        scratch_shapes=[pltpu.SemaphoreType.DMA((8 * nt,)), pltpu.SemaphoreType.DMA((8 * nt,))],
    )(*blocks)


def _swap_halves_multi(name, gs):
    nt = len(gs)
    n_chip = gs[0].shape[0]

    def body(*refs):
        g_refs, got_refs, send_sems, recv_sems = refs[:nt], refs[nt:2 * nt], refs[-2], refs[-1]
        x, y, c = _me()
        cps = [_remote(g_refs[q].at[j, 1 - c], got_refs[q].at[j], send_sems, recv_sems, q * n_chip + j, (x, y, 1 - c))
               for q in range(nt) for j in range(n_chip)]
        for cp in cps:
            cp.start()
        for cp in cps:
            cp.wait()

    return pl.pallas_call(
        body, name=name, in_specs=[ANY] * nt, out_specs=[ANY] * nt,
        out_shape=[jax.ShapeDtypeStruct((g.shape[0],) + g.shape[2:], g.dtype) for g in gs],
        scratch_shapes=[pltpu.SemaphoreType.DMA((nt * n_chip,)), pltpu.SemaphoreType.DMA((nt * n_chip,))],
    )(*gs)


def _swap_sibling_multi(name, xs):
    nt = len(xs)

    def body(*refs):
        x_refs, out_refs, send_sems, recv_sems = refs[:nt], refs[nt:2 * nt], refs[-2], refs[-1]
        x, y, c = _me()
        cps = [_remote(x_refs[q], out_refs[q], send_sems, recv_sems, q, (x, y, 1 - c)) for q in range(nt)]
        for cp in cps:
            cp.start()
        for cp in cps:
            cp.wait()

    return pl.pallas_call(
        body, name=name, in_specs=[ANY] * nt, out_specs=[ANY] * nt,
        out_shape=[jax.ShapeDtypeStruct(a.shape, a.dtype) for a in xs],
        scratch_shapes=[pltpu.SemaphoreType.DMA((nt,)), pltpu.SemaphoreType.DMA((nt,))],
    )(*xs)


def _chip_exchange_multi(name, ps):
    nt = len(ps)

    def body(*refs):
        p_refs, out_refs, bounce_refs = refs[:nt], refs[nt:2 * nt], refs[2 * nt:3 * nt]
        send_sems, recv_sems = refs[-2], refs[-1]
        x, y, c = _me()
        my_chip = 2 * x + y
        sibling = (x, y, 1 - c)
        chips = [(1 - x, y), (x, 1 - y), (1 - x, 1 - y)]
        cp = lambda k, q, src, dst, to: _remote(src, dst, send_sems, recv_sems, k * nt + q, to)
        sends = [cp(k, q, p_refs[q].at[2 * px + py], out_refs[q].at[my_chip], (px, py, c))
                 for k, (px, py) in enumerate(chips) for q in range(nt)]
        sends += [cp(3, q, p_refs[q].at[my_chip], bounce_refs[q], sibling) for q in range(nt)]
        for s_ in sends:
            s_.start()
        backs = []
        for q in range(nt):
            cp(3, q, p_refs[q].at[my_chip], bounce_refs[q], sibling).wait_recv()
            backs.append(cp(4, q, bounce_refs[q], out_refs[q].at[my_chip], sibling))
            backs[-1].start()
        for k, (px, py) in enumerate(chips):
            for q in range(nt):
                cp(k, q, p_refs[q].at[my_chip], out_refs[q].at[2 * px + py], (px, py, c)).wait_recv()
        for q in range(nt):
            cp(4, q, bounce_refs[q], out_refs[q].at[my_chip], sibling).wait_recv()
        for s_ in sends + backs:
            s_.wait_send()

    outs = pl.pallas_call(
        body, name=name, in_specs=[ANY] * nt, out_specs=[ANY] * (2 * nt),
        out_shape=[jax.ShapeDtypeStruct(p.shape, p.dtype) for p in ps]
        + [jax.ShapeDtypeStruct(p.shape[1:], p.dtype) for p in ps],
        scratch_shapes=[pltpu.SemaphoreType.DMA((5 * nt,)), pltpu.SemaphoreType.DMA((5 * nt,))],
    )(*ps)
    return outs[:nt]


def _row_tile(m):
    return m if m <= 384 else 128


def _add2_rows(name, a, b):
    n4, m, n = a.shape
    tm = _row_tile(m)
    out = _rows(name, lambda p, q: ((p + q,), ()), n4 * m, tm,
                [_rb(a.reshape(n4 * m, n), tm, n), _rb(b.reshape(n4 * m, n), tm, n)], [(n, BF16)])[0]
    return out.reshape(n4, m, n)


def _add4_rows(name, p):
    _, m, n = p.shape
    tm = _row_tile(m)
    nb = m // tm
    flat = p.reshape(4 * m, n)
    ins = [(flat, (tm, n), (lambda i, j=j: (j * nb + i, 0))) for j in range(4)]
    f32 = lambda v: v.astype(F32)
    return _rows(name, lambda a, b, c, d: ((((f32(a) + f32(b)) + f32(c)) + f32(d),), ()), m, tm, ins, [(n, F32)])[0]


def _in_proj_with_gather(n1, w_main, blocks):
    m, k = n1.shape
    n = w_main.shape[1]
    tm, tn = min(1024, m), 1024
    gi, gj = m // tm, n // tn
    last, mid = gi * gj - 1, (3 * gi * gj) // 4
    nt = len(blocks)

    def body(*refs):
        a_ref, b_ref, x_refs, o_ref, out_refs = refs[0], refs[1], refs[2:2 + nt], refs[2 + nt], refs[3 + nt:3 + 2 * nt]
        send_sems, recv_sems = refs[-2], refs[-1]
        step = pl.program_id(0) * gj + pl.program_id(1)
        x, y, c = _me()
        me, sibling = (x, y, c), (x, y, 1 - c)
        chips = [(1 - x, y), (x, 1 - y), (1 - x, 1 - y)]
        slot = lambda q, p: out_refs[q].at[4 * p[0] + 2 * p[1] + p[2]]

        def copies(kk, blk, to, from_input=False):
            return [_remote(x_refs[q] if from_input else slot(q, blk), slot(q, blk), send_sems, recv_sems, kk * nt + q, to)
                    for q in range(nt)]

        def first():
            out = copies(0, me, sibling, True)
            for j, chip in enumerate(chips):
                out += copies(1 + j, me, (*chip, c), True)
            return out

        @pl.when(step == 0)
        def _():
            for cp in first():
                cp.start()

        @pl.when(step == mid)
        def _():
            for j, chip in enumerate(chips):
                for cp in copies(1 + j, (*chip, c), me):
                    cp.wait_recv()
                for cp in copies(4 + j, (*chip, c), sibling):
                    cp.start()
            for cp in copies(0, sibling, me):
                cp.wait_recv()
            for cp in copies(7, sibling, sibling):
                cp.start()

        o_ref[...] = _dot(a_ref[...], b_ref[...], NN)

        @pl.when(step == last)
        def _():
            for j, chip in enumerate(chips):
                for cp in copies(4 + j, (*chip, 1 - c), me):
                    cp.wait_recv()
            for cp in copies(7, me, me):
                cp.wait_recv()
            sent = first() + copies(7, sibling, sibling)
            for j, chip in enumerate(chips):
                sent += copies(4 + j, (*chip, c), sibling)
            for cp in sent:
                cp.wait_send()

    outs = pl.pallas_call(
        body, name="in_proj", grid=(gi, gj),
        in_specs=[pl.BlockSpec((tm, k), lambda i, j: (i, 0)), pl.BlockSpec((k, tn), lambda i, j: (0, j))] + [ANY] * nt,
        out_specs=[pl.BlockSpec((tm, tn), lambda i, j: (i, j))] + [ANY] * nt,
        out_shape=[jax.ShapeDtypeStruct((m, n), F32)] + [jax.ShapeDtypeStruct((8,) + b.shape, b.dtype) for b in blocks],
        scratch_shapes=[pltpu.SemaphoreType.DMA((8 * nt,)), pltpu.SemaphoreType.DMA((8 * nt,))],
        compiler_params=_cparams(("arbitrary", "arbitrary")),
    )(n1, w_main, *blocks)
    return outs[0], outs[1:]


def _weight_halves(w_in, w_a, w_b, w_out, w_up, w_down, conv_w):
    c = lax.axis_index("c")
    bits = lax.bitcast_convert_type(conv_w, BF16).reshape(-1)
    extra = jnp.zeros(((WD_EXT_ROWS - W_DOWN_SHARD) * D,), BF16).at[:bits.shape[0]].set(bits)
    wd_ext = jnp.concatenate([w_down.astype(BF16), extra.reshape(-1, D)], axis=0)
    shards = [w_in.astype(BF16), w_a.astype(BF16), w_b.astype(BF16), w_out.astype(BF16), w_up.astype(BF16), wd_ext]
    return [lax.dynamic_slice_in_dim(t, c * (t.shape[0] // 2), t.shape[0] // 2, axis=0) for t in shards]


def _unpack_w_in(gathered):
    wi = gathered.reshape(N_CHIP, D, W_IN_SHARD).transpose(1, 0, 2).reshape(D, N_CHIP * W_IN_SHARD)
    w_main = jnp.concatenate([wi[:, :FF_COL], wi[:, FF_COL + FOX_H:]], axis=1)
    return w_main, jnp.pad(wi[:, FF_COL:FF_COL + FOX_H], ((0, 0), (0, LANES - FOX_H)))


def _unpack_later_weights(gathered):
    full = [g.reshape((N_CHIP, 2 * g.shape[1]) + g.shape[2:]) for g in gathered]
    wa, wb, wo = (full[i].reshape(D, D) for i in (0, 1, 2))
    wu = full[3].transpose(1, 0, 2).reshape(D, 2 * D_FF)
    wd = full[4][:, :W_DOWN_SHARD].reshape(D_FF, D)
    n_bits = 3 * W_UP_SHARD * 2
    cw_bits = full[4][:, W_DOWN_SHARD:].reshape(N_CHIP, -1)[:, :n_bits].reshape(N_CHIP, 3, W_UP_SHARD, 2)
    cw = lax.bitcast_convert_type(cw_bits, F32).transpose(1, 0, 2).reshape(3, 2 * D_FF)
    return wa, wb, wo, wu, wd, cw


def _chip_sums(tag, per_chip):
    c = lax.axis_index("c")
    gs = [t.reshape(N_CHIP, 2, t.shape[1] // 2, t.shape[2]) for t in per_chip]
    got = _swap_halves_multi("grad_swap_halves_" + tag, gs)
    return [_add2_rows("grad_chip_sum_%s%d" % (tag, q), lax.dynamic_index_in_dim(g, c, axis=1, keepdims=False), s_)
            for q, (g, s_) in enumerate(zip(gs, got))]


def _late_weight_chip_sums(d_a, d_b, d_o, d_u, d_d):
    return _chip_sums("late", [d_a.reshape(N_CHIP, -1, D), d_b.reshape(N_CHIP, -1, D), d_o.reshape(N_CHIP, -1, D),
                               d_u.reshape(D, N_CHIP, W_UP_SHARD).transpose(1, 0, 2), d_d.reshape(N_CHIP, -1, D)])


def _finish_grads(d_main, d_ff, late_pieces):
    c = lax.axis_index("c")
    d_in = jnp.concatenate(d_main[:7] + [d_ff[:, :FOX_H]] + d_main[7:], axis=1)
    sums = _chip_sums("w_in", [d_in.reshape(D, N_CHIP, W_IN_SHARD).transpose(1, 0, 2)])
    pieces = list(_chip_exchange_multi("grad_chip_exchange", sums)) + list(late_pieces)
    mine = [_add4_rows("grad_sum_chips_%d" % q, p) for q, p in enumerate(pieces)]
    other = _swap_sibling_multi("grad_share_half", mine)
    return [jnp.concatenate([jnp.where(c == 0, a, b), jnp.where(c == 0, b, a)], axis=0) for a, b in zip(mine, other)]


def _local_step(x, target, norm_mix, fox_f_bias, hg_lb_logits, hg_norm, norm_ffn, conv_b, norm_final,
                w_main, w_ff, later):
    bias = jnp.pad(fox_f_bias, ((0, 0), (0, LANES - FOX_H)))

    n1, n1t = _rms_fwd("norm_mix_fwd", x, norm_mix)
    if len(later) == 5:
        proj, gathered = _in_proj_with_gather(n1, w_main, later)
        wa, wb, wo, wu, wd, conv_w = _unpack_later_weights(gathered)
    else:
        proj = _mm("in_proj", n1, w_main, "nn", F32, 1024, 1024, D)
        wa, wb, wo, wu, wd, conv_w = later
    conv_w8 = jnp.pad(conv_w, ((0, 5), (0, 0)))
    pff = _mm("in_proj_forget", n1, w_ff, "nn", F32, 1024, LANES, D)
    qb, kb, vb, ka = _fox_prep(proj, pff, bias)
    o_b, laux = _fox_fwd(qb, kb, vb, ka)
    o_raw, states = _hg_fwd(proj, hg_lb_logits)
    o_a = _hg_post_fwd(o_raw, proj, hg_norm)
    pa = _mm("branch_a", o_a, wa, "nn", F32, 1024, 1024, D)
    pb = _mm("branch_b", o_b, wb, "nn", F32, 1024, 1024, D)
    merged = _merge_fwd(pa, pb, proj)
    h1 = _mm("out_proj", merged, wo, "nn", F32, 1024, 1024, D, res=x)
    n2, n2t = _rms_fwd("norm_ffn_fwd", h1, norm_ffn)
    u = _mm("ffn_up", n2, wu, "nn", F32, 1024, W_UP_SHARD, D)
    act, gelu_gate, dact_dgate = _convglu_fwd(u, conv_w8, conv_b)
    h2 = _mm("ffn_down", act, wd, "nn", F32, 512, 1024, D_FF, res=h1)
    (dh2,), (d_norm_final, loss_row) = _final(h2, target, norm_final)

    dact = _mm("ffn_down_dx", dh2, wd, "nt", BF16, 1024, D_FF, D)
    d_wd = _mm("ffn_down_dw", act, dh2, "tn", F32, D_FF // 2, 1024, DW_TK // 2)
    (du,), (d_conv_w8, d_conv_b) = _convglu_bwd(u, dact, gelu_gate, dact_dgate, conv_w8)
    dn2 = _mm("ffn_up_dx", du, wu, "nt", F32, 1024, 1024, W_UP_SHARD)
    d_wu = _mm("ffn_up_dw", n2t, du, "nn", F32, 1024, W_UP_SHARD, DW_TK)
    (dh1,), (d_norm_ffn,) = _rms_bwd("norm_ffn_bwd", h1, norm_ffn, [dn2], dh2)

    dmerged = _mm("out_proj_dx", dh1, wo, "nt", F32, 1024, 1024, D)
    d_wo = _mm("out_proj_dw", merged, dh1, "tn", F32, 1024, 1024, DW_TK)
    dpa, dpb, dga, dgb = _merge_bwd(dmerged, pa, pb, proj)
    do_a = _mm("branch_a_dx", dpa, wa, "nt", F32, 1024, 1024, D)
    do_b = _mm("branch_b_dx", dpb, wb, "nt", BF16, 1024, 1024, D)
    d_wa = _mm("branch_a_dw", o_a, dpa, "tn", F32, 1024, 1024, DW_TK)
    d_wb = _mm("branch_b_dw", o_b, dpb, "tn", F32, 1024, 1024, DW_TK)

    (do_raw, dhg), (d_hg_norm,) = _hg_post_bwd(do_a, o_raw, proj, hg_norm)
    dhq, dhf, dhi, d_lb_logits = _hg_bwd(proj, hg_lb_logits, states, do_raw)

    dfq, dfk, dfv, dcrow = _fox_bwd(qb, kb, vb, ka, o_b, laux, do_b)
    dct = jnp.pad(dcrow.reshape(FOX_H, x.shape[0]), ((0, LANES - FOX_H), (0, 0)))
    dff, d_bias = _fox_gate_bwd(dct, pff, bias)

    pieces = [dhq, dhf, dhi, dhg, dfq, dfk, dfv, dga, dgb]
    if len(later) == 5:
        dn1, late = _mm_sum_nt("in_proj_dx", pieces, w_main, (dff, w_ff), 1024, 1024,
                               exchange=_late_weight_chip_sums(d_wa, d_wb, d_wo, d_wu, d_wd))
    else:
        dn1, late = _mm_sum_nt("in_proj_dx", pieces, w_main, (dff, w_ff), 1024, 1024), (d_wa, d_wb, d_wo, d_wu, d_wd)
    d_w_main = [_mm("in_proj_dw_%d" % i, n1t, p, "nn", F32, 1024, 1024, DW_TK) for i, p in enumerate(pieces)]
    d_w_ff = _mm("in_proj_forget_dw", n1t, dff, "nn", F32, 1024, LANES, DW_TK)
    (dx,), (d_norm_mix,) = _rms_bwd("norm_mix_bwd", x, norm_mix, [dn1], dh1)

    small = dict(norm_mix=d_norm_mix, fox_f_bias=d_bias[:, :FOX_H], hg_lb_logits=d_lb_logits, hg_norm=d_hg_norm,
                 norm_ffn=d_norm_ffn, conv_b=d_conv_b, norm_final=d_norm_final, conv_w=d_conv_w8[:3], loss=loss_row)
    big = (d_w_main, d_w_ff) + tuple(late)
    return dx, small, big


SMALL_KEYS = ("norm_mix", "fox_f_bias", "hg_lb_logits", "hg_norm", "norm_ffn", "conv_b", "norm_final")


def _pack_small(parts):
    rows, layout = [], []
    for key, arr in parts:
        flat = arr.reshape(-1)
        n = flat.shape[0]
        nr = -(-n // LANES)
        rows.append(jnp.pad(flat, (0, nr * LANES - n)).reshape(nr, LANES))
        layout.append((key, arr.shape, n, nr))
    packed = jnp.concatenate(rows, axis=0)
    pad = -packed.shape[0] % 8
    return jnp.pad(packed, ((0, pad), (0, 0))), layout


def _unpack_small(packed, layout):
    out, r0 = {}, 0
    for key, shape, n, nr in layout:
        out[key] = packed[r0:r0 + nr].reshape(-1)[:n].reshape(shape)
        r0 += nr
    return out


def kernel(x, norm_mix, w_in, fox_f_bias, hg_lb_logits, hg_norm, w_branch_a, w_branch_b, w_out, norm_ffn, w_up, conv_w, conv_b, w_down, norm_final, loss_target, m_norm_mix, m_w_in, m_fox_f_bias, m_hg_lb_logits, m_hg_norm, m_w_branch_a, m_w_branch_b, m_w_out, m_norm_ffn, m_w_up, m_conv_w, m_conv_b, m_w_down, m_norm_final, v_norm_mix, v_w_in, v_fox_f_bias, v_hg_lb_logits, v_hg_norm, v_w_branch_a, v_w_branch_b, v_w_out, v_norm_ffn, v_w_up, v_conv_w, v_conv_b, v_w_down, v_norm_final):
    chip = 2 * lax.axis_index("x") + lax.axis_index("y")
    halves = _weight_halves(w_in[0], w_branch_a[0], w_branch_b[0], w_out[0], w_up[0], w_down[0], conv_w[0])
    w_main, w_ff = _unpack_w_in(_all_gather8_multi("all_gather_w_in", halves[:1])[0])
    dx, small, big = _local_step(
        x[0], loss_target[0], norm_mix, fox_f_bias, hg_lb_logits, hg_norm, norm_ffn, conv_b,
        norm_final.reshape(1, D), w_main, w_ff, halves[1:])

    packed, layout = _pack_small([(k, small[k]) for k in SMALL_KEYS + ("conv_w", "loss")])
    red = _unpack_small(_all_reduce_small("all_reduce_small", packed), layout)
    loss = red["loss"][0, 0]
    g_conv_w = lax.dynamic_slice_in_dim(red["conv_w"], chip * W_UP_SHARD, W_UP_SHARD, axis=1)

    g_big = _finish_grads(big[0], big[1], big[2:])

    names = ["norm_mix", "w_in", "fox_f_bias", "hg_lb_logits", "hg_norm", "w_branch_a", "w_branch_b", "w_out",
             "norm_ffn", "w_up", "conv_w", "conv_b", "w_down", "norm_final"]
    weights = dict(norm_mix=norm_mix, w_in=w_in, fox_f_bias=fox_f_bias, hg_lb_logits=hg_lb_logits, hg_norm=hg_norm,
                   w_branch_a=w_branch_a, w_branch_b=w_branch_b, w_out=w_out, norm_ffn=norm_ffn, w_up=w_up,
                   conv_w=conv_w, conv_b=conv_b, w_down=w_down, norm_final=norm_final)
    ms = dict(norm_mix=m_norm_mix, w_in=m_w_in, fox_f_bias=m_fox_f_bias, hg_lb_logits=m_hg_lb_logits,
              hg_norm=m_hg_norm, w_branch_a=m_w_branch_a, w_branch_b=m_w_branch_b, w_out=m_w_out,
              norm_ffn=m_norm_ffn, w_up=m_w_up, conv_w=m_conv_w, conv_b=m_conv_b, w_down=m_w_down,
              norm_final=m_norm_final)
    vs = dict(norm_mix=v_norm_mix, w_in=v_w_in, fox_f_bias=v_fox_f_bias, hg_lb_logits=v_hg_lb_logits,
              hg_norm=v_hg_norm, w_branch_a=v_w_branch_a, w_branch_b=v_w_branch_b, w_out=v_w_out,
              norm_ffn=v_norm_ffn, w_up=v_w_up, conv_w=v_conv_w, conv_b=v_conv_b, w_down=v_w_down,
              norm_final=v_norm_final)

    grads, deltas, new_m, new_v = {}, {}, {}, {}
    big_names = ["w_in", "w_branch_a", "w_branch_b", "w_out", "w_up", "w_down"]
    for name, g2 in zip(big_names, g_big):
        shape = weights[name].shape
        rows = g2.shape[0]
        d_, m_, v_ = _adamw("adamw_" + name, weights[name][0], g2, ms[name][0], vs[name][0], tm=rows // 8)
        grads[name], deltas[name], new_m[name], new_v[name] = (a.reshape(shape) for a in (g2, d_, m_, v_))
    shape = conv_w.shape
    d_, m_, v_ = _adamw("adamw_conv_w", conv_w[0], g_conv_w, m_conv_w[0], v_conv_w[0])
    grads["conv_w"], deltas["conv_w"], new_m["conv_w"], new_v["conv_w"] = (
        a.reshape(shape) for a in (g_conv_w, d_, m_, v_))
    gs = {k: red[k].reshape(weights[k].shape) for k in SMALL_KEYS}
    pw, lay = _pack_small([(k, weights[k]) for k in SMALL_KEYS])
    pg, _ = _pack_small([(k, gs[k]) for k in SMALL_KEYS])
    pm, _ = _pack_small([(k, ms[k]) for k in SMALL_KEYS])
    pv, _ = _pack_small([(k, vs[k]) for k in SMALL_KEYS])
    d_, m_, v_ = (_unpack_small(a, lay) for a in _adamw("adamw_small", pw, pg, pm, pv))
    for k in SMALL_KEYS:
        grads[k], deltas[k], new_m[k], new_v[k] = gs[k], d_[k], m_[k], v_[k]

    return (loss, dx[None], *[grads[n] for n in names], *[deltas[n] for n in names],
            *[new_m[n] for n in names], *[new_v[n] for n in names])
```

```python
import jax
import jax.numpy as jnp
from jax import lax
from jax.experimental import pallas as pl
from jax.experimental.pallas import tpu as pltpu

F32 = jnp.float32
BF16 = jnp.bfloat16

D = 1024
HG_H, HG_DK = 8, 128
FOX_H, FOX_D = 16, 64
D_FF = 2816
EPS = 1e-6
N_CHIP = 4
LANES = 128
W_IN_SHARD = 2308
W_UP_SHARD = 1408
W_DOWN_SHARD = 704
FF_COL = 7168
ADAM_LR, ADAM_B1, ADAM_B2, ADAM_EPS, ADAM_WD, ADAM_STEP = 0.001, 0.9, 0.999, 1e-08, 0.01, 10

HG_C = 16
HG_T = 1024
HG_UNROLL = 16
HG_UNROLL_BWD = 4
FOX_T = 1024
DW_TK = 2048
VMEM_LIMIT = 56 * 1024 * 1024
MESH = pl.DeviceIdType.MESH
ANY = pl.BlockSpec(memory_space=pl.ANY)


def _cparams(sem):
    return pltpu.CompilerParams(dimension_semantics=sem, vmem_limit_bytes=VMEM_LIMIT)


def _sigmoid(x):
    return 1.0 / (1.0 + jnp.exp(-x))


def _dot(a, b, dims):
    return lax.dot_general(a.astype(BF16), b.astype(BF16), (dims, ((), ())), preferred_element_type=F32)


NN = ((1,), (0,))
NT = ((1,), (1,))
TN = ((0,), (0,))


def _split_dot(tri, x, parts, dims=NN):
    acc = None
    r = x
    for _ in range(parts):
        p = r.astype(BF16)
        t = lax.dot_general(tri, p, (dims, ((), ())), preferred_element_type=F32)
        acc = t if acc is None else acc + t
        r = r - p.astype(F32)
    return acc


def _rb(arr, tm, width, cb=0):
    return (arr, (tm, width), lambda i: (i, cb))


def _cst(arr):
    return (arr, arr.shape, lambda i: (0,) * arr.ndim)


def _rows(name, fn, n_rows, tm, ins, outs, accs=()):
    n_in, n_out, n_acc = len(ins), len(outs), len(accs)
    nb = n_rows // tm

    def body(*refs):
        vals = [r[...] for r in refs[:n_in]]
        o, a = fn(*vals)
        for r, v in zip(refs[n_in:n_in + n_out], o):
            r[...] = v.astype(r.dtype)
        if n_acc:
            acc_refs = refs[n_in + n_out:]

            @pl.when(pl.program_id(0) == 0)
            def _():
                for r in acc_refs:
                    r[...] = jnp.zeros_like(r)

            for r, v in zip(acc_refs, a):
                r[...] += v

    in_specs = [pl.BlockSpec(bs, im) for (_, bs, im) in ins]
    out_specs = [pl.BlockSpec((tm, w), lambda i: (i, 0)) for (w, _) in outs]
    out_specs += [pl.BlockSpec((r, w), lambda i: (0, 0)) for (r, w) in accs]
    out_shape = [jax.ShapeDtypeStruct((n_rows, w), dt) for (w, dt) in outs]
    out_shape += [jax.ShapeDtypeStruct((r, w), F32) for (r, w) in accs]
    res = pl.pallas_call(
        body, name=name, grid=(nb,), in_specs=in_specs, out_specs=out_specs, out_shape=out_shape,
        compiler_params=_cparams(("arbitrary",)),
    )(*[a for a, _, _ in ins])
    return (res[:n_out], res[n_out:]) if n_acc else res


def _mm(name, a, b, mode, out_dtype, tm, tn, tk, res=None):
    if mode == "nn":
        (m, k), n = a.shape, b.shape[1]
    elif mode == "nt":
        (m, k), n = a.shape, b.shape[0]
    else:
        (k, m), n = a.shape, b.shape[1]
    tm, tn, tk = min(tm, m), min(tn, n), min(tk, k)
    assert m % tm == 0 and n % tn == 0 and k % tk == 0, (name, m, n, k, tm, tn, tk)
    if mode == "nn":
        a_spec = pl.BlockSpec((tm, tk), lambda i, j, kk: (i, kk))
        b_spec = pl.BlockSpec((tk, tn), lambda i, j, kk: (kk, j))
        dims = NN
    elif mode == "nt":
        a_spec = pl.BlockSpec((tm, tk), lambda i, j, kk: (i, kk))
        b_spec = pl.BlockSpec((tn, tk), lambda i, j, kk: (j, kk))
        dims = NT
    else:
        a_spec = pl.BlockSpec((tk, tm), lambda i, j, kk: (kk, i))
        b_spec = pl.BlockSpec((tk, tn), lambda i, j, kk: (kk, j))
        dims = TN
    nk = k // tk
    has_res = res is not None
    acc_in_out = out_dtype == F32 and not has_res

    def body(*refs):
        a_ref, b_ref = refs[0], refs[1]
        r_ref = refs[2] if has_res else None
        o_ref = refs[3] if has_res else refs[2]
        part = _dot(a_ref[...], b_ref[...], dims)

        def finish(val):
            if has_res:
                val = val + r_ref[...]
            o_ref[...] = val.astype(o_ref.dtype)

        if nk == 1:
            finish(part)
        elif acc_in_out:
            kk = pl.program_id(2)

            @pl.when(kk == 0)
            def _():
                o_ref[...] = part

            @pl.when(kk > 0)
            def _():
                o_ref[...] += part
        else:
            acc_ref = refs[-1]
            kk = pl.program_id(2)

            @pl.when(kk == 0)
            def _():
                acc_ref[...] = part

            @pl.when(kk > 0)
            def _():
                acc_ref[...] += part

            @pl.when(kk == nk - 1)
            def _():
                finish(acc_ref[...])

    in_specs = [a_spec, b_spec]
    args = [a, b]
    if has_res:
        in_specs.append(pl.BlockSpec((tm, tn), lambda i, j, kk: (i, j)))
        args.append(res)
    return pl.pallas_call(
        body, name=name, grid=(m // tm, n // tn, nk), in_specs=in_specs,
        out_specs=pl.BlockSpec((tm, tn), lambda i, j, kk: (i, j)),
        out_shape=jax.ShapeDtypeStruct((m, n), out_dtype),
        scratch_shapes=[pltpu.VMEM((tm, tn), F32)] if nk > 1 and not acc_in_out else [],
        compiler_params=_cparams(("arbitrary", "arbitrary", "arbitrary")),
    )(*args)


def _mm_sum_nt(name, pieces, w, extra, tm, tn, exchange=()):
    n_p = len(pieces)
    m, k = pieces[0].shape
    n = w.shape[0]
    xa, xb = extra
    ke = xa.shape[1]
    tm, tn = min(tm, m), min(tn, n)
    nx = len(exchange)
    n_steps = (m // tm) * (n // tn) * (n_p + 1)

    def exchange_steps(refs):
        e_refs = refs[n_p + 3:n_p + 3 + nx]
        out_refs, bounce_refs = refs[n_p + 4 + nx:n_p + 4 + 2 * nx], refs[n_p + 4 + 2 * nx:n_p + 4 + 3 * nx]
        send_sems, recv_sems = refs[-2], refs[-1]
        step = (pl.program_id(0) * (n // tn) + pl.program_id(1)) * (n_p + 1) + pl.program_id(2)
        x, y, c = _me()
        my_chip = 2 * x + y
        sibling = (x, y, 1 - c)
        chips = [(1 - x, y), (x, 1 - y), (1 - x, 1 - y)]
        cp = lambda kx, q, src, dst, to: _remote(src, dst, send_sems, recv_sems, kx * nx + q, to)
        sends = lambda: ([cp(kx, q, e_refs[q].at[2 * px + py], out_refs[q].at[my_chip], (px, py, c))
                          for kx, (px, py) in enumerate(chips) for q in range(nx)]
                         + [cp(3, q, e_refs[q].at[my_chip], bounce_refs[q], sibling) for q in range(nx)])
        backs = lambda: [cp(4, q, bounce_refs[q], out_refs[q].at[my_chip], sibling) for q in range(nx)]

        @pl.when(step == 0)
        def _():
            for s_ in sends():
                s_.start()

        @pl.when(step == n_steps // 2)
        def _():
            for q in range(nx):
                cp(3, q, e_refs[q].at[my_chip], bounce_refs[q], sibling).wait_recv()
            for s_ in backs():
                s_.start()

        @pl.when(step == n_steps - 1)
        def _():
            for kx, (px, py) in enumerate(chips):
                for q in range(nx):
                    cp(kx, q, e_refs[q].at[my_chip], out_refs[q].at[2 * px + py], (px, py, c)).wait_recv()
            for q in range(nx):
                cp(4, q, bounce_refs[q], out_refs[q].at[my_chip], sibling).wait_recv()
            for s_ in sends() + backs():
                s_.wait_send()

    def body(*refs):
        p_refs, w_ref, xa_ref, xb_ref, o_ref = refs[:n_p], refs[n_p], refs[n_p + 1], refs[n_p + 2], refs[n_p + 3 + nx]
        if nx:
            exchange_steps(refs)
        kk = pl.program_id(2)

        @pl.when(kk == 0)
        def _():
            o_ref[...] = _dot(p_refs[0][...], w_ref[...], NT)

        for i in range(1, n_p):
            @pl.when(kk == i)
            def _(i=i):
                o_ref[...] += _dot(p_refs[i][...], w_ref[...], NT)

        @pl.when(kk == n_p)
        def _():
            o_ref[...] += _dot(xa_ref[...], xb_ref[...], NT)

    in_specs = [pl.BlockSpec((tm, k), lambda i, j, kk: (i, 0)) for _ in range(n_p)]
    in_specs.append(pl.BlockSpec((tn, k), lambda i, j, kk: (j, jnp.minimum(kk, n_p - 1))))
    in_specs += [pl.BlockSpec((tm, ke), lambda i, j, kk: (i, 0)), pl.BlockSpec((tn, ke), lambda i, j, kk: (j, 0))]
    outs = pl.pallas_call(
        body, name=name, grid=(m // tm, n // tn, n_p + 1), in_specs=in_specs + [ANY] * nx,
        out_specs=[pl.BlockSpec((tm, tn), lambda i, j, kk: (i, j))] + [ANY] * (2 * nx),
        out_shape=[jax.ShapeDtypeStruct((m, n), F32)] + [jax.ShapeDtypeStruct(e.shape, e.dtype) for e in exchange]
        + [jax.ShapeDtypeStruct(e.shape[1:], e.dtype) for e in exchange],
        scratch_shapes=[pltpu.SemaphoreType.DMA((5 * nx,)), pltpu.SemaphoreType.DMA((5 * nx,))] if nx else [],
        compiler_params=_cparams(("arbitrary", "arbitrary", "arbitrary")),
    )(*pieces, w, xa, xb, *exchange)
    return (outs[0], outs[1:1 + nx]) if nx else outs[0]


def _rms_fwd(name, x, gain, tm=256):
    s = x.shape[0]

    def body(x_ref, g_ref, y_ref, yt_ref):
        xb = x_ref[...]
        y = xb * lax.rsqrt(jnp.mean(xb * xb, axis=-1, keepdims=True) + EPS) * g_ref[...]
        y_ref[...] = y.astype(BF16)
        yt_ref[...] = y.T.astype(BF16)

    return pl.pallas_call(
        body, name=name, grid=(s // tm,),
        in_specs=[pl.BlockSpec((tm, D), lambda i: (i, 0)), pl.BlockSpec((1, D), lambda i: (0, 0))],
        out_specs=[pl.BlockSpec((tm, D), lambda i: (i, 0)), pl.BlockSpec((D, tm), lambda i: (0, i))],
        out_shape=[jax.ShapeDtypeStruct((s, D), BF16), jax.ShapeDtypeStruct((D, s), BF16)],
        compiler_params=_cparams(("arbitrary",)),
    )(x, gain)


def _rms_bwd(name, x, gain, dns, dres, tm=256):
    s = x.shape[0]
    n_dn = len(dns)

    def fn(xb, g, *rest):
        dn = rest[0]
        for t in rest[1:n_dn]:
            dn = dn + t
        r = lax.rsqrt(jnp.mean(xb * xb, axis=-1, keepdims=True) + EPS)
        xhat = xb * r
        dxh = dn * g
        dx = r * (dxh - xhat * jnp.mean(dxh * xhat, axis=-1, keepdims=True)) + rest[n_dn]
        return (dx,), (jnp.sum(dn * xhat, axis=0, keepdims=True),)

    ins = [_rb(x, tm, D), _cst(gain)] + [_rb(t, tm, D) for t in dns] + [_rb(dres, tm, D)]
    return _rows(name, fn, s, tm, ins, [(D, F32)], [(1, D)])


def _final(h2, target, gain, tm=256):
    s = h2.shape[0]

    def fn(hb, tb, g):
        r = lax.rsqrt(jnp.mean(hb * hb, axis=-1, keepdims=True) + EPS)
        xhat = hb * r
        e = xhat * g - tb
        dy = e * (1.0 / D)
        dxh = dy * g
        dh = r * (dxh - xhat * jnp.mean(dxh * xhat, axis=-1, keepdims=True))
        lrow = 0.5 * jnp.sum(jnp.sum(e * e, axis=-1, keepdims=True) * (1.0 / D), axis=0, keepdims=True)
        return (dh,), (jnp.sum(dy * xhat, axis=0, keepdims=True), jnp.broadcast_to(lrow, (1, LANES)))

    return _rows("final_norm_loss", fn, s, tm, [_rb(h2, tm, D), _rb(target, tm, D), _cst(gain)],
                 [(D, F32)], [(1, D), (1, LANES)])


def _merge_fwd(pa, pb, proj, tm=256):
    s = pa.shape[0]

    def fn(a, b, ga, gb):
        return (_sigmoid(ga) * a + _sigmoid(gb) * b,), ()

    ins = [_rb(pa, tm, D), _rb(pb, tm, D), _rb(proj, tm, D, 7), _rb(proj, tm, D, 8)]
    return _rows("merge_fwd", fn, s, tm, ins, [(D, BF16)])[0]


def _merge_bwd(dmerged, pa, pb, proj, tm=256):
    s = pa.shape[0]

    def fn(dm, a, b, ga, gb):
        sa, sb = _sigmoid(ga), _sigmoid(gb)
        return (dm * sa, dm * sb, dm * a * sa * (1.0 - sa), dm * b * sb * (1.0 - sb)), ()

    ins = [_rb(dmerged, tm, D), _rb(pa, tm, D), _rb(pb, tm, D), _rb(proj, tm, D, 7), _rb(proj, tm, D, 8)]
    return _rows("merge_bwd", fn, s, tm, ins, [(D, BF16), (D, BF16), (D, BF16), (D, BF16)])


def _gelu_parts(x):
    cdf = 0.5 * (1.0 + lax.erf(x * 0.7071067811865476))
    pdf = 0.3989422804014327 * jnp.exp(-0.5 * x * x)
    return x * cdf, cdf + x * pdf


def _conv_taps(u_ext, n_out):
    cur = u_ext[8:8 + n_out]
    m1 = pltpu.roll(u_ext, 1, 0)[8:8 + n_out]
    m2 = pltpu.roll(u_ext, 2, 0)[8:8 + n_out]
    return m2, m1, cur


def _convglu_fwd(u, conv_w8, conv_b, tm=256):
    s, w = u.shape
    tb = tm // 8

    def fn(ub, up, cw, cb):
        i = pl.program_id(0)
        up = jnp.where(i == 0, 0.0, up)
        m2, m1, cur = _conv_taps(jnp.concatenate([up, ub], axis=0), tm)
        acc = cb + cw[0:1] * m2 + cw[1:2] * m1 + cw[2:3] * cur
        gl, dgl = _gelu_parts(acc[:, :D_FF])
        val = acc[:, D_FF:]
        return (gl * val, gl, val * dgl), ()

    ins = [_rb(u, tm, w), (u, (8, w), lambda i: (jnp.maximum(i * tb - 1, 0), 0)), _cst(conv_w8), _cst(conv_b)]
    return _rows("convglu_fwd", fn, s, tm, ins, [(D_FF, BF16)] * 3)


def _convglu_bwd(u, dact, gl, gd, conv_w8, tm=256):
    s, w = u.shape
    tb = tm // 8
    nb = s // tm

    def fn(ub, up, db, dn, glb, gln, gdb, gdn, cw):
        i = pl.program_id(0)
        up = jnp.where(i == 0, 0.0, up)
        dn = jnp.where(i == nb - 1, 0.0, dn.astype(F32))
        ne = tm + 8
        m2, m1, cur = _conv_taps(jnp.concatenate([up, ub], axis=0), tm)
        ext = lambda blk, nxt: jnp.concatenate([blk.astype(F32), nxt.astype(F32)], axis=0)
        de = ext(db, dn)
        dacc = jnp.concatenate([de * ext(gdb, gdn), de * ext(glb, gln)], axis=1)
        p1 = pltpu.roll(dacc, ne - 1, 0)[:tm]
        p2 = pltpu.roll(dacc, ne - 2, 0)[:tm]
        d0 = dacc[:tm]
        du = cw[2:3] * d0 + cw[1:2] * p1 + cw[0:1] * p2
        zero5 = jnp.zeros((5, w), F32)
        dcw = jnp.concatenate([
            jnp.sum(d0 * m2, axis=0, keepdims=True), jnp.sum(d0 * m1, axis=0, keepdims=True),
            jnp.sum(d0 * cur, axis=0, keepdims=True), zero5], axis=0)
        return (du,), (dcw, jnp.sum(d0, axis=0, keepdims=True))

    nxt = lambda arr: (arr, (8, D_FF), lambda i: (jnp.minimum((i + 1) * tb, s // 8 - 1), 0))
    ins = [_rb(u, tm, w), (u, (8, w), lambda i: (jnp.maximum(i * tb - 1, 0), 0)),
           _rb(dact, tm, D_FF), nxt(dact), _rb(gl, tm, D_FF), nxt(gl), _rb(gd, tm, D_FF), nxt(gd), _cst(conv_w8)]
    return _rows("convglu_bwd", fn, s, tm, ins, [(w, BF16)], [(8, w), (1, w)])


def _chunk_scan(x, t_iota, reverse):
    k = 1
    while k < HG_C:
        if reverse:
            x = x + jnp.where(t_iota < HG_C - k, pltpu.roll(x, HG_C - k, 0), 0.0)
        else:
            x = x + jnp.where(t_iota >= k, pltpu.roll(x, k, 0), 0.0)
        k *= 2
    return x


def _hg_gates(hq, hf, lb):
    sq = _sigmoid(hq)
    q = hq * sq
    sg = _sigmoid(hf)
    f = lb + (1.0 - lb) * sg
    return q, sq, sg, f, 1.0 - f, jnp.log(f)


def _lb_of(logits):
    l0, l1 = logits[0:1], logits[1:2]
    mx = jnp.maximum(l0, l1)
    e0, e1 = jnp.exp(l0 - mx), jnp.exp(l1 - mx)
    return e0 / (e0 + e1)


def _tri(n, lower):
    r = lax.broadcasted_iota(jnp.int32, (n, n), 0)
    c = lax.broadcasted_iota(jnp.int32, (n, n), 1)
    return jnp.where((r >= c) if lower else (r <= c), 1.0, 0.0).astype(BF16)


def _hg_intra_terms(q, kk, b, t_iota):
    ws, ps = [], []
    for s in range(HG_C):
        p = jnp.where(t_iota >= s, jnp.exp(b - b[s:s + 1]), 0.0)
        ps.append(p)
        ws.append(q * kk[s:s + 1] * p)
    return jnp.concatenate(ws, axis=0), ps


def _hg_fwd(proj, lb_logits):
    s = proj.shape[0]
    nt = s // HG_T
    nc = HG_T // HG_C

    def body(q_ref, f_ref, i_ref, l_ref, o_ref, st_ref, state):
        @pl.when(pl.program_id(1) == 0)
        def _():
            state[...] = jnp.zeros_like(state)

        st_ref[0, 0] = state[...]
        lb = _lb_of(l_ref[...])
        ones = jnp.ones((HG_DK, HG_DK), BF16)
        t_iota = lax.broadcasted_iota(jnp.int32, (HG_C, HG_DK), 0)
        cc = HG_C * HG_C

        def group(gi, st):
            units = []
            for u in range(HG_UNROLL):
                r = pl.ds(pl.multiple_of((gi * HG_UNROLL + u) * HG_C, HG_C), HG_C)
                q, _, _, _, kk, g = _hg_gates(q_ref[r, :], f_ref[r, :], lb)
                b = _chunk_scan(g, t_iota, False)
                b_end = b[HG_C - 1:HG_C]
                w_all, _ = _hg_intra_terms(q, kk, b, t_iota)
                units.append((r, i_ref[r, :], q * jnp.exp(b), jnp.exp(b_end), kk * jnp.exp(b_end - b), w_all))
            a_all = _dot(jnp.concatenate([un[5] for un in units], axis=0), ones, NN)
            kvs = [_dot(v, kd, TN) for (_, v, _, _, kd, _) in units]
            sts = [st]
            for (_, _, _, dec, _, _), kv in zip(units, kvs):
                sts.append(sts[-1] * dec + kv)
            for ui, (r, v, qd, _, _, _) in enumerate(units):
                o = _dot(qd, sts[ui], NT)
                for si in range(HG_C):
                    o = o + a_all[ui * cc + si * HG_C:ui * cc + (si + 1) * HG_C] * v[si:si + 1]
                o_ref[r, :] = o
            return sts[-1]

        state[...] = lax.fori_loop(0, nc // HG_UNROLL, group, state[...])

    col = lambda off: pl.BlockSpec((HG_T, HG_DK), lambda h, t: (t, off + h))
    return pl.pallas_call(
        body, name="hgrn2_fwd", grid=(HG_H, nt),
        in_specs=[col(0), col(8), col(16), pl.BlockSpec((2, HG_DK), lambda h, t: (0, h))],
        out_specs=[pl.BlockSpec((HG_T, HG_DK), lambda h, t: (t, h)),
                   pl.BlockSpec((1, 1, HG_DK, HG_DK), lambda h, t: (h, t, 0, 0))],
        out_shape=[jax.ShapeDtypeStruct((s, D), F32), jax.ShapeDtypeStruct((HG_H, nt, HG_DK, HG_DK), F32)],
        scratch_shapes=[pltpu.VMEM((HG_DK, HG_DK), F32)],
        compiler_params=_cparams(("arbitrary", "arbitrary")),
    )(proj, proj, proj, lb_logits)


def _hg_bwd(proj, lb_logits, states, do_raw):
    s = proj.shape[0]
    nt = s // HG_T
    nc = HG_T // HG_C

    def body(q_ref, f_ref, i_ref, l_ref, st_ref, do_ref, dq_ref, df_ref, di_ref, dl_ref, st_all, adj):
        tb = pl.program_id(1)

        @pl.when(tb == 0)
        def _():
            adj[...] = jnp.zeros_like(adj)
            dl_ref[...] = jnp.zeros_like(dl_ref)

        lb = _lb_of(l_ref[...])
        ones = jnp.ones((HG_DK, HG_DK), BF16)
        t_iota = lax.broadcasted_iota(jnp.int32, (HG_C, HG_DK), 0)
        cc = HG_C * HG_C

        def fwd_group(gi, st):
            terms = []
            for u in range(HG_UNROLL):
                ci = gi * HG_UNROLL + u
                r = pl.ds(pl.multiple_of(ci * HG_C, HG_C), HG_C)
                _, _, _, _, kk, g = _hg_gates(q_ref[r, :], f_ref[r, :], lb)
                b = _chunk_scan(g, t_iota, False)
                b_end = b[HG_C - 1:HG_C]
                terms.append((ci, jnp.exp(b_end), _dot(i_ref[r, :], kk * jnp.exp(b_end - b), TN)))
            for ci, dec, kv in terms:
                st_all[ci] = st
                st = st * dec + kv
            return st

        lax.fori_loop(0, nc // HG_UNROLL, fwd_group, st_ref[0, 0])

        def bwd_group(gj, dlb):
            units = []
            for u in range(HG_UNROLL_BWD):
                ci = nc - 1 - (gj * HG_UNROLL_BWD + u)
                r = pl.ds(pl.multiple_of(ci * HG_C, HG_C), HG_C)
                hq, hf, v, do = q_ref[r, :], f_ref[r, :], i_ref[r, :], do_ref[r, :]
                q, sq, sg, f, kk, g = _hg_gates(hq, hf, lb)
                b = _chunk_scan(g, t_iota, False)
                b_end = b[HG_C - 1:HG_C]
                e_b, e_be, dec = jnp.exp(b), jnp.exp(b_end - b), jnp.exp(b_end)
                w_all, ps = _hg_intra_terms(q, kk, b, t_iota)
                x_all = jnp.concatenate([do * v[si:si + 1] for si in range(HG_C)], axis=0)
                units.append(dict(ci=ci, r=r, hq=hq, v=v, do=do, q=q, sq=sq, sg=sg, f=f, kk=kk, e_b=e_b, e_be=e_be,
                                  dec=dec, kd=kk * e_be, w=w_all, ps=ps, x=x_all))
            both = _dot(jnp.concatenate([un["w"] for un in units] + [un["x"] for un in units], axis=0), ones, NN)
            st0s = [st_all[un["ci"]] for un in units]
            st_ends = [st0 * un["dec"] + _dot(un["v"], un["kd"], TN) for un, st0 in zip(units, st0s)]
            dqks = [_dot(un["do"], un["q"] * un["e_b"], TN) for un in units]
            es = [adj[...]]
            for un, dqk in zip(units, dqks):
                es.append(es[-1] * un["dec"] + dqk)
            adj[...] = es[-1]
            for ui, un in enumerate(units):
                e, q, kk, v, do = es[ui], un["q"], un["kk"], un["v"], un["do"]
                tail = jnp.sum(e * st_ends[ui], axis=0, keepdims=True)
                dq = un["e_b"] * _dot(do, st0s[ui], NN)
                dk = un["e_be"] * _dot(v, e, NN)
                dv = _dot(un["kd"], e, NT)
                a0 = ui * cc
                d0 = (HG_UNROLL_BWD + ui) * cc
                for si in range(HG_C):
                    da = both[d0 + si * HG_C:d0 + (si + 1) * HG_C]
                    aa = both[a0 + si * HG_C:a0 + (si + 1) * HG_C]
                    dap = da * un["ps"][si]
                    dq = dq + dap * kk[si:si + 1]
                    hit = t_iota == si
                    dk = dk + jnp.where(hit, jnp.sum(dap * q, axis=0, keepdims=True), 0.0)
                    dv = dv + jnp.where(hit, jnp.sum(aa * do, axis=0, keepdims=True), 0.0)
                dg = _chunk_scan(q * dq - kk * dk, t_iota, True) + tail
                dfg = dg / un["f"] - dk
                sq, sg, hq, r = un["sq"], un["sg"], un["hq"], un["r"]
                dq_ref[r, :] = (dq * sq * (1.0 + hq * (1.0 - sq))).astype(dq_ref.dtype)
                df_ref[r, :] = (dfg * (1.0 - lb) * sg * (1.0 - sg)).astype(df_ref.dtype)
                di_ref[r, :] = dv.astype(di_ref.dtype)
                dlb = dlb + jnp.sum(dfg * (1.0 - sg), axis=0, keepdims=True)
            return dlb

        dlb = lax.fori_loop(0, nc // HG_UNROLL_BWD, bwd_group, jnp.zeros((1, HG_DK), F32))
        dl0 = dlb * lb * (1.0 - lb)
        dl_ref[...] += jnp.concatenate([dl0, -dl0], axis=0)

    col = lambda off: pl.BlockSpec((HG_T, HG_DK), lambda h, t: (nt - 1 - t, off + h))
    out_col = pl.BlockSpec((HG_T, HG_DK), lambda h, t: (nt - 1 - t, h))
    return pl.pallas_call(
        body, name="hgrn2_bwd", grid=(HG_H, nt),
        in_specs=[col(0), col(8), col(16), pl.BlockSpec((2, HG_DK), lambda h, t: (0, h)),
                  pl.BlockSpec((1, 1, HG_DK, HG_DK), lambda h, t: (h, nt - 1 - t, 0, 0)), col(0)],
        out_specs=[out_col, out_col, out_col, pl.BlockSpec((2, HG_DK), lambda h, t: (0, h))],
        out_shape=[jax.ShapeDtypeStruct((s, D), BF16)] * 3 + [jax.ShapeDtypeStruct((2, D), F32)],
        scratch_shapes=[pltpu.VMEM((nc, HG_DK, HG_DK), F32), pltpu.VMEM((HG_DK, HG_DK), F32)],
        compiler_params=_cparams(("arbitrary", "arbitrary")),
    )(proj, proj, proj, lb_logits, states, do_raw)


def _hg_post_fwd(o_raw, proj, gnorm, tm=256):
    s = o_raw.shape[0]

    def fn(o, hg, gn):
        outs = []
        for h in range(HG_H):
            sl = slice(h * HG_DK, (h + 1) * HG_DK)
            oh, gh = o[:, sl], hg[:, sl]
            r = lax.rsqrt(jnp.mean(oh * oh, axis=-1, keepdims=True) + EPS)
            outs.append(oh * r * gn * (gh * _sigmoid(gh)))
        return (jnp.concatenate(outs, axis=1),), ()

    return _rows("hgrn2_out_fwd", fn, s, tm, [_rb(o_raw, tm, D), _rb(proj, tm, D, 3), _cst(gnorm)], [(D, BF16)])[0]


def _hg_post_bwd(do_a, o_raw, proj, gnorm, tm=256):
    s = o_raw.shape[0]

    def fn(da, o, hg, gn):
        dos, dhgs = [], []
        dgn = jnp.zeros((1, HG_DK), F32)
        for h in range(HG_H):
            sl = slice(h * HG_DK, (h + 1) * HG_DK)
            oh, gh, dh = o[:, sl], hg[:, sl], da[:, sl]
            r = lax.rsqrt(jnp.mean(oh * oh, axis=-1, keepdims=True) + EPS)
            xhat = oh * r
            sg = _sigmoid(gh)
            dy = dh * (gh * sg)
            dhgs.append(dh * xhat * gn * sg * (1.0 + gh * (1.0 - sg)))
            dgn = dgn + jnp.sum(dy * xhat, axis=0, keepdims=True)
            dxh = dy * gn
            dos.append(r * (dxh - xhat * jnp.mean(dxh * xhat, axis=-1, keepdims=True)))
        return (jnp.concatenate(dos, axis=1), jnp.concatenate(dhgs, axis=1)), (dgn,)

    ins = [_rb(do_a, tm, D), _rb(o_raw, tm, D), _rb(proj, tm, D, 3), _cst(gnorm)]
    return _rows("hgrn2_out_bwd", fn, s, tm, ins, [(D, F32), (D, BF16)], [(1, HG_DK)])


def _log_sigmoid(z):
    return jnp.minimum(z, 0.0) - jnp.log(1.0 + jnp.exp(-jnp.abs(z)))


def _fox_gate_bwd(dct, pff, bias, tm=256):
    s = pff.shape[0]
    nb = s // tm

    def body(d_ref, p_ref, b_ref, dff_ref, db_ref, carry):
        @pl.when(pl.program_id(0) == 0)
        def _():
            carry[...] = jnp.zeros_like(carry)
            db_ref[...] = jnp.zeros_like(db_ref)

        dc = d_ref[...].T
        dlf = _split_dot(_tri(tm, False), dc, 3) + carry[0:1]
        carry[...] = jnp.broadcast_to(dlf[0:1], carry.shape)
        dff = dlf * _sigmoid(-(p_ref[...] + b_ref[...]))
        dff_ref[...] = dff
        db_ref[...] += jnp.sum(dff, axis=0, keepdims=True)

    return pl.pallas_call(
        body, name="fox_gate_bwd", grid=(nb,),
        in_specs=[pl.BlockSpec((LANES, tm), lambda i: (0, nb - 1 - i)),
                  pl.BlockSpec((tm, LANES), lambda i: (nb - 1 - i, 0)), pl.BlockSpec((1, LANES), lambda i: (0, 0))],
        out_specs=[pl.BlockSpec((tm, LANES), lambda i: (nb - 1 - i, 0)), pl.BlockSpec((1, LANES), lambda i: (0, 0))],
        out_shape=[jax.ShapeDtypeStruct((s, LANES), F32), jax.ShapeDtypeStruct((1, LANES), F32)],
        scratch_shapes=[pltpu.VMEM((8, LANES), F32)],
        compiler_params=_cparams(("arbitrary",)),
    )(dct, pff, bias)


def _diag_mask(t):
    r = lax.broadcasted_iota(jnp.int32, (t, t), 0)
    c = lax.broadcasted_iota(jnp.int32, (t, t), 1)
    return r >= c


AUX_ONES = 6


def _pieces(x):
    h = x.astype(BF16)
    r = x - h.astype(F32)
    m = r.astype(BF16)
    return h, m, (r - m.astype(F32)).astype(BF16)


def _lane_put(lane, cols, base):
    out = None
    for i, col in enumerate(cols):
        term = jnp.where(lane == base + i, col.astype(F32), 0.0)
        out = term if out is None else out + term
    return out


def _fox_prep(proj, pff, bias, tm=256):
    s = pff.shape[0]

    def body(q_ref, k_ref, v_ref, p_ref, b_ref, qb_ref, kb_ref, vb_ref, ka_ref, carry):
        @pl.when(pl.program_id(0) == 0)
        def _():
            carry[...] = jnp.zeros_like(carry)

        qb_ref[...] = (q_ref[...] * 0.125).astype(BF16)
        kb_ref[...] = k_ref[...].astype(BF16)
        vb_ref[...] = v_ref[...].astype(BF16)
        lf = _log_sigmoid(p_ref[...] + b_ref[...])
        c = _split_dot(_tri(tm, True), lf, 3) + carry[0:1]
        carry[...] = jnp.broadcast_to(c[tm - 1:tm], carry.shape)
        lane = lax.broadcasted_iota(jnp.int32, (tm, LANES), 1)
        ones = jnp.where((lane >= AUX_ONES) & (lane < AUX_ONES + 6), 1.0, 0.0)
        for p in range(FOX_H // 2):
            aux = ones
            for z in range(2):
                col = jnp.sum(jnp.where(lane == 2 * p + z, c, 0.0), axis=1, keepdims=True)
                aux = aux + _lane_put(lane, _pieces(-col), 3 * z)
            ka_ref[:, p * LANES:(p + 1) * LANES] = aux.astype(BF16)

    row = lambda cb: pl.BlockSpec((tm, D), lambda i: (i, cb))
    return pl.pallas_call(
        body, name="fox_prep", grid=(s // tm,),
        in_specs=[row(4), row(5), row(6), pl.BlockSpec((tm, LANES), lambda i: (i, 0)),
                  pl.BlockSpec((1, LANES), lambda i: (0, 0))],
        out_specs=[row(0)] * 4, out_shape=[jax.ShapeDtypeStruct((s, D), BF16)] * 4,
        scratch_shapes=[pltpu.VMEM((8, LANES), F32)],
        compiler_params=_cparams(("arbitrary",)),
    )(proj, proj, proj, pff, bias)


def _fox_fwd(qb, kb, vb, ka):
    s = qb.shape[0]
    t = min(FOX_T, s)
    nq = s // t

    def body(q_ref, k_ref, v_ref, ka_ref, o_ref, la_ref):
        i = pl.program_id(1)
        lane = lax.broadcasted_iota(jnp.int32, (t, LANES), 1)
        in_a = lane < FOX_D
        q = q_ref[...]
        zero = jnp.zeros_like(q)
        qh = [jnp.where(in_a, q, zero), jnp.where(in_a, zero, q)]
        c_ones = [jnp.where((lane >= 3 * z) & (lane < 3 * z + 3), 1.0, 0.0) for z in range(2)]

        def keys(j):
            rows = pl.ds(pl.multiple_of(j * t, t), t)
            return jnp.concatenate([k_ref[rows, :], ka_ref[rows, :]], axis=1), rows

        dmask = _diag_mask(t)

        def logits(qx, kk, masked):
            e = lax.dot_general(qx, kk, (NT, ((), ())), preferred_element_type=F32)
            return jnp.where(dmask, e, -1e30) if masked else e

        qc = [jnp.concatenate([qh[z], c_ones[z].astype(BF16)], axis=1) for z in range(2)]

        def step(j, carry, masked):
            kk, rows = keys(j)
            vj = v_ref[rows, :]
            scores = [logits(qc[z], kk, masked) for z in range(2)]
            one = jnp.ones_like(vj)
            vh = [jnp.where(in_a, vj, one), jnp.where(in_a, one, vj)]
            out = []
            for z in range(2):
                m, acc = carry[z]
                m_new = jnp.maximum(m, jnp.max(scores[z], axis=1, keepdims=True))
                p = jnp.exp(scores[z] - m_new)
                out.append((m_new, jnp.exp(m - m_new) * acc + _dot(p, vh[z], NN)))
            return tuple(out)

        init = tuple((jnp.full((t, 1), -1e30, F32), jnp.zeros((t, LANES), F32)) for _ in range(2))
        (ma, acc_a), (mb, acc_b) = step(i, lax.fori_loop(0, i, lambda j, c: step(j, c, False), init), True)
        la = jnp.sum(jnp.where(lane == FOX_D, acc_a, 0.0), axis=1, keepdims=True)
        lb = jnp.sum(jnp.where(lane == 0, acc_b, 0.0), axis=1, keepdims=True)
        o_ref[...] = jnp.where(in_a, acc_a / la, acc_b / lb).astype(o_ref.dtype)
        la_ref[...] = (_lane_put(lane, _pieces(-(ma + jnp.log(la))), AUX_ONES)
                       + _lane_put(lane, _pieces(-(mb + jnp.log(lb))), AUX_ONES + 3)).astype(la_ref.dtype)

    blk = pl.BlockSpec((t, LANES), lambda p, i: (i, p))
    whole = pl.BlockSpec((s, LANES), lambda p, i: (0, p))
    return pl.pallas_call(
        body, name="fox_attn_fwd", grid=(FOX_H // 2, nq), in_specs=[blk, whole, whole, whole],
        out_specs=[blk, blk], out_shape=[jax.ShapeDtypeStruct((s, D), BF16)] * 2,
        compiler_params=_cparams(("arbitrary", "arbitrary")),
    )(qb, kb, vb, ka)


def _fox_bwd(qb, kb, vb, ka, ob, laux, dob):
    s = qb.shape[0]
    t = min(FOX_T, s)
    nq = s // t

    def body(q_ref, k_ref, v_ref, ka_ref, o_ref, la_ref, do_ref, dq_ref, dk_ref, dv_ref, dc_ref, dkt, dvt):
        i = pl.program_id(1)

        @pl.when(i == 0)
        def _():
            dkt[...] = jnp.zeros_like(dkt)
            dvt[...] = jnp.zeros_like(dvt)
            dc_ref[...] = jnp.zeros_like(dc_ref)

        lane = lax.broadcasted_iota(jnp.int32, (t, LANES), 1)
        in_a = lane < FOX_D
        q, do, la = q_ref[...], do_ref[...], la_ref[...].astype(F32)
        zero = jnp.zeros_like(q)
        qh = [jnp.where(in_a, q, zero), jnp.where(in_a, zero, q)]
        doh = [jnp.where(in_a, do, zero), jnp.where(in_a, zero, do)]
        qt = [h.astype(F32).T.astype(BF16) for h in qh]
        dot_ = [h.astype(F32).T.astype(BF16) for h in doh]
        prod = do.astype(F32) * o_ref[...].astype(F32)
        qx, dox = [], []
        for z in range(2):
            delta = jnp.sum(jnp.where(in_a if z == 0 else ~in_a, prod, 0.0), axis=1, keepdims=True)
            c_ones = jnp.where((lane >= 3 * z) & (lane < 3 * z + 3), 1.0, 0.0)
            lse_lanes = (lane >= AUX_ONES + 3 * z) & (lane < AUX_ONES + 3 * z + 3)
            qx.append(jnp.concatenate([qh[z], (c_ones + jnp.where(lse_lanes, la, 0.0)).astype(BF16)], axis=1))
            dox.append(jnp.concatenate([doh[z], _lane_put(lane, _pieces(-delta), 3 * z).astype(BF16)], axis=1))
        v_ones = jnp.where(lane < 6, 1.0, 0.0).astype(BF16)
        dmask = _diag_mask(t)

        def step(j, carry, masked):
            rows = pl.ds(pl.multiple_of(j * t, t), t)
            kj, vj = k_ref[rows, :], v_ref[rows, :]
            kk = jnp.concatenate([kj, ka_ref[rows, :]], axis=1)
            vv = jnp.concatenate([vj, v_ones], axis=1)
            out = []
            dk_add, dv_add = None, None
            for z in range(2):
                dq, rsum = carry[z]
                e = lax.dot_general(qx[z], kk, (NT, ((), ())), preferred_element_type=F32)
                if masked:
                    e = jnp.where(dmask, e, -1e30)
                p = jnp.exp(e)
                ds = p * lax.dot_general(dox[z], vv, (NT, ((), ())), preferred_element_type=F32)
                dkz, dvz = _dot(qt[z], ds, NN), _dot(dot_[z], p, NN)
                dk_add = dkz if dk_add is None else dk_add + dkz
                dv_add = dvz if dv_add is None else dv_add + dvz
                dc_ref[0, z, j] += -jnp.sum(ds, axis=0, keepdims=True)
                out.append((dq + _dot(ds, kj, NN), rsum + jnp.sum(ds, axis=1, keepdims=True)))
            dkt[j] += dk_add
            dvt[j] += dv_add
            return tuple(out)

        init = tuple((jnp.zeros((t, LANES), F32), jnp.zeros((t, 1), F32)) for _ in range(2))
        (dq_a, rs_a), (dq_b, rs_b) = step(i, lax.fori_loop(0, i, lambda j, c: step(j, c, False), init), True)
        for z, rs in enumerate((rs_a, rs_b)):
            dc_ref[0, z, i] += jnp.transpose(jnp.broadcast_to(rs, (t, LANES)))[0:1]
        dq_ref[...] = (jnp.where(in_a, dq_a, dq_b) * 0.125).astype(dq_ref.dtype)

        @pl.when(i == nq - 1)
        def _():
            for jb in range(nq):
                dk_ref[jb * t:(jb + 1) * t, :] = dkt[jb].T.astype(dk_ref.dtype)
                dv_ref[jb * t:(jb + 1) * t, :] = dvt[jb].T.astype(dv_ref.dtype)

    blk = pl.BlockSpec((t, LANES), lambda p, i: (i, p))
    whole = pl.BlockSpec((s, LANES), lambda p, i: (0, p))
    return pl.pallas_call(
        body, name="fox_attn_bwd", grid=(FOX_H // 2, nq),
        in_specs=[blk, whole, whole, whole, blk, blk, blk],
        out_specs=[blk, whole, whole, pl.BlockSpec((1, 2, nq, 1, t), lambda p, i: (p, 0, 0, 0, 0))],
        out_shape=[jax.ShapeDtypeStruct((s, D), BF16)] * 3 + [jax.ShapeDtypeStruct((FOX_H // 2, 2, nq, 1, t), F32)],
        scratch_shapes=[pltpu.VMEM((nq, LANES, t), F32), pltpu.VMEM((nq, LANES, t), F32)],
        compiler_params=_cparams(("arbitrary", "arbitrary")),
    )(qb, kb, vb, ka, ob, laux, dob)


def _adamw(name, w, g, m, v, tm=None):
    rows, width = w.shape
    tm = rows if tm is None else tm
    c1 = 1.0 - ADAM_B1 ** ADAM_STEP
    c2 = 1.0 - ADAM_B2 ** ADAM_STEP

    def fn(wb, gb, mb, vb):
        m_new = ADAM_B1 * mb + (1.0 - ADAM_B1) * gb
        v_new = ADAM_B2 * vb + (1.0 - ADAM_B2) * (gb * gb)
        delta = -ADAM_LR * ((m_new / c1) / (jnp.sqrt(v_new / c2) + ADAM_EPS) + ADAM_WD * wb)
        return (delta, m_new, v_new), ()

    ins = [_rb(a, tm, width) for a in (w, g, m, v)]
    return _rows(name, fn, rows, tm, ins, [(width, F32)] * 3)


def _me():
    return lax.axis_index("x"), lax.axis_index("y"), lax.axis_index("c")


def _all_reduce_small(name, block):
    r, n = block.shape

    def body(x_ref, sum_ref, gath, send_sems, recv_sems):
        x, y, c = _me()
        me = 4 * x + 2 * y + c
        gath[me] = x_ref[...]
        sends = []
        for k in range(1, 8):
            px = x ^ ((k >> 2) & 1)
            py = y ^ ((k >> 1) & 1)
            pc = c ^ (k & 1)
            sends.append(pltpu.make_async_remote_copy(
                src_ref=x_ref, dst_ref=gath.at[me], send_sem=send_sems.at[k - 1], recv_sem=recv_sems.at[k - 1],
                device_id=(px, py, pc), device_id_type=MESH))
        for cp in sends:
            cp.start()
        for k in range(1, 8):
            peer = me ^ k
            pltpu.make_async_remote_copy(
                src_ref=x_ref, dst_ref=gath.at[peer], send_sem=send_sems.at[k - 1], recv_sem=recv_sems.at[k - 1],
                device_id=(x, y, c), device_id_type=MESH).wait_recv()
        for cp in sends:
            cp.wait_send()
        acc = gath[0]
        for d in range(1, 8):
            acc = acc + gath[d]
        sum_ref[...] = acc

    vm = pl.BlockSpec(memory_space=pltpu.VMEM)
    return pl.pallas_call(
        body, name=name, in_specs=[vm], out_specs=vm, out_shape=jax.ShapeDtypeStruct((r, n), F32),
        scratch_shapes=[pltpu.VMEM((8, r, n), F32), pltpu.SemaphoreType.DMA((7,)), pltpu.SemaphoreType.DMA((7,))],
    )(block)


WD_EXT_ROWS = 736


def _remote(src, dst, send_sems, recv_sems, k, to):
    return pltpu.make_async_remote_copy(src_ref=src, dst_ref=dst, send_sem=send_sems.at[k], recv_sem=recv_sems.at[k],
                                        device_id=to, device_id_type=MESH)


def _all_gather8_multi(name, blocks):
    nt = len(blocks)

    def body(*refs):
        x_refs, out_refs, send_sems, recv_sems = refs[:nt], refs[nt:2 * nt], refs[-2], refs[-1]
        x, y, c = _me()
        me, sibling = (x, y, c), (x, y, 1 - c)
        chips = [(1 - x, y), (x, 1 - y), (1 - x, 1 - y)]
        slot = lambda q, p: out_refs[q].at[4 * p[0] + 2 * p[1] + p[2]]

        def copies(k, blk, to, from_input=False):
            return [_remote(x_refs[q] if from_input else slot(q, blk), slot(q, blk), send_sems, recv_sems, k * nt + q, to)
                    for q in range(nt)]

        first = copies(0, me, sibling, True)
        for j, chip in enumerate(chips):
            first += copies(1 + j, me, (*chip, c), True)
        for cp in first:
            cp.start()
        passed = []
        for j, chip in enumerate(chips):
            for cp in copies(1 + j, (*chip, c), me):
                cp.wait_recv()
            fwd = copies(4 + j, (*chip, c), sibling)
            for cp in fwd:
                cp.start()
            passed += fwd
        for cp in copies(0, sibling, me):
            cp.wait_recv()
        back = copies(7, sibling, sibling)
        for cp in back:
            cp.start()
        for j, chip in enumerate(chips):
            for cp in copies(4 + j, (*chip, 1 - c), me):
                cp.wait_recv()
        for cp in copies(7, me, me):
            cp.wait_recv()
        for cp in first + passed + back:
            cp.wait_send()

    return pl.pallas_call(
        body, name=name, in_specs=[ANY] * nt, out_specs=[ANY] * nt,
        out_shape=[jax.ShapeDtypeStruct((8,) + b.shape, b.dtype) for b in blocks],
        scratch_shapes=[pltpu.SemaphoreType.DMA((8 * nt,)), pltpu.SemaphoreType.DMA((8 * nt,))],
    )(*blocks)


def _swap_halves_multi(name, gs):
    nt = len(gs)
    n_chip = gs[0].shape[0]

    def body(*refs):
        g_refs, got_refs, send_sems, recv_sems = refs[:nt], refs[nt:2 * nt], refs[-2], refs[-1]
        x, y, c = _me()
        cps = [_remote(g_refs[q].at[j, 1 - c], got_refs[q].at[j], send_sems, recv_sems, q * n_chip + j, (x, y, 1 - c))
               for q in range(nt) for j in range(n_chip)]
        for cp in cps:
            cp.start()
        for cp in cps:
            cp.wait()

    return pl.pallas_call(
        body, name=name, in_specs=[ANY] * nt, out_specs=[ANY] * nt,
        out_shape=[jax.ShapeDtypeStruct((g.shape[0],) + g.shape[2:], g.dtype) for g in gs],
        scratch_shapes=[pltpu.SemaphoreType.DMA((nt * n_chip,)), pltpu.SemaphoreType.DMA((nt * n_chip,))],
    )(*gs)


def _swap_sibling_multi(name, xs):
    nt = len(xs)

    def body(*refs):
        x_refs, out_refs, send_sems, recv_sems = refs[:nt], refs[nt:2 * nt], refs[-2], refs[-1]
        x, y, c = _me()
        cps = [_remote(x_refs[q], out_refs[q], send_sems, recv_sems, q, (x, y, 1 - c)) for q in range(nt)]
        for cp in cps:
            cp.start()
        for cp in cps:
            cp.wait()

    return pl.pallas_call(
        body, name=name, in_specs=[ANY] * nt, out_specs=[ANY] * nt,
        out_shape=[jax.ShapeDtypeStruct(a.shape, a.dtype) for a in xs],
        scratch_shapes=[pltpu.SemaphoreType.DMA((nt,)), pltpu.SemaphoreType.DMA((nt,))],
    )(*xs)


def _chip_exchange_multi(name, ps):
    nt = len(ps)

    def body(*refs):
        p_refs, out_refs, bounce_refs = refs[:nt], refs[nt:2 * nt], refs[2 * nt:3 * nt]
        send_sems, recv_sems = refs[-2], refs[-1]
        x, y, c = _me()
        my_chip = 2 * x + y
        sibling = (x, y, 1 - c)
        chips = [(1 - x, y), (x, 1 - y), (1 - x, 1 - y)]
        cp = lambda k, q, src, dst, to: _remote(src, dst, send_sems, recv_sems, k * nt + q, to)
        sends = [cp(k, q, p_refs[q].at[2 * px + py], out_refs[q].at[my_chip], (px, py, c))
                 for k, (px, py) in enumerate(chips) for q in range(nt)]
        sends += [cp(3, q, p_refs[q].at[my_chip], bounce_refs[q], sibling) for q in range(nt)]
        for s_ in sends:
            s_.start()
        backs = []
        for q in range(nt):
            cp(3, q, p_refs[q].at[my_chip], bounce_refs[q], sibling).wait_recv()
            backs.append(cp(4, q, bounce_refs[q], out_refs[q].at[my_chip], sibling))
            backs[-1].start()
        for k, (px, py) in enumerate(chips):
            for q in range(nt):
                cp(k, q, p_refs[q].at[my_chip], out_refs[q].at[2 * px + py], (px, py, c)).wait_recv()
        for q in range(nt):
            cp(4, q, bounce_refs[q], out_refs[q].at[my_chip], sibling).wait_recv()
        for s_ in sends + backs:
            s_.wait_send()

    outs = pl.pallas_call(
        body, name=name, in_specs=[ANY] * nt, out_specs=[ANY] * (2 * nt),
        out_shape=[jax.ShapeDtypeStruct(p.shape, p.dtype) for p in ps]
        + [jax.ShapeDtypeStruct(p.shape[1:], p.dtype) for p in ps],
        scratch_shapes=[pltpu.SemaphoreType.DMA((5 * nt,)), pltpu.SemaphoreType.DMA((5 * nt,))],
    )(*ps)
    return outs[:nt]


def _row_tile(m):
    return m if m <= 384 else 128


def _add2_rows(name, a, b):
    n4, m, n = a.shape
    tm = _row_tile(m)
    out = _rows(name, lambda p, q: ((p + q,), ()), n4 * m, tm,
                [_rb(a.reshape(n4 * m, n), tm, n), _rb(b.reshape(n4 * m, n), tm, n)], [(n, BF16)])[0]
    return out.reshape(n4, m, n)


def _add4_rows(name, p):
    _, m, n = p.shape
    tm = _row_tile(m)
    nb = m // tm
    flat = p.reshape(4 * m, n)
    ins = [(flat, (tm, n), (lambda i, j=j: (j * nb + i, 0))) for j in range(4)]
    f32 = lambda v: v.astype(F32)
    return _rows(name, lambda a, b, c, d: ((((f32(a) + f32(b)) + f32(c)) + f32(d),), ()), m, tm, ins, [(n, F32)])[0]


def _in_proj_with_gather(n1, w_main, blocks):
    m, k = n1.shape
    n = w_main.shape[1]
    tm, tn = min(1024, m), 1024
    gi, gj = m // tm, n // tn
    last, mid = gi * gj - 1, (3 * gi * gj) // 4
    nt = len(blocks)

    def body(*refs):
        a_ref, b_ref, x_refs, o_ref, out_refs = refs[0], refs[1], refs[2:2 + nt], refs[2 + nt], refs[3 + nt:3 + 2 * nt]
        send_sems, recv_sems = refs[-2], refs[-1]
        step = pl.program_id(0) * gj + pl.program_id(1)
        x, y, c = _me()
        me, sibling = (x, y, c), (x, y, 1 - c)
        chips = [(1 - x, y), (x, 1 - y), (1 - x, 1 - y)]
        slot = lambda q, p: out_refs[q].at[4 * p[0] + 2 * p[1] + p[2]]

        def copies(kk, blk, to, from_input=False):
            return [_remote(x_refs[q] if from_input else slot(q, blk), slot(q, blk), send_sems, recv_sems, kk * nt + q, to)
                    for q in range(nt)]

        def first():
            out = copies(0, me, sibling, True)
            for j, chip in enumerate(chips):
                out += copies(1 + j, me, (*chip, c), True)
            return out

        @pl.when(step == 0)
        def _():
            for cp in first():
                cp.start()

        @pl.when(step == mid)
        def _():
            for j, chip in enumerate(chips):
                for cp in copies(1 + j, (*chip, c), me):
                    cp.wait_recv()
                for cp in copies(4 + j, (*chip, c), sibling):
                    cp.start()
            for cp in copies(0, sibling, me):
                cp.wait_recv()
            for cp in copies(7, sibling, sibling):
                cp.start()

        o_ref[...] = _dot(a_ref[...], b_ref[...], NN)

        @pl.when(step == last)
        def _():
            for j, chip in enumerate(chips):
                for cp in copies(4 + j, (*chip, 1 - c), me):
                    cp.wait_recv()
            for cp in copies(7, me, me):
                cp.wait_recv()
            sent = first() + copies(7, sibling, sibling)
            for j, chip in enumerate(chips):
                sent += copies(4 + j, (*chip, c), sibling)
            for cp in sent:
                cp.wait_send()

    outs = pl.pallas_call(
        body, name="in_proj", grid=(gi, gj),
        in_specs=[pl.BlockSpec((tm, k), lambda i, j: (i, 0)), pl.BlockSpec((k, tn), lambda i, j: (0, j))] + [ANY] * nt,
        out_specs=[pl.BlockSpec((tm, tn), lambda i, j: (i, j))] + [ANY] * nt,
        out_shape=[jax.ShapeDtypeStruct((m, n), F32)] + [jax.ShapeDtypeStruct((8,) + b.shape, b.dtype) for b in blocks],
        scratch_shapes=[pltpu.SemaphoreType.DMA((8 * nt,)), pltpu.SemaphoreType.DMA((8 * nt,))],
        compiler_params=_cparams(("arbitrary", "arbitrary")),
    )(n1, w_main, *blocks)
    return outs[0], outs[1:]


def _weight_halves(w_in, w_a, w_b, w_out, w_up, w_down, conv_w):
    c = lax.axis_index("c")
    bits = lax.bitcast_convert_type(conv_w, BF16).reshape(-1)
    extra = jnp.zeros(((WD_EXT_ROWS - W_DOWN_SHARD) * D,), BF16).at[:bits.shape[0]].set(bits)
    wd_ext = jnp.concatenate([w_down.astype(BF16), extra.reshape(-1, D)], axis=0)
    shards = [w_in.astype(BF16), w_a.astype(BF16), w_b.astype(BF16), w_out.astype(BF16), w_up.astype(BF16), wd_ext]
    return [lax.dynamic_slice_in_dim(t, c * (t.shape[0] // 2), t.shape[0] // 2, axis=0) for t in shards]


def _unpack_w_in(gathered):
    wi = gathered.reshape(N_CHIP, D, W_IN_SHARD).transpose(1, 0, 2).reshape(D, N_CHIP * W_IN_SHARD)
    w_main = jnp.concatenate([wi[:, :FF_COL], wi[:, FF_COL + FOX_H:]], axis=1)
    return w_main, jnp.pad(wi[:, FF_COL:FF_COL + FOX_H], ((0, 0), (0, LANES - FOX_H)))


def _unpack_later_weights(gathered):
    full = [g.reshape((N_CHIP, 2 * g.shape[1]) + g.shape[2:]) for g in gathered]
    wa, wb, wo = (full[i].reshape(D, D) for i in (0, 1, 2))
    wu = full[3].transpose(1, 0, 2).reshape(D, 2 * D_FF)
    wd = full[4][:, :W_DOWN_SHARD].reshape(D_FF, D)
    n_bits = 3 * W_UP_SHARD * 2
    cw_bits = full[4][:, W_DOWN_SHARD:].reshape(N_CHIP, -1)[:, :n_bits].reshape(N_CHIP, 3, W_UP_SHARD, 2)
    cw = lax.bitcast_convert_type(cw_bits, F32).transpose(1, 0, 2).reshape(3, 2 * D_FF)
    return wa, wb, wo, wu, wd, cw


def _chip_sums(tag, per_chip):
    c = lax.axis_index("c")
    gs = [t.reshape(N_CHIP, 2, t.shape[1] // 2, t.shape[2]) for t in per_chip]
    got = _swap_halves_multi("grad_swap_halves_" + tag, gs)
    return [_add2_rows("grad_chip_sum_%s%d" % (tag, q), lax.dynamic_index_in_dim(g, c, axis=1, keepdims=False), s_)
            for q, (g, s_) in enumerate(zip(gs, got))]


def _late_weight_chip_sums(d_a, d_b, d_o, d_u, d_d):
    return _chip_sums("late", [d_a.reshape(N_CHIP, -1, D), d_b.reshape(N_CHIP, -1, D), d_o.reshape(N_CHIP, -1, D),
                               d_u.reshape(D, N_CHIP, W_UP_SHARD).transpose(1, 0, 2), d_d.reshape(N_CHIP, -1, D)])


def _finish_grads(d_main, d_ff, late_pieces):
    c = lax.axis_index("c")
    d_in = jnp.concatenate(d_main[:7] + [d_ff[:, :FOX_H]] + d_main[7:], axis=1)
    sums = _chip_sums("w_in", [d_in.reshape(D, N_CHIP, W_IN_SHARD).transpose(1, 0, 2)])
    pieces = list(_chip_exchange_multi("grad_chip_exchange", sums)) + list(late_pieces)
    mine = [_add4_rows("grad_sum_chips_%d" % q, p) for q, p in enumerate(pieces)]
    other = _swap_sibling_multi("grad_share_half", mine)
    return [jnp.concatenate([jnp.where(c == 0, a, b), jnp.where(c == 0, b, a)], axis=0) for a, b in zip(mine, other)]


def _local_step(x, target, norm_mix, fox_f_bias, hg_lb_logits, hg_norm, norm_ffn, conv_b, norm_final,
                w_main, w_ff, later):
    bias = jnp.pad(fox_f_bias, ((0, 0), (0, LANES - FOX_H)))

    n1, n1t = _rms_fwd("norm_mix_fwd", x, norm_mix)
    if len(later) == 5:
        proj, gathered = _in_proj_with_gather(n1, w_main, later)
        wa, wb, wo, wu, wd, conv_w = _unpack_later_weights(gathered)
    else:
        proj = _mm("in_proj", n1, w_main, "nn", F32, 1024, 1024, D)
        wa, wb, wo, wu, wd, conv_w = later
    conv_w8 = jnp.pad(conv_w, ((0, 5), (0, 0)))
    pff = _mm("in_proj_forget", n1, w_ff, "nn", F32, 1024, LANES, D)
    qb, kb, vb, ka = _fox_prep(proj, pff, bias)
    o_b, laux = _fox_fwd(qb, kb, vb, ka)
    o_raw, states = _hg_fwd(proj, hg_lb_logits)
    o_a = _hg_post_fwd(o_raw, proj, hg_norm)
    pa = _mm("branch_a", o_a, wa, "nn", F32, 1024, 1024, D)
    pb = _mm("branch_b", o_b, wb, "nn", F32, 1024, 1024, D)
    merged = _merge_fwd(pa, pb, proj)
    h1 = _mm("out_proj", merged, wo, "nn", F32, 1024, 1024, D, res=x)
    n2, n2t = _rms_fwd("norm_ffn_fwd", h1, norm_ffn)
    u = _mm("ffn_up", n2, wu, "nn", F32, 1024, W_UP_SHARD, D)
    act, gelu_gate, dact_dgate = _convglu_fwd(u, conv_w8, conv_b)
    h2 = _mm("ffn_down", act, wd, "nn", F32, 512, 1024, D_FF, res=h1)
    (dh2,), (d_norm_final, loss_row) = _final(h2, target, norm_final)

    dact = _mm("ffn_down_dx", dh2, wd, "nt", BF16, 1024, D_FF, D)
    d_wd = _mm("ffn_down_dw", act, dh2, "tn", F32, D_FF // 2, 1024, DW_TK // 2)
    (du,), (d_conv_w8, d_conv_b) = _convglu_bwd(u, dact, gelu_gate, dact_dgate, conv_w8)
    dn2 = _mm("ffn_up_dx", du, wu, "nt", F32, 1024, 1024, W_UP_SHARD)
    d_wu = _mm("ffn_up_dw", n2t, du, "nn", F32, 1024, W_UP_SHARD, DW_TK)
    (dh1,), (d_norm_ffn,) = _rms_bwd("norm_ffn_bwd", h1, norm_ffn, [dn2], dh2)

    dmerged = _mm("out_proj_dx", dh1, wo, "nt", F32, 1024, 1024, D)
    d_wo = _mm("out_proj_dw", merged, dh1, "tn", F32, 1024, 1024, DW_TK)
    dpa, dpb, dga, dgb = _merge_bwd(dmerged, pa, pb, proj)
    do_a = _mm("branch_a_dx", dpa, wa, "nt", F32, 1024, 1024, D)
    do_b = _mm("branch_b_dx", dpb, wb, "nt", BF16, 1024, 1024, D)
    d_wa = _mm("branch_a_dw", o_a, dpa, "tn", F32, 1024, 1024, DW_TK)
    d_wb = _mm("branch_b_dw", o_b, dpb, "tn", F32, 1024, 1024, DW_TK)

    (do_raw, dhg), (d_hg_norm,) = _hg_post_bwd(do_a, o_raw, proj, hg_norm)
    dhq, dhf, dhi, d_lb_logits = _hg_bwd(proj, hg_lb_logits, states, do_raw)

    dfq, dfk, dfv, dcrow = _fox_bwd(qb, kb, vb, ka, o_b, laux, do_b)
    dct = jnp.pad(dcrow.reshape(FOX_H, x.shape[0]), ((0, LANES - FOX_H), (0, 0)))
    dff, d_bias = _fox_gate_bwd(dct, pff, bias)

    pieces = [dhq, dhf, dhi, dhg, dfq, dfk, dfv, dga, dgb]
    if len(later) == 5:
        dn1, late = _mm_sum_nt("in_proj_dx", pieces, w_main, (dff, w_ff), 1024, 1024,
                               exchange=_late_weight_chip_sums(d_wa, d_wb, d_wo, d_wu, d_wd))
    else:
        dn1, late = _mm_sum_nt("in_proj_dx", pieces, w_main, (dff, w_ff), 1024, 1024), (d_wa, d_wb, d_wo, d_wu, d_wd)
    d_w_main = [_mm("in_proj_dw_%d" % i, n1t, p, "nn", F32, 1024, 1024, 2 * DW_TK) for i, p in enumerate(pieces)]
    d_w_ff = _mm("in_proj_forget_dw", n1t, dff, "nn", F32, 1024, LANES, DW_TK)
    (dx,), (d_norm_mix,) = _rms_bwd("norm_mix_bwd", x, norm_mix, [dn1], dh1)

    small = dict(norm_mix=d_norm_mix, fox_f_bias=d_bias[:, :FOX_H], hg_lb_logits=d_lb_logits, hg_norm=d_hg_norm,
                 norm_ffn=d_norm_ffn, conv_b=d_conv_b, norm_final=d_norm_final, conv_w=d_conv_w8[:3], loss=loss_row)
    big = (d_w_main, d_w_ff) + tuple(late)
    return dx, small, big


SMALL_KEYS = ("norm_mix", "fox_f_bias", "hg_lb_logits", "hg_norm", "norm_ffn", "conv_b", "norm_final")


def _pack_small(parts):
    rows, layout = [], []
    for key, arr in parts:
        flat = arr.reshape(-1)
        n = flat.shape[0]
        nr = -(-n // LANES)
        rows.append(jnp.pad(flat, (0, nr * LANES - n)).reshape(nr, LANES))
        layout.append((key, arr.shape, n, nr))
    packed = jnp.concatenate(rows, axis=0)
    pad = -packed.shape[0] % 8
    return jnp.pad(packed, ((0, pad), (0, 0))), layout


def _unpack_small(packed, layout):
    out, r0 = {}, 0
    for key, shape, n, nr in layout:
        out[key] = packed[r0:r0 + nr].reshape(-1)[:n].reshape(shape)
        r0 += nr
    return out


def kernel(x, norm_mix, w_in, fox_f_bias, hg_lb_logits, hg_norm, w_branch_a, w_branch_b, w_out, norm_ffn, w_up, conv_w, conv_b, w_down, norm_final, loss_target, m_norm_mix, m_w_in, m_fox_f_bias, m_hg_lb_logits, m_hg_norm, m_w_branch_a, m_w_branch_b, m_w_out, m_norm_ffn, m_w_up, m_conv_w, m_conv_b, m_w_down, m_norm_final, v_norm_mix, v_w_in, v_fox_f_bias, v_hg_lb_logits, v_hg_norm, v_w_branch_a, v_w_branch_b, v_w_out, v_norm_ffn, v_w_up, v_conv_w, v_conv_b, v_w_down, v_norm_final):
    chip = 2 * lax.axis_index("x") + lax.axis_index("y")
    halves = _weight_halves(w_in[0], w_branch_a[0], w_branch_b[0], w_out[0], w_up[0], w_down[0], conv_w[0])
    w_main, w_ff = _unpack_w_in(_all_gather8_multi("all_gather_w_in", halves[:1])[0])
    dx, small, big = _local_step(
        x[0], loss_target[0], norm_mix, fox_f_bias, hg_lb_logits, hg_norm, norm_ffn, conv_b,
        norm_final.reshape(1, D), w_main, w_ff, halves[1:])

    packed, layout = _pack_small([(k, small[k]) for k in SMALL_KEYS + ("conv_w", "loss")])
    red = _unpack_small(_all_reduce_small("all_reduce_small", packed), layout)
    loss = red["loss"][0, 0]
    g_conv_w = lax.dynamic_slice_in_dim(red["conv_w"], chip * W_UP_SHARD, W_UP_SHARD, axis=1)

    g_big = _finish_grads(big[0], big[1], big[2:])

    names = ["norm_mix", "w_in", "fox_f_bias", "hg_lb_logits", "hg_norm", "w_branch_a", "w_branch_b", "w_out",
             "norm_ffn", "w_up", "conv_w", "conv_b", "w_down", "norm_final"]
    weights = dict(norm_mix=norm_mix, w_in=w_in, fox_f_bias=fox_f_bias, hg_lb_logits=hg_lb_logits, hg_norm=hg_norm,
                   w_branch_a=w_branch_a, w_branch_b=w_branch_b, w_out=w_out, norm_ffn=norm_ffn, w_up=w_up,
                   conv_w=conv_w, conv_b=conv_b, w_down=w_down, norm_final=norm_final)
    ms = dict(norm_mix=m_norm_mix, w_in=m_w_in, fox_f_bias=m_fox_f_bias, hg_lb_logits=m_hg_lb_logits,
              hg_norm=m_hg_norm, w_branch_a=m_w_branch_a, w_branch_b=m_w_branch_b, w_out=m_w_out,
              norm_ffn=m_norm_ffn, w_up=m_w_up, conv_w=m_conv_w, conv_b=m_conv_b, w_down=m_w_down,
              norm_final=m_norm_final)
    vs = dict(norm_mix=v_norm_mix, w_in=v_w_in, fox_f_bias=v_fox_f_bias, hg_lb_logits=v_hg_lb_logits,
              hg_norm=v_hg_norm, w_branch_a=v_w_branch_a, w_branch_b=v_w_branch_b, w_out=v_w_out,
              norm_ffn=v_norm_ffn, w_up=v_w_up, conv_w=v_conv_w, conv_b=v_conv_b, w_down=v_w_down,
              norm_final=v_norm_final)

    grads, deltas, new_m, new_v = {}, {}, {}, {}
    big_names = ["w_in", "w_branch_a", "w_branch_b", "w_out", "w_up", "w_down"]
    for name, g2 in zip(big_names, g_big):
        shape = weights[name].shape
        rows = g2.shape[0]
        d_, m_, v_ = _adamw("adamw_" + name, weights[name][0], g2, ms[name][0], vs[name][0], tm=rows // 8)
        grads[name], deltas[name], new_m[name], new_v[name] = (a.reshape(shape) for a in (g2, d_, m_, v_))
    shape = conv_w.shape
    d_, m_, v_ = _adamw("adamw_conv_w", conv_w[0], g_conv_w, m_conv_w[0], v_conv_w[0])
    grads["conv_w"], deltas["conv_w"], new_m["conv_w"], new_v["conv_w"] = (
        a.reshape(shape) for a in (g_conv_w, d_, m_, v_))
    gs = {k: red[k].reshape(weights[k].shape) for k in SMALL_KEYS}
    pw, lay = _pack_small([(k, weights[k]) for k in SMALL_KEYS])
    pg, _ = _pack_small([(k, gs[k]) for k in SMALL_KEYS])
    pm, _ = _pack_small([(k, ms[k]) for k in SMALL_KEYS])
    pv, _ = _pack_small([(k, vs[k]) for k in SMALL_KEYS])
    d_, m_, v_ = (_unpack_small(a, lay) for a in _adamw("adamw_small", pw, pg, pm, pv))
    for k in SMALL_KEYS:
        grads[k], deltas[k], new_m[k], new_v[k] = gs[k], d_[k], m_[k], v_[k]

    return (loss, dx[None], *[grads[n] for n in names], *[deltas[n] for n in names],
            *[new_m[n] for n in names], *[new_v[n] for n in names])
```

```python
import jax
import jax.numpy as jnp
from jax import lax
from jax.experimental import pallas as pl
from jax.experimental.pallas import tpu as pltpu

F32 = jnp.float32
BF16 = jnp.bfloat16

D = 1024
HG_H, HG_DK = 8, 128
FOX_H, FOX_D = 16, 64
D_FF = 2816
EPS = 1e-6
N_CHIP = 4
LANES = 128
W_IN_SHARD = 2308
W_UP_SHARD = 1408
W_DOWN_SHARD = 704
FF_COL = 7168
ADAM_LR, ADAM_B1, ADAM_B2, ADAM_EPS, ADAM_WD, ADAM_STEP = 0.001, 0.9, 0.999, 1e-08, 0.01, 10

HG_C = 16
HG_T = 1024
HG_UNROLL = 16
HG_UNROLL_BWD = 4
FOX_T = 1024
DW_TK = 2048
VMEM_LIMIT = 56 * 1024 * 1024
MESH = pl.DeviceIdType.MESH
ANY = pl.BlockSpec(memory_space=pl.ANY)


def _cparams(sem):
    return pltpu.CompilerParams(dimension_semantics=sem, vmem_limit_bytes=VMEM_LIMIT)


def _sigmoid(x):
    return 1.0 / (1.0 + jnp.exp(-x))


def _dot(a, b, dims):
    return lax.dot_general(a.astype(BF16), b.astype(BF16), (dims, ((), ())), preferred_element_type=F32)


NN = ((1,), (0,))
NT = ((1,), (1,))
TN = ((0,), (0,))


def _split_dot(tri, x, parts, dims=NN):
    acc = None
    r = x
    for _ in range(parts):
        p = r.astype(BF16)
        t = lax.dot_general(tri, p, (dims, ((), ())), preferred_element_type=F32)
        acc = t if acc is None else acc + t
        r = r - p.astype(F32)
    return acc


def _rb(arr, tm, width, cb=0):
    return (arr, (tm, width), lambda i: (i, cb))


def _cst(arr):
    return (arr, arr.shape, lambda i: (0,) * arr.ndim)


def _rows(name, fn, n_rows, tm, ins, outs, accs=()):
    n_in, n_out, n_acc = len(ins), len(outs), len(accs)
    nb = n_rows // tm

    def body(*refs):
        vals = [r[...] for r in refs[:n_in]]
        o, a = fn(*vals)
        for r, v in zip(refs[n_in:n_in + n_out], o):
            r[...] = v.astype(r.dtype)
        if n_acc:
            acc_refs = refs[n_in + n_out:]

            @pl.when(pl.program_id(0) == 0)
            def _():
                for r in acc_refs:
                    r[...] = jnp.zeros_like(r)

            for r, v in zip(acc_refs, a):
                r[...] += v

    in_specs = [pl.BlockSpec(bs, im) for (_, bs, im) in ins]
    out_specs = [pl.BlockSpec((tm, w), lambda i: (i, 0)) for (w, _) in outs]
    out_specs += [pl.BlockSpec((r, w), lambda i: (0, 0)) for (r, w) in accs]
    out_shape = [jax.ShapeDtypeStruct((n_rows, w), dt) for (w, dt) in outs]
    out_shape += [jax.ShapeDtypeStruct((r, w), F32) for (r, w) in accs]
    res = pl.pallas_call(
        body, name=name, grid=(nb,), in_specs=in_specs, out_specs=out_specs, out_shape=out_shape,
        compiler_params=_cparams(("arbitrary",)),
    )(*[a for a, _, _ in ins])
    return (res[:n_out], res[n_out:]) if n_acc else res


def _mm(name, a, b, mode, out_dtype, tm, tn, tk, res=None):
    if mode == "nn":
        (m, k), n = a.shape, b.shape[1]
    elif mode == "nt":
        (m, k), n = a.shape, b.shape[0]
    else:
        (k, m), n = a.shape, b.shape[1]
    tm, tn, tk = min(tm, m), min(tn, n), min(tk, k)
    assert m % tm == 0 and n % tn == 0 and k % tk == 0, (name, m, n, k, tm, tn, tk)
    if mode == "nn":
        a_spec = pl.BlockSpec((tm, tk), lambda i, j, kk: (i, kk))
        b_spec = pl.BlockSpec((tk, tn), lambda i, j, kk: (kk, j))
        dims = NN
    elif mode == "nt":
        a_spec = pl.BlockSpec((tm, tk), lambda i, j, kk: (i, kk))
        b_spec = pl.BlockSpec((tn, tk), lambda i, j, kk: (j, kk))
        dims = NT
    else:
        a_spec = pl.BlockSpec((tk, tm), lambda i, j, kk: (kk, i))
        b_spec = pl.BlockSpec((tk, tn), lambda i, j, kk: (kk, j))
        dims = TN
    nk = k // tk
    has_res = res is not None
    acc_in_out = out_dtype == F32 and not has_res

    def body(*refs):
        a_ref, b_ref = refs[0], refs[1]
        r_ref = refs[2] if has_res else None
        o_ref = refs[3] if has_res else refs[2]
        part = _dot(a_ref[...], b_ref[...], dims)

        def finish(val):
            if has_res:
                val = val + r_ref[...]
            o_ref[...] = val.astype(o_ref.dtype)

        if nk == 1:
            finish(part)
        elif acc_in_out:
            kk = pl.program_id(2)

            @pl.when(kk == 0)
            def _():
                o_ref[...] = part

            @pl.when(kk > 0)
            def _():
                o_ref[...] += part
        else:
            acc_ref = refs[-1]
            kk = pl.program_id(2)

            @pl.when(kk == 0)
            def _():
                acc_ref[...] = part

            @pl.when(kk > 0)
            def _():
                acc_ref[...] += part

            @pl.when(kk == nk - 1)
            def _():
                finish(acc_ref[...])

    in_specs = [a_spec, b_spec]
    args = [a, b]
    if has_res:
        in_specs.append(pl.BlockSpec((tm, tn), lambda i, j, kk: (i, j)))
        args.append(res)
    return pl.pallas_call(
        body, name=name, grid=(m // tm, n // tn, nk), in_specs=in_specs,
        out_specs=pl.BlockSpec((tm, tn), lambda i, j, kk: (i, j)),
        out_shape=jax.ShapeDtypeStruct((m, n), out_dtype),
        scratch_shapes=[pltpu.VMEM((tm, tn), F32)] if nk > 1 and not acc_in_out else [],
        compiler_params=_cparams(("arbitrary", "arbitrary", "arbitrary")),
    )(*args)


def _mm_sum_nt(name, pieces, w, extra, tm, tn, exchange=()):
    n_p = len(pieces)
    m, k = pieces[0].shape
    n = w.shape[0]
    xa, xb = extra
    ke = xa.shape[1]
    tm, tn = min(tm, m), min(tn, n)
    nx = len(exchange)
    n_steps = (m // tm) * (n // tn) * (n_p + 1)

    def exchange_steps(refs):
        e_refs = refs[n_p + 3:n_p + 3 + nx]
        out_refs, bounce_refs = refs[n_p + 4 + nx:n_p + 4 + 2 * nx], refs[n_p + 4 + 2 * nx:n_p + 4 + 3 * nx]
        send_sems, recv_sems = refs[-2], refs[-1]
        step = (pl.program_id(0) * (n // tn) + pl.program_id(1)) * (n_p + 1) + pl.program_id(2)
        x, y, c = _me()
        my_chip = 2 * x + y
        sibling = (x, y, 1 - c)
        chips = [(1 - x, y), (x, 1 - y), (1 - x, 1 - y)]
        cp = lambda kx, q, src, dst, to: _remote(src, dst, send_sems, recv_sems, kx * nx + q, to)
        sends = lambda: ([cp(kx, q, e_refs[q].at[2 * px + py], out_refs[q].at[my_chip], (px, py, c))
                          for kx, (px, py) in enumerate(chips) for q in range(nx)]
                         + [cp(3, q, e_refs[q].at[my_chip], bounce_refs[q], sibling) for q in range(nx)])
        backs = lambda: [cp(4, q, bounce_refs[q], out_refs[q].at[my_chip], sibling) for q in range(nx)]

        @pl.when(step == 0)
        def _():
            for s_ in sends():
                s_.start()

        @pl.when(step == n_steps // 2)
        def _():
            for q in range(nx):
                cp(3, q, e_refs[q].at[my_chip], bounce_refs[q], sibling).wait_recv()
            for s_ in backs():
                s_.start()

        @pl.when(step == n_steps - 1)
        def _():
            for kx, (px, py) in enumerate(chips):
                for q in range(nx):
                    cp(kx, q, e_refs[q].at[my_chip], out_refs[q].at[2 * px + py], (px, py, c)).wait_recv()
            for q in range(nx):
                cp(4, q, bounce_refs[q], out_refs[q].at[my_chip], sibling).wait_recv()
            for s_ in sends() + backs():
                s_.wait_send()

    def body(*refs):
        p_refs, w_ref, xa_ref, xb_ref, o_ref = refs[:n_p], refs[n_p], refs[n_p + 1], refs[n_p + 2], refs[n_p + 3 + nx]
        if nx:
            exchange_steps(refs)
        kk = pl.program_id(2)

        @pl.when(kk == 0)
        def _():
            o_ref[...] = _dot(p_refs[0][...], w_ref[...], NT)

        for i in range(1, n_p):
            @pl.when(kk == i)
            def _(i=i):
                o_ref[...] += _dot(p_refs[i][...], w_ref[...], NT)

        @pl.when(kk == n_p)
        def _():
            o_ref[...] += _dot(xa_ref[...], xb_ref[...], NT)

    in_specs = [pl.BlockSpec((tm, k), lambda i, j, kk: (i, 0)) for _ in range(n_p)]
    in_specs.append(pl.BlockSpec((tn, k), lambda i, j, kk: (j, jnp.minimum(kk, n_p - 1))))
    in_specs += [pl.BlockSpec((tm, ke), lambda i, j, kk: (i, 0)), pl.BlockSpec((tn, ke), lambda i, j, kk: (j, 0))]
    outs = pl.pallas_call(
        body, name=name, grid=(m // tm, n // tn, n_p + 1), in_specs=in_specs + [ANY] * nx,
        out_specs=[pl.BlockSpec((tm, tn), lambda i, j, kk: (i, j))] + [ANY] * (2 * nx),
        out_shape=[jax.ShapeDtypeStruct((m, n), F32)] + [jax.ShapeDtypeStruct(e.shape, e.dtype) for e in exchange]
        + [jax.ShapeDtypeStruct(e.shape[1:], e.dtype) for e in exchange],
        scratch_shapes=[pltpu.SemaphoreType.DMA((5 * nx,)), pltpu.SemaphoreType.DMA((5 * nx,))] if nx else [],
        compiler_params=_cparams(("arbitrary", "arbitrary", "arbitrary")),
    )(*pieces, w, xa, xb, *exchange)
    return (outs[0], outs[1:1 + nx]) if nx else outs[0]


def _rms_fwd(name, x, gain, tm=256):
    s = x.shape[0]

    def body(x_ref, g_ref, y_ref, yt_ref):
        xb = x_ref[...]
        y = xb * lax.rsqrt(jnp.mean(xb * xb, axis=-1, keepdims=True) + EPS) * g_ref[...]
        y_ref[...] = y.astype(BF16)
        yt_ref[...] = y.T.astype(BF16)

    return pl.pallas_call(
        body, name=name, grid=(s // tm,),
        in_specs=[pl.BlockSpec((tm, D), lambda i: (i, 0)), pl.BlockSpec((1, D), lambda i: (0, 0))],
        out_specs=[pl.BlockSpec((tm, D), lambda i: (i, 0)), pl.BlockSpec((D, tm), lambda i: (0, i))],
        out_shape=[jax.ShapeDtypeStruct((s, D), BF16), jax.ShapeDtypeStruct((D, s), BF16)],
        compiler_params=_cparams(("arbitrary",)),
    )(x, gain)


def _rms_bwd(name, x, gain, dns, dres, tm=256):
    s = x.shape[0]
    n_dn = len(dns)

    def fn(xb, g, *rest):
        dn = rest[0]
        for t in rest[1:n_dn]:
            dn = dn + t
        r = lax.rsqrt(jnp.mean(xb * xb, axis=-1, keepdims=True) + EPS)
        xhat = xb * r
        dxh = dn * g
        dx = r * (dxh - xhat * jnp.mean(dxh * xhat, axis=-1, keepdims=True)) + rest[n_dn]
        return (dx,), (jnp.sum(dn * xhat, axis=0, keepdims=True),)

    ins = [_rb(x, tm, D), _cst(gain)] + [_rb(t, tm, D) for t in dns] + [_rb(dres, tm, D)]
    return _rows(name, fn, s, tm, ins, [(D, F32)], [(1, D)])


def _final(h2, target, gain, tm=256):
    s = h2.shape[0]

    def fn(hb, tb, g):
        r = lax.rsqrt(jnp.mean(hb * hb, axis=-1, keepdims=True) + EPS)
        xhat = hb * r
        e = xhat * g - tb
        dy = e * (1.0 / D)
        dxh = dy * g
        dh = r * (dxh - xhat * jnp.mean(dxh * xhat, axis=-1, keepdims=True))
        lrow = 0.5 * jnp.sum(jnp.sum(e * e, axis=-1, keepdims=True) * (1.0 / D), axis=0, keepdims=True)
        return (dh,), (jnp.sum(dy * xhat, axis=0, keepdims=True), jnp.broadcast_to(lrow, (1, LANES)))

    return _rows("final_norm_loss", fn, s, tm, [_rb(h2, tm, D), _rb(target, tm, D), _cst(gain)],
                 [(D, F32)], [(1, D), (1, LANES)])


def _merge_fwd(pa, pb, proj, tm=256):
    s = pa.shape[0]

    def fn(a, b, ga, gb):
        return (_sigmoid(ga) * a + _sigmoid(gb) * b,), ()

    ins = [_rb(pa, tm, D), _rb(pb, tm, D), _rb(proj, tm, D, 7), _rb(proj, tm, D, 8)]
    return _rows("merge_fwd", fn, s, tm, ins, [(D, BF16)])[0]


def _merge_bwd(dmerged, pa, pb, proj, tm=256):
    s = pa.shape[0]

    def fn(dm, a, b, ga, gb):
        sa, sb = _sigmoid(ga), _sigmoid(gb)
        return (dm * sa, dm * sb, dm * a * sa * (1.0 - sa), dm * b * sb * (1.0 - sb)), ()

    ins = [_rb(dmerged, tm, D), _rb(pa, tm, D), _rb(pb, tm, D), _rb(proj, tm, D, 7), _rb(proj, tm, D, 8)]
    return _rows("merge_bwd", fn, s, tm, ins, [(D, BF16), (D, BF16), (D, BF16), (D, BF16)])


def _gelu_parts(x):
    cdf = 0.5 * (1.0 + lax.erf(x * 0.7071067811865476))
    pdf = 0.3989422804014327 * jnp.exp(-0.5 * x * x)
    return x * cdf, cdf + x * pdf


def _conv_taps(u_ext, n_out):
    cur = u_ext[8:8 + n_out]
    m1 = pltpu.roll(u_ext, 1, 0)[8:8 + n_out]
    m2 = pltpu.roll(u_ext, 2, 0)[8:8 + n_out]
    return m2, m1, cur


def _convglu_fwd(u, conv_w8, conv_b, tm=256):
    s, w = u.shape
    tb = tm // 8

    def fn(ub, up, cw, cb):
        i = pl.program_id(0)
        up = jnp.where(i == 0, 0.0, up)
        m2, m1, cur = _conv_taps(jnp.concatenate([up, ub], axis=0), tm)
        acc = cb + cw[0:1] * m2 + cw[1:2] * m1 + cw[2:3] * cur
        gl, dgl = _gelu_parts(acc[:, :D_FF])
        val = acc[:, D_FF:]
        return (gl * val, gl, val * dgl), ()

    ins = [_rb(u, tm, w), (u, (8, w), lambda i: (jnp.maximum(i * tb - 1, 0), 0)), _cst(conv_w8), _cst(conv_b)]
    return _rows("convglu_fwd", fn, s, tm, ins, [(D_FF, BF16)] * 3)


def _convglu_bwd(u, dact, gl, gd, conv_w8, tm=256):
    s, w = u.shape
    tb = tm // 8
    nb = s // tm

    def fn(ub, up, db, dn, glb, gln, gdb, gdn, cw):
        i = pl.program_id(0)
        up = jnp.where(i == 0, 0.0, up)
        dn = jnp.where(i == nb - 1, 0.0, dn.astype(F32))
        ne = tm + 8
        m2, m1, cur = _conv_taps(jnp.concatenate([up, ub], axis=0), tm)
        ext = lambda blk, nxt: jnp.concatenate([blk.astype(F32), nxt.astype(F32)], axis=0)
        de = ext(db, dn)
        dacc = jnp.concatenate([de * ext(gdb, gdn), de * ext(glb, gln)], axis=1)
        p1 = pltpu.roll(dacc, ne - 1, 0)[:tm]
        p2 = pltpu.roll(dacc, ne - 2, 0)[:tm]
        d0 = dacc[:tm]
        du = cw[2:3] * d0 + cw[1:2] * p1 + cw[0:1] * p2
        zero5 = jnp.zeros((5, w), F32)
        dcw = jnp.concatenate([
            jnp.sum(d0 * m2, axis=0, keepdims=True), jnp.sum(d0 * m1, axis=0, keepdims=True),
            jnp.sum(d0 * cur, axis=0, keepdims=True), zero5], axis=0)
        return (du,), (dcw, jnp.sum(d0, axis=0, keepdims=True))

    nxt = lambda arr: (arr, (8, D_FF), lambda i: (jnp.minimum((i + 1) * tb, s // 8 - 1), 0))
    ins = [_rb(u, tm, w), (u, (8, w), lambda i: (jnp.maximum(i * tb - 1, 0), 0)),
           _rb(dact, tm, D_FF), nxt(dact), _rb(gl, tm, D_FF), nxt(gl), _rb(gd, tm, D_FF), nxt(gd), _cst(conv_w8)]
    return _rows("convglu_bwd", fn, s, tm, ins, [(w, BF16)], [(8, w), (1, w)])


def _chunk_scan(x, t_iota, reverse):
    k = 1
    while k < HG_C:
        if reverse:
            x = x + jnp.where(t_iota < HG_C - k, pltpu.roll(x, HG_C - k, 0), 0.0)
        else:
            x = x + jnp.where(t_iota >= k, pltpu.roll(x, k, 0), 0.0)
        k *= 2
    return x


def _hg_gates(hq, hf, lb):
    sq = _sigmoid(hq)
    q = hq * sq
    sg = _sigmoid(hf)
    f = lb + (1.0 - lb) * sg
    return q, sq, sg, f, 1.0 - f, jnp.log(f)


def _lb_of(logits):
    l0, l1 = logits[0:1], logits[1:2]
    mx = jnp.maximum(l0, l1)
    e0, e1 = jnp.exp(l0 - mx), jnp.exp(l1 - mx)
    return e0 / (e0 + e1)


def _tri(n, lower):
    r = lax.broadcasted_iota(jnp.int32, (n, n), 0)
    c = lax.broadcasted_iota(jnp.int32, (n, n), 1)
    return jnp.where((r >= c) if lower else (r <= c), 1.0, 0.0).astype(BF16)


def _hg_intra_terms(q, kk, b, t_iota):
    ws, ps = [], []
    for s in range(HG_C):
        p = jnp.where(t_iota >= s, jnp.exp(b - b[s:s + 1]), 0.0)
        ps.append(p)
        ws.append(q * kk[s:s + 1] * p)
    return jnp.concatenate(ws, axis=0), ps


def _hg_fwd(proj, lb_logits):
    s = proj.shape[0]
    nt = s // HG_T
    nc = HG_T // HG_C

    def body(q_ref, f_ref, i_ref, l_ref, o_ref, st_ref, state):
        @pl.when(pl.program_id(1) == 0)
        def _():
            state[...] = jnp.zeros_like(state)

        st_ref[0, 0] = state[...]
        lb = _lb_of(l_ref[...])
        ones = jnp.ones((HG_DK, HG_DK), BF16)
        t_iota = lax.broadcasted_iota(jnp.int32, (HG_C, HG_DK), 0)
        cc = HG_C * HG_C

        def group(gi, st):
            units = []
            for u in range(HG_UNROLL):
                r = pl.ds(pl.multiple_of((gi * HG_UNROLL + u) * HG_C, HG_C), HG_C)
                q, _, _, _, kk, g = _hg_gates(q_ref[r, :], f_ref[r, :], lb)
                b = _chunk_scan(g, t_iota, False)
                b_end = b[HG_C - 1:HG_C]
                w_all, _ = _hg_intra_terms(q, kk, b, t_iota)
                units.append((r, i_ref[r, :], q * jnp.exp(b), jnp.exp(b_end), kk * jnp.exp(b_end - b), w_all))
            a_all = _dot(jnp.concatenate([un[5] for un in units], axis=0), ones, NN)
            kvs = [_dot(v, kd, TN) for (_, v, _, _, kd, _) in units]
            sts = [st]
            for (_, _, _, dec, _, _), kv in zip(units, kvs):
                sts.append(sts[-1] * dec + kv)
            for ui, (r, v, qd, _, _, _) in enumerate(units):
                o = _dot(qd, sts[ui], NT)
                for si in range(HG_C):
                    o = o + a_all[ui * cc + si * HG_C:ui * cc + (si + 1) * HG_C] * v[si:si + 1]
                o_ref[r, :] = o
            return sts[-1]

        state[...] = lax.fori_loop(0, nc // HG_UNROLL, group, state[...])

    col = lambda off: pl.BlockSpec((HG_T, HG_DK), lambda h, t: (t, off + h))
    return pl.pallas_call(
        body, name="hgrn2_fwd", grid=(HG_H, nt),
        in_specs=[col(0), col(8), col(16), pl.BlockSpec((2, HG_DK), lambda h, t: (0, h))],
        out_specs=[pl.BlockSpec((HG_T, HG_DK), lambda h, t: (t, h)),
                   pl.BlockSpec((1, 1, HG_DK, HG_DK), lambda h, t: (h, t, 0, 0))],
        out_shape=[jax.ShapeDtypeStruct((s, D), F32), jax.ShapeDtypeStruct((HG_H, nt, HG_DK, HG_DK), F32)],
        scratch_shapes=[pltpu.VMEM((HG_DK, HG_DK), F32)],
        compiler_params=_cparams(("arbitrary", "arbitrary")),
    )(proj, proj, proj, lb_logits)


def _hg_bwd(proj, lb_logits, states, do_raw):
    s = proj.shape[0]
    nt = s // HG_T
    nc = HG_T // HG_C

    def body(q_ref, f_ref, i_ref, l_ref, st_ref, do_ref, dq_ref, df_ref, di_ref, dl_ref, st_all, adj):
        tb = pl.program_id(1)

        @pl.when(tb == 0)
        def _():
            adj[...] = jnp.zeros_like(adj)
            dl_ref[...] = jnp.zeros_like(dl_ref)

        lb = _lb_of(l_ref[...])
        ones = jnp.ones((HG_DK, HG_DK), BF16)
        t_iota = lax.broadcasted_iota(jnp.int32, (HG_C, HG_DK), 0)
        cc = HG_C * HG_C

        def fwd_group(gi, st):
            terms = []
            for u in range(HG_UNROLL):
                ci = gi * HG_UNROLL + u
                r = pl.ds(pl.multiple_of(ci * HG_C, HG_C), HG_C)
                _, _, _, _, kk, g = _hg_gates(q_ref[r, :], f_ref[r, :], lb)
                b = _chunk_scan(g, t_iota, False)
                b_end = b[HG_C - 1:HG_C]
                terms.append((ci, jnp.exp(b_end), _dot(i_ref[r, :], kk * jnp.exp(b_end - b), TN)))
            for ci, dec, kv in terms:
                st_all[ci] = st
                st = st * dec + kv
            return st

        lax.fori_loop(0, nc // HG_UNROLL, fwd_group, st_ref[0, 0])

        def bwd_group(gj, dlb):
            units = []
            for u in range(HG_UNROLL_BWD):
                ci = nc - 1 - (gj * HG_UNROLL_BWD + u)
                r = pl.ds(pl.multiple_of(ci * HG_C, HG_C), HG_C)
                hq, hf, v, do = q_ref[r, :], f_ref[r, :], i_ref[r, :], do_ref[r, :]
                q, sq, sg, f, kk, g = _hg_gates(hq, hf, lb)
                b = _chunk_scan(g, t_iota, False)
                b_end = b[HG_C - 1:HG_C]
                e_b, e_be, dec = jnp.exp(b), jnp.exp(b_end - b), jnp.exp(b_end)
                w_all, ps = _hg_intra_terms(q, kk, b, t_iota)
                x_all = jnp.concatenate([do * v[si:si + 1] for si in range(HG_C)], axis=0)
                units.append(dict(ci=ci, r=r, hq=hq, v=v, do=do, q=q, sq=sq, sg=sg, f=f, kk=kk, e_b=e_b, e_be=e_be,
                                  dec=dec, kd=kk * e_be, w=w_all, ps=ps, x=x_all))
            both = _dot(jnp.concatenate([un["w"] for un in units] + [un["x"] for un in units], axis=0), ones, NN)
            st0s = [st_all[un["ci"]] for un in units]
            st_ends = [st0 * un["dec"] + _dot(un["v"], un["kd"], TN) for un, st0 in zip(units, st0s)]
            dqks = [_dot(un["do"], un["q"] * un["e_b"], TN) for un in units]
            es = [adj[...]]
            for un, dqk in zip(units, dqks):
                es.append(es[-1] * un["dec"] + dqk)
            adj[...] = es[-1]
            for ui, un in enumerate(units):
                e, q, kk, v, do = es[ui], un["q"], un["kk"], un["v"], un["do"]
                tail = jnp.sum(e * st_ends[ui], axis=0, keepdims=True)
                dq = un["e_b"] * _dot(do, st0s[ui], NN)
                dk = un["e_be"] * _dot(v, e, NN)
                dv = _dot(un["kd"], e, NT)
                a0 = ui * cc
                d0 = (HG_UNROLL_BWD + ui) * cc
                for si in range(HG_C):
                    da = both[d0 + si * HG_C:d0 + (si + 1) * HG_C]
                    aa = both[a0 + si * HG_C:a0 + (si + 1) * HG_C]
                    dap = da * un["ps"][si]
                    dq = dq + dap * kk[si:si + 1]
                    hit = t_iota == si
                    dk = dk + jnp.where(hit, jnp.sum(dap * q, axis=0, keepdims=True), 0.0)
                    dv = dv + jnp.where(hit, jnp.sum(aa * do, axis=0, keepdims=True), 0.0)
                dg = _chunk_scan(q * dq - kk * dk, t_iota, True) + tail
                dfg = dg / un["f"] - dk
                sq, sg, hq, r = un["sq"], un["sg"], un["hq"], un["r"]
                dq_ref[r, :] = (dq * sq * (1.0 + hq * (1.0 - sq))).astype(dq_ref.dtype)
                df_ref[r, :] = (dfg * (1.0 - lb) * sg * (1.0 - sg)).astype(df_ref.dtype)
                di_ref[r, :] = dv.astype(di_ref.dtype)
                dlb = dlb + jnp.sum(dfg * (1.0 - sg), axis=0, keepdims=True)
            return dlb

        dlb = lax.fori_loop(0, nc // HG_UNROLL_BWD, bwd_group, jnp.zeros((1, HG_DK), F32))
        dl0 = dlb * lb * (1.0 - lb)
        dl_ref[...] += jnp.concatenate([dl0, -dl0], axis=0)

    col = lambda off: pl.BlockSpec((HG_T, HG_DK), lambda h, t: (nt - 1 - t, off + h))
    out_col = pl.BlockSpec((HG_T, HG_DK), lambda h, t: (nt - 1 - t, h))
    return pl.pallas_call(
        body, name="hgrn2_bwd", grid=(HG_H, nt),
        in_specs=[col(0), col(8), col(16), pl.BlockSpec((2, HG_DK), lambda h, t: (0, h)),
                  pl.BlockSpec((1, 1, HG_DK, HG_DK), lambda h, t: (h, nt - 1 - t, 0, 0)), col(0)],
        out_specs=[out_col, out_col, out_col, pl.BlockSpec((2, HG_DK), lambda h, t: (0, h))],
        out_shape=[jax.ShapeDtypeStruct((s, D), BF16)] * 3 + [jax.ShapeDtypeStruct((2, D), F32)],
        scratch_shapes=[pltpu.VMEM((nc, HG_DK, HG_DK), F32), pltpu.VMEM((HG_DK, HG_DK), F32)],
        compiler_params=_cparams(("arbitrary", "arbitrary")),
    )(proj, proj, proj, lb_logits, states, do_raw)


def _hg_post_fwd(o_raw, proj, gnorm, tm=256):
    s = o_raw.shape[0]

    def fn(o, hg, gn):
        outs = []
        for h in range(HG_H):
            sl = slice(h * HG_DK, (h + 1) * HG_DK)
            oh, gh = o[:, sl], hg[:, sl]
            r = lax.rsqrt(jnp.mean(oh * oh, axis=-1, keepdims=True) + EPS)
            outs.append(oh * r * gn * (gh * _sigmoid(gh)))
        return (jnp.concatenate(outs, axis=1),), ()

    return _rows("hgrn2_out_fwd", fn, s, tm, [_rb(o_raw, tm, D), _rb(proj, tm, D, 3), _cst(gnorm)], [(D, BF16)])[0]


def _hg_post_bwd(do_a, o_raw, proj, gnorm, tm=256):
    s = o_raw.shape[0]

    def fn(da, o, hg, gn):
        dos, dhgs = [], []
        dgn = jnp.zeros((1, HG_DK), F32)
        for h in range(HG_H):
            sl = slice(h * HG_DK, (h + 1) * HG_DK)
            oh, gh, dh = o[:, sl], hg[:, sl], da[:, sl]
            r = lax.rsqrt(jnp.mean(oh * oh, axis=-1, keepdims=True) + EPS)
            xhat = oh * r
            sg = _sigmoid(gh)
            dy = dh * (gh * sg)
            dhgs.append(dh * xhat * gn * sg * (1.0 + gh * (1.0 - sg)))
            dgn = dgn + jnp.sum(dy * xhat, axis=0, keepdims=True)
            dxh = dy * gn
            dos.append(r * (dxh - xhat * jnp.mean(dxh * xhat, axis=-1, keepdims=True)))
        return (jnp.concatenate(dos, axis=1), jnp.concatenate(dhgs, axis=1)), (dgn,)

    ins = [_rb(do_a, tm, D), _rb(o_raw, tm, D), _rb(proj, tm, D, 3), _cst(gnorm)]
    return _rows("hgrn2_out_bwd", fn, s, tm, ins, [(D, F32), (D, BF16)], [(1, HG_DK)])


def _log_sigmoid(z):
    return jnp.minimum(z, 0.0) - jnp.log(1.0 + jnp.exp(-jnp.abs(z)))


def _fox_gate_bwd(dct, pff, bias, tm=256):
    s = pff.shape[0]
    nb = s // tm

    def body(d_ref, p_ref, b_ref, dff_ref, db_ref, carry):
        @pl.when(pl.program_id(0) == 0)
        def _():
            carry[...] = jnp.zeros_like(carry)
            db_ref[...] = jnp.zeros_like(db_ref)

        dc = d_ref[...].T
        dlf = _split_dot(_tri(tm, False), dc, 3) + carry[0:1]
        carry[...] = jnp.broadcast_to(dlf[0:1], carry.shape)
        dff = dlf * _sigmoid(-(p_ref[...] + b_ref[...]))
        dff_ref[...] = dff
        db_ref[...] += jnp.sum(dff, axis=0, keepdims=True)

    return pl.pallas_call(
        body, name="fox_gate_bwd", grid=(nb,),
        in_specs=[pl.BlockSpec((LANES, tm), lambda i: (0, nb - 1 - i)),
                  pl.BlockSpec((tm, LANES), lambda i: (nb - 1 - i, 0)), pl.BlockSpec((1, LANES), lambda i: (0, 0))],
        out_specs=[pl.BlockSpec((tm, LANES), lambda i: (nb - 1 - i, 0)), pl.BlockSpec((1, LANES), lambda i: (0, 0))],
        out_shape=[jax.ShapeDtypeStruct((s, LANES), F32), jax.ShapeDtypeStruct((1, LANES), F32)],
        scratch_shapes=[pltpu.VMEM((8, LANES), F32)],
        compiler_params=_cparams(("arbitrary",)),
    )(dct, pff, bias)


def _diag_mask(t):
    r = lax.broadcasted_iota(jnp.int32, (t, t), 0)
    c = lax.broadcasted_iota(jnp.int32, (t, t), 1)
    return r >= c


AUX_ONES = 6


def _pieces(x):
    h = x.astype(BF16)
    r = x - h.astype(F32)
    m = r.astype(BF16)
    return h, m, (r - m.astype(F32)).astype(BF16)


def _lane_put(lane, cols, base):
    out = None
    for i, col in enumerate(cols):
        term = jnp.where(lane == base + i, col.astype(F32), 0.0)
        out = term if out is None else out + term
    return out


def _fox_prep(proj, pff, bias, tm=256):
    s = pff.shape[0]

    def body(q_ref, k_ref, v_ref, p_ref, b_ref, qb_ref, kb_ref, vb_ref, ka_ref, carry):
        @pl.when(pl.program_id(0) == 0)
        def _():
            carry[...] = jnp.zeros_like(carry)

        qb_ref[...] = (q_ref[...] * 0.125).astype(BF16)
        kb_ref[...] = k_ref[...].astype(BF16)
        vb_ref[...] = v_ref[...].astype(BF16)
        lf = _log_sigmoid(p_ref[...] + b_ref[...])
        c = _split_dot(_tri(tm, True), lf, 3) + carry[0:1]
        carry[...] = jnp.broadcast_to(c[tm - 1:tm], carry.shape)
        lane = lax.broadcasted_iota(jnp.int32, (tm, LANES), 1)
        ones = jnp.where((lane >= AUX_ONES) & (lane < AUX_ONES + 6), 1.0, 0.0)
        for p in range(FOX_H // 2):
            aux = ones
            for z in range(2):
                col = jnp.sum(jnp.where(lane == 2 * p + z, c, 0.0), axis=1, keepdims=True)
                aux = aux + _lane_put(lane, _pieces(-col), 3 * z)
            ka_ref[:, p * LANES:(p + 1) * LANES] = aux.astype(BF16)

    row = lambda cb: pl.BlockSpec((tm, D), lambda i: (i, cb))
    return pl.pallas_call(
        body, name="fox_prep", grid=(s // tm,),
        in_specs=[row(4), row(5), row(6), pl.BlockSpec((tm, LANES), lambda i: (i, 0)),
                  pl.BlockSpec((1, LANES), lambda i: (0, 0))],
        out_specs=[row(0)] * 4, out_shape=[jax.ShapeDtypeStruct((s, D), BF16)] * 4,
        scratch_shapes=[pltpu.VMEM((8, LANES), F32)],
        compiler_params=_cparams(("arbitrary",)),
    )(proj, proj, proj, pff, bias)


def _fox_fwd(qb, kb, vb, ka):
    s = qb.shape[0]
    t = min(FOX_T, s)
    nq = s // t

    def body(q_ref, k_ref, v_ref, ka_ref, o_ref, la_ref):
        i = pl.program_id(1)
        lane = lax.broadcasted_iota(jnp.int32, (t, LANES), 1)
        in_a = lane < FOX_D
        q = q_ref[...]
        zero = jnp.zeros_like(q)
        qh = [jnp.where(in_a, q, zero), jnp.where(in_a, zero, q)]
        c_ones = [jnp.where((lane >= 3 * z) & (lane < 3 * z + 3), 1.0, 0.0) for z in range(2)]

        def keys(j):
            rows = pl.ds(pl.multiple_of(j * t, t), t)
            return jnp.concatenate([k_ref[rows, :], ka_ref[rows, :]], axis=1), rows

        dmask = _diag_mask(t)

        def logits(qx, kk, masked):
            e = lax.dot_general(qx, kk, (NT, ((), ())), preferred_element_type=F32)
            return jnp.where(dmask, e, -1e30) if masked else e

        qc = [jnp.concatenate([qh[z], c_ones[z].astype(BF16)], axis=1) for z in range(2)]

        def step(j, carry, masked):
            kk, rows = keys(j)
            vj = v_ref[rows, :]
            scores = [logits(qc[z], kk, masked) for z in range(2)]
            one = jnp.ones_like(vj)
            vh = [jnp.where(in_a, vj, one), jnp.where(in_a, one, vj)]
            out = []
            for z in range(2):
                m, acc = carry[z]
                ms, accs = [], []
                for r0 in range(0, t, t // 2):
                    rs = slice(r0, r0 + t // 2)
                    m_new = jnp.maximum(m[rs], jnp.max(scores[z][rs], axis=1, keepdims=True))
                    p = jnp.exp(scores[z][rs] - m_new)
                    ms.append(m_new)
                    accs.append(jnp.exp(m[rs] - m_new) * acc[rs] + _dot(p, vh[z], NN))
                out.append((jnp.concatenate(ms, axis=0), jnp.concatenate(accs, axis=0)))
            return tuple(out)

        init = tuple((jnp.full((t, 1), -1e30, F32), jnp.zeros((t, LANES), F32)) for _ in range(2))
        (ma, acc_a), (mb, acc_b) = step(i, lax.fori_loop(0, i, lambda j, c: step(j, c, False), init), True)
        la = jnp.sum(jnp.where(lane == FOX_D, acc_a, 0.0), axis=1, keepdims=True)
        lb = jnp.sum(jnp.where(lane == 0, acc_b, 0.0), axis=1, keepdims=True)
        o_ref[...] = jnp.where(in_a, acc_a / la, acc_b / lb).astype(o_ref.dtype)
        la_ref[...] = (_lane_put(lane, _pieces(-(ma + jnp.log(la))), AUX_ONES)
                       + _lane_put(lane, _pieces(-(mb + jnp.log(lb))), AUX_ONES + 3)).astype(la_ref.dtype)

    blk = pl.BlockSpec((t, LANES), lambda p, i: (i, p))
    whole = pl.BlockSpec((s, LANES), lambda p, i: (0, p))
    return pl.pallas_call(
        body, name="fox_attn_fwd", grid=(FOX_H // 2, nq), in_specs=[blk, whole, whole, whole],
        out_specs=[blk, blk], out_shape=[jax.ShapeDtypeStruct((s, D), BF16)] * 2,
        compiler_params=_cparams(("arbitrary", "arbitrary")),
    )(qb, kb, vb, ka)


def _fox_bwd(qb, kb, vb, ka, ob, laux, dob):
    s = qb.shape[0]
    t = min(FOX_T, s)
    nq = s // t

    def body(q_ref, k_ref, v_ref, ka_ref, o_ref, la_ref, do_ref, dq_ref, dk_ref, dv_ref, dc_ref, dkt, dvt):
        i = pl.program_id(1)

        @pl.when(i == 0)
        def _():
            dkt[...] = jnp.zeros_like(dkt)
            dvt[...] = jnp.zeros_like(dvt)
            dc_ref[...] = jnp.zeros_like(dc_ref)

        lane = lax.broadcasted_iota(jnp.int32, (t, LANES), 1)
        in_a = lane < FOX_D
        q, do, la = q_ref[...], do_ref[...], la_ref[...].astype(F32)
        zero = jnp.zeros_like(q)
        qh = [jnp.where(in_a, q, zero), jnp.where(in_a, zero, q)]
        doh = [jnp.where(in_a, do, zero), jnp.where(in_a, zero, do)]
        qt = [h.astype(F32).T.astype(BF16) for h in qh]
        dot_ = [h.astype(F32).T.astype(BF16) for h in doh]
        prod = do.astype(F32) * o_ref[...].astype(F32)
        qx, dox = [], []
        for z in range(2):
            delta = jnp.sum(jnp.where(in_a if z == 0 else ~in_a, prod, 0.0), axis=1, keepdims=True)
            c_ones = jnp.where((lane >= 3 * z) & (lane < 3 * z + 3), 1.0, 0.0)
            lse_lanes = (lane >= AUX_ONES + 3 * z) & (lane < AUX_ONES + 3 * z + 3)
            qx.append(jnp.concatenate([qh[z], (c_ones + jnp.where(lse_lanes, la, 0.0)).astype(BF16)], axis=1))
            dox.append(jnp.concatenate([doh[z], _lane_put(lane, _pieces(-delta), 3 * z).astype(BF16)], axis=1))
        v_ones = jnp.where(lane < 6, 1.0, 0.0).astype(BF16)
        dmask = _diag_mask(t)

        def step(j, carry, masked):
            rows = pl.ds(pl.multiple_of(j * t, t), t)
            kj, vj = k_ref[rows, :], v_ref[rows, :]
            kk = jnp.concatenate([kj, ka_ref[rows, :]], axis=1)
            vv = jnp.concatenate([vj, v_ones], axis=1)
            out = []
            dk_add, dv_add = None, None
            for z in range(2):
                dq, rsum = carry[z]
                e = lax.dot_general(qx[z], kk, (NT, ((), ())), preferred_element_type=F32)
                if masked:
                    e = jnp.where(dmask, e, -1e30)
                p = jnp.exp(e)
                ds = p * lax.dot_general(dox[z], vv, (NT, ((), ())), preferred_element_type=F32)
                dkz, dvz = _dot(qt[z], ds, NN), _dot(dot_[z], p, NN)
                dk_add = dkz if dk_add is None else dk_add + dkz
                dv_add = dvz if dv_add is None else dv_add + dvz
                dc_ref[0, z, j] += -jnp.sum(ds, axis=0, keepdims=True)
                out.append((dq + _dot(ds, kj, NN), rsum + jnp.sum(ds, axis=1, keepdims=True)))
            dkt[j] += dk_add
            dvt[j] += dv_add
            return tuple(out)

        init = tuple((jnp.zeros((t, LANES), F32), jnp.zeros((t, 1), F32)) for _ in range(2))
        (dq_a, rs_a), (dq_b, rs_b) = step(i, lax.fori_loop(0, i, lambda j, c: step(j, c, False), init), True)
        for z, rs in enumerate((rs_a, rs_b)):
            dc_ref[0, z, i] += jnp.transpose(jnp.broadcast_to(rs, (t, LANES)))[0:1]
        dq_ref[...] = (jnp.where(in_a, dq_a, dq_b) * 0.125).astype(dq_ref.dtype)

        @pl.when(i == nq - 1)
        def _():
            for jb in range(nq):
                dk_ref[jb * t:(jb + 1) * t, :] = dkt[jb].T.astype(dk_ref.dtype)
                dv_ref[jb * t:(jb + 1) * t, :] = dvt[jb].T.astype(dv_ref.dtype)

    blk = pl.BlockSpec((t, LANES), lambda p, i: (i, p))
    whole = pl.BlockSpec((s, LANES), lambda p, i: (0, p))
    return pl.pallas_call(
        body, name="fox_attn_bwd", grid=(FOX_H // 2, nq),
        in_specs=[blk, whole, whole, whole, blk, blk, blk],
        out_specs=[blk, whole, whole, pl.BlockSpec((1, 2, nq, 1, t), lambda p, i: (p, 0, 0, 0, 0))],
        out_shape=[jax.ShapeDtypeStruct((s, D), BF16)] * 3 + [jax.ShapeDtypeStruct((FOX_H // 2, 2, nq, 1, t), F32)],
        scratch_shapes=[pltpu.VMEM((nq, LANES, t), F32), pltpu.VMEM((nq, LANES, t), F32)],
        compiler_params=_cparams(("arbitrary", "arbitrary")),
    )(qb, kb, vb, ka, ob, laux, dob)


def _adamw(name, w, g, m, v, tm=None):
    rows, width = w.shape
    tm = rows if tm is None else tm
    c1 = 1.0 - ADAM_B1 ** ADAM_STEP
    c2 = 1.0 - ADAM_B2 ** ADAM_STEP

    def fn(wb, gb, mb, vb):
        m_new = ADAM_B1 * mb + (1.0 - ADAM_B1) * gb
        v_new = ADAM_B2 * vb + (1.0 - ADAM_B2) * (gb * gb)
        delta = -ADAM_LR * ((m_new / c1) / (jnp.sqrt(v_new / c2) + ADAM_EPS) + ADAM_WD * wb)
        return (delta, m_new, v_new), ()

    ins = [_rb(a, tm, width) for a in (w, g, m, v)]
    return _rows(name, fn, rows, tm, ins, [(width, F32)] * 3)


def _me():
    return lax.axis_index("x"), lax.axis_index("y"), lax.axis_index("c")


def _all_reduce_small(name, block):
    r, n = block.shape

    def body(x_ref, sum_ref, gath, send_sems, recv_sems):
        x, y, c = _me()
        me = 4 * x + 2 * y + c
        gath[me] = x_ref[...]
        sends = []
        for k in range(1, 8):
            px = x ^ ((k >> 2) & 1)
            py = y ^ ((k >> 1) & 1)
            pc = c ^ (k & 1)
            sends.append(pltpu.make_async_remote_copy(
                src_ref=x_ref, dst_ref=gath.at[me], send_sem=send_sems.at[k - 1], recv_sem=recv_sems.at[k - 1],
                device_id=(px, py, pc), device_id_type=MESH))
        for cp in sends:
            cp.start()
        for k in range(1, 8):
            peer = me ^ k
            pltpu.make_async_remote_copy(
                src_ref=x_ref, dst_ref=gath.at[peer], send_sem=send_sems.at[k - 1], recv_sem=recv_sems.at[k - 1],
                device_id=(x, y, c), device_id_type=MESH).wait_recv()
        for cp in sends:
            cp.wait_send()
        acc = gath[0]
        for d in range(1, 8):
            acc = acc + gath[d]
        sum_ref[...] = acc

    vm = pl.BlockSpec(memory_space=pltpu.VMEM)
    return pl.pallas_call(
        body, name=name, in_specs=[vm], out_specs=vm, out_shape=jax.ShapeDtypeStruct((r, n), F32),
        scratch_shapes=[pltpu.VMEM((8, r, n), F32), pltpu.SemaphoreType.DMA((7,)), pltpu.SemaphoreType.DMA((7,))],
    )(block)


WD_EXT_ROWS = 736


def _remote(src, dst, send_sems, recv_sems, k, to):
    return pltpu.make_async_remote_copy(src_ref=src, dst_ref=dst, send_sem=send_sems.at[k], recv_sem=recv_sems.at[k],
                                        device_id=to, device_id_type=MESH)


def _all_gather8_multi(name, blocks):
    nt = len(blocks)

    def body(*refs):
        x_refs, out_refs, send_sems, recv_sems = refs[:nt], refs[nt:2 * nt], refs[-2], refs[-1]
        x, y, c = _me()
        me, sibling = (x, y, c), (x, y, 1 - c)
        chips = [(1 - x, y), (x, 1 - y), (1 - x, 1 - y)]
        slot = lambda q, p: out_refs[q].at[4 * p[0] + 2 * p[1] + p[2]]

        def copies(k, blk, to, from_input=False):
            return [_remote(x_refs[q] if from_input else slot(q, blk), slot(q, blk), send_sems, recv_sems, k * nt + q, to)
                    for q in range(nt)]

        first = copies(0, me, sibling, True)
        for j, chip in enumerate(chips):
            first += copies(1 + j, me, (*chip, c), True)
        for cp in first:
            cp.start()
        passed = []
        for j, chip in enumerate(chips):
            for cp in copies(1 + j, (*chip, c), me):
                cp.wait_recv()
            fwd = copies(4 + j, (*chip, c), sibling)
            for cp in fwd:
                cp.start()
            passed += fwd
        for cp in copies(0, sibling, me):
            cp.wait_recv()
        back = copies(7, sibling, sibling)
        for cp in back:
            cp.start()
        for j, chip in enumerate(chips):
            for cp in copies(4 + j, (*chip, 1 - c), me):
                cp.wait_recv()
        for cp in copies(7, me, me):
            cp.wait_recv()
        for cp in first + passed + back:
            cp.wait_send()

    return pl.pallas_call(
        body, name=name, in_specs=[ANY] * nt, out_specs=[ANY] * nt,
        out_shape=[jax.ShapeDtypeStruct((8,) + b.shape, b.dtype) for b in blocks],
        scratch_shapes=[pltpu.SemaphoreType.DMA((8 * nt,)), pltpu.SemaphoreType.DMA((8 * nt,))],
    )(*blocks)


def _swap_halves_multi(name, gs):
    nt = len(gs)
    n_chip = gs[0].shape[0]

    def body(*refs):
        g_refs, got_refs, send_sems, recv_sems = refs[:nt], refs[nt:2 * nt], refs[-2], refs[-1]
        x, y, c = _me()
        cps = [_remote(g_refs[q].at[j, 1 - c], got_refs[q].at[j], send_sems, recv_sems, q * n_chip + j, (x, y, 1 - c))
               for q in range(nt) for j in range(n_chip)]
        for cp in cps:
            cp.start()
        for cp in cps:
            cp.wait()

    return pl.pallas_call(
        body, name=name, in_specs=[ANY] * nt, out_specs=[ANY] * nt,
        out_shape=[jax.ShapeDtypeStruct((g.shape[0],) + g.shape[2:], g.dtype) for g in gs],
        scratch_shapes=[pltpu.SemaphoreType.DMA((nt * n_chip,)), pltpu.SemaphoreType.DMA((nt * n_chip,))],
    )(*gs)


def _swap_sibling_multi(name, xs):
    nt = len(xs)

    def body(*refs):
        x_refs, out_refs, send_sems, recv_sems = refs[:nt], refs[nt:2 * nt], refs[-2], refs[-1]
        x, y, c = _me()
        cps = [_remote(x_refs[q], out_refs[q], send_sems, recv_sems, q, (x, y, 1 - c)) for q in range(nt)]
        for cp in cps:
            cp.start()
        for cp in cps:
            cp.wait()

    return pl.pallas_call(
        body, name=name, in_specs=[ANY] * nt, out_specs=[ANY] * nt,
        out_shape=[jax.ShapeDtypeStruct(a.shape, a.dtype) for a in xs],
        scratch_shapes=[pltpu.SemaphoreType.DMA((nt,)), pltpu.SemaphoreType.DMA((nt,))],
    )(*xs)


def _chip_exchange_multi(name, ps):
    nt = len(ps)

    def body(*refs):
        p_refs, out_refs, bounce_refs = refs[:nt], refs[nt:2 * nt], refs[2 * nt:3 * nt]
        send_sems, recv_sems = refs[-2], refs[-1]
        x, y, c = _me()
        my_chip = 2 * x + y
        sibling = (x, y, 1 - c)
        chips = [(1 - x, y), (x, 1 - y), (1 - x, 1 - y)]
        cp = lambda k, q, src, dst, to: _remote(src, dst, send_sems, recv_sems, k * nt + q, to)
        sends = [cp(k, q, p_refs[q].at[2 * px + py], out_refs[q].at[my_chip], (px, py, c))
                 for k, (px, py) in enumerate(chips) for q in range(nt)]
        sends += [cp(3, q, p_refs[q].at[my_chip], bounce_refs[q], sibling) for q in range(nt)]
        for s_ in sends:
            s_.start()
        backs = []
        for q in range(nt):
            cp(3, q, p_refs[q].at[my_chip], bounce_refs[q], sibling).wait_recv()
            backs.append(cp(4, q, bounce_refs[q], out_refs[q].at[my_chip], sibling))
            backs[-1].start()
        for k, (px, py) in enumerate(chips):
            for q in range(nt):
                cp(k, q, p_refs[q].at[my_chip], out_refs[q].at[2 * px + py], (px, py, c)).wait_recv()
        for q in range(nt):
            cp(4, q, bounce_refs[q], out_refs[q].at[my_chip], sibling).wait_recv()
        for s_ in sends + backs:
            s_.wait_send()

    outs = pl.pallas_call(
        body, name=name, in_specs=[ANY] * nt, out_specs=[ANY] * (2 * nt),
        out_shape=[jax.ShapeDtypeStruct(p.shape, p.dtype) for p in ps]
        + [jax.ShapeDtypeStruct(p.shape[1:], p.dtype) for p in ps],
        scratch_shapes=[pltpu.SemaphoreType.DMA((5 * nt,)), pltpu.SemaphoreType.DMA((5 * nt,))],
    )(*ps)
    return outs[:nt]


def _row_tile(m):
    return m if m <= 384 else 128


def _add2_rows(name, a, b):
    n4, m, n = a.shape
    tm = _row_tile(m)
    out = _rows(name, lambda p, q: ((p + q,), ()), n4 * m, tm,
                [_rb(a.reshape(n4 * m, n), tm, n), _rb(b.reshape(n4 * m, n), tm, n)], [(n, BF16)])[0]
    return out.reshape(n4, m, n)


def _add4_rows(name, p):
    _, m, n = p.shape
    tm = _row_tile(m)
    nb = m // tm
    flat = p.reshape(4 * m, n)
    ins = [(flat, (tm, n), (lambda i, j=j: (j * nb + i, 0))) for j in range(4)]
    f32 = lambda v: v.astype(F32)
    return _rows(name, lambda a, b, c, d: ((((f32(a) + f32(b)) + f32(c)) + f32(d),), ()), m, tm, ins, [(n, F32)])[0]


def _in_proj_with_gather(n1, w_main, blocks):
    m, k = n1.shape
    n = w_main.shape[1]
    tm, tn = min(1024, m), 1024
    gi, gj = m // tm, n // tn
    last, mid = gi * gj - 1, (3 * gi * gj) // 4
    nt = len(blocks)

    def body(*refs):
        a_ref, b_ref, x_refs, o_ref, out_refs = refs[0], refs[1], refs[2:2 + nt], refs[2 + nt], refs[3 + nt:3 + 2 * nt]
        send_sems, recv_sems = refs[-2], refs[-1]
        step = pl.program_id(0) * gj + pl.program_id(1)
        x, y, c = _me()
        me, sibling = (x, y, c), (x, y, 1 - c)
        chips = [(1 - x, y), (x, 1 - y), (1 - x, 1 - y)]
        slot = lambda q, p: out_refs[q].at[4 * p[0] + 2 * p[1] + p[2]]

        def copies(kk, blk, to, from_input=False):
            return [_remote(x_refs[q] if from_input else slot(q, blk), slot(q, blk), send_sems, recv_sems, kk * nt + q, to)
                    for q in range(nt)]

        def first():
            out = copies(0, me, sibling, True)
            for j, chip in enumerate(chips):
                out += copies(1 + j, me, (*chip, c), True)
            return out

        @pl.when(step == 0)
        def _():
            for cp in first():
                cp.start()

        @pl.when(step == mid)
        def _():
            for j, chip in enumerate(chips):
                for cp in copies(1 + j, (*chip, c), me):
                    cp.wait_recv()
                for cp in copies(4 + j, (*chip, c), sibling):
                    cp.start()
            for cp in copies(0, sibling, me):
                cp.wait_recv()
            for cp in copies(7, sibling, sibling):
                cp.start()

        o_ref[...] = _dot(a_ref[...], b_ref[...], NN)

        @pl.when(step == last)
        def _():
            for j, chip in enumerate(chips):
                for cp in copies(4 + j, (*chip, 1 - c), me):
                    cp.wait_recv()
            for cp in copies(7, me, me):
                cp.wait_recv()
            sent = first() + copies(7, sibling, sibling)
            for j, chip in enumerate(chips):
                sent += copies(4 + j, (*chip, c), sibling)
            for cp in sent:
                cp.wait_send()

    outs = pl.pallas_call(
        body, name="in_proj", grid=(gi, gj),
        in_specs=[pl.BlockSpec((tm, k), lambda i, j: (i, 0)), pl.BlockSpec((k, tn), lambda i, j: (0, j))] + [ANY] * nt,
        out_specs=[pl.BlockSpec((tm, tn), lambda i, j: (i, j))] + [ANY] * nt,
        out_shape=[jax.ShapeDtypeStruct((m, n), F32)] + [jax.ShapeDtypeStruct((8,) + b.shape, b.dtype) for b in blocks],
        scratch_shapes=[pltpu.SemaphoreType.DMA((8 * nt,)), pltpu.SemaphoreType.DMA((8 * nt,))],
        compiler_params=_cparams(("arbitrary", "arbitrary")),
    )(n1, w_main, *blocks)
    return outs[0], outs[1:]


def _weight_halves(w_in, w_a, w_b, w_out, w_up, w_down, conv_w):
    c = lax.axis_index("c")
    bits = lax.bitcast_convert_type(conv_w, BF16).reshape(-1)
    extra = jnp.zeros(((WD_EXT_ROWS - W_DOWN_SHARD) * D,), BF16).at[:bits.shape[0]].set(bits)
    wd_ext = jnp.concatenate([w_down.astype(BF16), extra.reshape(-1, D)], axis=0)
    shards = [w_in.astype(BF16), w_a.astype(BF16), w_b.astype(BF16), w_out.astype(BF16), w_up.astype(BF16), wd_ext]
    return [lax.dynamic_slice_in_dim(t, c * (t.shape[0] // 2), t.shape[0] // 2, axis=0) for t in shards]


def _unpack_w_in(gathered):
    wi = gathered.reshape(N_CHIP, D, W_IN_SHARD).transpose(1, 0, 2).reshape(D, N_CHIP * W_IN_SHARD)
    w_main = jnp.concatenate([wi[:, :FF_COL], wi[:, FF_COL + FOX_H:]], axis=1)
    return w_main, jnp.pad(wi[:, FF_COL:FF_COL + FOX_H], ((0, 0), (0, LANES - FOX_H)))


def _unpack_later_weights(gathered):
    full = [g.reshape((N_CHIP, 2 * g.shape[1]) + g.shape[2:]) for g in gathered]
    wa, wb, wo = (full[i].reshape(D, D) for i in (0, 1, 2))
    wu = full[3].transpose(1, 0, 2).reshape(D, 2 * D_FF)
    wd = full[4][:, :W_DOWN_SHARD].reshape(D_FF, D)
    n_bits = 3 * W_UP_SHARD * 2
    cw_bits = full[4][:, W_DOWN_SHARD:].reshape(N_CHIP, -1)[:, :n_bits].reshape(N_CHIP, 3, W_UP_SHARD, 2)
    cw = lax.bitcast_convert_type(cw_bits, F32).transpose(1, 0, 2).reshape(3, 2 * D_FF)
    return wa, wb, wo, wu, wd, cw


def _chip_sums(tag, per_chip):
    c = lax.axis_index("c")
    gs = [t.reshape(N_CHIP, 2, t.shape[1] // 2, t.shape[2]) for t in per_chip]
    got = _swap_halves_multi("grad_swap_halves_" + tag, gs)
    return [_add2_rows("grad_chip_sum_%s%d" % (tag, q), lax.dynamic_index_in_dim(g, c, axis=1, keepdims=False), s_)
            for q, (g, s_) in enumerate(zip(gs, got))]


def _late_weight_chip_sums(d_a, d_b, d_o, d_u, d_d):
    return _chip_sums("late", [d_a.reshape(N_CHIP, -1, D), d_b.reshape(N_CHIP, -1, D), d_o.reshape(N_CHIP, -1, D),
                               d_u.reshape(D, N_CHIP, W_UP_SHARD).transpose(1, 0, 2), d_d.reshape(N_CHIP, -1, D)])


def _finish_grads(d_main, d_ff, late_pieces):
    c = lax.axis_index("c")
    d_in = jnp.concatenate(d_main[:7] + [d_ff[:, :FOX_H]] + d_main[7:], axis=1)
    sums = _chip_sums("w_in", [d_in.reshape(D, N_CHIP, W_IN_SHARD).transpose(1, 0, 2)])
    pieces = list(_chip_exchange_multi("grad_chip_exchange", sums)) + list(late_pieces)
    mine = [_add4_rows("grad_sum_chips_%d" % q, p) for q, p in enumerate(pieces)]
    other = _swap_sibling_multi("grad_share_half", mine)
    return [jnp.concatenate([jnp.where(c == 0, a, b), jnp.where(c == 0, b, a)], axis=0) for a, b in zip(mine, other)]


def _local_step(x, target, norm_mix, fox_f_bias, hg_lb_logits, hg_norm, norm_ffn, conv_b, norm_final,
                w_main, w_ff, later):
    bias = jnp.pad(fox_f_bias, ((0, 0), (0, LANES - FOX_H)))

    n1, n1t = _rms_fwd("norm_mix_fwd", x, norm_mix)
    if len(later) == 5:
        proj, gathered = _in_proj_with_gather(n1, w_main, later)
        wa, wb, wo, wu, wd, conv_w = _unpack_later_weights(gathered)
    else:
        proj = _mm("in_proj", n1, w_main, "nn", F32, 1024, 1024, D)
        wa, wb, wo, wu, wd, conv_w = later
    conv_w8 = jnp.pad(conv_w, ((0, 5), (0, 0)))
    pff = _mm("in_proj_forget", n1, w_ff, "nn", F32, 1024, LANES, D)
    qb, kb, vb, ka = _fox_prep(proj, pff, bias)
    o_b, laux = _fox_fwd(qb, kb, vb, ka)
    o_raw, states = _hg_fwd(proj, hg_lb_logits)
    o_a = _hg_post_fwd(o_raw, proj, hg_norm)
    pa = _mm("branch_a", o_a, wa, "nn", F32, 1024, 1024, D)
    pb = _mm("branch_b", o_b, wb, "nn", F32, 1024, 1024, D)
    merged = _merge_fwd(pa, pb, proj)
    h1 = _mm("out_proj", merged, wo, "nn", F32, 1024, 1024, D, res=x)
    n2, n2t = _rms_fwd("norm_ffn_fwd", h1, norm_ffn)
    u = _mm("ffn_up", n2, wu, "nn", F32, 1024, W_UP_SHARD, D)
    act, gelu_gate, dact_dgate = _convglu_fwd(u, conv_w8, conv_b)
    h2 = _mm("ffn_down", act, wd, "nn", F32, 512, 1024, D_FF, res=h1)
    (dh2,), (d_norm_final, loss_row) = _final(h2, target, norm_final)

    dact = _mm("ffn_down_dx", dh2, wd, "nt", BF16, 1024, D_FF, D)
    d_wd = _mm("ffn_down_dw", act, dh2, "tn", F32, D_FF // 2, 1024, DW_TK // 2)
    (du,), (d_conv_w8, d_conv_b) = _convglu_bwd(u, dact, gelu_gate, dact_dgate, conv_w8)
    dn2 = _mm("ffn_up_dx", du, wu, "nt", F32, 1024, 1024, W_UP_SHARD)
    d_wu = _mm("ffn_up_dw", n2t, du, "nn", F32, 1024, W_UP_SHARD, DW_TK)
    (dh1,), (d_norm_ffn,) = _rms_bwd("norm_ffn_bwd", h1, norm_ffn, [dn2], dh2)

    dmerged = _mm("out_proj_dx", dh1, wo, "nt", F32, 1024, 1024, D)
    d_wo = _mm("out_proj_dw", merged, dh1, "tn", F32, 1024, 1024, DW_TK)
    dpa, dpb, dga, dgb = _merge_bwd(dmerged, pa, pb, proj)
    do_a = _mm("branch_a_dx", dpa, wa, "nt", F32, 1024, 1024, D)
    do_b = _mm("branch_b_dx", dpb, wb, "nt", BF16, 1024, 1024, D)
    d_wa = _mm("branch_a_dw", o_a, dpa, "tn", F32, 1024, 1024, DW_TK)
    d_wb = _mm("branch_b_dw", o_b, dpb, "tn", F32, 1024, 1024, DW_TK)

    (do_raw, dhg), (d_hg_norm,) = _hg_post_bwd(do_a, o_raw, proj, hg_norm)
    dhq, dhf, dhi, d_lb_logits = _hg_bwd(proj, hg_lb_logits, states, do_raw)

    dfq, dfk, dfv, dcrow = _fox_bwd(qb, kb, vb, ka, o_b, laux, do_b)
    dct = jnp.pad(dcrow.reshape(FOX_H, x.shape[0]), ((0, LANES - FOX_H), (0, 0)))
    dff, d_bias = _fox_gate_bwd(dct, pff, bias)

    pieces = [dhq, dhf, dhi, dhg, dfq, dfk, dfv, dga, dgb]
    if len(later) == 5:
        dn1, late = _mm_sum_nt("in_proj_dx", pieces, w_main, (dff, w_ff), 1024, 1024,
                               exchange=_late_weight_chip_sums(d_wa, d_wb, d_wo, d_wu, d_wd))
    else:
        dn1, late = _mm_sum_nt("in_proj_dx", pieces, w_main, (dff, w_ff), 1024, 1024), (d_wa, d_wb, d_wo, d_wu, d_wd)
    d_w_main = [_mm("in_proj_dw_%d" % i, n1t, p, "nn", F32, 1024, 1024, DW_TK) for i, p in enumerate(pieces)]
    d_w_ff = _mm("in_proj_forget_dw", n1t, dff, "nn", F32, 1024, LANES, DW_TK)
    (dx,), (d_norm_mix,) = _rms_bwd("norm_mix_bwd", x, norm_mix, [dn1], dh1)

    small = dict(norm_mix=d_norm_mix, fox_f_bias=d_bias[:, :FOX_H], hg_lb_logits=d_lb_logits, hg_norm=d_hg_norm,
                 norm_ffn=d_norm_ffn, conv_b=d_conv_b, norm_final=d_norm_final, conv_w=d_conv_w8[:3], loss=loss_row)
    big = (d_w_main, d_w_ff) + tuple(late)
    return dx, small, big


SMALL_KEYS = ("norm_mix", "fox_f_bias", "hg_lb_logits", "hg_norm", "norm_ffn", "conv_b", "norm_final")


def _pack_small(parts):
    rows, layout = [], []
    for key, arr in parts:
        flat = arr.reshape(-1)
        n = flat.shape[0]
        nr = -(-n // LANES)
        rows.append(jnp.pad(flat, (0, nr * LANES - n)).reshape(nr, LANES))
        layout.append((key, arr.shape, n, nr))
    packed = jnp.concatenate(rows, axis=0)
    pad = -packed.shape[0] % 8
    return jnp.pad(packed, ((0, pad), (0, 0))), layout


def _unpack_small(packed, layout):
    out, r0 = {}, 0
    for key, shape, n, nr in layout:
        out[key] = packed[r0:r0 + nr].reshape(-1)[:n].reshape(shape)
        r0 += nr
    return out


def kernel(x, norm_mix, w_in, fox_f_bias, hg_lb_logits, hg_norm, w_branch_a, w_branch_b, w_out, norm_ffn, w_up, conv_w, conv_b, w_down, norm_final, loss_target, m_norm_mix, m_w_in, m_fox_f_bias, m_hg_lb_logits, m_hg_norm, m_w_branch_a, m_w_branch_b, m_w_out, m_norm_ffn, m_w_up, m_conv_w, m_conv_b, m_w_down, m_norm_final, v_norm_mix, v_w_in, v_fox_f_bias, v_hg_lb_logits, v_hg_norm, v_w_branch_a, v_w_branch_b, v_w_out, v_norm_ffn, v_w_up, v_conv_w, v_conv_b, v_w_down, v_norm_final):
    chip = 2 * lax.axis_index("x") + lax.axis_index("y")
    halves = _weight_halves(w_in[0], w_branch_a[0], w_branch_b[0], w_out[0], w_up[0], w_down[0], conv_w[0])
    w_main, w_ff = _unpack_w_in(_all_gather8_multi("all_gather_w_in", halves[:1])[0])
    dx, small, big = _local_step(
        x[0], loss_target[0], norm_mix, fox_f_bias, hg_lb_logits, hg_norm, norm_ffn, conv_b,
        norm_final.reshape(1, D), w_main, w_ff, halves[1:])

    packed, layout = _pack_small([(k, small[k]) for k in SMALL_KEYS + ("conv_w", "loss")])
    red = _unpack_small(_all_reduce_small("all_reduce_small", packed), layout)
    loss = red["loss"][0, 0]
    g_conv_w = lax.dynamic_slice_in_dim(red["conv_w"], chip * W_UP_SHARD, W_UP_SHARD, axis=1)

    g_big = _finish_grads(big[0], big[1], big[2:])

    names = ["norm_mix", "w_in", "fox_f_bias", "hg_lb_logits", "hg_norm", "w_branch_a", "w_branch_b", "w_out",
             "norm_ffn", "w_up", "conv_w", "conv_b", "w_down", "norm_final"]
    weights = dict(norm_mix=norm_mix, w_in=w_in, fox_f_bias=fox_f_bias, hg_lb_logits=hg_lb_logits, hg_norm=hg_norm,
                   w_branch_a=w_branch_a, w_branch_b=w_branch_b, w_out=w_out, norm_ffn=norm_ffn, w_up=w_up,
                   conv_w=conv_w, conv_b=conv_b, w_down=w_down, norm_final=norm_final)
    ms = dict(norm_mix=m_norm_mix, w_in=m_w_in, fox_f_bias=m_fox_f_bias, hg_lb_logits=m_hg_lb_logits,
              hg_norm=m_hg_norm, w_branch_a=m_w_branch_a, w_branch_b=m_w_branch_b, w_out=m_w_out,
              norm_ffn=m_norm_ffn, w_up=m_w_up, conv_w=m_conv_w, conv_b=m_conv_b, w_down=m_w_down,
              norm_final=m_norm_final)
    vs = dict(norm_mix=v_norm_mix, w_in=v_w_in, fox_f_bias=v_fox_f_bias, hg_lb_logits=v_hg_lb_logits,
              hg_norm=v_hg_norm, w_branch_a=v_w_branch_a, w_branch_b=v_w_branch_b, w_out=v_w_out,
              norm_ffn=v_norm_ffn, w_up=v_w_up, conv_w=v_conv_w, conv_b=v_conv_b, w_down=v_w_down,
              norm_final=v_norm_final)

    grads, deltas, new_m, new_v = {}, {}, {}, {}
    big_names = ["w_in", "w_branch_a", "w_branch_b", "w_out", "w_up", "w_down"]
    for name, g2 in zip(big_names, g_big):
        shape = weights[name].shape
        rows = g2.shape[0]
        d_, m_, v_ = _adamw("adamw_" + name, weights[name][0], g2, ms[name][0], vs[name][0], tm=rows // 8)
        grads[name], deltas[name], new_m[name], new_v[name] = (a.reshape(shape) for a in (g2, d_, m_, v_))
    shape = conv_w.shape
    d_, m_, v_ = _adamw("adamw_conv_w", conv_w[0], g_conv_w, m_conv_w[0], v_conv_w[0])
    grads["conv_w"], deltas["conv_w"], new_m["conv_w"], new_v["conv_w"] = (
        a.reshape(shape) for a in (g_conv_w, d_, m_, v_))
    gs = {k: red[k].reshape(weights[k].shape) for k in SMALL_KEYS}
    pw, lay = _pack_small([(k, weights[k]) for k in SMALL_KEYS])
    pg, _ = _pack_small([(k, gs[k]) for k in SMALL_KEYS])
    pm, _ = _pack_small([(k, ms[k]) for k in SMALL_KEYS])
    pv, _ = _pack_small([(k, vs[k]) for k in SMALL_KEYS])
    d_, m_, v_ = (_unpack_small(a, lay) for a in _adamw("adamw_small", pw, pg, pm, pv))
    for k in SMALL_KEYS:
        grads[k], deltas[k], new_m[k], new_v[k] = gs[k], d_[k], m_[k], v_[k]

    return (loss, dx[None], *[grads[n] for n in names], *[deltas[n] for n in names],
            *[new_m[n] for n in names], *[new_v[n] for n in names])
```
